```python
import math
import jax, jax.numpy as jnp
from jax import lax
import numpy as np

D_MODEL = 1024
BATCH = 8
SEQ = 16384
DEPTH = 1

CTX_LEN = 256
GRID_W = 64
DN_HEAD_DIM = 128
DN_HEADS = D_MODEL // DN_HEAD_DIM
DN_DIM = DN_HEADS * DN_HEAD_DIM
SHORT_CONV = 5
CHUNK = 64
HEAD_DIM = 128
ATTN_HEADS = D_MODEL // HEAD_DIM
ATTN_KV_HEADS = ATTN_HEADS // 4
ATTN_GROUPS = ATTN_HEADS // ATTN_KV_HEADS
ATTN_DIM = ATTN_HEADS * HEAD_DIM
KV_DIM = ATTN_KV_HEADS * HEAD_DIM
WINDOW = 128
ATTN_BLOCK = 128
ROPE_BASE = 10000.0
ROPE_FREQS = HEAD_DIM // 4
D_FF = ((8 * D_MODEL // 3 + 127) // 128) * 128
FFN_CONV = 3
RMS_EPS = 1e-6
IN_SIZES = (3 * DN_DIM, DN_DIM, 2 * DN_HEADS, 2 * DN_HEADS, ATTN_DIM, KV_DIM, KV_DIM, 2 * D_MODEL)
IN_DIM = 4 * DN_DIM + 4 * DN_HEADS + ATTN_DIM + 2 * KV_DIM + 2 * D_MODEL

kernel_name = 'hybrid_deltanet_swa_dit_layer'


def rms_norm(x, w):
    xf = x.astype(jnp.float32)
    y = xf * lax.rsqrt(jnp.mean(xf * xf, axis=-1, keepdims=True) + RMS_EPS)
    return y.astype(x.dtype) * w


def l2_normalize(x):
    return x * lax.rsqrt(jnp.sum(x * x, axis=-1, keepdims=True) + RMS_EPS)


def modulate(h, shift, scale):
    return h * (1.0 + scale) + shift


def dwconv_centred(x, w):
    width = w.shape[0]
    r = width // 2
    length = x.shape[1]
    xp = jnp.pad(x, ((0, 0), (r, r), (0, 0)))
    out = xp[:, :length] * w[0]
    for j in range(1, width):
        out = out + xp[:, j:j + length] * w[j]
    return out


def split_in(p):
    cuts = np.cumsum(IN_SIZES)[:-1].tolist()
    return jnp.split(p, cuts, axis=-1)


def rope_tables_2d(rows, dtype):
    row = jnp.broadcast_to(jnp.arange(rows, dtype=jnp.float32)[:, None], (rows, GRID_W)).reshape(-1)
    col = jnp.broadcast_to(jnp.arange(GRID_W, dtype=jnp.float32)[None, :], (rows, GRID_W)).reshape(-1)
    inv_freq = ROPE_BASE ** (-jnp.arange(ROPE_FREQS, dtype=jnp.float32) / ROPE_FREQS)
    ang_r = row[:, None] * inv_freq
    ang_c = col[:, None] * inv_freq
    return (jnp.cos(ang_r).astype(dtype), jnp.sin(ang_r).astype(dtype),
            jnp.cos(ang_c).astype(dtype), jnp.sin(ang_c).astype(dtype))


def rope_1d(x, cos, sin):
    x1, x2 = jnp.split(x, 2, axis=-1)
    cos = cos[None, :, None, :]
    sin = sin[None, :, None, :]
    return jnp.concatenate([x1 * cos - x2 * sin, x2 * cos + x1 * sin], axis=-1)


def rope_2d(x, tables):
    cos_r, sin_r, cos_c, sin_c = tables
    half = HEAD_DIM // 2
    return jnp.concatenate([rope_1d(x[..., :half], cos_r, sin_r),
                            rope_1d(x[..., half:], cos_c, sin_c)], axis=-1)


def dn_prepare(qkv, b, a, conv_w, a_log, dt_bias):
    bsz, length = qkv.shape[:2]
    qkv = jax.nn.silu(dwconv_centred(qkv, conv_w)).astype(jnp.float32)
    q, k, v = jnp.split(qkv, 3, axis=-1)
    q = l2_normalize(q.reshape(bsz, length, DN_HEADS, DN_HEAD_DIM)) * (DN_HEAD_DIM ** -0.5)
    k = l2_normalize(k.reshape(bsz, length, DN_HEADS, DN_HEAD_DIM))
    v = v.reshape(bsz, length, DN_HEADS, DN_HEAD_DIM)
    beta = jax.nn.sigmoid(b.astype(jnp.float32)).reshape(bsz, length, 2, DN_HEADS)
    a = a.astype(jnp.float32).reshape(bsz, length, 2, DN_HEADS)
    g = -jnp.exp(a_log.astype(jnp.float32)) * jax.nn.softplus(a + dt_bias.astype(jnp.float32))
    return q, k, v, g, beta


def gated_delta_chunked(q, k, v, g, beta, s0):
    bsz, length, heads, _ = q.shape
    dv = v.shape[-1]
    n = length // CHUNK

    def to_chunks(t):
        t = t.reshape((bsz, n, CHUNK, heads) + t.shape[3:])
        return jnp.moveaxis(t, (1, 3), (0, 2))

    qc, kc, vc, bc = to_chunks(q), to_chunks(k), to_chunks(v), to_chunks(beta)
    gc = jnp.cumsum(to_chunks(g), axis=-1)
    idx = jnp.arange(CHUNK)
    incl = idx[:, None] >= idx[None, :]
    strict = idx[:, None] > idx[None, :]
    diff = gc[..., :, None] - gc[..., None, :]
    decay = jnp.where(incl, jnp.exp(jnp.where(incl, diff, 0.0)), 0.0)
    kk = jnp.einsum('nbhid,nbhjd->nbhij', kc, kc)
    a_mat = jnp.where(strict, bc[..., :, None] * kk * decay, 0.0)
    rhs = jnp.concatenate([vc * bc[..., None], kc * (bc * jnp.exp(gc))[..., None]], axis=-1)
    sol = lax.linalg.triangular_solve(a_mat, rhs, left_side=True, lower=True, unit_diagonal=True)
    u, w = sol[..., :dv], sol[..., dv:]
    qk = jnp.einsum('nbhid,nbhjd->nbhij', qc, kc) * decay
    qg = qc * jnp.exp(gc)[..., None]
    kd = kc * jnp.exp(gc[..., -1:] - gc)[..., None]
    g_last = jnp.exp(gc[..., -1])

    def step(s, xs):
        u_i, w_i, qg_i, qk_i, kd_i, gl_i = xs
        v_new = u_i - jnp.einsum('bhcd,bhde->bhce', w_i, s)
        o_i = jnp.einsum('bhcd,bhde->bhce', qg_i, s) + jnp.einsum('bhij,bhje->bhie', qk_i, v_new)
        s = s * gl_i[..., None, None] + jnp.einsum('bhcd,bhce->bhde', kd_i, v_new)
        return s, o_i

    s_fin, o = lax.scan(step, s0, (u, w, qg, qk, kd, g_last))
    o = jnp.moveaxis(o, (0, 2), (1, 3)).reshape(bsz, length, heads, dv)
    return o, s_fin


def bidirectional_delta(lat, ctx_side):
    qx, kx, vx, gx, bx = lat
    qc, kc, vc, gc, bc = ctx_side
    s0 = jnp.zeros((qc.shape[0], DN_HEADS, DN_HEAD_DIM, DN_HEAD_DIM), jnp.float32)
    rev = lambda t: jnp.flip(t, axis=1)
    oc_f, sc_f = gated_delta_chunked(qc, kc, vc, gc[:, :, 0], bc[:, :, 0], s0)
    ox_f, _ = gated_delta_chunked(qx, kx, vx, gx[:, :, 0], bx[:, :, 0], sc_f)
    oc_b, sc_b = gated_delta_chunked(rev(qc), rev(kc), rev(vc), rev(gc[:, :, 1]), rev(bc[:, :, 1]), s0)
    ox_b, _ = gated_delta_chunked(rev(qx), rev(kx), rev(vx), rev(gx[:, :, 1]), rev(bx[:, :, 1]), sc_b)
    return ox_f + rev(ox_b), oc_f + rev(oc_b)


def gated_head_norm(o, gate, w):
    bsz, length = o.shape[:2]
    y = rms_norm(o, w) * jax.nn.silu(gate.reshape(bsz, length, DN_HEADS, DN_HEAD_DIM).astype(jnp.float32))
    return y.reshape(bsz, length, DN_DIM).astype(gate.dtype)


def attn_heads(q, k, v, q_norm, k_norm):
    bsz, length = q.shape[:2]
    q = rms_norm(q.reshape(bsz, length, ATTN_HEADS, HEAD_DIM), q_norm)
    k = rms_norm(k.reshape(bsz, length, ATTN_KV_HEADS, HEAD_DIM), k_norm)
    v = v.reshape(bsz, length, ATTN_KV_HEADS, HEAD_DIM)
    return q, k, v


def window_attention_latent(q, k, v, k_ctx, v_ctx, sink):
    bsz, length = q.shape[:2]
    n_ctx = k_ctx.shape[1]
    nb = length // ATTN_BLOCK
    band = 3 * ATTN_BLOCK
    scale = HEAD_DIM ** -0.5
    pad = ((0, 0), (ATTN_BLOCK, ATTN_BLOCK), (0, 0), (0, 0))
    kp = jnp.pad(k, pad)
    vp = jnp.pad(v, pad)
    qb = jnp.moveaxis(q.reshape(bsz, nb, ATTN_BLOCK, ATTN_KV_HEADS, ATTN_GROUPS, HEAD_DIM), 1, 0)
    sink_l = jnp.broadcast_to(sink.astype(jnp.float32).reshape(1, ATTN_KV_HEADS, ATTN_GROUPS, 1, 1),
                              (bsz, ATTN_KV_HEADS, ATTN_GROUPS, ATTN_BLOCK, 1))

    def block(args):
        i, q_i = args
        start = i * ATTN_BLOCK
        k_i = lax.dynamic_slice_in_dim(kp, start, band, axis=1)
        v_i = lax.dynamic_slice_in_dim(vp, start, band, axis=1)
        qpos = start + jnp.arange(ATTN_BLOCK)
        kpos = start - ATTN_BLOCK + jnp.arange(band)
        ok = (jnp.abs(qpos[:, None] - kpos[None, :]) <= WINDOW) & (kpos[None, :] >= 0) & (kpos[None, :] < length)
        s_win = jnp.einsum('bqkgd,bskd->bkgqs', q_i, k_i).astype(jnp.float32) * scale
        s_win = jnp.where(ok, s_win, -jnp.inf)
        s_ctx = jnp.einsum('bqkgd,bskd->bkgqs', q_i, k_ctx).astype(jnp.float32) * scale
        p = jax.nn.softmax(jnp.concatenate([s_win, s_ctx, sink_l], axis=-1), axis=-1).astype(v.dtype)
        return (jnp.einsum('bkgqs,bskd->bqkgd', p[..., :band], v_i)
                + jnp.einsum('bkgqs,bskd->bqkgd', p[..., band:band + n_ctx], v_ctx))

    o = lax.map(block, (jnp.arange(nb), qb))
    return jnp.moveaxis(o, 0, 1).reshape(bsz, length, ATTN_DIM)


def context_attention(q, k, v, sink):
    bsz, n_ctx = q.shape[:2]
    qg = q.reshape(bsz, n_ctx, ATTN_KV_HEADS, ATTN_GROUPS, HEAD_DIM)
    s = jnp.einsum('bqkgd,bskd->bkgqs', qg, k).astype(jnp.float32) * (HEAD_DIM ** -0.5)
    sink_c = jnp.broadcast_to(sink.astype(jnp.float32).reshape(1, ATTN_KV_HEADS, ATTN_GROUPS, 1, 1),
                              (bsz, ATTN_KV_HEADS, ATTN_GROUPS, n_ctx, 1))
    p = jax.nn.softmax(jnp.concatenate([s, sink_c], axis=-1), axis=-1).astype(v.dtype)
    o = jnp.einsum('bkgqs,bskd->bqkgd', p[..., :n_ctx], v)
    return o.reshape(bsz, n_ctx, ATTN_DIM)


def merge_branches(y_dn, y_at, gates, w_branch_dn, w_branch_attn, w_out):
    g_dn, g_at = jnp.split(gates, 2, axis=-1)
    merged = jax.nn.sigmoid(g_dn) * (y_dn @ w_branch_dn) + jax.nn.sigmoid(g_at) * (y_at @ w_branch_attn)
    return merged @ w_out


def conv_ffn(h, up, conv_w, conv_b, down):
    u = dwconv_centred(h @ up, conv_w) + conv_b
    u_gate, u_val = jnp.split(u, 2, axis=-1)
    return (jax.nn.silu(u_gate) * u_val) @ down


def hybrid_layer(x, ctx, c, c_ctx, rope, w_ada, b_ada, norm_mix, norm_ffn, w_in, dn_conv, dn_a_log,
                 dn_dt_bias, dn_norm, q_norm, k_norm, attn_sink, w_branch_dn, w_branch_attn, w_out,
                 ffn_up, ffn_conv, ffn_conv_b, ffn_down, update_ctx):
    mod_x = (jax.nn.silu(c) @ w_ada + b_ada)[:, None, :]
    mod_c = (jax.nn.silu(c_ctx) @ w_ada + b_ada)[None, None, :]
    sh_a_x, sc_a_x, g_a_x, sh_f_x, sc_f_x, g_f_x = jnp.split(mod_x, 6, axis=-1)
    sh_a_c, sc_a_c, g_a_c, sh_f_c, sc_f_c, g_f_c = jnp.split(mod_c, 6, axis=-1)

    p_x = modulate(rms_norm(x, norm_mix), sh_a_x, sc_a_x) @ w_in
    p_c = modulate(rms_norm(ctx, norm_mix), sh_a_c, sc_a_c) @ w_in
    qkv_dn_x, gt_dn_x, b_dn_x, a_dn_x, q_at_x, k_at_x, v_at_x, mg_x = split_in(p_x)
    qkv_dn_c, gt_dn_c, b_dn_c, a_dn_c, q_at_c, k_at_c, v_at_c, mg_c = split_in(p_c)

    dn_x = dn_prepare(qkv_dn_x, b_dn_x, a_dn_x, dn_conv, dn_a_log, dn_dt_bias)
    dn_c = dn_prepare(qkv_dn_c, b_dn_c, a_dn_c, dn_conv, dn_a_log, dn_dt_bias)
    o_dn_x, o_dn_c = bidirectional_delta(dn_x, dn_c)
    y_dn_x = gated_head_norm(o_dn_x, gt_dn_x, dn_norm)

    qx, kx, vx = attn_heads(q_at_x, k_at_x, v_at_x, q_norm, k_norm)
    qc, kc, vc = attn_heads(q_at_c, k_at_c, v_at_c, q_norm, k_norm)
    qx = rope_2d(qx, rope)
    kx = rope_2d(kx, rope)
    y_at_x = window_attention_latent(qx, kx, vx, kc, vc, attn_sink)

    x = x + g_a_x * merge_branches(y_dn_x, y_at_x, mg_x, w_branch_dn, w_branch_attn, w_out)
    x = x + g_f_x * conv_ffn(modulate(rms_norm(x, norm_ffn), sh_f_x, sc_f_x), ffn_up, ffn_conv, ffn_conv_b, ffn_down)

    if update_ctx:
        y_dn_c = gated_head_norm(o_dn_c, gt_dn_c, dn_norm)
        y_at_c = context_attention(qc, kc, vc, attn_sink)
        ctx = ctx + g_a_c * merge_branches(y_dn_c, y_at_c, mg_c, w_branch_dn, w_branch_attn, w_out)
        ctx = ctx + g_f_c * conv_ffn(modulate(rms_norm(ctx, norm_ffn), sh_f_c, sc_f_c), ffn_up, ffn_conv, ffn_conv_b, ffn_down)
    return x, ctx


def _fwd_setup_inputs(seed: int = 0) -> dict:
    key = jax.random.key(seed)
    ks = jax.random.split(key, 24)

    def nrm(k, shape, s):
        return jax.random.normal(k, shape, jnp.float32) * s

    x = nrm(ks[0], (BATCH, SEQ, D_MODEL), 1.0)
    c = nrm(ks[1], (BATCH, D_MODEL), 1.0)
    ctx = nrm(ks[2], (BATCH, CTX_LEN, D_MODEL), 1.0)
    c_ctx = nrm(ks[3], (D_MODEL,), 1.0)
    w_ada = nrm(ks[4], (DEPTH, D_MODEL, 6 * D_MODEL), 0.5 * D_MODEL ** -0.5)
    b_ada = nrm(ks[5], (DEPTH, 6 * D_MODEL), 0.02)
    norm_mix = 1.0 + nrm(ks[6], (DEPTH, D_MODEL), 0.1)
    norm_ffn = 1.0 + nrm(ks[7], (DEPTH, D_MODEL), 0.1)
    w_in = nrm(ks[8], (DEPTH, D_MODEL, IN_DIM), D_MODEL ** -0.5)
    dn_conv = nrm(ks[9], (DEPTH, SHORT_CONV, 3 * DN_DIM), SHORT_CONV ** -0.5)
    dn_a_log = jnp.log(jax.random.uniform(ks[10], (DEPTH, 2, DN_HEADS), jnp.float32, minval=1.0, maxval=16.0))
    dt = jnp.exp(jax.random.uniform(ks[11], (DEPTH, 2, DN_HEADS), jnp.float32,
                                    minval=math.log(1e-3), maxval=math.log(1e-1)))
    dn_dt_bias = dt + jnp.log(-jnp.expm1(-dt))
    dn_norm = 1.0 + nrm(ks[12], (DEPTH, DN_HEAD_DIM), 0.1)
    q_norm = 1.0 + nrm(ks[13], (DEPTH, HEAD_DIM), 0.1)
    k_norm = 1.0 + nrm(ks[14], (DEPTH, HEAD_DIM), 0.1)
    attn_sink = nrm(ks[15], (DEPTH, ATTN_HEADS), 0.5)
    w_branch_dn = nrm(ks[16], (DEPTH, DN_DIM, D_MODEL), DN_DIM ** -0.5)
    w_branch_attn = nrm(ks[17], (DEPTH, ATTN_DIM, D_MODEL), ATTN_DIM ** -0.5)
    w_out = nrm(ks[18], (DEPTH, D_MODEL, D_MODEL), D_MODEL ** -0.5)
    ffn_up = nrm(ks[19], (DEPTH, D_MODEL, 2 * D_FF), D_MODEL ** -0.5)
    ffn_conv = nrm(ks[20], (DEPTH, FFN_CONV, 2 * D_FF), FFN_CONV ** -0.5)
    ffn_conv_b = nrm(ks[21], (DEPTH, 2 * D_FF), 0.02)
    ffn_down = nrm(ks[22], (DEPTH, D_FF, D_MODEL), D_FF ** -0.5)
    return {'x': x, 'c': c, 'ctx': ctx, 'c_ctx': c_ctx, 'w_ada': w_ada, 'b_ada': b_ada,
            'norm_mix': norm_mix, 'norm_ffn': norm_ffn, 'w_in': w_in, 'dn_conv': dn_conv,
            'dn_a_log': dn_a_log, 'dn_dt_bias': dn_dt_bias, 'dn_norm': dn_norm, 'q_norm': q_norm,
            'k_norm': k_norm, 'attn_sink': attn_sink, 'w_branch_dn': w_branch_dn,
            'w_branch_attn': w_branch_attn, 'w_out': w_out, 'ffn_up': ffn_up, 'ffn_conv': ffn_conv,
            'ffn_conv_b': ffn_conv_b, 'ffn_down': ffn_down}


def _fwd_reference(x, c, ctx, c_ctx, w_ada, b_ada, norm_mix, norm_ffn, w_in, dn_conv, dn_a_log, dn_dt_bias,
              dn_norm, q_norm, k_norm, attn_sink, w_branch_dn, w_branch_attn, w_out, ffn_up, ffn_conv,
              ffn_conv_b, ffn_down):
    rows = x.shape[1] // GRID_W
    rope = rope_tables_2d(rows, x.dtype)
    for layer in range(DEPTH):
        x, ctx = hybrid_layer(x, ctx, c, c_ctx, rope, w_ada[layer], b_ada[layer], norm_mix[layer],
                              norm_ffn[layer], w_in[layer], dn_conv[layer], dn_a_log[layer],
                              dn_dt_bias[layer], dn_norm[layer], q_norm[layer], k_norm[layer],
                              attn_sink[layer], w_branch_dn[layer], w_branch_attn[layer], w_out[layer],
                              ffn_up[layer], ffn_conv[layer], ffn_conv_b[layer], ffn_down[layer],
                              update_ctx=layer + 1 < DEPTH)
    return x


import jax as _jax
import jax.numpy as _jnp

TWIN_FORMAT = 'train_step'
FWD_PARAMS = ['x', 'c', 'ctx', 'c_ctx', 'w_ada', 'b_ada', 'norm_mix', 'norm_ffn', 'w_in', 'dn_conv', 'dn_a_log', 'dn_dt_bias', 'dn_norm', 'q_norm', 'k_norm', 'attn_sink', 'w_branch_dn', 'w_branch_attn', 'w_out', 'ffn_up', 'ffn_conv', 'ffn_conv_b', 'ffn_down']
TWIN_WEIGHTS = ['c_ctx', 'w_ada', 'b_ada', 'norm_mix', 'norm_ffn', 'w_in', 'dn_conv', 'dn_a_log', 'dn_dt_bias', 'dn_norm', 'q_norm', 'k_norm', 'attn_sink', 'w_branch_dn', 'w_branch_attn', 'w_out', 'ffn_up', 'ffn_conv', 'ffn_conv_b', 'ffn_down']
TWIN_DIFF_INPUT = 'x'
TWIN_INPUTS = ['x', 'c', 'ctx', 'c_ctx', 'w_ada', 'b_ada', 'norm_mix', 'norm_ffn', 'w_in', 'dn_conv', 'dn_a_log', 'dn_dt_bias', 'dn_norm', 'q_norm', 'k_norm', 'attn_sink', 'w_branch_dn', 'w_branch_attn', 'w_out', 'ffn_up', 'ffn_conv', 'ffn_conv_b', 'ffn_down', 'loss_target', 'm_c_ctx', 'm_w_ada', 'm_b_ada', 'm_norm_mix', 'm_norm_ffn', 'm_w_in', 'm_dn_conv', 'm_dn_a_log', 'm_dn_dt_bias', 'm_dn_norm', 'm_q_norm', 'm_k_norm', 'm_attn_sink', 'm_w_branch_dn', 'm_w_branch_attn', 'm_w_out', 'm_ffn_up', 'm_ffn_conv', 'm_ffn_conv_b', 'm_ffn_down', 'v_c_ctx', 'v_w_ada', 'v_b_ada', 'v_norm_mix', 'v_norm_ffn', 'v_w_in', 'v_dn_conv', 'v_dn_a_log', 'v_dn_dt_bias', 'v_dn_norm', 'v_q_norm', 'v_k_norm', 'v_attn_sink', 'v_w_branch_dn', 'v_w_branch_attn', 'v_w_out', 'v_ffn_up', 'v_ffn_conv', 'v_ffn_conv_b', 'v_ffn_down']
TWIN_OUTPUTS = ['loss', 'grad_x', 'grad_c_ctx', 'grad_w_ada', 'grad_b_ada', 'grad_norm_mix', 'grad_norm_ffn', 'grad_w_in', 'grad_dn_conv', 'grad_dn_a_log', 'grad_dn_dt_bias', 'grad_dn_norm', 'grad_q_norm', 'grad_k_norm', 'grad_attn_sink', 'grad_w_branch_dn', 'grad_w_branch_attn', 'grad_w_out', 'grad_ffn_up', 'grad_ffn_conv', 'grad_ffn_conv_b', 'grad_ffn_down', 'delta_c_ctx', 'delta_w_ada', 'delta_b_ada', 'delta_norm_mix', 'delta_norm_ffn', 'delta_w_in', 'delta_dn_conv', 'delta_dn_a_log', 'delta_dn_dt_bias', 'delta_dn_norm', 'delta_q_norm', 'delta_k_norm', 'delta_attn_sink', 'delta_w_branch_dn', 'delta_w_branch_attn', 'delta_w_out', 'delta_ffn_up', 'delta_ffn_conv', 'delta_ffn_conv_b', 'delta_ffn_down', 'new_m_c_ctx', 'new_m_w_ada', 'new_m_b_ada', 'new_m_norm_mix', 'new_m_norm_ffn', 'new_m_w_in', 'new_m_dn_conv', 'new_m_dn_a_log', 'new_m_dn_dt_bias', 'new_m_dn_norm', 'new_m_q_norm', 'new_m_k_norm', 'new_m_attn_sink', 'new_m_w_branch_dn', 'new_m_w_branch_attn', 'new_m_w_out', 'new_m_ffn_up', 'new_m_ffn_conv', 'new_m_ffn_conv_b', 'new_m_ffn_down', 'new_v_c_ctx', 'new_v_w_ada', 'new_v_b_ada', 'new_v_norm_mix', 'new_v_norm_ffn', 'new_v_w_in', 'new_v_dn_conv', 'new_v_dn_a_log', 'new_v_dn_dt_bias', 'new_v_dn_norm', 'new_v_q_norm', 'new_v_k_norm', 'new_v_attn_sink', 'new_v_w_branch_dn', 'new_v_w_branch_attn', 'new_v_w_out', 'new_v_ffn_up', 'new_v_ffn_conv', 'new_v_ffn_conv_b', 'new_v_ffn_down']
TWIN_LEAF_KINDS = {'loss': 'loss', 'grad_x': 'grad_x', 'grad_c_ctx': 'grad_w', 'grad_w_ada': 'grad_w', 'grad_b_ada': 'grad_w', 'grad_norm_mix': 'grad_w', 'grad_norm_ffn': 'grad_w', 'grad_w_in': 'grad_w', 'grad_dn_conv': 'grad_w', 'grad_dn_a_log': 'grad_w', 'grad_dn_dt_bias': 'grad_w', 'grad_dn_norm': 'grad_w', 'grad_q_norm': 'grad_w', 'grad_k_norm': 'grad_w', 'grad_attn_sink': 'grad_w', 'grad_w_branch_dn': 'grad_w', 'grad_w_branch_attn': 'grad_w', 'grad_w_out': 'grad_w', 'grad_ffn_up': 'grad_w', 'grad_ffn_conv': 'grad_w', 'grad_ffn_conv_b': 'grad_w', 'grad_ffn_down': 'grad_w', 'delta_c_ctx': 'delta_w', 'delta_w_ada': 'delta_w', 'delta_b_ada': 'delta_w', 'delta_norm_mix': 'delta_w', 'delta_norm_ffn': 'delta_w', 'delta_w_in': 'delta_w', 'delta_dn_conv': 'delta_w', 'delta_dn_a_log': 'delta_w', 'delta_dn_dt_bias': 'delta_w', 'delta_dn_norm': 'delta_w', 'delta_q_norm': 'delta_w', 'delta_k_norm': 'delta_w', 'delta_attn_sink': 'delta_w', 'delta_w_branch_dn': 'delta_w', 'delta_w_branch_attn': 'delta_w', 'delta_w_out': 'delta_w', 'delta_ffn_up': 'delta_w', 'delta_ffn_conv': 'delta_w', 'delta_ffn_conv_b': 'delta_w', 'delta_ffn_down': 'delta_w', 'new_m_c_ctx': 'new_m', 'new_m_w_ada': 'new_m', 'new_m_b_ada': 'new_m', 'new_m_norm_mix': 'new_m', 'new_m_norm_ffn': 'new_m', 'new_m_w_in': 'new_m', 'new_m_dn_conv': 'new_m', 'new_m_dn_a_log': 'new_m', 'new_m_dn_dt_bias': 'new_m', 'new_m_dn_norm': 'new_m', 'new_m_q_norm': 'new_m', 'new_m_k_norm': 'new_m', 'new_m_attn_sink': 'new_m', 'new_m_w_branch_dn': 'new_m', 'new_m_w_branch_attn': 'new_m', 'new_m_w_out': 'new_m', 'new_m_ffn_up': 'new_m', 'new_m_ffn_conv': 'new_m', 'new_m_ffn_conv_b': 'new_m', 'new_m_ffn_down': 'new_m', 'new_v_c_ctx': 'new_v', 'new_v_w_ada': 'new_v', 'new_v_b_ada': 'new_v', 'new_v_norm_mix': 'new_v', 'new_v_norm_ffn': 'new_v', 'new_v_w_in': 'new_v', 'new_v_dn_conv': 'new_v', 'new_v_dn_a_log': 'new_v', 'new_v_dn_dt_bias': 'new_v', 'new_v_dn_norm': 'new_v', 'new_v_q_norm': 'new_v', 'new_v_k_norm': 'new_v', 'new_v_attn_sink': 'new_v', 'new_v_w_branch_dn': 'new_v', 'new_v_w_branch_attn': 'new_v', 'new_v_w_out': 'new_v', 'new_v_ffn_up': 'new_v', 'new_v_ffn_conv': 'new_v', 'new_v_ffn_conv_b': 'new_v', 'new_v_ffn_down': 'new_v'}


def _forward(args):
    return _fwd_reference(*[args[k] for k in FWD_PARAMS])


def _output_shape():
    def fwd():
        inp = _fwd_setup_inputs(0)
        return _fwd_reference(*[inp[k] for k in FWD_PARAMS])
    out = _jax.eval_shape(fwd)
    return out.shape, out.dtype

N_MICROBATCH = 1
ADAM_LR = 0.001
ADAM_B1 = 0.9
ADAM_B2 = 0.999
ADAM_EPS = 1e-08
ADAM_WD = 0.01
ADAM_STEP = 10
PER_EXAMPLE_BATCH_AXIS = {'x': 0, 'c': 0, 'ctx': 0, 'loss_target': 0}
SHARED_INPUTS = []
_WEIGHT_DTYPES = {'c_ctx': _jnp.float32, 'w_ada': _jnp.float32, 'b_ada': _jnp.float32, 'norm_mix': _jnp.float32, 'norm_ffn': _jnp.float32, 'w_in': _jnp.float32, 'dn_conv': _jnp.float32, 'dn_a_log': _jnp.float32, 'dn_dt_bias': _jnp.float32, 'dn_norm': _jnp.float32, 'q_norm': _jnp.float32, 'k_norm': _jnp.float32, 'attn_sink': _jnp.float32, 'w_branch_dn': _jnp.float32, 'w_branch_attn': _jnp.float32, 'w_out': _jnp.float32, 'ffn_up': _jnp.float32, 'ffn_conv': _jnp.float32, 'ffn_conv_b': _jnp.float32, 'ffn_down': _jnp.float32}
MOMENT_SCALE = {'c_ctx': 2.696453e-01, 'w_ada': 2.410211e+00, 'b_ada': 6.792964e+00, 'norm_mix': 1.956416e+00, 'norm_ffn': 1.272890e+01, 'w_in': 1.286819e-01, 'dn_conv': 1.629476e-01, 'dn_a_log': 4.529947e-01, 'dn_dt_bias': 4.390845e-01, 'dn_norm': 1.357809e+01, 'q_norm': 1.814609e-01, 'k_norm': 1.671758e-01, 'attn_sink': 2.879484e-02, 'w_branch_dn': 1.821823e-01, 'w_branch_attn': 2.080598e-01, 'w_out': 2.044456e-01, 'ffn_up': 2.596694e-01, 'ffn_conv': 1.903285e+00, 'ffn_conv_b': 1.551285e+00, 'ffn_down': 1.750748e-01}


def _to_microbatches(a, axis):
    t = _jnp.moveaxis(a, axis, 0)
    t = t.reshape((N_MICROBATCH, t.shape[0] // N_MICROBATCH) + t.shape[1:])
    return _jnp.moveaxis(t, 1, axis + 1)


def setup_inputs(seed: int = 0) -> dict:
    inp = _fwd_setup_inputs(seed)
    key = _jax.random.fold_in(_jax.random.key(seed), 7919)
    shape, _ = _output_shape()
    out = dict(inp)
    out["loss_target"] = _jax.random.normal(_jax.random.fold_in(key, 0), shape, _jnp.float32)
    for i, name in enumerate(TWIN_WEIGHTS):
        w = inp[name].astype(_jnp.float32)
        if MOMENT_SCALE is None:
            s = _jnp.sqrt(_jnp.mean(_jnp.square(w)) + 1e-30)
        else:
            s = MOMENT_SCALE[name]
        km, kv = _jax.random.split(_jax.random.fold_in(key, i + 1))
        out[name] = w
        out["m_" + name] = s * _jax.random.normal(km, w.shape, _jnp.float32)
        out["v_" + name] = (s * s) * _jax.random.uniform(kv, w.shape, _jnp.float32, 0.5, 1.5)
    if N_MICROBATCH > 1:
        for name, axis in PER_EXAMPLE_BATCH_AXIS.items():
            out[name] = _to_microbatches(out[name], axis)
    return {'x': out['x'], 'c': out['c'], 'ctx': out['ctx'], 'c_ctx': out['c_ctx'], 'w_ada': out['w_ada'], 'b_ada': out['b_ada'], 'norm_mix': out['norm_mix'], 'norm_ffn': out['norm_ffn'], 'w_in': out['w_in'], 'dn_conv': out['dn_conv'], 'dn_a_log': out['dn_a_log'], 'dn_dt_bias': out['dn_dt_bias'], 'dn_norm': out['dn_norm'], 'q_norm': out['q_norm'], 'k_norm': out['k_norm'], 'attn_sink': out['attn_sink'], 'w_branch_dn': out['w_branch_dn'], 'w_branch_attn': out['w_branch_attn'], 'w_out': out['w_out'], 'ffn_up': out['ffn_up'], 'ffn_conv': out['ffn_conv'], 'ffn_conv_b': out['ffn_conv_b'], 'ffn_down': out['ffn_down'], 'loss_target': out['loss_target'], 'm_c_ctx': out['m_c_ctx'], 'm_w_ada': out['m_w_ada'], 'm_b_ada': out['m_b_ada'], 'm_norm_mix': out['m_norm_mix'], 'm_norm_ffn': out['m_norm_ffn'], 'm_w_in': out['m_w_in'], 'm_dn_conv': out['m_dn_conv'], 'm_dn_a_log': out['m_dn_a_log'], 'm_dn_dt_bias': out['m_dn_dt_bias'], 'm_dn_norm': out['m_dn_norm'], 'm_q_norm': out['m_q_norm'], 'm_k_norm': out['m_k_norm'], 'm_attn_sink': out['m_attn_sink'], 'm_w_branch_dn': out['m_w_branch_dn'], 'm_w_branch_attn': out['m_w_branch_attn'], 'm_w_out': out['m_w_out'], 'm_ffn_up': out['m_ffn_up'], 'm_ffn_conv': out['m_ffn_conv'], 'm_ffn_conv_b': out['m_ffn_conv_b'], 'm_ffn_down': out['m_ffn_down'], 'v_c_ctx': out['v_c_ctx'], 'v_w_ada': out['v_w_ada'], 'v_b_ada': out['v_b_ada'], 'v_norm_mix': out['v_norm_mix'], 'v_norm_ffn': out['v_norm_ffn'], 'v_w_in': out['v_w_in'], 'v_dn_conv': out['v_dn_conv'], 'v_dn_a_log': out['v_dn_a_log'], 'v_dn_dt_bias': out['v_dn_dt_bias'], 'v_dn_norm': out['v_dn_norm'], 'v_q_norm': out['v_q_norm'], 'v_k_norm': out['v_k_norm'], 'v_attn_sink': out['v_attn_sink'], 'v_w_branch_dn': out['v_w_branch_dn'], 'v_w_branch_attn': out['v_w_branch_attn'], 'v_w_out': out['v_w_out'], 'v_ffn_up': out['v_ffn_up'], 'v_ffn_conv': out['v_ffn_conv'], 'v_ffn_conv_b': out['v_ffn_conv_b'], 'v_ffn_down': out['v_ffn_down']}


def _loss(weights, diff, rest, loss_target):
    with _jax.named_scope("forward"):
        args = {**rest, TWIN_DIFF_INPUT: diff, **{k: w.astype(_WEIGHT_DTYPES[k]) for k, w in weights.items()}}
        y = _forward(args)
    with _jax.named_scope("loss_head"):
        err = _jnp.square(y.astype(_jnp.float32) - loss_target)
        return 0.5 * _jnp.sum(_jnp.mean(err, axis=-1)) if err.ndim else 0.5 * err


def _adamw(w, g, m, v):
    m = ADAM_B1 * m + (1.0 - ADAM_B1) * g
    v = ADAM_B2 * v + (1.0 - ADAM_B2) * _jnp.square(g)
    m_hat = m / (1.0 - ADAM_B1 ** ADAM_STEP)
    v_hat = v / (1.0 - ADAM_B2 ** ADAM_STEP)
    delta = -ADAM_LR * (m_hat / (_jnp.sqrt(v_hat) + ADAM_EPS) + ADAM_WD * w)
    return delta, m, v


def reference(x, c, ctx, c_ctx, w_ada, b_ada, norm_mix, norm_ffn, w_in, dn_conv, dn_a_log, dn_dt_bias, dn_norm, q_norm, k_norm, attn_sink, w_branch_dn, w_branch_attn, w_out, ffn_up, ffn_conv, ffn_conv_b, ffn_down, loss_target, m_c_ctx, m_w_ada, m_b_ada, m_norm_mix, m_norm_ffn, m_w_in, m_dn_conv, m_dn_a_log, m_dn_dt_bias, m_dn_norm, m_q_norm, m_k_norm, m_attn_sink, m_w_branch_dn, m_w_branch_attn, m_w_out, m_ffn_up, m_ffn_conv, m_ffn_conv_b, m_ffn_down, v_c_ctx, v_w_ada, v_b_ada, v_norm_mix, v_norm_ffn, v_w_in, v_dn_conv, v_dn_a_log, v_dn_dt_bias, v_dn_norm, v_q_norm, v_k_norm, v_attn_sink, v_w_branch_dn, v_w_branch_attn, v_w_out, v_ffn_up, v_ffn_conv, v_ffn_conv_b, v_ffn_down):
    given = dict(x=x, c=c, ctx=ctx, c_ctx=c_ctx, w_ada=w_ada, b_ada=b_ada, norm_mix=norm_mix, norm_ffn=norm_ffn, w_in=w_in, dn_conv=dn_conv, dn_a_log=dn_a_log, dn_dt_bias=dn_dt_bias, dn_norm=dn_norm, q_norm=q_norm, k_norm=k_norm, attn_sink=attn_sink, w_branch_dn=w_branch_dn, w_branch_attn=w_branch_attn, w_out=w_out, ffn_up=ffn_up, ffn_conv=ffn_conv, ffn_conv_b=ffn_conv_b, ffn_down=ffn_down, loss_target=loss_target, m_c_ctx=m_c_ctx, m_w_ada=m_w_ada, m_b_ada=m_b_ada, m_norm_mix=m_norm_mix, m_norm_ffn=m_norm_ffn, m_w_in=m_w_in, m_dn_conv=m_dn_conv, m_dn_a_log=m_dn_a_log, m_dn_dt_bias=m_dn_dt_bias, m_dn_norm=m_dn_norm, m_q_norm=m_q_norm, m_k_norm=m_k_norm, m_attn_sink=m_attn_sink, m_w_branch_dn=m_w_branch_dn, m_w_branch_attn=m_w_branch_attn, m_w_out=m_w_out, m_ffn_up=m_ffn_up, m_ffn_conv=m_ffn_conv, m_ffn_conv_b=m_ffn_conv_b, m_ffn_down=m_ffn_down, v_c_ctx=v_c_ctx, v_w_ada=v_w_ada, v_b_ada=v_b_ada, v_norm_mix=v_norm_mix, v_norm_ffn=v_norm_ffn, v_w_in=v_w_in, v_dn_conv=v_dn_conv, v_dn_a_log=v_dn_a_log, v_dn_dt_bias=v_dn_dt_bias, v_dn_norm=v_dn_norm, v_q_norm=v_q_norm, v_k_norm=v_k_norm, v_attn_sink=v_attn_sink, v_w_branch_dn=v_w_branch_dn, v_w_branch_attn=v_w_branch_attn, v_w_out=v_w_out, v_ffn_up=v_ffn_up, v_ffn_conv=v_ffn_conv, v_ffn_conv_b=v_ffn_conv_b, v_ffn_down=v_ffn_down)
    weights = {n: given[n] for n in TWIN_WEIGHTS}
    shared = {n: given[n] for n in SHARED_INPUTS}
    per_example = {n: given[n] for n in ['x', 'c', 'ctx']}
    grad_fn = _jax.value_and_grad(_loss, argnums=(0, 1))

    def one_microbatch(ex, loss_target):
        ex = dict(ex)
        diff = ex.pop(TWIN_DIFF_INPUT)
        return grad_fn(weights, diff, {**shared, **ex}, loss_target)

    if N_MICROBATCH == 1:
        loss, (grad_w, grad_x) = one_microbatch(per_example, given["loss_target"])
    else:
        def body(carry, xs):
            loss_sum, grad_sum = carry
            l_k, (gw_k, gx_k) = one_microbatch(xs[0], xs[1])
            with _jax.named_scope("update"):
                return (loss_sum + l_k, _jax.tree.map(_jnp.add, grad_sum, gw_k)), gx_k

        init = (_jnp.zeros((), _jnp.float32), _jax.tree.map(_jnp.zeros_like, weights))
        (loss, grad_w), grad_x = _jax.lax.scan(body, init, (per_example, given["loss_target"]))
    with _jax.named_scope("update"):
        delta_w, new_m, new_v = {}, {}, {}
        for n in TWIN_WEIGHTS:
            delta_w[n], new_m[n], new_v[n] = _adamw(weights[n], grad_w[n], given["m_" + n], given["v_" + n])
    return (loss, grad_x, *[grad_w[n] for n in TWIN_WEIGHTS], *[delta_w[n] for n in TWIN_WEIGHTS],
            *[new_m[n] for n in TWIN_WEIGHTS], *[new_v[n] for n in TWIN_WEIGHTS])
```

```python
import functools

import jax
import jax.numpy as jnp
from jax import lax
from jax.experimental import pallas as pl
from jax.experimental.pallas import tpu as pltpu

F32 = jnp.float32
BF = jnp.bfloat16
HI = lax.Precision.HIGHEST
MESH = pl.DeviceIdType.MESH

D = 1024
NH = 8
HD = 128
KVH = 2
GRP = 4
KV = KVH * HD
DFF = 2816
CB = 128
GRID_W = 64
ROPE_BASE = 10000.0
EPS = 1e-6
N_DEV = 8
PW = 8192
O_QKV, O_GT, O_Q, O_MG, O_K, O_V, O_BA = 0, 3072, 4096, 5120, 7168, 7424, 7680
IN_SIZES = (3072, 1024, 16, 16, 1024, 256, 256, 2048)
IN_DIM = sum(IN_SIZES)
ADAM_LR, ADAM_B1, ADAM_B2, ADAM_EPS, ADAM_WD, ADAM_STEP = 0.001, 0.9, 0.999, 1e-08, 0.01, 10
VMEM_LIMIT = 56 * 1024 * 1024


def _cp():
    return pltpu.CompilerParams(vmem_limit_bytes=VMEM_LIMIT)


def _tile(n, cands):
    for c in cands:
        if n % c == 0:
            return c
    return n


def _iota2(shape):
    return lax.broadcasted_iota(jnp.int32, shape, 0), lax.broadcasted_iota(jnp.int32, shape, 1)


_DIMS = {"nn": ((1,), (0,)), "nt": ((1,), (1,)), "tn": ((0,), (0,))}


def _mm(a, b, *, form, out_dtype, name, tm=None, tn=None, tk=None):
    if form == "tn":
        K, M = a.shape
        N = b.shape[1]
    else:
        M, K = a.shape
        N = b.shape[0] if form == "nt" else b.shape[1]
    tm = tm or _tile(M, (1024, 640, 512, 256, 128))
    tn = tn or _tile(N, (512, 256, 128))
    tk = tk or _tile(K, (2048, 1408, 1024, 640, 512, 256, 128))
    nk = K // tk
    dims = (_DIMS[form], ((), ()))

    def body(a_ref, b_ref, o_ref, *acc):
        k = pl.program_id(2)
        part = lax.dot_general(a_ref[...].astype(BF), b_ref[...].astype(BF), dims, preferred_element_type=F32)
        if nk == 1:
            o_ref[...] = part.astype(out_dtype)
        else:
            acc_ref = acc[0]

            @pl.when(k == 0)
            def _():
                acc_ref[...] = part

            @pl.when(k > 0)
            def _():
                acc_ref[...] += part

            @pl.when(k == nk - 1)
            def _():
                o_ref[...] = acc_ref[...].astype(out_dtype)

    if form == "tn":
        a_spec = pl.BlockSpec((tk, tm), lambda i, j, k: (k, i))
    else:
        a_spec = pl.BlockSpec((tm, tk), lambda i, j, k: (i, k))
    if form == "nt":
        b_spec = pl.BlockSpec((tn, tk), lambda i, j, k: (j, k))
    else:
        b_spec = pl.BlockSpec((tk, tn), lambda i, j, k: (k, j))
    return pl.pallas_call(
        body, grid=(M // tm, N // tn, nk), name=name,
        in_specs=[a_spec, b_spec], out_specs=pl.BlockSpec((tm, tn), lambda i, j, k: (i, j)),
        out_shape=jax.ShapeDtypeStruct((M, N), out_dtype),
        scratch_shapes=[] if nk == 1 else [pltpu.VMEM((tm, tn), F32)],
        compiler_params=_cp(),
    )(a, b)


def _norm_mod_fn(x, nw, sh, sc):
    y = x * lax.rsqrt(jnp.mean(x * x, axis=-1, keepdims=True) + EPS)
    return (y * nw) * (1.0 + sc) + sh


def _norm_mod_fwd(x, nw, sh, sc, nlat, name):
    T = x.shape[0]
    tb = _tile(T, (256, 128))
    nlb = nlat // tb

    def body(x_ref, nw_ref, sh_ref, sc_ref, h_ref):
        h_ref[...] = _norm_mod_fn(x_ref[...], nw_ref[...], sh_ref[0], sc_ref[0]).astype(BF)

    seg = pl.BlockSpec((1, 1, D), lambda i: (jnp.where(i >= nlb, 1, 0), 0, 0))
    return pl.pallas_call(
        body, grid=(T // tb,), name=name,
        in_specs=[pl.BlockSpec((tb, D), lambda i: (i, 0)), pl.BlockSpec((1, D), lambda i: (0, 0)), seg, seg],
        out_specs=pl.BlockSpec((tb, D), lambda i: (i, 0)),
        out_shape=jax.ShapeDtypeStruct((T, D), BF),
    )(x, nw, sh, sc)


def _norm_mod_bwd(x, nw, sh, sc, dh, nlat, name):
    T = x.shape[0]
    tb = _tile(T, (256, 128))
    nlb = nlat // tb

    def body(x_ref, nw_ref, sh_ref, sc_ref, dh_ref, dx_ref, dnw_ref, dsh_ref, dsc_ref):
        i = pl.program_id(0)
        _, vjp = jax.vjp(_norm_mod_fn, x_ref[...], nw_ref[...], sh_ref[0], sc_ref[0])
        dx, dnw, dsh, dsc = vjp(dh_ref[...])
        dx_ref[...] = dx

        @pl.when(i == 0)
        def _():
            dnw_ref[...] = jnp.zeros_like(dnw_ref)

        @pl.when((i == 0) | (i == nlb))
        def _():
            dsh_ref[...] = jnp.zeros_like(dsh_ref)
            dsc_ref[...] = jnp.zeros_like(dsc_ref)

        dnw_ref[...] += dnw
        dsh_ref[0] += dsh
        dsc_ref[0] += dsc

    seg = pl.BlockSpec((1, 1, D), lambda i: (jnp.where(i >= nlb, 1, 0), 0, 0))
    row = pl.BlockSpec((tb, D), lambda i: (i, 0))
    one = pl.BlockSpec((1, D), lambda i: (0, 0))
    return pl.pallas_call(
        body, grid=(T // tb,), name=name,
        in_specs=[row, one, seg, seg, row], out_specs=[row, one, seg, seg],
        out_shape=[jax.ShapeDtypeStruct((T, D), F32), jax.ShapeDtypeStruct((1, D), F32),
                   jax.ShapeDtypeStruct((2, 1, D), F32), jax.ShapeDtypeStruct((2, 1, D), F32)],
    )(x, nw, sh, sc, dh)


def _halo_specs(tb, tc, nrows, col0):
    r8 = tb // 8
    cur = pl.BlockSpec((tb, tc), lambda j, i: (i, col0 + j))
    prev = pl.BlockSpec((8, tc), lambda j, i: (jnp.maximum(i * r8 - 1, 0), col0 + j))
    nxt = pl.BlockSpec((8, tc), lambda j, i: (jnp.minimum((i + 1) * r8, nrows // 8 - 1), col0 + j))
    return cur, prev, nxt


def _extend(prev_ref, cur_ref, next_ref, i, starts, ends):
    keep_p = functools.reduce(lambda a, b: a & b, [i != s for s in starts])
    keep_n = functools.reduce(lambda a, b: a & b, [i != e for e in ends])
    p = jnp.where(keep_p, prev_ref[...].astype(F32), 0.0)
    n = jnp.where(keep_n, next_ref[...].astype(F32), 0.0)
    return jnp.concatenate([p, cur_ref[...].astype(F32), n], axis=0)


def _shifted(xe, shift, tb):
    n = tb + 16
    s = shift % n
    xs = xe if s == 0 else pltpu.roll(xe, s, 0)
    return xs[8:8 + tb]


def _conv_fwd(x, w8, bias, *, width, col0, ncols, tc, seg_rows, name):
    T = x.shape[0]
    tb = _tile(T, (256, 128))
    r = width // 2
    bounds = [0]
    for s in seg_rows:
        bounds.append(bounds[-1] + s // tb)
    starts, ends = bounds[:-1], [b - 1 for b in bounds[1:]]

    def body(cur_ref, prev_ref, next_ref, w_ref, b_ref, o_ref):
        i = pl.program_id(1)
        xe = _extend(prev_ref, cur_ref, next_ref, i, starts, ends)
        acc = _shifted(xe, r, tb) * w_ref[0:1, :]
        for j in range(1, width):
            acc = acc + _shifted(xe, r - j, tb) * w_ref[j:j + 1, :]
        o_ref[...] = acc + b_ref[...]

    cur, prev, nxt = _halo_specs(tb, tc, T, col0)
    return pl.pallas_call(
        body, grid=(ncols // tc, T // tb), name=name,
        in_specs=[cur, prev, nxt, pl.BlockSpec((8, tc), lambda j, i: (0, j)), pl.BlockSpec((1, tc), lambda j, i: (0, j))],
        out_specs=pl.BlockSpec((tb, tc), lambda j, i: (i, j)),
        out_shape=jax.ShapeDtypeStruct((T, ncols), F32),
    )(x, x, x, w8, bias)


def _conv_bwd(x, dc, w8, *, width, col0, ncols, tc, seg_rows, name):
    T = x.shape[0]
    tb = _tile(T, (256, 128))
    r = width // 2
    bounds = [0]
    for s in seg_rows:
        bounds.append(bounds[-1] + s // tb)
    starts, ends = bounds[:-1], [b - 1 for b in bounds[1:]]

    def body(cur_ref, prev_ref, next_ref, dcur_ref, dprev_ref, dnext_ref, w_ref, dx_ref, dw_ref, db_ref):
        i = pl.program_id(1)
        xe = _extend(prev_ref, cur_ref, next_ref, i, starts, ends)
        de = _extend(dprev_ref, dcur_ref, dnext_ref, i, starts, ends)
        dcur = dcur_ref[...]

        @pl.when(i == 0)
        def _():
            dw_ref[...] = jnp.zeros_like(dw_ref)
            db_ref[...] = jnp.zeros_like(db_ref)

        acc = _shifted(de, -r, tb) * w_ref[0:1, :]
        for j in range(1, width):
            acc = acc + _shifted(de, j - r, tb) * w_ref[j:j + 1, :]
        dx_ref[...] = acc.astype(BF)
        for j in range(width):
            dw_ref[j:j + 1, :] += jnp.sum(dcur * _shifted(xe, r - j, tb), axis=0, keepdims=True)
        db_ref[...] += jnp.sum(dcur, axis=0, keepdims=True)

    cur, prev, nxt = _halo_specs(tb, tc, T, col0)
    dcur, dprev, dnxt = _halo_specs(tb, tc, T, 0)
    wspec = pl.BlockSpec((8, tc), lambda j, i: (0, j))
    return pl.pallas_call(
        body, grid=(ncols // tc, T // tb), name=name,
        in_specs=[cur, prev, nxt, dcur, dprev, dnxt, wspec],
        out_specs=[pl.BlockSpec((tb, tc), lambda j, i: (i, j)), wspec, pl.BlockSpec((1, tc), lambda j, i: (0, j))],
        out_shape=[jax.ShapeDtypeStruct((T, ncols), BF), jax.ShapeDtypeStruct((8, ncols), F32),
                   jax.ShapeDtypeStruct((1, ncols), F32)],
    )(x, x, x, dc, dc, dc, w8)


def _softplus(x):
    return jnp.maximum(x, 0.0) + jnp.log(1.0 + jnp.exp(-jnp.abs(x)))


def _gates_fn(ba, alog_row, dt_row):
    col = lax.broadcasted_iota(jnp.int32, ba.shape, 1)
    beta = jax.nn.sigmoid(ba)
    g = -jnp.exp(alog_row) * _softplus(ba + dt_row)
    return jnp.where(col < 16, beta, jnp.where(col < 32, g, 0.0))


def _qkv_post_fn(c, kind):
    y = jax.nn.silu(c)
    if kind == 2:
        return y
    n = y * lax.rsqrt(jnp.sum(y * y, axis=-1, keepdims=True) + EPS)
    return n * (HD ** -0.5) if kind == 0 else n


def _dn_post_fwd(conv, p, alog_row, dt_row):
    T = conv.shape[0]
    tb = _tile(T, (256, 128))

    def body(c_ref, ba_ref, al_ref, dt_ref, q_ref, k_ref, v_ref, gb_ref):
        outs = (q_ref, k_ref, v_ref)
        for kind in range(3):
            for h in range(NH):
                src = slice(kind * D + h * HD, kind * D + (h + 1) * HD)
                outs[kind][:, h * HD:(h + 1) * HD] = _qkv_post_fn(c_ref[:, src], kind)
        gb_ref[...] = _gates_fn(ba_ref[...], al_ref[...], dt_ref[...])

    row = pl.BlockSpec((tb, D), lambda i: (i, 0))
    one = pl.BlockSpec((1, 128), lambda i: (0, 0))
    return pl.pallas_call(
        body, grid=(T // tb,), name="dn_post_fwd",
        in_specs=[pl.BlockSpec((tb, 3 * D), lambda i: (i, 0)), pl.BlockSpec((tb, 128), lambda i: (i, O_BA // 128)), one, one],
        out_specs=[row, row, row, pl.BlockSpec((tb, 128), lambda i: (i, 0))],
        out_shape=[jax.ShapeDtypeStruct((T, D), F32)] * 3 + [jax.ShapeDtypeStruct((T, 128), F32)],
    )(conv, p, alog_row, dt_row)


def _dn_post_bwd(conv, p, alog_row, dt_row, dq, dk, dv, dgb):
    T = conv.shape[0]
    tb = _tile(T, (256, 128))

    def body(c_ref, ba_ref, al_ref, dt_ref, dq_ref, dk_ref, dv_ref, dgb_ref, dc_ref, dba_ref, dal_ref, ddt_ref):
        i = pl.program_id(0)
        douts = (dq_ref, dk_ref, dv_ref)
        for kind in range(3):
            for h in range(NH):
                src = slice(kind * D + h * HD, kind * D + (h + 1) * HD)
                _, vjp = jax.vjp(functools.partial(_qkv_post_fn, kind=kind), c_ref[:, src])
                dc_ref[:, src] = vjp(douts[kind][:, h * HD:(h + 1) * HD])[0]
        _, vjp = jax.vjp(_gates_fn, ba_ref[...], al_ref[...], dt_ref[...])
        dba, dal, ddt = vjp(dgb_ref[...])
        dba_ref[...] = dba.astype(BF)

        @pl.when(i == 0)
        def _():
            dal_ref[...] = jnp.zeros_like(dal_ref)
            ddt_ref[...] = jnp.zeros_like(ddt_ref)
        dal_ref[...] += dal
        ddt_ref[...] += ddt

    row = pl.BlockSpec((tb, D), lambda i: (i, 0))
    one = pl.BlockSpec((1, 128), lambda i: (0, 0))
    nar = pl.BlockSpec((tb, 128), lambda i: (i, 0))
    return pl.pallas_call(
        body, grid=(T // tb,), name="dn_post_bwd",
        in_specs=[pl.BlockSpec((tb, 3 * D), lambda i: (i, 0)), pl.BlockSpec((tb, 128), lambda i: (i, O_BA // 128)), one, one,
                  row, row, row, nar],
        out_specs=[pl.BlockSpec((tb, 3 * D), lambda i: (i, 0)), nar, one, one],
        out_shape=[jax.ShapeDtypeStruct((T, 3 * D), F32), jax.ShapeDtypeStruct((T, 128), BF),
                   jax.ShapeDtypeStruct((1, 128), F32), jax.ShapeDtypeStruct((1, 128), F32)],
    )(conv, p, alog_row, dt_row, dq, dk, dv, dgb)


def _dot_hi(a, b):
    return jnp.dot(a, b, precision=HI, preferred_element_type=F32)


def _dot_bf(a, b):
    return jnp.dot(a.astype(BF), b.astype(BF), preferred_element_type=F32)


def _dot_nt_bf(a, b):
    return lax.dot_general(a.astype(BF), b.astype(BF), (_DIMS["nt"], ((), ())), preferred_element_type=F32)


def _dot_tn_bf(a, b):
    return lax.dot_general(a.astype(BF), b.astype(BF), (_DIMS["tn"], ((), ())), preferred_element_type=F32)


def _unit_tri_inverse(a):
    r, c = _iota2((CB, CB))
    eye = (r == c).astype(F32)
    a8 = jnp.where((r // 8) == (c // 8), a, 0.0)
    a2 = _dot_hi(a8, a8)
    a4 = _dot_hi(a2, a2)
    t = _dot_hi(_dot_hi(eye - a8, eye + a2), eye + a4)
    b = 8
    while b < CB:
        e = jnp.where(((r // (2 * b)) == (c // (2 * b))) & ((r // b) != (c // b)), a, 0.0)
        t = t - _dot_hi(_dot_hi(t, e), t)
        b *= 2
    return t


def _dn1_head(q, k, v, gb, hsel_b, hsel_g, reverse):
    r, c = _iota2((CB, CB))
    incl = (c >= r) if reverse else (c <= r)
    strict = (c > r) if reverse else (c < r)
    gcum = _dot_hi(incl.astype(F32), gb)
    beta = _dot_hi(gb, hsel_b)
    gc = _dot_hi(gcum, hsel_g)
    decay = jnp.where(incl, jnp.exp(jnp.where(incl, gc - gc.T, 0.0)), 0.0)
    a = jnp.where(strict, beta * _dot_nt_bf(k, k) * decay, 0.0)
    t = _unit_tri_inverse(a)
    eg = jnp.exp(gc)
    u = _dot_hi(t, v * beta)
    w = _dot_hi(t, k * (beta * eg))
    qkd = _dot_nt_bf(q, k) * decay
    last = 0 if reverse else CB - 1
    glog = jnp.sum(jnp.where(r == last, gc, 0.0), axis=0, keepdims=True)
    return u, w, q * eg, k * jnp.exp(glog - gc), qkd, jnp.exp(glog)


def _sel(col):
    r, _ = _iota2((128, 128))
    return (r == col).astype(F32)


def _dn1_fwd(q, k, v, gb, direction):
    T = q.shape[0]
    nb = T // CB

    def body(q_ref, k_ref, v_ref, gb_ref, u_ref, w_ref, qg_ref, kd_ref, qkd_ref, gl_ref):
        h = pl.program_id(1)
        u, w, qg, kd, qkd, gl = _dn1_head(q_ref[...], k_ref[...], v_ref[...], gb_ref[...],
                                           _sel(direction * NH + h), _sel(16 + direction * NH + h), direction == 1)
        u_ref[...] = u
        w_ref[...] = w.astype(BF)
        qg_ref[...] = qg.astype(BF)
        kd_ref[...] = kd.astype(BF)
        qkd_ref[...] = qkd.astype(BF)
        gl_ref[0] = gl

    hb = pl.BlockSpec((CB, HD), lambda i, h: (i, h))
    return pl.pallas_call(
        body, grid=(nb, NH), name=f"dn1_fwd_{direction}",
        in_specs=[hb, hb, hb, pl.BlockSpec((CB, 128), lambda i, h: (i, 0))],
        out_specs=[hb, hb, hb, hb, hb, pl.BlockSpec((1, 1, 128), lambda i, h: (i * NH + h, 0, 0))],
        out_shape=[jax.ShapeDtypeStruct((T, D), F32)] + [jax.ShapeDtypeStruct((T, D), BF)] * 4
        + [jax.ShapeDtypeStruct((nb * NH, 1, 128), F32)],
    )(q, k, v, gb)


def _dn1_bwd(q, k, v, gb, du, dw, dqg, dkd, dqkd, dgl, direction):
    T = q.shape[0]
    nb = T // CB

    def body(q_ref, k_ref, v_ref, gb_ref, du_ref, dw_ref, dqg_ref, dkd_ref, dqkd_ref, dgl_ref, dq_ref, dk_ref, dv_ref, dgb_ref):
        h = pl.program_id(1)
        f = functools.partial(_dn1_head, hsel_b=_sel(direction * NH + h), hsel_g=_sel(16 + direction * NH + h),
                              reverse=direction == 1)
        _, vjp = jax.vjp(f, q_ref[...], k_ref[...], v_ref[...], gb_ref[...])
        dq, dk, dv, dgb = vjp((du_ref[...], dw_ref[...], dqg_ref[...], dkd_ref[...], dqkd_ref[...], dgl_ref[0]))
        dq_ref[...] = dq
        dk_ref[...] = dk
        dv_ref[...] = dv

        @pl.when(h == 0)
        def _():
            dgb_ref[...] = jnp.zeros_like(dgb_ref)
        dgb_ref[...] += dgb

    hb = pl.BlockSpec((CB, HD), lambda i, h: (i, h))
    gbs = pl.BlockSpec((CB, 128), lambda i, h: (i, 0))
    gls = pl.BlockSpec((1, 1, 128), lambda i, h: (i * NH + h, 0, 0))
    return pl.pallas_call(
        body, grid=(nb, NH), name=f"dn1_bwd_{direction}",
        in_specs=[hb, hb, hb, gbs, hb, hb, hb, hb, hb, gls], out_specs=[hb, hb, hb, gbs],
        out_shape=[jax.ShapeDtypeStruct((T, D), F32)] * 3 + [jax.ShapeDtypeStruct((T, 128), F32)],
    )(q, k, v, gb, du, dw, dqg, dkd, dqkd, dgl)


def _dn2_step(u, w, qg, kd, qkd, glrow, s):
    v_new = u - _dot_bf(w, s)
    o = _dot_bf(qg, s) + _dot_bf(qkd, v_new)
    return o, s * glrow + _dot_tn_bf(kd, v_new)


def _scan_order(direction, nlat_b, nall_b):
    if direction == 0:
        return lambda i: (i + nlat_b) % nall_b
    return lambda i: nall_b - 1 - i


def _dn2_fwd(u, w, qg, kd, qkd, gl, direction, nlat):
    T = u.shape[0]
    nb = T // CB
    blk = _scan_order(direction, nlat // CB, nb)

    def body(u_ref, w_ref, qg_ref, kd_ref, qkd_ref, gl_ref, o_ref, sall_ref, s_scr):
        @pl.when(pl.program_id(0) == 0)
        def _():
            s_scr[...] = jnp.zeros_like(s_scr)
        sall_ref[0] = s_scr[...]
        for h in range(NH):
            sl = slice(h * HD, (h + 1) * HD)
            o, s_next = _dn2_step(u_ref[:, sl], w_ref[:, sl], qg_ref[:, sl], kd_ref[:, sl], qkd_ref[:, sl], gl_ref[h], s_scr[h])
            o_ref[:, sl] = o
            s_scr[h] = s_next

    tb = pl.BlockSpec((CB, D), lambda i: (blk(i), 0))
    return pl.pallas_call(
        body, grid=(nb,), name=f"dn2_fwd_{direction}",
        in_specs=[tb] * 5 + [pl.BlockSpec((NH, 1, 128), lambda i: (blk(i), 0, 0))],
        out_specs=[tb, pl.BlockSpec((1, NH, HD, HD), lambda i: (blk(i), 0, 0, 0))],
        out_shape=[jax.ShapeDtypeStruct((T, D), F32), jax.ShapeDtypeStruct((nb, NH, HD, HD), F32)],
        scratch_shapes=[pltpu.VMEM((NH, HD, HD), F32)],
    )(u, w, qg, kd, qkd, gl)


def _dn2_bwd(u, w, qg, kd, qkd, gl, sall, do, direction, nlat):
    T = u.shape[0]
    nb = T // CB
    nlat_b = nlat // CB
    fwd_blk = _scan_order(direction, nlat_b, nb)
    blk = lambda i: fwd_blk(nb - 1 - i)

    def body(u_ref, w_ref, qg_ref, kd_ref, qkd_ref, gl_ref, sall_ref, do_ref,
             du_ref, dw_ref, dqg_ref, dkd_ref, dqkd_ref, dgl_ref, ds_scr):
        i = pl.program_id(0)

        @pl.when(i == 0)
        def _():
            ds_scr[...] = jnp.zeros_like(ds_scr)
        is_lat = blk(i) < nlat_b
        for h in range(NH):
            sl = slice(h * HD, (h + 1) * HD)
            args = (u_ref[:, sl], w_ref[:, sl].astype(F32), qg_ref[:, sl].astype(F32), kd_ref[:, sl].astype(F32),
                    qkd_ref[:, sl].astype(F32), gl_ref[h], sall_ref[0, h])
            _, vjp = jax.vjp(_dn2_step, *args)
            du, dw, dqg, dkd, dqkd, dgl, ds = vjp((jnp.where(is_lat, do_ref[:, sl], 0.0), ds_scr[h]))
            du_ref[:, sl] = du
            dw_ref[:, sl] = dw
            dqg_ref[:, sl] = dqg
            dkd_ref[:, sl] = dkd
            dqkd_ref[:, sl] = dqkd
            dgl_ref[h] = dgl
            ds_scr[h] = ds

    tb = pl.BlockSpec((CB, D), lambda i: (blk(i), 0))
    gls = pl.BlockSpec((NH, 1, 128), lambda i: (blk(i), 0, 0))
    return pl.pallas_call(
        body, grid=(nb,), name=f"dn2_bwd_{direction}",
        in_specs=[tb] * 5 + [gls, pl.BlockSpec((1, NH, HD, HD), lambda i: (blk(i), 0, 0, 0)),
                             pl.BlockSpec((CB, D), lambda i: (jnp.minimum(blk(i), nlat_b - 1), 0))],
        out_specs=[tb] * 5 + [gls],
        out_shape=[jax.ShapeDtypeStruct((T, D), F32)] * 5 + [jax.ShapeDtypeStruct((nb * NH, 1, 128), F32)],
        scratch_shapes=[pltpu.VMEM((NH, HD, HD), F32)],
    )(u, w, qg, kd, qkd, gl, sall, do)


def _ghn_fn(o, gt, w):
    y = o * lax.rsqrt(jnp.mean(o * o, axis=-1, keepdims=True) + EPS)
    return (y * w) * jax.nn.silu(gt)


def _ghn_fwd(o_f, o_b, p, w, nlat):
    tb = _tile(nlat, (256, 128))

    def body(of_ref, ob_ref, gt_ref, w_ref, y_ref):
        for h in range(NH):
            sl = slice(h * HD, (h + 1) * HD)
            y_ref[:, sl] = _ghn_fn(of_ref[:, sl] + ob_ref[:, sl], gt_ref[:, sl], w_ref[...]).astype(BF)

    row = pl.BlockSpec((tb, D), lambda i: (i, 0))
    return pl.pallas_call(
        body, grid=(nlat // tb,), name="ghn_fwd",
        in_specs=[row, row, pl.BlockSpec((tb, D), lambda i: (i, O_GT // D)), pl.BlockSpec((1, HD), lambda i: (0, 0))],
        out_specs=row, out_shape=jax.ShapeDtypeStruct((nlat, D), BF),
    )(o_f, o_b, p, w)


def _ghn_bwd(o_f, o_b, p, w, dy, nlat):
    tb = _tile(nlat, (256, 128))

    def body(of_ref, ob_ref, gt_ref, w_ref, dy_ref, do_ref, dgt_ref, dw_ref):
        @pl.when(pl.program_id(0) == 0)
        def _():
            dw_ref[...] = jnp.zeros_like(dw_ref)
        for h in range(NH):
            sl = slice(h * HD, (h + 1) * HD)
            _, vjp = jax.vjp(_ghn_fn, of_ref[:, sl] + ob_ref[:, sl], gt_ref[:, sl], w_ref[...])
            do, dgt, dw = vjp(dy_ref[:, sl])
            do_ref[:, sl] = do
            dgt_ref[:, sl] = dgt.astype(BF)
            dw_ref[...] += dw

    row = pl.BlockSpec((tb, D), lambda i: (i, 0))
    one = pl.BlockSpec((1, HD), lambda i: (0, 0))
    return pl.pallas_call(
        body, grid=(nlat // tb,), name="ghn_bwd",
        in_specs=[row, row, pl.BlockSpec((tb, D), lambda i: (i, O_GT // D)), one, row],
        out_specs=[row, row, one],
        out_shape=[jax.ShapeDtypeStruct((nlat, D), F32), jax.ShapeDtypeStruct((nlat, D), BF), jax.ShapeDtypeStruct((1, HD), F32)],
    )(o_f, o_b, p, w, dy)


@jax.custom_vjp
def _swap32(x):
    lane = lax.broadcasted_iota(jnp.int32, x.shape, 1)
    return jnp.where((lane & 32) == 0, pltpu.roll(x, 96, 1), pltpu.roll(x, 32, 1))


_swap32.defvjp(lambda x: (_swap32(x), None), lambda _, g: (_swap32(g),))


def _qk_post_fn(x, w, cos, sin):
    y = (x * lax.rsqrt(jnp.mean(x * x, axis=-1, keepdims=True) + EPS)) * w
    return y * cos + _swap32(y) * sin


def _attn_prep_fwd(p, qn, kn, cos, sin):
    T = p.shape[0]
    tb = _tile(T, (256, 128))

    def body(q_ref, k_ref, v_ref, qn_ref, kn_ref, cos_ref, sin_ref, qr_ref, kr_ref, vb_ref):
        cos_v, sin_v = cos_ref[...], sin_ref[...]
        for h in range(NH):
            sl = slice(h * HD, (h + 1) * HD)
            qr_ref[:, sl] = _qk_post_fn(q_ref[:, sl], qn_ref[...], cos_v, sin_v).astype(BF)
        for h in range(KVH):
            sl = slice(h * HD, (h + 1) * HD)
            kr_ref[:, sl] = _qk_post_fn(k_ref[:, sl], kn_ref[...], cos_v, sin_v).astype(BF)
        vb_ref[...] = v_ref[...].astype(BF)

    one = pl.BlockSpec((1, HD), lambda i: (0, 0))
    tab = pl.BlockSpec((tb, HD), lambda i: (i, 0))
    return pl.pallas_call(
        body, grid=(T // tb,), name="attn_prep_fwd",
        in_specs=[pl.BlockSpec((tb, D), lambda i: (i, O_Q // D)), pl.BlockSpec((tb, KV), lambda i: (i, O_K // KV)),
                  pl.BlockSpec((tb, KV), lambda i: (i, O_V // KV)), one, one, tab, tab],
        out_specs=[pl.BlockSpec((tb, D), lambda i: (i, 0)), pl.BlockSpec((tb, KV), lambda i: (i, 0)),
                   pl.BlockSpec((tb, KV), lambda i: (i, 0))],
        out_shape=[jax.ShapeDtypeStruct((T, D), BF), jax.ShapeDtypeStruct((T, KV), BF), jax.ShapeDtypeStruct((T, KV), BF)],
    )(p, p, p, qn, kn, cos, sin)


def _attn_prep_bwd(p, qn, kn, cos, sin, dqr, dkp, dvp, dkc, dvc, nlat):
    T = p.shape[0]
    nqb = nlat // CB
    ncb = (T - nlat) // CB

    def body(q_ref, k_ref, v_ref, qn_ref, kn_ref, cos_ref, sin_ref, dqr_ref, dka_ref, dkb_ref, dkc3_ref, dva_ref, dvb_ref, dvc3_ref,
             dkctx_ref, dvctx_ref, dq_ref, dk_ref, dv_ref, dqn_ref, dkn_ref):
        i = pl.program_id(0)
        is_lat = i < nqb
        cos_v, sin_v = cos_ref[...], sin_ref[...]

        @pl.when(i == 0)
        def _():
            dqn_ref[...] = jnp.zeros_like(dqn_ref)
            dkn_ref[...] = jnp.zeros_like(dkn_ref)

        def band_sum(a_ref, b_ref, c_ref, ctx_ref):
            s = b_ref[0] + jnp.where(i > 0, a_ref[0], 0.0) + jnp.where(i < nqb - 1, c_ref[0], 0.0)
            return jnp.where(is_lat, s, ctx_ref[...])

        dkr = band_sum(dka_ref, dkb_ref, dkc3_ref, dkctx_ref)
        dv_ref[...] = band_sum(dva_ref, dvb_ref, dvc3_ref, dvctx_ref).astype(BF)
        for h in range(NH):
            sl = slice(h * HD, (h + 1) * HD)
            _, vjp = jax.vjp(_qk_post_fn, q_ref[:, sl], qn_ref[...], cos_v, sin_v)
            dq, dqn, _, _ = vjp(jnp.where(is_lat, dqr_ref[:, sl], 0.0))
            dq_ref[:, sl] = dq.astype(BF)
            dqn_ref[...] += dqn
        for h in range(KVH):
            sl = slice(h * HD, (h + 1) * HD)
            _, vjp = jax.vjp(_qk_post_fn, k_ref[:, sl], kn_ref[...], cos_v, sin_v)
            dk, dkn, _, _ = vjp(dkr[:, sl])
            dk_ref[:, sl] = dk.astype(BF)
            dkn_ref[...] += dkn

    one = pl.BlockSpec((1, HD), lambda i: (0, 0))
    tab = pl.BlockSpec((CB, HD), lambda i: (i, 0))
    lat = lambda i: jnp.minimum(i, nqb - 1)

    def part(off, slot):
        return pl.BlockSpec((1, CB, KV), lambda i: (jnp.clip(lat(i) + off, 0, nqb - 1) * 3 + slot, 0, 0))

    ctxs = pl.BlockSpec((CB, KV), lambda i: (jnp.clip(i - nqb, 0, ncb - 1), 0))
    kvs = pl.BlockSpec((CB, KV), lambda i: (i, 0))
    return pl.pallas_call(
        body, grid=(T // CB,), name="attn_prep_bwd",
        in_specs=[pl.BlockSpec((CB, D), lambda i: (i, O_Q // D)), pl.BlockSpec((CB, KV), lambda i: (i, O_K // KV)),
                  pl.BlockSpec((CB, KV), lambda i: (i, O_V // KV)), one, one, tab, tab,
                  pl.BlockSpec((CB, D), lambda i: (lat(i), 0)),
                  part(-1, 2), part(0, 1), part(1, 0), part(-1, 2), part(0, 1), part(1, 0), ctxs, ctxs],
        out_specs=[pl.BlockSpec((CB, D), lambda i: (i, 0)), kvs, kvs, one, one],
        out_shape=[jax.ShapeDtypeStruct((T, D), BF), jax.ShapeDtypeStruct((T, KV), BF), jax.ShapeDtypeStruct((T, KV), BF),
                   jax.ShapeDtypeStruct((1, HD), F32), jax.ShapeDtypeStruct((1, HD), F32)],
    )(p, p, p, qn, kn, cos, sin, dqr, dkp, dkp, dkp, dvp, dvp, dvp, dkc, dvc)


def _attn_group_fn(q0, q1, q2, q3, kall, vall, s0, s1, s2, s3, mask):
    q = jnp.concatenate([q0, q1, q2, q3], axis=0)
    s = _dot_nt_bf(q, kall) * (HD ** -0.5)
    s = jnp.where(mask, s, -1e30)
    sk = jnp.concatenate([jnp.broadcast_to(jnp.mean(t, axis=1, keepdims=True), (CB, 1)) for t in (s0, s1, s2, s3)], axis=0)
    m = lax.stop_gradient(jnp.maximum(jnp.max(s, axis=1, keepdims=True), sk))
    e = jnp.exp(s - m)
    den = jnp.sum(e, axis=1, keepdims=True) + jnp.exp(sk - m)
    return _dot_bf(e / den, vall)


def _attn_mask(i, nqb, lc):
    r, c = _iota2((GRP * CB, 3 * CB + lc))
    rel = c - (r & (CB - 1))
    lo = jnp.where(i > 0, 0, CB)
    hi = jnp.where(i < nqb - 1, 3 * CB, 2 * CB)
    return ((rel >= 0) & (rel <= 2 * CB) & (c >= lo) & (c < hi)) | (c >= 3 * CB)


def _attn_specs(nqb, lc, nlat):
    qs = pl.BlockSpec((CB, GRP * HD), lambda kh, i: (i, kh))
    ka = pl.BlockSpec((CB, HD), lambda kh, i: (jnp.maximum(i - 1, 0), kh))
    kb = pl.BlockSpec((CB, HD), lambda kh, i: (i, kh))
    kc = pl.BlockSpec((CB, HD), lambda kh, i: (jnp.minimum(i + 1, nqb - 1), kh))
    kx = pl.BlockSpec((lc, HD), lambda kh, i: (nlat // lc, kh))
    sk = pl.BlockSpec((1, 8, 128), lambda kh, i: (kh, 0, 0))
    return qs, ka, kb, kc, kx, sk


def _attn_fwd(qr, kr, vb, sink, nlat):
    lc = kr.shape[0] - nlat
    nqb = nlat // CB
    qs, ka, kb, kc, kx, sk = _attn_specs(nqb, lc, nlat)

    def body(q_ref, ka_ref, kb_ref, kc_ref, kx_ref, va_ref, vb_ref, vc_ref, vx_ref, sk_ref, o_ref):
        i = pl.program_id(1)
        kall = jnp.concatenate([ka_ref[...], kb_ref[...], kc_ref[...], kx_ref[...]], axis=0)
        vall = jnp.concatenate([va_ref[...], vb_ref[...], vc_ref[...], vx_ref[...]], axis=0)
        qh = [q_ref[:, g * HD:(g + 1) * HD] for g in range(GRP)]
        sinks = [sk_ref[0, g:g + 1, :] for g in range(GRP)]
        o = _attn_group_fn(*qh, kall, vall, *sinks, _attn_mask(i, nqb, lc))
        for g in range(GRP):
            o_ref[:, g * HD:(g + 1) * HD] = o[g * CB:(g + 1) * CB].astype(BF)

    return pl.pallas_call(
        body, grid=(KVH, nqb), name="attn_fwd",
        in_specs=[qs, ka, kb, kc, kx, ka, kb, kc, kx, sk], out_specs=qs,
        out_shape=jax.ShapeDtypeStruct((nlat, D), BF), compiler_params=_cp(),
    )(qr, kr, kr, kr, kr, vb, vb, vb, vb, sink)


def _attn_bwd(qr, kr, vb, sink, dy, nlat):
    lc = kr.shape[0] - nlat
    nqb = nlat // CB
    qs, ka, kb, kc, kx, sk = _attn_specs(nqb, lc, nlat)

    def body(q_ref, ka_ref, kb_ref, kc_ref, kx_ref, va_ref, vb_ref, vc_ref, vx_ref, sk_ref, dy_ref,
             dq_ref, dkp_ref, dvp_ref, dkx_ref, dvx_ref, dsk_ref):
        i = pl.program_id(1)
        kall = jnp.concatenate([ka_ref[...], kb_ref[...], kc_ref[...], kx_ref[...]], axis=0).astype(F32)
        vall = jnp.concatenate([va_ref[...], vb_ref[...], vc_ref[...], vx_ref[...]], axis=0).astype(F32)
        qh = [q_ref[:, g * HD:(g + 1) * HD].astype(F32) for g in range(GRP)]
        f = functools.partial(_attn_group_fn, mask=_attn_mask(i, nqb, lc))
        _, vjp = jax.vjp(f, *qh, kall, vall, *[sk_ref[0, g:g + 1, :] for g in range(GRP)])
        dyv = jnp.concatenate([dy_ref[:, g * HD:(g + 1) * HD] for g in range(GRP)], axis=0)
        d = vjp(dyv)
        for g in range(GRP):
            dq_ref[:, g * HD:(g + 1) * HD] = d[g]
        dk, dv = d[4], d[5]
        for t in range(3):
            dkp_ref[t] = dk[t * CB:(t + 1) * CB]
            dvp_ref[t] = dv[t * CB:(t + 1) * CB]

        @pl.when(i == 0)
        def _():
            dkx_ref[...] = jnp.zeros_like(dkx_ref)
            dvx_ref[...] = jnp.zeros_like(dvx_ref)
            dsk_ref[...] = jnp.zeros_like(dsk_ref)
        dkx_ref[...] += dk[3 * CB:]
        dvx_ref[...] += dv[3 * CB:]
        for g in range(GRP):
            dsk_ref[0, g:g + 1, :] += d[6 + g]

    dys = pl.BlockSpec((CB, GRP * HD), lambda kh, i: (i, kh))
    parts = pl.BlockSpec((3, CB, HD), lambda kh, i: (i, 0, kh))
    ctxo = pl.BlockSpec((lc, HD), lambda kh, i: (0, kh))
    return pl.pallas_call(
        body, grid=(KVH, nqb), name="attn_bwd",
        in_specs=[qs, ka, kb, kc, kx, ka, kb, kc, kx, sk, dys],
        out_specs=[dys, parts, parts, ctxo, ctxo, sk],
        out_shape=[jax.ShapeDtypeStruct((nlat, D), F32), jax.ShapeDtypeStruct((3 * nqb, CB, KV), F32),
                   jax.ShapeDtypeStruct((3 * nqb, CB, KV), F32), jax.ShapeDtypeStruct((lc, KV), F32),
                   jax.ShapeDtypeStruct((lc, KV), F32), jax.ShapeDtypeStruct((KVH, 8, 128), F32)],
        compiler_params=_cp(),
    )(qr, kr, kr, kr, kr, vb, vb, vb, vb, sink, dy)


def _merge_fn(z_dn, z_at, g_dn, g_at):
    return jax.nn.sigmoid(g_dn) * z_dn + jax.nn.sigmoid(g_at) * z_at


def _merge_fwd(z_dn, z_at, p, nlat):
    tb = _tile(nlat, (256, 128))

    def body(zd_ref, za_ref, gd_ref, ga_ref, o_ref):
        o_ref[...] = _merge_fn(zd_ref[...], za_ref[...], gd_ref[...], ga_ref[...]).astype(BF)

    row = pl.BlockSpec((tb, D), lambda i: (i, 0))
    return pl.pallas_call(
        body, grid=(nlat // tb,), name="merge_fwd",
        in_specs=[row, row, pl.BlockSpec((tb, D), lambda i: (i, O_MG // D)), pl.BlockSpec((tb, D), lambda i: (i, O_MG // D + 1))],
        out_specs=row, out_shape=jax.ShapeDtypeStruct((nlat, D), BF),
    )(z_dn, z_at, p, p)


def _merge_bwd(z_dn, z_at, p, dm, nlat):
    tb = _tile(nlat, (256, 128))

    def body(zd_ref, za_ref, gd_ref, ga_ref, dm_ref, dzd_ref, dza_ref, dg_ref):
        _, vjp = jax.vjp(_merge_fn, zd_ref[...], za_ref[...], gd_ref[...], ga_ref[...])
        dzd, dza, dgd, dga = vjp(dm_ref[...])
        dzd_ref[...] = dzd.astype(BF)
        dza_ref[...] = dza.astype(BF)
        dg_ref[:, :D] = dgd.astype(BF)
        dg_ref[:, D:] = dga.astype(BF)

    row = pl.BlockSpec((tb, D), lambda i: (i, 0))
    return pl.pallas_call(
        body, grid=(nlat // tb,), name="merge_bwd",
        in_specs=[row, row, pl.BlockSpec((tb, D), lambda i: (i, O_MG // D)), pl.BlockSpec((tb, D), lambda i: (i, O_MG // D + 1)), row],
        out_specs=[row, row, pl.BlockSpec((tb, 2 * D), lambda i: (i, 0))],
        out_shape=[jax.ShapeDtypeStruct((nlat, D), BF), jax.ShapeDtypeStruct((nlat, D), BF), jax.ShapeDtypeStruct((nlat, 2 * D), BF)],
    )(z_dn, z_at, p, p, dm)


def _resid_fwd(x, gate, y):
    n = y.shape[0]
    tb = _tile(n, (256, 128))

    def body(x_ref, g_ref, y_ref, o_ref):
        o_ref[...] = x_ref[...] + g_ref[...] * y_ref[...]

    row = pl.BlockSpec((tb, D), lambda i: (i, 0))
    return pl.pallas_call(
        body, grid=(n // tb,), name="resid_fwd",
        in_specs=[row, pl.BlockSpec((1, D), lambda i: (0, 0)), row], out_specs=row,
        out_shape=jax.ShapeDtypeStruct((n, D), F32),
    )(x, gate, y)


def _resid_bwd(dx1a, dx1b, gate, y):
    n = y.shape[0]
    tb = _tile(n, (256, 128))

    def body(a_ref, b_ref, g_ref, y_ref, dx_ref, dy_ref, dg_ref):
        dx = a_ref[...] + b_ref[...]
        dx_ref[...] = dx
        dy_ref[...] = (g_ref[...] * dx).astype(BF)

        @pl.when(pl.program_id(0) == 0)
        def _():
            dg_ref[...] = jnp.zeros_like(dg_ref)
        dg_ref[...] += jnp.sum(dx * y_ref[...], axis=0, keepdims=True)

    row = pl.BlockSpec((tb, D), lambda i: (i, 0))
    one = pl.BlockSpec((1, D), lambda i: (0, 0))
    return pl.pallas_call(
        body, grid=(n // tb,), name="resid_bwd",
        in_specs=[row, row, one, row], out_specs=[row, row, one],
        out_shape=[jax.ShapeDtypeStruct((n, D), F32), jax.ShapeDtypeStruct((n, D), BF), jax.ShapeDtypeStruct((1, D), F32)],
    )(dx1a, dx1b, gate, y)


def _swiglu_fn(ug, uv):
    return jax.nn.silu(ug) * uv


def _swiglu_fwd(uc):
    n = uc.shape[0]
    tb = _tile(n, (128,))
    tc = DFF // 2

    def body(u_ref, o_ref):
        for j in range(2):
            o_ref[:, j * tc:(j + 1) * tc] = _swiglu_fn(u_ref[:, j * tc:(j + 1) * tc], u_ref[:, DFF + j * tc:DFF + (j + 1) * tc]).astype(BF)

    return pl.pallas_call(
        body, grid=(n // tb,), name="swiglu_fwd",
        in_specs=[pl.BlockSpec((tb, 2 * DFF), lambda i: (i, 0))],
        out_specs=pl.BlockSpec((tb, DFF), lambda i: (i, 0)), out_shape=jax.ShapeDtypeStruct((n, DFF), BF),
    )(uc)


def _swiglu_bwd(uc, da):
    n = uc.shape[0]
    tb = _tile(n, (128,))
    tc = DFF // 2

    def body(u_ref, da_ref, du_ref):
        for j in range(2):
            gs, vs = slice(j * tc, (j + 1) * tc), slice(DFF + j * tc, DFF + (j + 1) * tc)
            _, vjp = jax.vjp(_swiglu_fn, u_ref[:, gs], u_ref[:, vs])
            du_ref[:, gs], du_ref[:, vs] = vjp(da_ref[:, gs])

    full = pl.BlockSpec((tb, 2 * DFF), lambda i: (i, 0))
    return pl.pallas_call(
        body, grid=(n // tb,), name="swiglu_bwd", in_specs=[full, pl.BlockSpec((tb, DFF), lambda i: (i, 0))], out_specs=full,
        out_shape=jax.ShapeDtypeStruct((n, 2 * DFF), F32),
    )(uc, da)


def _loss_kernel(x1, gate, ff, target):
    n = x1.shape[0]
    tb = _tile(n, (256, 128))

    def body(x_ref, g_ref, f_ref, t_ref, loss_ref, dy_ref, dff_ref, dg_ref):
        err = x_ref[...] + g_ref[...] * f_ref[...] - t_ref[...]
        dy = err * (1.0 / D)
        dy_ref[...] = dy
        dff_ref[...] = (g_ref[...] * dy).astype(BF)

        @pl.when(pl.program_id(0) == 0)
        def _():
            loss_ref[...] = jnp.zeros_like(loss_ref)
            dg_ref[...] = jnp.zeros_like(dg_ref)
        part = 0.5 * jnp.sum(jnp.sum(err * err, axis=1, keepdims=True) * (1.0 / D), axis=0, keepdims=True)
        loss_ref[...] += jnp.broadcast_to(part, (1, 128))
        dg_ref[...] += jnp.sum(dy * f_ref[...], axis=0, keepdims=True)

    row = pl.BlockSpec((tb, D), lambda i: (i, 0))
    one = pl.BlockSpec((1, D), lambda i: (0, 0))
    return pl.pallas_call(
        body, grid=(n // tb,), name="loss",
        in_specs=[row, one, row, row], out_specs=[pl.BlockSpec((1, 128), lambda i: (0, 0)), row, row, one],
        out_shape=[jax.ShapeDtypeStruct((1, 128), F32), jax.ShapeDtypeStruct((n, D), F32),
                   jax.ShapeDtypeStruct((n, D), BF), jax.ShapeDtypeStruct((1, D), F32)],
    )(x1, gate, ff, target)


def _rope_tables(nlat, lc):
    t = jnp.arange(nlat)
    row = (t // GRID_W).astype(F32)
    col = (t % GRID_W).astype(F32)
    inv_freq = ROPE_BASE ** (-jnp.arange(32, dtype=F32) / 32)
    ar, ac = row[:, None] * inv_freq, col[:, None] * inv_freq
    cos = jnp.concatenate([jnp.cos(ar), jnp.cos(ar), jnp.cos(ac), jnp.cos(ac)], axis=1)
    sin = jnp.concatenate([-jnp.sin(ar), jnp.sin(ar), -jnp.sin(ac), jnp.sin(ac)], axis=1)
    cos = jnp.concatenate([cos, jnp.ones((lc, HD), F32)], axis=0)
    sin = jnp.concatenate([sin, jnp.zeros((lc, HD), F32)], axis=0)
    return cos, sin


def _pad_rows8(w):
    return jnp.concatenate([w, jnp.zeros((8 - w.shape[0], w.shape[1]), w.dtype)], axis=0)


def _pack_w_in(w):
    cuts = [sum(IN_SIZES[:i]) for i in range(len(IN_SIZES) + 1)]
    qkv, gt, b, a, q, k, v, mg = [w[:, cuts[i]:cuts[i + 1]] for i in range(len(IN_SIZES))]
    return jnp.concatenate([qkv, gt, q, mg, k, v, b, a, jnp.zeros((w.shape[0], PW - O_BA - 32), w.dtype)], axis=1)


def _unpack_w_in(g):
    return jnp.concatenate([g[:, O_QKV:O_GT], g[:, O_GT:O_Q], g[:, O_BA:O_BA + 32], g[:, O_Q:O_MG], g[:, O_K:O_V],
                            g[:, O_V:O_BA], g[:, O_MG:O_K]], axis=1)


def _local_step(x, ctx, mod_x, mod_c, target, w_in_p, w_bdn, w_bat, w_out, w_up, w_down,
                norm_mix, norm_ffn, dn_conv, a_log, dt_bias, dn_norm, q_norm, k_norm, sink, ffn_conv, ffn_conv_b):
    L, LC = x.shape[0], ctx.shape[0]
    T = L + LC
    xc = jnp.concatenate([x, ctx], axis=0)
    seg = lambda r: jnp.stack([mod_x[r], mod_c[r]])[:, None, :]
    sh_a, sc_a = seg(0), seg(1)
    g_a, g_f = mod_x[2][None], mod_x[5][None]
    sh_f, sc_f = mod_x[3][None, None], mod_x[4][None, None]
    sh_f2 = jnp.concatenate([sh_f, sh_f], axis=0)
    sc_f2 = jnp.concatenate([sc_f, sc_f], axis=0)
    cos, sin = _rope_tables(L, LC)
    dnc8 = _pad_rows8(dn_conv)
    ffc8 = _pad_rows8(ffn_conv)
    zeros128 = jnp.zeros((1, 128), F32)
    alog_row = zeros128.at[0, 16:32].set(a_log.reshape(16))
    dt_row = zeros128.at[0, 16:32].set(dt_bias.reshape(16))
    sinkb = jnp.zeros((KVH, 8, 128), F32).at[:, :GRP, :].set(jnp.broadcast_to(sink.reshape(KVH, GRP, 1), (KVH, GRP, 128)))

    h1 = _norm_mod_fwd(xc, norm_mix, sh_a, sc_a, L, "norm_mix_fwd")
    p = _mm(h1, w_in_p, form="nn", out_dtype=F32, name="in_proj")
    conv = _conv_fwd(p, dnc8, jnp.zeros((1, 3 * D), F32), width=5, col0=0, ncols=3 * D, tc=512, seg_rows=(L, LC), name="dn_conv_fwd")
    q, k, v, gb = _dn_post_fwd(conv, p, alog_row, dt_row)
    o_dir, saved = [], []
    for d in (0, 1):
        u, w, qg, kd, qkd, gl = _dn1_fwd(q, k, v, gb, d)
        o, sall = _dn2_fwd(u, w, qg, kd, qkd, gl, d, L)
        o_dir.append(o)
        saved.append((u, w, qg, kd, qkd, gl, sall))
    y_dn = _ghn_fwd(o_dir[0], o_dir[1], p, dn_norm, L)
    qr, kr, vb = _attn_prep_fwd(p, q_norm, k_norm, cos, sin)
    y_at = _attn_fwd(qr, kr, vb, sinkb, L)
    z_dn = _mm(y_dn, w_bdn, form="nn", out_dtype=F32, name="branch_dn")
    z_at = _mm(y_at, w_bat, form="nn", out_dtype=F32, name="branch_at")
    merged = _merge_fwd(z_dn, z_at, p, L)
    mix = _mm(merged, w_out, form="nn", out_dtype=F32, name="out_proj")
    x1 = _resid_fwd(xc, g_a, mix)
    h2 = _norm_mod_fwd(x1, norm_ffn, sh_f2, sc_f2, L, "norm_ffn_fwd")
    u_raw = _mm(h2, w_up, form="nn", out_dtype=F32, name="ffn_up")
    uc = _conv_fwd(u_raw, ffc8, ffn_conv_b, width=3, col0=0, ncols=2 * DFF, tc=DFF // 2, seg_rows=(L,), name="ffn_conv_fwd")
    act = _swiglu_fwd(uc)
    ff = _mm(act, w_down, form="nn", out_dtype=F32, name="ffn_down")
    loss_row, dy, dff, dg_f = _loss_kernel(x1, g_f, ff, target)

    g_down = _mm(act, dff, form="tn", out_dtype=F32, name="g_ffn_down")
    dact = _mm(dff, w_down, form="nt", out_dtype=F32, name="d_act")
    duc = _swiglu_bwd(uc, dact)
    du_raw, g_ffc8, g_ffb = _conv_bwd(u_raw, duc, ffc8, width=3, col0=0, ncols=2 * DFF, tc=DFF // 2, seg_rows=(L,), name="ffn_conv_bwd")
    g_up = _mm(h2, du_raw, form="tn", out_dtype=F32, name="g_ffn_up")
    dh2 = _mm(du_raw, w_up, form="nt", out_dtype=F32, name="d_h2")
    dx1n, g_nffn, dsh_f, dsc_f = _norm_mod_bwd(x1, norm_ffn, sh_f2, sc_f2, dh2, L, "norm_ffn_bwd")
    dx1, dmix, dg_a = _resid_bwd(dy, dx1n, g_a, mix)

    g_out = _mm(merged, dmix, form="tn", out_dtype=F32, name="g_w_out")
    dmerged = _mm(dmix, w_out, form="nt", out_dtype=F32, name="d_merged")
    dz_dn, dz_at, dmg = _merge_bwd(z_dn, z_at, p, dmerged, L)
    g_bdn = _mm(y_dn, dz_dn, form="tn", out_dtype=F32, name="g_branch_dn")
    g_bat = _mm(y_at, dz_at, form="tn", out_dtype=F32, name="g_branch_at")
    dy_dn = _mm(dz_dn, w_bdn, form="nt", out_dtype=F32, name="d_y_dn")
    dy_at = _mm(dz_at, w_bat, form="nt", out_dtype=F32, name="d_y_at")
    dqr, dkp, dvp, dkx, dvx, dsink = _attn_bwd(qr, kr, vb, sinkb, dy_at, L)
    dq_raw, dk_raw, dv_raw, g_qn, g_kn = _attn_prep_bwd(p, q_norm, k_norm, cos, sin, dqr, dkp, dvp, dkx, dvx, L)
    do, dgt, g_dnn = _ghn_bwd(o_dir[0], o_dir[1], p, dn_norm, dy_dn, L)
    dq = dk = dv = dgb = None
    for d in (0, 1):
        u, w, qg, kd, qkd, gl, sall = saved[d]
        du, dw, dqg, dkd, dqkd, dgl = _dn2_bwd(u, w, qg, kd, qkd, gl, sall, do, d, L)
        parts = _dn1_bwd(q, k, v, gb, du, dw, dqg, dkd, dqkd, dgl, d)
        if d == 0:
            dq, dk, dv, dgb = parts
        else:
            dq, dk, dv, dgb = _add4(dq, dk, dv, dgb, *parts)
    dconv, dba, g_alog, g_dt = _dn_post_bwd(conv, p, alog_row, dt_row, dq, dk, dv, dgb)
    dqkv_raw, g_dnc8, _ = _conv_bwd(p, dconv, dnc8, width=5, col0=0, ncols=3 * D, tc=512, seg_rows=(L, LC), name="dn_conv_bwd")
    padc = lambda a: jnp.concatenate([a, jnp.zeros((LC, a.shape[1]), a.dtype)], axis=0)
    dp = jnp.concatenate([dqkv_raw, padc(dgt), dq_raw, padc(dmg), dk_raw, dv_raw, dba, jnp.zeros((T, PW - O_BA - 128), BF)], axis=1)
    g_in = _mm(h1, dp, form="tn", out_dtype=F32, name="g_w_in")
    dh1 = _mm(dp, w_in_p, form="nt", out_dtype=F32, name="d_h1")
    dxc, g_nmix, dsh_a, dsc_a = _norm_mod_bwd(xc, norm_mix, sh_a, sc_a, dh1, L, "norm_mix_bwd")
    grad_x = _add2(dx1, dxc)

    zero = jnp.zeros((D,), F32)
    dmod_x = jnp.stack([dsh_a[0, 0], dsc_a[0, 0], dg_a[0], dsh_f[0, 0], dsc_f[0, 0], dg_f[0]])
    dmod_c = jnp.stack([dsh_a[1, 0], dsc_a[1, 0], zero, zero, zero, zero])
    small = dict(
        dmod_x=dmod_x, dmod_c=dmod_c, norm_mix=g_nmix, norm_ffn=g_nffn, dn_conv=g_dnc8[:5], dn_a_log=g_alog[0, 16:32].reshape(2, 8),
        dn_dt_bias=g_dt[0, 16:32].reshape(2, 8), dn_norm=g_dnn, q_norm=g_qn, k_norm=g_kn,
        attn_sink=jnp.sum(dsink[:, :GRP, :], axis=2).reshape(1, NH), ffn_conv=g_ffc8[:3], ffn_conv_b=g_ffb)
    return loss_row[0, 0], grad_x, (g_in, g_bdn, g_bat, g_out, g_up, g_down), small


def _elementwise(fn, args, out_dtypes, name, rows=None):
    n = rows or args[0].shape[0]
    ncol = args[0].shape[1]
    tb = _tile(n, (256, 128, 8))
    nout = len(out_dtypes)

    def body(*refs):
        outs = fn(*[r[...] for r in refs[:len(args)]])
        for o_ref, o in zip(refs[len(args):], outs):
            o_ref[...] = o.astype(o_ref.dtype)

    spec = pl.BlockSpec((tb, ncol), lambda i: (i, 0))
    return pl.pallas_call(
        body, grid=(n // tb,), name=name, in_specs=[spec] * len(args), out_specs=[spec] * nout,
        out_shape=[jax.ShapeDtypeStruct((n, ncol), dt) for dt in out_dtypes],
    )(*args)


def _add2(a, b):
    return _elementwise(lambda x, y: (x + y,), [a, b], [F32], "add2", rows=a.shape[0])[0]


def _add4(a0, a1, a2, a3, b0, b1, b2, b3):
    s = _elementwise(lambda x0, x1, x2, y0, y1, y2: (x0 + y0, x1 + y1, x2 + y2), [a0, a1, a2, b0, b1, b2], [F32] * 3, "add_dqkv")
    g = _elementwise(lambda x, y: (x + y,), [a3, b3], [F32], "add_dgb")
    return s[0], s[1], s[2], g[0]


def _exchange(arrays, scatter, name):
    n = len(arrays)

    def body(*refs):
        ins, outs = refs[:n], refs[n:2 * n]
        send_sems, recv_sems, local_sems = refs[2 * n:]
        x, y, c = lax.axis_index("x"), lax.axis_index("y"), lax.axis_index("c")
        me = 4 * x + 2 * y + c
        started = []
        for k in range(n):
            local = pltpu.make_async_copy(ins[k].at[me] if scatter else ins[k], outs[k].at[me], local_sems.at[k])
            local.start()
            started.append(local)
        pending = []
        for k in range(n):
            for m in range(1, N_DEV):
                px = 1 - x if m & 4 else x
                py = 1 - y if m & 2 else y
                pc = 1 - c if m & 1 else c
                peer = 4 * px + 2 * py + pc
                src = ins[k].at[peer] if scatter else ins[k]
                sem = k * (N_DEV - 1) + m - 1
                push = pltpu.make_async_remote_copy(src_ref=src, dst_ref=outs[k].at[me], send_sem=send_sems.at[sem],
                                                    recv_sem=recv_sems.at[sem], device_id=(px, py, pc), device_id_type=MESH)
                push.start()
                landed = pltpu.make_async_remote_copy(src_ref=src, dst_ref=outs[k].at[peer], send_sem=send_sems.at[sem],
                                                      recv_sem=recv_sems.at[sem], device_id=(px, py, pc), device_id_type=MESH)
                pending.append((push, landed))
        for push, landed in pending:
            landed.wait_recv()
        for push, landed in pending:
            push.wait_send()
        for local in started:
            local.wait()

    hbm = pl.BlockSpec(memory_space=pl.ANY)
    out_shape = [jax.ShapeDtypeStruct(a.shape if scatter else (N_DEV,) + a.shape, a.dtype) for a in arrays]
    return pl.pallas_call(
        body, name=name, in_specs=[hbm] * n, out_specs=[hbm] * n, out_shape=out_shape,
        scratch_shapes=[pltpu.SemaphoreType.DMA((n * (N_DEV - 1),)), pltpu.SemaphoreType.DMA((n * (N_DEV - 1),)),
                        pltpu.SemaphoreType.DMA((n,))],
    )(*arrays)


def _ada_fwd(c16, w_ada, b_ada):
    def body(c_ref, w_ref, b_ref, o_ref):
        o_ref[...] = _dot_hi(jax.nn.silu(c_ref[...]), w_ref[...]) + b_ref[...]

    return pl.pallas_call(body, name="ada_fwd", out_shape=jax.ShapeDtypeStruct((16, w_ada.shape[1]), F32))(c16, w_ada, b_ada)


def _ada_bwd(c16, w_ada, dmx, dmc):
    def body(c_ref, w_ref, dmx_ref, dmc_ref, gw_ref, pc_ref):
        dmc_tot = dmc_ref[0:1, :]
        for d in range(1, N_DEV):
            dmc_tot = dmc_tot + dmc_ref[d:d + 1, :]
        dm16 = jnp.concatenate([dmx_ref[...], jnp.broadcast_to(dmc_tot, (8, dmc_tot.shape[1]))], axis=0)
        row = lax.broadcasted_iota(jnp.int32, dm16.shape, 0)
        dm16 = jnp.where(row <= 8, dm16, 0.0)
        s = jax.nn.silu(c_ref[...])
        gw_ref[...] = lax.dot_general(s, dm16, (_DIMS["tn"], ((), ())), precision=HI, preferred_element_type=F32)
        pc = lax.dot_general(dm16, w_ref[...], (_DIMS["nt"], ((), ())), precision=HI, preferred_element_type=F32)
        pc_ref[...] = pc[8:9, :]

    return pl.pallas_call(body, name="ada_bwd", out_shape=[jax.ShapeDtypeStruct(w_ada.shape, F32), jax.ShapeDtypeStruct((1, D), F32)],
                          compiler_params=_cp())(c16, w_ada, dmx, dmc)


def _cctx_grad(pc_all, c_ctx_row):
    def body(pc_ref, c_ref, g_ref):
        tot = pc_ref[0]
        for d in range(1, N_DEV):
            tot = tot + pc_ref[d]
        _, vjp = jax.vjp(jax.nn.silu, c_ref[...])
        g_ref[...] = vjp(tot)[0]

    return pl.pallas_call(body, name="cctx_grad", out_shape=jax.ShapeDtypeStruct((1, D), F32))(pc_all, c_ctx_row)


def _adamw(parts, w, m, v, name):
    ns, R, C = parts.shape
    tb = _tile(R, (128, 64, 32, 16, 8))

    def body(p_ref, w_ref, m_ref, v_ref, g_ref, d_ref, mo_ref, vo_ref):
        g = p_ref[0]
        for s in range(1, ns):
            g = g + p_ref[s]
        m2 = ADAM_B1 * m_ref[...] + (1.0 - ADAM_B1) * g
        v2 = ADAM_B2 * v_ref[...] + (1.0 - ADAM_B2) * jnp.square(g)
        m_hat = m2 / (1.0 - ADAM_B1 ** ADAM_STEP)
        v_hat = v2 / (1.0 - ADAM_B2 ** ADAM_STEP)
        g_ref[...] = g
        d_ref[...] = -ADAM_LR * (m_hat / (jnp.sqrt(v_hat) + ADAM_EPS) + ADAM_WD * w_ref[...])
        mo_ref[...] = m2
        vo_ref[...] = v2

    row = pl.BlockSpec((tb, C), lambda i: (i, 0))
    return pl.pallas_call(
        body, grid=(R // tb,), name=name,
        in_specs=[pl.BlockSpec((ns, tb, C), lambda i: (0, i, 0)), row, row, row], out_specs=[row] * 4,
        out_shape=[jax.ShapeDtypeStruct((R, C), F32)] * 4, compiler_params=_cp(),
    )(parts, w, m, v)


_SMALL = (("dmod_x", 6 * D), ("dmod_c", 6 * D), ("b_ada", 6 * D), ("norm_mix", D), ("norm_ffn", D), ("dn_a_log", 16),
          ("dn_dt_bias", 16), ("dn_norm", HD), ("q_norm", HD), ("k_norm", HD), ("attn_sink", NH), ("ffn_conv_b", 2 * DFF),
          ("dn_conv", 5 * 3 * D), ("ffn_conv", 3 * 2 * DFF))
_SMALL_ROWS = -(-sum(n for _, n in _SMALL) // 1024) * 8


def _pack_small(d):
    flat = jnp.concatenate([d[k].reshape(-1).astype(F32) if k in d else jnp.zeros((n,), F32) for k, n in _SMALL])
    return jnp.concatenate([flat, jnp.zeros((_SMALL_ROWS * 128 - flat.shape[0],), F32)]).reshape(_SMALL_ROWS, 128)


def _unpack_small(a):
    flat = a.reshape(a.shape[:-2] + (-1,))
    out, off = {}, 0
    for k, n in _SMALL:
        out[k] = flat[..., off:off + n]
        off += n
    return out


def kernel(x, c, ctx, c_ctx, w_ada, b_ada, norm_mix, norm_ffn, w_in, dn_conv, dn_a_log, dn_dt_bias, dn_norm, q_norm, k_norm, attn_sink, w_branch_dn, w_branch_attn, w_out, ffn_up, ffn_conv, ffn_conv_b, ffn_down, loss_target, m_c_ctx, m_w_ada, m_b_ada, m_norm_mix, m_norm_ffn, m_w_in, m_dn_conv, m_dn_a_log, m_dn_dt_bias, m_dn_norm, m_q_norm, m_k_norm, m_attn_sink, m_w_branch_dn, m_w_branch_attn, m_w_out, m_ffn_up, m_ffn_conv, m_ffn_conv_b, m_ffn_down, v_c_ctx, v_w_ada, v_b_ada, v_norm_mix, v_norm_ffn, v_w_in, v_dn_conv, v_dn_a_log, v_dn_dt_bias, v_dn_norm, v_q_norm, v_k_norm, v_attn_sink, v_w_branch_dn, v_w_branch_attn, v_w_out, v_ffn_up, v_ffn_conv, v_ffn_conv_b, v_ffn_down):
    me = 4 * lax.axis_index("x") + 2 * lax.axis_index("y") + lax.axis_index("c")
    ada_cols = w_ada.shape[2]

    gathered = _exchange([w_in[0].astype(BF), w_branch_dn[0].astype(BF), w_branch_attn[0].astype(BF), w_out[0].astype(BF),
                          ffn_up[0].astype(BF), ffn_down[0].astype(BF), c, dn_conv[0], ffn_conv[0]],
                         scatter=False, name="gather_weights")
    cols = lambda a: jnp.swapaxes(a, 0, 1).reshape(a.shape[1], -1)
    rows = lambda a: a.reshape(-1, a.shape[2])
    w_in_p = _pack_w_in(cols(gathered[0]))
    w_bdn, w_bat, w_o = rows(gathered[1]), rows(gathered[2]), rows(gathered[3])
    w_up, w_down = cols(gathered[4]), rows(gathered[5])
    c_all = gathered[6][:, 0, :]

    c16 = jnp.concatenate([c_all, c_ctx[None], jnp.zeros((7, D), F32)], axis=0)
    b_loc = lax.dynamic_slice_in_dim(b_ada, me * ada_cols, ada_cols, axis=1)
    mod_part = _ada_fwd(c16, w_ada[0], b_loc)
    mod_all = cols(_exchange([mod_part], scatter=False, name="gather_mod")[0])
    mod_x = lax.dynamic_slice_in_dim(mod_all, me, 1, axis=0).reshape(6, D)
    mod_c = mod_all[8].reshape(6, D)

    loss_loc, grad_x, big, small = _local_step(
        x[0], ctx[0], mod_x, mod_c, loss_target[0], w_in_p, w_bdn, w_bat, w_o, w_up, w_down,
        norm_mix, norm_ffn, cols(gathered[7]), dn_a_log[0], dn_dt_bias[0], dn_norm, q_norm, k_norm, attn_sink[0], cols(gathered[8]),
        ffn_conv_b)
    loss = lax.psum(loss_loc, ("x", "y", "c"))

    g_in, g_bdn, g_bat, g_out, g_up, g_down = big
    col_blocks = lambda g: jnp.swapaxes(g.reshape(g.shape[0], N_DEV, -1), 0, 1)
    row_blocks = lambda g: g.reshape(N_DEV, -1, g.shape[1])
    landed = _exchange([col_blocks(_unpack_w_in(g_in)), row_blocks(g_bdn), row_blocks(g_bat), row_blocks(g_out),
                        col_blocks(g_up), row_blocks(g_down)], scatter=True, name="scatter_grads")
    res = {}
    res["w_in"] = _adamw(landed[0], w_in[0], m_w_in[0], v_w_in[0], "adamw_w_in")
    res["w_branch_dn"] = _adamw(landed[1], w_branch_dn[0], m_w_branch_dn[0], v_w_branch_dn[0], "adamw_w_branch_dn")
    res["w_branch_attn"] = _adamw(landed[2], w_branch_attn[0], m_w_branch_attn[0], v_w_branch_attn[0], "adamw_w_branch_attn")
    res["w_out"] = _adamw(landed[3], w_out[0], m_w_out[0], v_w_out[0], "adamw_w_out")
    res["ffn_up"] = _adamw(landed[4], ffn_up[0], m_ffn_up[0], v_ffn_up[0], "adamw_ffn_up")
    res["ffn_down"] = _adamw(landed[5], ffn_down[0], m_ffn_down[0], v_ffn_down[0], "adamw_ffn_down")

    small = dict(small)
    small["b_ada"] = small["dmod_x"] + small["dmod_c"]
    parts = _exchange([_pack_small(small)], scatter=False, name="gather_small")[0]
    per_dev = _unpack_small(parts)
    given = dict(b_ada=(b_ada, m_b_ada, v_b_ada), norm_mix=(norm_mix, m_norm_mix, v_norm_mix), norm_ffn=(norm_ffn, m_norm_ffn, v_norm_ffn),
                 dn_a_log=(dn_a_log, m_dn_a_log, v_dn_a_log), dn_dt_bias=(dn_dt_bias, m_dn_dt_bias, v_dn_dt_bias),
                 dn_norm=(dn_norm, m_dn_norm, v_dn_norm), q_norm=(q_norm, m_q_norm, v_q_norm), k_norm=(k_norm, m_k_norm, v_k_norm),
                 attn_sink=(attn_sink, m_attn_sink, v_attn_sink), ffn_conv_b=(ffn_conv_b, m_ffn_conv_b, v_ffn_conv_b))
    packs = [_pack_small({k: t[j] for k, t in given.items()}) for j in range(3)]
    upd = [_unpack_small(a) for a in _adamw(parts, packs[0], packs[1], packs[2], "adamw_small")]
    for k, t in given.items():
        res[k] = tuple(u[k].reshape(t[0].shape) for u in upd)
    dnc = lax.dynamic_slice_in_dim(upd[0]["dn_conv"].reshape(5, 3 * D), me * dn_conv.shape[2], dn_conv.shape[2], axis=1)
    ffc = lax.dynamic_slice_in_dim(upd[0]["ffn_conv"].reshape(3, 2 * DFF), me * ffn_conv.shape[2], ffn_conv.shape[2], axis=1)
    r8 = lambda a: _pad_rows8(a)
    t = _adamw(r8(dnc)[None], r8(dn_conv[0]), r8(m_dn_conv[0]), r8(v_dn_conv[0]), "adamw_dn_conv")
    res["dn_conv"] = tuple(a[:5][None] for a in t)
    t = _adamw(r8(ffc)[None], r8(ffn_conv[0]), r8(m_ffn_conv[0]), r8(v_ffn_conv[0]), "adamw_ffn_conv")
    res["ffn_conv"] = tuple(a[:3][None] for a in t)

    dmx = lax.dynamic_slice_in_dim(per_dev["dmod_x"], me * ada_cols, ada_cols, axis=1)
    dmc = lax.dynamic_slice_in_dim(per_dev["dmod_c"], me * ada_cols, ada_cols, axis=1)
    g_ada, pc = _ada_bwd(c16, w_ada[0], dmx, dmc)
    res["w_ada"] = _adamw(g_ada[None], w_ada[0], m_w_ada[0], v_w_ada[0], "adamw_w_ada")
    pc_all = _exchange([pc], scatter=False, name="gather_cctx")[0]
    g_cctx = _cctx_grad(pc_all, c_ctx[None])
    r8b = lambda a: jnp.broadcast_to(a, (8, D))
    t = _adamw(r8b(g_cctx)[None], r8b(c_ctx[None]), r8b(m_c_ctx[None]), r8b(v_c_ctx[None]), "adamw_c_ctx")
    res["c_ctx"] = tuple(a[0] for a in t)

    names = ("c_ctx", "w_ada", "b_ada", "norm_mix", "norm_ffn", "w_in", "dn_conv", "dn_a_log", "dn_dt_bias", "dn_norm", "q_norm",
             "k_norm", "attn_sink", "w_branch_dn", "w_branch_attn", "w_out", "ffn_up", "ffn_conv", "ffn_conv_b", "ffn_down")
    lead = ("w_ada", "w_in", "w_branch_dn", "w_branch_attn", "w_out", "ffn_up", "ffn_down")
    fix = lambda k, a: a[None] if k in lead else a
    outs = [loss, grad_x[None]]
    for j in range(4):
        outs += [fix(k, res[k][j]) for k in names]
    return tuple(outs)
```

```python
import functools

import jax
import jax.numpy as jnp
from jax import lax
from jax.experimental import pallas as pl
from jax.experimental.pallas import tpu as pltpu

F32 = jnp.float32
BF = jnp.bfloat16
HI = lax.Precision.HIGHEST
MESH = pl.DeviceIdType.MESH

D = 1024
NH = 8
HD = 128
KVH = 2
GRP = 4
KV = KVH * HD
DFF = 2816
CB = 128
GRID_W = 64
ROPE_BASE = 10000.0
EPS = 1e-6
N_DEV = 8
PW = 8192
O_QKV, O_GT, O_Q, O_MG, O_K, O_V, O_BA = 0, 3072, 4096, 5120, 7168, 7424, 7680
IN_SIZES = (3072, 1024, 16, 16, 1024, 256, 256, 2048)
IN_DIM = sum(IN_SIZES)
ADAM_LR, ADAM_B1, ADAM_B2, ADAM_EPS, ADAM_WD, ADAM_STEP = 0.001, 0.9, 0.999, 1e-08, 0.01, 10
VMEM_LIMIT = 56 * 1024 * 1024


def _cp():
    return pltpu.CompilerParams(vmem_limit_bytes=VMEM_LIMIT)


def _tile(n, cands):
    for c in cands:
        if n % c == 0:
            return c
    return n


def _iota2(shape):
    return lax.broadcasted_iota(jnp.int32, shape, 0), lax.broadcasted_iota(jnp.int32, shape, 1)


_DIMS = {"nn": ((1,), (0,)), "nt": ((1,), (1,)), "tn": ((0,), (0,))}


def _mm(a, b, *, form, out_dtype, name, tm=None, tn=None, tk=None):
    if form == "tn":
        K, M = a.shape
        N = b.shape[1]
    else:
        M, K = a.shape
        N = b.shape[0] if form == "nt" else b.shape[1]
    tm = tm or _tile(M, (1024, 640, 512, 256, 128))
    tn = tn or _tile(N, (1408, 1024, 512, 256, 128))
    tk = tk or _tile(K, (2048, 1408, 1024, 640, 512, 256, 128))
    nk = K // tk
    dims = (_DIMS[form], ((), ()))

    def body(a_ref, b_ref, o_ref, *acc):
        k = pl.program_id(2)
        part = lax.dot_general(a_ref[...].astype(BF), b_ref[...].astype(BF), dims, preferred_element_type=F32)
        if nk == 1:
            o_ref[...] = part.astype(out_dtype)
        else:
            acc_ref = acc[0]

            @pl.when(k == 0)
            def _():
                acc_ref[...] = part

            @pl.when(k > 0)
            def _():
                acc_ref[...] += part

            @pl.when(k == nk - 1)
            def _():
                o_ref[...] = acc_ref[...].astype(out_dtype)

    if form == "tn":
        a_spec = pl.BlockSpec((tk, tm), lambda i, j, k: (k, i))
    else:
        a_spec = pl.BlockSpec((tm, tk), lambda i, j, k: (i, k))
    if form == "nt":
        b_spec = pl.BlockSpec((tn, tk), lambda i, j, k: (j, k))
    else:
        b_spec = pl.BlockSpec((tk, tn), lambda i, j, k: (k, j))
    return pl.pallas_call(
        body, grid=(M // tm, N // tn, nk), name=name,
        in_specs=[a_spec, b_spec], out_specs=pl.BlockSpec((tm, tn), lambda i, j, k: (i, j)),
        out_shape=jax.ShapeDtypeStruct((M, N), out_dtype),
        scratch_shapes=[] if nk == 1 else [pltpu.VMEM((tm, tn), F32)],
        compiler_params=_cp(),
    )(a, b)


def _norm_mod_fn(x, nw, sh, sc):
    y = x * lax.rsqrt(jnp.mean(x * x, axis=-1, keepdims=True) + EPS)
    return (y * nw) * (1.0 + sc) + sh


def _norm_mod_fwd(x, nw, sh, sc, nlat, name):
    T = x.shape[0]
    tb = _tile(T, (256, 128))
    nlb = nlat // tb

    def body(x_ref, nw_ref, sh_ref, sc_ref, h_ref):
        h_ref[...] = _norm_mod_fn(x_ref[...], nw_ref[...], sh_ref[0], sc_ref[0]).astype(BF)

    seg = pl.BlockSpec((1, 1, D), lambda i: (jnp.where(i >= nlb, 1, 0), 0, 0))
    return pl.pallas_call(
        body, grid=(T // tb,), name=name,
        in_specs=[pl.BlockSpec((tb, D), lambda i: (i, 0)), pl.BlockSpec((1, D), lambda i: (0, 0)), seg, seg],
        out_specs=pl.BlockSpec((tb, D), lambda i: (i, 0)),
        out_shape=jax.ShapeDtypeStruct((T, D), BF),
    )(x, nw, sh, sc)


def _norm_mod_bwd(x, nw, sh, sc, dh, nlat, name):
    T = x.shape[0]
    tb = _tile(T, (256, 128))
    nlb = nlat // tb

    def body(x_ref, nw_ref, sh_ref, sc_ref, dh_ref, dx_ref, dnw_ref, dsh_ref, dsc_ref):
        i = pl.program_id(0)
        _, vjp = jax.vjp(_norm_mod_fn, x_ref[...], nw_ref[...], sh_ref[0], sc_ref[0])
        dx, dnw, dsh, dsc = vjp(dh_ref[...])
        dx_ref[...] = dx

        @pl.when(i == 0)
        def _():
            dnw_ref[...] = jnp.zeros_like(dnw_ref)

        @pl.when((i == 0) | (i == nlb))
        def _():
            dsh_ref[...] = jnp.zeros_like(dsh_ref)
            dsc_ref[...] = jnp.zeros_like(dsc_ref)

        dnw_ref[...] += dnw
        dsh_ref[0] += dsh
        dsc_ref[0] += dsc

    seg = pl.BlockSpec((1, 1, D), lambda i: (jnp.where(i >= nlb, 1, 0), 0, 0))
    row = pl.BlockSpec((tb, D), lambda i: (i, 0))
    one = pl.BlockSpec((1, D), lambda i: (0, 0))
    return pl.pallas_call(
        body, grid=(T // tb,), name=name,
        in_specs=[row, one, seg, seg, row], out_specs=[row, one, seg, seg],
        out_shape=[jax.ShapeDtypeStruct((T, D), F32), jax.ShapeDtypeStruct((1, D), F32),
                   jax.ShapeDtypeStruct((2, 1, D), F32), jax.ShapeDtypeStruct((2, 1, D), F32)],
    )(x, nw, sh, sc, dh)


def _halo_specs(tb, tc, nrows, col0):
    r8 = tb // 8
    cur = pl.BlockSpec((tb, tc), lambda j, i: (i, col0 + j))
    prev = pl.BlockSpec((8, tc), lambda j, i: (jnp.maximum(i * r8 - 1, 0), col0 + j))
    nxt = pl.BlockSpec((8, tc), lambda j, i: (jnp.minimum((i + 1) * r8, nrows // 8 - 1), col0 + j))
    return cur, prev, nxt


def _extend(prev_ref, cur_ref, next_ref, i, starts, ends):
    keep_p = functools.reduce(lambda a, b: a & b, [i != s for s in starts])
    keep_n = functools.reduce(lambda a, b: a & b, [i != e for e in ends])
    p = jnp.where(keep_p, prev_ref[...].astype(F32), 0.0)
    n = jnp.where(keep_n, next_ref[...].astype(F32), 0.0)
    return jnp.concatenate([p, cur_ref[...].astype(F32), n], axis=0)


def _shifted(xe, shift, tb):
    n = tb + 16
    s = shift % n
    xs = xe if s == 0 else pltpu.roll(xe, s, 0)
    return xs[8:8 + tb]


def _conv_fwd(x, w8, bias, *, width, col0, ncols, tc, seg_rows, name):
    T = x.shape[0]
    tb = _tile(T, (256, 128))
    r = width // 2
    bounds = [0]
    for s in seg_rows:
        bounds.append(bounds[-1] + s // tb)
    starts, ends = bounds[:-1], [b - 1 for b in bounds[1:]]

    def body(cur_ref, prev_ref, next_ref, w_ref, b_ref, o_ref):
        i = pl.program_id(1)
        xe = _extend(prev_ref, cur_ref, next_ref, i, starts, ends)
        acc = _shifted(xe, r, tb) * w_ref[0:1, :]
        for j in range(1, width):
            acc = acc + _shifted(xe, r - j, tb) * w_ref[j:j + 1, :]
        o_ref[...] = acc + b_ref[...]

    cur, prev, nxt = _halo_specs(tb, tc, T, col0)
    return pl.pallas_call(
        body, grid=(ncols // tc, T // tb), name=name,
        in_specs=[cur, prev, nxt, pl.BlockSpec((8, tc), lambda j, i: (0, j)), pl.BlockSpec((1, tc), lambda j, i: (0, j))],
        out_specs=pl.BlockSpec((tb, tc), lambda j, i: (i, j)),
        out_shape=jax.ShapeDtypeStruct((T, ncols), F32),
    )(x, x, x, w8, bias)


def _conv_bwd(x, dc, w8, *, width, col0, ncols, tc, seg_rows, name):
    T = x.shape[0]
    tb = _tile(T, (256, 128))
    r = width // 2
    bounds = [0]
    for s in seg_rows:
        bounds.append(bounds[-1] + s // tb)
    starts, ends = bounds[:-1], [b - 1 for b in bounds[1:]]

    def body(cur_ref, prev_ref, next_ref, dcur_ref, dprev_ref, dnext_ref, w_ref, dx_ref, dw_ref, db_ref):
        i = pl.program_id(1)
        xe = _extend(prev_ref, cur_ref, next_ref, i, starts, ends)
        de = _extend(dprev_ref, dcur_ref, dnext_ref, i, starts, ends)
        dcur = dcur_ref[...]

        @pl.when(i == 0)
        def _():
            dw_ref[...] = jnp.zeros_like(dw_ref)
            db_ref[...] = jnp.zeros_like(db_ref)

        acc = _shifted(de, -r, tb) * w_ref[0:1, :]
        for j in range(1, width):
            acc = acc + _shifted(de, j - r, tb) * w_ref[j:j + 1, :]
        dx_ref[...] = acc.astype(BF)
        for j in range(width):
            dw_ref[j:j + 1, :] += jnp.sum(dcur * _shifted(xe, r - j, tb), axis=0, keepdims=True)
        db_ref[...] += jnp.sum(dcur, axis=0, keepdims=True)

    cur, prev, nxt = _halo_specs(tb, tc, T, col0)
    dcur, dprev, dnxt = _halo_specs(tb, tc, T, 0)
    wspec = pl.BlockSpec((8, tc), lambda j, i: (0, j))
    return pl.pallas_call(
        body, grid=(ncols // tc, T // tb), name=name,
        in_specs=[cur, prev, nxt, dcur, dprev, dnxt, wspec],
        out_specs=[pl.BlockSpec((tb, tc), lambda j, i: (i, j)), wspec, pl.BlockSpec((1, tc), lambda j, i: (0, j))],
        out_shape=[jax.ShapeDtypeStruct((T, ncols), BF), jax.ShapeDtypeStruct((8, ncols), F32),
                   jax.ShapeDtypeStruct((1, ncols), F32)],
    )(x, x, x, dc, dc, dc, w8)


def _softplus(x):
    return jnp.maximum(x, 0.0) + jnp.log(1.0 + jnp.exp(-jnp.abs(x)))


def _gates_fn(ba, alog_row, dt_row):
    col = lax.broadcasted_iota(jnp.int32, ba.shape, 1)
    beta = jax.nn.sigmoid(ba)
    g = -jnp.exp(alog_row) * _softplus(ba + dt_row)
    return jnp.where(col < 16, beta, jnp.where(col < 32, g, 0.0))


def _qkv_post_fn(c, kind):
    y = jax.nn.silu(c)
    if kind == 2:
        return y
    n = y * lax.rsqrt(jnp.sum(y * y, axis=-1, keepdims=True) + EPS)
    return n * (HD ** -0.5) if kind == 0 else n


def _dn_post_fwd(conv, p, alog_row, dt_row):
    T = conv.shape[0]
    tb = _tile(T, (256, 128))

    def body(c_ref, ba_ref, al_ref, dt_ref, q_ref, k_ref, v_ref, gb_ref):
        outs = (q_ref, k_ref, v_ref)
        for kind in range(3):
            for h in range(NH):
                src = slice(kind * D + h * HD, kind * D + (h + 1) * HD)
                outs[kind][:, h * HD:(h + 1) * HD] = _qkv_post_fn(c_ref[:, src], kind)
        gb_ref[...] = _gates_fn(ba_ref[...], al_ref[...], dt_ref[...])

    row = pl.BlockSpec((tb, D), lambda i: (i, 0))
    one = pl.BlockSpec((1, 128), lambda i: (0, 0))
    return pl.pallas_call(
        body, grid=(T // tb,), name="dn_post_fwd",
        in_specs=[pl.BlockSpec((tb, 3 * D), lambda i: (i, 0)), pl.BlockSpec((tb, 128), lambda i: (i, O_BA // 128)), one, one],
        out_specs=[row, row, row, pl.BlockSpec((tb, 128), lambda i: (i, 0))],
        out_shape=[jax.ShapeDtypeStruct((T, D), F32)] * 3 + [jax.ShapeDtypeStruct((T, 128), F32)],
    )(conv, p, alog_row, dt_row)


def _dn_post_bwd(conv, p, alog_row, dt_row, dq, dk, dv, dgb):
    T = conv.shape[0]
    tb = _tile(T, (256, 128))

    def body(c_ref, ba_ref, al_ref, dt_ref, dq_ref, dk_ref, dv_ref, dgb_ref, dc_ref, dba_ref, dal_ref, ddt_ref):
        i = pl.program_id(0)
        douts = (dq_ref, dk_ref, dv_ref)
        for kind in range(3):
            for h in range(NH):
                src = slice(kind * D + h * HD, kind * D + (h + 1) * HD)
                _, vjp = jax.vjp(functools.partial(_qkv_post_fn, kind=kind), c_ref[:, src])
                dc_ref[:, src] = vjp(douts[kind][:, h * HD:(h + 1) * HD])[0]
        _, vjp = jax.vjp(_gates_fn, ba_ref[...], al_ref[...], dt_ref[...])
        dba, dal, ddt = vjp(dgb_ref[...])
        dba_ref[...] = dba.astype(BF)

        @pl.when(i == 0)
        def _():
            dal_ref[...] = jnp.zeros_like(dal_ref)
            ddt_ref[...] = jnp.zeros_like(ddt_ref)
        dal_ref[...] += dal
        ddt_ref[...] += ddt

    row = pl.BlockSpec((tb, D), lambda i: (i, 0))
    one = pl.BlockSpec((1, 128), lambda i: (0, 0))
    nar = pl.BlockSpec((tb, 128), lambda i: (i, 0))
    return pl.pallas_call(
        body, grid=(T // tb,), name="dn_post_bwd",
        in_specs=[pl.BlockSpec((tb, 3 * D), lambda i: (i, 0)), pl.BlockSpec((tb, 128), lambda i: (i, O_BA // 128)), one, one,
                  row, row, row, nar],
        out_specs=[pl.BlockSpec((tb, 3 * D), lambda i: (i, 0)), nar, one, one],
        out_shape=[jax.ShapeDtypeStruct((T, 3 * D), F32), jax.ShapeDtypeStruct((T, 128), BF),
                   jax.ShapeDtypeStruct((1, 128), F32), jax.ShapeDtypeStruct((1, 128), F32)],
    )(conv, p, alog_row, dt_row, dq, dk, dv, dgb)


def _dot_hi(a, b):
    return jnp.dot(a, b, precision=HI, preferred_element_type=F32)


def _dot_bf(a, b):
    return jnp.dot(a.astype(BF), b.astype(BF), preferred_element_type=F32)


def _dot_nt_bf(a, b):
    return lax.dot_general(a.astype(BF), b.astype(BF), (_DIMS["nt"], ((), ())), preferred_element_type=F32)


def _dot_tn_bf(a, b):
    return lax.dot_general(a.astype(BF), b.astype(BF), (_DIMS["tn"], ((), ())), preferred_element_type=F32)


def _dot_h3(a, b):
    return jnp.dot(a, b, precision=lax.Precision.HIGH, preferred_element_type=F32)


def _unit_tri_inverses(mats):
    r, c = _iota2((CB, CB))
    eye = (r == c).astype(F32)
    a8 = [jnp.where((r // 8) == (c // 8), a, 0.0) for a in mats]
    a2 = [_dot_h3(x, x) for x in a8]
    a4 = [_dot_h3(x, x) for x in a2]
    t = [_dot_h3(eye - x, eye + y) for x, y in zip(a8, a2)]
    t = [_dot_h3(x, eye + y) for x, y in zip(t, a4)]
    b = 8
    while b < CB:
        mask = ((r // (2 * b)) == (c // (2 * b))) & ((r // b) != (c // b))
        te = [_dot_h3(x, jnp.where(mask, a, 0.0)) for x, a in zip(t, mats)]
        t = [x - _dot_h3(y, x) for x, y in zip(t, te)]
        b *= 2
    return t


@jax.custom_vjp
def _saved_inverse(a, t):
    return t


_saved_inverse.defvjp(lambda a, t: (t, t), lambda t, dt: (-_dot_h3(_dot_h3(t.T, dt), t.T), jnp.zeros_like(t)))


def _dn1_decay(gc, reverse):
    r, c = _iota2((CB, CB))
    incl = (c >= r) if reverse else (c <= r)
    return jnp.where(incl, jnp.exp(jnp.where(incl, gc - gc.T, 0.0)), 0.0)


def _dn1_system(k, beta, decay, reverse):
    r, c = _iota2((CB, CB))
    return jnp.where((c > r) if reverse else (c < r), beta * _dot_nt_bf(k, k) * decay, 0.0)


def _dn1_outputs(q, k, v, beta, gc, decay, t, reverse):
    r, _ = _iota2((CB, CB))
    eg = jnp.exp(gc)
    u = _dot_h3(t, v * beta)
    w = _dot_h3(t, k * (beta * eg))
    qkd = _dot_nt_bf(q, k) * decay
    glog = jnp.sum(jnp.where(r == (0 if reverse else CB - 1), gc, 0.0), axis=0, keepdims=True)
    return u, w, q * eg, k * jnp.exp(glog - gc), qkd, jnp.exp(glog)


def _dn1_head(q, k, v, beta, gc, t_saved, reverse):
    decay = _dn1_decay(gc, reverse)
    t = _saved_inverse(_dn1_system(k, beta, decay, reverse), t_saved)
    return _dn1_outputs(q, k, v, beta, gc, decay, t, reverse)


def _cum_matrix(upper):
    r, c = _iota2((CB, CB))
    return ((c >= r) if upper else (c <= r)).astype(F32)


def _lane_bcast(x, col):
    return jnp.broadcast_to(x[:, col:col + 1], x.shape)


def _dn1_fwd(q, k, v, gb, direction):
    T = q.shape[0]
    nb = T // CB
    reverse = direction == 1

    def body(q_ref, k_ref, v_ref, gb_ref, u_ref, w_ref, qg_ref, kd_ref, qkd_ref, gl_ref, t_ref):
        gbv = gb_ref[...]
        gcum = _dot_h3(_cum_matrix(reverse), gbv)
        sls = [slice(h * HD, (h + 1) * HD) for h in range(NH)]
        betas = [_lane_bcast(gbv, direction * NH + h) for h in range(NH)]
        gcs = [_lane_bcast(gcum, 16 + direction * NH + h) for h in range(NH)]
        decays = [_dn1_decay(gc, reverse) for gc in gcs]
        systems = [_dn1_system(k_ref[:, sl], beta, decay, reverse) for sl, beta, decay in zip(sls, betas, decays)]
        for h, t in enumerate(_unit_tri_inverses(systems)):
            sl = sls[h]
            t_ref[:, sl] = t
            u, w, qg, kd, qkd, gl = _dn1_outputs(q_ref[:, sl], k_ref[:, sl], v_ref[:, sl], betas[h], gcs[h], decays[h], t, reverse)
            u_ref[:, sl] = u
            w_ref[:, sl] = w.astype(BF)
            qg_ref[:, sl] = qg.astype(BF)
            kd_ref[:, sl] = kd.astype(BF)
            qkd_ref[:, sl] = qkd.astype(BF)
            gl_ref[h] = gl

    tb = pl.BlockSpec((CB, D), lambda i: (i, 0))
    return pl.pallas_call(
        body, grid=(nb,), name=f"dn1_fwd_{direction}",
        in_specs=[tb, tb, tb, pl.BlockSpec((CB, 128), lambda i: (i, 0))],
        out_specs=[tb, tb, tb, tb, tb, pl.BlockSpec((NH, 1, 128), lambda i: (i, 0, 0)), tb],
        out_shape=[jax.ShapeDtypeStruct((T, D), F32)] + [jax.ShapeDtypeStruct((T, D), BF)] * 4
        + [jax.ShapeDtypeStruct((nb * NH, 1, 128), F32), jax.ShapeDtypeStruct((T, D), F32)],
        compiler_params=_cp(),
    )(q, k, v, gb)


def _dn1_bwd(q, k, v, gb, tinv, du, dw, dqg, dkd, dqkd, dgl, direction):
    T = q.shape[0]
    nb = T // CB
    reverse = direction == 1

    def body(q_ref, k_ref, v_ref, gb_ref, t_ref, du_ref, dw_ref, dqg_ref, dkd_ref, dqkd_ref, dgl_ref, dq_ref, dk_ref, dv_ref, dgb_ref):
        gbv = gb_ref[...]
        gcum = _dot_h3(_cum_matrix(reverse), gbv)
        lane = lax.broadcasted_iota(jnp.int32, (CB, 128), 1)
        dgb = jnp.zeros((CB, 128), F32)
        dgcum = jnp.zeros((CB, 128), F32)
        for h in range(NH):
            sl = slice(h * HD, (h + 1) * HD)
            cb, cg = direction * NH + h, 16 + direction * NH + h
            f = functools.partial(_dn1_head, t_saved=t_ref[:, sl], reverse=reverse)
            _, vjp = jax.vjp(f, q_ref[:, sl], k_ref[:, sl], v_ref[:, sl], _lane_bcast(gbv, cb), _lane_bcast(gcum, cg))
            dq, dk, dv, dbeta, dgc = vjp((du_ref[:, sl], dw_ref[:, sl], dqg_ref[:, sl], dkd_ref[:, sl], dqkd_ref[:, sl], dgl_ref[h]))
            dq_ref[:, sl] = dq
            dk_ref[:, sl] = dk
            dv_ref[:, sl] = dv
            dgb = dgb + jnp.where(lane == cb, jnp.sum(dbeta, axis=1, keepdims=True), 0.0)
            dgcum = dgcum + jnp.where(lane == cg, jnp.sum(dgc, axis=1, keepdims=True), 0.0)
        dgb_ref[...] = dgb + _dot_h3(_cum_matrix(not reverse), dgcum)

    tb = pl.BlockSpec((CB, D), lambda i: (i, 0))
    gbs = pl.BlockSpec((CB, 128), lambda i: (i, 0))
    gls = pl.BlockSpec((NH, 1, 128), lambda i: (i, 0, 0))
    return pl.pallas_call(
        body, grid=(nb,), name=f"dn1_bwd_{direction}",
        in_specs=[tb, tb, tb, gbs, tb, tb, tb, tb, tb, tb, gls], out_specs=[tb, tb, tb, gbs],
        out_shape=[jax.ShapeDtypeStruct((T, D), F32)] * 3 + [jax.ShapeDtypeStruct((T, 128), F32)],
        compiler_params=_cp(),
    )(q, k, v, gb, tinv, du, dw, dqg, dkd, dqkd, dgl)


def _dn2_step(u, w, qg, kd, qkd, glrow, s):
    v_new = u - _dot_bf(w, s)
    o = _dot_bf(qg, s) + _dot_bf(qkd, v_new)
    return o, s * glrow + _dot_tn_bf(kd, v_new)


def _scan_order(direction, nlat_b, nall_b):
    if direction == 0:
        return lambda i: (i + nlat_b) % nall_b
    return lambda i: nall_b - 1 - i


def _dn2_fwd(u, w, qg, kd, qkd, gl, direction, nlat):
    T = u.shape[0]
    nb = T // CB
    blk = _scan_order(direction, nlat // CB, nb)

    def body(u_ref, w_ref, qg_ref, kd_ref, qkd_ref, gl_ref, o_ref, sall_ref, s_scr):
        @pl.when(pl.program_id(0) == 0)
        def _():
            s_scr[...] = jnp.zeros_like(s_scr)
        sall_ref[0] = s_scr[...]
        for h in range(NH):
            sl = slice(h * HD, (h + 1) * HD)
            o, s_next = _dn2_step(u_ref[:, sl], w_ref[:, sl], qg_ref[:, sl], kd_ref[:, sl], qkd_ref[:, sl], gl_ref[h], s_scr[h])
            o_ref[:, sl] = o
            s_scr[h] = s_next

    tb = pl.BlockSpec((CB, D), lambda i: (blk(i), 0))
    return pl.pallas_call(
        body, grid=(nb,), name=f"dn2_fwd_{direction}",
        in_specs=[tb] * 5 + [pl.BlockSpec((NH, 1, 128), lambda i: (blk(i), 0, 0))],
        out_specs=[tb, pl.BlockSpec((1, NH, HD, HD), lambda i: (blk(i), 0, 0, 0))],
        out_shape=[jax.ShapeDtypeStruct((T, D), F32), jax.ShapeDtypeStruct((nb, NH, HD, HD), F32)],
        scratch_shapes=[pltpu.VMEM((NH, HD, HD), F32)],
    )(u, w, qg, kd, qkd, gl)


def _dn2_bwd(u, w, qg, kd, qkd, gl, sall, do, direction, nlat):
    T = u.shape[0]
    nb = T // CB
    nlat_b = nlat // CB
    fwd_blk = _scan_order(direction, nlat_b, nb)
    blk = lambda i: fwd_blk(nb - 1 - i)

    def body(u_ref, w_ref, qg_ref, kd_ref, qkd_ref, gl_ref, sall_ref, do_ref,
             du_ref, dw_ref, dqg_ref, dkd_ref, dqkd_ref, dgl_ref, ds_scr):
        i = pl.program_id(0)

        @pl.when(i == 0)
        def _():
            ds_scr[...] = jnp.zeros_like(ds_scr)
        is_lat = blk(i) < nlat_b
        for h in range(NH):
            sl = slice(h * HD, (h + 1) * HD)
            args = (u_ref[:, sl], w_ref[:, sl].astype(F32), qg_ref[:, sl].astype(F32), kd_ref[:, sl].astype(F32),
                    qkd_ref[:, sl].astype(F32), gl_ref[h], sall_ref[0, h])
            _, vjp = jax.vjp(_dn2_step, *args)
            du, dw, dqg, dkd, dqkd, dgl, ds = vjp((jnp.where(is_lat, do_ref[:, sl], 0.0), ds_scr[h]))
            du_ref[:, sl] = du
            dw_ref[:, sl] = dw
            dqg_ref[:, sl] = dqg
            dkd_ref[:, sl] = dkd
            dqkd_ref[:, sl] = dqkd
            dgl_ref[h] = dgl
            ds_scr[h] = ds

    tb = pl.BlockSpec((CB, D), lambda i: (blk(i), 0))
    gls = pl.BlockSpec((NH, 1, 128), lambda i: (blk(i), 0, 0))
    return pl.pallas_call(
        body, grid=(nb,), name=f"dn2_bwd_{direction}",
        in_specs=[tb] * 5 + [gls, pl.BlockSpec((1, NH, HD, HD), lambda i: (blk(i), 0, 0, 0)),
                             pl.BlockSpec((CB, D), lambda i: (jnp.minimum(blk(i), nlat_b - 1), 0))],
        out_specs=[tb] * 5 + [gls],
        out_shape=[jax.ShapeDtypeStruct((T, D), F32)] * 5 + [jax.ShapeDtypeStruct((nb * NH, 1, 128), F32)],
        scratch_shapes=[pltpu.VMEM((NH, HD, HD), F32)],
    )(u, w, qg, kd, qkd, gl, sall, do)


def _ghn_fn(o, gt, w):
    y = o * lax.rsqrt(jnp.mean(o * o, axis=-1, keepdims=True) + EPS)
    return (y * w) * jax.nn.silu(gt)


def _ghn_fwd(o_f, o_b, p, w, nlat):
    tb = _tile(nlat, (256, 128))

    def body(of_ref, ob_ref, gt_ref, w_ref, y_ref):
        for h in range(NH):
            sl = slice(h * HD, (h + 1) * HD)
            y_ref[:, sl] = _ghn_fn(of_ref[:, sl] + ob_ref[:, sl], gt_ref[:, sl], w_ref[...]).astype(BF)

    row = pl.BlockSpec((tb, D), lambda i: (i, 0))
    return pl.pallas_call(
        body, grid=(nlat // tb,), name="ghn_fwd",
        in_specs=[row, row, pl.BlockSpec((tb, D), lambda i: (i, O_GT // D)), pl.BlockSpec((1, HD), lambda i: (0, 0))],
        out_specs=row, out_shape=jax.ShapeDtypeStruct((nlat, D), BF),
    )(o_f, o_b, p, w)


def _ghn_bwd(o_f, o_b, p, w, dy, nlat):
    tb = _tile(nlat, (256, 128))

    def body(of_ref, ob_ref, gt_ref, w_ref, dy_ref, do_ref, dgt_ref, dw_ref):
        @pl.when(pl.program_id(0) == 0)
        def _():
            dw_ref[...] = jnp.zeros_like(dw_ref)
        for h in range(NH):
            sl = slice(h * HD, (h + 1) * HD)
            _, vjp = jax.vjp(_ghn_fn, of_ref[:, sl] + ob_ref[:, sl], gt_ref[:, sl], w_ref[...])
            do, dgt, dw = vjp(dy_ref[:, sl])
            do_ref[:, sl] = do
            dgt_ref[:, sl] = dgt.astype(BF)
            dw_ref[...] += dw

    row = pl.BlockSpec((tb, D), lambda i: (i, 0))
    one = pl.BlockSpec((1, HD), lambda i: (0, 0))
    return pl.pallas_call(
        body, grid=(nlat // tb,), name="ghn_bwd",
        in_specs=[row, row, pl.BlockSpec((tb, D), lambda i: (i, O_GT // D)), one, row],
        out_specs=[row, row, one],
        out_shape=[jax.ShapeDtypeStruct((nlat, D), F32), jax.ShapeDtypeStruct((nlat, D), BF), jax.ShapeDtypeStruct((1, HD), F32)],
    )(o_f, o_b, p, w, dy)


@jax.custom_vjp
def _swap32(x):
    lane = lax.broadcasted_iota(jnp.int32, x.shape, 1)
    return jnp.where((lane & 32) == 0, pltpu.roll(x, 96, 1), pltpu.roll(x, 32, 1))


_swap32.defvjp(lambda x: (_swap32(x), None), lambda _, g: (_swap32(g),))


def _qk_post_fn(x, w, cos, sin):
    y = (x * lax.rsqrt(jnp.mean(x * x, axis=-1, keepdims=True) + EPS)) * w
    return y * cos + _swap32(y) * sin


def _attn_prep_fwd(p, qn, kn, cos, sin):
    T = p.shape[0]
    tb = _tile(T, (256, 128))

    def body(q_ref, k_ref, v_ref, qn_ref, kn_ref, cos_ref, sin_ref, qr_ref, kr_ref, vb_ref):
        cos_v, sin_v = cos_ref[...], sin_ref[...]
        for h in range(NH):
            sl = slice(h * HD, (h + 1) * HD)
            qr_ref[:, sl] = _qk_post_fn(q_ref[:, sl], qn_ref[...], cos_v, sin_v).astype(BF)
        for h in range(KVH):
            sl = slice(h * HD, (h + 1) * HD)
            kr_ref[:, sl] = _qk_post_fn(k_ref[:, sl], kn_ref[...], cos_v, sin_v).astype(BF)
        vb_ref[...] = v_ref[...].astype(BF)

    one = pl.BlockSpec((1, HD), lambda i: (0, 0))
    tab = pl.BlockSpec((tb, HD), lambda i: (i, 0))
    return pl.pallas_call(
        body, grid=(T // tb,), name="attn_prep_fwd",
        in_specs=[pl.BlockSpec((tb, D), lambda i: (i, O_Q // D)), pl.BlockSpec((tb, KV), lambda i: (i, O_K // KV)),
                  pl.BlockSpec((tb, KV), lambda i: (i, O_V // KV)), one, one, tab, tab],
        out_specs=[pl.BlockSpec((tb, D), lambda i: (i, 0)), pl.BlockSpec((tb, KV), lambda i: (i, 0)),
                   pl.BlockSpec((tb, KV), lambda i: (i, 0))],
        out_shape=[jax.ShapeDtypeStruct((T, D), BF), jax.ShapeDtypeStruct((T, KV), BF), jax.ShapeDtypeStruct((T, KV), BF)],
    )(p, p, p, qn, kn, cos, sin)


def _attn_prep_bwd(p, qn, kn, cos, sin, dqr, dkp, dvp, dkc, dvc, nlat):
    T = p.shape[0]
    nqb = nlat // CB
    ncb = (T - nlat) // CB

    def body(q_ref, k_ref, v_ref, qn_ref, kn_ref, cos_ref, sin_ref, dqr_ref, dka_ref, dkb_ref, dkc3_ref, dva_ref, dvb_ref, dvc3_ref,
             dkctx_ref, dvctx_ref, dq_ref, dk_ref, dv_ref, dqn_ref, dkn_ref):
        i = pl.program_id(0)
        is_lat = i < nqb
        cos_v, sin_v = cos_ref[...], sin_ref[...]

        @pl.when(i == 0)
        def _():
            dqn_ref[...] = jnp.zeros_like(dqn_ref)
            dkn_ref[...] = jnp.zeros_like(dkn_ref)

        def band_sum(a_ref, b_ref, c_ref, ctx_ref):
            s = b_ref[0] + jnp.where(i > 0, a_ref[0], 0.0) + jnp.where(i < nqb - 1, c_ref[0], 0.0)
            return jnp.where(is_lat, s, ctx_ref[...])

        dkr = band_sum(dka_ref, dkb_ref, dkc3_ref, dkctx_ref)
        dv_ref[...] = band_sum(dva_ref, dvb_ref, dvc3_ref, dvctx_ref).astype(BF)
        for h in range(NH):
            sl = slice(h * HD, (h + 1) * HD)
            _, vjp = jax.vjp(_qk_post_fn, q_ref[:, sl], qn_ref[...], cos_v, sin_v)
            dq, dqn, _, _ = vjp(jnp.where(is_lat, dqr_ref[:, sl], 0.0))
            dq_ref[:, sl] = dq.astype(BF)
            dqn_ref[...] += dqn
        for h in range(KVH):
            sl = slice(h * HD, (h + 1) * HD)
            _, vjp = jax.vjp(_qk_post_fn, k_ref[:, sl], kn_ref[...], cos_v, sin_v)
            dk, dkn, _, _ = vjp(dkr[:, sl])
            dk_ref[:, sl] = dk.astype(BF)
            dkn_ref[...] += dkn

    one = pl.BlockSpec((1, HD), lambda i: (0, 0))
    tab = pl.BlockSpec((CB, HD), lambda i: (i, 0))
    lat = lambda i: jnp.minimum(i, nqb - 1)

    def part(off, slot):
        return pl.BlockSpec((1, CB, KV), lambda i: (jnp.clip(lat(i) + off, 0, nqb - 1) * 3 + slot, 0, 0))

    ctxs = pl.BlockSpec((CB, KV), lambda i: (jnp.clip(i - nqb, 0, ncb - 1), 0))
    kvs = pl.BlockSpec((CB, KV), lambda i: (i, 0))
    return pl.pallas_call(
        body, grid=(T // CB,), name="attn_prep_bwd",
        in_specs=[pl.BlockSpec((CB, D), lambda i: (i, O_Q // D)), pl.BlockSpec((CB, KV), lambda i: (i, O_K // KV)),
                  pl.BlockSpec((CB, KV), lambda i: (i, O_V // KV)), one, one, tab, tab,
                  pl.BlockSpec((CB, D), lambda i: (lat(i), 0)),
                  part(-1, 2), part(0, 1), part(1, 0), part(-1, 2), part(0, 1), part(1, 0), ctxs, ctxs],
        out_specs=[pl.BlockSpec((CB, D), lambda i: (i, 0)), kvs, kvs, one, one],
        out_shape=[jax.ShapeDtypeStruct((T, D), BF), jax.ShapeDtypeStruct((T, KV), BF), jax.ShapeDtypeStruct((T, KV), BF),
                   jax.ShapeDtypeStruct((1, HD), F32), jax.ShapeDtypeStruct((1, HD), F32)],
    )(p, p, p, qn, kn, cos, sin, dqr, dkp, dkp, dkp, dvp, dvp, dvp, dkc, dvc)


def _attn_group_fn(q0, q1, q2, q3, kall, vall, s0, s1, s2, s3, mask):
    q = jnp.concatenate([q0, q1, q2, q3], axis=0)
    s = _dot_nt_bf(q, kall) * (HD ** -0.5)
    s = jnp.where(mask, s, -1e30)
    sk = jnp.concatenate([jnp.broadcast_to(jnp.mean(t, axis=1, keepdims=True), (CB, 1)) for t in (s0, s1, s2, s3)], axis=0)
    m = lax.stop_gradient(jnp.maximum(jnp.max(s, axis=1, keepdims=True), sk))
    e = jnp.exp(s - m)
    den = jnp.sum(e, axis=1, keepdims=True) + jnp.exp(sk - m)
    return _dot_bf(e / den, vall)


def _attn_mask(i, nqb, lc):
    r, c = _iota2((GRP * CB, 3 * CB + lc))
    rel = c - (r & (CB - 1))
    lo = jnp.where(i > 0, 0, CB)
    hi = jnp.where(i < nqb - 1, 3 * CB, 2 * CB)
    return ((rel >= 0) & (rel <= 2 * CB) & (c >= lo) & (c < hi)) | (c >= 3 * CB)


def _attn_specs(nqb, lc, nlat):
    qs = pl.BlockSpec((CB, GRP * HD), lambda kh, i: (i, kh))
    ka = pl.BlockSpec((CB, HD), lambda kh, i: (jnp.maximum(i - 1, 0), kh))
    kb = pl.BlockSpec((CB, HD), lambda kh, i: (i, kh))
    kc = pl.BlockSpec((CB, HD), lambda kh, i: (jnp.minimum(i + 1, nqb - 1), kh))
    kx = pl.BlockSpec((lc, HD), lambda kh, i: (nlat // lc, kh))
    sk = pl.BlockSpec((1, 8, 128), lambda kh, i: (kh, 0, 0))
    return qs, ka, kb, kc, kx, sk


def _attn_fwd(qr, kr, vb, sink, nlat):
    lc = kr.shape[0] - nlat
    nqb = nlat // CB
    qs, ka, kb, kc, kx, sk = _attn_specs(nqb, lc, nlat)

    def body(q_ref, ka_ref, kb_ref, kc_ref, kx_ref, va_ref, vb_ref, vc_ref, vx_ref, sk_ref, o_ref):
        i = pl.program_id(1)
        kall = jnp.concatenate([ka_ref[...], kb_ref[...], kc_ref[...], kx_ref[...]], axis=0)
        vall = jnp.concatenate([va_ref[...], vb_ref[...], vc_ref[...], vx_ref[...]], axis=0)
        qh = [q_ref[:, g * HD:(g + 1) * HD] for g in range(GRP)]
        sinks = [sk_ref[0, g:g + 1, :] for g in range(GRP)]
        o = _attn_group_fn(*qh, kall, vall, *sinks, _attn_mask(i, nqb, lc))
        for g in range(GRP):
            o_ref[:, g * HD:(g + 1) * HD] = o[g * CB:(g + 1) * CB].astype(BF)

    return pl.pallas_call(
        body, grid=(KVH, nqb), name="attn_fwd",
        in_specs=[qs, ka, kb, kc, kx, ka, kb, kc, kx, sk], out_specs=qs,
        out_shape=jax.ShapeDtypeStruct((nlat, D), BF), compiler_params=_cp(),
    )(qr, kr, kr, kr, kr, vb, vb, vb, vb, sink)


def _attn_bwd(qr, kr, vb, sink, dy, nlat):
    lc = kr.shape[0] - nlat
    nqb = nlat // CB
    qs, ka, kb, kc, kx, sk = _attn_specs(nqb, lc, nlat)

    def body(q_ref, ka_ref, kb_ref, kc_ref, kx_ref, va_ref, vb_ref, vc_ref, vx_ref, sk_ref, dy_ref,
             dq_ref, dkp_ref, dvp_ref, dkx_ref, dvx_ref, dsk_ref):
        i = pl.program_id(1)
        kall = jnp.concatenate([ka_ref[...], kb_ref[...], kc_ref[...], kx_ref[...]], axis=0).astype(F32)
        vall = jnp.concatenate([va_ref[...], vb_ref[...], vc_ref[...], vx_ref[...]], axis=0).astype(F32)
        qh = [q_ref[:, g * HD:(g + 1) * HD].astype(F32) for g in range(GRP)]
        f = functools.partial(_attn_group_fn, mask=_attn_mask(i, nqb, lc))
        _, vjp = jax.vjp(f, *qh, kall, vall, *[sk_ref[0, g:g + 1, :] for g in range(GRP)])
        dyv = jnp.concatenate([dy_ref[:, g * HD:(g + 1) * HD] for g in range(GRP)], axis=0)
        d = vjp(dyv)
        for g in range(GRP):
            dq_ref[:, g * HD:(g + 1) * HD] = d[g]
        dk, dv = d[4], d[5]
        for t in range(3):
            dkp_ref[t] = dk[t * CB:(t + 1) * CB]
            dvp_ref[t] = dv[t * CB:(t + 1) * CB]

        @pl.when(i == 0)
        def _():
            dkx_ref[...] = jnp.zeros_like(dkx_ref)
            dvx_ref[...] = jnp.zeros_like(dvx_ref)
            dsk_ref[...] = jnp.zeros_like(dsk_ref)
        dkx_ref[...] += dk[3 * CB:]
        dvx_ref[...] += dv[3 * CB:]
        for g in range(GRP):
            dsk_ref[0, g:g + 1, :] += d[6 + g]

    dys = pl.BlockSpec((CB, GRP * HD), lambda kh, i: (i, kh))
    parts = pl.BlockSpec((3, CB, HD), lambda kh, i: (i, 0, kh))
    ctxo = pl.BlockSpec((lc, HD), lambda kh, i: (0, kh))
    return pl.pallas_call(
        body, grid=(KVH, nqb), name="attn_bwd",
        in_specs=[qs, ka, kb, kc, kx, ka, kb, kc, kx, sk, dys],
        out_specs=[dys, parts, parts, ctxo, ctxo, sk],
        out_shape=[jax.ShapeDtypeStruct((nlat, D), F32), jax.ShapeDtypeStruct((3 * nqb, CB, KV), F32),
                   jax.ShapeDtypeStruct((3 * nqb, CB, KV), F32), jax.ShapeDtypeStruct((lc, KV), F32),
                   jax.ShapeDtypeStruct((lc, KV), F32), jax.ShapeDtypeStruct((KVH, 8, 128), F32)],
        compiler_params=_cp(),
    )(qr, kr, kr, kr, kr, vb, vb, vb, vb, sink, dy)


def _merge_fn(z_dn, z_at, g_dn, g_at):
    return jax.nn.sigmoid(g_dn) * z_dn + jax.nn.sigmoid(g_at) * z_at


def _merge_fwd(z_dn, z_at, p, nlat):
    tb = _tile(nlat, (256, 128))

    def body(zd_ref, za_ref, gd_ref, ga_ref, o_ref):
        o_ref[...] = _merge_fn(zd_ref[...], za_ref[...], gd_ref[...], ga_ref[...]).astype(BF)

    row = pl.BlockSpec((tb, D), lambda i: (i, 0))
    return pl.pallas_call(
        body, grid=(nlat // tb,), name="merge_fwd",
        in_specs=[row, row, pl.BlockSpec((tb, D), lambda i: (i, O_MG // D)), pl.BlockSpec((tb, D), lambda i: (i, O_MG // D + 1))],
        out_specs=row, out_shape=jax.ShapeDtypeStruct((nlat, D), BF),
    )(z_dn, z_at, p, p)


def _merge_bwd(z_dn, z_at, p, dm, nlat):
    tb = _tile(nlat, (256, 128))

    def body(zd_ref, za_ref, gd_ref, ga_ref, dm_ref, dzd_ref, dza_ref, dg_ref):
        _, vjp = jax.vjp(_merge_fn, zd_ref[...], za_ref[...], gd_ref[...], ga_ref[...])
        dzd, dza, dgd, dga = vjp(dm_ref[...])
        dzd_ref[...] = dzd.astype(BF)
        dza_ref[...] = dza.astype(BF)
        dg_ref[:, :D] = dgd.astype(BF)
        dg_ref[:, D:] = dga.astype(BF)

    row = pl.BlockSpec((tb, D), lambda i: (i, 0))
    return pl.pallas_call(
        body, grid=(nlat // tb,), name="merge_bwd",
        in_specs=[row, row, pl.BlockSpec((tb, D), lambda i: (i, O_MG // D)), pl.BlockSpec((tb, D), lambda i: (i, O_MG // D + 1)), row],
        out_specs=[row, row, pl.BlockSpec((tb, 2 * D), lambda i: (i, 0))],
        out_shape=[jax.ShapeDtypeStruct((nlat, D), BF), jax.ShapeDtypeStruct((nlat, D), BF), jax.ShapeDtypeStruct((nlat, 2 * D), BF)],
    )(z_dn, z_at, p, p, dm)


def _resid_fwd(x, gate, y):
    n = y.shape[0]
    tb = _tile(n, (256, 128))

    def body(x_ref, g_ref, y_ref, o_ref):
        o_ref[...] = x_ref[...] + g_ref[...] * y_ref[...]

    row = pl.BlockSpec((tb, D), lambda i: (i, 0))
    return pl.pallas_call(
        body, grid=(n // tb,), name="resid_fwd",
        in_specs=[row, pl.BlockSpec((1, D), lambda i: (0, 0)), row], out_specs=row,
        out_shape=jax.ShapeDtypeStruct((n, D), F32),
    )(x, gate, y)


def _resid_bwd(dx1a, dx1b, gate, y):
    n = y.shape[0]
    tb = _tile(n, (256, 128))

    def body(a_ref, b_ref, g_ref, y_ref, dx_ref, dy_ref, dg_ref):
        dx = a_ref[...] + b_ref[...]
        dx_ref[...] = dx
        dy_ref[...] = (g_ref[...] * dx).astype(BF)

        @pl.when(pl.program_id(0) == 0)
        def _():
            dg_ref[...] = jnp.zeros_like(dg_ref)
        dg_ref[...] += jnp.sum(dx * y_ref[...], axis=0, keepdims=True)

    row = pl.BlockSpec((tb, D), lambda i: (i, 0))
    one = pl.BlockSpec((1, D), lambda i: (0, 0))
    return pl.pallas_call(
        body, grid=(n // tb,), name="resid_bwd",
        in_specs=[row, row, one, row], out_specs=[row, row, one],
        out_shape=[jax.ShapeDtypeStruct((n, D), F32), jax.ShapeDtypeStruct((n, D), BF), jax.ShapeDtypeStruct((1, D), F32)],
    )(dx1a, dx1b, gate, y)


def _swiglu_fn(ug, uv):
    return jax.nn.silu(ug) * uv


def _swiglu_fwd(uc):
    n = uc.shape[0]
    tb = _tile(n, (128,))
    tc = DFF // 2

    def body(u_ref, o_ref):
        for j in range(2):
            o_ref[:, j * tc:(j + 1) * tc] = _swiglu_fn(u_ref[:, j * tc:(j + 1) * tc], u_ref[:, DFF + j * tc:DFF + (j + 1) * tc]).astype(BF)

    return pl.pallas_call(
        body, grid=(n // tb,), name="swiglu_fwd",
        in_specs=[pl.BlockSpec((tb, 2 * DFF), lambda i: (i, 0))],
        out_specs=pl.BlockSpec((tb, DFF), lambda i: (i, 0)), out_shape=jax.ShapeDtypeStruct((n, DFF), BF),
    )(uc)


def _swiglu_bwd(uc, da):
    n = uc.shape[0]
    tb = _tile(n, (128,))
    tc = DFF // 2

    def body(u_ref, da_ref, du_ref):
        for j in range(2):
            gs, vs = slice(j * tc, (j + 1) * tc), slice(DFF + j * tc, DFF + (j + 1) * tc)
            _, vjp = jax.vjp(_swiglu_fn, u_ref[:, gs], u_ref[:, vs])
            du_ref[:, gs], du_ref[:, vs] = vjp(da_ref[:, gs])

    full = pl.BlockSpec((tb, 2 * DFF), lambda i: (i, 0))
    return pl.pallas_call(
        body, grid=(n // tb,), name="swiglu_bwd", in_specs=[full, pl.BlockSpec((tb, DFF), lambda i: (i, 0))], out_specs=full,
        out_shape=jax.ShapeDtypeStruct((n, 2 * DFF), F32),
    )(uc, da)


def _loss_kernel(x1, gate, ff, target):
    n = x1.shape[0]
    tb = _tile(n, (256, 128))

    def body(x_ref, g_ref, f_ref, t_ref, loss_ref, dy_ref, dff_ref, dg_ref):
        err = x_ref[...] + g_ref[...] * f_ref[...] - t_ref[...]
        dy = err * (1.0 / D)
        dy_ref[...] = dy
        dff_ref[...] = (g_ref[...] * dy).astype(BF)

        @pl.when(pl.program_id(0) == 0)
        def _():
            loss_ref[...] = jnp.zeros_like(loss_ref)
            dg_ref[...] = jnp.zeros_like(dg_ref)
        part = 0.5 * jnp.sum(jnp.sum(err * err, axis=1, keepdims=True) * (1.0 / D), axis=0, keepdims=True)
        loss_ref[...] += jnp.broadcast_to(part, (1, 128))
        dg_ref[...] += jnp.sum(dy * f_ref[...], axis=0, keepdims=True)

    row = pl.BlockSpec((tb, D), lambda i: (i, 0))
    one = pl.BlockSpec((1, D), lambda i: (0, 0))
    return pl.pallas_call(
        body, grid=(n // tb,), name="loss",
        in_specs=[row, one, row, row], out_specs=[pl.BlockSpec((1, 128), lambda i: (0, 0)), row, row, one],
        out_shape=[jax.ShapeDtypeStruct((1, 128), F32), jax.ShapeDtypeStruct((n, D), F32),
                   jax.ShapeDtypeStruct((n, D), BF), jax.ShapeDtypeStruct((1, D), F32)],
    )(x1, gate, ff, target)


def _rope_tables(nlat, lc):
    t = jnp.arange(nlat)
    row = (t // GRID_W).astype(F32)
    col = (t % GRID_W).astype(F32)
    inv_freq = ROPE_BASE ** (-jnp.arange(32, dtype=F32) / 32)
    ar, ac = row[:, None] * inv_freq, col[:, None] * inv_freq
    cos = jnp.concatenate([jnp.cos(ar), jnp.cos(ar), jnp.cos(ac), jnp.cos(ac)], axis=1)
    sin = jnp.concatenate([-jnp.sin(ar), jnp.sin(ar), -jnp.sin(ac), jnp.sin(ac)], axis=1)
    cos = jnp.concatenate([cos, jnp.ones((lc, HD), F32)], axis=0)
    sin = jnp.concatenate([sin, jnp.zeros((lc, HD), F32)], axis=0)
    return cos, sin


def _pad_rows8(w):
    return jnp.concatenate([w, jnp.zeros((8 - w.shape[0], w.shape[1]), w.dtype)], axis=0)


def _pack_w_in(w):
    cuts = [sum(IN_SIZES[:i]) for i in range(len(IN_SIZES) + 1)]
    qkv, gt, b, a, q, k, v, mg = [w[:, cuts[i]:cuts[i + 1]] for i in range(len(IN_SIZES))]
    return jnp.concatenate([qkv, gt, q, mg, k, v, b, a, jnp.zeros((w.shape[0], PW - O_BA - 32), w.dtype)], axis=1)


def _unpack_w_in(g):
    return jnp.concatenate([g[:, O_QKV:O_GT], g[:, O_GT:O_Q], g[:, O_BA:O_BA + 32], g[:, O_Q:O_MG], g[:, O_K:O_V],
                            g[:, O_V:O_BA], g[:, O_MG:O_K]], axis=1)


def _local_step(x, ctx, mod_x, mod_c, target, w_in_p, w_bdn, w_bat, w_out, w_up, w_down,
                norm_mix, norm_ffn, dn_conv, a_log, dt_bias, dn_norm, q_norm, k_norm, sink, ffn_conv, ffn_conv_b):
    L, LC = x.shape[0], ctx.shape[0]
    T = L + LC
    xc = jnp.concatenate([x, ctx], axis=0)
    seg = lambda r: jnp.stack([mod_x[r], mod_c[r]])[:, None, :]
    sh_a, sc_a = seg(0), seg(1)
    g_a, g_f = mod_x[2][None], mod_x[5][None]
    sh_f, sc_f = mod_x[3][None, None], mod_x[4][None, None]
    sh_f2 = jnp.concatenate([sh_f, sh_f], axis=0)
    sc_f2 = jnp.concatenate([sc_f, sc_f], axis=0)
    cos, sin = _rope_tables(L, LC)
    dnc8 = _pad_rows8(dn_conv)
    ffc8 = _pad_rows8(ffn_conv)
    gate_row = lambda a: jnp.concatenate([jnp.zeros((1, 16), F32), a.reshape(1, 16), jnp.zeros((1, 96), F32)], axis=1)
    alog_row, dt_row = gate_row(a_log), gate_row(dt_bias)
    sinkb = jnp.concatenate([jnp.broadcast_to(sink.reshape(KVH, GRP, 1), (KVH, GRP, 128)), jnp.zeros((KVH, 8 - GRP, 128), F32)], axis=1)

    h1 = _norm_mod_fwd(xc, norm_mix, sh_a, sc_a, L, "norm_mix_fwd")
    p = _mm(h1, w_in_p, form="nn", out_dtype=F32, name="in_proj")
    conv = _conv_fwd(p, dnc8, jnp.zeros((1, 3 * D), F32), width=5, col0=0, ncols=3 * D, tc=512, seg_rows=(L, LC), name="dn_conv_fwd")
    q, k, v, gb = _dn_post_fwd(conv, p, alog_row, dt_row)
    o_dir, saved = [], []
    for d in (0, 1):
        u, w, qg, kd, qkd, gl, tinv = _dn1_fwd(q, k, v, gb, d)
        o, sall = _dn2_fwd(u, w, qg, kd, qkd, gl, d, L)
        o_dir.append(o)
        saved.append((u, w, qg, kd, qkd, gl, sall, tinv))
    y_dn = _ghn_fwd(o_dir[0], o_dir[1], p, dn_norm, L)
    qr, kr, vb = _attn_prep_fwd(p, q_norm, k_norm, cos, sin)
    y_at = _attn_fwd(qr, kr, vb, sinkb, L)
    z_dn = _mm(y_dn, w_bdn, form="nn", out_dtype=F32, name="branch_dn")
    z_at = _mm(y_at, w_bat, form="nn", out_dtype=F32, name="branch_at")
    merged = _merge_fwd(z_dn, z_at, p, L)
    mix = _mm(merged, w_out, form="nn", out_dtype=F32, name="out_proj")
    x1 = _resid_fwd(xc, g_a, mix)
    h2 = _norm_mod_fwd(x1, norm_ffn, sh_f2, sc_f2, L, "norm_ffn_fwd")
    u_raw = _mm(h2, w_up, form="nn", out_dtype=F32, name="ffn_up")
    uc = _conv_fwd(u_raw, ffc8, ffn_conv_b, width=3, col0=0, ncols=2 * DFF, tc=DFF // 2, seg_rows=(L,), name="ffn_conv_fwd")
    act = _swiglu_fwd(uc)
    ff = _mm(act, w_down, form="nn", out_dtype=F32, name="ffn_down")
    loss_row, dy, dff, dg_f = _loss_kernel(x1, g_f, ff, target)

    g_down = _mm(act, dff, form="tn", out_dtype=F32, name="g_ffn_down")
    dact = _mm(dff, w_down, form="nt", out_dtype=F32, name="d_act")
    duc = _swiglu_bwd(uc, dact)
    du_raw, g_ffc8, g_ffb = _conv_bwd(u_raw, duc, ffc8, width=3, col0=0, ncols=2 * DFF, tc=DFF // 2, seg_rows=(L,), name="ffn_conv_bwd")
    g_up = _mm(h2, du_raw, form="tn", out_dtype=F32, name="g_ffn_up")
    dh2 = _mm(du_raw, w_up, form="nt", out_dtype=F32, name="d_h2")
    dx1n, g_nffn, dsh_f, dsc_f = _norm_mod_bwd(x1, norm_ffn, sh_f2, sc_f2, dh2, L, "norm_ffn_bwd")
    dx1, dmix, dg_a = _resid_bwd(dy, dx1n, g_a, mix)

    g_out = _mm(merged, dmix, form="tn", out_dtype=F32, name="g_w_out")
    dmerged = _mm(dmix, w_out, form="nt", out_dtype=F32, name="d_merged")
    dz_dn, dz_at, dmg = _merge_bwd(z_dn, z_at, p, dmerged, L)
    g_bdn = _mm(y_dn, dz_dn, form="tn", out_dtype=F32, name="g_branch_dn")
    g_bat = _mm(y_at, dz_at, form="tn", out_dtype=F32, name="g_branch_at")
    dy_dn = _mm(dz_dn, w_bdn, form="nt", out_dtype=F32, name="d_y_dn")
    dy_at = _mm(dz_at, w_bat, form="nt", out_dtype=F32, name="d_y_at")
    dqr, dkp, dvp, dkx, dvx, dsink = _attn_bwd(qr, kr, vb, sinkb, dy_at, L)
    dq_raw, dk_raw, dv_raw, g_qn, g_kn = _attn_prep_bwd(p, q_norm, k_norm, cos, sin, dqr, dkp, dvp, dkx, dvx, L)
    do, dgt, g_dnn = _ghn_bwd(o_dir[0], o_dir[1], p, dn_norm, dy_dn, L)
    dq = dk = dv = dgb = None
    for d in (0, 1):
        u, w, qg, kd, qkd, gl, sall, tinv = saved[d]
        du, dw, dqg, dkd, dqkd, dgl = _dn2_bwd(u, w, qg, kd, qkd, gl, sall, do, d, L)
        parts = _dn1_bwd(q, k, v, gb, tinv, du, dw, dqg, dkd, dqkd, dgl, d)
        if d == 0:
            dq, dk, dv, dgb = parts
        else:
            dq, dk, dv, dgb = _add4(dq, dk, dv, dgb, *parts)
    dconv, dba, g_alog, g_dt = _dn_post_bwd(conv, p, alog_row, dt_row, dq, dk, dv, dgb)
    dqkv_raw, g_dnc8, _ = _conv_bwd(p, dconv, dnc8, width=5, col0=0, ncols=3 * D, tc=512, seg_rows=(L, LC), name="dn_conv_bwd")
    padc = lambda a: jnp.concatenate([a, jnp.zeros((LC, a.shape[1]), a.dtype)], axis=0)
    dp = jnp.concatenate([dqkv_raw, padc(dgt), dq_raw, padc(dmg), dk_raw, dv_raw, dba, jnp.zeros((T, PW - O_BA - 128), BF)], axis=1)
    g_in = _mm(h1, dp, form="tn", out_dtype=F32, name="g_w_in")
    dh1 = _mm(dp, w_in_p, form="nt", out_dtype=F32, name="d_h1")
    dxc, g_nmix, dsh_a, dsc_a = _norm_mod_bwd(xc, norm_mix, sh_a, sc_a, dh1, L, "norm_mix_bwd")
    grad_x = _add2(dx1, dxc)

    zero = jnp.zeros((D,), F32)
    dmod_x = jnp.stack([dsh_a[0, 0], dsc_a[0, 0], dg_a[0], dsh_f[0, 0], dsc_f[0, 0], dg_f[0]])
    dmod_c = jnp.stack([dsh_a[1, 0], dsc_a[1, 0], zero, zero, zero, zero])
    small = dict(
        dmod_x=dmod_x, dmod_c=dmod_c, norm_mix=g_nmix, norm_ffn=g_nffn, dn_conv=g_dnc8[:5], dn_a_log=g_alog[0, 16:32].reshape(2, 8),
        dn_dt_bias=g_dt[0, 16:32].reshape(2, 8), dn_norm=g_dnn, q_norm=g_qn, k_norm=g_kn,
        attn_sink=jnp.sum(dsink[:, :GRP, :], axis=2).reshape(1, NH), ffn_conv=g_ffc8[:3], ffn_conv_b=g_ffb)
    return loss_row[0, 0], grad_x, (g_in, g_bdn, g_bat, g_out, g_up, g_down), small


def _elementwise(fn, args, out_dtypes, name, rows=None):
    n = rows or args[0].shape[0]
    ncol = args[0].shape[1]
    tb = _tile(n, (256, 128, 8))
    nout = len(out_dtypes)

    def body(*refs):
        outs = fn(*[r[...] for r in refs[:len(args)]])
        for o_ref, o in zip(refs[len(args):], outs):
            o_ref[...] = o.astype(o_ref.dtype)

    spec = pl.BlockSpec((tb, ncol), lambda i: (i, 0))
    return pl.pallas_call(
        body, grid=(n // tb,), name=name, in_specs=[spec] * len(args), out_specs=[spec] * nout,
        out_shape=[jax.ShapeDtypeStruct((n, ncol), dt) for dt in out_dtypes],
    )(*args)


def _add2(a, b):
    return _elementwise(lambda x, y: (x + y,), [a, b], [F32], "add2", rows=a.shape[0])[0]


def _add4(a0, a1, a2, a3, b0, b1, b2, b3):
    s = _elementwise(lambda x0, x1, x2, y0, y1, y2: (x0 + y0, x1 + y1, x2 + y2), [a0, a1, a2, b0, b1, b2], [F32] * 3, "add_dqkv")
    g = _elementwise(lambda x, y: (x + y,), [a3, b3], [F32], "add_dgb")
    return s[0], s[1], s[2], g[0]


def _exchange(arrays, scatter, name):
    n = len(arrays)

    def body(*refs):
        ins, outs = refs[:n], refs[n:2 * n]
        send_sems, recv_sems, local_sems = refs[2 * n:]
        x, y, c = lax.axis_index("x"), lax.axis_index("y"), lax.axis_index("c")
        me = 4 * x + 2 * y + c
        started = []
        for k in range(n):
            local = pltpu.make_async_copy(ins[k].at[me] if scatter else ins[k], outs[k].at[me], local_sems.at[k])
            local.start()
            started.append(local)
        pending = []
        for k in range(n):
            for m in range(1, N_DEV):
                px = 1 - x if m & 4 else x
                py = 1 - y if m & 2 else y
                pc = 1 - c if m & 1 else c
                peer = 4 * px + 2 * py + pc
                src = ins[k].at[peer] if scatter else ins[k]
                sem = k * (N_DEV - 1) + m - 1
                push = pltpu.make_async_remote_copy(src_ref=src, dst_ref=outs[k].at[me], send_sem=send_sems.at[sem],
                                                    recv_sem=recv_sems.at[sem], device_id=(px, py, pc), device_id_type=MESH)
                push.start()
                landed = pltpu.make_async_remote_copy(src_ref=src, dst_ref=outs[k].at[peer], send_sem=send_sems.at[sem],
                                                      recv_sem=recv_sems.at[sem], device_id=(px, py, pc), device_id_type=MESH)
                pending.append((push, landed))
        for push, landed in pending:
            landed.wait_recv()
        for push, landed in pending:
            push.wait_send()
        for local in started:
            local.wait()

    hbm = pl.BlockSpec(memory_space=pl.ANY)
    out_shape = [jax.ShapeDtypeStruct(a.shape if scatter else (N_DEV,) + a.shape, a.dtype) for a in arrays]
    return pl.pallas_call(
        body, name=name, in_specs=[hbm] * n, out_specs=[hbm] * n, out_shape=out_shape,
        scratch_shapes=[pltpu.SemaphoreType.DMA((n * (N_DEV - 1),)), pltpu.SemaphoreType.DMA((n * (N_DEV - 1),)),
                        pltpu.SemaphoreType.DMA((n,))],
    )(*arrays)


def _ada_fwd(c16, w_ada, b_ada):
    def body(c_ref, w_ref, b_ref, o_ref):
        o_ref[...] = _dot_hi(jax.nn.silu(c_ref[...]), w_ref[...]) + b_ref[...]

    return pl.pallas_call(body, name="ada_fwd", out_shape=jax.ShapeDtypeStruct((16, w_ada.shape[1]), F32))(c16, w_ada, b_ada)


def _ada_bwd(c16, w_ada, dmx, dmc):
    def body(c_ref, w_ref, dmx_ref, dmc_ref, gw_ref, pc_ref):
        dmc_tot = dmc_ref[0:1, :]
        for d in range(1, N_DEV):
            dmc_tot = dmc_tot + dmc_ref[d:d + 1, :]
        dm16 = jnp.concatenate([dmx_ref[...], jnp.broadcast_to(dmc_tot, (8, dmc_tot.shape[1]))], axis=0)
        row = lax.broadcasted_iota(jnp.int32, dm16.shape, 0)
        dm16 = jnp.where(row <= 8, dm16, 0.0)
        s = jax.nn.silu(c_ref[...])
        gw_ref[...] = lax.dot_general(s, dm16, (_DIMS["tn"], ((), ())), precision=HI, preferred_element_type=F32)
        pc = lax.dot_general(dm16, w_ref[...], (_DIMS["nt"], ((), ())), precision=HI, preferred_element_type=F32)
        pc_ref[...] = pc[8:9, :]

    return pl.pallas_call(body, name="ada_bwd", out_shape=[jax.ShapeDtypeStruct(w_ada.shape, F32), jax.ShapeDtypeStruct((1, D), F32)],
                          compiler_params=_cp())(c16, w_ada, dmx, dmc)


def _cctx_grad(pc_all, c_ctx_row):
    def body(pc_ref, c_ref, g_ref):
        tot = pc_ref[0]
        for d in range(1, N_DEV):
            tot = tot + pc_ref[d]
        _, vjp = jax.vjp(jax.nn.silu, c_ref[...])
        g_ref[...] = vjp(tot)[0]

    return pl.pallas_call(body, name="cctx_grad", out_shape=jax.ShapeDtypeStruct((1, D), F32))(pc_all, c_ctx_row)


def _adamw(parts, w, m, v, name):
    ns, R, C = parts.shape
    tb = _tile(R, (128, 64, 32, 16, 8))

    def body(p_ref, w_ref, m_ref, v_ref, g_ref, d_ref, mo_ref, vo_ref):
        g = p_ref[0]
        for s in range(1, ns):
            g = g + p_ref[s]
        m2 = ADAM_B1 * m_ref[...] + (1.0 - ADAM_B1) * g
        v2 = ADAM_B2 * v_ref[...] + (1.0 - ADAM_B2) * jnp.square(g)
        m_hat = m2 / (1.0 - ADAM_B1 ** ADAM_STEP)
        v_hat = v2 / (1.0 - ADAM_B2 ** ADAM_STEP)
        g_ref[...] = g
        d_ref[...] = -ADAM_LR * (m_hat / (jnp.sqrt(v_hat) + ADAM_EPS) + ADAM_WD * w_ref[...])
        mo_ref[...] = m2
        vo_ref[...] = v2

    row = pl.BlockSpec((tb, C), lambda i: (i, 0))
    return pl.pallas_call(
        body, grid=(R // tb,), name=name,
        in_specs=[pl.BlockSpec((ns, tb, C), lambda i: (0, i, 0)), row, row, row], out_specs=[row] * 4,
        out_shape=[jax.ShapeDtypeStruct((R, C), F32)] * 4, compiler_params=_cp(),
    )(parts, w, m, v)


_SMALL = (("dmod_x", 6 * D), ("dmod_c", 6 * D), ("b_ada", 6 * D), ("norm_mix", D), ("norm_ffn", D), ("dn_a_log", 16),
          ("dn_dt_bias", 16), ("dn_norm", HD), ("q_norm", HD), ("k_norm", HD), ("attn_sink", NH), ("ffn_conv_b", 2 * DFF),
          ("dn_conv", 5 * 3 * D), ("ffn_conv", 3 * 2 * DFF))
_SMALL_ROWS = -(-sum(n for _, n in _SMALL) // 1024) * 8


def _pack_small(d):
    flat = jnp.concatenate([d[k].reshape(-1).astype(F32) if k in d else jnp.zeros((n,), F32) for k, n in _SMALL])
    return jnp.concatenate([flat, jnp.zeros((_SMALL_ROWS * 128 - flat.shape[0],), F32)]).reshape(_SMALL_ROWS, 128)


def _unpack_small(a):
    flat = a.reshape(a.shape[:-2] + (-1,))
    out, off = {}, 0
    for k, n in _SMALL:
        out[k] = flat[..., off:off + n]
        off += n
    return out


def kernel(x, c, ctx, c_ctx, w_ada, b_ada, norm_mix, norm_ffn, w_in, dn_conv, dn_a_log, dn_dt_bias, dn_norm, q_norm, k_norm, attn_sink, w_branch_dn, w_branch_attn, w_out, ffn_up, ffn_conv, ffn_conv_b, ffn_down, loss_target, m_c_ctx, m_w_ada, m_b_ada, m_norm_mix, m_norm_ffn, m_w_in, m_dn_conv, m_dn_a_log, m_dn_dt_bias, m_dn_norm, m_q_norm, m_k_norm, m_attn_sink, m_w_branch_dn, m_w_branch_attn, m_w_out, m_ffn_up, m_ffn_conv, m_ffn_conv_b, m_ffn_down, v_c_ctx, v_w_ada, v_b_ada, v_norm_mix, v_norm_ffn, v_w_in, v_dn_conv, v_dn_a_log, v_dn_dt_bias, v_dn_norm, v_q_norm, v_k_norm, v_attn_sink, v_w_branch_dn, v_w_branch_attn, v_w_out, v_ffn_up, v_ffn_conv, v_ffn_conv_b, v_ffn_down):
    me = 4 * lax.axis_index("x") + 2 * lax.axis_index("y") + lax.axis_index("c")
    ada_cols = w_ada.shape[2]

    gathered = _exchange([w_in[0].astype(BF), w_branch_dn[0].astype(BF), w_branch_attn[0].astype(BF), w_out[0].astype(BF),
                          ffn_up[0].astype(BF), ffn_down[0].astype(BF), c, dn_conv[0], ffn_conv[0]],
                         scatter=False, name="gather_weights")
    cols = lambda a: jnp.swapaxes(a, 0, 1).reshape(a.shape[1], -1)
    rows = lambda a: a.reshape(-1, a.shape[2])
    w_in_p = _pack_w_in(cols(gathered[0]))
    w_bdn, w_bat, w_o = rows(gathered[1]), rows(gathered[2]), rows(gathered[3])
    w_up, w_down = cols(gathered[4]), rows(gathered[5])
    c_all = gathered[6][:, 0, :]

    c16 = jnp.concatenate([c_all, c_ctx[None], jnp.zeros((7, D), F32)], axis=0)
    b_loc = lax.dynamic_slice_in_dim(b_ada, me * ada_cols, ada_cols, axis=1)
    mod_part = _ada_fwd(c16, w_ada[0], b_loc)
    mod_all = cols(_exchange([mod_part], scatter=False, name="gather_mod")[0])
    mod_x = lax.dynamic_slice_in_dim(mod_all, me, 1, axis=0).reshape(6, D)
    mod_c = mod_all[8].reshape(6, D)

    loss_loc, grad_x, big, small = _local_step(
        x[0], ctx[0], mod_x, mod_c, loss_target[0], w_in_p, w_bdn, w_bat, w_o, w_up, w_down,
        norm_mix, norm_ffn, cols(gathered[7]), dn_a_log[0], dn_dt_bias[0], dn_norm, q_norm, k_norm, attn_sink[0], cols(gathered[8]),
        ffn_conv_b)
    loss = lax.psum(loss_loc, ("x", "y", "c"))

    g_in, g_bdn, g_bat, g_out, g_up, g_down = big
    col_blocks = lambda g: jnp.swapaxes(g.reshape(g.shape[0], N_DEV, -1), 0, 1)
    row_blocks = lambda g: g.reshape(N_DEV, -1, g.shape[1])
    landed = _exchange([col_blocks(_unpack_w_in(g_in)), row_blocks(g_bdn), row_blocks(g_bat), row_blocks(g_out),
                        col_blocks(g_up), row_blocks(g_down)], scatter=True, name="scatter_grads")
    res = {}
    res["w_in"] = _adamw(landed[0], w_in[0], m_w_in[0], v_w_in[0], "adamw_w_in")
    res["w_branch_dn"] = _adamw(landed[1], w_branch_dn[0], m_w_branch_dn[0], v_w_branch_dn[0], "adamw_w_branch_dn")
    res["w_branch_attn"] = _adamw(landed[2], w_branch_attn[0], m_w_branch_attn[0], v_w_branch_attn[0], "adamw_w_branch_attn")
    res["w_out"] = _adamw(landed[3], w_out[0], m_w_out[0], v_w_out[0], "adamw_w_out")
    res["ffn_up"] = _adamw(landed[4], ffn_up[0], m_ffn_up[0], v_ffn_up[0], "adamw_ffn_up")
    res["ffn_down"] = _adamw(landed[5], ffn_down[0], m_ffn_down[0], v_ffn_down[0], "adamw_ffn_down")

    small = dict(small)
    small["b_ada"] = small["dmod_x"] + small["dmod_c"]
    parts = _exchange([_pack_small(small)], scatter=False, name="gather_small")[0]
    per_dev = _unpack_small(parts)
    given = dict(b_ada=(b_ada, m_b_ada, v_b_ada), norm_mix=(norm_mix, m_norm_mix, v_norm_mix), norm_ffn=(norm_ffn, m_norm_ffn, v_norm_ffn),
                 dn_a_log=(dn_a_log, m_dn_a_log, v_dn_a_log), dn_dt_bias=(dn_dt_bias, m_dn_dt_bias, v_dn_dt_bias),
                 dn_norm=(dn_norm, m_dn_norm, v_dn_norm), q_norm=(q_norm, m_q_norm, v_q_norm), k_norm=(k_norm, m_k_norm, v_k_norm),
                 attn_sink=(attn_sink, m_attn_sink, v_attn_sink), ffn_conv_b=(ffn_conv_b, m_ffn_conv_b, v_ffn_conv_b))
    packs = [_pack_small({k: t[j] for k, t in given.items()}) for j in range(3)]
    upd = [_unpack_small(a) for a in _adamw(parts, packs[0], packs[1], packs[2], "adamw_small")]
    for k, t in given.items():
        res[k] = tuple(u[k].reshape(t[0].shape) for u in upd)
    dnc = lax.dynamic_slice_in_dim(upd[0]["dn_conv"].reshape(5, 3 * D), me * dn_conv.shape[2], dn_conv.shape[2], axis=1)
    ffc = lax.dynamic_slice_in_dim(upd[0]["ffn_conv"].reshape(3, 2 * DFF), me * ffn_conv.shape[2], ffn_conv.shape[2], axis=1)
    r8 = lambda a: _pad_rows8(a)
    t = _adamw(r8(dnc)[None], r8(dn_conv[0]), r8(m_dn_conv[0]), r8(v_dn_conv[0]), "adamw_dn_conv")
    res["dn_conv"] = tuple(a[:5][None] for a in t)
    t = _adamw(r8(ffc)[None], r8(ffn_conv[0]), r8(m_ffn_conv[0]), r8(v_ffn_conv[0]), "adamw_ffn_conv")
    res["ffn_conv"] = tuple(a[:3][None] for a in t)

    dmx = lax.dynamic_slice_in_dim(per_dev["dmod_x"], me * ada_cols, ada_cols, axis=1)
    dmc = lax.dynamic_slice_in_dim(per_dev["dmod_c"], me * ada_cols, ada_cols, axis=1)
    g_ada, pc = _ada_bwd(c16, w_ada[0], dmx, dmc)
    res["w_ada"] = _adamw(g_ada[None], w_ada[0], m_w_ada[0], v_w_ada[0], "adamw_w_ada")
    pc_all = _exchange([pc], scatter=False, name="gather_cctx")[0]
    g_cctx = _cctx_grad(pc_all, c_ctx[None])
    r8b = lambda a: jnp.broadcast_to(a, (8, D))
    t = _adamw(r8b(g_cctx)[None], r8b(c_ctx[None]), r8b(m_c_ctx[None]), r8b(v_c_ctx[None]), "adamw_c_ctx")
    res["c_ctx"] = tuple(a[0] for a in t)

    names = ("c_ctx", "w_ada", "b_ada", "norm_mix", "norm_ffn", "w_in", "dn_conv", "dn_a_log", "dn_dt_bias", "dn_norm", "q_norm",
             "k_norm", "attn_sink", "w_branch_dn", "w_branch_attn", "w_out", "ffn_up", "ffn_conv", "ffn_conv_b", "ffn_down")
    lead = ("w_ada", "w_in", "w_branch_dn", "w_branch_attn", "w_out", "ffn_up", "ffn_down")
    fix = lambda k, a: a[None] if k in lead else a
    outs = [loss, grad_x[None]]
    for j in range(4):
        outs += [fix(k, res[k][j]) for k in names]
    return tuple(outs)
```

```python
import functools

import jax
import jax.numpy as jnp
from jax import lax
from jax.experimental import pallas as pl
from jax.experimental.pallas import tpu as pltpu

F32 = jnp.float32
BF = jnp.bfloat16
HI = lax.Precision.HIGHEST
MESH = pl.DeviceIdType.MESH

D = 1024
NH = 8
HD = 128
KVH = 2
GRP = 4
KV = KVH * HD
DFF = 2816
CB = 128
GRID_W = 64
ROPE_BASE = 10000.0
EPS = 1e-6
N_DEV = 8
PW = 8192
O_QKV, O_GT, O_Q, O_MG, O_K, O_V, O_BA = 0, 3072, 4096, 5120, 7168, 7424, 7680
IN_SIZES = (3072, 1024, 16, 16, 1024, 256, 256, 2048)
IN_DIM = sum(IN_SIZES)
ADAM_LR, ADAM_B1, ADAM_B2, ADAM_EPS, ADAM_WD, ADAM_STEP = 0.001, 0.9, 0.999, 1e-08, 0.01, 10
VMEM_LIMIT = 56 * 1024 * 1024


def _cp():
    return pltpu.CompilerParams(vmem_limit_bytes=VMEM_LIMIT)


def _tile(n, cands):
    for c in cands:
        if n % c == 0:
            return c
    return n


def _iota2(shape):
    return lax.broadcasted_iota(jnp.int32, shape, 0), lax.broadcasted_iota(jnp.int32, shape, 1)


_DIMS = {"nn": ((1,), (0,)), "nt": ((1,), (1,)), "tn": ((0,), (0,))}


def _mm(a, b, *, form, out_dtype, name, tm=None, tn=None, tk=None):
    if form == "tn":
        K, M = a.shape
        N = b.shape[1]
    else:
        M, K = a.shape
        N = b.shape[0] if form == "nt" else b.shape[1]
    tm = tm or _tile(M, (1024, 640, 512, 256, 128))
    tn = tn or _tile(N, (1408, 1024, 512, 256, 128))
    tk = tk or _tile(K, (2048, 1408, 1024, 640, 512, 256, 128))
    nk = K // tk
    dims = (_DIMS[form], ((), ()))

    def body(a_ref, b_ref, o_ref, *acc):
        k = pl.program_id(2)
        part = lax.dot_general(a_ref[...].astype(BF), b_ref[...].astype(BF), dims, preferred_element_type=F32)
        if nk == 1:
            o_ref[...] = part.astype(out_dtype)
        else:
            acc_ref = acc[0]

            @pl.when(k == 0)
            def _():
                acc_ref[...] = part

            @pl.when(k > 0)
            def _():
                acc_ref[...] += part

            @pl.when(k == nk - 1)
            def _():
                o_ref[...] = acc_ref[...].astype(out_dtype)

    if form == "tn":
        a_spec = pl.BlockSpec((tk, tm), lambda i, j, k: (k, i))
    else:
        a_spec = pl.BlockSpec((tm, tk), lambda i, j, k: (i, k))
    if form == "nt":
        b_spec = pl.BlockSpec((tn, tk), lambda i, j, k: (j, k))
    else:
        b_spec = pl.BlockSpec((tk, tn), lambda i, j, k: (k, j))
    return pl.pallas_call(
        body, grid=(M // tm, N // tn, nk), name=name,
        in_specs=[a_spec, b_spec], out_specs=pl.BlockSpec((tm, tn), lambda i, j, k: (i, j)),
        out_shape=jax.ShapeDtypeStruct((M, N), out_dtype),
        scratch_shapes=[] if nk == 1 else [pltpu.VMEM((tm, tn), F32)],
        compiler_params=_cp(),
    )(a, b)


def _norm_mod_fn(x, nw, sh, sc):
    y = x * lax.rsqrt(jnp.mean(x * x, axis=-1, keepdims=True) + EPS)
    return (y * nw) * (1.0 + sc) + sh


def _norm_mod_fwd(x, nw, sh, sc, nlat, name):
    T = x.shape[0]
    tb = _tile(T, (256, 128))
    nlb = nlat // tb

    def body(x_ref, nw_ref, sh_ref, sc_ref, h_ref):
        h_ref[...] = _norm_mod_fn(x_ref[...], nw_ref[...], sh_ref[0], sc_ref[0]).astype(BF)

    seg = pl.BlockSpec((1, 1, D), lambda i: (jnp.where(i >= nlb, 1, 0), 0, 0))
    return pl.pallas_call(
        body, grid=(T // tb,), name=name,
        in_specs=[pl.BlockSpec((tb, D), lambda i: (i, 0)), pl.BlockSpec((1, D), lambda i: (0, 0)), seg, seg],
        out_specs=pl.BlockSpec((tb, D), lambda i: (i, 0)),
        out_shape=jax.ShapeDtypeStruct((T, D), BF),
    )(x, nw, sh, sc)


def _norm_mod_bwd(x, nw, sh, sc, dh, nlat, name):
    T = x.shape[0]
    tb = _tile(T, (256, 128))
    nlb = nlat // tb

    def body(x_ref, nw_ref, sh_ref, sc_ref, dh_ref, dx_ref, dnw_ref, dsh_ref, dsc_ref):
        i = pl.program_id(0)
        _, vjp = jax.vjp(_norm_mod_fn, x_ref[...], nw_ref[...], sh_ref[0], sc_ref[0])
        dx, dnw, dsh, dsc = vjp(dh_ref[...])
        dx_ref[...] = dx

        @pl.when(i == 0)
        def _():
            dnw_ref[...] = jnp.zeros_like(dnw_ref)

        @pl.when((i == 0) | (i == nlb))
        def _():
            dsh_ref[...] = jnp.zeros_like(dsh_ref)
            dsc_ref[...] = jnp.zeros_like(dsc_ref)

        dnw_ref[...] += dnw
        dsh_ref[0] += dsh
        dsc_ref[0] += dsc

    seg = pl.BlockSpec((1, 1, D), lambda i: (jnp.where(i >= nlb, 1, 0), 0, 0))
    row = pl.BlockSpec((tb, D), lambda i: (i, 0))
    one = pl.BlockSpec((1, D), lambda i: (0, 0))
    return pl.pallas_call(
        body, grid=(T // tb,), name=name,
        in_specs=[row, one, seg, seg, row], out_specs=[row, one, seg, seg],
        out_shape=[jax.ShapeDtypeStruct((T, D), F32), jax.ShapeDtypeStruct((1, D), F32),
                   jax.ShapeDtypeStruct((2, 1, D), F32), jax.ShapeDtypeStruct((2, 1, D), F32)],
    )(x, nw, sh, sc, dh)


def _halo_specs(tb, tc, nrows, col0):
    r8 = tb // 8
    cur = pl.BlockSpec((tb, tc), lambda j, i: (i, col0 + j))
    prev = pl.BlockSpec((8, tc), lambda j, i: (jnp.maximum(i * r8 - 1, 0), col0 + j))
    nxt = pl.BlockSpec((8, tc), lambda j, i: (jnp.minimum((i + 1) * r8, nrows // 8 - 1), col0 + j))
    return cur, prev, nxt


def _extend(prev_ref, cur_ref, next_ref, i, starts, ends):
    keep_p = functools.reduce(lambda a, b: a & b, [i != s for s in starts])
    keep_n = functools.reduce(lambda a, b: a & b, [i != e for e in ends])
    p = jnp.where(keep_p, prev_ref[...].astype(F32), 0.0)
    n = jnp.where(keep_n, next_ref[...].astype(F32), 0.0)
    return jnp.concatenate([p, cur_ref[...].astype(F32), n], axis=0)


def _shifted(xe, shift, tb):
    n = tb + 16
    s = shift % n
    xs = xe if s == 0 else pltpu.roll(xe, s, 0)
    return xs[8:8 + tb]


def _conv_fwd(x, w8, bias, *, width, col0, ncols, tc, seg_rows, name):
    T = x.shape[0]
    tb = _tile(T, (256, 128))
    r = width // 2
    bounds = [0]
    for s in seg_rows:
        bounds.append(bounds[-1] + s // tb)
    starts, ends = bounds[:-1], [b - 1 for b in bounds[1:]]

    def body(cur_ref, prev_ref, next_ref, w_ref, b_ref, o_ref):
        i = pl.program_id(1)
        xe = _extend(prev_ref, cur_ref, next_ref, i, starts, ends)
        acc = _shifted(xe, r, tb) * w_ref[0:1, :]
        for j in range(1, width):
            acc = acc + _shifted(xe, r - j, tb) * w_ref[j:j + 1, :]
        o_ref[...] = acc + b_ref[...]

    cur, prev, nxt = _halo_specs(tb, tc, T, col0)
    return pl.pallas_call(
        body, grid=(ncols // tc, T // tb), name=name,
        in_specs=[cur, prev, nxt, pl.BlockSpec((8, tc), lambda j, i: (0, j)), pl.BlockSpec((1, tc), lambda j, i: (0, j))],
        out_specs=pl.BlockSpec((tb, tc), lambda j, i: (i, j)),
        out_shape=jax.ShapeDtypeStruct((T, ncols), F32),
    )(x, x, x, w8, bias)


def _conv_bwd(x, dc, w8, *, width, col0, ncols, tc, seg_rows, name):
    T = x.shape[0]
    tb = _tile(T, (256, 128))
    r = width // 2
    bounds = [0]
    for s in seg_rows:
        bounds.append(bounds[-1] + s // tb)
    starts, ends = bounds[:-1], [b - 1 for b in bounds[1:]]

    def body(cur_ref, prev_ref, next_ref, dcur_ref, dprev_ref, dnext_ref, w_ref, dx_ref, dw_ref, db_ref):
        i = pl.program_id(1)
        xe = _extend(prev_ref, cur_ref, next_ref, i, starts, ends)
        de = _extend(dprev_ref, dcur_ref, dnext_ref, i, starts, ends)
        dcur = dcur_ref[...]

        @pl.when(i == 0)
        def _():
            dw_ref[...] = jnp.zeros_like(dw_ref)
            db_ref[...] = jnp.zeros_like(db_ref)

        acc = _shifted(de, -r, tb) * w_ref[0:1, :]
        for j in range(1, width):
            acc = acc + _shifted(de, j - r, tb) * w_ref[j:j + 1, :]
        dx_ref[...] = acc.astype(BF)
        for j in range(width):
            dw_ref[j:j + 1, :] += jnp.sum(dcur * _shifted(xe, r - j, tb), axis=0, keepdims=True)
        db_ref[...] += jnp.sum(dcur, axis=0, keepdims=True)

    cur, prev, nxt = _halo_specs(tb, tc, T, col0)
    dcur, dprev, dnxt = _halo_specs(tb, tc, T, 0)
    wspec = pl.BlockSpec((8, tc), lambda j, i: (0, j))
    return pl.pallas_call(
        body, grid=(ncols // tc, T // tb), name=name,
        in_specs=[cur, prev, nxt, dcur, dprev, dnxt, wspec],
        out_specs=[pl.BlockSpec((tb, tc), lambda j, i: (i, j)), wspec, pl.BlockSpec((1, tc), lambda j, i: (0, j))],
        out_shape=[jax.ShapeDtypeStruct((T, ncols), BF), jax.ShapeDtypeStruct((8, ncols), F32),
                   jax.ShapeDtypeStruct((1, ncols), F32)],
    )(x, x, x, dc, dc, dc, w8)


def _softplus(x):
    return jnp.maximum(x, 0.0) + jnp.log(1.0 + jnp.exp(-jnp.abs(x)))


def _gates_fn(ba, alog_row, dt_row):
    col = lax.broadcasted_iota(jnp.int32, ba.shape, 1)
    beta = jax.nn.sigmoid(ba)
    g = -jnp.exp(alog_row) * _softplus(ba + dt_row)
    return jnp.where(col < 16, beta, jnp.where(col < 32, g, 0.0))


def _qkv_post_fn(c, kind):
    y = jax.nn.silu(c)
    if kind == 2:
        return y
    n = y * lax.rsqrt(jnp.sum(y * y, axis=-1, keepdims=True) + EPS)
    return n * (HD ** -0.5) if kind == 0 else n


def _dn_post_fwd(conv, p, alog_row, dt_row):
    T = conv.shape[0]
    tb = _tile(T, (256, 128))

    def body(c_ref, ba_ref, al_ref, dt_ref, q_ref, k_ref, v_ref, gb_ref):
        outs = (q_ref, k_ref, v_ref)
        for kind in range(3):
            for h in range(NH):
                src = slice(kind * D + h * HD, kind * D + (h + 1) * HD)
                outs[kind][:, h * HD:(h + 1) * HD] = _qkv_post_fn(c_ref[:, src], kind)
        gb_ref[...] = _gates_fn(ba_ref[...], al_ref[...], dt_ref[...])

    row = pl.BlockSpec((tb, D), lambda i: (i, 0))
    one = pl.BlockSpec((1, 128), lambda i: (0, 0))
    return pl.pallas_call(
        body, grid=(T // tb,), name="dn_post_fwd",
        in_specs=[pl.BlockSpec((tb, 3 * D), lambda i: (i, 0)), pl.BlockSpec((tb, 128), lambda i: (i, O_BA // 128)), one, one],
        out_specs=[row, row, row, pl.BlockSpec((tb, 128), lambda i: (i, 0))],
        out_shape=[jax.ShapeDtypeStruct((T, D), F32)] * 3 + [jax.ShapeDtypeStruct((T, 128), F32)],
    )(conv, p, alog_row, dt_row)


def _dn_post_bwd(conv, p, alog_row, dt_row, dq, dk, dv, dgb):
    T = conv.shape[0]
    tb = _tile(T, (256, 128))

    def body(c_ref, ba_ref, al_ref, dt_ref, dq_ref, dk_ref, dv_ref, dgb_ref, dc_ref, dba_ref, dal_ref, ddt_ref):
        i = pl.program_id(0)
        douts = (dq_ref, dk_ref, dv_ref)
        for kind in range(3):
            for h in range(NH):
                src = slice(kind * D + h * HD, kind * D + (h + 1) * HD)
                _, vjp = jax.vjp(functools.partial(_qkv_post_fn, kind=kind), c_ref[:, src])
                dc_ref[:, src] = vjp(douts[kind][:, h * HD:(h + 1) * HD])[0]
        _, vjp = jax.vjp(_gates_fn, ba_ref[...], al_ref[...], dt_ref[...])
        dba, dal, ddt = vjp(dgb_ref[...])
        dba_ref[...] = dba.astype(BF)

        @pl.when(i == 0)
        def _():
            dal_ref[...] = jnp.zeros_like(dal_ref)
            ddt_ref[...] = jnp.zeros_like(ddt_ref)
        dal_ref[...] += dal
        ddt_ref[...] += ddt

    row = pl.BlockSpec((tb, D), lambda i: (i, 0))
    one = pl.BlockSpec((1, 128), lambda i: (0, 0))
    nar = pl.BlockSpec((tb, 128), lambda i: (i, 0))
    return pl.pallas_call(
        body, grid=(T // tb,), name="dn_post_bwd",
        in_specs=[pl.BlockSpec((tb, 3 * D), lambda i: (i, 0)), pl.BlockSpec((tb, 128), lambda i: (i, O_BA // 128)), one, one,
                  row, row, row, nar],
        out_specs=[pl.BlockSpec((tb, 3 * D), lambda i: (i, 0)), nar, one, one],
        out_shape=[jax.ShapeDtypeStruct((T, 3 * D), F32), jax.ShapeDtypeStruct((T, 128), BF),
                   jax.ShapeDtypeStruct((1, 128), F32), jax.ShapeDtypeStruct((1, 128), F32)],
    )(conv, p, alog_row, dt_row, dq, dk, dv, dgb)


def _dot_hi(a, b):
    return jnp.dot(a, b, precision=HI, preferred_element_type=F32)


def _dot_bf(a, b):
    return jnp.dot(a.astype(BF), b.astype(BF), preferred_element_type=F32)


def _dot_nt_bf(a, b):
    return lax.dot_general(a.astype(BF), b.astype(BF), (_DIMS["nt"], ((), ())), preferred_element_type=F32)


def _dot_tn_bf(a, b):
    return lax.dot_general(a.astype(BF), b.astype(BF), (_DIMS["tn"], ((), ())), preferred_element_type=F32)


def _dot_h3(a, b):
    return jnp.dot(a, b, precision=lax.Precision.HIGH, preferred_element_type=F32)


def _unit_tri_inverses(mats):
    r, c = _iota2((CB, CB))
    eye = (r == c).astype(F32)
    a8 = [jnp.where((r // 8) == (c // 8), a, 0.0) for a in mats]
    a2 = [_dot_h3(x, x) for x in a8]
    a4 = [_dot_h3(x, x) for x in a2]
    t = [_dot_h3(eye - x, eye + y) for x, y in zip(a8, a2)]
    t = [_dot_h3(x, eye + y) for x, y in zip(t, a4)]
    b = 8
    while b < CB:
        mask = ((r // (2 * b)) == (c // (2 * b))) & ((r // b) != (c // b))
        te = [_dot_h3(x, jnp.where(mask, a, 0.0)) for x, a in zip(t, mats)]
        t = [x - _dot_h3(y, x) for x, y in zip(t, te)]
        b *= 2
    return t


@jax.custom_vjp
def _saved_inverse(a, t):
    return t


_saved_inverse.defvjp(lambda a, t: (t, t), lambda t, dt: (-_dot_h3(_dot_h3(t.T, dt), t.T), jnp.zeros_like(t)))


def _dn1_decay(gc, reverse):
    r, c = _iota2((CB, CB))
    incl = (c >= r) if reverse else (c <= r)
    return jnp.where(incl, jnp.exp(jnp.where(incl, gc - gc.T, 0.0)), 0.0)


def _dn1_heads(qs, ks, vs, betas, gcs, ts_saved, reverse, kks=None, qks=None):
    r, c = _iota2((CB, CB))
    strict = (c > r) if reverse else (c < r)
    decays = [_dn1_decay(gc, reverse) for gc in gcs]
    kks = kks or [_dot_nt_bf(k, k) for k in ks]
    systems = [jnp.where(strict, b * kk * dc, 0.0) for b, kk, dc in zip(betas, kks, decays)]
    if ts_saved is None:
        ts = _unit_tri_inverses(systems)
    else:
        ts = [_saved_inverse(a, t) for a, t in zip(systems, ts_saved)]
    egs = [jnp.exp(gc) for gc in gcs]
    us = [_dot_h3(t, v * b) for t, v, b in zip(ts, vs, betas)]
    ws = [_dot_h3(t, k * (b * eg)) for t, k, b, eg in zip(ts, ks, betas, egs)]
    qks = qks or [_dot_nt_bf(q, k) for q, k in zip(qs, ks)]
    last = 0 if reverse else CB - 1
    glogs = [jnp.sum(jnp.where(r == last, gc, 0.0), axis=0, keepdims=True) for gc in gcs]
    outs = [(u, w, q * eg, k * jnp.exp(gl - gc), qk * dc, jnp.exp(gl))
            for u, w, q, k, eg, gl, gc, qk, dc in zip(us, ws, qs, ks, egs, glogs, gcs, qks, decays)]
    return outs, ts


def _cum_matrix(upper):
    r, c = _iota2((CB, CB))
    return ((c >= r) if upper else (c <= r)).astype(F32)


def _lane_bcast(x, col):
    return jnp.broadcast_to(x[:, col:col + 1], x.shape)


_HEAD_SLICES = [slice(h * HD, (h + 1) * HD) for h in range(NH)]


def _dn1_fwd(q, k, v, gb):
    T = q.shape[0]
    nb = T // CB

    def body(q_ref, k_ref, v_ref, gb_ref, *out_refs):
        gbv = gb_ref[...]
        qs = [q_ref[:, sl] for sl in _HEAD_SLICES]
        ks = [k_ref[:, sl] for sl in _HEAD_SLICES]
        vs = [v_ref[:, sl] for sl in _HEAD_SLICES]
        kks = [_dot_nt_bf(x, x) for x in ks]
        qks = [_dot_nt_bf(x, y) for x, y in zip(qs, ks)]
        for d in (0, 1):
            u_ref, w_ref, qg_ref, kd_ref, qkd_ref, gl_ref, t_ref = out_refs[7 * d:7 * d + 7]
            gcum = _dot_h3(_cum_matrix(d == 1), gbv)
            betas = [_lane_bcast(gbv, d * NH + h) for h in range(NH)]
            gcs = [_lane_bcast(gcum, 16 + d * NH + h) for h in range(NH)]
            outs, ts = _dn1_heads(qs, ks, vs, betas, gcs, None, d == 1, kks, qks)
            for h, sl in enumerate(_HEAD_SLICES):
                u, w, qg, kd, qkd, gl = outs[h]
                u_ref[:, sl] = u
                w_ref[:, sl] = w.astype(BF)
                qg_ref[:, sl] = qg.astype(BF)
                kd_ref[:, sl] = kd.astype(BF)
                qkd_ref[:, sl] = qkd.astype(BF)
                gl_ref[h] = gl
                t_ref[:, sl] = ts[h]

    tb = pl.BlockSpec((CB, D), lambda i: (i, 0))
    one_dir_specs = [tb, tb, tb, tb, tb, pl.BlockSpec((NH, 1, 128), lambda i: (i, 0, 0)), tb]
    one_dir_shapes = ([jax.ShapeDtypeStruct((T, D), F32)] + [jax.ShapeDtypeStruct((T, D), BF)] * 4
                      + [jax.ShapeDtypeStruct((nb * NH, 1, 128), F32), jax.ShapeDtypeStruct((T, D), F32)])
    outs = pl.pallas_call(
        body, grid=(nb,), name="dn1_fwd",
        in_specs=[tb, tb, tb, pl.BlockSpec((CB, 128), lambda i: (i, 0))],
        out_specs=one_dir_specs * 2, out_shape=one_dir_shapes * 2, compiler_params=_cp(),
    )(q, k, v, gb)
    return [tuple(outs[:7]), tuple(outs[7:])]


def _dn1_bwd(q, k, v, gb, tinvs, cots):
    T = q.shape[0]
    nb = T // CB

    def body(q_ref, k_ref, v_ref, gb_ref, *refs):
        dir_refs, (dq_ref, dk_ref, dv_ref, dgb_ref) = refs[:14], refs[14:]
        gbv = gb_ref[...]
        qs = [q_ref[:, sl] for sl in _HEAD_SLICES]
        ks = [k_ref[:, sl] for sl in _HEAD_SLICES]
        vs = [v_ref[:, sl] for sl in _HEAD_SLICES]
        lane = lax.broadcasted_iota(jnp.int32, (CB, 128), 1)
        dgb = jnp.zeros((CB, 128), F32)
        for d in (0, 1):
            t_ref, du_ref, dw_ref, dqg_ref, dkd_ref, dqkd_ref, dgl_ref = dir_refs[7 * d:7 * d + 7]
            gcum = _dot_h3(_cum_matrix(d == 1), gbv)
            betas = [_lane_bcast(gbv, d * NH + h) for h in range(NH)]
            gcs = [_lane_bcast(gcum, 16 + d * NH + h) for h in range(NH)]
            ts = [t_ref[:, sl] for sl in _HEAD_SLICES]
            f = lambda qs, ks, vs, betas, gcs: _dn1_heads(qs, ks, vs, betas, gcs, ts, d == 1)[0]
            _, vjp = jax.vjp(f, qs, ks, vs, betas, gcs)
            cot = [(du_ref[:, sl], dw_ref[:, sl], dqg_ref[:, sl], dkd_ref[:, sl], dqkd_ref[:, sl], dgl_ref[h])
                   for h, sl in enumerate(_HEAD_SLICES)]
            dqs, dks, dvs, dbetas, dgcs = vjp(cot)
            dgcum = jnp.zeros((CB, 128), F32)
            for h, sl in enumerate(_HEAD_SLICES):
                if d == 0:
                    dq_ref[:, sl] = dqs[h]
                    dk_ref[:, sl] = dks[h]
                    dv_ref[:, sl] = dvs[h]
                else:
                    dq_ref[:, sl] += dqs[h]
                    dk_ref[:, sl] += dks[h]
                    dv_ref[:, sl] += dvs[h]
                dgb = dgb + jnp.where(lane == d * NH + h, jnp.sum(dbetas[h], axis=1, keepdims=True), 0.0)
                dgcum = dgcum + jnp.where(lane == 16 + d * NH + h, jnp.sum(dgcs[h], axis=1, keepdims=True), 0.0)
            dgb = dgb + _dot_h3(_cum_matrix(d == 0), dgcum)
        dgb_ref[...] = dgb

    tb = pl.BlockSpec((CB, D), lambda i: (i, 0))
    gbs = pl.BlockSpec((CB, 128), lambda i: (i, 0))
    gls = pl.BlockSpec((NH, 1, 128), lambda i: (i, 0, 0))
    args = []
    for d in (0, 1):
        args += [tinvs[d], *cots[d]]
    return pl.pallas_call(
        body, grid=(nb,), name="dn1_bwd",
        in_specs=[tb, tb, tb, gbs] + [tb, tb, tb, tb, tb, tb, gls] * 2, out_specs=[tb, tb, tb, gbs],
        out_shape=[jax.ShapeDtypeStruct((T, D), F32)] * 3 + [jax.ShapeDtypeStruct((T, 128), F32)],
        compiler_params=_cp(),
    )(q, k, v, gb, *args)


def _dn2_step(u, w, qg, kd, qkd, glrow, s):
    v_new = u - _dot_bf(w, s)
    o = _dot_bf(qg, s) + _dot_bf(qkd, v_new)
    return o, s * glrow + _dot_tn_bf(kd, v_new)


def _scan_order(direction, nlat_b, nall_b):
    if direction == 0:
        return lambda i: (i + nlat_b) % nall_b
    return lambda i: nall_b - 1 - i


def _dn2_fwd(per_dir, nlat):
    T = per_dir[0][0].shape[0]
    nb = T // CB
    blks = [_scan_order(d, nlat // CB, nb) for d in (0, 1)]

    def body(*refs):
        ins, outs, s_scr = refs[:12], refs[12:16], refs[16]

        @pl.when(pl.program_id(0) == 0)
        def _():
            s_scr[...] = jnp.zeros_like(s_scr)
        for d in (0, 1):
            outs[2 * d + 1][0] = s_scr[d]
        for h, sl in enumerate(_HEAD_SLICES):
            for d in (0, 1):
                u_ref, w_ref, qg_ref, kd_ref, qkd_ref, gl_ref = ins[6 * d:6 * d + 6]
                o, s_next = _dn2_step(u_ref[:, sl], w_ref[:, sl], qg_ref[:, sl], kd_ref[:, sl], qkd_ref[:, sl], gl_ref[h], s_scr[d, h])
                outs[2 * d][:, sl] = o
                s_scr[d, h] = s_next

    in_specs, out_specs, args = [], [], []
    for d in (0, 1):
        blk = blks[d]
        tb = pl.BlockSpec((CB, D), lambda i, blk=blk: (blk(i), 0))
        in_specs += [tb] * 5 + [pl.BlockSpec((NH, 1, 128), lambda i, blk=blk: (blk(i), 0, 0))]
        out_specs += [tb, pl.BlockSpec((1, NH, HD, HD), lambda i, blk=blk: (blk(i), 0, 0, 0))]
        args += list(per_dir[d])
    outs = pl.pallas_call(
        body, grid=(nb,), name="dn2_fwd", in_specs=in_specs, out_specs=out_specs,
        out_shape=[jax.ShapeDtypeStruct((T, D), F32), jax.ShapeDtypeStruct((nb, NH, HD, HD), F32)] * 2,
        scratch_shapes=[pltpu.VMEM((2, NH, HD, HD), F32)], compiler_params=_cp(),
    )(*args)
    return [tuple(outs[:2]), tuple(outs[2:])]


def _dn2_bwd(per_dir, do, nlat):
    T = per_dir[0][0].shape[0]
    nb = T // CB
    nlat_b = nlat // CB
    fwd = [_scan_order(d, nlat_b, nb) for d in (0, 1)]
    blks = [lambda i, f=f: f(nb - 1 - i) for f in fwd]

    def body(*refs):
        ins, outs, ds_scr = refs[:16], refs[16:28], refs[28]
        i = pl.program_id(0)

        @pl.when(i == 0)
        def _():
            ds_scr[...] = jnp.zeros_like(ds_scr)
        for h, sl in enumerate(_HEAD_SLICES):
            for d in (0, 1):
                u_ref, w_ref, qg_ref, kd_ref, qkd_ref, gl_ref, sall_ref, do_ref = ins[8 * d:8 * d + 8]
                du_ref, dw_ref, dqg_ref, dkd_ref, dqkd_ref, dgl_ref = outs[6 * d:6 * d + 6]
                args = (u_ref[:, sl], w_ref[:, sl].astype(F32), qg_ref[:, sl].astype(F32), kd_ref[:, sl].astype(F32),
                        qkd_ref[:, sl].astype(F32), gl_ref[h], sall_ref[0, h])
                _, vjp = jax.vjp(_dn2_step, *args)
                is_lat = blks[d](i) < nlat_b
                du, dw, dqg, dkd, dqkd, dgl, ds = vjp((jnp.where(is_lat, do_ref[:, sl], 0.0), ds_scr[d, h]))
                du_ref[:, sl] = du
                dw_ref[:, sl] = dw
                dqg_ref[:, sl] = dqg
                dkd_ref[:, sl] = dkd
                dqkd_ref[:, sl] = dqkd
                dgl_ref[h] = dgl
                ds_scr[d, h] = ds

    in_specs, out_specs, args = [], [], []
    for d in (0, 1):
        blk = blks[d]
        tb = pl.BlockSpec((CB, D), lambda i, blk=blk: (blk(i), 0))
        gls = pl.BlockSpec((NH, 1, 128), lambda i, blk=blk: (blk(i), 0, 0))
        in_specs += [tb] * 5 + [gls, pl.BlockSpec((1, NH, HD, HD), lambda i, blk=blk: (blk(i), 0, 0, 0)),
                                pl.BlockSpec((CB, D), lambda i, blk=blk: (jnp.minimum(blk(i), nlat_b - 1), 0))]
        out_specs += [tb] * 5 + [gls]
        args += list(per_dir[d]) + [do]
    outs = pl.pallas_call(
        body, grid=(nb,), name="dn2_bwd", in_specs=in_specs, out_specs=out_specs,
        out_shape=([jax.ShapeDtypeStruct((T, D), F32)] * 5 + [jax.ShapeDtypeStruct((nb * NH, 1, 128), F32)]) * 2,
        scratch_shapes=[pltpu.VMEM((2, NH, HD, HD), F32)], compiler_params=_cp(),
    )(*args)
    return [tuple(outs[:6]), tuple(outs[6:])]


def _ghn_fn(o, gt, w):
    y = o * lax.rsqrt(jnp.mean(o * o, axis=-1, keepdims=True) + EPS)
    return (y * w) * jax.nn.silu(gt)


def _ghn_fwd(o_f, o_b, p, w, nlat):
    tb = _tile(nlat, (256, 128))

    def body(of_ref, ob_ref, gt_ref, w_ref, y_ref):
        for h in range(NH):
            sl = slice(h * HD, (h + 1) * HD)
            y_ref[:, sl] = _ghn_fn(of_ref[:, sl] + ob_ref[:, sl], gt_ref[:, sl], w_ref[...]).astype(BF)

    row = pl.BlockSpec((tb, D), lambda i: (i, 0))
    return pl.pallas_call(
        body, grid=(nlat // tb,), name="ghn_fwd",
        in_specs=[row, row, pl.BlockSpec((tb, D), lambda i: (i, O_GT // D)), pl.BlockSpec((1, HD), lambda i: (0, 0))],
        out_specs=row, out_shape=jax.ShapeDtypeStruct((nlat, D), BF),
    )(o_f, o_b, p, w)


def _ghn_bwd(o_f, o_b, p, w, dy, nlat):
    tb = _tile(nlat, (256, 128))

    def body(of_ref, ob_ref, gt_ref, w_ref, dy_ref, do_ref, dgt_ref, dw_ref):
        @pl.when(pl.program_id(0) == 0)
        def _():
            dw_ref[...] = jnp.zeros_like(dw_ref)
        for h in range(NH):
            sl = slice(h * HD, (h + 1) * HD)
            _, vjp = jax.vjp(_ghn_fn, of_ref[:, sl] + ob_ref[:, sl], gt_ref[:, sl], w_ref[...])
            do, dgt, dw = vjp(dy_ref[:, sl])
            do_ref[:, sl] = do
            dgt_ref[:, sl] = dgt.astype(BF)
            dw_ref[...] += dw

    row = pl.BlockSpec((tb, D), lambda i: (i, 0))
    one = pl.BlockSpec((1, HD), lambda i: (0, 0))
    return pl.pallas_call(
        body, grid=(nlat // tb,), name="ghn_bwd",
        in_specs=[row, row, pl.BlockSpec((tb, D), lambda i: (i, O_GT // D)), one, row],
        out_specs=[row, row, one],
        out_shape=[jax.ShapeDtypeStruct((nlat, D), F32), jax.ShapeDtypeStruct((nlat, D), BF), jax.ShapeDtypeStruct((1, HD), F32)],
    )(o_f, o_b, p, w, dy)


@jax.custom_vjp
def _swap32(x):
    lane = lax.broadcasted_iota(jnp.int32, x.shape, 1)
    return jnp.where((lane & 32) == 0, pltpu.roll(x, 96, 1), pltpu.roll(x, 32, 1))


_swap32.defvjp(lambda x: (_swap32(x), None), lambda _, g: (_swap32(g),))


def _qk_post_fn(x, w, cos, sin):
    y = (x * lax.rsqrt(jnp.mean(x * x, axis=-1, keepdims=True) + EPS)) * w
    return y * cos + _swap32(y) * sin


def _attn_prep_fwd(p, qn, kn, cos, sin):
    T = p.shape[0]
    tb = _tile(T, (256, 128))

    def body(q_ref, k_ref, v_ref, qn_ref, kn_ref, cos_ref, sin_ref, qr_ref, kr_ref, vb_ref):
        cos_v, sin_v = cos_ref[...], sin_ref[...]
        for h in range(NH):
            sl = slice(h * HD, (h + 1) * HD)
            qr_ref[:, sl] = _qk_post_fn(q_ref[:, sl], qn_ref[...], cos_v, sin_v).astype(BF)
        for h in range(KVH):
            sl = slice(h * HD, (h + 1) * HD)
            kr_ref[:, sl] = _qk_post_fn(k_ref[:, sl], kn_ref[...], cos_v, sin_v).astype(BF)
        vb_ref[...] = v_ref[...].astype(BF)

    one = pl.BlockSpec((1, HD), lambda i: (0, 0))
    tab = pl.BlockSpec((tb, HD), lambda i: (i, 0))
    return pl.pallas_call(
        body, grid=(T // tb,), name="attn_prep_fwd",
        in_specs=[pl.BlockSpec((tb, D), lambda i: (i, O_Q // D)), pl.BlockSpec((tb, KV), lambda i: (i, O_K // KV)),
                  pl.BlockSpec((tb, KV), lambda i: (i, O_V // KV)), one, one, tab, tab],
        out_specs=[pl.BlockSpec((tb, D), lambda i: (i, 0)), pl.BlockSpec((tb, KV), lambda i: (i, 0)),
                   pl.BlockSpec((tb, KV), lambda i: (i, 0))],
        out_shape=[jax.ShapeDtypeStruct((T, D), BF), jax.ShapeDtypeStruct((T, KV), BF), jax.ShapeDtypeStruct((T, KV), BF)],
    )(p, p, p, qn, kn, cos, sin)


def _attn_prep_bwd(p, qn, kn, cos, sin, dqr, dkp, dvp, dkc, dvc, nlat):
    T = p.shape[0]
    nqb = nlat // CB
    ncb = (T - nlat) // CB

    def body(q_ref, k_ref, v_ref, qn_ref, kn_ref, cos_ref, sin_ref, dqr_ref, dka_ref, dkb_ref, dkc3_ref, dva_ref, dvb_ref, dvc3_ref,
             dkctx_ref, dvctx_ref, dq_ref, dk_ref, dv_ref, dqn_ref, dkn_ref):
        i = pl.program_id(0)
        is_lat = i < nqb
        cos_v, sin_v = cos_ref[...], sin_ref[...]

        @pl.when(i == 0)
        def _():
            dqn_ref[...] = jnp.zeros_like(dqn_ref)
            dkn_ref[...] = jnp.zeros_like(dkn_ref)

        def band_sum(a_ref, b_ref, c_ref, ctx_ref):
            s = b_ref[0] + jnp.where(i > 0, a_ref[0], 0.0) + jnp.where(i < nqb - 1, c_ref[0], 0.0)
            return jnp.where(is_lat, s, ctx_ref[...])

        dkr = band_sum(dka_ref, dkb_ref, dkc3_ref, dkctx_ref)
        dv_ref[...] = band_sum(dva_ref, dvb_ref, dvc3_ref, dvctx_ref).astype(BF)
        for h in range(NH):
            sl = slice(h * HD, (h + 1) * HD)
            _, vjp = jax.vjp(_qk_post_fn, q_ref[:, sl], qn_ref[...], cos_v, sin_v)
            dq, dqn, _, _ = vjp(jnp.where(is_lat, dqr_ref[:, sl], 0.0))
            dq_ref[:, sl] = dq.astype(BF)
            dqn_ref[...] += dqn
        for h in range(KVH):
            sl = slice(h * HD, (h + 1) * HD)
            _, vjp = jax.vjp(_qk_post_fn, k_ref[:, sl], kn_ref[...], cos_v, sin_v)
            dk, dkn, _, _ = vjp(dkr[:, sl])
            dk_ref[:, sl] = dk.astype(BF)
            dkn_ref[...] += dkn

    one = pl.BlockSpec((1, HD), lambda i: (0, 0))
    tab = pl.BlockSpec((CB, HD), lambda i: (i, 0))
    lat = lambda i: jnp.minimum(i, nqb - 1)

    def part(off, slot):
        return pl.BlockSpec((1, CB, KV), lambda i: (jnp.clip(lat(i) + off, 0, nqb - 1) * 3 + slot, 0, 0))

    ctxs = pl.BlockSpec((CB, KV), lambda i: (jnp.clip(i - nqb, 0, ncb - 1), 0))
    kvs = pl.BlockSpec((CB, KV), lambda i: (i, 0))
    return pl.pallas_call(
        body, grid=(T // CB,), name="attn_prep_bwd",
        in_specs=[pl.BlockSpec((CB, D), lambda i: (i, O_Q // D)), pl.BlockSpec((CB, KV), lambda i: (i, O_K // KV)),
                  pl.BlockSpec((CB, KV), lambda i: (i, O_V // KV)), one, one, tab, tab,
                  pl.BlockSpec((CB, D), lambda i: (lat(i), 0)),
                  part(-1, 2), part(0, 1), part(1, 0), part(-1, 2), part(0, 1), part(1, 0), ctxs, ctxs],
        out_specs=[pl.BlockSpec((CB, D), lambda i: (i, 0)), kvs, kvs, one, one],
        out_shape=[jax.ShapeDtypeStruct((T, D), BF), jax.ShapeDtypeStruct((T, KV), BF), jax.ShapeDtypeStruct((T, KV), BF),
                   jax.ShapeDtypeStruct((1, HD), F32), jax.ShapeDtypeStruct((1, HD), F32)],
    )(p, p, p, qn, kn, cos, sin, dqr, dkp, dkp, dkp, dvp, dvp, dvp, dkc, dvc)


def _attn_group_fn(q0, q1, q2, q3, kall, vall, s0, s1, s2, s3, bias):
    q = jnp.concatenate([q0, q1, q2, q3], axis=0)
    s = _dot_nt_bf(q, kall) * (HD ** -0.5) + bias
    sk = jnp.concatenate([jnp.broadcast_to(jnp.mean(t, axis=1, keepdims=True), (CB, 1)) for t in (s0, s1, s2, s3)], axis=0)
    m = lax.stop_gradient(jnp.maximum(jnp.max(s, axis=1, keepdims=True), sk))
    e = jnp.exp(s - m)
    den = jnp.sum(e, axis=1, keepdims=True) + jnp.exp(sk - m)
    return _dot_bf(e / den, vall)


def _attn_bias(lc):
    r, c = _iota2((GRP * CB, 3 * CB + lc))
    rel = c - (r & (CB - 1))
    win = (rel >= 0) & (rel <= 2 * CB)
    ctx = c >= 3 * CB
    seen = [(win & (c >= CB)) | ctx, win | ctx, (win & (c < 2 * CB)) | ctx]
    return jnp.stack([jnp.where(s, 0.0, -1e30) for s in seen]).astype(F32)


def _attn_specs(nqb, lc, nlat):
    assert nqb >= 2
    qs = pl.BlockSpec((CB, GRP * HD), lambda kh, i: (i, kh))
    ka = pl.BlockSpec((CB, HD), lambda kh, i: (jnp.maximum(i - 1, 0), kh))
    kb = pl.BlockSpec((CB, HD), lambda kh, i: (i, kh))
    kc = pl.BlockSpec((CB, HD), lambda kh, i: (jnp.minimum(i + 1, nqb - 1), kh))
    kx = pl.BlockSpec((lc, HD), lambda kh, i: (nlat // lc, kh))
    sk = pl.BlockSpec((1, 8, 128), lambda kh, i: (kh, 0, 0))
    bs = pl.BlockSpec((1, GRP * CB, 3 * CB + lc), lambda kh, i: (jnp.where(i == 0, 0, jnp.where(i == nqb - 1, 2, 1)), 0, 0))
    return qs, ka, kb, kc, kx, sk, bs


def _attn_fwd(qr, kr, vb, sink, nlat):
    lc = kr.shape[0] - nlat
    nqb = nlat // CB
    qs, ka, kb, kc, kx, sk, bs = _attn_specs(nqb, lc, nlat)

    def body(q_ref, ka_ref, kb_ref, kc_ref, kx_ref, va_ref, vb_ref, vc_ref, vx_ref, sk_ref, bias_ref, o_ref):
        kall = jnp.concatenate([ka_ref[...], kb_ref[...], kc_ref[...], kx_ref[...]], axis=0)
        vall = jnp.concatenate([va_ref[...], vb_ref[...], vc_ref[...], vx_ref[...]], axis=0)
        qh = [q_ref[:, g * HD:(g + 1) * HD] for g in range(GRP)]
        sinks = [sk_ref[0, g:g + 1, :] for g in range(GRP)]
        o = _attn_group_fn(*qh, kall, vall, *sinks, bias_ref[0])
        for g in range(GRP):
            o_ref[:, g * HD:(g + 1) * HD] = o[g * CB:(g + 1) * CB].astype(BF)

    return pl.pallas_call(
        body, grid=(KVH, nqb), name="attn_fwd",
        in_specs=[qs, ka, kb, kc, kx, ka, kb, kc, kx, sk, bs], out_specs=qs,
        out_shape=jax.ShapeDtypeStruct((nlat, D), BF), compiler_params=_cp(),
    )(qr, kr, kr, kr, kr, vb, vb, vb, vb, sink, _attn_bias(lc))


def _attn_bwd(qr, kr, vb, sink, dy, nlat):
    lc = kr.shape[0] - nlat
    nqb = nlat // CB
    qs, ka, kb, kc, kx, sk, bs = _attn_specs(nqb, lc, nlat)

    def body(q_ref, ka_ref, kb_ref, kc_ref, kx_ref, va_ref, vb_ref, vc_ref, vx_ref, sk_ref, dy_ref, bias_ref,
             dq_ref, dkp_ref, dvp_ref, dkx_ref, dvx_ref, dsk_ref):
        i = pl.program_id(1)
        kall = jnp.concatenate([ka_ref[...], kb_ref[...], kc_ref[...], kx_ref[...]], axis=0).astype(F32)
        vall = jnp.concatenate([va_ref[...], vb_ref[...], vc_ref[...], vx_ref[...]], axis=0).astype(F32)
        qh = [q_ref[:, g * HD:(g + 1) * HD].astype(F32) for g in range(GRP)]
        f = functools.partial(_attn_group_fn, bias=bias_ref[0])
        _, vjp = jax.vjp(f, *qh, kall, vall, *[sk_ref[0, g:g + 1, :] for g in range(GRP)])
        dyv = jnp.concatenate([dy_ref[:, g * HD:(g + 1) * HD] for g in range(GRP)], axis=0)
        d = vjp(dyv)
        for g in range(GRP):
            dq_ref[:, g * HD:(g + 1) * HD] = d[g]
        dk, dv = d[4], d[5]
        for t in range(3):
            dkp_ref[t] = dk[t * CB:(t + 1) * CB]
            dvp_ref[t] = dv[t * CB:(t + 1) * CB]

        @pl.when(i == 0)
        def _():
            dkx_ref[...] = jnp.zeros_like(dkx_ref)
            dvx_ref[...] = jnp.zeros_like(dvx_ref)
            dsk_ref[...] = jnp.zeros_like(dsk_ref)
        dkx_ref[...] += dk[3 * CB:]
        dvx_ref[...] += dv[3 * CB:]
        for g in range(GRP):
            dsk_ref[0, g:g + 1, :] += d[6 + g]

    dys = pl.BlockSpec((CB, GRP * HD), lambda kh, i: (i, kh))
    parts = pl.BlockSpec((3, CB, HD), lambda kh, i: (i, 0, kh))
    ctxo = pl.BlockSpec((lc, HD), lambda kh, i: (0, kh))
    return pl.pallas_call(
        body, grid=(KVH, nqb), name="attn_bwd",
        in_specs=[qs, ka, kb, kc, kx, ka, kb, kc, kx, sk, dys, bs],
        out_specs=[dys, parts, parts, ctxo, ctxo, sk],
        out_shape=[jax.ShapeDtypeStruct((nlat, D), F32), jax.ShapeDtypeStruct((3 * nqb, CB, KV), F32),
                   jax.ShapeDtypeStruct((3 * nqb, CB, KV), F32), jax.ShapeDtypeStruct((lc, KV), F32),
                   jax.ShapeDtypeStruct((lc, KV), F32), jax.ShapeDtypeStruct((KVH, 8, 128), F32)],
        compiler_params=_cp(),
    )(qr, kr, kr, kr, kr, vb, vb, vb, vb, sink, dy, _attn_bias(lc))


def _merge_fn(z_dn, z_at, g_dn, g_at):
    return jax.nn.sigmoid(g_dn) * z_dn + jax.nn.sigmoid(g_at) * z_at


def _merge_fwd(z_dn, z_at, p, nlat):
    tb = _tile(nlat, (256, 128))

    def body(zd_ref, za_ref, gd_ref, ga_ref, o_ref):
        o_ref[...] = _merge_fn(zd_ref[...], za_ref[...], gd_ref[...], ga_ref[...]).astype(BF)

    row = pl.BlockSpec((tb, D), lambda i: (i, 0))
    return pl.pallas_call(
        body, grid=(nlat // tb,), name="merge_fwd",
        in_specs=[row, row, pl.BlockSpec((tb, D), lambda i: (i, O_MG // D)), pl.BlockSpec((tb, D), lambda i: (i, O_MG // D + 1))],
        out_specs=row, out_shape=jax.ShapeDtypeStruct((nlat, D), BF),
    )(z_dn, z_at, p, p)


def _merge_bwd(z_dn, z_at, p, dm, nlat):
    tb = _tile(nlat, (256, 128))

    def body(zd_ref, za_ref, gd_ref, ga_ref, dm_ref, dzd_ref, dza_ref, dg_ref):
        _, vjp = jax.vjp(_merge_fn, zd_ref[...], za_ref[...], gd_ref[...], ga_ref[...])
        dzd, dza, dgd, dga = vjp(dm_ref[...])
        dzd_ref[...] = dzd.astype(BF)
        dza_ref[...] = dza.astype(BF)
        dg_ref[:, :D] = dgd.astype(BF)
        dg_ref[:, D:] = dga.astype(BF)

    row = pl.BlockSpec((tb, D), lambda i: (i, 0))
    return pl.pallas_call(
        body, grid=(nlat // tb,), name="merge_bwd",
        in_specs=[row, row, pl.BlockSpec((tb, D), lambda i: (i, O_MG // D)), pl.BlockSpec((tb, D), lambda i: (i, O_MG // D + 1)), row],
        out_specs=[row, row, pl.BlockSpec((tb, 2 * D), lambda i: (i, 0))],
        out_shape=[jax.ShapeDtypeStruct((nlat, D), BF), jax.ShapeDtypeStruct((nlat, D), BF), jax.ShapeDtypeStruct((nlat, 2 * D), BF)],
    )(z_dn, z_at, p, p, dm)


def _resid_fwd(x, gate, y):
    n = y.shape[0]
    tb = _tile(n, (256, 128))

    def body(x_ref, g_ref, y_ref, o_ref):
        o_ref[...] = x_ref[...] + g_ref[...] * y_ref[...]

    row = pl.BlockSpec((tb, D), lambda i: (i, 0))
    return pl.pallas_call(
        body, grid=(n // tb,), name="resid_fwd",
        in_specs=[row, pl.BlockSpec((1, D), lambda i: (0, 0)), row], out_specs=row,
        out_shape=jax.ShapeDtypeStruct((n, D), F32),
    )(x, gate, y)


def _resid_bwd(dx1a, dx1b, gate, y):
    n = y.shape[0]
    tb = _tile(n, (256, 128))

    def body(a_ref, b_ref, g_ref, y_ref, dx_ref, dy_ref, dg_ref):
        dx = a_ref[...] + b_ref[...]
        dx_ref[...] = dx
        dy_ref[...] = (g_ref[...] * dx).astype(BF)

        @pl.when(pl.program_id(0) == 0)
        def _():
            dg_ref[...] = jnp.zeros_like(dg_ref)
        dg_ref[...] += jnp.sum(dx * y_ref[...], axis=0, keepdims=True)

    row = pl.BlockSpec((tb, D), lambda i: (i, 0))
    one = pl.BlockSpec((1, D), lambda i: (0, 0))
    return pl.pallas_call(
        body, grid=(n // tb,), name="resid_bwd",
        in_specs=[row, row, one, row], out_specs=[row, row, one],
        out_shape=[jax.ShapeDtypeStruct((n, D), F32), jax.ShapeDtypeStruct((n, D), BF), jax.ShapeDtypeStruct((1, D), F32)],
    )(dx1a, dx1b, gate, y)


def _swiglu_fn(ug, uv):
    return jax.nn.silu(ug) * uv


def _swiglu_fwd(uc):
    n = uc.shape[0]
    tb = _tile(n, (128,))
    tc = DFF // 2

    def body(u_ref, o_ref):
        for j in range(2):
            o_ref[:, j * tc:(j + 1) * tc] = _swiglu_fn(u_ref[:, j * tc:(j + 1) * tc], u_ref[:, DFF + j * tc:DFF + (j + 1) * tc]).astype(BF)

    return pl.pallas_call(
        body, grid=(n // tb,), name="swiglu_fwd",
        in_specs=[pl.BlockSpec((tb, 2 * DFF), lambda i: (i, 0))],
        out_specs=pl.BlockSpec((tb, DFF), lambda i: (i, 0)), out_shape=jax.ShapeDtypeStruct((n, DFF), BF),
    )(uc)


def _swiglu_bwd(uc, da):
    n = uc.shape[0]
    tb = _tile(n, (128,))
    tc = DFF // 2

    def body(u_ref, da_ref, du_ref):
        for j in range(2):
            gs, vs = slice(j * tc, (j + 1) * tc), slice(DFF + j * tc, DFF + (j + 1) * tc)
            _, vjp = jax.vjp(_swiglu_fn, u_ref[:, gs], u_ref[:, vs])
            du_ref[:, gs], du_ref[:, vs] = vjp(da_ref[:, gs])

    full = pl.BlockSpec((tb, 2 * DFF), lambda i: (i, 0))
    return pl.pallas_call(
        body, grid=(n // tb,), name="swiglu_bwd", in_specs=[full, pl.BlockSpec((tb, DFF), lambda i: (i, 0))], out_specs=full,
        out_shape=jax.ShapeDtypeStruct((n, 2 * DFF), F32),
    )(uc, da)


def _loss_kernel(x1, gate, ff, target):
    n = x1.shape[0]
    tb = _tile(n, (256, 128))

    def body(x_ref, g_ref, f_ref, t_ref, loss_ref, dy_ref, dff_ref, dg_ref):
        err = x_ref[...] + g_ref[...] * f_ref[...] - t_ref[...]
        dy = err * (1.0 / D)
        dy_ref[...] = dy
        dff_ref[...] = (g_ref[...] * dy).astype(BF)

        @pl.when(pl.program_id(0) == 0)
        def _():
            loss_ref[...] = jnp.zeros_like(loss_ref)
            dg_ref[...] = jnp.zeros_like(dg_ref)
        part = 0.5 * jnp.sum(jnp.sum(err * err, axis=1, keepdims=True) * (1.0 / D), axis=0, keepdims=True)
        loss_ref[...] += jnp.broadcast_to(part, (1, 128))
        dg_ref[...] += jnp.sum(dy * f_ref[...], axis=0, keepdims=True)

    row = pl.BlockSpec((tb, D), lambda i: (i, 0))
    one = pl.BlockSpec((1, D), lambda i: (0, 0))
    return pl.pallas_call(
        body, grid=(n // tb,), name="loss",
        in_specs=[row, one, row, row], out_specs=[pl.BlockSpec((1, 128), lambda i: (0, 0)), row, row, one],
        out_shape=[jax.ShapeDtypeStruct((1, 128), F32), jax.ShapeDtypeStruct((n, D), F32),
                   jax.ShapeDtypeStruct((n, D), BF), jax.ShapeDtypeStruct((1, D), F32)],
    )(x1, gate, ff, target)


def _rope_tables(nlat, lc):
    t = jnp.arange(nlat)
    row = (t // GRID_W).astype(F32)
    col = (t % GRID_W).astype(F32)
    inv_freq = ROPE_BASE ** (-jnp.arange(32, dtype=F32) / 32)
    ar, ac = row[:, None] * inv_freq, col[:, None] * inv_freq
    cos = jnp.concatenate([jnp.cos(ar), jnp.cos(ar), jnp.cos(ac), jnp.cos(ac)], axis=1)
    sin = jnp.concatenate([-jnp.sin(ar), jnp.sin(ar), -jnp.sin(ac), jnp.sin(ac)], axis=1)
    cos = jnp.concatenate([cos, jnp.ones((lc, HD), F32)], axis=0)
    sin = jnp.concatenate([sin, jnp.zeros((lc, HD), F32)], axis=0)
    return cos, sin


def _pad_rows8(w):
    return jnp.concatenate([w, jnp.zeros((8 - w.shape[0], w.shape[1]), w.dtype)], axis=0)


def _pack_w_in(w):
    cuts = [sum(IN_SIZES[:i]) for i in range(len(IN_SIZES) + 1)]
    qkv, gt, b, a, q, k, v, mg = [w[:, cuts[i]:cuts[i + 1]] for i in range(len(IN_SIZES))]
    return jnp.concatenate([qkv, gt, q, mg, k, v, b, a, jnp.zeros((w.shape[0], PW - O_BA - 32), w.dtype)], axis=1)


def _unpack_w_in(g):
    return jnp.concatenate([g[:, O_QKV:O_GT], g[:, O_GT:O_Q], g[:, O_BA:O_BA + 32], g[:, O_Q:O_MG], g[:, O_K:O_V],
                            g[:, O_V:O_BA], g[:, O_MG:O_K]], axis=1)


def _local_step(x, ctx, mod_x, mod_c, target, w_in_p, w_bdn, w_bat, w_out, w_up, w_down,
                norm_mix, norm_ffn, dn_conv, a_log, dt_bias, dn_norm, q_norm, k_norm, sink, ffn_conv, ffn_conv_b):
    L, LC = x.shape[0], ctx.shape[0]
    T = L + LC
    xc = jnp.concatenate([x, ctx], axis=0)
    seg = lambda r: jnp.stack([mod_x[r], mod_c[r]])[:, None, :]
    sh_a, sc_a = seg(0), seg(1)
    g_a, g_f = mod_x[2][None], mod_x[5][None]
    sh_f, sc_f = mod_x[3][None, None], mod_x[4][None, None]
    sh_f2 = jnp.concatenate([sh_f, sh_f], axis=0)
    sc_f2 = jnp.concatenate([sc_f, sc_f], axis=0)
    cos, sin = _rope_tables(L, LC)
    dnc8 = _pad_rows8(dn_conv)
    ffc8 = _pad_rows8(ffn_conv)
    gate_row = lambda a: jnp.concatenate([jnp.zeros((1, 16), F32), a.reshape(1, 16), jnp.zeros((1, 96), F32)], axis=1)
    alog_row, dt_row = gate_row(a_log), gate_row(dt_bias)
    sinkb = jnp.concatenate([jnp.broadcast_to(sink.reshape(KVH, GRP, 1), (KVH, GRP, 128)), jnp.zeros((KVH, 8 - GRP, 128), F32)], axis=1)

    h1 = _norm_mod_fwd(xc, norm_mix, sh_a, sc_a, L, "norm_mix_fwd")
    p = _mm(h1, w_in_p, form="nn", out_dtype=F32, name="in_proj")
    conv = _conv_fwd(p, dnc8, jnp.zeros((1, 3 * D), F32), width=5, col0=0, ncols=3 * D, tc=512, seg_rows=(L, LC), name="dn_conv_fwd")
    q, k, v, gb = _dn_post_fwd(conv, p, alog_row, dt_row)
    wy = _dn1_fwd(q, k, v, gb)
    scans = _dn2_fwd([t[:6] for t in wy], L)
    o_dir = [s[0] for s in scans]
    y_dn = _ghn_fwd(o_dir[0], o_dir[1], p, dn_norm, L)
    qr, kr, vb = _attn_prep_fwd(p, q_norm, k_norm, cos, sin)
    y_at = _attn_fwd(qr, kr, vb, sinkb, L)
    z_dn = _mm(y_dn, w_bdn, form="nn", out_dtype=F32, name="branch_dn")
    z_at = _mm(y_at, w_bat, form="nn", out_dtype=F32, name="branch_at")
    merged = _merge_fwd(z_dn, z_at, p, L)
    mix = _mm(merged, w_out, form="nn", out_dtype=F32, name="out_proj")
    x1 = _resid_fwd(xc, g_a, mix)
    h2 = _norm_mod_fwd(x1, norm_ffn, sh_f2, sc_f2, L, "norm_ffn_fwd")
    u_raw = _mm(h2, w_up, form="nn", out_dtype=F32, name="ffn_up")
    uc = _conv_fwd(u_raw, ffc8, ffn_conv_b, width=3, col0=0, ncols=2 * DFF, tc=DFF // 2, seg_rows=(L,), name="ffn_conv_fwd")
    act = _swiglu_fwd(uc)
    ff = _mm(act, w_down, form="nn", out_dtype=F32, name="ffn_down")
    loss_row, dy, dff, dg_f = _loss_kernel(x1, g_f, ff, target)

    g_down = _mm(act, dff, form="tn", out_dtype=BF, name="g_ffn_down")
    dact = _mm(dff, w_down, form="nt", out_dtype=F32, name="d_act")
    duc = _swiglu_bwd(uc, dact)
    du_raw, g_ffc8, g_ffb = _conv_bwd(u_raw, duc, ffc8, width=3, col0=0, ncols=2 * DFF, tc=DFF // 2, seg_rows=(L,), name="ffn_conv_bwd")
    g_up = _mm(h2, du_raw, form="tn", out_dtype=BF, name="g_ffn_up")
    dh2 = _mm(du_raw, w_up, form="nt", out_dtype=F32, name="d_h2")
    dx1n, g_nffn, dsh_f, dsc_f = _norm_mod_bwd(x1, norm_ffn, sh_f2, sc_f2, dh2, L, "norm_ffn_bwd")
    dx1, dmix, dg_a = _resid_bwd(dy, dx1n, g_a, mix)

    g_out = _mm(merged, dmix, form="tn", out_dtype=BF, name="g_w_out")
    dmerged = _mm(dmix, w_out, form="nt", out_dtype=F32, name="d_merged")
    dz_dn, dz_at, dmg = _merge_bwd(z_dn, z_at, p, dmerged, L)
    g_bdn = _mm(y_dn, dz_dn, form="tn", out_dtype=BF, name="g_branch_dn")
    g_bat = _mm(y_at, dz_at, form="tn", out_dtype=BF, name="g_branch_at")
    dy_dn = _mm(dz_dn, w_bdn, form="nt", out_dtype=F32, name="d_y_dn")
    dy_at = _mm(dz_at, w_bat, form="nt", out_dtype=F32, name="d_y_at")
    dqr, dkp, dvp, dkx, dvx, dsink = _attn_bwd(qr, kr, vb, sinkb, dy_at, L)
    dq_raw, dk_raw, dv_raw, g_qn, g_kn = _attn_prep_bwd(p, q_norm, k_norm, cos, sin, dqr, dkp, dvp, dkx, dvx, L)
    do, dgt, g_dnn = _ghn_bwd(o_dir[0], o_dir[1], p, dn_norm, dy_dn, L)
    cots = _dn2_bwd([wy[d][:6] + (scans[d][1],) for d in (0, 1)], do, L)
    dq, dk, dv, dgb = _dn1_bwd(q, k, v, gb, [t[6] for t in wy], cots)
    dconv, dba, g_alog, g_dt = _dn_post_bwd(conv, p, alog_row, dt_row, dq, dk, dv, dgb)
    dqkv_raw, g_dnc8, _ = _conv_bwd(p, dconv, dnc8, width=5, col0=0, ncols=3 * D, tc=512, seg_rows=(L, LC), name="dn_conv_bwd")
    padc = lambda a: jnp.concatenate([a, jnp.zeros((LC, a.shape[1]), a.dtype)], axis=0)
    dp = jnp.concatenate([dqkv_raw, padc(dgt), dq_raw, padc(dmg), dk_raw, dv_raw, dba, jnp.zeros((T, PW - O_BA - 128), BF)], axis=1)
    g_in = _mm(h1, dp, form="tn", out_dtype=BF, name="g_w_in")
    dh1 = _mm(dp, w_in_p, form="nt", out_dtype=F32, name="d_h1")
    dxc, g_nmix, dsh_a, dsc_a = _norm_mod_bwd(xc, norm_mix, sh_a, sc_a, dh1, L, "norm_mix_bwd")
    grad_x = _add2(dx1, dxc)

    zero = jnp.zeros((D,), F32)
    dmod_x = jnp.stack([dsh_a[0, 0], dsc_a[0, 0], dg_a[0], dsh_f[0, 0], dsc_f[0, 0], dg_f[0]])
    dmod_c = jnp.stack([dsh_a[1, 0], dsc_a[1, 0], zero, zero, zero, zero])
    small = dict(
        dmod_x=dmod_x, dmod_c=dmod_c, norm_mix=g_nmix, norm_ffn=g_nffn, dn_conv=g_dnc8[:5], dn_a_log=g_alog[0, 16:32].reshape(2, 8),
        dn_dt_bias=g_dt[0, 16:32].reshape(2, 8), dn_norm=g_dnn, q_norm=g_qn, k_norm=g_kn,
        attn_sink=jnp.sum(dsink[:, :GRP, :], axis=2).reshape(1, NH), ffn_conv=g_ffc8[:3], ffn_conv_b=g_ffb)
    return loss_row[0, 0], grad_x, (g_in, g_bdn, g_bat, g_out, g_up, g_down), small


def _elementwise(fn, args, out_dtypes, name, rows=None):
    n = rows or args[0].shape[0]
    ncol = args[0].shape[1]
    tb = _tile(n, (256, 128, 8))
    nout = len(out_dtypes)

    def body(*refs):
        outs = fn(*[r[...] for r in refs[:len(args)]])
        for o_ref, o in zip(refs[len(args):], outs):
            o_ref[...] = o.astype(o_ref.dtype)

    spec = pl.BlockSpec((tb, ncol), lambda i: (i, 0))
    return pl.pallas_call(
        body, grid=(n // tb,), name=name, in_specs=[spec] * len(args), out_specs=[spec] * nout,
        out_shape=[jax.ShapeDtypeStruct((n, ncol), dt) for dt in out_dtypes],
    )(*args)


def _add2(a, b):
    return _elementwise(lambda x, y: (x + y,), [a, b], [F32], "add2", rows=a.shape[0])[0]


def _exchange(arrays, scatter, name):
    n = len(arrays)

    def body(*refs):
        ins, outs = refs[:n], refs[n:2 * n]
        send_sems, recv_sems, local_sems = refs[2 * n:]
        x, y, c = lax.axis_index("x"), lax.axis_index("y"), lax.axis_index("c")
        me = 4 * x + 2 * y + c
        started = []
        for k in range(n):
            local = pltpu.make_async_copy(ins[k].at[me] if scatter else ins[k], outs[k].at[me], local_sems.at[k])
            local.start()
            started.append(local)
        pending = []
        for k in range(n):
            for m in range(1, N_DEV):
                px = 1 - x if m & 4 else x
                py = 1 - y if m & 2 else y
                pc = 1 - c if m & 1 else c
                peer = 4 * px + 2 * py + pc
                src = ins[k].at[peer] if scatter else ins[k]
                sem = k * (N_DEV - 1) + m - 1
                push = pltpu.make_async_remote_copy(src_ref=src, dst_ref=outs[k].at[me], send_sem=send_sems.at[sem],
                                                    recv_sem=recv_sems.at[sem], device_id=(px, py, pc), device_id_type=MESH)
                push.start()
                landed = pltpu.make_async_remote_copy(src_ref=src, dst_ref=outs[k].at[peer], send_sem=send_sems.at[sem],
                                                      recv_sem=recv_sems.at[sem], device_id=(px, py, pc), device_id_type=MESH)
                pending.append((push, landed))
        for push, landed in pending:
            landed.wait_recv()
        for push, landed in pending:
            push.wait_send()
        for local in started:
            local.wait()

    hbm = pl.BlockSpec(memory_space=pl.ANY)
    out_shape = [jax.ShapeDtypeStruct(a.shape if scatter else (N_DEV,) + a.shape, a.dtype) for a in arrays]
    return pl.pallas_call(
        body, name=name, in_specs=[hbm] * n, out_specs=[hbm] * n, out_shape=out_shape,
        scratch_shapes=[pltpu.SemaphoreType.DMA((n * (N_DEV - 1),)), pltpu.SemaphoreType.DMA((n * (N_DEV - 1),)),
                        pltpu.SemaphoreType.DMA((n,))],
    )(*arrays)


def _ada_fwd(c16, w_ada, b_ada):
    def body(c_ref, w_ref, b_ref, o_ref):
        o_ref[...] = _dot_hi(jax.nn.silu(c_ref[...]), w_ref[...]) + b_ref[...]

    return pl.pallas_call(body, name="ada_fwd", out_shape=jax.ShapeDtypeStruct((16, w_ada.shape[1]), F32))(c16, w_ada, b_ada)


def _ada_bwd(c16, w_ada, dmx, dmc):
    def body(c_ref, w_ref, dmx_ref, dmc_ref, gw_ref, pc_ref):
        dmc_tot = dmc_ref[0:1, :]
        for d in range(1, N_DEV):
            dmc_tot = dmc_tot + dmc_ref[d:d + 1, :]
        dm16 = jnp.concatenate([dmx_ref[...], jnp.broadcast_to(dmc_tot, (8, dmc_tot.shape[1]))], axis=0)
        row = lax.broadcasted_iota(jnp.int32, dm16.shape, 0)
        dm16 = jnp.where(row <= 8, dm16, 0.0)
        s = jax.nn.silu(c_ref[...])
        gw_ref[...] = lax.dot_general(s, dm16, (_DIMS["tn"], ((), ())), precision=HI, preferred_element_type=F32)
        pc = lax.dot_general(dm16, w_ref[...], (_DIMS["nt"], ((), ())), precision=HI, preferred_element_type=F32)
        pc_ref[...] = pc[8:9, :]

    return pl.pallas_call(body, name="ada_bwd", out_shape=[jax.ShapeDtypeStruct(w_ada.shape, F32), jax.ShapeDtypeStruct((1, D), F32)],
                          compiler_params=_cp())(c16, w_ada, dmx, dmc)


def _cctx_grad(pc_all, c_ctx_row):
    def body(pc_ref, c_ref, g_ref):
        tot = pc_ref[0]
        for d in range(1, N_DEV):
            tot = tot + pc_ref[d]
        _, vjp = jax.vjp(jax.nn.silu, c_ref[...])
        g_ref[...] = vjp(tot)[0]

    return pl.pallas_call(body, name="cctx_grad", out_shape=jax.ShapeDtypeStruct((1, D), F32))(pc_all, c_ctx_row)


def _adamw(parts, w, m, v, name):
    ns, R, C = parts.shape
    tb = _tile(R, (128, 64, 32, 16, 8))

    def body(p_ref, w_ref, m_ref, v_ref, g_ref, d_ref, mo_ref, vo_ref):
        g = p_ref[0].astype(F32)
        for s in range(1, ns):
            g = g + p_ref[s].astype(F32)
        m2 = ADAM_B1 * m_ref[...] + (1.0 - ADAM_B1) * g
        v2 = ADAM_B2 * v_ref[...] + (1.0 - ADAM_B2) * jnp.square(g)
        m_hat = m2 / (1.0 - ADAM_B1 ** ADAM_STEP)
        v_hat = v2 / (1.0 - ADAM_B2 ** ADAM_STEP)
        g_ref[...] = g
        d_ref[...] = -ADAM_LR * (m_hat / (jnp.sqrt(v_hat) + ADAM_EPS) + ADAM_WD * w_ref[...])
        mo_ref[...] = m2
        vo_ref[...] = v2

    row = pl.BlockSpec((tb, C), lambda i: (i, 0))
    return pl.pallas_call(
        body, grid=(R // tb,), name=name,
        in_specs=[pl.BlockSpec((ns, tb, C), lambda i: (0, i, 0)), row, row, row], out_specs=[row] * 4,
        out_shape=[jax.ShapeDtypeStruct((R, C), F32)] * 4, compiler_params=_cp(),
    )(parts, w, m, v)


_SMALL = (("dmod_x", 6 * D), ("dmod_c", 6 * D), ("b_ada", 6 * D), ("norm_mix", D), ("norm_ffn", D), ("dn_a_log", 16),
          ("dn_dt_bias", 16), ("dn_norm", HD), ("q_norm", HD), ("k_norm", HD), ("attn_sink", NH), ("ffn_conv_b", 2 * DFF),
          ("dn_conv", 5 * 3 * D), ("ffn_conv", 3 * 2 * DFF))
_SMALL_ROWS = -(-sum(n for _, n in _SMALL) // 1024) * 8


def _pack_small(d):
    flat = jnp.concatenate([d[k].reshape(-1).astype(F32) if k in d else jnp.zeros((n,), F32) for k, n in _SMALL])
    return jnp.concatenate([flat, jnp.zeros((_SMALL_ROWS * 128 - flat.shape[0],), F32)]).reshape(_SMALL_ROWS, 128)


def _unpack_small(a):
    flat = a.reshape(a.shape[:-2] + (-1,))
    out, off = {}, 0
    for k, n in _SMALL:
        out[k] = flat[..., off:off + n]
        off += n
    return out


def kernel(x, c, ctx, c_ctx, w_ada, b_ada, norm_mix, norm_ffn, w_in, dn_conv, dn_a_log, dn_dt_bias, dn_norm, q_norm, k_norm, attn_sink, w_branch_dn, w_branch_attn, w_out, ffn_up, ffn_conv, ffn_conv_b, ffn_down, loss_target, m_c_ctx, m_w_ada, m_b_ada, m_norm_mix, m_norm_ffn, m_w_in, m_dn_conv, m_dn_a_log, m_dn_dt_bias, m_dn_norm, m_q_norm, m_k_norm, m_attn_sink, m_w_branch_dn, m_w_branch_attn, m_w_out, m_ffn_up, m_ffn_conv, m_ffn_conv_b, m_ffn_down, v_c_ctx, v_w_ada, v_b_ada, v_norm_mix, v_norm_ffn, v_w_in, v_dn_conv, v_dn_a_log, v_dn_dt_bias, v_dn_norm, v_q_norm, v_k_norm, v_attn_sink, v_w_branch_dn, v_w_branch_attn, v_w_out, v_ffn_up, v_ffn_conv, v_ffn_conv_b, v_ffn_down):
    me = 4 * lax.axis_index("x") + 2 * lax.axis_index("y") + lax.axis_index("c")
    ada_cols = w_ada.shape[2]

    gathered = _exchange([w_in[0].astype(BF), w_branch_dn[0].astype(BF), w_branch_attn[0].astype(BF), w_out[0].astype(BF),
                          ffn_up[0].astype(BF), ffn_down[0].astype(BF), c, dn_conv[0], ffn_conv[0]],
                         scatter=False, name="gather_weights")
    cols = lambda a: jnp.swapaxes(a, 0, 1).reshape(a.shape[1], -1)
    rows = lambda a: a.reshape(-1, a.shape[2])
    w_in_p = _pack_w_in(cols(gathered[0]))
    w_bdn, w_bat, w_o = rows(gathered[1]), rows(gathered[2]), rows(gathered[3])
    w_up, w_down = cols(gathered[4]), rows(gathered[5])
    c_all = gathered[6][:, 0, :]

    c16 = jnp.concatenate([c_all, c_ctx[None], jnp.zeros((7, D), F32)], axis=0)
    b_loc = lax.dynamic_slice_in_dim(b_ada, me * ada_cols, ada_cols, axis=1)
    mod_part = _ada_fwd(c16, w_ada[0], b_loc)
    mod_all = cols(_exchange([mod_part], scatter=False, name="gather_mod")[0])
    mod_x = lax.dynamic_slice_in_dim(mod_all, me, 1, axis=0).reshape(6, D)
    mod_c = mod_all[8].reshape(6, D)

    loss_loc, grad_x, big, small = _local_step(
        x[0], ctx[0], mod_x, mod_c, loss_target[0], w_in_p, w_bdn, w_bat, w_o, w_up, w_down,
        norm_mix, norm_ffn, cols(gathered[7]), dn_a_log[0], dn_dt_bias[0], dn_norm, q_norm, k_norm, attn_sink[0], cols(gathered[8]),
        ffn_conv_b)
    loss = lax.psum(loss_loc, ("x", "y", "c"))

    g_in, g_bdn, g_bat, g_out, g_up, g_down = big
    col_blocks = lambda g: jnp.swapaxes(g.reshape(g.shape[0], N_DEV, -1), 0, 1)
    row_blocks = lambda g: g.reshape(N_DEV, -1, g.shape[1])
    landed = _exchange([col_blocks(_unpack_w_in(g_in)), row_blocks(g_bdn), row_blocks(g_bat), row_blocks(g_out),
                        col_blocks(g_up), row_blocks(g_down)], scatter=True, name="scatter_grads")
    res = {}
    res["w_in"] = _adamw(landed[0], w_in[0], m_w_in[0], v_w_in[0], "adamw_w_in")
    res["w_branch_dn"] = _adamw(landed[1], w_branch_dn[0], m_w_branch_dn[0], v_w_branch_dn[0], "adamw_w_branch_dn")
    res["w_branch_attn"] = _adamw(landed[2], w_branch_attn[0], m_w_branch_attn[0], v_w_branch_attn[0], "adamw_w_branch_attn")
    res["w_out"] = _adamw(landed[3], w_out[0], m_w_out[0], v_w_out[0], "adamw_w_out")
    res["ffn_up"] = _adamw(landed[4], ffn_up[0], m_ffn_up[0], v_ffn_up[0], "adamw_ffn_up")
    res["ffn_down"] = _adamw(landed[5], ffn_down[0], m_ffn_down[0], v_ffn_down[0], "adamw_ffn_down")

    small = dict(small)
    small["b_ada"] = small["dmod_x"] + small["dmod_c"]
    parts = _exchange([_pack_small(small)], scatter=False, name="gather_small")[0]
    per_dev = _unpack_small(parts)
    given = dict(b_ada=(b_ada, m_b_ada, v_b_ada), norm_mix=(norm_mix, m_norm_mix, v_norm_mix), norm_ffn=(norm_ffn, m_norm_ffn, v_norm_ffn),
                 dn_a_log=(dn_a_log, m_dn_a_log, v_dn_a_log), dn_dt_bias=(dn_dt_bias, m_dn_dt_bias, v_dn_dt_bias),
                 dn_norm=(dn_norm, m_dn_norm, v_dn_norm), q_norm=(q_norm, m_q_norm, v_q_norm), k_norm=(k_norm, m_k_norm, v_k_norm),
                 attn_sink=(attn_sink, m_attn_sink, v_attn_sink), ffn_conv_b=(ffn_conv_b, m_ffn_conv_b, v_ffn_conv_b))
    packs = [_pack_small({k: t[j] for k, t in given.items()}) for j in range(3)]
    upd = [_unpack_small(a) for a in _adamw(parts, packs[0], packs[1], packs[2], "adamw_small")]
    for k, t in given.items():
        res[k] = tuple(u[k].reshape(t[0].shape) for u in upd)
    dnc = lax.dynamic_slice_in_dim(upd[0]["dn_conv"].reshape(5, 3 * D), me * dn_conv.shape[2], dn_conv.shape[2], axis=1)
    ffc = lax.dynamic_slice_in_dim(upd[0]["ffn_conv"].reshape(3, 2 * DFF), me * ffn_conv.shape[2], ffn_conv.shape[2], axis=1)
    r8 = lambda a: _pad_rows8(a)
    t = _adamw(r8(dnc)[None], r8(dn_conv[0]), r8(m_dn_conv[0]), r8(v_dn_conv[0]), "adamw_dn_conv")
    res["dn_conv"] = tuple(a[:5][None] for a in t)
    t = _adamw(r8(ffc)[None], r8(ffn_conv[0]), r8(m_ffn_conv[0]), r8(v_ffn_conv[0]), "adamw_ffn_conv")
    res["ffn_conv"] = tuple(a[:3][None] for a in t)

    dmx = lax.dynamic_slice_in_dim(per_dev["dmod_x"], me * ada_cols, ada_cols, axis=1)
    dmc = lax.dynamic_slice_in_dim(per_dev["dmod_c"], me * ada_cols, ada_cols, axis=1)
    g_ada, pc = _ada_bwd(c16, w_ada[0], dmx, dmc)
    res["w_ada"] = _adamw(g_ada[None], w_ada[0], m_w_ada[0], v_w_ada[0], "adamw_w_ada")
    pc_all = _exchange([pc], scatter=False, name="gather_cctx")[0]
    g_cctx = _cctx_grad(pc_all, c_ctx[None])
    r8b = lambda a: jnp.broadcast_to(a, (8, D))
    t = _adamw(r8b(g_cctx)[None], r8b(c_ctx[None]), r8b(m_c_ctx[None]), r8b(v_c_ctx[None]), "adamw_c_ctx")
    res["c_ctx"] = tuple(a[0] for a in t)

    names = ("c_ctx", "w_ada", "b_ada", "norm_mix", "norm_ffn", "w_in", "dn_conv", "dn_a_log", "dn_dt_bias", "dn_norm", "q_norm",
             "k_norm", "attn_sink", "w_branch_dn", "w_branch_attn", "w_out", "ffn_up", "ffn_conv", "ffn_conv_b", "ffn_down")
    lead = ("w_ada", "w_in", "w_branch_dn", "w_branch_attn", "w_out", "ffn_up", "ffn_down")
    fix = lambda k, a: a[None] if k in lead else a
    outs = [loss, grad_x[None]]
    for j in range(4):
        outs += [fix(k, res[k][j]) for k in names]
    return tuple(outs)
```

```python
import functools

import jax
import jax.numpy as jnp
from jax import lax
from jax.experimental import pallas as pl
from jax.experimental.pallas import tpu as pltpu

F32 = jnp.float32
BF = jnp.bfloat16
HI = lax.Precision.HIGHEST
MESH = pl.DeviceIdType.MESH

D = 1024
NH = 8
HD = 128
KVH = 2
GRP = 4
KV = KVH * HD
DFF = 2816
CB = 128
GRID_W = 64
ROPE_BASE = 10000.0
EPS = 1e-6
N_DEV = 8
PW = 8192
O_QKV, O_GT, O_Q, O_MG, O_K, O_V, O_BA = 0, 3072, 4096, 5120, 7168, 7424, 7680
IN_SIZES = (3072, 1024, 16, 16, 1024, 256, 256, 2048)
IN_DIM = sum(IN_SIZES)
ADAM_LR, ADAM_B1, ADAM_B2, ADAM_EPS, ADAM_WD, ADAM_STEP = 0.001, 0.9, 0.999, 1e-08, 0.01, 10
VMEM_LIMIT = 56 * 1024 * 1024


def _cp():
    return pltpu.CompilerParams(vmem_limit_bytes=VMEM_LIMIT)


def _tile(n, cands):
    for c in cands:
        if n % c == 0:
            return c
    return n


def _iota2(shape):
    return lax.broadcasted_iota(jnp.int32, shape, 0), lax.broadcasted_iota(jnp.int32, shape, 1)


_DIMS = {"nn": ((1,), (0,)), "nt": ((1,), (1,)), "tn": ((0,), (0,))}


def _mm(a, b, *, form, out_dtype, name, tm=None, tn=None, tk=None):
    if form == "tn":
        K, M = a.shape
        N = b.shape[1]
    else:
        M, K = a.shape
        N = b.shape[0] if form == "nt" else b.shape[1]
    tm = tm or _tile(M, (1024, 640, 512, 256, 128))
    tn = tn or _tile(N, (1408, 1024, 512, 256, 128))
    tk = tk or _tile(K, (2048, 1408, 1280, 1024, 640, 512, 256, 128))
    nk = K // tk
    dims = (_DIMS[form], ((), ()))

    def body(a_ref, b_ref, o_ref, *acc):
        k = pl.program_id(2)
        part = lax.dot_general(a_ref[...].astype(BF), b_ref[...].astype(BF), dims, preferred_element_type=F32)
        if nk == 1:
            o_ref[...] = part.astype(out_dtype)
        else:
            acc_ref = acc[0]

            @pl.when(k == 0)
            def _():
                acc_ref[...] = part

            @pl.when(k > 0)
            def _():
                acc_ref[...] += part

            @pl.when(k == nk - 1)
            def _():
                o_ref[...] = acc_ref[...].astype(out_dtype)

    if form == "tn":
        a_spec = pl.BlockSpec((tk, tm), lambda i, j, k: (k, i))
    else:
        a_spec = pl.BlockSpec((tm, tk), lambda i, j, k: (i, k))
    if form == "nt":
        b_spec = pl.BlockSpec((tn, tk), lambda i, j, k: (j, k))
    else:
        b_spec = pl.BlockSpec((tk, tn), lambda i, j, k: (k, j))
    return pl.pallas_call(
        body, grid=(M // tm, N // tn, nk), name=name,
        in_specs=[a_spec, b_spec], out_specs=pl.BlockSpec((tm, tn), lambda i, j, k: (i, j)),
        out_shape=jax.ShapeDtypeStruct((M, N), out_dtype),
        scratch_shapes=[] if nk == 1 else [pltpu.VMEM((tm, tn), F32)],
        compiler_params=_cp(),
    )(a, b)


def _norm_mod_fn(x, nw, sh, sc):
    y = x * lax.rsqrt(jnp.mean(x * x, axis=-1, keepdims=True) + EPS)
    return (y * nw) * (1.0 + sc) + sh


def _norm_mod_fwd(x, nw, sh, sc, nlat, name):
    T = x.shape[0]
    tb = _tile(T, (256, 128))
    nlb = nlat // tb

    def body(x_ref, nw_ref, sh_ref, sc_ref, h_ref):
        h_ref[...] = _norm_mod_fn(x_ref[...], nw_ref[...], sh_ref[0], sc_ref[0]).astype(BF)

    seg = pl.BlockSpec((1, 1, D), lambda i: (jnp.where(i >= nlb, 1, 0), 0, 0))
    return pl.pallas_call(
        body, grid=(T // tb,), name=name,
        in_specs=[pl.BlockSpec((tb, D), lambda i: (i, 0)), pl.BlockSpec((1, D), lambda i: (0, 0)), seg, seg],
        out_specs=pl.BlockSpec((tb, D), lambda i: (i, 0)),
        out_shape=jax.ShapeDtypeStruct((T, D), BF),
    )(x, nw, sh, sc)


def _norm_mod_bwd(x, nw, sh, sc, dh, nlat, name):
    T = x.shape[0]
    tb = _tile(T, (256, 128))
    nlb = nlat // tb

    def body(x_ref, nw_ref, sh_ref, sc_ref, dh_ref, dx_ref, dnw_ref, dsh_ref, dsc_ref):
        i = pl.program_id(0)
        _, vjp = jax.vjp(_norm_mod_fn, x_ref[...], nw_ref[...], sh_ref[0], sc_ref[0])
        dx, dnw, dsh, dsc = vjp(dh_ref[...])
        dx_ref[...] = dx

        @pl.when(i == 0)
        def _():
            dnw_ref[...] = jnp.zeros_like(dnw_ref)

        @pl.when((i == 0) | (i == nlb))
        def _():
            dsh_ref[...] = jnp.zeros_like(dsh_ref)
            dsc_ref[...] = jnp.zeros_like(dsc_ref)

        dnw_ref[...] += dnw
        dsh_ref[0] += dsh
        dsc_ref[0] += dsc

    seg = pl.BlockSpec((1, 1, D), lambda i: (jnp.where(i >= nlb, 1, 0), 0, 0))
    row = pl.BlockSpec((tb, D), lambda i: (i, 0))
    one = pl.BlockSpec((1, D), lambda i: (0, 0))
    return pl.pallas_call(
        body, grid=(T // tb,), name=name,
        in_specs=[row, one, seg, seg, row], out_specs=[row, one, seg, seg],
        out_shape=[jax.ShapeDtypeStruct((T, D), F32), jax.ShapeDtypeStruct((1, D), F32),
                   jax.ShapeDtypeStruct((2, 1, D), F32), jax.ShapeDtypeStruct((2, 1, D), F32)],
    )(x, nw, sh, sc, dh)


HALO = 8


def _halo_specs(tb, width, nrows, col=0):
    r8 = tb // HALO
    cur = pl.BlockSpec((tb, width), lambda i: (i, col))
    prev = pl.BlockSpec((HALO, width), lambda i: (jnp.maximum(i * r8 - 1, 0), col))
    nxt = pl.BlockSpec((HALO, width), lambda i: (jnp.minimum((i + 1) * r8, nrows // HALO - 1), col))
    return [cur, prev, nxt]


def _segment_edges(seg_rows, tb):
    bounds = [0]
    for s in seg_rows:
        bounds.append(bounds[-1] + s // tb)
    return bounds[:-1], [b - 1 for b in bounds[1:]]


def _keep_halos(i, starts, ends):
    keep_p = functools.reduce(lambda a, b: a & b, [i != s for s in starts])
    keep_n = functools.reduce(lambda a, b: a & b, [i != e for e in ends])
    return keep_p, keep_n


def _ext_rows(refs, cols, keep):
    cur_ref, prev_ref, next_ref = refs
    p = jnp.where(keep[0], prev_ref[:, cols].astype(F32), 0.0)
    n = jnp.where(keep[1], next_ref[:, cols].astype(F32), 0.0)
    return jnp.concatenate([p, cur_ref[:, cols].astype(F32), n], axis=0)


def _conv_rows(xe, w_ref, cols, width, transpose=False):
    r = width // 2
    n = xe.shape[0]
    acc = None
    for j in range(width):
        s = ((j - r) if transpose else (r - j)) % n
        term = (xe if s == 0 else pltpu.roll(xe, s, 0)) * w_ref[j:j + 1, cols]
        acc = term if acc is None else acc + term
    return acc


def _tap_grads(dcur, xe, width, tb):
    r = width // 2
    n = xe.shape[0]
    out = []
    for j in range(width):
        s = (r - j) % n
        xs = (xe if s == 0 else pltpu.roll(xe, s, 0))[HALO:HALO + tb]
        out.append(jnp.sum(dcur * xs, axis=0, keepdims=True))
    return out


def _softplus(x):
    return jnp.maximum(x, 0.0) + jnp.log(1.0 + jnp.exp(-jnp.abs(x)))


def _gates_fn(ba, alog_row, dt_row):
    col = lax.broadcasted_iota(jnp.int32, ba.shape, 1)
    beta = jax.nn.sigmoid(ba)
    g = -jnp.exp(alog_row) * _softplus(ba + dt_row)
    return jnp.where(col < 16, beta, jnp.where(col < 32, g, 0.0))


def _qkv_post_fn(c, kind):
    y = jax.nn.silu(c)
    if kind == 2:
        return y
    n = y * lax.rsqrt(jnp.sum(y * y, axis=-1, keepdims=True) + EPS)
    return n * (HD ** -0.5) if kind == 0 else n


DN_TAPS = 5
FFN_TAPS = 3


def _dn_pre_fwd(p, w8, alog_row, dt_row, seg_rows):
    T = p.shape[0]
    tb = _tile(T, (256, 128))
    starts, ends = _segment_edges(seg_rows, tb)

    def body(cur_ref, prev_ref, next_ref, ba_ref, w_ref, al_ref, dt_ref, q_ref, k_ref, v_ref, gb_ref):
        keep = _keep_halos(pl.program_id(0), starts, ends)
        outs = (q_ref, k_ref, v_ref)
        for kind in range(3):
            for h in range(NH):
                cols = slice(kind * D + h * HD, kind * D + (h + 1) * HD)
                xe = _ext_rows((cur_ref, prev_ref, next_ref), cols, keep)
                conv = _conv_rows(xe, w_ref, cols, DN_TAPS)[HALO:HALO + tb]
                outs[kind][:, h * HD:(h + 1) * HD] = _qkv_post_fn(conv, kind)
        gb_ref[...] = _gates_fn(ba_ref[...], al_ref[...], dt_ref[...])

    row = pl.BlockSpec((tb, D), lambda i: (i, 0))
    one = pl.BlockSpec((1, 128), lambda i: (0, 0))
    return pl.pallas_call(
        body, grid=(T // tb,), name="dn_pre_fwd",
        in_specs=_halo_specs(tb, 3 * D, T) + [pl.BlockSpec((tb, 128), lambda i: (i, O_BA // 128)),
                                              pl.BlockSpec((8, 3 * D), lambda i: (0, 0)), one, one],
        out_specs=[row, row, row, pl.BlockSpec((tb, 128), lambda i: (i, 0))],
        out_shape=[jax.ShapeDtypeStruct((T, D), F32)] * 3 + [jax.ShapeDtypeStruct((T, 128), F32)],
        compiler_params=_cp(),
    )(p, p, p, p, w8, alog_row, dt_row)


def _dn_pre_bwd(p, w8, alog_row, dt_row, dq, dk, dv, dgb, seg_rows):
    T = p.shape[0]
    tb = _tile(T, (256, 128))
    starts, ends = _segment_edges(seg_rows, tb)

    def body(cur_ref, prev_ref, next_ref, ba_ref, w_ref, al_ref, dt_ref,
             dq_c, dq_p, dq_n, dk_c, dk_p, dk_n, dv_c, dv_p, dv_n, dgb_ref, dx_ref, dba_ref, dw_ref, dal_ref, ddt_ref):
        i = pl.program_id(0)
        keep = _keep_halos(i, starts, ends)

        @pl.when(i == 0)
        def _():
            dw_ref[...] = jnp.zeros_like(dw_ref)
            dal_ref[...] = jnp.zeros_like(dal_ref)
            ddt_ref[...] = jnp.zeros_like(ddt_ref)

        douts = ((dq_c, dq_p, dq_n), (dk_c, dk_p, dk_n), (dv_c, dv_p, dv_n))
        for kind in range(3):
            for h in range(NH):
                cols = slice(kind * D + h * HD, kind * D + (h + 1) * HD)
                xe = _ext_rows((cur_ref, prev_ref, next_ref), cols, keep)
                conv = _conv_rows(xe, w_ref, cols, DN_TAPS)
                dye = _ext_rows(douts[kind], slice(h * HD, (h + 1) * HD), keep)
                _, vjp = jax.vjp(functools.partial(_qkv_post_fn, kind=kind), conv)
                dce = vjp(dye)[0]
                dx_ref[:, cols] = _conv_rows(dce, w_ref, cols, DN_TAPS, transpose=True)[HALO:HALO + tb].astype(BF)
                for j, g in enumerate(_tap_grads(dce[HALO:HALO + tb], xe, DN_TAPS, tb)):
                    dw_ref[j:j + 1, cols] += g
        _, vjp = jax.vjp(_gates_fn, ba_ref[...], al_ref[...], dt_ref[...])
        dba, dal, ddt = vjp(dgb_ref[...])
        dba_ref[...] = dba.astype(BF)
        dal_ref[...] += dal
        ddt_ref[...] += ddt

    one = pl.BlockSpec((1, 128), lambda i: (0, 0))
    nar = pl.BlockSpec((tb, 128), lambda i: (i, 0))
    wspec = pl.BlockSpec((8, 3 * D), lambda i: (0, 0))
    return pl.pallas_call(
        body, grid=(T // tb,), name="dn_pre_bwd",
        in_specs=_halo_specs(tb, 3 * D, T) + [pl.BlockSpec((tb, 128), lambda i: (i, O_BA // 128)), wspec, one, one]
        + _halo_specs(tb, D, T) * 3 + [nar],
        out_specs=[pl.BlockSpec((tb, 3 * D), lambda i: (i, 0)), nar, wspec, one, one],
        out_shape=[jax.ShapeDtypeStruct((T, 3 * D), BF), jax.ShapeDtypeStruct((T, 128), BF), jax.ShapeDtypeStruct((8, 3 * D), F32),
                   jax.ShapeDtypeStruct((1, 128), F32), jax.ShapeDtypeStruct((1, 128), F32)],
        compiler_params=_cp(),
    )(p, p, p, p, w8, alog_row, dt_row, dq, dq, dq, dk, dk, dk, dv, dv, dv, dgb)


def _dot_hi(a, b):
    return jnp.dot(a, b, precision=HI, preferred_element_type=F32)


def _dot_bf(a, b):
    return jnp.dot(a.astype(BF), b.astype(BF), preferred_element_type=F32)


def _dot_nt_bf(a, b):
    return lax.dot_general(a.astype(BF), b.astype(BF), (_DIMS["nt"], ((), ())), preferred_element_type=F32)


def _dot_tn_bf(a, b):
    return lax.dot_general(a.astype(BF), b.astype(BF), (_DIMS["tn"], ((), ())), preferred_element_type=F32)


def _dot_h3(a, b):
    return jnp.dot(a, b, precision=lax.Precision.HIGH, preferred_element_type=F32)


def _unit_tri_inverses(mats):
    r, c = _iota2((CB, CB))
    eye = (r == c).astype(F32)
    a8 = [jnp.where((r // 8) == (c // 8), a, 0.0) for a in mats]
    a2 = [_dot_h3(x, x) for x in a8]
    a4 = [_dot_h3(x, x) for x in a2]
    t = [_dot_h3(eye - x, eye + y) for x, y in zip(a8, a2)]
    t = [_dot_h3(x, eye + y) for x, y in zip(t, a4)]
    b = 8
    while b < CB:
        mask = ((r // (2 * b)) == (c // (2 * b))) & ((r // b) != (c // b))
        te = [_dot_h3(x, jnp.where(mask, a, 0.0)) for x, a in zip(t, mats)]
        t = [x - _dot_h3(y, x) for x, y in zip(t, te)]
        b *= 2
    return t


@jax.custom_vjp
def _saved_inverse(a, t):
    return t


_saved_inverse.defvjp(lambda a, t: (t, t), lambda t, dt: (-_dot_h3(_dot_h3(t.T, dt), t.T), jnp.zeros_like(t)))


def _dn1_decay(gc, reverse):
    r, c = _iota2((CB, CB))
    incl = (c >= r) if reverse else (c <= r)
    return jnp.where(incl, jnp.exp(jnp.where(incl, gc - gc.T, 0.0)), 0.0)


def _dn1_heads(qs, ks, vs, betas, gcs, ts_saved, reverse, kks=None, qks=None):
    r, c = _iota2((CB, CB))
    strict = (c > r) if reverse else (c < r)
    decays = [_dn1_decay(gc, reverse) for gc in gcs]
    kks = kks or [_dot_nt_bf(k, k) for k in ks]
    systems = [jnp.where(strict, b * kk * dc, 0.0) for b, kk, dc in zip(betas, kks, decays)]
    if ts_saved is None:
        ts = _unit_tri_inverses(systems)
    else:
        ts = [_saved_inverse(a, t) for a, t in zip(systems, ts_saved)]
    egs = [jnp.exp(gc) for gc in gcs]
    us = [_dot_h3(t, v * b) for t, v, b in zip(ts, vs, betas)]
    ws = [_dot_h3(t, k * (b * eg)) for t, k, b, eg in zip(ts, ks, betas, egs)]
    qks = qks or [_dot_nt_bf(q, k) for q, k in zip(qs, ks)]
    last = 0 if reverse else CB - 1
    glogs = [jnp.sum(jnp.where(r == last, gc, 0.0), axis=0, keepdims=True) for gc in gcs]
    outs = [(u, w, q * eg, k * jnp.exp(gl - gc), qk * dc, jnp.exp(gl))
            for u, w, q, k, eg, gl, gc, qk, dc in zip(us, ws, qs, ks, egs, glogs, gcs, qks, decays)]
    return outs, ts


def _cum_matrix(upper):
    r, c = _iota2((CB, CB))
    return ((c >= r) if upper else (c <= r)).astype(F32)


def _lane_bcast(x, col):
    return jnp.broadcast_to(x[:, col:col + 1], x.shape)


_HEAD_SLICES = [slice(h * HD, (h + 1) * HD) for h in range(NH)]


def _dn1_fwd(q, k, v, gb):
    T = q.shape[0]
    nb = T // CB

    def body(q_ref, k_ref, v_ref, gb_ref, *out_refs):
        gbv = gb_ref[...]
        qs = [q_ref[:, sl] for sl in _HEAD_SLICES]
        ks = [k_ref[:, sl] for sl in _HEAD_SLICES]
        vs = [v_ref[:, sl] for sl in _HEAD_SLICES]
        kks = [_dot_nt_bf(x, x) for x in ks]
        qks = [_dot_nt_bf(x, y) for x, y in zip(qs, ks)]
        for d in (0, 1):
            u_ref, w_ref, qg_ref, kd_ref, qkd_ref, gl_ref, t_ref = out_refs[7 * d:7 * d + 7]
            gcum = _dot_h3(_cum_matrix(d == 1), gbv)
            betas = [_lane_bcast(gbv, d * NH + h) for h in range(NH)]
            gcs = [_lane_bcast(gcum, 16 + d * NH + h) for h in range(NH)]
            outs, ts = _dn1_heads(qs, ks, vs, betas, gcs, None, d == 1, kks, qks)
            for h, sl in enumerate(_HEAD_SLICES):
                u, w, qg, kd, qkd, gl = outs[h]
                u_ref[:, sl] = u
                w_ref[:, sl] = w.astype(BF)
                qg_ref[:, sl] = qg.astype(BF)
                kd_ref[:, sl] = kd.astype(BF)
                qkd_ref[:, sl] = qkd.astype(BF)
                gl_ref[h] = gl
                t_ref[:, sl] = ts[h]

    tb = pl.BlockSpec((CB, D), lambda i: (i, 0))
    one_dir_specs = [tb, tb, tb, tb, tb, pl.BlockSpec((NH, 1, 128), lambda i: (i, 0, 0)), tb]
    one_dir_shapes = ([jax.ShapeDtypeStruct((T, D), F32)] + [jax.ShapeDtypeStruct((T, D), BF)] * 4
                      + [jax.ShapeDtypeStruct((nb * NH, 1, 128), F32), jax.ShapeDtypeStruct((T, D), F32)])
    outs = pl.pallas_call(
        body, grid=(nb,), name="dn1_fwd",
        in_specs=[tb, tb, tb, pl.BlockSpec((CB, 128), lambda i: (i, 0))],
        out_specs=one_dir_specs * 2, out_shape=one_dir_shapes * 2, compiler_params=_cp(),
    )(q, k, v, gb)
    return [tuple(outs[:7]), tuple(outs[7:])]


def _dn1_bwd(q, k, v, gb, tinvs, cots):
    T = q.shape[0]
    nb = T // CB

    def body(q_ref, k_ref, v_ref, gb_ref, *refs):
        dir_refs, (dq_ref, dk_ref, dv_ref, dgb_ref) = refs[:14], refs[14:]
        gbv = gb_ref[...]
        qs = [q_ref[:, sl] for sl in _HEAD_SLICES]
        ks = [k_ref[:, sl] for sl in _HEAD_SLICES]
        vs = [v_ref[:, sl] for sl in _HEAD_SLICES]
        lane = lax.broadcasted_iota(jnp.int32, (CB, 128), 1)
        dgb = jnp.zeros((CB, 128), F32)
        for d in (0, 1):
            t_ref, du_ref, dw_ref, dqg_ref, dkd_ref, dqkd_ref, dgl_ref = dir_refs[7 * d:7 * d + 7]
            gcum = _dot_h3(_cum_matrix(d == 1), gbv)
            betas = [_lane_bcast(gbv, d * NH + h) for h in range(NH)]
            gcs = [_lane_bcast(gcum, 16 + d * NH + h) for h in range(NH)]
            ts = [t_ref[:, sl] for sl in _HEAD_SLICES]
            f = lambda qs, ks, vs, betas, gcs: _dn1_heads(qs, ks, vs, betas, gcs, ts, d == 1)[0]
            _, vjp = jax.vjp(f, qs, ks, vs, betas, gcs)
            cot = [(du_ref[:, sl], dw_ref[:, sl], dqg_ref[:, sl], dkd_ref[:, sl], dqkd_ref[:, sl], dgl_ref[h])
                   for h, sl in enumerate(_HEAD_SLICES)]
            dqs, dks, dvs, dbetas, dgcs = vjp(cot)
            dgcum = jnp.zeros((CB, 128), F32)
            for h, sl in enumerate(_HEAD_SLICES):
                if d == 0:
                    dq_ref[:, sl] = dqs[h]
                    dk_ref[:, sl] = dks[h]
                    dv_ref[:, sl] = dvs[h]
                else:
                    dq_ref[:, sl] += dqs[h]
                    dk_ref[:, sl] += dks[h]
                    dv_ref[:, sl] += dvs[h]
                dgb = dgb + jnp.where(lane == d * NH + h, jnp.sum(dbetas[h], axis=1, keepdims=True), 0.0)
                dgcum = dgcum + jnp.where(lane == 16 + d * NH + h, jnp.sum(dgcs[h], axis=1, keepdims=True), 0.0)
            dgb = dgb + _dot_h3(_cum_matrix(d == 0), dgcum)
        dgb_ref[...] = dgb

    tb = pl.BlockSpec((CB, D), lambda i: (i, 0))
    gbs = pl.BlockSpec((CB, 128), lambda i: (i, 0))
    gls = pl.BlockSpec((NH, 1, 128), lambda i: (i, 0, 0))
    args = []
    for d in (0, 1):
        args += [tinvs[d], *cots[d]]
    return pl.pallas_call(
        body, grid=(nb,), name="dn1_bwd",
        in_specs=[tb, tb, tb, gbs] + [tb, tb, tb, tb, tb, tb, gls] * 2, out_specs=[tb, tb, tb, gbs],
        out_shape=[jax.ShapeDtypeStruct((T, D), F32)] * 3 + [jax.ShapeDtypeStruct((T, 128), F32)],
        compiler_params=_cp(),
    )(q, k, v, gb, *args)


def _dn2_step(u, w, qg, kd, qkd, glrow, s):
    v_new = u - _dot_bf(w, s)
    o = _dot_bf(qg, s) + _dot_bf(qkd, v_new)
    return o, s * glrow + _dot_tn_bf(kd, v_new)


def _scan_order(direction, nlat_b, nall_b):
    if direction == 0:
        return lambda i: (i + nlat_b) % nall_b
    return lambda i: nall_b - 1 - i


def _dn2_fwd(per_dir, nlat):
    T = per_dir[0][0].shape[0]
    nb = T // CB
    blks = [_scan_order(d, nlat // CB, nb) for d in (0, 1)]

    def body(*refs):
        ins, outs, s_scr = refs[:12], refs[12:16], refs[16]

        @pl.when(pl.program_id(0) == 0)
        def _():
            s_scr[...] = jnp.zeros_like(s_scr)
        for d in (0, 1):
            outs[2 * d + 1][0] = s_scr[d]
        for h, sl in enumerate(_HEAD_SLICES):
            for d in (0, 1):
                u_ref, w_ref, qg_ref, kd_ref, qkd_ref, gl_ref = ins[6 * d:6 * d + 6]
                o, s_next = _dn2_step(u_ref[:, sl], w_ref[:, sl], qg_ref[:, sl], kd_ref[:, sl], qkd_ref[:, sl], gl_ref[h], s_scr[d, h])
                outs[2 * d][:, sl] = o
                s_scr[d, h] = s_next

    in_specs, out_specs, args = [], [], []
    for d in (0, 1):
        blk = blks[d]
        tb = pl.BlockSpec((CB, D), lambda i, blk=blk: (blk(i), 0))
        in_specs += [tb] * 5 + [pl.BlockSpec((NH, 1, 128), lambda i, blk=blk: (blk(i), 0, 0))]
        out_specs += [tb, pl.BlockSpec((1, NH, HD, HD), lambda i, blk=blk: (blk(i), 0, 0, 0))]
        args += list(per_dir[d])
    outs = pl.pallas_call(
        body, grid=(nb,), name="dn2_fwd", in_specs=in_specs, out_specs=out_specs,
        out_shape=[jax.ShapeDtypeStruct((T, D), F32), jax.ShapeDtypeStruct((nb, NH, HD, HD), F32)] * 2,
        scratch_shapes=[pltpu.VMEM((2, NH, HD, HD), F32)], compiler_params=_cp(),
    )(*args)
    return [tuple(outs[:2]), tuple(outs[2:])]


def _dn2_bwd(per_dir, do, nlat):
    T = per_dir[0][0].shape[0]
    nb = T // CB
    nlat_b = nlat // CB
    fwd = [_scan_order(d, nlat_b, nb) for d in (0, 1)]
    blks = [lambda i, f=f: f(nb - 1 - i) for f in fwd]

    def body(*refs):
        ins, outs, ds_scr = refs[:16], refs[16:28], refs[28]
        i = pl.program_id(0)

        @pl.when(i == 0)
        def _():
            ds_scr[...] = jnp.zeros_like(ds_scr)
        for h, sl in enumerate(_HEAD_SLICES):
            for d in (0, 1):
                u_ref, w_ref, qg_ref, kd_ref, qkd_ref, gl_ref, sall_ref, do_ref = ins[8 * d:8 * d + 8]
                du_ref, dw_ref, dqg_ref, dkd_ref, dqkd_ref, dgl_ref = outs[6 * d:6 * d + 6]
                args = (u_ref[:, sl], w_ref[:, sl].astype(F32), qg_ref[:, sl].astype(F32), kd_ref[:, sl].astype(F32),
                        qkd_ref[:, sl].astype(F32), gl_ref[h], sall_ref[0, h])
                _, vjp = jax.vjp(_dn2_step, *args)
                is_lat = blks[d](i) < nlat_b
                du, dw, dqg, dkd, dqkd, dgl, ds = vjp((jnp.where(is_lat, do_ref[:, sl], 0.0), ds_scr[d, h]))
                du_ref[:, sl] = du
                dw_ref[:, sl] = dw
                dqg_ref[:, sl] = dqg
                dkd_ref[:, sl] = dkd
                dqkd_ref[:, sl] = dqkd
                dgl_ref[h] = dgl
                ds_scr[d, h] = ds

    in_specs, out_specs, args = [], [], []
    for d in (0, 1):
        blk = blks[d]
        tb = pl.BlockSpec((CB, D), lambda i, blk=blk: (blk(i), 0))
        gls = pl.BlockSpec((NH, 1, 128), lambda i, blk=blk: (blk(i), 0, 0))
        in_specs += [tb] * 5 + [gls, pl.BlockSpec((1, NH, HD, HD), lambda i, blk=blk: (blk(i), 0, 0, 0)),
                                pl.BlockSpec((CB, D), lambda i, blk=blk: (jnp.minimum(blk(i), nlat_b - 1), 0))]
        out_specs += [tb] * 5 + [gls]
        args += list(per_dir[d]) + [do]
    outs = pl.pallas_call(
        body, grid=(nb,), name="dn2_bwd", in_specs=in_specs, out_specs=out_specs,
        out_shape=([jax.ShapeDtypeStruct((T, D), F32)] * 5 + [jax.ShapeDtypeStruct((nb * NH, 1, 128), F32)]) * 2,
        scratch_shapes=[pltpu.VMEM((2, NH, HD, HD), F32)], compiler_params=_cp(),
    )(*args)
    return [tuple(outs[:6]), tuple(outs[6:])]


def _ghn_fn(o, gt, w):
    y = o * lax.rsqrt(jnp.mean(o * o, axis=-1, keepdims=True) + EPS)
    return (y * w) * jax.nn.silu(gt)


def _ghn_fwd(o_f, o_b, p, w, nlat):
    tb = _tile(nlat, (256, 128))

    def body(of_ref, ob_ref, gt_ref, w_ref, y_ref):
        for h in range(NH):
            sl = slice(h * HD, (h + 1) * HD)
            y_ref[:, sl] = _ghn_fn(of_ref[:, sl] + ob_ref[:, sl], gt_ref[:, sl], w_ref[...]).astype(BF)

    row = pl.BlockSpec((tb, D), lambda i: (i, 0))
    return pl.pallas_call(
        body, grid=(nlat // tb,), name="ghn_fwd",
        in_specs=[row, row, pl.BlockSpec((tb, D), lambda i: (i, O_GT // D)), pl.BlockSpec((1, HD), lambda i: (0, 0))],
        out_specs=row, out_shape=jax.ShapeDtypeStruct((nlat, D), BF),
    )(o_f, o_b, p, w)


def _ghn_bwd(o_f, o_b, p, w, dy, nlat):
    tb = _tile(nlat, (256, 128))

    def body(of_ref, ob_ref, gt_ref, w_ref, dy_ref, do_ref, dgt_ref, dw_ref):
        @pl.when(pl.program_id(0) == 0)
        def _():
            dw_ref[...] = jnp.zeros_like(dw_ref)
        for h in range(NH):
            sl = slice(h * HD, (h + 1) * HD)
            _, vjp = jax.vjp(_ghn_fn, of_ref[:, sl] + ob_ref[:, sl], gt_ref[:, sl], w_ref[...])
            do, dgt, dw = vjp(dy_ref[:, sl])
            do_ref[:, sl] = do
            dgt_ref[:, sl] = dgt.astype(BF)
            dw_ref[...] += dw

    row = pl.BlockSpec((tb, D), lambda i: (i, 0))
    one = pl.BlockSpec((1, HD), lambda i: (0, 0))
    return pl.pallas_call(
        body, grid=(nlat // tb,), name="ghn_bwd",
        in_specs=[row, row, pl.BlockSpec((tb, D), lambda i: (i, O_GT // D)), one, row],
        out_specs=[row, row, one],
        out_shape=[jax.ShapeDtypeStruct((nlat, D), F32), jax.ShapeDtypeStruct((nlat, D), BF), jax.ShapeDtypeStruct((1, HD), F32)],
    )(o_f, o_b, p, w, dy)


@jax.custom_vjp
def _swap32(x):
    lane = lax.broadcasted_iota(jnp.int32, x.shape, 1)
    return jnp.where((lane & 32) == 0, pltpu.roll(x, 96, 1), pltpu.roll(x, 32, 1))


_swap32.defvjp(lambda x: (_swap32(x), None), lambda _, g: (_swap32(g),))


def _qk_post_fn(x, w, cos, sin):
    y = (x * lax.rsqrt(jnp.mean(x * x, axis=-1, keepdims=True) + EPS)) * w
    return y * cos + _swap32(y) * sin


def _attn_prep_fwd(p, qn, kn, cos, sin):
    T = p.shape[0]
    tb = _tile(T, (256, 128))

    def body(q_ref, k_ref, v_ref, qn_ref, kn_ref, cos_ref, sin_ref, qr_ref, kr_ref, vb_ref):
        cos_v, sin_v = cos_ref[...], sin_ref[...]
        for h in range(NH):
            sl = slice(h * HD, (h + 1) * HD)
            qr_ref[:, sl] = _qk_post_fn(q_ref[:, sl], qn_ref[...], cos_v, sin_v).astype(BF)
        for h in range(KVH):
            sl = slice(h * HD, (h + 1) * HD)
            kr_ref[:, sl] = _qk_post_fn(k_ref[:, sl], kn_ref[...], cos_v, sin_v).astype(BF)
        vb_ref[...] = v_ref[...].astype(BF)

    one = pl.BlockSpec((1, HD), lambda i: (0, 0))
    tab = pl.BlockSpec((tb, HD), lambda i: (i, 0))
    return pl.pallas_call(
        body, grid=(T // tb,), name="attn_prep_fwd",
        in_specs=[pl.BlockSpec((tb, D), lambda i: (i, O_Q // D)), pl.BlockSpec((tb, KV), lambda i: (i, O_K // KV)),
                  pl.BlockSpec((tb, KV), lambda i: (i, O_V // KV)), one, one, tab, tab],
        out_specs=[pl.BlockSpec((tb, D), lambda i: (i, 0)), pl.BlockSpec((tb, KV), lambda i: (i, 0)),
                   pl.BlockSpec((tb, KV), lambda i: (i, 0))],
        out_shape=[jax.ShapeDtypeStruct((T, D), BF), jax.ShapeDtypeStruct((T, KV), BF), jax.ShapeDtypeStruct((T, KV), BF)],
    )(p, p, p, qn, kn, cos, sin)


def _attn_prep_bwd(p, qn, kn, cos, sin, dqr, dkp, dvp, dkc, dvc, nlat):
    T = p.shape[0]
    nqb = nlat // CB
    ncb = (T - nlat) // CB

    def body(q_ref, k_ref, v_ref, qn_ref, kn_ref, cos_ref, sin_ref, dqr_ref, dka_ref, dkb_ref, dkc3_ref, dva_ref, dvb_ref, dvc3_ref,
             dkctx_ref, dvctx_ref, dq_ref, dk_ref, dv_ref, dqn_ref, dkn_ref):
        i = pl.program_id(0)
        is_lat = i < nqb
        cos_v, sin_v = cos_ref[...], sin_ref[...]

        @pl.when(i == 0)
        def _():
            dqn_ref[...] = jnp.zeros_like(dqn_ref)
            dkn_ref[...] = jnp.zeros_like(dkn_ref)

        def band_sum(a_ref, b_ref, c_ref, ctx_ref):
            s = b_ref[0] + jnp.where(i > 0, a_ref[0], 0.0) + jnp.where(i < nqb - 1, c_ref[0], 0.0)
            return jnp.where(is_lat, s, ctx_ref[...])

        dkr = band_sum(dka_ref, dkb_ref, dkc3_ref, dkctx_ref)
        dv_ref[...] = band_sum(dva_ref, dvb_ref, dvc3_ref, dvctx_ref).astype(BF)
        for h in range(NH):
            sl = slice(h * HD, (h + 1) * HD)
            _, vjp = jax.vjp(_qk_post_fn, q_ref[:, sl], qn_ref[...], cos_v, sin_v)
            dq, dqn, _, _ = vjp(jnp.where(is_lat, dqr_ref[:, sl], 0.0))
            dq_ref[:, sl] = dq.astype(BF)
            dqn_ref[...] += dqn
        for h in range(KVH):
            sl = slice(h * HD, (h + 1) * HD)
            _, vjp = jax.vjp(_qk_post_fn, k_ref[:, sl], kn_ref[...], cos_v, sin_v)
            dk, dkn, _, _ = vjp(dkr[:, sl])
            dk_ref[:, sl] = dk.astype(BF)
            dkn_ref[...] += dkn

    one = pl.BlockSpec((1, HD), lambda i: (0, 0))
    tab = pl.BlockSpec((CB, HD), lambda i: (i, 0))
    lat = lambda i: jnp.minimum(i, nqb - 1)

    def part(off, slot):
        return pl.BlockSpec((1, CB, KV), lambda i: (jnp.clip(lat(i) + off, 0, nqb - 1) * 3 + slot, 0, 0))

    ctxs = pl.BlockSpec((CB, KV), lambda i: (jnp.clip(i - nqb, 0, ncb - 1), 0))
    kvs = pl.BlockSpec((CB, KV), lambda i: (i, 0))
    return pl.pallas_call(
        body, grid=(T // CB,), name="attn_prep_bwd",
        in_specs=[pl.BlockSpec((CB, D), lambda i: (i, O_Q // D)), pl.BlockSpec((CB, KV), lambda i: (i, O_K // KV)),
                  pl.BlockSpec((CB, KV), lambda i: (i, O_V // KV)), one, one, tab, tab,
                  pl.BlockSpec((CB, D), lambda i: (lat(i), 0)),
                  part(-1, 2), part(0, 1), part(1, 0), part(-1, 2), part(0, 1), part(1, 0), ctxs, ctxs],
        out_specs=[pl.BlockSpec((CB, D), lambda i: (i, 0)), kvs, kvs, one, one],
        out_shape=[jax.ShapeDtypeStruct((T, D), BF), jax.ShapeDtypeStruct((T, KV), BF), jax.ShapeDtypeStruct((T, KV), BF),
                   jax.ShapeDtypeStruct((1, HD), F32), jax.ShapeDtypeStruct((1, HD), F32)],
    )(p, p, p, qn, kn, cos, sin, dqr, dkp, dkp, dkp, dvp, dvp, dvp, dkc, dvc)


def _attn_group_fn(q0, q1, q2, q3, kall, vall, s0, s1, s2, s3, bias):
    q = jnp.concatenate([q0, q1, q2, q3], axis=0)
    s = _dot_nt_bf(q, kall) * (HD ** -0.5) + bias
    sk = jnp.concatenate([jnp.broadcast_to(jnp.mean(t, axis=1, keepdims=True), (CB, 1)) for t in (s0, s1, s2, s3)], axis=0)
    m = lax.stop_gradient(jnp.maximum(jnp.max(s, axis=1, keepdims=True), sk))
    e = jnp.exp(s - m)
    den = jnp.sum(e, axis=1, keepdims=True) + jnp.exp(sk - m)
    return _dot_bf(e / den, vall)


def _attn_bias(lc):
    r, c = _iota2((GRP * CB, 3 * CB + lc))
    rel = c - (r & (CB - 1))
    win = (rel >= 0) & (rel <= 2 * CB)
    ctx = c >= 3 * CB
    seen = [(win & (c >= CB)) | ctx, win | ctx, (win & (c < 2 * CB)) | ctx]
    return jnp.stack([jnp.where(s, 0.0, -1e30) for s in seen]).astype(F32)


def _attn_specs(nqb, lc, nlat):
    assert nqb >= 2
    qs = pl.BlockSpec((CB, GRP * HD), lambda kh, i: (i, kh))
    ka = pl.BlockSpec((CB, HD), lambda kh, i: (jnp.maximum(i - 1, 0), kh))
    kb = pl.BlockSpec((CB, HD), lambda kh, i: (i, kh))
    kc = pl.BlockSpec((CB, HD), lambda kh, i: (jnp.minimum(i + 1, nqb - 1), kh))
    kx = pl.BlockSpec((lc, HD), lambda kh, i: (nlat // lc, kh))
    sk = pl.BlockSpec((1, 8, 128), lambda kh, i: (kh, 0, 0))
    bs = pl.BlockSpec((1, GRP * CB, 3 * CB + lc), lambda kh, i: (jnp.where(i == 0, 0, jnp.where(i == nqb - 1, 2, 1)), 0, 0))
    return qs, ka, kb, kc, kx, sk, bs


def _attn_fwd(qr, kr, vb, sink, nlat):
    lc = kr.shape[0] - nlat
    nqb = nlat // CB
    qs, ka, kb, kc, kx, sk, bs = _attn_specs(nqb, lc, nlat)

    def body(q_ref, ka_ref, kb_ref, kc_ref, kx_ref, va_ref, vb_ref, vc_ref, vx_ref, sk_ref, bias_ref, o_ref):
        kall = jnp.concatenate([ka_ref[...], kb_ref[...], kc_ref[...], kx_ref[...]], axis=0)
        vall = jnp.concatenate([va_ref[...], vb_ref[...], vc_ref[...], vx_ref[...]], axis=0)
        qh = [q_ref[:, g * HD:(g + 1) * HD] for g in range(GRP)]
        sinks = [sk_ref[0, g:g + 1, :] for g in range(GRP)]
        o = _attn_group_fn(*qh, kall, vall, *sinks, bias_ref[0])
        for g in range(GRP):
            o_ref[:, g * HD:(g + 1) * HD] = o[g * CB:(g + 1) * CB].astype(BF)

    return pl.pallas_call(
        body, grid=(KVH, nqb), name="attn_fwd",
        in_specs=[qs, ka, kb, kc, kx, ka, kb, kc, kx, sk, bs], out_specs=qs,
        out_shape=jax.ShapeDtypeStruct((nlat, D), BF), compiler_params=_cp(),
    )(qr, kr, kr, kr, kr, vb, vb, vb, vb, sink, _attn_bias(lc))


def _attn_bwd(qr, kr, vb, sink, dy, nlat):
    lc = kr.shape[0] - nlat
    nqb = nlat // CB
    qs, ka, kb, kc, kx, sk, bs = _attn_specs(nqb, lc, nlat)

    def body(q_ref, ka_ref, kb_ref, kc_ref, kx_ref, va_ref, vb_ref, vc_ref, vx_ref, sk_ref, dy_ref, bias_ref,
             dq_ref, dkp_ref, dvp_ref, dkx_ref, dvx_ref, dsk_ref):
        i = pl.program_id(1)
        kall = jnp.concatenate([ka_ref[...], kb_ref[...], kc_ref[...], kx_ref[...]], axis=0).astype(F32)
        vall = jnp.concatenate([va_ref[...], vb_ref[...], vc_ref[...], vx_ref[...]], axis=0).astype(F32)
        qh = [q_ref[:, g * HD:(g + 1) * HD].astype(F32) for g in range(GRP)]
        f = functools.partial(_attn_group_fn, bias=bias_ref[0])
        _, vjp = jax.vjp(f, *qh, kall, vall, *[sk_ref[0, g:g + 1, :] for g in range(GRP)])
        dyv = jnp.concatenate([dy_ref[:, g * HD:(g + 1) * HD] for g in range(GRP)], axis=0)
        d = vjp(dyv)
        for g in range(GRP):
            dq_ref[:, g * HD:(g + 1) * HD] = d[g]
        dk, dv = d[4], d[5]
        for t in range(3):
            dkp_ref[t] = dk[t * CB:(t + 1) * CB]
            dvp_ref[t] = dv[t * CB:(t + 1) * CB]

        @pl.when(i == 0)
        def _():
            dkx_ref[...] = jnp.zeros_like(dkx_ref)
            dvx_ref[...] = jnp.zeros_like(dvx_ref)
            dsk_ref[...] = jnp.zeros_like(dsk_ref)
        dkx_ref[...] += dk[3 * CB:]
        dvx_ref[...] += dv[3 * CB:]
        for g in range(GRP):
            dsk_ref[0, g:g + 1, :] += d[6 + g]

    dys = pl.BlockSpec((CB, GRP * HD), lambda kh, i: (i, kh))
    parts = pl.BlockSpec((3, CB, HD), lambda kh, i: (i, 0, kh))
    ctxo = pl.BlockSpec((lc, HD), lambda kh, i: (0, kh))
    return pl.pallas_call(
        body, grid=(KVH, nqb), name="attn_bwd",
        in_specs=[qs, ka, kb, kc, kx, ka, kb, kc, kx, sk, dys, bs],
        out_specs=[dys, parts, parts, ctxo, ctxo, sk],
        out_shape=[jax.ShapeDtypeStruct((nlat, D), F32), jax.ShapeDtypeStruct((3 * nqb, CB, KV), F32),
                   jax.ShapeDtypeStruct((3 * nqb, CB, KV), F32), jax.ShapeDtypeStruct((lc, KV), F32),
                   jax.ShapeDtypeStruct((lc, KV), F32), jax.ShapeDtypeStruct((KVH, 8, 128), F32)],
        compiler_params=_cp(),
    )(qr, kr, kr, kr, kr, vb, vb, vb, vb, sink, dy, _attn_bias(lc))


def _merge_fn(z_dn, z_at, g_dn, g_at):
    return jax.nn.sigmoid(g_dn) * z_dn + jax.nn.sigmoid(g_at) * z_at


def _merge_fwd(z_dn, z_at, p, nlat):
    tb = _tile(nlat, (256, 128))

    def body(zd_ref, za_ref, gd_ref, ga_ref, o_ref):
        o_ref[...] = _merge_fn(zd_ref[...], za_ref[...], gd_ref[...], ga_ref[...]).astype(BF)

    row = pl.BlockSpec((tb, D), lambda i: (i, 0))
    return pl.pallas_call(
        body, grid=(nlat // tb,), name="merge_fwd",
        in_specs=[row, row, pl.BlockSpec((tb, D), lambda i: (i, O_MG // D)), pl.BlockSpec((tb, D), lambda i: (i, O_MG // D + 1))],
        out_specs=row, out_shape=jax.ShapeDtypeStruct((nlat, D), BF),
    )(z_dn, z_at, p, p)


def _merge_bwd(z_dn, z_at, p, dm, nlat):
    tb = _tile(nlat, (256, 128))

    def body(zd_ref, za_ref, gd_ref, ga_ref, dm_ref, dzd_ref, dza_ref, dg_ref):
        _, vjp = jax.vjp(_merge_fn, zd_ref[...], za_ref[...], gd_ref[...], ga_ref[...])
        dzd, dza, dgd, dga = vjp(dm_ref[...])
        dzd_ref[...] = dzd.astype(BF)
        dza_ref[...] = dza.astype(BF)
        dg_ref[:, :D] = dgd.astype(BF)
        dg_ref[:, D:] = dga.astype(BF)

    row = pl.BlockSpec((tb, D), lambda i: (i, 0))
    return pl.pallas_call(
        body, grid=(nlat // tb,), name="merge_bwd",
        in_specs=[row, row, pl.BlockSpec((tb, D), lambda i: (i, O_MG // D)), pl.BlockSpec((tb, D), lambda i: (i, O_MG // D + 1)), row],
        out_specs=[row, row, pl.BlockSpec((tb, 2 * D), lambda i: (i, 0))],
        out_shape=[jax.ShapeDtypeStruct((nlat, D), BF), jax.ShapeDtypeStruct((nlat, D), BF), jax.ShapeDtypeStruct((nlat, 2 * D), BF)],
    )(z_dn, z_at, p, p, dm)


def _resid_fwd(x, gate, y):
    n = y.shape[0]
    tb = _tile(n, (256, 128))

    def body(x_ref, g_ref, y_ref, o_ref):
        o_ref[...] = x_ref[...] + g_ref[...] * y_ref[...]

    row = pl.BlockSpec((tb, D), lambda i: (i, 0))
    return pl.pallas_call(
        body, grid=(n // tb,), name="resid_fwd",
        in_specs=[row, pl.BlockSpec((1, D), lambda i: (0, 0)), row], out_specs=row,
        out_shape=jax.ShapeDtypeStruct((n, D), F32),
    )(x, gate, y)


def _resid_bwd(dx1a, dx1b, gate, y):
    n = y.shape[0]
    tb = _tile(n, (256, 128))

    def body(a_ref, b_ref, g_ref, y_ref, dx_ref, dy_ref, dg_ref):
        dx = a_ref[...] + b_ref[...]
        dx_ref[...] = dx
        dy_ref[...] = (g_ref[...] * dx).astype(BF)

        @pl.when(pl.program_id(0) == 0)
        def _():
            dg_ref[...] = jnp.zeros_like(dg_ref)
        dg_ref[...] += jnp.sum(dx * y_ref[...], axis=0, keepdims=True)

    row = pl.BlockSpec((tb, D), lambda i: (i, 0))
    one = pl.BlockSpec((1, D), lambda i: (0, 0))
    return pl.pallas_call(
        body, grid=(n // tb,), name="resid_bwd",
        in_specs=[row, row, one, row], out_specs=[row, row, one],
        out_shape=[jax.ShapeDtypeStruct((n, D), F32), jax.ShapeDtypeStruct((n, D), BF), jax.ShapeDtypeStruct((1, D), F32)],
    )(dx1a, dx1b, gate, y)


def _swiglu_fn(ug, uv):
    return jax.nn.silu(ug) * uv


FFN_GROUP = 256


def _ffn_mid_fwd(u, w8, bias):
    n = u.shape[0]
    tb = _tile(n, (256, 128))
    starts, ends = _segment_edges((n,), tb)

    def body(cur_ref, prev_ref, next_ref, w_ref, b_ref, o_ref):
        keep = _keep_halos(pl.program_id(0), starts, ends)
        for c0 in range(0, DFF, FFN_GROUP):
            halves = []
            for cols in (slice(c0, c0 + FFN_GROUP), slice(DFF + c0, DFF + c0 + FFN_GROUP)):
                xe = _ext_rows((cur_ref, prev_ref, next_ref), cols, keep)
                halves.append(_conv_rows(xe, w_ref, cols, FFN_TAPS)[HALO:HALO + tb] + b_ref[:, cols])
            o_ref[:, c0:c0 + FFN_GROUP] = _swiglu_fn(*halves).astype(BF)

    return pl.pallas_call(
        body, grid=(n // tb,), name="ffn_mid_fwd",
        in_specs=_halo_specs(tb, 2 * DFF, n) + [pl.BlockSpec((8, 2 * DFF), lambda i: (0, 0)), pl.BlockSpec((1, 2 * DFF), lambda i: (0, 0))],
        out_specs=pl.BlockSpec((tb, DFF), lambda i: (i, 0)), out_shape=jax.ShapeDtypeStruct((n, DFF), BF),
        compiler_params=_cp(),
    )(u, u, u, w8, bias)


def _ffn_mid_bwd(u, w8, bias, da):
    n = u.shape[0]
    tb = _tile(n, (256, 128))
    starts, ends = _segment_edges((n,), tb)

    def body(cur_ref, prev_ref, next_ref, w_ref, b_ref, da_c, da_p, da_n, du_ref, dw_ref, db_ref):
        i = pl.program_id(0)
        keep = _keep_halos(i, starts, ends)

        @pl.when(i == 0)
        def _():
            dw_ref[...] = jnp.zeros_like(dw_ref)
            db_ref[...] = jnp.zeros_like(db_ref)

        for c0 in range(0, DFF, FFN_GROUP):
            col_pair = (slice(c0, c0 + FFN_GROUP), slice(DFF + c0, DFF + c0 + FFN_GROUP))
            xes = [_ext_rows((cur_ref, prev_ref, next_ref), cols, keep) for cols in col_pair]
            convs = [_conv_rows(xe, w_ref, cols, FFN_TAPS) + b_ref[:, cols] for xe, cols in zip(xes, col_pair)]
            dae = _ext_rows((da_c, da_p, da_n), col_pair[0], keep)
            _, vjp = jax.vjp(_swiglu_fn, *convs)
            for xe, cols, dce in zip(xes, col_pair, vjp(dae)):
                du_ref[:, cols] = _conv_rows(dce, w_ref, cols, FFN_TAPS, transpose=True)[HALO:HALO + tb].astype(BF)
                dcur = dce[HALO:HALO + tb]
                for j, g in enumerate(_tap_grads(dcur, xe, FFN_TAPS, tb)):
                    dw_ref[j:j + 1, cols] += g
                db_ref[:, cols] += jnp.sum(dcur, axis=0, keepdims=True)

    wspec = pl.BlockSpec((8, 2 * DFF), lambda i: (0, 0))
    bspec = pl.BlockSpec((1, 2 * DFF), lambda i: (0, 0))
    return pl.pallas_call(
        body, grid=(n // tb,), name="ffn_mid_bwd",
        in_specs=_halo_specs(tb, 2 * DFF, n) + [wspec, bspec] + _halo_specs(tb, DFF, n),
        out_specs=[pl.BlockSpec((tb, 2 * DFF), lambda i: (i, 0)), wspec, bspec],
        out_shape=[jax.ShapeDtypeStruct((n, 2 * DFF), BF), jax.ShapeDtypeStruct((8, 2 * DFF), F32), jax.ShapeDtypeStruct((1, 2 * DFF), F32)],
        compiler_params=_cp(),
    )(u, u, u, w8, bias, da, da, da)


def _loss_kernel(x1, gate, ff, target):
    n = x1.shape[0]
    tb = _tile(n, (256, 128))

    def body(x_ref, g_ref, f_ref, t_ref, loss_ref, dy_ref, dff_ref, dg_ref):
        err = x_ref[...] + g_ref[...] * f_ref[...] - t_ref[...]
        dy = err * (1.0 / D)
        dy_ref[...] = dy
        dff_ref[...] = (g_ref[...] * dy).astype(BF)

        @pl.when(pl.program_id(0) == 0)
        def _():
            loss_ref[...] = jnp.zeros_like(loss_ref)
            dg_ref[...] = jnp.zeros_like(dg_ref)
        part = 0.5 * jnp.sum(jnp.sum(err * err, axis=1, keepdims=True) * (1.0 / D), axis=0, keepdims=True)
        loss_ref[...] += jnp.broadcast_to(part, (1, 128))
        dg_ref[...] += jnp.sum(dy * f_ref[...], axis=0, keepdims=True)

    row = pl.BlockSpec((tb, D), lambda i: (i, 0))
    one = pl.BlockSpec((1, D), lambda i: (0, 0))
    return pl.pallas_call(
        body, grid=(n // tb,), name="loss",
        in_specs=[row, one, row, row], out_specs=[pl.BlockSpec((1, 128), lambda i: (0, 0)), row, row, one],
        out_shape=[jax.ShapeDtypeStruct((1, 128), F32), jax.ShapeDtypeStruct((n, D), F32),
                   jax.ShapeDtypeStruct((n, D), BF), jax.ShapeDtypeStruct((1, D), F32)],
    )(x1, gate, ff, target)


def _rope_tables(nlat, lc):
    t = jnp.arange(nlat)
    row = (t // GRID_W).astype(F32)
    col = (t % GRID_W).astype(F32)
    inv_freq = ROPE_BASE ** (-jnp.arange(32, dtype=F32) / 32)
    ar, ac = row[:, None] * inv_freq, col[:, None] * inv_freq
    cos = jnp.concatenate([jnp.cos(ar), jnp.cos(ar), jnp.cos(ac), jnp.cos(ac)], axis=1)
    sin = jnp.concatenate([-jnp.sin(ar), jnp.sin(ar), -jnp.sin(ac), jnp.sin(ac)], axis=1)
    cos = jnp.concatenate([cos, jnp.ones((lc, HD), F32)], axis=0)
    sin = jnp.concatenate([sin, jnp.zeros((lc, HD), F32)], axis=0)
    return cos, sin


def _pad_rows8(w):
    return jnp.concatenate([w, jnp.zeros((8 - w.shape[0], w.shape[1]), w.dtype)], axis=0)


def _pack_w_in(w):
    cuts = [sum(IN_SIZES[:i]) for i in range(len(IN_SIZES) + 1)]
    qkv, gt, b, a, q, k, v, mg = [w[:, cuts[i]:cuts[i + 1]] for i in range(len(IN_SIZES))]
    return jnp.concatenate([qkv, gt, q, mg, k, v, b, a, jnp.zeros((w.shape[0], PW - O_BA - 32), w.dtype)], axis=1)


def _unpack_w_in(g):
    return jnp.concatenate([g[:, O_QKV:O_GT], g[:, O_GT:O_Q], g[:, O_BA:O_BA + 32], g[:, O_Q:O_MG], g[:, O_K:O_V],
                            g[:, O_V:O_BA], g[:, O_MG:O_K]], axis=1)


def _local_step(x, ctx, mod_x, mod_c, target, w_in_p, w_bdn, w_bat, w_out, w_up, w_down,
                norm_mix, norm_ffn, dn_conv, a_log, dt_bias, dn_norm, q_norm, k_norm, sink, ffn_conv, ffn_conv_b):
    L, LC = x.shape[0], ctx.shape[0]
    T = L + LC
    xc = jnp.concatenate([x, ctx], axis=0)
    seg = lambda r: jnp.stack([mod_x[r], mod_c[r]])[:, None, :]
    sh_a, sc_a = seg(0), seg(1)
    g_a, g_f = mod_x[2][None], mod_x[5][None]
    sh_f, sc_f = mod_x[3][None, None], mod_x[4][None, None]
    sh_f2 = jnp.concatenate([sh_f, sh_f], axis=0)
    sc_f2 = jnp.concatenate([sc_f, sc_f], axis=0)
    cos, sin = _rope_tables(L, LC)
    dnc8 = _pad_rows8(dn_conv)
    ffc8 = _pad_rows8(ffn_conv)
    gate_row = lambda a: jnp.concatenate([jnp.zeros((1, 16), F32), a.reshape(1, 16), jnp.zeros((1, 96), F32)], axis=1)
    alog_row, dt_row = gate_row(a_log), gate_row(dt_bias)
    sinkb = jnp.concatenate([jnp.broadcast_to(sink.reshape(KVH, GRP, 1), (KVH, GRP, 128)), jnp.zeros((KVH, 8 - GRP, 128), F32)], axis=1)

    h1 = _norm_mod_fwd(xc, norm_mix, sh_a, sc_a, L, "norm_mix_fwd")
    p = _mm(h1, w_in_p, form="nn", out_dtype=F32, name="in_proj")
    q, k, v, gb = _dn_pre_fwd(p, dnc8, alog_row, dt_row, (L, LC))
    wy = _dn1_fwd(q, k, v, gb)
    scans = _dn2_fwd([t[:6] for t in wy], L)
    o_dir = [s[0] for s in scans]
    y_dn = _ghn_fwd(o_dir[0], o_dir[1], p, dn_norm, L)
    qr, kr, vb = _attn_prep_fwd(p, q_norm, k_norm, cos, sin)
    y_at = _attn_fwd(qr, kr, vb, sinkb, L)
    z_dn = _mm(y_dn, w_bdn, form="nn", out_dtype=F32, name="branch_dn")
    z_at = _mm(y_at, w_bat, form="nn", out_dtype=F32, name="branch_at")
    merged = _merge_fwd(z_dn, z_at, p, L)
    mix = _mm(merged, w_out, form="nn", out_dtype=F32, name="out_proj")
    x1 = _resid_fwd(xc, g_a, mix)
    h2 = _norm_mod_fwd(x1, norm_ffn, sh_f2, sc_f2, L, "norm_ffn_fwd")
    u_raw = _mm(h2, w_up, form="nn", out_dtype=F32, name="ffn_up")
    act = _ffn_mid_fwd(u_raw, ffc8, ffn_conv_b)
    ff = _mm(act, w_down, form="nn", out_dtype=F32, name="ffn_down")
    loss_row, dy, dff, dg_f = _loss_kernel(x1, g_f, ff, target)

    g_down = _mm(act, dff, form="tn", out_dtype=BF, name="g_ffn_down")
    dact = _mm(dff, w_down, form="nt", out_dtype=F32, name="d_act")
    du_raw, g_ffc8, g_ffb = _ffn_mid_bwd(u_raw, ffc8, ffn_conv_b, dact)
    g_up = _mm(h2, du_raw, form="tn", out_dtype=BF, name="g_ffn_up")
    dh2 = _mm(du_raw, w_up, form="nt", out_dtype=F32, name="d_h2")
    dx1n, g_nffn, dsh_f, dsc_f = _norm_mod_bwd(x1, norm_ffn, sh_f2, sc_f2, dh2, L, "norm_ffn_bwd")
    dx1, dmix, dg_a = _resid_bwd(dy, dx1n, g_a, mix)

    g_out = _mm(merged, dmix, form="tn", out_dtype=BF, name="g_w_out")
    dmerged = _mm(dmix, w_out, form="nt", out_dtype=F32, name="d_merged")
    dz_dn, dz_at, dmg = _merge_bwd(z_dn, z_at, p, dmerged, L)
    g_bdn = _mm(y_dn, dz_dn, form="tn", out_dtype=BF, name="g_branch_dn")
    g_bat = _mm(y_at, dz_at, form="tn", out_dtype=BF, name="g_branch_at")
    dy_dn = _mm(dz_dn, w_bdn, form="nt", out_dtype=F32, name="d_y_dn")
    dy_at = _mm(dz_at, w_bat, form="nt", out_dtype=F32, name="d_y_at")
    dqr, dkp, dvp, dkx, dvx, dsink = _attn_bwd(qr, kr, vb, sinkb, dy_at, L)
    dq_raw, dk_raw, dv_raw, g_qn, g_kn = _attn_prep_bwd(p, q_norm, k_norm, cos, sin, dqr, dkp, dvp, dkx, dvx, L)
    do, dgt, g_dnn = _ghn_bwd(o_dir[0], o_dir[1], p, dn_norm, dy_dn, L)
    cots = _dn2_bwd([wy[d][:6] + (scans[d][1],) for d in (0, 1)], do, L)
    dq, dk, dv, dgb = _dn1_bwd(q, k, v, gb, [t[6] for t in wy], cots)
    dqkv_raw, dba, g_dnc8, g_alog, g_dt = _dn_pre_bwd(p, dnc8, alog_row, dt_row, dq, dk, dv, dgb, (L, LC))
    padc = lambda a: jnp.concatenate([a, jnp.zeros((LC, a.shape[1]), a.dtype)], axis=0)
    dp = jnp.concatenate([dqkv_raw, padc(dgt), dq_raw, padc(dmg), dk_raw, dv_raw, dba, jnp.zeros((T, PW - O_BA - 128), BF)], axis=1)
    g_in = _mm(h1, dp, form="tn", out_dtype=BF, name="g_w_in")
    dh1 = _mm(dp, w_in_p, form="nt", out_dtype=F32, name="d_h1")
    dxc, g_nmix, dsh_a, dsc_a = _norm_mod_bwd(xc, norm_mix, sh_a, sc_a, dh1, L, "norm_mix_bwd")
    grad_x = _add2(dx1, dxc)

    zero = jnp.zeros((D,), F32)
    dmod_x = jnp.stack([dsh_a[0, 0], dsc_a[0, 0], dg_a[0], dsh_f[0, 0], dsc_f[0, 0], dg_f[0]])
    dmod_c = jnp.stack([dsh_a[1, 0], dsc_a[1, 0], zero, zero, zero, zero])
    small = dict(
        dmod_x=dmod_x, dmod_c=dmod_c, norm_mix=g_nmix, norm_ffn=g_nffn, dn_conv=g_dnc8[:5], dn_a_log=g_alog[0, 16:32].reshape(2, 8),
        dn_dt_bias=g_dt[0, 16:32].reshape(2, 8), dn_norm=g_dnn, q_norm=g_qn, k_norm=g_kn,
        attn_sink=jnp.sum(dsink[:, :GRP, :], axis=2).reshape(1, NH), ffn_conv=g_ffc8[:3], ffn_conv_b=g_ffb)
    return loss_row[0, 0], grad_x, (g_in, g_bdn, g_bat, g_out, g_up, g_down), small


def _elementwise(fn, args, out_dtypes, name, rows=None):
    n = rows or args[0].shape[0]
    ncol = args[0].shape[1]
    tb = _tile(n, (256, 128, 8))
    nout = len(out_dtypes)

    def body(*refs):
        outs = fn(*[r[...] for r in refs[:len(args)]])
        for o_ref, o in zip(refs[len(args):], outs):
            o_ref[...] = o.astype(o_ref.dtype)

    spec = pl.BlockSpec((tb, ncol), lambda i: (i, 0))
    return pl.pallas_call(
        body, grid=(n // tb,), name=name, in_specs=[spec] * len(args), out_specs=[spec] * nout,
        out_shape=[jax.ShapeDtypeStruct((n, ncol), dt) for dt in out_dtypes],
    )(*args)


def _add2(a, b):
    return _elementwise(lambda x, y: (x + y,), [a, b], [F32], "add2", rows=a.shape[0])[0]


def _exchange(arrays, scatter, name):
    n = len(arrays)

    def body(*refs):
        ins, outs = refs[:n], refs[n:2 * n]
        send_sems, recv_sems, local_sems = refs[2 * n:]
        x, y, c = lax.axis_index("x"), lax.axis_index("y"), lax.axis_index("c")
        me = 4 * x + 2 * y + c
        started = []
        for k in range(n):
            local = pltpu.make_async_copy(ins[k].at[me] if scatter else ins[k], outs[k].at[me], local_sems.at[k])
            local.start()
            started.append(local)
        pending = []
        for k in range(n):
            for m in range(1, N_DEV):
                px = 1 - x if m & 4 else x
                py = 1 - y if m & 2 else y
                pc = 1 - c if m & 1 else c
                peer = 4 * px + 2 * py + pc
                src = ins[k].at[peer] if scatter else ins[k]
                sem = k * (N_DEV - 1) + m - 1
                push = pltpu.make_async_remote_copy(src_ref=src, dst_ref=outs[k].at[me], send_sem=send_sems.at[sem],
                                                    recv_sem=recv_sems.at[sem], device_id=(px, py, pc), device_id_type=MESH)
                push.start()
                landed = pltpu.make_async_remote_copy(src_ref=src, dst_ref=outs[k].at[peer], send_sem=send_sems.at[sem],
                                                      recv_sem=recv_sems.at[sem], device_id=(px, py, pc), device_id_type=MESH)
                pending.append((push, landed))
        for push, landed in pending:
            landed.wait_recv()
        for push, landed in pending:
            push.wait_send()
        for local in started:
            local.wait()

    hbm = pl.BlockSpec(memory_space=pl.ANY)
    out_shape = [jax.ShapeDtypeStruct(a.shape if scatter else (N_DEV,) + a.shape, a.dtype) for a in arrays]
    return pl.pallas_call(
        body, name=name, in_specs=[hbm] * n, out_specs=[hbm] * n, out_shape=out_shape,
        scratch_shapes=[pltpu.SemaphoreType.DMA((n * (N_DEV - 1),)), pltpu.SemaphoreType.DMA((n * (N_DEV - 1),)),
                        pltpu.SemaphoreType.DMA((n,))],
    )(*arrays)


def _ada_fwd(c16, w_ada, b_ada):
    def body(c_ref, w_ref, b_ref, o_ref):
        o_ref[...] = _dot_hi(jax.nn.silu(c_ref[...]), w_ref[...]) + b_ref[...]

    return pl.pallas_call(body, name="ada_fwd", out_shape=jax.ShapeDtypeStruct((16, w_ada.shape[1]), F32))(c16, w_ada, b_ada)


def _ada_bwd(c16, w_ada, dmx, dmc):
    def body(c_ref, w_ref, dmx_ref, dmc_ref, gw_ref, pc_ref):
        dmc_tot = dmc_ref[0:1, :]
        for d in range(1, N_DEV):
            dmc_tot = dmc_tot + dmc_ref[d:d + 1, :]
        dm16 = jnp.concatenate([dmx_ref[...], jnp.broadcast_to(dmc_tot, (8, dmc_tot.shape[1]))], axis=0)
        row = lax.broadcasted_iota(jnp.int32, dm16.shape, 0)
        dm16 = jnp.where(row <= 8, dm16, 0.0)
        s = jax.nn.silu(c_ref[...])
        gw_ref[...] = lax.dot_general(s, dm16, (_DIMS["tn"], ((), ())), precision=HI, preferred_element_type=F32)
        pc = lax.dot_general(dm16, w_ref[...], (_DIMS["nt"], ((), ())), precision=HI, preferred_element_type=F32)
        pc_ref[...] = pc[8:9, :]

    return pl.pallas_call(body, name="ada_bwd", out_shape=[jax.ShapeDtypeStruct(w_ada.shape, F32), jax.ShapeDtypeStruct((1, D), F32)],
                          compiler_params=_cp())(c16, w_ada, dmx, dmc)


def _cctx_grad(pc_all, c_ctx_row):
    def body(pc_ref, c_ref, g_ref):
        tot = pc_ref[0]
        for d in range(1, N_DEV):
            tot = tot + pc_ref[d]
        _, vjp = jax.vjp(jax.nn.silu, c_ref[...])
        g_ref[...] = vjp(tot)[0]

    return pl.pallas_call(body, name="cctx_grad", out_shape=jax.ShapeDtypeStruct((1, D), F32))(pc_all, c_ctx_row)


def _adamw(parts, w, m, v, name):
    ns, R, C = parts.shape
    tb = _tile(R, (128, 64, 32, 16, 8))

    def body(p_ref, w_ref, m_ref, v_ref, g_ref, d_ref, mo_ref, vo_ref):
        g = p_ref[0].astype(F32)
        for s in range(1, ns):
            g = g + p_ref[s].astype(F32)
        m2 = ADAM_B1 * m_ref[...] + (1.0 - ADAM_B1) * g
        v2 = ADAM_B2 * v_ref[...] + (1.0 - ADAM_B2) * jnp.square(g)
        m_hat = m2 / (1.0 - ADAM_B1 ** ADAM_STEP)
        v_hat = v2 / (1.0 - ADAM_B2 ** ADAM_STEP)
        g_ref[...] = g
        d_ref[...] = -ADAM_LR * (m_hat / (jnp.sqrt(v_hat) + ADAM_EPS) + ADAM_WD * w_ref[...])
        mo_ref[...] = m2
        vo_ref[...] = v2

    row = pl.BlockSpec((tb, C), lambda i: (i, 0))
    return pl.pallas_call(
        body, grid=(R // tb,), name=name,
        in_specs=[pl.BlockSpec((ns, tb, C), lambda i: (0, i, 0)), row, row, row], out_specs=[row] * 4,
        out_shape=[jax.ShapeDtypeStruct((R, C), F32)] * 4, compiler_params=_cp(),
    )(parts, w, m, v)


_SMALL = (("dmod_x", 6 * D), ("dmod_c", 6 * D), ("b_ada", 6 * D), ("norm_mix", D), ("norm_ffn", D), ("dn_a_log", 16),
          ("dn_dt_bias", 16), ("dn_norm", HD), ("q_norm", HD), ("k_norm", HD), ("attn_sink", NH), ("ffn_conv_b", 2 * DFF),
          ("dn_conv", 5 * 3 * D), ("ffn_conv", 3 * 2 * DFF))
_SMALL_ROWS = -(-sum(n for _, n in _SMALL) // 1024) * 8


def _pack_small(d):
    flat = jnp.concatenate([d[k].reshape(-1).astype(F32) if k in d else jnp.zeros((n,), F32) for k, n in _SMALL])
    return jnp.concatenate([flat, jnp.zeros((_SMALL_ROWS * 128 - flat.shape[0],), F32)]).reshape(_SMALL_ROWS, 128)


def _unpack_small(a):
    flat = a.reshape(a.shape[:-2] + (-1,))
    out, off = {}, 0
    for k, n in _SMALL:
        out[k] = flat[..., off:off + n]
        off += n
    return out


def kernel(x, c, ctx, c_ctx, w_ada, b_ada, norm_mix, norm_ffn, w_in, dn_conv, dn_a_log, dn_dt_bias, dn_norm, q_norm, k_norm, attn_sink, w_branch_dn, w_branch_attn, w_out, ffn_up, ffn_conv, ffn_conv_b, ffn_down, loss_target, m_c_ctx, m_w_ada, m_b_ada, m_norm_mix, m_norm_ffn, m_w_in, m_dn_conv, m_dn_a_log, m_dn_dt_bias, m_dn_norm, m_q_norm, m_k_norm, m_attn_sink, m_w_branch_dn, m_w_branch_attn, m_w_out, m_ffn_up, m_ffn_conv, m_ffn_conv_b, m_ffn_down, v_c_ctx, v_w_ada, v_b_ada, v_norm_mix, v_norm_ffn, v_w_in, v_dn_conv, v_dn_a_log, v_dn_dt_bias, v_dn_norm, v_q_norm, v_k_norm, v_attn_sink, v_w_branch_dn, v_w_branch_attn, v_w_out, v_ffn_up, v_ffn_conv, v_ffn_conv_b, v_ffn_down):
    me = 4 * lax.axis_index("x") + 2 * lax.axis_index("y") + lax.axis_index("c")
    ada_cols = w_ada.shape[2]

    gathered = _exchange([w_in[0].astype(BF), w_branch_dn[0].astype(BF), w_branch_attn[0].astype(BF), w_out[0].astype(BF),
                          ffn_up[0].astype(BF), ffn_down[0].astype(BF), c, dn_conv[0], ffn_conv[0]],
                         scatter=False, name="gather_weights")
    cols = lambda a: jnp.swapaxes(a, 0, 1).reshape(a.shape[1], -1)
    rows = lambda a: a.reshape(-1, a.shape[2])
    w_in_p = _pack_w_in(cols(gathered[0]))
    w_bdn, w_bat, w_o = rows(gathered[1]), rows(gathered[2]), rows(gathered[3])
    w_up, w_down = cols(gathered[4]), rows(gathered[5])
    c_all = gathered[6][:, 0, :]

    c16 = jnp.concatenate([c_all, c_ctx[None], jnp.zeros((7, D), F32)], axis=0)
    b_loc = lax.dynamic_slice_in_dim(b_ada, me * ada_cols, ada_cols, axis=1)
    mod_part = _ada_fwd(c16, w_ada[0], b_loc)
    mod_all = cols(_exchange([mod_part], scatter=False, name="gather_mod")[0])
    mod_x = lax.dynamic_slice_in_dim(mod_all, me, 1, axis=0).reshape(6, D)
    mod_c = mod_all[8].reshape(6, D)

    loss_loc, grad_x, big, small = _local_step(
        x[0], ctx[0], mod_x, mod_c, loss_target[0], w_in_p, w_bdn, w_bat, w_o, w_up, w_down,
        norm_mix, norm_ffn, cols(gathered[7]), dn_a_log[0], dn_dt_bias[0], dn_norm, q_norm, k_norm, attn_sink[0], cols(gathered[8]),
        ffn_conv_b)
    loss = lax.psum(loss_loc, ("x", "y", "c"))

    g_in, g_bdn, g_bat, g_out, g_up, g_down = big
    col_blocks = lambda g: jnp.swapaxes(g.reshape(g.shape[0], N_DEV, -1), 0, 1)
    row_blocks = lambda g: g.reshape(N_DEV, -1, g.shape[1])
    landed = _exchange([col_blocks(_unpack_w_in(g_in)), row_blocks(g_bdn), row_blocks(g_bat), row_blocks(g_out),
                        col_blocks(g_up), row_blocks(g_down)], scatter=True, name="scatter_grads")
    res = {}
    res["w_in"] = _adamw(landed[0], w_in[0], m_w_in[0], v_w_in[0], "adamw_w_in")
    res["w_branch_dn"] = _adamw(landed[1], w_branch_dn[0], m_w_branch_dn[0], v_w_branch_dn[0], "adamw_w_branch_dn")
    res["w_branch_attn"] = _adamw(landed[2], w_branch_attn[0], m_w_branch_attn[0], v_w_branch_attn[0], "adamw_w_branch_attn")
    res["w_out"] = _adamw(landed[3], w_out[0], m_w_out[0], v_w_out[0], "adamw_w_out")
    res["ffn_up"] = _adamw(landed[4], ffn_up[0], m_ffn_up[0], v_ffn_up[0], "adamw_ffn_up")
    res["ffn_down"] = _adamw(landed[5], ffn_down[0], m_ffn_down[0], v_ffn_down[0], "adamw_ffn_down")

    small = dict(small)
    small["b_ada"] = small["dmod_x"] + small["dmod_c"]
    parts = _exchange([_pack_small(small)], scatter=False, name="gather_small")[0]
    per_dev = _unpack_small(parts)
    given = dict(b_ada=(b_ada, m_b_ada, v_b_ada), norm_mix=(norm_mix, m_norm_mix, v_norm_mix), norm_ffn=(norm_ffn, m_norm_ffn, v_norm_ffn),
                 dn_a_log=(dn_a_log, m_dn_a_log, v_dn_a_log), dn_dt_bias=(dn_dt_bias, m_dn_dt_bias, v_dn_dt_bias),
                 dn_norm=(dn_norm, m_dn_norm, v_dn_norm), q_norm=(q_norm, m_q_norm, v_q_norm), k_norm=(k_norm, m_k_norm, v_k_norm),
                 attn_sink=(attn_sink, m_attn_sink, v_attn_sink), ffn_conv_b=(ffn_conv_b, m_ffn_conv_b, v_ffn_conv_b))
    packs = [_pack_small({k: t[j] for k, t in given.items()}) for j in range(3)]
    upd = [_unpack_small(a) for a in _adamw(parts, packs[0], packs[1], packs[2], "adamw_small")]
    for k, t in given.items():
        res[k] = tuple(u[k].reshape(t[0].shape) for u in upd)
    dnc = lax.dynamic_slice_in_dim(upd[0]["dn_conv"].reshape(5, 3 * D), me * dn_conv.shape[2], dn_conv.shape[2], axis=1)
    ffc = lax.dynamic_slice_in_dim(upd[0]["ffn_conv"].reshape(3, 2 * DFF), me * ffn_conv.shape[2], ffn_conv.shape[2], axis=1)
    r8 = lambda a: _pad_rows8(a)
    t = _adamw(r8(dnc)[None], r8(dn_conv[0]), r8(m_dn_conv[0]), r8(v_dn_conv[0]), "adamw_dn_conv")
    res["dn_conv"] = tuple(a[:5][None] for a in t)
    t = _adamw(r8(ffc)[None], r8(ffn_conv[0]), r8(m_ffn_conv[0]), r8(v_ffn_conv[0]), "adamw_ffn_conv")
    res["ffn_conv"] = tuple(a[:3][None] for a in t)

    dmx = lax.dynamic_slice_in_dim(per_dev["dmod_x"], me * ada_cols, ada_cols, axis=1)
    dmc = lax.dynamic_slice_in_dim(per_dev["dmod_c"], me * ada_cols, ada_cols, axis=1)
    g_ada, pc = _ada_bwd(c16, w_ada[0], dmx, dmc)
    res["w_ada"] = _adamw(g_ada[None], w_ada[0], m_w_ada[0], v_w_ada[0], "adamw_w_ada")
    pc_all = _exchange([pc], scatter=False, name="gather_cctx")[0]
    g_cctx = _cctx_grad(pc_all, c_ctx[None])
    r8b = lambda a: jnp.broadcast_to(a, (8, D))
    t = _adamw(r8b(g_cctx)[None], r8b(c_ctx[None]), r8b(m_c_ctx[None]), r8b(v_c_ctx[None]), "adamw_c_ctx")
    res["c_ctx"] = tuple(a[0] for a in t)

    names = ("c_ctx", "w_ada", "b_ada", "norm_mix", "norm_ffn", "w_in", "dn_conv", "dn_a_log", "dn_dt_bias", "dn_norm", "q_norm",
             "k_norm", "attn_sink", "w_branch_dn", "w_branch_attn", "w_out", "ffn_up", "ffn_conv", "ffn_conv_b", "ffn_down")
    lead = ("w_ada", "w_in", "w_branch_dn", "w_branch_attn", "w_out", "ffn_up", "ffn_down")
    fix = lambda k, a: a[None] if k in lead else a
    outs = [loss, grad_x[None]]
    for j in range(4):
        outs += [fix(k, res[k][j]) for k in names]
    return tuple(outs)
```

```python
import functools

import jax
import jax.numpy as jnp
from jax import lax
from jax.experimental import pallas as pl
from jax.experimental.pallas import tpu as pltpu

F32 = jnp.float32
BF = jnp.bfloat16
HI = lax.Precision.HIGHEST
MESH = pl.DeviceIdType.MESH

D = 1024
NH = 8
HD = 128
KVH = 2
GRP = 4
KV = KVH * HD
DFF = 2816
CB = 128
GRID_W = 64
ROPE_BASE = 10000.0
EPS = 1e-6
N_DEV = 8
PW = 8192
O_QKV, O_GT, O_Q, O_MG, O_K, O_V, O_BA = 0, 3072, 4096, 5120, 7168, 7424, 7680
IN_SIZES = (3072, 1024, 16, 16, 1024, 256, 256, 2048)
IN_DIM = sum(IN_SIZES)
ADAM_LR, ADAM_B1, ADAM_B2, ADAM_EPS, ADAM_WD, ADAM_STEP = 0.001, 0.9, 0.999, 1e-08, 0.01, 10
VMEM_LIMIT = 56 * 1024 * 1024


def _cp():
    return pltpu.CompilerParams(vmem_limit_bytes=VMEM_LIMIT)


def _tile(n, cands):
    for c in cands:
        if n % c == 0:
            return c
    return n


def _iota2(shape):
    return lax.broadcasted_iota(jnp.int32, shape, 0), lax.broadcasted_iota(jnp.int32, shape, 1)


_DIMS = {"nn": ((1,), (0,)), "nt": ((1,), (1,)), "tn": ((0,), (0,))}


def _exchange_copies(ins, outs, send_sems, recv_sems, local_sems, scatter):
    x, y, c = lax.axis_index("x"), lax.axis_index("y"), lax.axis_index("c")
    me = 4 * x + 2 * y + c
    local, remote = [], []
    for k in range(len(ins)):
        local.append(pltpu.make_async_copy(ins[k].at[me] if scatter else ins[k], outs[k].at[me], local_sems.at[k]))
        for m in range(1, N_DEV):
            px = 1 - x if m & 4 else x
            py = 1 - y if m & 2 else y
            pc = 1 - c if m & 1 else c
            peer = 4 * px + 2 * py + pc
            src = ins[k].at[peer] if scatter else ins[k]
            sem = k * (N_DEV - 1) + m - 1
            push = pltpu.make_async_remote_copy(src_ref=src, dst_ref=outs[k].at[me], send_sem=send_sems.at[sem],
                                                recv_sem=recv_sems.at[sem], device_id=(px, py, pc), device_id_type=MESH)
            landing = pltpu.make_async_remote_copy(src_ref=src, dst_ref=outs[k].at[peer], send_sem=send_sems.at[sem],
                                                   recv_sem=recv_sems.at[sem], device_id=(px, py, pc), device_id_type=MESH)
            remote.append((push, landing))
    return local, remote


def _exchange_start(*args):
    local, remote = _exchange_copies(*args)
    for cp in local:
        cp.start()
    for push, _ in remote:
        push.start()


def _exchange_wait(*args):
    local, remote = _exchange_copies(*args)
    for _, landing in remote:
        landing.wait_recv()
    for push, _ in remote:
        push.wait_send()
    for cp in local:
        cp.wait()


def _exchange_shapes(arrays, scatter):
    out_shape = [jax.ShapeDtypeStruct(a.shape if scatter else (N_DEV,) + a.shape, a.dtype) for a in arrays]
    n = len(arrays)
    sems = [pltpu.SemaphoreType.DMA((n * (N_DEV - 1),)), pltpu.SemaphoreType.DMA((n * (N_DEV - 1),)), pltpu.SemaphoreType.DMA((n,))]
    return out_shape, sems


def _mm(a, b, *, form, out_dtype, name, tm=None, tn=None, tk=None, exchange=None):
    if form == "tn":
        K, M = a.shape
        N = b.shape[1]
    else:
        M, K = a.shape
        N = b.shape[0] if form == "nt" else b.shape[1]
    tm = tm or _tile(M, (1024, 640, 512, 256, 128))
    tn = tn or _tile(N, (1408, 1024, 512, 256, 128))
    tk = tk or _tile(K, (2048, 1408, 1280, 1024, 640, 512, 256, 128))
    ni, nj, nk = M // tm, N // tn, K // tk
    dims = (_DIMS[form], ((), ()))
    ex_arrays, scatter = exchange if exchange else ([], False)
    nx = len(ex_arrays)

    def body(a_ref, b_ref, *refs):
        ex_in, o_ref, ex_out, scratch = refs[:nx], refs[nx], refs[nx + 1:2 * nx + 1], refs[2 * nx + 1:]
        i, j, k = pl.program_id(0), pl.program_id(1), pl.program_id(2)
        if nx:
            sems = scratch[-3:]

            @pl.when((i == 0) & (j == 0) & (k == 0))
            def _():
                _exchange_start(ex_in, ex_out, *sems, scatter)

        part = lax.dot_general(a_ref[...].astype(BF), b_ref[...].astype(BF), dims, preferred_element_type=F32)
        if nk == 1:
            o_ref[...] = part.astype(out_dtype)
        else:
            acc_ref = scratch[0]

            @pl.when(k == 0)
            def _():
                acc_ref[...] = part

            @pl.when(k > 0)
            def _():
                acc_ref[...] += part

            @pl.when(k == nk - 1)
            def _():
                o_ref[...] = acc_ref[...].astype(out_dtype)

        if nx:
            @pl.when((i == ni - 1) & (j == nj - 1) & (k == nk - 1))
            def _():
                _exchange_wait(ex_in, ex_out, *sems, scatter)

    if form == "tn":
        a_spec = pl.BlockSpec((tk, tm), lambda i, j, k: (k, i))
    else:
        a_spec = pl.BlockSpec((tm, tk), lambda i, j, k: (i, k))
    if form == "nt":
        b_spec = pl.BlockSpec((tn, tk), lambda i, j, k: (j, k))
    else:
        b_spec = pl.BlockSpec((tk, tn), lambda i, j, k: (k, j))
    hbm = pl.BlockSpec(memory_space=pl.ANY)
    ex_shapes, ex_sems = _exchange_shapes(ex_arrays, scatter) if nx else ([], [])
    outs = pl.pallas_call(
        body, grid=(ni, nj, nk), name=name,
        in_specs=[a_spec, b_spec] + [hbm] * nx, out_specs=[pl.BlockSpec((tm, tn), lambda i, j, k: (i, j))] + [hbm] * nx,
        out_shape=[jax.ShapeDtypeStruct((M, N), out_dtype)] + ex_shapes,
        scratch_shapes=([] if nk == 1 else [pltpu.VMEM((tm, tn), F32)]) + ex_sems,
        compiler_params=_cp(),
    )(a, b, *ex_arrays)
    return (outs[0], list(outs[1:])) if nx else outs[0]


def _norm_mod_fn(x, nw, sh, sc):
    y = x * lax.rsqrt(jnp.mean(x * x, axis=-1, keepdims=True) + EPS)
    return (y * nw) * (1.0 + sc) + sh


def _norm_mod_fwd(x, nw, sh, sc, nlat, name):
    T = x.shape[0]
    tb = _tile(T, (256, 128))
    nlb = nlat // tb

    def body(x_ref, nw_ref, sh_ref, sc_ref, h_ref):
        h_ref[...] = _norm_mod_fn(x_ref[...], nw_ref[...], sh_ref[0], sc_ref[0]).astype(BF)

    seg = pl.BlockSpec((1, 1, D), lambda i: (jnp.where(i >= nlb, 1, 0), 0, 0))
    return pl.pallas_call(
        body, grid=(T // tb,), name=name,
        in_specs=[pl.BlockSpec((tb, D), lambda i: (i, 0)), pl.BlockSpec((1, D), lambda i: (0, 0)), seg, seg],
        out_specs=pl.BlockSpec((tb, D), lambda i: (i, 0)),
        out_shape=jax.ShapeDtypeStruct((T, D), BF),
    )(x, nw, sh, sc)


def _norm_mod_bwd(x, nw, sh, sc, dh, nlat, name):
    T = x.shape[0]
    tb = _tile(T, (256, 128))
    nlb = nlat // tb

    def body(x_ref, nw_ref, sh_ref, sc_ref, dh_ref, dx_ref, dnw_ref, dsh_ref, dsc_ref):
        i = pl.program_id(0)
        _, vjp = jax.vjp(_norm_mod_fn, x_ref[...], nw_ref[...], sh_ref[0], sc_ref[0])
        dx, dnw, dsh, dsc = vjp(dh_ref[...])
        dx_ref[...] = dx

        @pl.when(i == 0)
        def _():
            dnw_ref[...] = jnp.zeros_like(dnw_ref)

        @pl.when((i == 0) | (i == nlb))
        def _():
            dsh_ref[...] = jnp.zeros_like(dsh_ref)
            dsc_ref[...] = jnp.zeros_like(dsc_ref)

        dnw_ref[...] += dnw
        dsh_ref[0] += dsh
        dsc_ref[0] += dsc

    seg = pl.BlockSpec((1, 1, D), lambda i: (jnp.where(i >= nlb, 1, 0), 0, 0))
    row = pl.BlockSpec((tb, D), lambda i: (i, 0))
    one = pl.BlockSpec((1, D), lambda i: (0, 0))
    return pl.pallas_call(
        body, grid=(T // tb,), name=name,
        in_specs=[row, one, seg, seg, row], out_specs=[row, one, seg, seg],
        out_shape=[jax.ShapeDtypeStruct((T, D), F32), jax.ShapeDtypeStruct((1, D), F32),
                   jax.ShapeDtypeStruct((2, 1, D), F32), jax.ShapeDtypeStruct((2, 1, D), F32)],
    )(x, nw, sh, sc, dh)


HALO = 8


def _halo_specs(tb, width, nrows, col=0):
    r8 = tb // HALO
    cur = pl.BlockSpec((tb, width), lambda i: (i, col))
    prev = pl.BlockSpec((HALO, width), lambda i: (jnp.maximum(i * r8 - 1, 0), col))
    nxt = pl.BlockSpec((HALO, width), lambda i: (jnp.minimum((i + 1) * r8, nrows // HALO - 1), col))
    return [cur, prev, nxt]


def _segment_edges(seg_rows, tb):
    bounds = [0]
    for s in seg_rows:
        bounds.append(bounds[-1] + s // tb)
    return bounds[:-1], [b - 1 for b in bounds[1:]]


def _keep_halos(i, starts, ends):
    keep_p = functools.reduce(lambda a, b: a & b, [i != s for s in starts])
    keep_n = functools.reduce(lambda a, b: a & b, [i != e for e in ends])
    return keep_p, keep_n


def _ext_rows(refs, cols, keep):
    cur_ref, prev_ref, next_ref = refs
    p = jnp.where(keep[0], prev_ref[:, cols].astype(F32), 0.0)
    n = jnp.where(keep[1], next_ref[:, cols].astype(F32), 0.0)
    return jnp.concatenate([p, cur_ref[:, cols].astype(F32), n], axis=0)


def _conv_rows(xe, w_ref, cols, width, transpose=False):
    r = width // 2
    n = xe.shape[0]
    acc = None
    for j in range(width):
        s = ((j - r) if transpose else (r - j)) % n
        term = (xe if s == 0 else pltpu.roll(xe, s, 0)) * w_ref[j:j + 1, cols]
        acc = term if acc is None else acc + term
    return acc


def _tap_grads(dcur, xe, width, tb):
    r = width // 2
    n = xe.shape[0]
    out = []
    for j in range(width):
        s = (r - j) % n
        xs = (xe if s == 0 else pltpu.roll(xe, s, 0))[HALO:HALO + tb]
        out.append(jnp.sum(dcur * xs, axis=0, keepdims=True))
    return out


def _softplus(x):
    return jnp.maximum(x, 0.0) + jnp.log(1.0 + jnp.exp(-jnp.abs(x)))


def _gates_fn(ba, alog_row, dt_row):
    col = lax.broadcasted_iota(jnp.int32, ba.shape, 1)
    beta = jax.nn.sigmoid(ba)
    g = -jnp.exp(alog_row) * _softplus(ba + dt_row)
    return jnp.where(col < 16, beta, jnp.where(col < 32, g, 0.0))


def _qkv_post_fn(c, kind):
    y = jax.nn.silu(c)
    if kind == 2:
        return y
    n = y * lax.rsqrt(jnp.sum(y * y, axis=-1, keepdims=True) + EPS)
    return n * (HD ** -0.5) if kind == 0 else n


DN_TAPS = 5
FFN_TAPS = 3


def _dn_pre_fwd(p, w8, alog_row, dt_row, seg_rows):
    T = p.shape[0]
    tb = _tile(T, (256, 128))
    starts, ends = _segment_edges(seg_rows, tb)

    def body(cur_ref, prev_ref, next_ref, ba_ref, w_ref, al_ref, dt_ref, q_ref, k_ref, v_ref, gb_ref):
        keep = _keep_halos(pl.program_id(0), starts, ends)
        outs = (q_ref, k_ref, v_ref)
        for kind in range(3):
            for h in range(NH):
                cols = slice(kind * D + h * HD, kind * D + (h + 1) * HD)
                xe = _ext_rows((cur_ref, prev_ref, next_ref), cols, keep)
                conv = _conv_rows(xe, w_ref, cols, DN_TAPS)[HALO:HALO + tb]
                outs[kind][:, h * HD:(h + 1) * HD] = _qkv_post_fn(conv, kind)
        gb_ref[...] = _gates_fn(ba_ref[...], al_ref[...], dt_ref[...])

    row = pl.BlockSpec((tb, D), lambda i: (i, 0))
    one = pl.BlockSpec((1, 128), lambda i: (0, 0))
    return pl.pallas_call(
        body, grid=(T // tb,), name="dn_pre_fwd",
        in_specs=_halo_specs(tb, 3 * D, T) + [pl.BlockSpec((tb, 128), lambda i: (i, O_BA // 128)),
                                              pl.BlockSpec((8, 3 * D), lambda i: (0, 0)), one, one],
        out_specs=[row, row, row, pl.BlockSpec((tb, 128), lambda i: (i, 0))],
        out_shape=[jax.ShapeDtypeStruct((T, D), F32)] * 3 + [jax.ShapeDtypeStruct((T, 128), F32)],
        compiler_params=_cp(),
    )(p, p, p, p, w8, alog_row, dt_row)


def _dn_pre_bwd(p, w8, alog_row, dt_row, dq, dk, dv, dgb, seg_rows):
    T = p.shape[0]
    tb = _tile(T, (256, 128))
    starts, ends = _segment_edges(seg_rows, tb)

    def body(cur_ref, prev_ref, next_ref, ba_ref, w_ref, al_ref, dt_ref,
             dq_c, dq_p, dq_n, dk_c, dk_p, dk_n, dv_c, dv_p, dv_n, dgb_ref, dx_ref, dba_ref, dw_ref, dal_ref, ddt_ref):
        i = pl.program_id(0)
        keep = _keep_halos(i, starts, ends)

        @pl.when(i == 0)
        def _():
            dw_ref[...] = jnp.zeros_like(dw_ref)
            dal_ref[...] = jnp.zeros_like(dal_ref)
            ddt_ref[...] = jnp.zeros_like(ddt_ref)

        douts = ((dq_c, dq_p, dq_n), (dk_c, dk_p, dk_n), (dv_c, dv_p, dv_n))
        for kind in range(3):
            for h in range(NH):
                cols = slice(kind * D + h * HD, kind * D + (h + 1) * HD)
                xe = _ext_rows((cur_ref, prev_ref, next_ref), cols, keep)
                conv = _conv_rows(xe, w_ref, cols, DN_TAPS)
                dye = _ext_rows(douts[kind], slice(h * HD, (h + 1) * HD), keep)
                _, vjp = jax.vjp(functools.partial(_qkv_post_fn, kind=kind), conv)
                dce = vjp(dye)[0]
                dx_ref[:, cols] = _conv_rows(dce, w_ref, cols, DN_TAPS, transpose=True)[HALO:HALO + tb].astype(BF)
                for j, g in enumerate(_tap_grads(dce[HALO:HALO + tb], xe, DN_TAPS, tb)):
                    dw_ref[j:j + 1, cols] += g
        _, vjp = jax.vjp(_gates_fn, ba_ref[...], al_ref[...], dt_ref[...])
        dba, dal, ddt = vjp(dgb_ref[...])
        dba_ref[...] = dba.astype(BF)
        dal_ref[...] += dal
        ddt_ref[...] += ddt

    one = pl.BlockSpec((1, 128), lambda i: (0, 0))
    nar = pl.BlockSpec((tb, 128), lambda i: (i, 0))
    wspec = pl.BlockSpec((8, 3 * D), lambda i: (0, 0))
    return pl.pallas_call(
        body, grid=(T // tb,), name="dn_pre_bwd",
        in_specs=_halo_specs(tb, 3 * D, T) + [pl.BlockSpec((tb, 128), lambda i: (i, O_BA // 128)), wspec, one, one]
        + _halo_specs(tb, D, T) * 3 + [nar],
        out_specs=[pl.BlockSpec((tb, 3 * D), lambda i: (i, 0)), nar, wspec, one, one],
        out_shape=[jax.ShapeDtypeStruct((T, 3 * D), BF), jax.ShapeDtypeStruct((T, 128), BF), jax.ShapeDtypeStruct((8, 3 * D), F32),
                   jax.ShapeDtypeStruct((1, 128), F32), jax.ShapeDtypeStruct((1, 128), F32)],
        compiler_params=_cp(),
    )(p, p, p, p, w8, alog_row, dt_row, dq, dq, dq, dk, dk, dk, dv, dv, dv, dgb)


def _dot_hi(a, b):
    return jnp.dot(a, b, precision=HI, preferred_element_type=F32)


def _dot_bf(a, b):
    return jnp.dot(a.astype(BF), b.astype(BF), preferred_element_type=F32)


def _dot_nt_bf(a, b):
    return lax.dot_general(a.astype(BF), b.astype(BF), (_DIMS["nt"], ((), ())), preferred_element_type=F32)


def _dot_tn_bf(a, b):
    return lax.dot_general(a.astype(BF), b.astype(BF), (_DIMS["tn"], ((), ())), preferred_element_type=F32)


def _dot_h3(a, b):
    return jnp.dot(a, b, precision=lax.Precision.HIGH, preferred_element_type=F32)


def _unit_tri_inverses(mats):
    r, c = _iota2((CB, CB))
    eye = (r == c).astype(F32)
    a8 = [jnp.where((r // 8) == (c // 8), a, 0.0) for a in mats]
    a2 = [_dot_h3(x, x) for x in a8]
    a4 = [_dot_h3(x, x) for x in a2]
    t = [_dot_h3(eye - x, eye + y) for x, y in zip(a8, a2)]
    t = [_dot_h3(x, eye + y) for x, y in zip(t, a4)]
    b = 8
    while b < CB:
        mask = ((r // (2 * b)) == (c // (2 * b))) & ((r // b) != (c // b))
        te = [_dot_h3(x, jnp.where(mask, a, 0.0)) for x, a in zip(t, mats)]
        t = [x - _dot_h3(y, x) for x, y in zip(t, te)]
        b *= 2
    return t


@jax.custom_vjp
def _saved_inverse(a, t):
    return t


_saved_inverse.defvjp(lambda a, t: (t, t), lambda t, dt: (-_dot_h3(_dot_h3(t.T, dt), t.T), jnp.zeros_like(t)))


def _dn1_decay(gc, reverse):
    r, c = _iota2((CB, CB))
    incl = (c >= r) if reverse else (c <= r)
    return jnp.where(incl, jnp.exp(jnp.where(incl, gc - gc.T, 0.0)), 0.0)


def _dn1_heads(qs, ks, vs, betas, gcs, ts_saved, reverse, kks=None, qks=None):
    r, c = _iota2((CB, CB))
    strict = (c > r) if reverse else (c < r)
    decays = [_dn1_decay(gc, reverse) for gc in gcs]
    kks = kks or [_dot_nt_bf(k, k) for k in ks]
    systems = [jnp.where(strict, b * kk * dc, 0.0) for b, kk, dc in zip(betas, kks, decays)]
    if ts_saved is None:
        ts = _unit_tri_inverses(systems)
    else:
        ts = [_saved_inverse(a, t) for a, t in zip(systems, ts_saved)]
    egs = [jnp.exp(gc) for gc in gcs]
    us = [_dot_h3(t, v * b) for t, v, b in zip(ts, vs, betas)]
    ws = [_dot_h3(t, k * (b * eg)) for t, k, b, eg in zip(ts, ks, betas, egs)]
    qks = qks or [_dot_nt_bf(q, k) for q, k in zip(qs, ks)]
    last = 0 if reverse else CB - 1
    glogs = [jnp.sum(jnp.where(r == last, gc, 0.0), axis=0, keepdims=True) for gc in gcs]
    outs = [(u, w, q * eg, k * jnp.exp(gl - gc), qk * dc, jnp.exp(gl))
            for u, w, q, k, eg, gl, gc, qk, dc in zip(us, ws, qs, ks, egs, glogs, gcs, qks, decays)]
    return outs, ts


def _cum_matrix(upper):
    r, c = _iota2((CB, CB))
    return ((c >= r) if upper else (c <= r)).astype(F32)


def _lane_bcast(x, col):
    return jnp.broadcast_to(x[:, col:col + 1], x.shape)


_HEAD_SLICES = [slice(h * HD, (h + 1) * HD) for h in range(NH)]


def _dn1_fwd(q, k, v, gb):
    T = q.shape[0]
    nb = T // CB

    def body(q_ref, k_ref, v_ref, gb_ref, *out_refs):
        gbv = gb_ref[...]
        qs = [q_ref[:, sl] for sl in _HEAD_SLICES]
        ks = [k_ref[:, sl] for sl in _HEAD_SLICES]
        vs = [v_ref[:, sl] for sl in _HEAD_SLICES]
        kks = [_dot_nt_bf(x, x) for x in ks]
        qks = [_dot_nt_bf(x, y) for x, y in zip(qs, ks)]
        for d in (0, 1):
            u_ref, w_ref, qg_ref, kd_ref, qkd_ref, gl_ref, t_ref = out_refs[7 * d:7 * d + 7]
            gcum = _dot_h3(_cum_matrix(d == 1), gbv)
            betas = [_lane_bcast(gbv, d * NH + h) for h in range(NH)]
            gcs = [_lane_bcast(gcum, 16 + d * NH + h) for h in range(NH)]
            outs, ts = _dn1_heads(qs, ks, vs, betas, gcs, None, d == 1, kks, qks)
            for h, sl in enumerate(_HEAD_SLICES):
                u, w, qg, kd, qkd, gl = outs[h]
                u_ref[:, sl] = u
                w_ref[:, sl] = w.astype(BF)
                qg_ref[:, sl] = qg.astype(BF)
                kd_ref[:, sl] = kd.astype(BF)
                qkd_ref[:, sl] = qkd.astype(BF)
                gl_ref[h] = gl
                t_ref[:, sl] = ts[h]

    tb = pl.BlockSpec((CB, D), lambda i: (i, 0))
    one_dir_specs = [tb, tb, tb, tb, tb, pl.BlockSpec((NH, 1, 128), lambda i: (i, 0, 0)), tb]
    one_dir_shapes = ([jax.ShapeDtypeStruct((T, D), F32)] + [jax.ShapeDtypeStruct((T, D), BF)] * 4
                      + [jax.ShapeDtypeStruct((nb * NH, 1, 128), F32), jax.ShapeDtypeStruct((T, D), F32)])
    outs = pl.pallas_call(
        body, grid=(nb,), name="dn1_fwd",
        in_specs=[tb, tb, tb, pl.BlockSpec((CB, 128), lambda i: (i, 0))],
        out_specs=one_dir_specs * 2, out_shape=one_dir_shapes * 2, compiler_params=_cp(),
    )(q, k, v, gb)
    return [tuple(outs[:7]), tuple(outs[7:])]


def _dn1_bwd(q, k, v, gb, tinvs, cots):
    T = q.shape[0]
    nb = T // CB

    def body(q_ref, k_ref, v_ref, gb_ref, *refs):
        dir_refs, (dq_ref, dk_ref, dv_ref, dgb_ref) = refs[:14], refs[14:]
        gbv = gb_ref[...]
        qs = [q_ref[:, sl] for sl in _HEAD_SLICES]
        ks = [k_ref[:, sl] for sl in _HEAD_SLICES]
        vs = [v_ref[:, sl] for sl in _HEAD_SLICES]
        lane = lax.broadcasted_iota(jnp.int32, (CB, 128), 1)
        dgb = jnp.zeros((CB, 128), F32)
        for d in (0, 1):
            t_ref, du_ref, dw_ref, dqg_ref, dkd_ref, dqkd_ref, dgl_ref = dir_refs[7 * d:7 * d + 7]
            gcum = _dot_h3(_cum_matrix(d == 1), gbv)
            betas = [_lane_bcast(gbv, d * NH + h) for h in range(NH)]
            gcs = [_lane_bcast(gcum, 16 + d * NH + h) for h in range(NH)]
            ts = [t_ref[:, sl] for sl in _HEAD_SLICES]
            f = lambda qs, ks, vs, betas, gcs: _dn1_heads(qs, ks, vs, betas, gcs, ts, d == 1)[0]
            _, vjp = jax.vjp(f, qs, ks, vs, betas, gcs)
            cot = [(du_ref[:, sl], dw_ref[:, sl], dqg_ref[:, sl], dkd_ref[:, sl], dqkd_ref[:, sl], dgl_ref[h])
                   for h, sl in enumerate(_HEAD_SLICES)]
            dqs, dks, dvs, dbetas, dgcs = vjp(cot)
            dgcum = jnp.zeros((CB, 128), F32)
            for h, sl in enumerate(_HEAD_SLICES):
                if d == 0:
                    dq_ref[:, sl] = dqs[h]
                    dk_ref[:, sl] = dks[h]
                    dv_ref[:, sl] = dvs[h]
                else:
                    dq_ref[:, sl] += dqs[h]
                    dk_ref[:, sl] += dks[h]
                    dv_ref[:, sl] += dvs[h]
                dgb = dgb + jnp.where(lane == d * NH + h, jnp.sum(dbetas[h], axis=1, keepdims=True), 0.0)
                dgcum = dgcum + jnp.where(lane == 16 + d * NH + h, jnp.sum(dgcs[h], axis=1, keepdims=True), 0.0)
            dgb = dgb + _dot_h3(_cum_matrix(d == 0), dgcum)
        dgb_ref[...] = dgb

    tb = pl.BlockSpec((CB, D), lambda i: (i, 0))
    gbs = pl.BlockSpec((CB, 128), lambda i: (i, 0))
    gls = pl.BlockSpec((NH, 1, 128), lambda i: (i, 0, 0))
    args = []
    for d in (0, 1):
        args += [tinvs[d], *cots[d]]
    return pl.pallas_call(
        body, grid=(nb,), name="dn1_bwd",
        in_specs=[tb, tb, tb, gbs] + [tb, tb, tb, tb, tb, tb, gls] * 2, out_specs=[tb, tb, tb, gbs],
        out_shape=[jax.ShapeDtypeStruct((T, D), F32)] * 3 + [jax.ShapeDtypeStruct((T, 128), F32)],
        compiler_params=_cp(),
    )(q, k, v, gb, *args)


def _dn2_step(u, w, qg, kd, qkd, glrow, s):
    v_new = u - _dot_bf(w, s)
    o = _dot_bf(qg, s) + _dot_bf(qkd, v_new)
    return o, s * glrow + _dot_tn_bf(kd, v_new)


def _scan_order(direction, nlat_b, nall_b):
    if direction == 0:
        return lambda i: (i + nlat_b) % nall_b
    return lambda i: nall_b - 1 - i


def _dn2_fwd(per_dir, nlat):
    T = per_dir[0][0].shape[0]
    nb = T // CB
    blks = [_scan_order(d, nlat // CB, nb) for d in (0, 1)]

    def body(*refs):
        ins, outs, s_scr = refs[:12], refs[12:16], refs[16]

        @pl.when(pl.program_id(0) == 0)
        def _():
            s_scr[...] = jnp.zeros_like(s_scr)
        for d in (0, 1):
            outs[2 * d + 1][0] = s_scr[d]
        for h, sl in enumerate(_HEAD_SLICES):
            for d in (0, 1):
                u_ref, w_ref, qg_ref, kd_ref, qkd_ref, gl_ref = ins[6 * d:6 * d + 6]
                o, s_next = _dn2_step(u_ref[:, sl], w_ref[:, sl], qg_ref[:, sl], kd_ref[:, sl], qkd_ref[:, sl], gl_ref[h], s_scr[d, h])
                outs[2 * d][:, sl] = o
                s_scr[d, h] = s_next

    in_specs, out_specs, args = [], [], []
    for d in (0, 1):
        blk = blks[d]
        tb = pl.BlockSpec((CB, D), lambda i, blk=blk: (blk(i), 0))
        in_specs += [tb] * 5 + [pl.BlockSpec((NH, 1, 128), lambda i, blk=blk: (blk(i), 0, 0))]
        out_specs += [tb, pl.BlockSpec((1, NH, HD, HD), lambda i, blk=blk: (blk(i), 0, 0, 0))]
        args += list(per_dir[d])
    outs = pl.pallas_call(
        body, grid=(nb,), name="dn2_fwd", in_specs=in_specs, out_specs=out_specs,
        out_shape=[jax.ShapeDtypeStruct((T, D), F32), jax.ShapeDtypeStruct((nb, NH, HD, HD), F32)] * 2,
        scratch_shapes=[pltpu.VMEM((2, NH, HD, HD), F32)], compiler_params=_cp(),
    )(*args)
    return [tuple(outs[:2]), tuple(outs[2:])]


def _dn2_bwd(per_dir, do, nlat):
    T = per_dir[0][0].shape[0]
    nb = T // CB
    nlat_b = nlat // CB
    fwd = [_scan_order(d, nlat_b, nb) for d in (0, 1)]
    blks = [lambda i, f=f: f(nb - 1 - i) for f in fwd]

    def body(*refs):
        ins, outs, ds_scr = refs[:16], refs[16:28], refs[28]
        i = pl.program_id(0)

        @pl.when(i == 0)
        def _():
            ds_scr[...] = jnp.zeros_like(ds_scr)
        for h, sl in enumerate(_HEAD_SLICES):
            for d in (0, 1):
                u_ref, w_ref, qg_ref, kd_ref, qkd_ref, gl_ref, sall_ref, do_ref = ins[8 * d:8 * d + 8]
                du_ref, dw_ref, dqg_ref, dkd_ref, dqkd_ref, dgl_ref = outs[6 * d:6 * d + 6]
                args = (u_ref[:, sl], w_ref[:, sl].astype(F32), qg_ref[:, sl].astype(F32), kd_ref[:, sl].astype(F32),
                        qkd_ref[:, sl].astype(F32), gl_ref[h], sall_ref[0, h])
                _, vjp = jax.vjp(_dn2_step, *args)
                is_lat = blks[d](i) < nlat_b
                du, dw, dqg, dkd, dqkd, dgl, ds = vjp((jnp.where(is_lat, do_ref[:, sl], 0.0), ds_scr[d, h]))
                du_ref[:, sl] = du
                dw_ref[:, sl] = dw
                dqg_ref[:, sl] = dqg
                dkd_ref[:, sl] = dkd
                dqkd_ref[:, sl] = dqkd
                dgl_ref[h] = dgl
                ds_scr[d, h] = ds

    in_specs, out_specs, args = [], [], []
    for d in (0, 1):
        blk = blks[d]
        tb = pl.BlockSpec((CB, D), lambda i, blk=blk: (blk(i), 0))
        gls = pl.BlockSpec((NH, 1, 128), lambda i, blk=blk: (blk(i), 0, 0))
        in_specs += [tb] * 5 + [gls, pl.BlockSpec((1, NH, HD, HD), lambda i, blk=blk: (blk(i), 0, 0, 0)),
                                pl.BlockSpec((CB, D), lambda i, blk=blk: (jnp.minimum(blk(i), nlat_b - 1), 0))]
        out_specs += [tb] * 5 + [gls]
        args += list(per_dir[d]) + [do]
    outs = pl.pallas_call(
        body, grid=(nb,), name="dn2_bwd", in_specs=in_specs, out_specs=out_specs,
        out_shape=([jax.ShapeDtypeStruct((T, D), F32)] * 5 + [jax.ShapeDtypeStruct((nb * NH, 1, 128), F32)]) * 2,
        scratch_shapes=[pltpu.VMEM((2, NH, HD, HD), F32)], compiler_params=_cp(),
    )(*args)
    return [tuple(outs[:6]), tuple(outs[6:])]


def _ghn_fn(o, gt, w):
    y = o * lax.rsqrt(jnp.mean(o * o, axis=-1, keepdims=True) + EPS)
    return (y * w) * jax.nn.silu(gt)


def _ghn_fwd(o_f, o_b, p, w, nlat):
    tb = _tile(nlat, (256, 128))

    def body(of_ref, ob_ref, gt_ref, w_ref, y_ref):
        for h in range(NH):
            sl = slice(h * HD, (h + 1) * HD)
            y_ref[:, sl] = _ghn_fn(of_ref[:, sl] + ob_ref[:, sl], gt_ref[:, sl], w_ref[...]).astype(BF)

    row = pl.BlockSpec((tb, D), lambda i: (i, 0))
    return pl.pallas_call(
        body, grid=(nlat // tb,), name="ghn_fwd",
        in_specs=[row, row, pl.BlockSpec((tb, D), lambda i: (i, O_GT // D)), pl.BlockSpec((1, HD), lambda i: (0, 0))],
        out_specs=row, out_shape=jax.ShapeDtypeStruct((nlat, D), BF),
    )(o_f, o_b, p, w)


def _ghn_bwd(o_f, o_b, p, w, dy, nlat):
    tb = _tile(nlat, (256, 128))

    def body(of_ref, ob_ref, gt_ref, w_ref, dy_ref, do_ref, dgt_ref, dw_ref):
        @pl.when(pl.program_id(0) == 0)
        def _():
            dw_ref[...] = jnp.zeros_like(dw_ref)
        for h in range(NH):
            sl = slice(h * HD, (h + 1) * HD)
            _, vjp = jax.vjp(_ghn_fn, of_ref[:, sl] + ob_ref[:, sl], gt_ref[:, sl], w_ref[...])
            do, dgt, dw = vjp(dy_ref[:, sl])
            do_ref[:, sl] = do
            dgt_ref[:, sl] = dgt.astype(BF)
            dw_ref[...] += dw

    row = pl.BlockSpec((tb, D), lambda i: (i, 0))
    one = pl.BlockSpec((1, HD), lambda i: (0, 0))
    return pl.pallas_call(
        body, grid=(nlat // tb,), name="ghn_bwd",
        in_specs=[row, row, pl.BlockSpec((tb, D), lambda i: (i, O_GT // D)), one, row],
        out_specs=[row, row, one],
        out_shape=[jax.ShapeDtypeStruct((nlat, D), F32), jax.ShapeDtypeStruct((nlat, D), BF), jax.ShapeDtypeStruct((1, HD), F32)],
    )(o_f, o_b, p, w, dy)


@jax.custom_vjp
def _swap32(x):
    lane = lax.broadcasted_iota(jnp.int32, x.shape, 1)
    return jnp.where((lane & 32) == 0, pltpu.roll(x, 96, 1), pltpu.roll(x, 32, 1))


_swap32.defvjp(lambda x: (_swap32(x), None), lambda _, g: (_swap32(g),))


def _qk_post_fn(x, w, cos, sin):
    y = (x * lax.rsqrt(jnp.mean(x * x, axis=-1, keepdims=True) + EPS)) * w
    return y * cos + _swap32(y) * sin


def _attn_prep_fwd(p, qn, kn, cos, sin):
    T = p.shape[0]
    tb = _tile(T, (256, 128))

    def body(q_ref, k_ref, v_ref, qn_ref, kn_ref, cos_ref, sin_ref, qr_ref, kr_ref, vb_ref):
        cos_v, sin_v = cos_ref[...], sin_ref[...]
        for h in range(NH):
            sl = slice(h * HD, (h + 1) * HD)
            qr_ref[:, sl] = _qk_post_fn(q_ref[:, sl], qn_ref[...], cos_v, sin_v).astype(BF)
        for h in range(KVH):
            sl = slice(h * HD, (h + 1) * HD)
            kr_ref[:, sl] = _qk_post_fn(k_ref[:, sl], kn_ref[...], cos_v, sin_v).astype(BF)
        vb_ref[...] = v_ref[...].astype(BF)

    one = pl.BlockSpec((1, HD), lambda i: (0, 0))
    tab = pl.BlockSpec((tb, HD), lambda i: (i, 0))
    return pl.pallas_call(
        body, grid=(T // tb,), name="attn_prep_fwd",
        in_specs=[pl.BlockSpec((tb, D), lambda i: (i, O_Q // D)), pl.BlockSpec((tb, KV), lambda i: (i, O_K // KV)),
                  pl.BlockSpec((tb, KV), lambda i: (i, O_V // KV)), one, one, tab, tab],
        out_specs=[pl.BlockSpec((tb, D), lambda i: (i, 0)), pl.BlockSpec((tb, KV), lambda i: (i, 0)),
                   pl.BlockSpec((tb, KV), lambda i: (i, 0))],
        out_shape=[jax.ShapeDtypeStruct((T, D), BF), jax.ShapeDtypeStruct((T, KV), BF), jax.ShapeDtypeStruct((T, KV), BF)],
    )(p, p, p, qn, kn, cos, sin)


def _attn_prep_bwd(p, qn, kn, cos, sin, dqr, dkp, dvp, dkc, dvc, nlat):
    T = p.shape[0]
    nqb = nlat // CB
    ncb = (T - nlat) // CB

    def body(q_ref, k_ref, v_ref, qn_ref, kn_ref, cos_ref, sin_ref, dqr_ref, dka_ref, dkb_ref, dkc3_ref, dva_ref, dvb_ref, dvc3_ref,
             dkctx_ref, dvctx_ref, dq_ref, dk_ref, dv_ref, dqn_ref, dkn_ref):
        i = pl.program_id(0)
        is_lat = i < nqb
        cos_v, sin_v = cos_ref[...], sin_ref[...]

        @pl.when(i == 0)
        def _():
            dqn_ref[...] = jnp.zeros_like(dqn_ref)
            dkn_ref[...] = jnp.zeros_like(dkn_ref)

        def band_sum(a_ref, b_ref, c_ref, ctx_ref):
            s = b_ref[0] + jnp.where(i > 0, a_ref[0], 0.0) + jnp.where(i < nqb - 1, c_ref[0], 0.0)
            return jnp.where(is_lat, s, ctx_ref[...])

        dkr = band_sum(dka_ref, dkb_ref, dkc3_ref, dkctx_ref)
        dv_ref[...] = band_sum(dva_ref, dvb_ref, dvc3_ref, dvctx_ref).astype(BF)
        for h in range(NH):
            sl = slice(h * HD, (h + 1) * HD)
            _, vjp = jax.vjp(_qk_post_fn, q_ref[:, sl], qn_ref[...], cos_v, sin_v)
            dq, dqn, _, _ = vjp(jnp.where(is_lat, dqr_ref[:, sl], 0.0))
            dq_ref[:, sl] = dq.astype(BF)
            dqn_ref[...] += dqn
        for h in range(KVH):
            sl = slice(h * HD, (h + 1) * HD)
            _, vjp = jax.vjp(_qk_post_fn, k_ref[:, sl], kn_ref[...], cos_v, sin_v)
            dk, dkn, _, _ = vjp(dkr[:, sl])
            dk_ref[:, sl] = dk.astype(BF)
            dkn_ref[...] += dkn

    one = pl.BlockSpec((1, HD), lambda i: (0, 0))
    tab = pl.BlockSpec((CB, HD), lambda i: (i, 0))
    lat = lambda i: jnp.minimum(i, nqb - 1)

    def part(off, slot):
        return pl.BlockSpec((1, CB, KV), lambda i: (jnp.clip(lat(i) + off, 0, nqb - 1) * 3 + slot, 0, 0))

    ctxs = pl.BlockSpec((CB, KV), lambda i: (jnp.clip(i - nqb, 0, ncb - 1), 0))
    kvs = pl.BlockSpec((CB, KV), lambda i: (i, 0))
    return pl.pallas_call(
        body, grid=(T // CB,), name="attn_prep_bwd",
        in_specs=[pl.BlockSpec((CB, D), lambda i: (i, O_Q // D)), pl.BlockSpec((CB, KV), lambda i: (i, O_K // KV)),
                  pl.BlockSpec((CB, KV), lambda i: (i, O_V // KV)), one, one, tab, tab,
                  pl.BlockSpec((CB, D), lambda i: (lat(i), 0)),
                  part(-1, 2), part(0, 1), part(1, 0), part(-1, 2), part(0, 1), part(1, 0), ctxs, ctxs],
        out_specs=[pl.BlockSpec((CB, D), lambda i: (i, 0)), kvs, kvs, one, one],
        out_shape=[jax.ShapeDtypeStruct((T, D), BF), jax.ShapeDtypeStruct((T, KV), BF), jax.ShapeDtypeStruct((T, KV), BF),
                   jax.ShapeDtypeStruct((1, HD), F32), jax.ShapeDtypeStruct((1, HD), F32)],
    )(p, p, p, qn, kn, cos, sin, dqr, dkp, dkp, dkp, dvp, dvp, dvp, dkc, dvc)


def _attn_group_fn(q0, q1, q2, q3, kall, vall, s0, s1, s2, s3, bias):
    q = jnp.concatenate([q0, q1, q2, q3], axis=0)
    s = _dot_nt_bf(q, kall) * (HD ** -0.5) + bias
    sk = jnp.concatenate([jnp.broadcast_to(jnp.mean(t, axis=1, keepdims=True), (CB, 1)) for t in (s0, s1, s2, s3)], axis=0)
    m = lax.stop_gradient(jnp.maximum(jnp.max(s, axis=1, keepdims=True), sk))
    e = jnp.exp(s - m)
    den = jnp.sum(e, axis=1, keepdims=True) + jnp.exp(sk - m)
    return _dot_bf(e / den, vall)


def _attn_bias(lc):
    r, c = _iota2((GRP * CB, 3 * CB + lc))
    rel = c - (r & (CB - 1))
    win = (rel >= 0) & (rel <= 2 * CB)
    ctx = c >= 3 * CB
    seen = [(win & (c >= CB)) | ctx, win | ctx, (win & (c < 2 * CB)) | ctx]
    return jnp.stack([jnp.where(s, 0.0, -1e30) for s in seen]).astype(F32)


def _attn_specs(nqb, lc, nlat):
    assert nqb >= 2
    qs = pl.BlockSpec((CB, GRP * HD), lambda kh, i: (i, kh))
    ka = pl.BlockSpec((CB, HD), lambda kh, i: (jnp.maximum(i - 1, 0), kh))
    kb = pl.BlockSpec((CB, HD), lambda kh, i: (i, kh))
    kc = pl.BlockSpec((CB, HD), lambda kh, i: (jnp.minimum(i + 1, nqb - 1), kh))
    kx = pl.BlockSpec((lc, HD), lambda kh, i: (nlat // lc, kh))
    sk = pl.BlockSpec((1, 8, 128), lambda kh, i: (kh, 0, 0))
    bs = pl.BlockSpec((1, GRP * CB, 3 * CB + lc), lambda kh, i: (jnp.where(i == 0, 0, jnp.where(i == nqb - 1, 2, 1)), 0, 0))
    return qs, ka, kb, kc, kx, sk, bs


def _attn_fwd(qr, kr, vb, sink, nlat):
    lc = kr.shape[0] - nlat
    nqb = nlat // CB
    qs, ka, kb, kc, kx, sk, bs = _attn_specs(nqb, lc, nlat)

    def body(q_ref, ka_ref, kb_ref, kc_ref, kx_ref, va_ref, vb_ref, vc_ref, vx_ref, sk_ref, bias_ref, o_ref):
        kall = jnp.concatenate([ka_ref[...], kb_ref[...], kc_ref[...], kx_ref[...]], axis=0)
        vall = jnp.concatenate([va_ref[...], vb_ref[...], vc_ref[...], vx_ref[...]], axis=0)
        qh = [q_ref[:, g * HD:(g + 1) * HD] for g in range(GRP)]
        sinks = [sk_ref[0, g:g + 1, :] for g in range(GRP)]
        o = _attn_group_fn(*qh, kall, vall, *sinks, bias_ref[0])
        for g in range(GRP):
            o_ref[:, g * HD:(g + 1) * HD] = o[g * CB:(g + 1) * CB].astype(BF)

    return pl.pallas_call(
        body, grid=(KVH, nqb), name="attn_fwd",
        in_specs=[qs, ka, kb, kc, kx, ka, kb, kc, kx, sk, bs], out_specs=qs,
        out_shape=jax.ShapeDtypeStruct((nlat, D), BF), compiler_params=_cp(),
    )(qr, kr, kr, kr, kr, vb, vb, vb, vb, sink, _attn_bias(lc))


def _attn_bwd(qr, kr, vb, sink, dy, nlat):
    lc = kr.shape[0] - nlat
    nqb = nlat // CB
    qs, ka, kb, kc, kx, sk, bs = _attn_specs(nqb, lc, nlat)

    def body(q_ref, ka_ref, kb_ref, kc_ref, kx_ref, va_ref, vb_ref, vc_ref, vx_ref, sk_ref, dy_ref, bias_ref,
             dq_ref, dkp_ref, dvp_ref, dkx_ref, dvx_ref, dsk_ref):
        i = pl.program_id(1)
        kall = jnp.concatenate([ka_ref[...], kb_ref[...], kc_ref[...], kx_ref[...]], axis=0).astype(F32)
        vall = jnp.concatenate([va_ref[...], vb_ref[...], vc_ref[...], vx_ref[...]], axis=0).astype(F32)
        qh = [q_ref[:, g * HD:(g + 1) * HD].astype(F32) for g in range(GRP)]
        f = functools.partial(_attn_group_fn, bias=bias_ref[0])
        _, vjp = jax.vjp(f, *qh, kall, vall, *[sk_ref[0, g:g + 1, :] for g in range(GRP)])
        dyv = jnp.concatenate([dy_ref[:, g * HD:(g + 1) * HD] for g in range(GRP)], axis=0)
        d = vjp(dyv)
        for g in range(GRP):
            dq_ref[:, g * HD:(g + 1) * HD] = d[g]
        dk, dv = d[4], d[5]
        for t in range(3):
            dkp_ref[t] = dk[t * CB:(t + 1) * CB]
            dvp_ref[t] = dv[t * CB:(t + 1) * CB]

        @pl.when(i == 0)
        def _():
            dkx_ref[...] = jnp.zeros_like(dkx_ref)
            dvx_ref[...] = jnp.zeros_like(dvx_ref)
            dsk_ref[...] = jnp.zeros_like(dsk_ref)
        dkx_ref[...] += dk[3 * CB:]
        dvx_ref[...] += dv[3 * CB:]
        for g in range(GRP):
            dsk_ref[0, g:g + 1, :] += d[6 + g]

    dys = pl.BlockSpec((CB, GRP * HD), lambda kh, i: (i, kh))
    parts = pl.BlockSpec((3, CB, HD), lambda kh, i: (i, 0, kh))
    ctxo = pl.BlockSpec((lc, HD), lambda kh, i: (0, kh))
    return pl.pallas_call(
        body, grid=(KVH, nqb), name="attn_bwd",
        in_specs=[qs, ka, kb, kc, kx, ka, kb, kc, kx, sk, dys, bs],
        out_specs=[dys, parts, parts, ctxo, ctxo, sk],
        out_shape=[jax.ShapeDtypeStruct((nlat, D), F32), jax.ShapeDtypeStruct((3 * nqb, CB, KV), F32),
                   jax.ShapeDtypeStruct((3 * nqb, CB, KV), F32), jax.ShapeDtypeStruct((lc, KV), F32),
                   jax.ShapeDtypeStruct((lc, KV), F32), jax.ShapeDtypeStruct((KVH, 8, 128), F32)],
        compiler_params=_cp(),
    )(qr, kr, kr, kr, kr, vb, vb, vb, vb, sink, dy, _attn_bias(lc))


def _merge_fn(z_dn, z_at, g_dn, g_at):
    return jax.nn.sigmoid(g_dn) * z_dn + jax.nn.sigmoid(g_at) * z_at


def _merge_fwd(z_dn, z_at, p, nlat):
    tb = _tile(nlat, (256, 128))

    def body(zd_ref, za_ref, gd_ref, ga_ref, o_ref):
        o_ref[...] = _merge_fn(zd_ref[...], za_ref[...], gd_ref[...], ga_ref[...]).astype(BF)

    row = pl.BlockSpec((tb, D), lambda i: (i, 0))
    return pl.pallas_call(
        body, grid=(nlat // tb,), name="merge_fwd",
        in_specs=[row, row, pl.BlockSpec((tb, D), lambda i: (i, O_MG // D)), pl.BlockSpec((tb, D), lambda i: (i, O_MG // D + 1))],
        out_specs=row, out_shape=jax.ShapeDtypeStruct((nlat, D), BF),
    )(z_dn, z_at, p, p)


def _merge_bwd(z_dn, z_at, p, dm, nlat):
    tb = _tile(nlat, (256, 128))

    def body(zd_ref, za_ref, gd_ref, ga_ref, dm_ref, dzd_ref, dza_ref, dg_ref):
        _, vjp = jax.vjp(_merge_fn, zd_ref[...], za_ref[...], gd_ref[...], ga_ref[...])
        dzd, dza, dgd, dga = vjp(dm_ref[...])
        dzd_ref[...] = dzd.astype(BF)
        dza_ref[...] = dza.astype(BF)
        dg_ref[:, :D] = dgd.astype(BF)
        dg_ref[:, D:] = dga.astype(BF)

    row = pl.BlockSpec((tb, D), lambda i: (i, 0))
    return pl.pallas_call(
        body, grid=(nlat // tb,), name="merge_bwd",
        in_specs=[row, row, pl.BlockSpec((tb, D), lambda i: (i, O_MG // D)), pl.BlockSpec((tb, D), lambda i: (i, O_MG // D + 1)), row],
        out_specs=[row, row, pl.BlockSpec((tb, 2 * D), lambda i: (i, 0))],
        out_shape=[jax.ShapeDtypeStruct((nlat, D), BF), jax.ShapeDtypeStruct((nlat, D), BF), jax.ShapeDtypeStruct((nlat, 2 * D), BF)],
    )(z_dn, z_at, p, p, dm)


def _resid_fwd(x, gate, y):
    n = y.shape[0]
    tb = _tile(n, (256, 128))

    def body(x_ref, g_ref, y_ref, o_ref):
        o_ref[...] = x_ref[...] + g_ref[...] * y_ref[...]

    row = pl.BlockSpec((tb, D), lambda i: (i, 0))
    return pl.pallas_call(
        body, grid=(n // tb,), name="resid_fwd",
        in_specs=[row, pl.BlockSpec((1, D), lambda i: (0, 0)), row], out_specs=row,
        out_shape=jax.ShapeDtypeStruct((n, D), F32),
    )(x, gate, y)


def _resid_bwd(dx1a, dx1b, gate, y):
    n = y.shape[0]
    tb = _tile(n, (256, 128))

    def body(a_ref, b_ref, g_ref, y_ref, dx_ref, dy_ref, dg_ref):
        dx = a_ref[...] + b_ref[...]
        dx_ref[...] = dx
        dy_ref[...] = (g_ref[...] * dx).astype(BF)

        @pl.when(pl.program_id(0) == 0)
        def _():
            dg_ref[...] = jnp.zeros_like(dg_ref)
        dg_ref[...] += jnp.sum(dx * y_ref[...], axis=0, keepdims=True)

    row = pl.BlockSpec((tb, D), lambda i: (i, 0))
    one = pl.BlockSpec((1, D), lambda i: (0, 0))
    return pl.pallas_call(
        body, grid=(n // tb,), name="resid_bwd",
        in_specs=[row, row, one, row], out_specs=[row, row, one],
        out_shape=[jax.ShapeDtypeStruct((n, D), F32), jax.ShapeDtypeStruct((n, D), BF), jax.ShapeDtypeStruct((1, D), F32)],
    )(dx1a, dx1b, gate, y)


def _swiglu_fn(ug, uv):
    return jax.nn.silu(ug) * uv


FFN_GROUP = 256


def _ffn_mid_fwd(u, w8, bias):
    n = u.shape[0]
    tb = _tile(n, (256, 128))
    starts, ends = _segment_edges((n,), tb)

    def body(cur_ref, prev_ref, next_ref, w_ref, b_ref, o_ref):
        keep = _keep_halos(pl.program_id(0), starts, ends)
        for c0 in range(0, DFF, FFN_GROUP):
            halves = []
            for cols in (slice(c0, c0 + FFN_GROUP), slice(DFF + c0, DFF + c0 + FFN_GROUP)):
                xe = _ext_rows((cur_ref, prev_ref, next_ref), cols, keep)
                halves.append(_conv_rows(xe, w_ref, cols, FFN_TAPS)[HALO:HALO + tb] + b_ref[:, cols])
            o_ref[:, c0:c0 + FFN_GROUP] = _swiglu_fn(*halves).astype(BF)

    return pl.pallas_call(
        body, grid=(n // tb,), name="ffn_mid_fwd",
        in_specs=_halo_specs(tb, 2 * DFF, n) + [pl.BlockSpec((8, 2 * DFF), lambda i: (0, 0)), pl.BlockSpec((1, 2 * DFF), lambda i: (0, 0))],
        out_specs=pl.BlockSpec((tb, DFF), lambda i: (i, 0)), out_shape=jax.ShapeDtypeStruct((n, DFF), BF),
        compiler_params=_cp(),
    )(u, u, u, w8, bias)


def _ffn_mid_bwd(u, w8, bias, da):
    n = u.shape[0]
    tb = _tile(n, (256, 128))
    starts, ends = _segment_edges((n,), tb)

    def body(cur_ref, prev_ref, next_ref, w_ref, b_ref, da_c, da_p, da_n, du_ref, dw_ref, db_ref):
        i = pl.program_id(0)
        keep = _keep_halos(i, starts, ends)

        @pl.when(i == 0)
        def _():
            dw_ref[...] = jnp.zeros_like(dw_ref)
            db_ref[...] = jnp.zeros_like(db_ref)

        for c0 in range(0, DFF, FFN_GROUP):
            col_pair = (slice(c0, c0 + FFN_GROUP), slice(DFF + c0, DFF + c0 + FFN_GROUP))
            xes = [_ext_rows((cur_ref, prev_ref, next_ref), cols, keep) for cols in col_pair]
            convs = [_conv_rows(xe, w_ref, cols, FFN_TAPS) + b_ref[:, cols] for xe, cols in zip(xes, col_pair)]
            dae = _ext_rows((da_c, da_p, da_n), col_pair[0], keep)
            _, vjp = jax.vjp(_swiglu_fn, *convs)
            for xe, cols, dce in zip(xes, col_pair, vjp(dae)):
                du_ref[:, cols] = _conv_rows(dce, w_ref, cols, FFN_TAPS, transpose=True)[HALO:HALO + tb].astype(BF)
                dcur = dce[HALO:HALO + tb]
                for j, g in enumerate(_tap_grads(dcur, xe, FFN_TAPS, tb)):
                    dw_ref[j:j + 1, cols] += g
                db_ref[:, cols] += jnp.sum(dcur, axis=0, keepdims=True)

    wspec = pl.BlockSpec((8, 2 * DFF), lambda i: (0, 0))
    bspec = pl.BlockSpec((1, 2 * DFF), lambda i: (0, 0))
    return pl.pallas_call(
        body, grid=(n // tb,), name="ffn_mid_bwd",
        in_specs=_halo_specs(tb, 2 * DFF, n) + [wspec, bspec] + _halo_specs(tb, DFF, n),
        out_specs=[pl.BlockSpec((tb, 2 * DFF), lambda i: (i, 0)), wspec, bspec],
        out_shape=[jax.ShapeDtypeStruct((n, 2 * DFF), BF), jax.ShapeDtypeStruct((8, 2 * DFF), F32), jax.ShapeDtypeStruct((1, 2 * DFF), F32)],
        compiler_params=_cp(),
    )(u, u, u, w8, bias, da, da, da)


def _loss_kernel(x1, gate, ff, target):
    n = x1.shape[0]
    tb = _tile(n, (256, 128))

    def body(x_ref, g_ref, f_ref, t_ref, loss_ref, dy_ref, dff_ref, dg_ref):
        err = x_ref[...] + g_ref[...] * f_ref[...] - t_ref[...]
        dy = err * (1.0 / D)
        dy_ref[...] = dy
        dff_ref[...] = (g_ref[...] * dy).astype(BF)

        @pl.when(pl.program_id(0) == 0)
        def _():
            loss_ref[...] = jnp.zeros_like(loss_ref)
            dg_ref[...] = jnp.zeros_like(dg_ref)
        part = 0.5 * jnp.sum(jnp.sum(err * err, axis=1, keepdims=True) * (1.0 / D), axis=0, keepdims=True)
        loss_ref[...] += jnp.broadcast_to(part, (1, 128))
        dg_ref[...] += jnp.sum(dy * f_ref[...], axis=0, keepdims=True)

    row = pl.BlockSpec((tb, D), lambda i: (i, 0))
    one = pl.BlockSpec((1, D), lambda i: (0, 0))
    return pl.pallas_call(
        body, grid=(n // tb,), name="loss",
        in_specs=[row, one, row, row], out_specs=[pl.BlockSpec((1, 128), lambda i: (0, 0)), row, row, one],
        out_shape=[jax.ShapeDtypeStruct((1, 128), F32), jax.ShapeDtypeStruct((n, D), F32),
                   jax.ShapeDtypeStruct((n, D), BF), jax.ShapeDtypeStruct((1, D), F32)],
    )(x1, gate, ff, target)


def _rope_tables(nlat, lc):
    t = jnp.arange(nlat)
    row = (t // GRID_W).astype(F32)
    col = (t % GRID_W).astype(F32)
    inv_freq = ROPE_BASE ** (-jnp.arange(32, dtype=F32) / 32)
    ar, ac = row[:, None] * inv_freq, col[:, None] * inv_freq
    cos = jnp.concatenate([jnp.cos(ar), jnp.cos(ar), jnp.cos(ac), jnp.cos(ac)], axis=1)
    sin = jnp.concatenate([-jnp.sin(ar), jnp.sin(ar), -jnp.sin(ac), jnp.sin(ac)], axis=1)
    cos = jnp.concatenate([cos, jnp.ones((lc, HD), F32)], axis=0)
    sin = jnp.concatenate([sin, jnp.zeros((lc, HD), F32)], axis=0)
    return cos, sin


def _pad_rows8(w):
    return jnp.concatenate([w, jnp.zeros((8 - w.shape[0], w.shape[1]), w.dtype)], axis=0)


def _pack_w_in(w):
    cuts = [sum(IN_SIZES[:i]) for i in range(len(IN_SIZES) + 1)]
    qkv, gt, b, a, q, k, v, mg = [w[:, cuts[i]:cuts[i + 1]] for i in range(len(IN_SIZES))]
    return jnp.concatenate([qkv, gt, q, mg, k, v, b, a, jnp.zeros((w.shape[0], PW - O_BA - 32), w.dtype)], axis=1)


def _unpack_w_in(g):
    return jnp.concatenate([g[:, O_QKV:O_GT], g[:, O_GT:O_Q], g[:, O_BA:O_BA + 32], g[:, O_Q:O_MG], g[:, O_K:O_V],
                            g[:, O_V:O_BA], g[:, O_MG:O_K]], axis=1)


def _local_step(x, ctx, mod_x, mod_c, target, project_in, project_back,
                norm_mix, norm_ffn, dn_conv, a_log, dt_bias, dn_norm, q_norm, k_norm, sink, ffn_conv, ffn_conv_b):
    L, LC = x.shape[0], ctx.shape[0]
    T = L + LC
    xc = jnp.concatenate([x, ctx], axis=0)
    seg = lambda r: jnp.stack([mod_x[r], mod_c[r]])[:, None, :]
    sh_a, sc_a = seg(0), seg(1)
    g_a, g_f = mod_x[2][None], mod_x[5][None]
    sh_f, sc_f = mod_x[3][None, None], mod_x[4][None, None]
    sh_f2 = jnp.concatenate([sh_f, sh_f], axis=0)
    sc_f2 = jnp.concatenate([sc_f, sc_f], axis=0)
    cos, sin = _rope_tables(L, LC)
    dnc8 = _pad_rows8(dn_conv)
    ffc8 = _pad_rows8(ffn_conv)
    gate_row = lambda a: jnp.concatenate([jnp.zeros((1, 16), F32), a.reshape(1, 16), jnp.zeros((1, 96), F32)], axis=1)
    alog_row, dt_row = gate_row(a_log), gate_row(dt_bias)
    sinkb = jnp.concatenate([jnp.broadcast_to(sink.reshape(KVH, GRP, 1), (KVH, GRP, 128)), jnp.zeros((KVH, 8 - GRP, 128), F32)], axis=1)

    h1 = _norm_mod_fwd(xc, norm_mix, sh_a, sc_a, L, "norm_mix_fwd")
    p, (w_in_p, w_bdn, w_bat, w_out, w_up, w_down) = project_in(h1)
    q, k, v, gb = _dn_pre_fwd(p, dnc8, alog_row, dt_row, (L, LC))
    wy = _dn1_fwd(q, k, v, gb)
    scans = _dn2_fwd([t[:6] for t in wy], L)
    o_dir = [s[0] for s in scans]
    y_dn = _ghn_fwd(o_dir[0], o_dir[1], p, dn_norm, L)
    qr, kr, vb = _attn_prep_fwd(p, q_norm, k_norm, cos, sin)
    y_at = _attn_fwd(qr, kr, vb, sinkb, L)
    z_dn = _mm(y_dn, w_bdn, form="nn", out_dtype=F32, name="branch_dn")
    z_at = _mm(y_at, w_bat, form="nn", out_dtype=F32, name="branch_at")
    merged = _merge_fwd(z_dn, z_at, p, L)
    mix = _mm(merged, w_out, form="nn", out_dtype=F32, name="out_proj")
    x1 = _resid_fwd(xc, g_a, mix)
    h2 = _norm_mod_fwd(x1, norm_ffn, sh_f2, sc_f2, L, "norm_ffn_fwd")
    u_raw = _mm(h2, w_up, form="nn", out_dtype=F32, name="ffn_up")
    act = _ffn_mid_fwd(u_raw, ffc8, ffn_conv_b)
    ff = _mm(act, w_down, form="nn", out_dtype=F32, name="ffn_down")
    loss_row, dy, dff, dg_f = _loss_kernel(x1, g_f, ff, target)

    g_down = _mm(act, dff, form="tn", out_dtype=BF, name="g_ffn_down")
    dact = _mm(dff, w_down, form="nt", out_dtype=F32, name="d_act")
    du_raw, g_ffc8, g_ffb = _ffn_mid_bwd(u_raw, ffc8, ffn_conv_b, dact)
    g_up = _mm(h2, du_raw, form="tn", out_dtype=BF, name="g_ffn_up")
    dh2 = _mm(du_raw, w_up, form="nt", out_dtype=F32, name="d_h2")
    dx1n, g_nffn, dsh_f, dsc_f = _norm_mod_bwd(x1, norm_ffn, sh_f2, sc_f2, dh2, L, "norm_ffn_bwd")
    dx1, dmix, dg_a = _resid_bwd(dy, dx1n, g_a, mix)

    g_out = _mm(merged, dmix, form="tn", out_dtype=BF, name="g_w_out")
    dmerged = _mm(dmix, w_out, form="nt", out_dtype=F32, name="d_merged")
    dz_dn, dz_at, dmg = _merge_bwd(z_dn, z_at, p, dmerged, L)
    g_bdn = _mm(y_dn, dz_dn, form="tn", out_dtype=BF, name="g_branch_dn")
    g_bat = _mm(y_at, dz_at, form="tn", out_dtype=BF, name="g_branch_at")
    dy_dn = _mm(dz_dn, w_bdn, form="nt", out_dtype=F32, name="d_y_dn")
    dy_at = _mm(dz_at, w_bat, form="nt", out_dtype=F32, name="d_y_at")
    dqr, dkp, dvp, dkx, dvx, dsink = _attn_bwd(qr, kr, vb, sinkb, dy_at, L)
    dq_raw, dk_raw, dv_raw, g_qn, g_kn = _attn_prep_bwd(p, q_norm, k_norm, cos, sin, dqr, dkp, dvp, dkx, dvx, L)
    do, dgt, g_dnn = _ghn_bwd(o_dir[0], o_dir[1], p, dn_norm, dy_dn, L)
    cots = _dn2_bwd([wy[d][:6] + (scans[d][1],) for d in (0, 1)], do, L)
    dq, dk, dv, dgb = _dn1_bwd(q, k, v, gb, [t[6] for t in wy], cots)
    dqkv_raw, dba, g_dnc8, g_alog, g_dt = _dn_pre_bwd(p, dnc8, alog_row, dt_row, dq, dk, dv, dgb, (L, LC))
    padc = lambda a: jnp.concatenate([a, jnp.zeros((LC, a.shape[1]), a.dtype)], axis=0)
    dp = jnp.concatenate([dqkv_raw, padc(dgt), dq_raw, padc(dmg), dk_raw, dv_raw, dba, jnp.zeros((T, PW - O_BA - 128), BF)], axis=1)
    big, dh1 = project_back(h1, dp, w_in_p, (g_bdn, g_bat, g_out, g_up, g_down))
    dxc, g_nmix, dsh_a, dsc_a = _norm_mod_bwd(xc, norm_mix, sh_a, sc_a, dh1, L, "norm_mix_bwd")
    grad_x = _add2(dx1, dxc)

    zero = jnp.zeros((D,), F32)
    dmod_x = jnp.stack([dsh_a[0, 0], dsc_a[0, 0], dg_a[0], dsh_f[0, 0], dsc_f[0, 0], dg_f[0]])
    dmod_c = jnp.stack([dsh_a[1, 0], dsc_a[1, 0], zero, zero, zero, zero])
    small = dict(
        dmod_x=dmod_x, dmod_c=dmod_c, norm_mix=g_nmix, norm_ffn=g_nffn, dn_conv=g_dnc8[:5], dn_a_log=g_alog[0, 16:32].reshape(2, 8),
        dn_dt_bias=g_dt[0, 16:32].reshape(2, 8), dn_norm=g_dnn, q_norm=g_qn, k_norm=g_kn,
        attn_sink=jnp.sum(dsink[:, :GRP, :], axis=2).reshape(1, NH), ffn_conv=g_ffc8[:3], ffn_conv_b=g_ffb)
    return loss_row[0, 0], grad_x, big, small


def _elementwise(fn, args, out_dtypes, name, rows=None):
    n = rows or args[0].shape[0]
    ncol = args[0].shape[1]
    tb = _tile(n, (256, 128, 8))
    nout = len(out_dtypes)

    def body(*refs):
        outs = fn(*[r[...] for r in refs[:len(args)]])
        for o_ref, o in zip(refs[len(args):], outs):
            o_ref[...] = o.astype(o_ref.dtype)

    spec = pl.BlockSpec((tb, ncol), lambda i: (i, 0))
    return pl.pallas_call(
        body, grid=(n // tb,), name=name, in_specs=[spec] * len(args), out_specs=[spec] * nout,
        out_shape=[jax.ShapeDtypeStruct((n, ncol), dt) for dt in out_dtypes],
    )(*args)


def _add2(a, b):
    return _elementwise(lambda x, y: (x + y,), [a, b], [F32], "add2", rows=a.shape[0])[0]


def _exchange(arrays, scatter, name):
    n = len(arrays)

    def body(*refs):
        args = (refs[:n], refs[n:2 * n], *refs[2 * n:], scatter)
        _exchange_start(*args)
        _exchange_wait(*args)

    hbm = pl.BlockSpec(memory_space=pl.ANY)
    out_shape, sems = _exchange_shapes(arrays, scatter)
    return pl.pallas_call(body, name=name, in_specs=[hbm] * n, out_specs=[hbm] * n, out_shape=out_shape,
                          scratch_shapes=sems)(*arrays)


def _ada_fwd(c16, w_ada, b_ada):
    def body(c_ref, w_ref, b_ref, o_ref):
        o_ref[...] = _dot_hi(jax.nn.silu(c_ref[...]), w_ref[...]) + b_ref[...]

    return pl.pallas_call(body, name="ada_fwd", out_shape=jax.ShapeDtypeStruct((16, w_ada.shape[1]), F32))(c16, w_ada, b_ada)


def _ada_bwd(c16, w_ada, dmx, dmc):
    def body(c_ref, w_ref, dmx_ref, dmc_ref, gw_ref, pc_ref):
        dmc_tot = dmc_ref[0:1, :]
        for d in range(1, N_DEV):
            dmc_tot = dmc_tot + dmc_ref[d:d + 1, :]
        dm16 = jnp.concatenate([dmx_ref[...], jnp.broadcast_to(dmc_tot, (8, dmc_tot.shape[1]))], axis=0)
        row = lax.broadcasted_iota(jnp.int32, dm16.shape, 0)
        dm16 = jnp.where(row <= 8, dm16, 0.0)
        s = jax.nn.silu(c_ref[...])
        gw_ref[...] = lax.dot_general(s, dm16, (_DIMS["tn"], ((), ())), precision=HI, preferred_element_type=F32)
        pc = lax.dot_general(dm16, w_ref[...], (_DIMS["nt"], ((), ())), precision=HI, preferred_element_type=F32)
        pc_ref[...] = pc[8:9, :]

    return pl.pallas_call(body, name="ada_bwd", out_shape=[jax.ShapeDtypeStruct(w_ada.shape, F32), jax.ShapeDtypeStruct((1, D), F32)],
                          compiler_params=_cp())(c16, w_ada, dmx, dmc)


def _cctx_grad(pc_all, c_ctx_row):
    def body(pc_ref, c_ref, g_ref):
        tot = pc_ref[0]
        for d in range(1, N_DEV):
            tot = tot + pc_ref[d]
        _, vjp = jax.vjp(jax.nn.silu, c_ref[...])
        g_ref[...] = vjp(tot)[0]

    return pl.pallas_call(body, name="cctx_grad", out_shape=jax.ShapeDtypeStruct((1, D), F32))(pc_all, c_ctx_row)


def _adamw(parts, w, m, v, name):
    ns, R, C = parts.shape
    tb = _tile(R, (128, 64, 32, 16, 8))

    def body(p_ref, w_ref, m_ref, v_ref, g_ref, d_ref, mo_ref, vo_ref):
        g = p_ref[0].astype(F32)
        for s in range(1, ns):
            g = g + p_ref[s].astype(F32)
        m2 = ADAM_B1 * m_ref[...] + (1.0 - ADAM_B1) * g
        v2 = ADAM_B2 * v_ref[...] + (1.0 - ADAM_B2) * jnp.square(g)
        m_hat = m2 / (1.0 - ADAM_B1 ** ADAM_STEP)
        v_hat = v2 / (1.0 - ADAM_B2 ** ADAM_STEP)
        g_ref[...] = g
        d_ref[...] = -ADAM_LR * (m_hat / (jnp.sqrt(v_hat) + ADAM_EPS) + ADAM_WD * w_ref[...])
        mo_ref[...] = m2
        vo_ref[...] = v2

    row = pl.BlockSpec((tb, C), lambda i: (i, 0))
    return pl.pallas_call(
        body, grid=(R // tb,), name=name,
        in_specs=[pl.BlockSpec((ns, tb, C), lambda i: (0, i, 0)), row, row, row], out_specs=[row] * 4,
        out_shape=[jax.ShapeDtypeStruct((R, C), F32)] * 4, compiler_params=_cp(),
    )(parts, w, m, v)


_SMALL = (("dmod_x", 6 * D), ("dmod_c", 6 * D), ("b_ada", 6 * D), ("norm_mix", D), ("norm_ffn", D), ("dn_a_log", 16),
          ("dn_dt_bias", 16), ("dn_norm", HD), ("q_norm", HD), ("k_norm", HD), ("attn_sink", NH), ("ffn_conv_b", 2 * DFF),
          ("dn_conv", 5 * 3 * D), ("ffn_conv", 3 * 2 * DFF))
_SMALL_ROWS = -(-sum(n for _, n in _SMALL) // 1024) * 8


def _pack_small(d):
    flat = jnp.concatenate([d[k].reshape(-1).astype(F32) if k in d else jnp.zeros((n,), F32) for k, n in _SMALL])
    return jnp.concatenate([flat, jnp.zeros((_SMALL_ROWS * 128 - flat.shape[0],), F32)]).reshape(_SMALL_ROWS, 128)


def _unpack_small(a):
    flat = a.reshape(a.shape[:-2] + (-1,))
    out, off = {}, 0
    for k, n in _SMALL:
        out[k] = flat[..., off:off + n]
        off += n
    return out


def kernel(x, c, ctx, c_ctx, w_ada, b_ada, norm_mix, norm_ffn, w_in, dn_conv, dn_a_log, dn_dt_bias, dn_norm, q_norm, k_norm, attn_sink, w_branch_dn, w_branch_attn, w_out, ffn_up, ffn_conv, ffn_conv_b, ffn_down, loss_target, m_c_ctx, m_w_ada, m_b_ada, m_norm_mix, m_norm_ffn, m_w_in, m_dn_conv, m_dn_a_log, m_dn_dt_bias, m_dn_norm, m_q_norm, m_k_norm, m_attn_sink, m_w_branch_dn, m_w_branch_attn, m_w_out, m_ffn_up, m_ffn_conv, m_ffn_conv_b, m_ffn_down, v_c_ctx, v_w_ada, v_b_ada, v_norm_mix, v_norm_ffn, v_w_in, v_dn_conv, v_dn_a_log, v_dn_dt_bias, v_dn_norm, v_q_norm, v_k_norm, v_attn_sink, v_w_branch_dn, v_w_branch_attn, v_w_out, v_ffn_up, v_ffn_conv, v_ffn_conv_b, v_ffn_down):
    me = 4 * lax.axis_index("x") + 2 * lax.axis_index("y") + lax.axis_index("c")
    ada_cols = w_ada.shape[2]

    cols = lambda a: jnp.swapaxes(a, 0, 1).reshape(a.shape[1], -1)
    rows = lambda a: a.reshape(-1, a.shape[2])
    col_blocks = lambda g: jnp.swapaxes(g.reshape(g.shape[0], N_DEV, -1), 0, 1)
    row_blocks = lambda g: g.reshape(N_DEV, -1, g.shape[1])

    gathered = _exchange([w_in[0].astype(BF), c, dn_conv[0], ffn_conv[0]], scatter=False, name="gather_first")
    w_in_packed = _pack_w_in(cols(gathered[0]))
    c_all = gathered[1][:, 0, :]

    def project_in(h1):
        p, rest = _mm(h1, w_in_packed, form="nn", out_dtype=F32, name="in_proj",
                      exchange=([w_branch_dn[0].astype(BF), w_branch_attn[0].astype(BF), w_out[0].astype(BF),
                                 ffn_up[0].astype(BF), ffn_down[0].astype(BF)], False))
        return p, (w_in_packed, rows(rest[0]), rows(rest[1]), rows(rest[2]), cols(rest[3]), rows(rest[4]))

    def project_back(h1, dp, w_in_p, grads):
        g_bdn, g_bat, g_out, g_up, g_down = grads
        g_in, landed_rest = _mm(h1, dp, form="tn", out_dtype=BF, name="g_w_in",
                                exchange=([row_blocks(g_bdn), row_blocks(g_bat), row_blocks(g_out), col_blocks(g_up),
                                           row_blocks(g_down)], True))
        dh1, landed_in = _mm(dp, w_in_p, form="nt", out_dtype=F32, name="d_h1",
                             exchange=([col_blocks(_unpack_w_in(g_in))], True))
        return [landed_in[0]] + landed_rest, dh1

    c16 = jnp.concatenate([c_all, c_ctx[None], jnp.zeros((7, D), F32)], axis=0)
    b_loc = lax.dynamic_slice_in_dim(b_ada, me * ada_cols, ada_cols, axis=1)
    mod_part = _ada_fwd(c16, w_ada[0], b_loc)
    mod_all = cols(_exchange([mod_part], scatter=False, name="gather_mod")[0])
    mod_x = lax.dynamic_slice_in_dim(mod_all, me, 1, axis=0).reshape(6, D)
    mod_c = mod_all[8].reshape(6, D)

    loss_loc, grad_x, landed, small = _local_step(
        x[0], ctx[0], mod_x, mod_c, loss_target[0], project_in, project_back,
        norm_mix, norm_ffn, cols(gathered[2]), dn_a_log[0], dn_dt_bias[0], dn_norm, q_norm, k_norm, attn_sink[0], cols(gathered[3]),
        ffn_conv_b)
    loss = lax.psum(loss_loc, ("x", "y", "c"))

    res = {}
    res["w_in"] = _adamw(landed[0], w_in[0], m_w_in[0], v_w_in[0], "adamw_w_in")
    res["w_branch_dn"] = _adamw(landed[1], w_branch_dn[0], m_w_branch_dn[0], v_w_branch_dn[0], "adamw_w_branch_dn")
    res["w_branch_attn"] = _adamw(landed[2], w_branch_attn[0], m_w_branch_attn[0], v_w_branch_attn[0], "adamw_w_branch_attn")
    res["w_out"] = _adamw(landed[3], w_out[0], m_w_out[0], v_w_out[0], "adamw_w_out")
    res["ffn_up"] = _adamw(landed[4], ffn_up[0], m_ffn_up[0], v_ffn_up[0], "adamw_ffn_up")
    res["ffn_down"] = _adamw(landed[5], ffn_down[0], m_ffn_down[0], v_ffn_down[0], "adamw_ffn_down")

    small = dict(small)
    small["b_ada"] = small["dmod_x"] + small["dmod_c"]
    parts = _exchange([_pack_small(small)], scatter=False, name="gather_small")[0]
    per_dev = _unpack_small(parts)
    given = dict(b_ada=(b_ada, m_b_ada, v_b_ada), norm_mix=(norm_mix, m_norm_mix, v_norm_mix), norm_ffn=(norm_ffn, m_norm_ffn, v_norm_ffn),
                 dn_a_log=(dn_a_log, m_dn_a_log, v_dn_a_log), dn_dt_bias=(dn_dt_bias, m_dn_dt_bias, v_dn_dt_bias),
                 dn_norm=(dn_norm, m_dn_norm, v_dn_norm), q_norm=(q_norm, m_q_norm, v_q_norm), k_norm=(k_norm, m_k_norm, v_k_norm),
                 attn_sink=(attn_sink, m_attn_sink, v_attn_sink), ffn_conv_b=(ffn_conv_b, m_ffn_conv_b, v_ffn_conv_b))
    packs = [_pack_small({k: t[j] for k, t in given.items()}) for j in range(3)]
    upd = [_unpack_small(a) for a in _adamw(parts, packs[0], packs[1], packs[2], "adamw_small")]
    for k, t in given.items():
        res[k] = tuple(u[k].reshape(t[0].shape) for u in upd)
    dnc = lax.dynamic_slice_in_dim(upd[0]["dn_conv"].reshape(5, 3 * D), me * dn_conv.shape[2], dn_conv.shape[2], axis=1)
    ffc = lax.dynamic_slice_in_dim(upd[0]["ffn_conv"].reshape(3, 2 * DFF), me * ffn_conv.shape[2], ffn_conv.shape[2], axis=1)
    r8 = lambda a: _pad_rows8(a)
    t = _adamw(r8(dnc)[None], r8(dn_conv[0]), r8(m_dn_conv[0]), r8(v_dn_conv[0]), "adamw_dn_conv")
    res["dn_conv"] = tuple(a[:5][None] for a in t)
    t = _adamw(r8(ffc)[None], r8(ffn_conv[0]), r8(m_ffn_conv[0]), r8(v_ffn_conv[0]), "adamw_ffn_conv")
    res["ffn_conv"] = tuple(a[:3][None] for a in t)

    dmx = lax.dynamic_slice_in_dim(per_dev["dmod_x"], me * ada_cols, ada_cols, axis=1)
    dmc = lax.dynamic_slice_in_dim(per_dev["dmod_c"], me * ada_cols, ada_cols, axis=1)
    g_ada, pc = _ada_bwd(c16, w_ada[0], dmx, dmc)
    res["w_ada"] = _adamw(g_ada[None], w_ada[0], m_w_ada[0], v_w_ada[0], "adamw_w_ada")
    pc_all = _exchange([pc], scatter=False, name="gather_cctx")[0]
    g_cctx = _cctx_grad(pc_all, c_ctx[None])
    r8b = lambda a: jnp.broadcast_to(a, (8, D))
    t = _adamw(r8b(g_cctx)[None], r8b(c_ctx[None]), r8b(m_c_ctx[None]), r8b(v_c_ctx[None]), "adamw_c_ctx")
    res["c_ctx"] = tuple(a[0] for a in t)

    names = ("c_ctx", "w_ada", "b_ada", "norm_mix", "norm_ffn", "w_in", "dn_conv", "dn_a_log", "dn_dt_bias", "dn_norm", "q_norm",
             "k_norm", "attn_sink", "w_branch_dn", "w_branch_attn", "w_out", "ffn_up", "ffn_conv", "ffn_conv_b", "ffn_down")
    lead = ("w_ada", "w_in", "w_branch_dn", "w_branch_attn", "w_out", "ffn_up", "ffn_down")
    fix = lambda k, a: a[None] if k in lead else a
    outs = [loss, grad_x[None]]
    for j in range(4):
        outs += [fix(k, res[k][j]) for k in names]
    return tuple(outs)
```

```python
import functools

import jax
import jax.numpy as jnp
from jax import lax
from jax.experimental import pallas as pl
from jax.experimental.pallas import tpu as pltpu

F32 = jnp.float32
BF = jnp.bfloat16
HI = lax.Precision.HIGHEST
MESH = pl.DeviceIdType.MESH

D = 1024
NH = 8
HD = 128
KVH = 2
GRP = 4
KV = KVH * HD
DFF = 2816
CB = 128
GRID_W = 64
ROPE_BASE = 10000.0
EPS = 1e-6
N_DEV = 8
PW = 8192
O_QKV, O_GT, O_Q, O_MG, O_K, O_V, O_BA = 0, 3072, 4096, 5120, 7168, 7424, 7680
IN_SIZES = (3072, 1024, 16, 16, 1024, 256, 256, 2048)
IN_DIM = sum(IN_SIZES)
ADAM_LR, ADAM_B1, ADAM_B2, ADAM_EPS, ADAM_WD, ADAM_STEP = 0.001, 0.9, 0.999, 1e-08, 0.01, 10
VMEM_LIMIT = 56 * 1024 * 1024


def _cp():
    return pltpu.CompilerParams(vmem_limit_bytes=VMEM_LIMIT)


def _tile(n, cands):
    for c in cands:
        if n % c == 0:
            return c
    return n


def _iota2(shape):
    return lax.broadcasted_iota(jnp.int32, shape, 0), lax.broadcasted_iota(jnp.int32, shape, 1)


_DIMS = {"nn": ((1,), (0,)), "nt": ((1,), (1,)), "tn": ((0,), (0,))}


def _exchange_copies(ins, outs, send_sems, recv_sems, local_sems, scatter, landings):
    x, y, c = lax.axis_index("x"), lax.axis_index("y"), lax.axis_index("c")
    me = 4 * x + 2 * y + c
    local, remote = [], []
    for k in range(len(ins)):
        local.append(pltpu.make_async_copy(ins[k].at[me] if scatter else ins[k], outs[k].at[me], local_sems.at[k]))
        for m in range(1, N_DEV):
            px = 1 - x if m & 4 else x
            py = 1 - y if m & 2 else y
            pc = 1 - c if m & 1 else c
            peer = 4 * px + 2 * py + pc
            src = ins[k].at[peer] if scatter else ins[k]
            sem = k * (N_DEV - 1) + m - 1
            push = pltpu.make_async_remote_copy(src_ref=src, dst_ref=outs[k].at[me], send_sem=send_sems.at[sem],
                                                recv_sem=recv_sems.at[sem], device_id=(px, py, pc), device_id_type=MESH)
            landing = None
            if landings:
                landing = pltpu.make_async_remote_copy(src_ref=src, dst_ref=outs[k].at[peer], send_sem=send_sems.at[sem],
                                                       recv_sem=recv_sems.at[sem], device_id=(px, py, pc), device_id_type=MESH)
            remote.append((push, landing))
    return local, remote


def _exchange_start(*args):
    local, remote = _exchange_copies(*args, landings=False)
    for cp in local:
        cp.start()
    for push, _ in remote:
        push.start()


def _exchange_wait(*args):
    local, remote = _exchange_copies(*args, landings=True)
    for _, landing in remote:
        landing.wait_recv()
    for push, _ in remote:
        push.wait_send()
    for cp in local:
        cp.wait()


def _exchange_shapes(arrays, scatter):
    out_shape = [jax.ShapeDtypeStruct(a.shape if scatter else (N_DEV,) + a.shape, a.dtype) for a in arrays]
    n = len(arrays)
    sems = [pltpu.SemaphoreType.DMA((n * (N_DEV - 1),)), pltpu.SemaphoreType.DMA((n * (N_DEV - 1),)), pltpu.SemaphoreType.DMA((n,))]
    return out_shape, sems


def _mm(a, b, *, form, out_dtype, name, tm=None, tn=None, tk=None, exchange=None):
    if form == "tn":
        K, M = a.shape
        N = b.shape[1]
    else:
        M, K = a.shape
        N = b.shape[0] if form == "nt" else b.shape[1]
    tm = tm or _tile(M, (1024, 640, 512, 256, 128))
    tn = tn or _tile(N, (1408, 1024, 512, 256, 128))
    tk = tk or _tile(K, (2048, 1408, 1280, 1024, 640, 512, 256, 128))
    ni, nj, nk = M // tm, N // tn, K // tk
    dims = (_DIMS[form], ((), ()))
    ex_arrays, scatter = exchange if exchange else ([], False)
    nx = len(ex_arrays)

    def body(a_ref, b_ref, *refs):
        ex_in, o_ref, ex_out, scratch = refs[:nx], refs[nx], refs[nx + 1:2 * nx + 1], refs[2 * nx + 1:]
        i, j, k = pl.program_id(0), pl.program_id(1), pl.program_id(2)
        if nx:
            sems = scratch[-3:]

            @pl.when((i == 0) & (j == 0) & (k == 0))
            def _():
                _exchange_start(ex_in, ex_out, *sems, scatter)

        part = lax.dot_general(a_ref[...].astype(BF), b_ref[...].astype(BF), dims, preferred_element_type=F32)
        if nk == 1:
            o_ref[...] = part.astype(out_dtype)
        else:
            acc_ref = scratch[0]

            @pl.when(k == 0)
            def _():
                acc_ref[...] = part

            @pl.when(k > 0)
            def _():
                acc_ref[...] += part

            @pl.when(k == nk - 1)
            def _():
                o_ref[...] = acc_ref[...].astype(out_dtype)

        if nx:
            @pl.when((i == ni - 1) & (j == nj - 1) & (k == nk - 1))
            def _():
                _exchange_wait(ex_in, ex_out, *sems, scatter)

    if form == "tn":
        a_spec = pl.BlockSpec((tk, tm), lambda i, j, k: (k, i))
    else:
        a_spec = pl.BlockSpec((tm, tk), lambda i, j, k: (i, k))
    if form == "nt":
        b_spec = pl.BlockSpec((tn, tk), lambda i, j, k: (j, k))
    else:
        b_spec = pl.BlockSpec((tk, tn), lambda i, j, k: (k, j))
    hbm = pl.BlockSpec(memory_space=pl.ANY)
    ex_shapes, ex_sems = _exchange_shapes(ex_arrays, scatter) if nx else ([], [])
    outs = pl.pallas_call(
        body, grid=(ni, nj, nk), name=name,
        in_specs=[a_spec, b_spec] + [hbm] * nx, out_specs=[pl.BlockSpec((tm, tn), lambda i, j, k: (i, j))] + [hbm] * nx,
        out_shape=[jax.ShapeDtypeStruct((M, N), out_dtype)] + ex_shapes,
        scratch_shapes=([] if nk == 1 else [pltpu.VMEM((tm, tn), F32)]) + ex_sems,
        compiler_params=_cp(),
    )(a, b, *ex_arrays)
    return (outs[0], list(outs[1:])) if nx else outs[0]


def _norm_mod_fn(x, nw, sh, sc):
    y = x * lax.rsqrt(jnp.mean(x * x, axis=-1, keepdims=True) + EPS)
    return (y * nw) * (1.0 + sc) + sh


def _norm_mod_fwd(x, nw, sh, sc, nlat, name):
    T = x.shape[0]
    tb = _tile(T, (256, 128))
    nlb = nlat // tb

    def body(x_ref, nw_ref, sh_ref, sc_ref, h_ref):
        h_ref[...] = _norm_mod_fn(x_ref[...], nw_ref[...], sh_ref[0], sc_ref[0]).astype(BF)

    seg = pl.BlockSpec((1, 1, D), lambda i: (jnp.where(i >= nlb, 1, 0), 0, 0))
    return pl.pallas_call(
        body, grid=(T // tb,), name=name,
        in_specs=[pl.BlockSpec((tb, D), lambda i: (i, 0)), pl.BlockSpec((1, D), lambda i: (0, 0)), seg, seg],
        out_specs=pl.BlockSpec((tb, D), lambda i: (i, 0)),
        out_shape=jax.ShapeDtypeStruct((T, D), BF),
    )(x, nw, sh, sc)


def _norm_mod_bwd(x, nw, sh, sc, dh, *, row0, nrows, name, residual=None):
    tb = _tile(nrows, (256, 128))
    b0 = row0 // tb

    def body(x_ref, nw_ref, sh_ref, sc_ref, dh_ref, *refs):
        dnw_ref, dsh_ref, dsc_ref = refs[-3:]
        _, vjp = jax.vjp(_norm_mod_fn, x_ref[...], nw_ref[...], sh_ref[...], sc_ref[...])
        dx, dnw, dsh, dsc = vjp(dh_ref[...])
        if residual is not None:
            refs[1][...] = dx + refs[0][...]

        @pl.when(pl.program_id(0) == 0)
        def _():
            dnw_ref[...] = jnp.zeros_like(dnw_ref)
            dsh_ref[...] = jnp.zeros_like(dsh_ref)
            dsc_ref[...] = jnp.zeros_like(dsc_ref)

        dnw_ref[...] += dnw
        dsh_ref[...] += dsh
        dsc_ref[...] += dsc

    row = pl.BlockSpec((tb, D), lambda i: (b0 + i, 0))
    out_row = pl.BlockSpec((tb, D), lambda i: (i, 0))
    one = pl.BlockSpec((1, D), lambda i: (0, 0))
    with_dx = residual is not None
    return pl.pallas_call(
        body, grid=(nrows // tb,), name=name,
        in_specs=[row, one, one, one, row] + [out_row] * with_dx, out_specs=[out_row] * with_dx + [one] * 3,
        out_shape=[jax.ShapeDtypeStruct((nrows, D), F32)] * with_dx + [jax.ShapeDtypeStruct((1, D), F32)] * 3,
    )(x, nw, sh, sc, dh, *([residual] if with_dx else []))


def _resid_norm_fwd(x, gate, y, nw, sh, sc):
    n = y.shape[0]
    tb = _tile(n, (256, 128))

    def body(x_ref, g_ref, y_ref, nw_ref, sh_ref, sc_ref, x1_ref, h_ref):
        x1 = x_ref[...] + g_ref[...] * y_ref[...]
        x1_ref[...] = x1
        h_ref[...] = _norm_mod_fn(x1, nw_ref[...], sh_ref[...], sc_ref[...]).astype(BF)

    row = pl.BlockSpec((tb, D), lambda i: (i, 0))
    one = pl.BlockSpec((1, D), lambda i: (0, 0))
    return pl.pallas_call(
        body, grid=(n // tb,), name="resid_norm_fwd",
        in_specs=[row, one, row, one, one, one], out_specs=[row, row],
        out_shape=[jax.ShapeDtypeStruct((n, D), F32), jax.ShapeDtypeStruct((n, D), BF)],
    )(x, gate, y, nw, sh, sc)


def _resid_norm_bwd(x1, gate, y, nw, sh, sc, dh, dx1_direct):
    n = y.shape[0]
    tb = _tile(n, (256, 128))

    def body(x1_ref, g_ref, y_ref, nw_ref, sh_ref, sc_ref, dh_ref, dd_ref, dx_ref, dy_ref, dg_ref, dnw_ref, dsh_ref, dsc_ref):
        _, vjp = jax.vjp(_norm_mod_fn, x1_ref[...], nw_ref[...], sh_ref[...], sc_ref[...])
        dxn, dnw, dsh, dsc = vjp(dh_ref[...])
        dx = dxn + dd_ref[...]
        dx_ref[...] = dx
        dy_ref[...] = (g_ref[...] * dx).astype(BF)

        @pl.when(pl.program_id(0) == 0)
        def _():
            for r in (dg_ref, dnw_ref, dsh_ref, dsc_ref):
                r[...] = jnp.zeros_like(r)

        dg_ref[...] += jnp.sum(dx * y_ref[...], axis=0, keepdims=True)
        dnw_ref[...] += dnw
        dsh_ref[...] += dsh
        dsc_ref[...] += dsc

    row = pl.BlockSpec((tb, D), lambda i: (i, 0))
    one = pl.BlockSpec((1, D), lambda i: (0, 0))
    return pl.pallas_call(
        body, grid=(n // tb,), name="resid_norm_bwd",
        in_specs=[row, one, row, one, one, one, row, row], out_specs=[row, row] + [one] * 4,
        out_shape=[jax.ShapeDtypeStruct((n, D), F32), jax.ShapeDtypeStruct((n, D), BF)] + [jax.ShapeDtypeStruct((1, D), F32)] * 4,
    )(x1, gate, y, nw, sh, sc, dh, dx1_direct)


HALO = 8


def _halo_specs(tb, width, nrows, col=0):
    r8 = tb // HALO
    cur = pl.BlockSpec((tb, width), lambda i: (i, col))
    prev = pl.BlockSpec((HALO, width), lambda i: (jnp.maximum(i * r8 - 1, 0), col))
    nxt = pl.BlockSpec((HALO, width), lambda i: (jnp.minimum((i + 1) * r8, nrows // HALO - 1), col))
    return [cur, prev, nxt]


def _segment_edges(seg_rows, tb):
    bounds = [0]
    for s in seg_rows:
        bounds.append(bounds[-1] + s // tb)
    return bounds[:-1], [b - 1 for b in bounds[1:]]


def _keep_halos(i, starts, ends):
    keep_p = functools.reduce(lambda a, b: a & b, [i != s for s in starts])
    keep_n = functools.reduce(lambda a, b: a & b, [i != e for e in ends])
    return keep_p, keep_n


def _ext_rows(refs, cols, keep):
    cur_ref, prev_ref, next_ref = refs
    p = jnp.where(keep[0], prev_ref[:, cols].astype(F32), 0.0)
    n = jnp.where(keep[1], next_ref[:, cols].astype(F32), 0.0)
    return jnp.concatenate([p, cur_ref[:, cols].astype(F32), n], axis=0)


def _conv_rows(xe, w_ref, cols, width, transpose=False):
    r = width // 2
    n = xe.shape[0]
    acc = None
    for j in range(width):
        s = ((j - r) if transpose else (r - j)) % n
        term = (xe if s == 0 else pltpu.roll(xe, s, 0)) * w_ref[j:j + 1, cols]
        acc = term if acc is None else acc + term
    return acc


def _tap_grads(dcur, xe, width, tb):
    r = width // 2
    n = xe.shape[0]
    out = []
    for j in range(width):
        s = (r - j) % n
        xs = (xe if s == 0 else pltpu.roll(xe, s, 0))[HALO:HALO + tb]
        out.append(jnp.sum(dcur * xs, axis=0, keepdims=True))
    return out


def _softplus(x):
    return jnp.maximum(x, 0.0) + jnp.log(1.0 + jnp.exp(-jnp.abs(x)))


def _gates_fn(ba, alog_row, dt_row):
    col = lax.broadcasted_iota(jnp.int32, ba.shape, 1)
    beta = jax.nn.sigmoid(ba)
    g = -jnp.exp(alog_row) * _softplus(ba + dt_row)
    return jnp.where(col < 16, beta, jnp.where(col < 32, g, 0.0))


def _qkv_post_fn(c, kind):
    y = jax.nn.silu(c)
    if kind == 2:
        return y
    n = y * lax.rsqrt(jnp.sum(y * y, axis=-1, keepdims=True) + EPS)
    return n * (HD ** -0.5) if kind == 0 else n


DN_TAPS = 5
FFN_TAPS = 3


def _dn_pre_fwd(p, w8, alog_row, dt_row, seg_rows):
    T = p.shape[0]
    tb = _tile(T, (256, 128))
    starts, ends = _segment_edges(seg_rows, tb)

    def body(cur_ref, prev_ref, next_ref, ba_ref, w_ref, al_ref, dt_ref, q_ref, k_ref, v_ref, gb_ref):
        keep = _keep_halos(pl.program_id(0), starts, ends)
        outs = (q_ref, k_ref, v_ref)
        for kind in range(3):
            for h in range(NH):
                cols = slice(kind * D + h * HD, kind * D + (h + 1) * HD)
                xe = _ext_rows((cur_ref, prev_ref, next_ref), cols, keep)
                conv = _conv_rows(xe, w_ref, cols, DN_TAPS)[HALO:HALO + tb]
                outs[kind][:, h * HD:(h + 1) * HD] = _qkv_post_fn(conv, kind)
        gb_ref[...] = _gates_fn(ba_ref[...], al_ref[...], dt_ref[...])

    row = pl.BlockSpec((tb, D), lambda i: (i, 0))
    one = pl.BlockSpec((1, 128), lambda i: (0, 0))
    return pl.pallas_call(
        body, grid=(T // tb,), name="dn_pre_fwd",
        in_specs=_halo_specs(tb, 3 * D, T) + [pl.BlockSpec((tb, 128), lambda i: (i, O_BA // 128)),
                                              pl.BlockSpec((8, 3 * D), lambda i: (0, 0)), one, one],
        out_specs=[row, row, row, pl.BlockSpec((tb, 128), lambda i: (i, 0))],
        out_shape=[jax.ShapeDtypeStruct((T, D), F32)] * 3 + [jax.ShapeDtypeStruct((T, 128), F32)],
        compiler_params=_cp(),
    )(p, p, p, p, w8, alog_row, dt_row)


def _dn_pre_bwd(p, w8, alog_row, dt_row, dq, dk, dv, dgb, seg_rows):
    T = p.shape[0]
    tb = _tile(T, (256, 128))
    starts, ends = _segment_edges(seg_rows, tb)

    def body(cur_ref, prev_ref, next_ref, ba_ref, w_ref, al_ref, dt_ref,
             dq_c, dq_p, dq_n, dk_c, dk_p, dk_n, dv_c, dv_p, dv_n, dgb_ref, dx_ref, dba_ref, dw_ref, dal_ref, ddt_ref):
        i = pl.program_id(0)
        keep = _keep_halos(i, starts, ends)

        @pl.when(i == 0)
        def _():
            dw_ref[...] = jnp.zeros_like(dw_ref)
            dal_ref[...] = jnp.zeros_like(dal_ref)
            ddt_ref[...] = jnp.zeros_like(ddt_ref)

        douts = ((dq_c, dq_p, dq_n), (dk_c, dk_p, dk_n), (dv_c, dv_p, dv_n))
        for kind in range(3):
            for h in range(NH):
                cols = slice(kind * D + h * HD, kind * D + (h + 1) * HD)
                xe = _ext_rows((cur_ref, prev_ref, next_ref), cols, keep)
                conv = _conv_rows(xe, w_ref, cols, DN_TAPS)
                dye = _ext_rows(douts[kind], slice(h * HD, (h + 1) * HD), keep)
                _, vjp = jax.vjp(functools.partial(_qkv_post_fn, kind=kind), conv)
                dce = vjp(dye)[0]
                dx_ref[:, cols] = _conv_rows(dce, w_ref, cols, DN_TAPS, transpose=True)[HALO:HALO + tb].astype(BF)
                for j, g in enumerate(_tap_grads(dce[HALO:HALO + tb], xe, DN_TAPS, tb)):
                    dw_ref[j:j + 1, cols] += g
        _, vjp = jax.vjp(_gates_fn, ba_ref[...], al_ref[...], dt_ref[...])
        dba, dal, ddt = vjp(dgb_ref[...])
        dba_ref[...] = dba.astype(BF)
        dal_ref[...] += dal
        ddt_ref[...] += ddt

    one = pl.BlockSpec((1, 128), lambda i: (0, 0))
    nar = pl.BlockSpec((tb, 128), lambda i: (i, 0))
    wspec = pl.BlockSpec((8, 3 * D), lambda i: (0, 0))
    return pl.pallas_call(
        body, grid=(T // tb,), name="dn_pre_bwd",
        in_specs=_halo_specs(tb, 3 * D, T) + [pl.BlockSpec((tb, 128), lambda i: (i, O_BA // 128)), wspec, one, one]
        + _halo_specs(tb, D, T) * 3 + [nar],
        out_specs=[pl.BlockSpec((tb, 3 * D), lambda i: (i, 0)), nar, wspec, one, one],
        out_shape=[jax.ShapeDtypeStruct((T, 3 * D), BF), jax.ShapeDtypeStruct((T, 128), BF), jax.ShapeDtypeStruct((8, 3 * D), F32),
                   jax.ShapeDtypeStruct((1, 128), F32), jax.ShapeDtypeStruct((1, 128), F32)],
        compiler_params=_cp(),
    )(p, p, p, p, w8, alog_row, dt_row, dq, dq, dq, dk, dk, dk, dv, dv, dv, dgb)


def _dot_hi(a, b):
    return jnp.dot(a, b, precision=HI, preferred_element_type=F32)


def _dot_bf(a, b):
    return jnp.dot(a.astype(BF), b.astype(BF), preferred_element_type=F32)


def _dot_nt_bf(a, b):
    return lax.dot_general(a.astype(BF), b.astype(BF), (_DIMS["nt"], ((), ())), preferred_element_type=F32)


def _dot_tn_bf(a, b):
    return lax.dot_general(a.astype(BF), b.astype(BF), (_DIMS["tn"], ((), ())), preferred_element_type=F32)


def _dot_h3(a, b):
    return jnp.dot(a, b, precision=lax.Precision.HIGH, preferred_element_type=F32)


def _unit_tri_inverses(mats):
    r, c = _iota2((CB, CB))
    eye = (r == c).astype(F32)
    a8 = [jnp.where((r // 8) == (c // 8), a, 0.0) for a in mats]
    a2 = [_dot_h3(x, x) for x in a8]
    a4 = [_dot_h3(x, x) for x in a2]
    t = [_dot_h3(eye - x, eye + y) for x, y in zip(a8, a2)]
    t = [_dot_h3(x, eye + y) for x, y in zip(t, a4)]
    b = 8
    while b < CB:
        mask = ((r // (2 * b)) == (c // (2 * b))) & ((r // b) != (c // b))
        te = [_dot_h3(x, jnp.where(mask, a, 0.0)) for x, a in zip(t, mats)]
        t = [x - _dot_h3(y, x) for x, y in zip(t, te)]
        b *= 2
    return t


@jax.custom_vjp
def _saved_inverse(a, t):
    return t


_saved_inverse.defvjp(lambda a, t: (t, t), lambda t, dt: (-_dot_h3(_dot_h3(t.T, dt), t.T), jnp.zeros_like(t)))


def _dn1_decay(gc, reverse):
    r, c = _iota2((CB, CB))
    incl = (c >= r) if reverse else (c <= r)
    return jnp.where(incl, jnp.exp(jnp.where(incl, gc - gc.T, 0.0)), 0.0)


def _dn1_heads(qs, ks, vs, betas, gcs, ts_saved, reverse, kks=None, qks=None):
    r, c = _iota2((CB, CB))
    strict = (c > r) if reverse else (c < r)
    decays = [_dn1_decay(gc, reverse) for gc in gcs]
    kks = kks or [_dot_nt_bf(k, k) for k in ks]
    systems = [jnp.where(strict, b * kk * dc, 0.0) for b, kk, dc in zip(betas, kks, decays)]
    if ts_saved is None:
        ts = _unit_tri_inverses(systems)
    else:
        ts = [_saved_inverse(a, t) for a, t in zip(systems, ts_saved)]
    egs = [jnp.exp(gc) for gc in gcs]
    us = [_dot_h3(t, v * b) for t, v, b in zip(ts, vs, betas)]
    ws = [_dot_h3(t, k * (b * eg)) for t, k, b, eg in zip(ts, ks, betas, egs)]
    qks = qks or [_dot_nt_bf(q, k) for q, k in zip(qs, ks)]
    last = 0 if reverse else CB - 1
    glogs = [jnp.sum(jnp.where(r == last, gc, 0.0), axis=0, keepdims=True) for gc in gcs]
    outs = [(u, w, q * eg, k * jnp.exp(gl - gc), qk * dc, jnp.exp(gl))
            for u, w, q, k, eg, gl, gc, qk, dc in zip(us, ws, qs, ks, egs, glogs, gcs, qks, decays)]
    return outs, ts


def _cum_matrix(upper):
    r, c = _iota2((CB, CB))
    return ((c >= r) if upper else (c <= r)).astype(F32)


def _lane_bcast(x, col):
    return jnp.broadcast_to(x[:, col:col + 1], x.shape)


_HEAD_SLICES = [slice(h * HD, (h + 1) * HD) for h in range(NH)]


def _dn1_fwd(q, k, v, gb):
    T = q.shape[0]
    nb = T // CB

    def body(q_ref, k_ref, v_ref, gb_ref, *out_refs):
        gbv = gb_ref[...]
        qs = [q_ref[:, sl] for sl in _HEAD_SLICES]
        ks = [k_ref[:, sl] for sl in _HEAD_SLICES]
        vs = [v_ref[:, sl] for sl in _HEAD_SLICES]
        kks = [_dot_nt_bf(x, x) for x in ks]
        qks = [_dot_nt_bf(x, y) for x, y in zip(qs, ks)]
        for d in (0, 1):
            u_ref, w_ref, qg_ref, kd_ref, qkd_ref, gl_ref, t_ref = out_refs[7 * d:7 * d + 7]
            gcum = _dot_h3(_cum_matrix(d == 1), gbv)
            betas = [_lane_bcast(gbv, d * NH + h) for h in range(NH)]
            gcs = [_lane_bcast(gcum, 16 + d * NH + h) for h in range(NH)]
            outs, ts = _dn1_heads(qs, ks, vs, betas, gcs, None, d == 1, kks, qks)
            for h, sl in enumerate(_HEAD_SLICES):
                u, w, qg, kd, qkd, gl = outs[h]
                u_ref[:, sl] = u
                w_ref[:, sl] = w.astype(BF)
                qg_ref[:, sl] = qg.astype(BF)
                kd_ref[:, sl] = kd.astype(BF)
                qkd_ref[:, sl] = qkd.astype(BF)
                gl_ref[h] = gl
                t_ref[:, sl] = ts[h]

    tb = pl.BlockSpec((CB, D), lambda i: (i, 0))
    one_dir_specs = [tb, tb, tb, tb, tb, pl.BlockSpec((NH, 1, 128), lambda i: (i, 0, 0)), tb]
    one_dir_shapes = ([jax.ShapeDtypeStruct((T, D), F32)] + [jax.ShapeDtypeStruct((T, D), BF)] * 4
                      + [jax.ShapeDtypeStruct((nb * NH, 1, 128), F32), jax.ShapeDtypeStruct((T, D), F32)])
    outs = pl.pallas_call(
        body, grid=(nb,), name="dn1_fwd",
        in_specs=[tb, tb, tb, pl.BlockSpec((CB, 128), lambda i: (i, 0))],
        out_specs=one_dir_specs * 2, out_shape=one_dir_shapes * 2, compiler_params=_cp(),
    )(q, k, v, gb)
    return [tuple(outs[:7]), tuple(outs[7:])]


def _dn1_bwd(q, k, v, gb, tinvs, cots):
    T = q.shape[0]
    nb = T // CB

    def body(q_ref, k_ref, v_ref, gb_ref, *refs):
        dir_refs, (dq_ref, dk_ref, dv_ref, dgb_ref) = refs[:14], refs[14:]
        gbv = gb_ref[...]
        qs = [q_ref[:, sl] for sl in _HEAD_SLICES]
        ks = [k_ref[:, sl] for sl in _HEAD_SLICES]
        vs = [v_ref[:, sl] for sl in _HEAD_SLICES]
        lane = lax.broadcasted_iota(jnp.int32, (CB, 128), 1)
        dgb = jnp.zeros((CB, 128), F32)
        for d in (0, 1):
            t_ref, du_ref, dw_ref, dqg_ref, dkd_ref, dqkd_ref, dgl_ref = dir_refs[7 * d:7 * d + 7]
            gcum = _dot_h3(_cum_matrix(d == 1), gbv)
            betas = [_lane_bcast(gbv, d * NH + h) for h in range(NH)]
            gcs = [_lane_bcast(gcum, 16 + d * NH + h) for h in range(NH)]
            ts = [t_ref[:, sl] for sl in _HEAD_SLICES]
            f = lambda qs, ks, vs, betas, gcs: _dn1_heads(qs, ks, vs, betas, gcs, ts, d == 1)[0]
            _, vjp = jax.vjp(f, qs, ks, vs, betas, gcs)
            cot = [(du_ref[:, sl], dw_ref[:, sl], dqg_ref[:, sl], dkd_ref[:, sl], dqkd_ref[:, sl], dgl_ref[h])
                   for h, sl in enumerate(_HEAD_SLICES)]
            dqs, dks, dvs, dbetas, dgcs = vjp(cot)
            dgcum = jnp.zeros((CB, 128), F32)
            for h, sl in enumerate(_HEAD_SLICES):
                if d == 0:
                    dq_ref[:, sl] = dqs[h]
                    dk_ref[:, sl] = dks[h]
                    dv_ref[:, sl] = dvs[h]
                else:
                    dq_ref[:, sl] += dqs[h]
                    dk_ref[:, sl] += dks[h]
                    dv_ref[:, sl] += dvs[h]
                dgb = dgb + jnp.where(lane == d * NH + h, jnp.sum(dbetas[h], axis=1, keepdims=True), 0.0)
                dgcum = dgcum + jnp.where(lane == 16 + d * NH + h, jnp.sum(dgcs[h], axis=1, keepdims=True), 0.0)
            dgb = dgb + _dot_h3(_cum_matrix(d == 0), dgcum)
        dgb_ref[...] = dgb

    tb = pl.BlockSpec((CB, D), lambda i: (i, 0))
    gbs = pl.BlockSpec((CB, 128), lambda i: (i, 0))
    gls = pl.BlockSpec((NH, 1, 128), lambda i: (i, 0, 0))
    args = []
    for d in (0, 1):
        args += [tinvs[d], *cots[d]]
    return pl.pallas_call(
        body, grid=(nb,), name="dn1_bwd",
        in_specs=[tb, tb, tb, gbs] + [tb, tb, tb, tb, tb, tb, gls] * 2, out_specs=[tb, tb, tb, gbs],
        out_shape=[jax.ShapeDtypeStruct((T, D), F32)] * 3 + [jax.ShapeDtypeStruct((T, 128), F32)],
        compiler_params=_cp(),
    )(q, k, v, gb, *args)


def _dn2_steps(chains):
    ws = [_dot_bf(w, s) for _, w, _, _, _, _, s in chains]
    v_new = [c[0] - x for c, x in zip(chains, ws)]
    o_state = [_dot_bf(c[2], c[6]) for c in chains]
    o_local = [_dot_bf(c[4], vn) for c, vn in zip(chains, v_new)]
    grow = [_dot_tn_bf(c[3], vn) for c, vn in zip(chains, v_new)]
    return [a + b for a, b in zip(o_state, o_local)], [c[6] * c[5] + g for c, g in zip(chains, grow)]


def _scan_order(direction, nlat_b, nall_b):
    if direction == 0:
        return lambda i: (i + nlat_b) % nall_b
    return lambda i: nall_b - 1 - i


def _dn2_fwd(per_dir, nlat):
    T = per_dir[0][0].shape[0]
    nb = T // CB
    blks = [_scan_order(d, nlat // CB, nb) for d in (0, 1)]

    def body(*refs):
        ins, outs, s_scr = refs[:12], refs[12:16], refs[16]

        @pl.when(pl.program_id(0) == 0)
        def _():
            s_scr[...] = jnp.zeros_like(s_scr)
        for d in (0, 1):
            outs[2 * d + 1][0] = s_scr[d]
        where = [(d, h, sl) for h, sl in enumerate(_HEAD_SLICES) for d in (0, 1)]
        chains = []
        for d, h, sl in where:
            u_ref, w_ref, qg_ref, kd_ref, qkd_ref, gl_ref = ins[6 * d:6 * d + 6]
            chains.append((u_ref[:, sl], w_ref[:, sl], qg_ref[:, sl], kd_ref[:, sl], qkd_ref[:, sl], gl_ref[h], s_scr[d, h]))
        os, states = _dn2_steps(chains)
        for (d, h, sl), o, s_next in zip(where, os, states):
            outs[2 * d][:, sl] = o
            s_scr[d, h] = s_next

    in_specs, out_specs, args = [], [], []
    for d in (0, 1):
        blk = blks[d]
        tb = pl.BlockSpec((CB, D), lambda i, blk=blk: (blk(i), 0))
        in_specs += [tb] * 5 + [pl.BlockSpec((NH, 1, 128), lambda i, blk=blk: (blk(i), 0, 0))]
        out_specs += [tb, pl.BlockSpec((1, NH, HD, HD), lambda i, blk=blk: (blk(i), 0, 0, 0))]
        args += list(per_dir[d])
    outs = pl.pallas_call(
        body, grid=(nb,), name="dn2_fwd", in_specs=in_specs, out_specs=out_specs,
        out_shape=[jax.ShapeDtypeStruct((T, D), F32), jax.ShapeDtypeStruct((nb, NH, HD, HD), F32)] * 2,
        scratch_shapes=[pltpu.VMEM((2, NH, HD, HD), F32)], compiler_params=_cp(),
    )(*args)
    return [tuple(outs[:2]), tuple(outs[2:])]


def _dn2_bwd(per_dir, do, nlat):
    T = per_dir[0][0].shape[0]
    nb = T // CB
    nlat_b = nlat // CB
    fwd = [_scan_order(d, nlat_b, nb) for d in (0, 1)]
    blks = [lambda i, f=f: f(nb - 1 - i) for f in fwd]

    def body(*refs):
        ins, outs, ds_scr = refs[:16], refs[16:28], refs[28]
        i = pl.program_id(0)

        @pl.when(i == 0)
        def _():
            ds_scr[...] = jnp.zeros_like(ds_scr)
        where = [(d, h, sl) for h, sl in enumerate(_HEAD_SLICES) for d in (0, 1)]
        chains, cot_o, cot_s = [], [], []
        for d, h, sl in where:
            u_ref, w_ref, qg_ref, kd_ref, qkd_ref, gl_ref, sall_ref, do_ref = ins[8 * d:8 * d + 8]
            chains.append((u_ref[:, sl], w_ref[:, sl].astype(F32), qg_ref[:, sl].astype(F32), kd_ref[:, sl].astype(F32),
                           qkd_ref[:, sl].astype(F32), gl_ref[h], sall_ref[0, h]))
            cot_o.append(jnp.where(blks[d](i) < nlat_b, do_ref[:, sl], 0.0))
            cot_s.append(ds_scr[d, h])
        _, vjp = jax.vjp(_dn2_steps, chains)
        for (d, h, sl), (du, dw, dqg, dkd, dqkd, dgl, ds) in zip(where, vjp((cot_o, cot_s))[0]):
            du_ref, dw_ref, dqg_ref, dkd_ref, dqkd_ref, dgl_ref = outs[6 * d:6 * d + 6]
            du_ref[:, sl] = du
            dw_ref[:, sl] = dw
            dqg_ref[:, sl] = dqg
            dkd_ref[:, sl] = dkd
            dqkd_ref[:, sl] = dqkd
            dgl_ref[h] = dgl
            ds_scr[d, h] = ds

    in_specs, out_specs, args = [], [], []
    for d in (0, 1):
        blk = blks[d]
        tb = pl.BlockSpec((CB, D), lambda i, blk=blk: (blk(i), 0))
        gls = pl.BlockSpec((NH, 1, 128), lambda i, blk=blk: (blk(i), 0, 0))
        in_specs += [tb] * 5 + [gls, pl.BlockSpec((1, NH, HD, HD), lambda i, blk=blk: (blk(i), 0, 0, 0)),
                                pl.BlockSpec((CB, D), lambda i, blk=blk: (jnp.minimum(blk(i), nlat_b - 1), 0))]
        out_specs += [tb] * 5 + [gls]
        args += list(per_dir[d]) + [do]
    outs = pl.pallas_call(
        body, grid=(nb,), name="dn2_bwd", in_specs=in_specs, out_specs=out_specs,
        out_shape=([jax.ShapeDtypeStruct((T, D), F32)] * 5 + [jax.ShapeDtypeStruct((nb * NH, 1, 128), F32)]) * 2,
        scratch_shapes=[pltpu.VMEM((2, NH, HD, HD), F32)], compiler_params=_cp(),
    )(*args)
    return [tuple(outs[:6]), tuple(outs[6:])]


def _ghn_fn(o, gt, w):
    y = o * lax.rsqrt(jnp.mean(o * o, axis=-1, keepdims=True) + EPS)
    return (y * w) * jax.nn.silu(gt)


def _ghn_fwd(o_f, o_b, p, w, nlat):
    tb = _tile(nlat, (256, 128))

    def body(of_ref, ob_ref, gt_ref, w_ref, y_ref):
        for h in range(NH):
            sl = slice(h * HD, (h + 1) * HD)
            y_ref[:, sl] = _ghn_fn(of_ref[:, sl] + ob_ref[:, sl], gt_ref[:, sl], w_ref[...]).astype(BF)

    row = pl.BlockSpec((tb, D), lambda i: (i, 0))
    return pl.pallas_call(
        body, grid=(nlat // tb,), name="ghn_fwd",
        in_specs=[row, row, pl.BlockSpec((tb, D), lambda i: (i, O_GT // D)), pl.BlockSpec((1, HD), lambda i: (0, 0))],
        out_specs=row, out_shape=jax.ShapeDtypeStruct((nlat, D), BF),
    )(o_f, o_b, p, w)


def _ghn_bwd(o_f, o_b, p, w, dy, nlat):
    tb = _tile(nlat, (256, 128))

    def body(of_ref, ob_ref, gt_ref, w_ref, dy_ref, do_ref, dgt_ref, dw_ref):
        @pl.when(pl.program_id(0) == 0)
        def _():
            dw_ref[...] = jnp.zeros_like(dw_ref)
        for h in range(NH):
            sl = slice(h * HD, (h + 1) * HD)
            _, vjp = jax.vjp(_ghn_fn, of_ref[:, sl] + ob_ref[:, sl], gt_ref[:, sl], w_ref[...])
            do, dgt, dw = vjp(dy_ref[:, sl])
            do_ref[:, sl] = do
            dgt_ref[:, sl] = dgt.astype(BF)
            dw_ref[...] += dw

    row = pl.BlockSpec((tb, D), lambda i: (i, 0))
    one = pl.BlockSpec((1, HD), lambda i: (0, 0))
    return pl.pallas_call(
        body, grid=(nlat // tb,), name="ghn_bwd",
        in_specs=[row, row, pl.BlockSpec((tb, D), lambda i: (i, O_GT // D)), one, row],
        out_specs=[row, row, one],
        out_shape=[jax.ShapeDtypeStruct((nlat, D), F32), jax.ShapeDtypeStruct((nlat, D), BF), jax.ShapeDtypeStruct((1, HD), F32)],
    )(o_f, o_b, p, w, dy)


@jax.custom_vjp
def _swap32(x):
    lane = lax.broadcasted_iota(jnp.int32, x.shape, 1)
    return jnp.where((lane & 32) == 0, pltpu.roll(x, 96, 1), pltpu.roll(x, 32, 1))


_swap32.defvjp(lambda x: (_swap32(x), None), lambda _, g: (_swap32(g),))


def _qk_post_fn(x, w, cos, sin):
    y = (x * lax.rsqrt(jnp.mean(x * x, axis=-1, keepdims=True) + EPS)) * w
    return y * cos + _swap32(y) * sin


def _attn_prep_fwd(p, qn, kn, cos, sin):
    T = p.shape[0]
    tb = _tile(T, (256, 128))

    def body(q_ref, k_ref, v_ref, qn_ref, kn_ref, cos_ref, sin_ref, qr_ref, kr_ref, vb_ref):
        cos_v, sin_v = cos_ref[...], sin_ref[...]
        for h in range(NH):
            sl = slice(h * HD, (h + 1) * HD)
            qr_ref[:, sl] = _qk_post_fn(q_ref[:, sl], qn_ref[...], cos_v, sin_v).astype(BF)
        for h in range(KVH):
            sl = slice(h * HD, (h + 1) * HD)
            kr_ref[:, sl] = _qk_post_fn(k_ref[:, sl], kn_ref[...], cos_v, sin_v).astype(BF)
        vb_ref[...] = v_ref[...].astype(BF)

    one = pl.BlockSpec((1, HD), lambda i: (0, 0))
    tab = pl.BlockSpec((tb, HD), lambda i: (i, 0))
    return pl.pallas_call(
        body, grid=(T // tb,), name="attn_prep_fwd",
        in_specs=[pl.BlockSpec((tb, D), lambda i: (i, O_Q // D)), pl.BlockSpec((tb, KV), lambda i: (i, O_K // KV)),
                  pl.BlockSpec((tb, KV), lambda i: (i, O_V // KV)), one, one, tab, tab],
        out_specs=[pl.BlockSpec((tb, D), lambda i: (i, 0)), pl.BlockSpec((tb, KV), lambda i: (i, 0)),
                   pl.BlockSpec((tb, KV), lambda i: (i, 0))],
        out_shape=[jax.ShapeDtypeStruct((T, D), BF), jax.ShapeDtypeStruct((T, KV), BF), jax.ShapeDtypeStruct((T, KV), BF)],
    )(p, p, p, qn, kn, cos, sin)


def _attn_prep_bwd(p, qn, kn, cos, sin, dqr, dkp, dvp, dkc, dvc, nlat):
    T = p.shape[0]
    nqb = nlat // CB
    ncb = (T - nlat) // CB

    def body(q_ref, k_ref, v_ref, qn_ref, kn_ref, cos_ref, sin_ref, dqr_ref, dka_ref, dkb_ref, dkc3_ref, dva_ref, dvb_ref, dvc3_ref,
             dkctx_ref, dvctx_ref, dq_ref, dk_ref, dv_ref, dqn_ref, dkn_ref):
        i = pl.program_id(0)
        is_lat = i < nqb
        cos_v, sin_v = cos_ref[...], sin_ref[...]

        @pl.when(i == 0)
        def _():
            dqn_ref[...] = jnp.zeros_like(dqn_ref)
            dkn_ref[...] = jnp.zeros_like(dkn_ref)

        def band_sum(a_ref, b_ref, c_ref, ctx_ref):
            s = b_ref[0] + jnp.where(i > 0, a_ref[0], 0.0) + jnp.where(i < nqb - 1, c_ref[0], 0.0)
            return jnp.where(is_lat, s, ctx_ref[...])

        dkr = band_sum(dka_ref, dkb_ref, dkc3_ref, dkctx_ref)
        dv_ref[...] = band_sum(dva_ref, dvb_ref, dvc3_ref, dvctx_ref).astype(BF)
        for h in range(NH):
            sl = slice(h * HD, (h + 1) * HD)
            _, vjp = jax.vjp(_qk_post_fn, q_ref[:, sl], qn_ref[...], cos_v, sin_v)
            dq, dqn, _, _ = vjp(jnp.where(is_lat, dqr_ref[:, sl], 0.0))
            dq_ref[:, sl] = dq.astype(BF)
            dqn_ref[...] += dqn
        for h in range(KVH):
            sl = slice(h * HD, (h + 1) * HD)
            _, vjp = jax.vjp(_qk_post_fn, k_ref[:, sl], kn_ref[...], cos_v, sin_v)
            dk, dkn, _, _ = vjp(dkr[:, sl])
            dk_ref[:, sl] = dk.astype(BF)
            dkn_ref[...] += dkn

    one = pl.BlockSpec((1, HD), lambda i: (0, 0))
    tab = pl.BlockSpec((CB, HD), lambda i: (i, 0))
    lat = lambda i: jnp.minimum(i, nqb - 1)

    def part(off, slot):
        return pl.BlockSpec((1, CB, KV), lambda i: (jnp.clip(lat(i) + off, 0, nqb - 1) * 3 + slot, 0, 0))

    ctxs = pl.BlockSpec((CB, KV), lambda i: (jnp.clip(i - nqb, 0, ncb - 1), 0))
    kvs = pl.BlockSpec((CB, KV), lambda i: (i, 0))
    return pl.pallas_call(
        body, grid=(T // CB,), name="attn_prep_bwd",
        in_specs=[pl.BlockSpec((CB, D), lambda i: (i, O_Q // D)), pl.BlockSpec((CB, KV), lambda i: (i, O_K // KV)),
                  pl.BlockSpec((CB, KV), lambda i: (i, O_V // KV)), one, one, tab, tab,
                  pl.BlockSpec((CB, D), lambda i: (lat(i), 0)),
                  part(-1, 2), part(0, 1), part(1, 0), part(-1, 2), part(0, 1), part(1, 0), ctxs, ctxs],
        out_specs=[pl.BlockSpec((CB, D), lambda i: (i, 0)), kvs, kvs, one, one],
        out_shape=[jax.ShapeDtypeStruct((T, D), BF), jax.ShapeDtypeStruct((T, KV), BF), jax.ShapeDtypeStruct((T, KV), BF),
                   jax.ShapeDtypeStruct((1, HD), F32), jax.ShapeDtypeStruct((1, HD), F32)],
    )(p, p, p, qn, kn, cos, sin, dqr, dkp, dkp, dkp, dvp, dvp, dvp, dkc, dvc)


def _attn_groups_fn(qs, kalls, valls, sinks, bias):
    groups = range(KVH)
    q = [jnp.concatenate(qs[GRP * g:GRP * (g + 1)], axis=0) for g in groups]
    s = [_dot_nt_bf(q[g], kalls[g]) * (HD ** -0.5) + bias for g in groups]
    sk = [jnp.concatenate([jnp.broadcast_to(jnp.mean(t, axis=1, keepdims=True), (CB, 1)) for t in sinks[GRP * g:GRP * (g + 1)]],
                          axis=0) for g in groups]
    m = [lax.stop_gradient(jnp.maximum(jnp.max(s[g], axis=1, keepdims=True), sk[g])) for g in groups]
    e = [jnp.exp(s[g] - m[g]) for g in groups]
    den = [jnp.sum(e[g], axis=1, keepdims=True) + jnp.exp(sk[g] - m[g]) for g in groups]
    return [_dot_bf(e[g] / den[g], valls[g]) for g in groups]


def _attn_bias(lc):
    r, c = _iota2((GRP * CB, 3 * CB + lc))
    rel = c - (r & (CB - 1))
    win = (rel >= 0) & (rel <= 2 * CB)
    ctx = c >= 3 * CB
    seen = [(win & (c >= CB)) | ctx, win | ctx, (win & (c < 2 * CB)) | ctx]
    return jnp.stack([jnp.where(s, 0.0, -1e30) for s in seen]).astype(F32)


def _attn_specs(nqb, lc, nlat):
    assert nqb >= 2
    qs = pl.BlockSpec((CB, D), lambda i: (i, 0))
    ka = pl.BlockSpec((CB, KV), lambda i: (jnp.maximum(i - 1, 0), 0))
    kb = pl.BlockSpec((CB, KV), lambda i: (i, 0))
    kc = pl.BlockSpec((CB, KV), lambda i: (jnp.minimum(i + 1, nqb - 1), 0))
    kx = pl.BlockSpec((lc, KV), lambda i: (nlat // lc, 0))
    sk = pl.BlockSpec((KVH, 8, 128), lambda i: (0, 0, 0))
    bs = pl.BlockSpec((1, GRP * CB, 3 * CB + lc), lambda i: (jnp.where(i == 0, 0, jnp.where(i == nqb - 1, 2, 1)), 0, 0))
    return qs, ka, kb, kc, kx, sk, bs


def _attn_operands(q_ref, k_refs, v_refs, sk_ref, dtype):
    sls = [slice(g * HD, (g + 1) * HD) for g in range(KVH)]
    kalls = [jnp.concatenate([r[:, sl] for r in k_refs], axis=0).astype(dtype) for sl in sls]
    valls = [jnp.concatenate([r[:, sl] for r in v_refs], axis=0).astype(dtype) for sl in sls]
    qs = [q_ref[:, sl].astype(dtype) for sl in _HEAD_SLICES]
    sinks = [sk_ref[h // GRP, (h % GRP):(h % GRP) + 1, :] for h in range(NH)]
    return qs, kalls, valls, sinks


def _attn_fwd(qr, kr, vb, sink, nlat):
    lc = kr.shape[0] - nlat
    nqb = nlat // CB
    qs, ka, kb, kc, kx, sk, bs = _attn_specs(nqb, lc, nlat)

    def body(q_ref, ka_ref, kb_ref, kc_ref, kx_ref, va_ref, vb_ref, vc_ref, vx_ref, sk_ref, bias_ref, o_ref):
        operands = _attn_operands(q_ref, (ka_ref, kb_ref, kc_ref, kx_ref), (va_ref, vb_ref, vc_ref, vx_ref), sk_ref, BF)
        outs = _attn_groups_fn(*operands, bias_ref[0])
        for h, sl in enumerate(_HEAD_SLICES):
            o_ref[:, sl] = outs[h // GRP][(h % GRP) * CB:(h % GRP + 1) * CB].astype(BF)

    return pl.pallas_call(
        body, grid=(nqb,), name="attn_fwd",
        in_specs=[qs, ka, kb, kc, kx, ka, kb, kc, kx, sk, bs], out_specs=qs,
        out_shape=jax.ShapeDtypeStruct((nlat, D), BF), compiler_params=_cp(),
    )(qr, kr, kr, kr, kr, vb, vb, vb, vb, sink, _attn_bias(lc))


def _attn_bwd(qr, kr, vb, sink, dy, nlat):
    lc = kr.shape[0] - nlat
    nqb = nlat // CB
    qs, ka, kb, kc, kx, sk, bs = _attn_specs(nqb, lc, nlat)

    def body(q_ref, ka_ref, kb_ref, kc_ref, kx_ref, va_ref, vb_ref, vc_ref, vx_ref, sk_ref, dy_ref, bias_ref,
             dq_ref, dkp_ref, dvp_ref, dkx_ref, dvx_ref, dsk_ref):
        operands = _attn_operands(q_ref, (ka_ref, kb_ref, kc_ref, kx_ref), (va_ref, vb_ref, vc_ref, vx_ref), sk_ref, F32)
        _, vjp = jax.vjp(functools.partial(_attn_groups_fn, bias=bias_ref[0]), *operands)
        dys_g = [jnp.concatenate([dy_ref[:, sl] for sl in _HEAD_SLICES[GRP * g:GRP * (g + 1)]], axis=0) for g in range(KVH)]
        dqs, dks, dvs, dsinks = vjp(dys_g)

        @pl.when(pl.program_id(0) == 0)
        def _():
            dkx_ref[...] = jnp.zeros_like(dkx_ref)
            dvx_ref[...] = jnp.zeros_like(dvx_ref)
            dsk_ref[...] = jnp.zeros_like(dsk_ref)

        for h, sl in enumerate(_HEAD_SLICES):
            dq_ref[:, sl] = dqs[h]
            dsk_ref[h // GRP, (h % GRP):(h % GRP) + 1, :] += dsinks[h]
        for g in range(KVH):
            sl = slice(g * HD, (g + 1) * HD)
            for t in range(3):
                dkp_ref[t, :, sl] = dks[g][t * CB:(t + 1) * CB]
                dvp_ref[t, :, sl] = dvs[g][t * CB:(t + 1) * CB]
            dkx_ref[:, sl] += dks[g][3 * CB:]
            dvx_ref[:, sl] += dvs[g][3 * CB:]

    dys = qs
    parts = pl.BlockSpec((3, CB, KV), lambda i: (i, 0, 0))
    ctxo = pl.BlockSpec((lc, KV), lambda i: (0, 0))
    return pl.pallas_call(
        body, grid=(nqb,), name="attn_bwd",
        in_specs=[qs, ka, kb, kc, kx, ka, kb, kc, kx, sk, dys, bs],
        out_specs=[dys, parts, parts, ctxo, ctxo, sk],
        out_shape=[jax.ShapeDtypeStruct((nlat, D), F32), jax.ShapeDtypeStruct((3 * nqb, CB, KV), F32),
                   jax.ShapeDtypeStruct((3 * nqb, CB, KV), F32), jax.ShapeDtypeStruct((lc, KV), F32),
                   jax.ShapeDtypeStruct((lc, KV), F32), jax.ShapeDtypeStruct((KVH, 8, 128), F32)],
        compiler_params=_cp(),
    )(qr, kr, kr, kr, kr, vb, vb, vb, vb, sink, dy, _attn_bias(lc))


def _merge_fn(z_dn, z_at, g_dn, g_at):
    return jax.nn.sigmoid(g_dn) * z_dn + jax.nn.sigmoid(g_at) * z_at


def _merge_fwd(z_dn, z_at, p, nlat):
    tb = _tile(nlat, (256, 128))

    def body(zd_ref, za_ref, gd_ref, ga_ref, o_ref):
        o_ref[...] = _merge_fn(zd_ref[...], za_ref[...], gd_ref[...], ga_ref[...]).astype(BF)

    row = pl.BlockSpec((tb, D), lambda i: (i, 0))
    return pl.pallas_call(
        body, grid=(nlat // tb,), name="merge_fwd",
        in_specs=[row, row, pl.BlockSpec((tb, D), lambda i: (i, O_MG // D)), pl.BlockSpec((tb, D), lambda i: (i, O_MG // D + 1))],
        out_specs=row, out_shape=jax.ShapeDtypeStruct((nlat, D), BF),
    )(z_dn, z_at, p, p)


def _merge_bwd(z_dn, z_at, p, dm, nlat):
    tb = _tile(nlat, (256, 128))

    def body(zd_ref, za_ref, gd_ref, ga_ref, dm_ref, dzd_ref, dza_ref, dg_ref):
        _, vjp = jax.vjp(_merge_fn, zd_ref[...], za_ref[...], gd_ref[...], ga_ref[...])
        dzd, dza, dgd, dga = vjp(dm_ref[...])
        dzd_ref[...] = dzd.astype(BF)
        dza_ref[...] = dza.astype(BF)
        dg_ref[:, :D] = dgd.astype(BF)
        dg_ref[:, D:] = dga.astype(BF)

    row = pl.BlockSpec((tb, D), lambda i: (i, 0))
    return pl.pallas_call(
        body, grid=(nlat // tb,), name="merge_bwd",
        in_specs=[row, row, pl.BlockSpec((tb, D), lambda i: (i, O_MG // D)), pl.BlockSpec((tb, D), lambda i: (i, O_MG // D + 1)), row],
        out_specs=[row, row, pl.BlockSpec((tb, 2 * D), lambda i: (i, 0))],
        out_shape=[jax.ShapeDtypeStruct((nlat, D), BF), jax.ShapeDtypeStruct((nlat, D), BF), jax.ShapeDtypeStruct((nlat, 2 * D), BF)],
    )(z_dn, z_at, p, p, dm)


def _swiglu_fn(ug, uv):
    return jax.nn.silu(ug) * uv


FFN_GROUP = 256


def _ffn_mid_fwd(u, w8, bias):
    n = u.shape[0]
    tb = _tile(n, (256, 128))
    starts, ends = _segment_edges((n,), tb)

    def body(cur_ref, prev_ref, next_ref, w_ref, b_ref, o_ref):
        keep = _keep_halos(pl.program_id(0), starts, ends)
        for c0 in range(0, DFF, FFN_GROUP):
            halves = []
            for cols in (slice(c0, c0 + FFN_GROUP), slice(DFF + c0, DFF + c0 + FFN_GROUP)):
                xe = _ext_rows((cur_ref, prev_ref, next_ref), cols, keep)
                halves.append(_conv_rows(xe, w_ref, cols, FFN_TAPS)[HALO:HALO + tb] + b_ref[:, cols])
            o_ref[:, c0:c0 + FFN_GROUP] = _swiglu_fn(*halves).astype(BF)

    return pl.pallas_call(
        body, grid=(n // tb,), name="ffn_mid_fwd",
        in_specs=_halo_specs(tb, 2 * DFF, n) + [pl.BlockSpec((8, 2 * DFF), lambda i: (0, 0)), pl.BlockSpec((1, 2 * DFF), lambda i: (0, 0))],
        out_specs=pl.BlockSpec((tb, DFF), lambda i: (i, 0)), out_shape=jax.ShapeDtypeStruct((n, DFF), BF),
        compiler_params=_cp(),
    )(u, u, u, w8, bias)


def _ffn_mid_bwd(u, w8, bias, da):
    n = u.shape[0]
    tb = _tile(n, (256, 128))
    starts, ends = _segment_edges((n,), tb)

    def body(cur_ref, prev_ref, next_ref, w_ref, b_ref, da_c, da_p, da_n, du_ref, dw_ref, db_ref):
        i = pl.program_id(0)
        keep = _keep_halos(i, starts, ends)

        @pl.when(i == 0)
        def _():
            dw_ref[...] = jnp.zeros_like(dw_ref)
            db_ref[...] = jnp.zeros_like(db_ref)

        for c0 in range(0, DFF, FFN_GROUP):
            col_pair = (slice(c0, c0 + FFN_GROUP), slice(DFF + c0, DFF + c0 + FFN_GROUP))
            xes = [_ext_rows((cur_ref, prev_ref, next_ref), cols, keep) for cols in col_pair]
            convs = [_conv_rows(xe, w_ref, cols, FFN_TAPS) + b_ref[:, cols] for xe, cols in zip(xes, col_pair)]
            dae = _ext_rows((da_c, da_p, da_n), col_pair[0], keep)
            _, vjp = jax.vjp(_swiglu_fn, *convs)
            for xe, cols, dce in zip(xes, col_pair, vjp(dae)):
                du_ref[:, cols] = _conv_rows(dce, w_ref, cols, FFN_TAPS, transpose=True)[HALO:HALO + tb].astype(BF)
                dcur = dce[HALO:HALO + tb]
                for j, g in enumerate(_tap_grads(dcur, xe, FFN_TAPS, tb)):
                    dw_ref[j:j + 1, cols] += g
                db_ref[:, cols] += jnp.sum(dcur, axis=0, keepdims=True)

    wspec = pl.BlockSpec((8, 2 * DFF), lambda i: (0, 0))
    bspec = pl.BlockSpec((1, 2 * DFF), lambda i: (0, 0))
    return pl.pallas_call(
        body, grid=(n // tb,), name="ffn_mid_bwd",
        in_specs=_halo_specs(tb, 2 * DFF, n) + [wspec, bspec] + _halo_specs(tb, DFF, n),
        out_specs=[pl.BlockSpec((tb, 2 * DFF), lambda i: (i, 0)), wspec, bspec],
        out_shape=[jax.ShapeDtypeStruct((n, 2 * DFF), BF), jax.ShapeDtypeStruct((8, 2 * DFF), F32), jax.ShapeDtypeStruct((1, 2 * DFF), F32)],
        compiler_params=_cp(),
    )(u, u, u, w8, bias, da, da, da)


def _loss_kernel(x1, gate, ff, target):
    n = x1.shape[0]
    tb = _tile(n, (256, 128))

    def body(x_ref, g_ref, f_ref, t_ref, loss_ref, dy_ref, dff_ref, dg_ref):
        err = x_ref[...] + g_ref[...] * f_ref[...] - t_ref[...]
        dy = err * (1.0 / D)
        dy_ref[...] = dy
        dff_ref[...] = (g_ref[...] * dy).astype(BF)

        @pl.when(pl.program_id(0) == 0)
        def _():
            loss_ref[...] = jnp.zeros_like(loss_ref)
            dg_ref[...] = jnp.zeros_like(dg_ref)
        part = 0.5 * jnp.sum(jnp.sum(err * err, axis=1, keepdims=True) * (1.0 / D), axis=0, keepdims=True)
        loss_ref[...] += jnp.broadcast_to(part, (1, 128))
        dg_ref[...] += jnp.sum(dy * f_ref[...], axis=0, keepdims=True)

    row = pl.BlockSpec((tb, D), lambda i: (i, 0))
    one = pl.BlockSpec((1, D), lambda i: (0, 0))
    return pl.pallas_call(
        body, grid=(n // tb,), name="loss",
        in_specs=[row, one, row, row], out_specs=[pl.BlockSpec((1, 128), lambda i: (0, 0)), row, row, one],
        out_shape=[jax.ShapeDtypeStruct((1, 128), F32), jax.ShapeDtypeStruct((n, D), F32),
                   jax.ShapeDtypeStruct((n, D), BF), jax.ShapeDtypeStruct((1, D), F32)],
    )(x1, gate, ff, target)


def _rope_tables(nlat, lc):
    t = jnp.arange(nlat)
    row = (t // GRID_W).astype(F32)
    col = (t % GRID_W).astype(F32)
    inv_freq = ROPE_BASE ** (-jnp.arange(32, dtype=F32) / 32)
    ar, ac = row[:, None] * inv_freq, col[:, None] * inv_freq
    cos = jnp.concatenate([jnp.cos(ar), jnp.cos(ar), jnp.cos(ac), jnp.cos(ac)], axis=1)
    sin = jnp.concatenate([-jnp.sin(ar), jnp.sin(ar), -jnp.sin(ac), jnp.sin(ac)], axis=1)
    cos = jnp.concatenate([cos, jnp.ones((lc, HD), F32)], axis=0)
    sin = jnp.concatenate([sin, jnp.zeros((lc, HD), F32)], axis=0)
    return cos, sin


def _pad_rows8(w):
    return jnp.concatenate([w, jnp.zeros((8 - w.shape[0], w.shape[1]), w.dtype)], axis=0)


def _pack_w_in(w):
    cuts = [sum(IN_SIZES[:i]) for i in range(len(IN_SIZES) + 1)]
    qkv, gt, b, a, q, k, v, mg = [w[:, cuts[i]:cuts[i + 1]] for i in range(len(IN_SIZES))]
    return jnp.concatenate([qkv, gt, q, mg, k, v, b, a, jnp.zeros((w.shape[0], PW - O_BA - 32), w.dtype)], axis=1)


def _unpack_w_in(g):
    return jnp.concatenate([g[:, O_QKV:O_GT], g[:, O_GT:O_Q], g[:, O_BA:O_BA + 32], g[:, O_Q:O_MG], g[:, O_K:O_V],
                            g[:, O_V:O_BA], g[:, O_MG:O_K]], axis=1)


def _local_step(x, ctx, mod_x, mod_c, target, project_in, project_back,
                norm_mix, norm_ffn, dn_conv, a_log, dt_bias, dn_norm, q_norm, k_norm, sink, ffn_conv, ffn_conv_b):
    L, LC = x.shape[0], ctx.shape[0]
    T = L + LC
    xc = jnp.concatenate([x, ctx], axis=0)
    seg = lambda r: jnp.stack([mod_x[r], mod_c[r]])[:, None, :]
    sh_a, sc_a = seg(0), seg(1)
    g_a, g_f = mod_x[2][None], mod_x[5][None]
    sh_f, sc_f = mod_x[3][None], mod_x[4][None]
    cos, sin = _rope_tables(L, LC)
    dnc8 = _pad_rows8(dn_conv)
    ffc8 = _pad_rows8(ffn_conv)
    gate_row = lambda a: jnp.concatenate([jnp.zeros((1, 16), F32), a.reshape(1, 16), jnp.zeros((1, 96), F32)], axis=1)
    alog_row, dt_row = gate_row(a_log), gate_row(dt_bias)
    sinkb = jnp.concatenate([jnp.broadcast_to(sink.reshape(KVH, GRP, 1), (KVH, GRP, 128)), jnp.zeros((KVH, 8 - GRP, 128), F32)], axis=1)

    h1 = _norm_mod_fwd(xc, norm_mix, sh_a, sc_a, L, "norm_mix_fwd")
    p, (w_in_p, w_bdn, w_bat, w_out, w_up, w_down) = project_in(h1)
    q, k, v, gb = _dn_pre_fwd(p, dnc8, alog_row, dt_row, (L, LC))
    wy = _dn1_fwd(q, k, v, gb)
    scans = _dn2_fwd([t[:6] for t in wy], L)
    o_dir = [s[0] for s in scans]
    y_dn = _ghn_fwd(o_dir[0], o_dir[1], p, dn_norm, L)
    qr, kr, vb = _attn_prep_fwd(p, q_norm, k_norm, cos, sin)
    y_at = _attn_fwd(qr, kr, vb, sinkb, L)
    z_dn = _mm(y_dn, w_bdn, form="nn", out_dtype=F32, name="branch_dn")
    z_at = _mm(y_at, w_bat, form="nn", out_dtype=F32, name="branch_at")
    merged = _merge_fwd(z_dn, z_at, p, L)
    mix = _mm(merged, w_out, form="nn", out_dtype=F32, name="out_proj")
    x1, h2 = _resid_norm_fwd(xc, g_a, mix, norm_ffn, sh_f, sc_f)
    u_raw = _mm(h2, w_up, form="nn", out_dtype=F32, name="ffn_up")
    act = _ffn_mid_fwd(u_raw, ffc8, ffn_conv_b)
    ff = _mm(act, w_down, form="nn", out_dtype=F32, name="ffn_down")
    loss_row, dy, dff, dg_f = _loss_kernel(x1, g_f, ff, target)

    g_down = _mm(act, dff, form="tn", out_dtype=BF, name="g_ffn_down")
    dact = _mm(dff, w_down, form="nt", out_dtype=F32, name="d_act")
    du_raw, g_ffc8, g_ffb = _ffn_mid_bwd(u_raw, ffc8, ffn_conv_b, dact)
    g_up = _mm(h2, du_raw, form="tn", out_dtype=BF, name="g_ffn_up")
    dh2 = _mm(du_raw, w_up, form="nt", out_dtype=F32, name="d_h2")
    dx1, dmix, dg_a, g_nffn, dsh_f, dsc_f = _resid_norm_bwd(x1, g_a, mix, norm_ffn, sh_f, sc_f, dh2, dy)

    g_out = _mm(merged, dmix, form="tn", out_dtype=BF, name="g_w_out")
    dmerged = _mm(dmix, w_out, form="nt", out_dtype=F32, name="d_merged")
    dz_dn, dz_at, dmg = _merge_bwd(z_dn, z_at, p, dmerged, L)
    g_bdn = _mm(y_dn, dz_dn, form="tn", out_dtype=BF, name="g_branch_dn")
    g_bat = _mm(y_at, dz_at, form="tn", out_dtype=BF, name="g_branch_at")
    dy_dn = _mm(dz_dn, w_bdn, form="nt", out_dtype=F32, name="d_y_dn")
    dy_at = _mm(dz_at, w_bat, form="nt", out_dtype=F32, name="d_y_at")
    dqr, dkp, dvp, dkx, dvx, dsink = _attn_bwd(qr, kr, vb, sinkb, dy_at, L)
    dq_raw, dk_raw, dv_raw, g_qn, g_kn = _attn_prep_bwd(p, q_norm, k_norm, cos, sin, dqr, dkp, dvp, dkx, dvx, L)
    do, dgt, g_dnn = _ghn_bwd(o_dir[0], o_dir[1], p, dn_norm, dy_dn, L)
    cots = _dn2_bwd([wy[d][:6] + (scans[d][1],) for d in (0, 1)], do, L)
    dq, dk, dv, dgb = _dn1_bwd(q, k, v, gb, [t[6] for t in wy], cots)
    dqkv_raw, dba, g_dnc8, g_alog, g_dt = _dn_pre_bwd(p, dnc8, alog_row, dt_row, dq, dk, dv, dgb, (L, LC))
    padc = lambda a: jnp.concatenate([a, jnp.zeros((LC, a.shape[1]), a.dtype)], axis=0)
    dp = jnp.concatenate([dqkv_raw, padc(dgt), dq_raw, padc(dmg), dk_raw, dv_raw, dba, jnp.zeros((T, PW - O_BA - 128), BF)], axis=1)
    big, dh1 = project_back(h1, dp, w_in_p, (g_bdn, g_bat, g_out, g_up, g_down))
    grad_x, g_nmix_x, dsh_a, dsc_a = _norm_mod_bwd(xc, norm_mix, mod_x[0][None], mod_x[1][None], dh1, row0=0, nrows=L,
                                                   name="norm_mix_bwd", residual=dx1)
    g_nmix_c, dsh_c, dsc_c = _norm_mod_bwd(xc, norm_mix, mod_c[0][None], mod_c[1][None], dh1, row0=L, nrows=LC,
                                           name="norm_mix_bwd_ctx")
    g_nmix = g_nmix_x + g_nmix_c

    zero = jnp.zeros((D,), F32)
    dmod_x = jnp.stack([dsh_a[0], dsc_a[0], dg_a[0], dsh_f[0], dsc_f[0], dg_f[0]])
    dmod_c = jnp.stack([dsh_c[0], dsc_c[0], zero, zero, zero, zero])
    small = dict(
        dmod_x=dmod_x, dmod_c=dmod_c, norm_mix=g_nmix, norm_ffn=g_nffn, dn_conv=g_dnc8[:5], dn_a_log=g_alog[0, 16:32].reshape(2, 8),
        dn_dt_bias=g_dt[0, 16:32].reshape(2, 8), dn_norm=g_dnn, q_norm=g_qn, k_norm=g_kn,
        attn_sink=jnp.sum(dsink[:, :GRP, :], axis=2).reshape(1, NH), ffn_conv=g_ffc8[:3], ffn_conv_b=g_ffb)
    return loss_row[0, 0], grad_x, big, small


def _exchange(arrays, scatter, name):
    n = len(arrays)

    def body(*refs):
        args = (refs[:n], refs[n:2 * n], *refs[2 * n:], scatter)
        _exchange_start(*args)
        _exchange_wait(*args)

    hbm = pl.BlockSpec(memory_space=pl.ANY)
    out_shape, sems = _exchange_shapes(arrays, scatter)
    return pl.pallas_call(body, name=name, in_specs=[hbm] * n, out_specs=[hbm] * n, out_shape=out_shape,
                          scratch_shapes=sems)(*arrays)


def _ada_fwd(c16, w_ada, b_ada):
    def body(c_ref, w_ref, b_ref, o_ref):
        o_ref[...] = _dot_hi(jax.nn.silu(c_ref[...]), w_ref[...]) + b_ref[...]

    return pl.pallas_call(body, name="ada_fwd", out_shape=jax.ShapeDtypeStruct((16, w_ada.shape[1]), F32))(c16, w_ada, b_ada)


def _ada_bwd(c16, w_ada, dmx, dmc):
    def body(c_ref, w_ref, dmx_ref, dmc_ref, gw_ref, pc_ref):
        dmc_tot = dmc_ref[0:1, :]
        for d in range(1, N_DEV):
            dmc_tot = dmc_tot + dmc_ref[d:d + 1, :]
        dm16 = jnp.concatenate([dmx_ref[...], jnp.broadcast_to(dmc_tot, (8, dmc_tot.shape[1]))], axis=0)
        row = lax.broadcasted_iota(jnp.int32, dm16.shape, 0)
        dm16 = jnp.where(row <= 8, dm16, 0.0)
        s = jax.nn.silu(c_ref[...])
        gw_ref[...] = lax.dot_general(s, dm16, (_DIMS["tn"], ((), ())), precision=HI, preferred_element_type=F32)
        pc = lax.dot_general(dm16, w_ref[...], (_DIMS["nt"], ((), ())), precision=HI, preferred_element_type=F32)
        pc_ref[...] = pc[8:9, :]

    return pl.pallas_call(body, name="ada_bwd", out_shape=[jax.ShapeDtypeStruct(w_ada.shape, F32), jax.ShapeDtypeStruct((1, D), F32)],
                          compiler_params=_cp())(c16, w_ada, dmx, dmc)


def _cctx_grad(pc_all, c_ctx_row):
    def body(pc_ref, c_ref, g_ref):
        tot = pc_ref[0]
        for d in range(1, N_DEV):
            tot = tot + pc_ref[d]
        _, vjp = jax.vjp(jax.nn.silu, c_ref[...])
        g_ref[...] = vjp(tot)[0]

    return pl.pallas_call(body, name="cctx_grad", out_shape=jax.ShapeDtypeStruct((1, D), F32))(pc_all, c_ctx_row)


def _adamw(parts, w, m, v, name):
    ns, R, C = parts.shape
    tb = _tile(R, (128, 64, 32, 16, 8))

    def body(p_ref, w_ref, m_ref, v_ref, g_ref, d_ref, mo_ref, vo_ref):
        g = p_ref[0].astype(F32)
        for s in range(1, ns):
            g = g + p_ref[s].astype(F32)
        m2 = ADAM_B1 * m_ref[...] + (1.0 - ADAM_B1) * g
        v2 = ADAM_B2 * v_ref[...] + (1.0 - ADAM_B2) * jnp.square(g)
        m_hat = m2 / (1.0 - ADAM_B1 ** ADAM_STEP)
        v_hat = v2 / (1.0 - ADAM_B2 ** ADAM_STEP)
        g_ref[...] = g
        d_ref[...] = -ADAM_LR * (m_hat / (jnp.sqrt(v_hat) + ADAM_EPS) + ADAM_WD * w_ref[...])
        mo_ref[...] = m2
        vo_ref[...] = v2

    row = pl.BlockSpec((tb, C), lambda i: (i, 0))
    return pl.pallas_call(
        body, grid=(R // tb,), name=name,
        in_specs=[pl.BlockSpec((ns, tb, C), lambda i: (0, i, 0)), row, row, row], out_specs=[row] * 4,
        out_shape=[jax.ShapeDtypeStruct((R, C), F32)] * 4, compiler_params=_cp(),
    )(parts, w, m, v)


_SMALL = (("dmod_x", 6 * D), ("dmod_c", 6 * D), ("b_ada", 6 * D), ("norm_mix", D), ("norm_ffn", D), ("dn_a_log", 16),
          ("dn_dt_bias", 16), ("dn_norm", HD), ("q_norm", HD), ("k_norm", HD), ("attn_sink", NH), ("ffn_conv_b", 2 * DFF),
          ("dn_conv", 5 * 3 * D), ("ffn_conv", 3 * 2 * DFF))
_SMALL_ROWS = -(-sum(n for _, n in _SMALL) // 1024) * 8


def _pack_small(d):
    flat = jnp.concatenate([d[k].reshape(-1).astype(F32) if k in d else jnp.zeros((n,), F32) for k, n in _SMALL])
    return jnp.concatenate([flat, jnp.zeros((_SMALL_ROWS * 128 - flat.shape[0],), F32)]).reshape(_SMALL_ROWS, 128)


def _unpack_small(a):
    flat = a.reshape(a.shape[:-2] + (-1,))
    out, off = {}, 0
    for k, n in _SMALL:
        out[k] = flat[..., off:off + n]
        off += n
    return out


def kernel(x, c, ctx, c_ctx, w_ada, b_ada, norm_mix, norm_ffn, w_in, dn_conv, dn_a_log, dn_dt_bias, dn_norm, q_norm, k_norm, attn_sink, w_branch_dn, w_branch_attn, w_out, ffn_up, ffn_conv, ffn_conv_b, ffn_down, loss_target, m_c_ctx, m_w_ada, m_b_ada, m_norm_mix, m_norm_ffn, m_w_in, m_dn_conv, m_dn_a_log, m_dn_dt_bias, m_dn_norm, m_q_norm, m_k_norm, m_attn_sink, m_w_branch_dn, m_w_branch_attn, m_w_out, m_ffn_up, m_ffn_conv, m_ffn_conv_b, m_ffn_down, v_c_ctx, v_w_ada, v_b_ada, v_norm_mix, v_norm_ffn, v_w_in, v_dn_conv, v_dn_a_log, v_dn_dt_bias, v_dn_norm, v_q_norm, v_k_norm, v_attn_sink, v_w_branch_dn, v_w_branch_attn, v_w_out, v_ffn_up, v_ffn_conv, v_ffn_conv_b, v_ffn_down):
    me = 4 * lax.axis_index("x") + 2 * lax.axis_index("y") + lax.axis_index("c")
    ada_cols = w_ada.shape[2]

    cols = lambda a: jnp.swapaxes(a, 0, 1).reshape(a.shape[1], -1)
    rows = lambda a: a.reshape(-1, a.shape[2])
    col_blocks = lambda g: jnp.swapaxes(g.reshape(g.shape[0], N_DEV, -1), 0, 1)
    row_blocks = lambda g: g.reshape(N_DEV, -1, g.shape[1])

    gathered = _exchange([w_in[0].astype(BF), c, dn_conv[0], ffn_conv[0]], scatter=False, name="gather_first")
    w_in_packed = _pack_w_in(cols(gathered[0]))
    c_all = gathered[1][:, 0, :]

    def project_in(h1):
        p, rest = _mm(h1, w_in_packed, form="nn", out_dtype=F32, name="in_proj",
                      exchange=([w_branch_dn[0].astype(BF), w_branch_attn[0].astype(BF), w_out[0].astype(BF),
                                 ffn_up[0].astype(BF), ffn_down[0].astype(BF)], False))
        return p, (w_in_packed, rows(rest[0]), rows(rest[1]), rows(rest[2]), cols(rest[3]), rows(rest[4]))

    def project_back(h1, dp, w_in_p, grads):
        g_bdn, g_bat, g_out, g_up, g_down = grads
        g_in, landed_rest = _mm(h1, dp, form="tn", out_dtype=BF, name="g_w_in",
                                exchange=([row_blocks(g_bdn), row_blocks(g_bat), row_blocks(g_out), col_blocks(g_up),
                                           row_blocks(g_down)], True))
        dh1, landed_in = _mm(dp, w_in_p, form="nt", out_dtype=F32, name="d_h1",
                             exchange=([col_blocks(_unpack_w_in(g_in))], True))
        return [landed_in[0]] + landed_rest, dh1

    c16 = jnp.concatenate([c_all, c_ctx[None], jnp.zeros((7, D), F32)], axis=0)
    b_loc = lax.dynamic_slice_in_dim(b_ada, me * ada_cols, ada_cols, axis=1)
    mod_part = _ada_fwd(c16, w_ada[0], b_loc)
    mod_all = cols(_exchange([mod_part], scatter=False, name="gather_mod")[0])
    mod_x = lax.dynamic_slice_in_dim(mod_all, me, 1, axis=0).reshape(6, D)
    mod_c = mod_all[8].reshape(6, D)

    loss_loc, grad_x, landed, small = _local_step(
        x[0], ctx[0], mod_x, mod_c, loss_target[0], project_in, project_back,
        norm_mix, norm_ffn, cols(gathered[2]), dn_a_log[0], dn_dt_bias[0], dn_norm, q_norm, k_norm, attn_sink[0], cols(gathered[3]),
        ffn_conv_b)
    loss = lax.psum(loss_loc, ("x", "y", "c"))

    res = {}
    res["w_in"] = _adamw(landed[0], w_in[0], m_w_in[0], v_w_in[0], "adamw_w_in")
    res["w_branch_dn"] = _adamw(landed[1], w_branch_dn[0], m_w_branch_dn[0], v_w_branch_dn[0], "adamw_w_branch_dn")
    res["w_branch_attn"] = _adamw(landed[2], w_branch_attn[0], m_w_branch_attn[0], v_w_branch_attn[0], "adamw_w_branch_attn")
    res["w_out"] = _adamw(landed[3], w_out[0], m_w_out[0], v_w_out[0], "adamw_w_out")
    res["ffn_up"] = _adamw(landed[4], ffn_up[0], m_ffn_up[0], v_ffn_up[0], "adamw_ffn_up")
    res["ffn_down"] = _adamw(landed[5], ffn_down[0], m_ffn_down[0], v_ffn_down[0], "adamw_ffn_down")

    small = dict(small)
    small["b_ada"] = small["dmod_x"] + small["dmod_c"]
    parts = _exchange([_pack_small(small)], scatter=False, name="gather_small")[0]
    per_dev = _unpack_small(parts)
    given = dict(b_ada=(b_ada, m_b_ada, v_b_ada), norm_mix=(norm_mix, m_norm_mix, v_norm_mix), norm_ffn=(norm_ffn, m_norm_ffn, v_norm_ffn),
                 dn_a_log=(dn_a_log, m_dn_a_log, v_dn_a_log), dn_dt_bias=(dn_dt_bias, m_dn_dt_bias, v_dn_dt_bias),
                 dn_norm=(dn_norm, m_dn_norm, v_dn_norm), q_norm=(q_norm, m_q_norm, v_q_norm), k_norm=(k_norm, m_k_norm, v_k_norm),
                 attn_sink=(attn_sink, m_attn_sink, v_attn_sink), ffn_conv_b=(ffn_conv_b, m_ffn_conv_b, v_ffn_conv_b))
    packs = [_pack_small({k: t[j] for k, t in given.items()}) for j in range(3)]
    upd = [_unpack_small(a) for a in _adamw(parts, packs[0], packs[1], packs[2], "adamw_small")]
    for k, t in given.items():
        res[k] = tuple(u[k].reshape(t[0].shape) for u in upd)
    dnc = lax.dynamic_slice_in_dim(upd[0]["dn_conv"].reshape(5, 3 * D), me * dn_conv.shape[2], dn_conv.shape[2], axis=1)
    ffc = lax.dynamic_slice_in_dim(upd[0]["ffn_conv"].reshape(3, 2 * DFF), me * ffn_conv.shape[2], ffn_conv.shape[2], axis=1)
    r8 = lambda a: _pad_rows8(a)
    t = _adamw(r8(dnc)[None], r8(dn_conv[0]), r8(m_dn_conv[0]), r8(v_dn_conv[0]), "adamw_dn_conv")
    res["dn_conv"] = tuple(a[:5][None] for a in t)
    t = _adamw(r8(ffc)[None], r8(ffn_conv[0]), r8(m_ffn_conv[0]), r8(v_ffn_conv[0]), "adamw_ffn_conv")
    res["ffn_conv"] = tuple(a[:3][None] for a in t)

    dmx = lax.dynamic_slice_in_dim(per_dev["dmod_x"], me * ada_cols, ada_cols, axis=1)
    dmc = lax.dynamic_slice_in_dim(per_dev["dmod_c"], me * ada_cols, ada_cols, axis=1)
    g_ada, pc = _ada_bwd(c16, w_ada[0], dmx, dmc)
    res["w_ada"] = _adamw(g_ada[None], w_ada[0], m_w_ada[0], v_w_ada[0], "adamw_w_ada")
    pc_all = _exchange([pc], scatter=False, name="gather_cctx")[0]
    g_cctx = _cctx_grad(pc_all, c_ctx[None])
    r8b = lambda a: jnp.broadcast_to(a, (8, D))
    t = _adamw(r8b(g_cctx)[None], r8b(c_ctx[None]), r8b(m_c_ctx[None]), r8b(v_c_ctx[None]), "adamw_c_ctx")
    res["c_ctx"] = tuple(a[0] for a in t)

    names = ("c_ctx", "w_ada", "b_ada", "norm_mix", "norm_ffn", "w_in", "dn_conv", "dn_a_log", "dn_dt_bias", "dn_norm", "q_norm",
             "k_norm", "attn_sink", "w_branch_dn", "w_branch_attn", "w_out", "ffn_up", "ffn_conv", "ffn_conv_b", "ffn_down")
    lead = ("w_ada", "w_in", "w_branch_dn", "w_branch_attn", "w_out", "ffn_up", "ffn_down")
    fix = lambda k, a: a[None] if k in lead else a
    outs = [loss, grad_x[None]]
    for j in range(4):
        outs += [fix(k, res[k][j]) for k in names]
    return tuple(outs)
```

```python
import functools

import jax
import jax.numpy as jnp
from jax import lax
from jax.experimental import pallas as pl
from jax.experimental.pallas import tpu as pltpu

F32 = jnp.float32
BF = jnp.bfloat16
HI = lax.Precision.HIGHEST
MESH = pl.DeviceIdType.MESH

D = 1024
NH = 8
HD = 128
KVH = 2
GRP = 4
KV = KVH * HD
DFF = 2816
CB = 128
GRID_W = 64
ROPE_BASE = 10000.0
EPS = 1e-6
N_DEV = 8
PW = 8192
O_QKV, O_GT, O_Q, O_MG, O_K, O_V, O_BA = 0, 3072, 4096, 5120, 7168, 7424, 7680
IN_SIZES = (3072, 1024, 16, 16, 1024, 256, 256, 2048)
IN_DIM = sum(IN_SIZES)
ADAM_LR, ADAM_B1, ADAM_B2, ADAM_EPS, ADAM_WD, ADAM_STEP = 0.001, 0.9, 0.999, 1e-08, 0.01, 10
VMEM_LIMIT = 56 * 1024 * 1024


def _cp():
    return pltpu.CompilerParams(vmem_limit_bytes=VMEM_LIMIT)


def _tile(n, cands):
    for c in cands:
        if n % c == 0:
            return c
    return n


def _iota2(shape):
    return lax.broadcasted_iota(jnp.int32, shape, 0), lax.broadcasted_iota(jnp.int32, shape, 1)


_DIMS = {"nn": ((1,), (0,)), "nt": ((1,), (1,)), "tn": ((0,), (0,))}


def _exchange_copies(ins, outs, send_sems, recv_sems, local_sems, scatter, landings):
    x, y, c = lax.axis_index("x"), lax.axis_index("y"), lax.axis_index("c")
    me = 4 * x + 2 * y + c
    local, remote = [], []
    for k in range(len(ins)):
        local.append(pltpu.make_async_copy(ins[k].at[me] if scatter else ins[k], outs[k].at[me], local_sems.at[k]))
        for m in range(1, N_DEV):
            px = 1 - x if m & 4 else x
            py = 1 - y if m & 2 else y
            pc = 1 - c if m & 1 else c
            peer = 4 * px + 2 * py + pc
            src = ins[k].at[peer] if scatter else ins[k]
            sem = k * (N_DEV - 1) + m - 1
            push = pltpu.make_async_remote_copy(src_ref=src, dst_ref=outs[k].at[me], send_sem=send_sems.at[sem],
                                                recv_sem=recv_sems.at[sem], device_id=(px, py, pc), device_id_type=MESH)
            landing = None
            if landings:
                landing = pltpu.make_async_remote_copy(src_ref=src, dst_ref=outs[k].at[peer], send_sem=send_sems.at[sem],
                                                       recv_sem=recv_sems.at[sem], device_id=(px, py, pc), device_id_type=MESH)
            remote.append((push, landing))
    return local, remote


def _exchange_start(*args):
    local, remote = _exchange_copies(*args, landings=False)
    for cp in local:
        cp.start()
    for push, _ in remote:
        push.start()


def _exchange_wait(*args):
    local, remote = _exchange_copies(*args, landings=True)
    for _, landing in remote:
        landing.wait_recv()
    for push, _ in remote:
        push.wait_send()
    for cp in local:
        cp.wait()


def _exchange_shapes(arrays, scatter):
    out_shape = [jax.ShapeDtypeStruct(a.shape if scatter else (N_DEV,) + a.shape, a.dtype) for a in arrays]
    n = len(arrays)
    sems = [pltpu.SemaphoreType.DMA((n * (N_DEV - 1),)), pltpu.SemaphoreType.DMA((n * (N_DEV - 1),)), pltpu.SemaphoreType.DMA((n,))]
    return out_shape, sems


def _mm(a, b, *, form, out_dtype, name, tm=None, tn=None, tk=None, exchange=None):
    if form == "tn":
        K, M = a.shape
        N = b.shape[1]
    else:
        M, K = a.shape
        N = b.shape[0] if form == "nt" else b.shape[1]
    tm = tm or _tile(M, (1280, 1024, 640, 512, 256, 128))
    tn = tn or _tile(N, (1408, 1024, 512, 256, 128))
    tk = tk or _tile(K, (2048, 1408, 1280, 1024, 640, 512, 256, 128))
    ni, nj, nk = M // tm, N // tn, K // tk
    dims = (_DIMS[form], ((), ()))
    ex_arrays, scatter = exchange if exchange else ([], False)
    nx = len(ex_arrays)

    def body(a_ref, b_ref, *refs):
        ex_in, o_ref, ex_out, scratch = refs[:nx], refs[nx], refs[nx + 1:2 * nx + 1], refs[2 * nx + 1:]
        i, j, k = pl.program_id(0), pl.program_id(1), pl.program_id(2)
        if nx:
            sems = scratch[-3:]

            @pl.when((i == 0) & (j == 0) & (k == 0))
            def _():
                _exchange_start(ex_in, ex_out, *sems, scatter)

        part = lax.dot_general(a_ref[...].astype(BF), b_ref[...].astype(BF), dims, preferred_element_type=F32)
        if nk == 1:
            o_ref[...] = part.astype(out_dtype)
        else:
            acc_ref = scratch[0]

            @pl.when(k == 0)
            def _():
                acc_ref[...] = part

            @pl.when(k > 0)
            def _():
                acc_ref[...] += part

            @pl.when(k == nk - 1)
            def _():
                o_ref[...] = acc_ref[...].astype(out_dtype)

        if nx:
            @pl.when((i == ni - 1) & (j == nj - 1) & (k == nk - 1))
            def _():
                _exchange_wait(ex_in, ex_out, *sems, scatter)

    if form == "tn":
        a_spec = pl.BlockSpec((tk, tm), lambda i, j, k: (k, i))
    else:
        a_spec = pl.BlockSpec((tm, tk), lambda i, j, k: (i, k))
    if form == "nt":
        b_spec = pl.BlockSpec((tn, tk), lambda i, j, k: (j, k))
    else:
        b_spec = pl.BlockSpec((tk, tn), lambda i, j, k: (k, j))
    hbm = pl.BlockSpec(memory_space=pl.ANY)
    ex_shapes, ex_sems = _exchange_shapes(ex_arrays, scatter) if nx else ([], [])
    outs = pl.pallas_call(
        body, grid=(ni, nj, nk), name=name,
        in_specs=[a_spec, b_spec] + [hbm] * nx, out_specs=[pl.BlockSpec((tm, tn), lambda i, j, k: (i, j))] + [hbm] * nx,
        out_shape=[jax.ShapeDtypeStruct((M, N), out_dtype)] + ex_shapes,
        scratch_shapes=([] if nk == 1 else [pltpu.VMEM((tm, tn), F32)]) + ex_sems,
        compiler_params=_cp(),
    )(a, b, *ex_arrays)
    return (outs[0], list(outs[1:])) if nx else outs[0]


def _norm_mod_fn(x, nw, sh, sc):
    y = x * lax.rsqrt(jnp.mean(x * x, axis=-1, keepdims=True) + EPS)
    return (y * nw) * (1.0 + sc) + sh


def _norm_mod_fwd(x, ctx, nw, sh, sc, name):
    nlat = x.shape[0]
    T = nlat + ctx.shape[0]
    tb = _tile(ctx.shape[0], (256, 128))
    nlb = nlat // tb

    def body(x_ref, c_ref, nw_ref, sh_ref, sc_ref, h_ref):
        rows = jnp.where(pl.program_id(0) < nlb, x_ref[...], c_ref[...])
        h_ref[...] = _norm_mod_fn(rows, nw_ref[...], sh_ref[0], sc_ref[0]).astype(BF)

    seg = pl.BlockSpec((1, 1, D), lambda i: (jnp.where(i >= nlb, 1, 0), 0, 0))
    return pl.pallas_call(
        body, grid=(T // tb,), name=name,
        in_specs=[pl.BlockSpec((tb, D), lambda i: (jnp.minimum(i, nlb - 1), 0)),
                  pl.BlockSpec((tb, D), lambda i: (jnp.maximum(i - nlb, 0), 0)), pl.BlockSpec((1, D), lambda i: (0, 0)), seg, seg],
        out_specs=pl.BlockSpec((tb, D), lambda i: (i, 0)),
        out_shape=jax.ShapeDtypeStruct((T, D), BF),
    )(x, ctx, nw, sh, sc)


def _norm_mod_bwd(x, nw, sh, sc, dh, *, row0, name, residual=None):
    nrows = x.shape[0]
    tb = _tile(nrows, (256, 128))
    b0 = row0 // tb

    def body(x_ref, nw_ref, sh_ref, sc_ref, dh_ref, *refs):
        dnw_ref, dsh_ref, dsc_ref = refs[-3:]
        _, vjp = jax.vjp(_norm_mod_fn, x_ref[...], nw_ref[...], sh_ref[...], sc_ref[...])
        dx, dnw, dsh, dsc = vjp(dh_ref[...])
        if residual is not None:
            refs[1][...] = dx + refs[0][...]

        @pl.when(pl.program_id(0) == 0)
        def _():
            dnw_ref[...] = jnp.zeros_like(dnw_ref)
            dsh_ref[...] = jnp.zeros_like(dsh_ref)
            dsc_ref[...] = jnp.zeros_like(dsc_ref)

        dnw_ref[...] += dnw
        dsh_ref[...] += dsh
        dsc_ref[...] += dsc

    dh_row = pl.BlockSpec((tb, D), lambda i: (b0 + i, 0))
    out_row = pl.BlockSpec((tb, D), lambda i: (i, 0))
    one = pl.BlockSpec((1, D), lambda i: (0, 0))
    with_dx = residual is not None
    return pl.pallas_call(
        body, grid=(nrows // tb,), name=name,
        in_specs=[out_row, one, one, one, dh_row] + [out_row] * with_dx, out_specs=[out_row] * with_dx + [one] * 3,
        out_shape=[jax.ShapeDtypeStruct((nrows, D), F32)] * with_dx + [jax.ShapeDtypeStruct((1, D), F32)] * 3,
    )(x, nw, sh, sc, dh, *([residual] if with_dx else []))


def _resid_norm_fwd(x, gate, y, nw, sh, sc):
    n = y.shape[0]
    tb = _tile(n, (256, 128))

    def body(x_ref, g_ref, y_ref, nw_ref, sh_ref, sc_ref, x1_ref, h_ref):
        x1 = x_ref[...] + g_ref[...] * y_ref[...]
        x1_ref[...] = x1
        h_ref[...] = _norm_mod_fn(x1, nw_ref[...], sh_ref[...], sc_ref[...]).astype(BF)

    row = pl.BlockSpec((tb, D), lambda i: (i, 0))
    one = pl.BlockSpec((1, D), lambda i: (0, 0))
    return pl.pallas_call(
        body, grid=(n // tb,), name="resid_norm_fwd",
        in_specs=[row, one, row, one, one, one], out_specs=[row, row],
        out_shape=[jax.ShapeDtypeStruct((n, D), F32), jax.ShapeDtypeStruct((n, D), BF)],
    )(x, gate, y, nw, sh, sc)


def _resid_norm_bwd(x1, gate, y, nw, sh, sc, dh, dx1_direct):
    n = y.shape[0]
    tb = _tile(n, (256, 128))

    def body(x1_ref, g_ref, y_ref, nw_ref, sh_ref, sc_ref, dh_ref, dd_ref, dx_ref, dy_ref, dg_ref, dnw_ref, dsh_ref, dsc_ref):
        _, vjp = jax.vjp(_norm_mod_fn, x1_ref[...], nw_ref[...], sh_ref[...], sc_ref[...])
        dxn, dnw, dsh, dsc = vjp(dh_ref[...])
        dx = dxn + dd_ref[...]
        dx_ref[...] = dx
        dy_ref[...] = (g_ref[...] * dx).astype(BF)

        @pl.when(pl.program_id(0) == 0)
        def _():
            for r in (dg_ref, dnw_ref, dsh_ref, dsc_ref):
                r[...] = jnp.zeros_like(r)

        dg_ref[...] += jnp.sum(dx * y_ref[...], axis=0, keepdims=True)
        dnw_ref[...] += dnw
        dsh_ref[...] += dsh
        dsc_ref[...] += dsc

    row = pl.BlockSpec((tb, D), lambda i: (i, 0))
    one = pl.BlockSpec((1, D), lambda i: (0, 0))
    return pl.pallas_call(
        body, grid=(n // tb,), name="resid_norm_bwd",
        in_specs=[row, one, row, one, one, one, row, row], out_specs=[row, row] + [one] * 4,
        out_shape=[jax.ShapeDtypeStruct((n, D), F32), jax.ShapeDtypeStruct((n, D), BF)] + [jax.ShapeDtypeStruct((1, D), F32)] * 4,
    )(x1, gate, y, nw, sh, sc, dh, dx1_direct)


HALO = 8


def _halo_specs(tb, width, nrows, col=0):
    r8 = tb // HALO
    cur = pl.BlockSpec((tb, width), lambda i: (i, col))
    prev = pl.BlockSpec((HALO, width), lambda i: (jnp.maximum(i * r8 - 1, 0), col))
    nxt = pl.BlockSpec((HALO, width), lambda i: (jnp.minimum((i + 1) * r8, nrows // HALO - 1), col))
    return [cur, prev, nxt]


def _segment_edges(seg_rows, tb):
    bounds = [0]
    for s in seg_rows:
        bounds.append(bounds[-1] + s // tb)
    return bounds[:-1], [b - 1 for b in bounds[1:]]


def _keep_halos(i, starts, ends):
    keep_p = functools.reduce(lambda a, b: a & b, [i != s for s in starts])
    keep_n = functools.reduce(lambda a, b: a & b, [i != e for e in ends])
    return keep_p, keep_n


def _ext_rows(refs, cols, keep):
    cur_ref, prev_ref, next_ref = refs
    p = jnp.where(keep[0], prev_ref[:, cols].astype(F32), 0.0)
    n = jnp.where(keep[1], next_ref[:, cols].astype(F32), 0.0)
    return jnp.concatenate([p, cur_ref[:, cols].astype(F32), n], axis=0)


def _shifted_rows(xe, width, transpose=False):
    r = width // 2
    n = xe.shape[0]
    out = []
    for j in range(width):
        s = ((j - r) if transpose else (r - j)) % n
        out.append(xe if s == 0 else pltpu.roll(xe, s, 0))
    return out


def _conv_rows(shifted, w_ref, cols):
    acc = None
    for j, xs in enumerate(shifted):
        term = xs * w_ref[j:j + 1, cols]
        acc = term if acc is None else acc + term
    return acc


def _tap_grads(dcur, shifted, tb):
    return [jnp.sum(dcur * xs[HALO:HALO + tb], axis=0, keepdims=True) for xs in shifted]


def _softplus(x):
    return jnp.maximum(x, 0.0) + jnp.log(1.0 + jnp.exp(-jnp.abs(x)))


def _gates_fn(ba, alog_row, dt_row):
    col = lax.broadcasted_iota(jnp.int32, ba.shape, 1)
    beta = jax.nn.sigmoid(ba)
    g = -jnp.exp(alog_row) * _softplus(ba + dt_row)
    return jnp.where(col < 16, beta, jnp.where(col < 32, g, 0.0))


def _qkv_post_fn(c, kind):
    y = jax.nn.silu(c)
    if kind == 2:
        return y
    n = y * lax.rsqrt(jnp.sum(y * y, axis=-1, keepdims=True) + EPS)
    return n * (HD ** -0.5) if kind == 0 else n


DN_TAPS = 5
FFN_TAPS = 3


def _dn_pre_fwd(p, w8, alog_row, dt_row, seg_rows):
    T = p.shape[0]
    tb = _tile(T, (256, 128))
    starts, ends = _segment_edges(seg_rows, tb)

    def body(cur_ref, prev_ref, next_ref, ba_ref, w_ref, al_ref, dt_ref, q_ref, k_ref, v_ref, gb_ref):
        keep = _keep_halos(pl.program_id(0), starts, ends)
        outs = (q_ref, k_ref, v_ref)
        for kind in range(3):
            for h in range(NH):
                cols = slice(kind * D + h * HD, kind * D + (h + 1) * HD)
                xe = _ext_rows((cur_ref, prev_ref, next_ref), cols, keep)
                conv = _conv_rows(_shifted_rows(xe, DN_TAPS), w_ref, cols)[HALO:HALO + tb]
                outs[kind][:, h * HD:(h + 1) * HD] = _qkv_post_fn(conv, kind)
        gb_ref[...] = _gates_fn(ba_ref[...], al_ref[...], dt_ref[...])

    row = pl.BlockSpec((tb, D), lambda i: (i, 0))
    one = pl.BlockSpec((1, 128), lambda i: (0, 0))
    return pl.pallas_call(
        body, grid=(T // tb,), name="dn_pre_fwd",
        in_specs=_halo_specs(tb, 3 * D, T) + [pl.BlockSpec((tb, 128), lambda i: (i, O_BA // 128)),
                                              pl.BlockSpec((8, 3 * D), lambda i: (0, 0)), one, one],
        out_specs=[row, row, row, pl.BlockSpec((tb, 128), lambda i: (i, 0))],
        out_shape=[jax.ShapeDtypeStruct((T, D), F32)] * 3 + [jax.ShapeDtypeStruct((T, 128), F32)],
        compiler_params=_cp(),
    )(p, p, p, p, w8, alog_row, dt_row)


def _dn_pre_bwd(p, w8, alog_row, dt_row, dq, dk, dv, dgb, seg_rows):
    T = p.shape[0]
    tb = _tile(T, (256, 128))
    starts, ends = _segment_edges(seg_rows, tb)

    def body(cur_ref, prev_ref, next_ref, ba_ref, w_ref, al_ref, dt_ref,
             dq_c, dq_p, dq_n, dk_c, dk_p, dk_n, dv_c, dv_p, dv_n, dgb_ref, dx_ref, dba_ref, dw_ref, dal_ref, ddt_ref):
        i = pl.program_id(0)
        keep = _keep_halos(i, starts, ends)

        @pl.when(i == 0)
        def _():
            dw_ref[...] = jnp.zeros_like(dw_ref)
            dal_ref[...] = jnp.zeros_like(dal_ref)
            ddt_ref[...] = jnp.zeros_like(ddt_ref)

        douts = ((dq_c, dq_p, dq_n), (dk_c, dk_p, dk_n), (dv_c, dv_p, dv_n))
        for kind in range(3):
            for h in range(NH):
                cols = slice(kind * D + h * HD, kind * D + (h + 1) * HD)
                xe = _ext_rows((cur_ref, prev_ref, next_ref), cols, keep)
                shifted = _shifted_rows(xe, DN_TAPS)
                conv = _conv_rows(shifted, w_ref, cols)
                dye = _ext_rows(douts[kind], slice(h * HD, (h + 1) * HD), keep)
                _, vjp = jax.vjp(functools.partial(_qkv_post_fn, kind=kind), conv)
                dce = vjp(dye)[0]
                dx_ref[:, cols] = _conv_rows(_shifted_rows(dce, DN_TAPS, transpose=True), w_ref, cols)[HALO:HALO + tb].astype(BF)
                for j, g in enumerate(_tap_grads(dce[HALO:HALO + tb], shifted, tb)):
                    dw_ref[j:j + 1, cols] += g
        _, vjp = jax.vjp(_gates_fn, ba_ref[...], al_ref[...], dt_ref[...])
        dba, dal, ddt = vjp(dgb_ref[...])
        dba_ref[...] = dba.astype(BF)
        dal_ref[...] += dal
        ddt_ref[...] += ddt

    one = pl.BlockSpec((1, 128), lambda i: (0, 0))
    nar = pl.BlockSpec((tb, 128), lambda i: (i, 0))
    wspec = pl.BlockSpec((8, 3 * D), lambda i: (0, 0))
    return pl.pallas_call(
        body, grid=(T // tb,), name="dn_pre_bwd",
        in_specs=_halo_specs(tb, 3 * D, T) + [pl.BlockSpec((tb, 128), lambda i: (i, O_BA // 128)), wspec, one, one]
        + _halo_specs(tb, D, T) * 3 + [nar],
        out_specs=[pl.BlockSpec((tb, 3 * D), lambda i: (i, 0)), nar, wspec, one, one],
        out_shape=[jax.ShapeDtypeStruct((T, 3 * D), BF), jax.ShapeDtypeStruct((T, 128), BF), jax.ShapeDtypeStruct((8, 3 * D), F32),
                   jax.ShapeDtypeStruct((1, 128), F32), jax.ShapeDtypeStruct((1, 128), F32)],
        compiler_params=_cp(),
    )(p, p, p, p, w8, alog_row, dt_row, dq, dq, dq, dk, dk, dk, dv, dv, dv, dgb)


def _dot_hi(a, b):
    return jnp.dot(a, b, precision=HI, preferred_element_type=F32)


def _dot_bf(a, b):
    return jnp.dot(a.astype(BF), b.astype(BF), preferred_element_type=F32)


def _dot_nt_bf(a, b):
    return lax.dot_general(a.astype(BF), b.astype(BF), (_DIMS["nt"], ((), ())), preferred_element_type=F32)


def _dot_tn_bf(a, b):
    return lax.dot_general(a.astype(BF), b.astype(BF), (_DIMS["tn"], ((), ())), preferred_element_type=F32)


def _dot_h3(a, b):
    return jnp.dot(a, b, precision=lax.Precision.HIGH, preferred_element_type=F32)


def _dot_split(fine, coarse, form):
    hi = fine.astype(BF)
    lo = (fine - hi.astype(F32)).astype(BF)
    cb = coarse.astype(BF)
    if form == "tn":
        return lax.dot_general(jnp.concatenate([cb, cb], axis=0), jnp.concatenate([hi, lo], axis=0),
                               (_DIMS["tn"], ((), ())), preferred_element_type=F32)
    parts = jnp.concatenate([hi, lo], axis=1)
    if form == "nt":
        return lax.dot_general(parts, jnp.concatenate([cb, cb], axis=1), (_DIMS["nt"], ((), ())), preferred_element_type=F32)
    return jnp.dot(parts, jnp.concatenate([cb, cb], axis=0), preferred_element_type=F32)


@jax.custom_vjp
def _mm_split(a, b):
    return _dot_split(a, b, "nn")


_mm_split.defvjp(lambda a, b: (_dot_split(a, b, "nn"), (a, b)),
                 lambda res, dc: (_dot_split(dc, res[1], "nt"), _dot_split(dc, res[0], "tn")))


def _unit_tri_inverses(mats):
    r, c = _iota2((CB, CB))
    eye = (r == c).astype(F32)
    a8 = [jnp.where((r // 8) == (c // 8), a, 0.0) for a in mats]
    a2 = [_dot_split(x, x, "nn") for x in a8]
    a4 = [_dot_split(x, x, "nn") for x in a2]
    t = [_dot_split(eye - x, eye + y, "nn") for x, y in zip(a8, a2)]
    t = [_dot_split(x, eye + y, "nn") for x, y in zip(t, a4)]
    b = 8
    while b < CB:
        mask = ((r // (2 * b)) == (c // (2 * b))) & ((r // b) != (c // b))
        te = [_dot_split(x, jnp.where(mask, a, 0.0), "nn") for x, a in zip(t, mats)]
        t = [x - _dot_split(y, x, "nn") for x, y in zip(t, te)]
        b *= 2
    return t


@jax.custom_vjp
def _saved_inverse(a, t):
    return t


_saved_inverse.defvjp(lambda a, t: (t, t),
                      lambda t, dt: (-_dot_split(_dot_split(dt, t, "nt"), t, "tn"), jnp.zeros_like(t)))


def _dn1_decay(gc, reverse):
    r, c = _iota2((CB, CB))
    incl = (c >= r) if reverse else (c <= r)
    return jnp.where(incl, jnp.exp(jnp.where(incl, gc - gc.T, 0.0)), 0.0)


def _dn1_heads(qs, ks, vs, betas, gcs, ts_saved, reverse, kks=None, qks=None):
    r, c = _iota2((CB, CB))
    strict = (c > r) if reverse else (c < r)
    decays = [_dn1_decay(gc, reverse) for gc in gcs]
    kks = kks or [_dot_nt_bf(k, k) for k in ks]
    systems = [jnp.where(strict, b * kk * dc, 0.0) for b, kk, dc in zip(betas, kks, decays)]
    if ts_saved is None:
        ts = _unit_tri_inverses(systems)
    else:
        ts = [_saved_inverse(a, t) for a, t in zip(systems, ts_saved)]
    egs = [jnp.exp(gc) for gc in gcs]
    us = [_mm_split(t, v * b) for t, v, b in zip(ts, vs, betas)]
    ws = [_mm_split(t, k * (b * eg)) for t, k, b, eg in zip(ts, ks, betas, egs)]
    qks = qks or [_dot_nt_bf(q, k) for q, k in zip(qs, ks)]
    last = 0 if reverse else CB - 1
    glogs = [jnp.sum(jnp.where(r == last, gc, 0.0), axis=0, keepdims=True) for gc in gcs]
    outs = [(u, w, q * eg, k * jnp.exp(gl - gc), qk * dc, jnp.exp(gl))
            for u, w, q, k, eg, gl, gc, qk, dc in zip(us, ws, qs, ks, egs, glogs, gcs, qks, decays)]
    return outs, ts


def _cum_matrix(upper):
    r, c = _iota2((CB, CB))
    return ((c >= r) if upper else (c <= r)).astype(F32)


def _lane_bcast(x, col):
    return jnp.broadcast_to(x[:, col:col + 1], x.shape)


_HEAD_SLICES = [slice(h * HD, (h + 1) * HD) for h in range(NH)]


def _dn1_fwd(q, k, v, gb):
    T = q.shape[0]
    nb = T // CB

    def body(q_ref, k_ref, v_ref, gb_ref, *out_refs):
        gbv = gb_ref[...]
        qs = [q_ref[:, sl] for sl in _HEAD_SLICES]
        ks = [k_ref[:, sl] for sl in _HEAD_SLICES]
        vs = [v_ref[:, sl] for sl in _HEAD_SLICES]
        kks = [_dot_nt_bf(x, x) for x in ks]
        qks = [_dot_nt_bf(x, y) for x, y in zip(qs, ks)]
        for d in (0, 1):
            u_ref, w_ref, qg_ref, kd_ref, qkd_ref, gl_ref, t_ref = out_refs[7 * d:7 * d + 7]
            gcum = _dot_h3(_cum_matrix(d == 1), gbv)
            betas = [_lane_bcast(gbv, d * NH + h) for h in range(NH)]
            gcs = [_lane_bcast(gcum, 16 + d * NH + h) for h in range(NH)]
            outs, ts = _dn1_heads(qs, ks, vs, betas, gcs, None, d == 1, kks, qks)
            for h, sl in enumerate(_HEAD_SLICES):
                u, w, qg, kd, qkd, gl = outs[h]
                u_ref[:, sl] = u
                w_ref[:, sl] = w.astype(BF)
                qg_ref[:, sl] = qg.astype(BF)
                kd_ref[:, sl] = kd.astype(BF)
                qkd_ref[:, sl] = qkd.astype(BF)
                gl_ref[h] = gl
                t_ref[:, sl] = ts[h]

    tb = pl.BlockSpec((CB, D), lambda i: (i, 0))
    one_dir_specs = [tb, tb, tb, tb, tb, pl.BlockSpec((NH, 1, 128), lambda i: (i, 0, 0)), tb]
    one_dir_shapes = ([jax.ShapeDtypeStruct((T, D), F32)] + [jax.ShapeDtypeStruct((T, D), BF)] * 4
                      + [jax.ShapeDtypeStruct((nb * NH, 1, 128), F32), jax.ShapeDtypeStruct((T, D), F32)])
    outs = pl.pallas_call(
        body, grid=(nb,), name="dn1_fwd",
        in_specs=[tb, tb, tb, pl.BlockSpec((CB, 128), lambda i: (i, 0))],
        out_specs=one_dir_specs * 2, out_shape=one_dir_shapes * 2, compiler_params=_cp(),
    )(q, k, v, gb)
    return [tuple(outs[:7]), tuple(outs[7:])]


def _dn1_bwd(q, k, v, gb, tinvs, cots):
    T = q.shape[0]
    nb = T // CB

    def body(q_ref, k_ref, v_ref, gb_ref, *refs):
        dir_refs, (dq_ref, dk_ref, dv_ref, dgb_ref) = refs[:14], refs[14:]
        gbv = gb_ref[...]
        qs = [q_ref[:, sl] for sl in _HEAD_SLICES]
        ks = [k_ref[:, sl] for sl in _HEAD_SLICES]
        vs = [v_ref[:, sl] for sl in _HEAD_SLICES]
        lane = lax.broadcasted_iota(jnp.int32, (CB, 128), 1)
        dgb = jnp.zeros((CB, 128), F32)
        for d in (0, 1):
            t_ref, du_ref, dw_ref, dqg_ref, dkd_ref, dqkd_ref, dgl_ref = dir_refs[7 * d:7 * d + 7]
            gcum = _dot_h3(_cum_matrix(d == 1), gbv)
            betas = [_lane_bcast(gbv, d * NH + h) for h in range(NH)]
            gcs = [_lane_bcast(gcum, 16 + d * NH + h) for h in range(NH)]
            ts = [t_ref[:, sl] for sl in _HEAD_SLICES]
            f = lambda qs, ks, vs, betas, gcs: _dn1_heads(qs, ks, vs, betas, gcs, ts, d == 1)[0]
            _, vjp = jax.vjp(f, qs, ks, vs, betas, gcs)
            cot = [(du_ref[:, sl], dw_ref[:, sl], dqg_ref[:, sl], dkd_ref[:, sl], dqkd_ref[:, sl], dgl_ref[h])
                   for h, sl in enumerate(_HEAD_SLICES)]
            dqs, dks, dvs, dbetas, dgcs = vjp(cot)
            dgcum = jnp.zeros((CB, 128), F32)
            for h, sl in enumerate(_HEAD_SLICES):
                if d == 0:
                    dq_ref[:, sl] = dqs[h]
                    dk_ref[:, sl] = dks[h]
                    dv_ref[:, sl] = dvs[h]
                else:
                    dq_ref[:, sl] += dqs[h]
                    dk_ref[:, sl] += dks[h]
                    dv_ref[:, sl] += dvs[h]
                dgb = dgb + jnp.where(lane == d * NH + h, jnp.sum(dbetas[h], axis=1, keepdims=True), 0.0)
                dgcum = dgcum + jnp.where(lane == 16 + d * NH + h, jnp.sum(dgcs[h], axis=1, keepdims=True), 0.0)
            dgb = dgb + _dot_h3(_cum_matrix(d == 0), dgcum)
        dgb_ref[...] = dgb

    tb = pl.BlockSpec((CB, D), lambda i: (i, 0))
    gbs = pl.BlockSpec((CB, 128), lambda i: (i, 0))
    gls = pl.BlockSpec((NH, 1, 128), lambda i: (i, 0, 0))
    args = []
    for d in (0, 1):
        args += [tinvs[d], *cots[d]]
    return pl.pallas_call(
        body, grid=(nb,), name="dn1_bwd",
        in_specs=[tb, tb, tb, gbs] + [tb, tb, tb, tb, tb, tb, gls] * 2, out_specs=[tb, tb, tb, gbs],
        out_shape=[jax.ShapeDtypeStruct((T, D), F32)] * 3 + [jax.ShapeDtypeStruct((T, 128), F32)],
        compiler_params=_cp(),
    )(q, k, v, gb, *args)


def _dn2_steps(chains):
    ws = [_dot_bf(w, s) for _, w, _, _, _, _, s in chains]
    v_new = [c[0] - x for c, x in zip(chains, ws)]
    o_state = [_dot_bf(c[2], c[6]) for c in chains]
    o_local = [_dot_bf(c[4], vn) for c, vn in zip(chains, v_new)]
    grow = [_dot_tn_bf(c[3], vn) for c, vn in zip(chains, v_new)]
    return [a + b for a, b in zip(o_state, o_local)], [c[6] * c[5] + g for c, g in zip(chains, grow)]


def _scan_order(direction, nlat_b, nall_b):
    if direction == 0:
        return lambda i: (i + nlat_b) % nall_b
    return lambda i: nall_b - 1 - i


def _dn2_fwd(per_dir, nlat):
    T = per_dir[0][0].shape[0]
    nb = T // CB
    blks = [_scan_order(d, nlat // CB, nb) for d in (0, 1)]

    def body(*refs):
        ins, outs, s_scr = refs[:12], refs[12:16], refs[16]

        @pl.when(pl.program_id(0) == 0)
        def _():
            s_scr[...] = jnp.zeros_like(s_scr)
        for d in (0, 1):
            outs[2 * d + 1][0] = s_scr[d]
        where = [(d, h, sl) for h, sl in enumerate(_HEAD_SLICES) for d in (0, 1)]
        chains = []
        for d, h, sl in where:
            u_ref, w_ref, qg_ref, kd_ref, qkd_ref, gl_ref = ins[6 * d:6 * d + 6]
            chains.append((u_ref[:, sl], w_ref[:, sl], qg_ref[:, sl], kd_ref[:, sl], qkd_ref[:, sl], gl_ref[h], s_scr[d, h]))
        os, states = _dn2_steps(chains)
        for (d, h, sl), o, s_next in zip(where, os, states):
            outs[2 * d][:, sl] = o
            s_scr[d, h] = s_next

    in_specs, out_specs, args = [], [], []
    for d in (0, 1):
        blk = blks[d]
        tb = pl.BlockSpec((CB, D), lambda i, blk=blk: (blk(i), 0))
        in_specs += [tb] * 5 + [pl.BlockSpec((NH, 1, 128), lambda i, blk=blk: (blk(i), 0, 0))]
        out_specs += [tb, pl.BlockSpec((1, NH, HD, HD), lambda i, blk=blk: (blk(i), 0, 0, 0))]
        args += list(per_dir[d])
    outs = pl.pallas_call(
        body, grid=(nb,), name="dn2_fwd", in_specs=in_specs, out_specs=out_specs,
        out_shape=[jax.ShapeDtypeStruct((T, D), F32), jax.ShapeDtypeStruct((nb, NH, HD, HD), F32)] * 2,
        scratch_shapes=[pltpu.VMEM((2, NH, HD, HD), F32)], compiler_params=_cp(),
    )(*args)
    return [tuple(outs[:2]), tuple(outs[2:])]


def _dn2_bwd(per_dir, do, nlat):
    T = per_dir[0][0].shape[0]
    nb = T // CB
    nlat_b = nlat // CB
    fwd = [_scan_order(d, nlat_b, nb) for d in (0, 1)]
    blks = [lambda i, f=f: f(nb - 1 - i) for f in fwd]

    def body(*refs):
        ins, outs, ds_scr = refs[:16], refs[16:28], refs[28]
        i = pl.program_id(0)

        @pl.when(i == 0)
        def _():
            ds_scr[...] = jnp.zeros_like(ds_scr)
        where = [(d, h, sl) for h, sl in enumerate(_HEAD_SLICES) for d in (0, 1)]
        chains, cot_o, cot_s = [], [], []
        for d, h, sl in where:
            u_ref, w_ref, qg_ref, kd_ref, qkd_ref, gl_ref, sall_ref, do_ref = ins[8 * d:8 * d + 8]
            chains.append((u_ref[:, sl], w_ref[:, sl].astype(F32), qg_ref[:, sl].astype(F32), kd_ref[:, sl].astype(F32),
                           qkd_ref[:, sl].astype(F32), gl_ref[h], sall_ref[0, h]))
            cot_o.append(jnp.where(blks[d](i) < nlat_b, do_ref[:, sl], 0.0))
            cot_s.append(ds_scr[d, h])
        _, vjp = jax.vjp(_dn2_steps, chains)
        for (d, h, sl), (du, dw, dqg, dkd, dqkd, dgl, ds) in zip(where, vjp((cot_o, cot_s))[0]):
            du_ref, dw_ref, dqg_ref, dkd_ref, dqkd_ref, dgl_ref = outs[6 * d:6 * d + 6]
            du_ref[:, sl] = du
            dw_ref[:, sl] = dw
            dqg_ref[:, sl] = dqg
            dkd_ref[:, sl] = dkd
            dqkd_ref[:, sl] = dqkd
            dgl_ref[h] = dgl
            ds_scr[d, h] = ds

    in_specs, out_specs, args = [], [], []
    for d in (0, 1):
        blk = blks[d]
        tb = pl.BlockSpec((CB, D), lambda i, blk=blk: (blk(i), 0))
        gls = pl.BlockSpec((NH, 1, 128), lambda i, blk=blk: (blk(i), 0, 0))
        in_specs += [tb] * 5 + [gls, pl.BlockSpec((1, NH, HD, HD), lambda i, blk=blk: (blk(i), 0, 0, 0)),
                                pl.BlockSpec((CB, D), lambda i, blk=blk: (jnp.minimum(blk(i), nlat_b - 1), 0))]
        out_specs += [tb] * 5 + [gls]
        args += list(per_dir[d]) + [do]
    outs = pl.pallas_call(
        body, grid=(nb,), name="dn2_bwd", in_specs=in_specs, out_specs=out_specs,
        out_shape=([jax.ShapeDtypeStruct((T, D), F32)] * 5 + [jax.ShapeDtypeStruct((nb * NH, 1, 128), F32)]) * 2,
        scratch_shapes=[pltpu.VMEM((2, NH, HD, HD), F32)], compiler_params=_cp(),
    )(*args)
    return [tuple(outs[:6]), tuple(outs[6:])]


def _ghn_fn(o, gt, w):
    y = o * lax.rsqrt(jnp.mean(o * o, axis=-1, keepdims=True) + EPS)
    return (y * w) * jax.nn.silu(gt)


def _ghn_fwd(o_f, o_b, p, w, nlat):
    tb = _tile(nlat, (256, 128))

    def body(of_ref, ob_ref, gt_ref, w_ref, y_ref):
        for h in range(NH):
            sl = slice(h * HD, (h + 1) * HD)
            y_ref[:, sl] = _ghn_fn(of_ref[:, sl] + ob_ref[:, sl], gt_ref[:, sl], w_ref[...]).astype(BF)

    row = pl.BlockSpec((tb, D), lambda i: (i, 0))
    return pl.pallas_call(
        body, grid=(nlat // tb,), name="ghn_fwd",
        in_specs=[row, row, pl.BlockSpec((tb, D), lambda i: (i, O_GT // D)), pl.BlockSpec((1, HD), lambda i: (0, 0))],
        out_specs=row, out_shape=jax.ShapeDtypeStruct((nlat, D), BF),
    )(o_f, o_b, p, w)


def _ghn_bwd(o_f, o_b, p, w, dy, nlat):
    T = p.shape[0]
    tb = _tile(nlat, (256, 128))
    nlb = nlat // tb

    def body(of_ref, ob_ref, gt_ref, w_ref, dy_ref, do_ref, dgt_ref, dw_ref):
        is_lat = pl.program_id(0) < nlb

        @pl.when(pl.program_id(0) == 0)
        def _():
            dw_ref[...] = jnp.zeros_like(dw_ref)
        for h in range(NH):
            sl = slice(h * HD, (h + 1) * HD)
            _, vjp = jax.vjp(_ghn_fn, of_ref[:, sl] + ob_ref[:, sl], gt_ref[:, sl], w_ref[...])
            do, dgt, dw = vjp(dy_ref[:, sl])
            do_ref[:, sl] = do
            dgt_ref[:, sl] = jnp.where(is_lat, dgt, 0.0).astype(BF)
            dw_ref[...] += jnp.where(is_lat, dw, 0.0)

    lat = lambda i: jnp.minimum(i, nlb - 1)
    row = pl.BlockSpec((tb, D), lambda i: (lat(i), 0))
    one = pl.BlockSpec((1, HD), lambda i: (0, 0))
    return pl.pallas_call(
        body, grid=(T // tb,), name="ghn_bwd",
        in_specs=[row, row, pl.BlockSpec((tb, D), lambda i: (lat(i), O_GT // D)), one, row],
        out_specs=[row, pl.BlockSpec((tb, D), lambda i: (i, 0)), one],
        out_shape=[jax.ShapeDtypeStruct((nlat, D), F32), jax.ShapeDtypeStruct((T, D), BF), jax.ShapeDtypeStruct((1, HD), F32)],
    )(o_f, o_b, p, w, dy)


@jax.custom_vjp
def _swap32(x):
    lane = lax.broadcasted_iota(jnp.int32, x.shape, 1)
    return jnp.where((lane & 32) == 0, pltpu.roll(x, 96, 1), pltpu.roll(x, 32, 1))


_swap32.defvjp(lambda x: (_swap32(x), None), lambda _, g: (_swap32(g),))


def _qk_post_fn(x, w, cos, sin):
    y = (x * lax.rsqrt(jnp.mean(x * x, axis=-1, keepdims=True) + EPS)) * w
    return y * cos + _swap32(y) * sin


def _attn_prep_fwd(p, qn, kn, cos, sin):
    T = p.shape[0]
    tb = _tile(T, (256, 128))

    def body(q_ref, k_ref, v_ref, qn_ref, kn_ref, cos_ref, sin_ref, qr_ref, kr_ref, vb_ref):
        cos_v, sin_v = cos_ref[...], sin_ref[...]
        for h in range(NH):
            sl = slice(h * HD, (h + 1) * HD)
            qr_ref[:, sl] = _qk_post_fn(q_ref[:, sl], qn_ref[...], cos_v, sin_v).astype(BF)
        for h in range(KVH):
            sl = slice(h * HD, (h + 1) * HD)
            kr_ref[:, sl] = _qk_post_fn(k_ref[:, sl], kn_ref[...], cos_v, sin_v).astype(BF)
        vb_ref[...] = v_ref[...].astype(BF)

    one = pl.BlockSpec((1, HD), lambda i: (0, 0))
    tab = pl.BlockSpec((tb, HD), lambda i: (i, 0))
    return pl.pallas_call(
        body, grid=(T // tb,), name="attn_prep_fwd",
        in_specs=[pl.BlockSpec((tb, D), lambda i: (i, O_Q // D)), pl.BlockSpec((tb, KV), lambda i: (i, O_K // KV)),
                  pl.BlockSpec((tb, KV), lambda i: (i, O_V // KV)), one, one, tab, tab],
        out_specs=[pl.BlockSpec((tb, D), lambda i: (i, 0)), pl.BlockSpec((tb, KV), lambda i: (i, 0)),
                   pl.BlockSpec((tb, KV), lambda i: (i, 0))],
        out_shape=[jax.ShapeDtypeStruct((T, D), BF), jax.ShapeDtypeStruct((T, KV), BF), jax.ShapeDtypeStruct((T, KV), BF)],
    )(p, p, p, qn, kn, cos, sin)


def _attn_prep_bwd(p, qn, kn, cos, sin, dqr, dkp, dvp, dkc, dvc, nlat):
    T = p.shape[0]
    nqb = nlat // CB
    ncb = (T - nlat) // CB

    def body(q_ref, k_ref, v_ref, qn_ref, kn_ref, cos_ref, sin_ref, dqr_ref, dka_ref, dkb_ref, dkc3_ref, dva_ref, dvb_ref, dvc3_ref,
             dkctx_ref, dvctx_ref, dq_ref, dk_ref, dv_ref, dqn_ref, dkn_ref):
        i = pl.program_id(0)
        is_lat = i < nqb
        cos_v, sin_v = cos_ref[...], sin_ref[...]

        @pl.when(i == 0)
        def _():
            dqn_ref[...] = jnp.zeros_like(dqn_ref)
            dkn_ref[...] = jnp.zeros_like(dkn_ref)

        def band_sum(a_ref, b_ref, c_ref, ctx_ref):
            s = b_ref[0] + jnp.where(i > 0, a_ref[0], 0.0) + jnp.where(i < nqb - 1, c_ref[0], 0.0)
            return jnp.where(is_lat, s, ctx_ref[...])

        dkr = band_sum(dka_ref, dkb_ref, dkc3_ref, dkctx_ref)
        dv_ref[...] = band_sum(dva_ref, dvb_ref, dvc3_ref, dvctx_ref).astype(BF)
        for h in range(NH):
            sl = slice(h * HD, (h + 1) * HD)
            _, vjp = jax.vjp(_qk_post_fn, q_ref[:, sl], qn_ref[...], cos_v, sin_v)
            dq, dqn, _, _ = vjp(jnp.where(is_lat, dqr_ref[:, sl], 0.0))
            dq_ref[:, sl] = dq.astype(BF)
            dqn_ref[...] += dqn
        for h in range(KVH):
            sl = slice(h * HD, (h + 1) * HD)
            _, vjp = jax.vjp(_qk_post_fn, k_ref[:, sl], kn_ref[...], cos_v, sin_v)
            dk, dkn, _, _ = vjp(dkr[:, sl])
            dk_ref[:, sl] = dk.astype(BF)
            dkn_ref[...] += dkn

    one = pl.BlockSpec((1, HD), lambda i: (0, 0))
    tab = pl.BlockSpec((CB, HD), lambda i: (i, 0))
    lat = lambda i: jnp.minimum(i, nqb - 1)

    def part(off, slot):
        return pl.BlockSpec((1, CB, KV), lambda i: (jnp.clip(lat(i) + off, 0, nqb - 1) * 3 + slot, 0, 0))

    ctxs = pl.BlockSpec((CB, KV), lambda i: (jnp.clip(i - nqb, 0, ncb - 1), 0))
    kvs = pl.BlockSpec((CB, KV), lambda i: (i, 0))
    return pl.pallas_call(
        body, grid=(T // CB,), name="attn_prep_bwd",
        in_specs=[pl.BlockSpec((CB, D), lambda i: (i, O_Q // D)), pl.BlockSpec((CB, KV), lambda i: (i, O_K // KV)),
                  pl.BlockSpec((CB, KV), lambda i: (i, O_V // KV)), one, one, tab, tab,
                  pl.BlockSpec((CB, D), lambda i: (lat(i), 0)),
                  part(-1, 2), part(0, 1), part(1, 0), part(-1, 2), part(0, 1), part(1, 0), ctxs, ctxs],
        out_specs=[pl.BlockSpec((CB, D), lambda i: (i, 0)), kvs, kvs, one, one],
        out_shape=[jax.ShapeDtypeStruct((T, D), BF), jax.ShapeDtypeStruct((T, KV), BF), jax.ShapeDtypeStruct((T, KV), BF),
                   jax.ShapeDtypeStruct((1, HD), F32), jax.ShapeDtypeStruct((1, HD), F32)],
    )(p, p, p, qn, kn, cos, sin, dqr, dkp, dkp, dkp, dvp, dvp, dvp, dkc, dvc)


def _attn_groups_fn(qs, kalls, valls, sinks, bias):
    groups = range(KVH)
    q = [jnp.concatenate(qs[GRP * g:GRP * (g + 1)], axis=0) for g in groups]
    s = [_dot_nt_bf(q[g], kalls[g]) * (HD ** -0.5) + bias for g in groups]
    sk = [jnp.concatenate([jnp.broadcast_to(jnp.mean(t, axis=1, keepdims=True), (CB, 1)) for t in sinks[GRP * g:GRP * (g + 1)]],
                          axis=0) for g in groups]
    m = [lax.stop_gradient(jnp.maximum(jnp.max(s[g], axis=1, keepdims=True), sk[g])) for g in groups]
    e = [jnp.exp(s[g] - m[g]) for g in groups]
    den = [jnp.sum(e[g], axis=1, keepdims=True) + jnp.exp(sk[g] - m[g]) for g in groups]
    return [_dot_bf(e[g] / den[g], valls[g]) for g in groups]


def _attn_bias(lc):
    r, c = _iota2((GRP * CB, 3 * CB + lc))
    rel = c - (r & (CB - 1))
    win = (rel >= 0) & (rel <= 2 * CB)
    ctx = c >= 3 * CB
    seen = [(win & (c >= CB)) | ctx, win | ctx, (win & (c < 2 * CB)) | ctx]
    return jnp.stack([jnp.where(s, 0.0, -1e30) for s in seen]).astype(F32)


def _attn_specs(nqb, lc, nlat):
    assert nqb >= 2
    qs = pl.BlockSpec((CB, D), lambda i: (i, 0))
    ka = pl.BlockSpec((CB, KV), lambda i: (jnp.maximum(i - 1, 0), 0))
    kb = pl.BlockSpec((CB, KV), lambda i: (i, 0))
    kc = pl.BlockSpec((CB, KV), lambda i: (jnp.minimum(i + 1, nqb - 1), 0))
    kx = pl.BlockSpec((lc, KV), lambda i: (nlat // lc, 0))
    sk = pl.BlockSpec((KVH, 8, 128), lambda i: (0, 0, 0))
    bs = pl.BlockSpec((1, GRP * CB, 3 * CB + lc), lambda i: (jnp.where(i == 0, 0, jnp.where(i == nqb - 1, 2, 1)), 0, 0))
    return qs, ka, kb, kc, kx, sk, bs


def _attn_operands(q_ref, k_refs, v_refs, sk_ref, dtype):
    sls = [slice(g * HD, (g + 1) * HD) for g in range(KVH)]
    kalls = [jnp.concatenate([r[:, sl] for r in k_refs], axis=0).astype(dtype) for sl in sls]
    valls = [jnp.concatenate([r[:, sl] for r in v_refs], axis=0).astype(dtype) for sl in sls]
    qs = [q_ref[:, sl].astype(dtype) for sl in _HEAD_SLICES]
    sinks = [sk_ref[h // GRP, (h % GRP):(h % GRP) + 1, :] for h in range(NH)]
    return qs, kalls, valls, sinks


def _attn_fwd(qr, kr, vb, sink, nlat):
    lc = kr.shape[0] - nlat
    nqb = nlat // CB
    qs, ka, kb, kc, kx, sk, bs = _attn_specs(nqb, lc, nlat)

    def body(q_ref, ka_ref, kb_ref, kc_ref, kx_ref, va_ref, vb_ref, vc_ref, vx_ref, sk_ref, bias_ref, o_ref):
        operands = _attn_operands(q_ref, (ka_ref, kb_ref, kc_ref, kx_ref), (va_ref, vb_ref, vc_ref, vx_ref), sk_ref, BF)
        outs = _attn_groups_fn(*operands, bias_ref[0])
        for h, sl in enumerate(_HEAD_SLICES):
            o_ref[:, sl] = outs[h // GRP][(h % GRP) * CB:(h % GRP + 1) * CB].astype(BF)

    return pl.pallas_call(
        body, grid=(nqb,), name="attn_fwd",
        in_specs=[qs, ka, kb, kc, kx, ka, kb, kc, kx, sk, bs], out_specs=qs,
        out_shape=jax.ShapeDtypeStruct((nlat, D), BF), compiler_params=_cp(),
    )(qr, kr, kr, kr, kr, vb, vb, vb, vb, sink, _attn_bias(lc))


def _attn_bwd(qr, kr, vb, sink, dy, nlat):
    lc = kr.shape[0] - nlat
    nqb = nlat // CB
    qs, ka, kb, kc, kx, sk, bs = _attn_specs(nqb, lc, nlat)

    def body(q_ref, ka_ref, kb_ref, kc_ref, kx_ref, va_ref, vb_ref, vc_ref, vx_ref, sk_ref, dy_ref, bias_ref,
             dq_ref, dkp_ref, dvp_ref, dkx_ref, dvx_ref, dsk_ref):
        operands = _attn_operands(q_ref, (ka_ref, kb_ref, kc_ref, kx_ref), (va_ref, vb_ref, vc_ref, vx_ref), sk_ref, F32)
        _, vjp = jax.vjp(functools.partial(_attn_groups_fn, bias=bias_ref[0]), *operands)
        dys_g = [jnp.concatenate([dy_ref[:, sl] for sl in _HEAD_SLICES[GRP * g:GRP * (g + 1)]], axis=0) for g in range(KVH)]
        dqs, dks, dvs, dsinks = vjp(dys_g)

        @pl.when(pl.program_id(0) == 0)
        def _():
            dkx_ref[...] = jnp.zeros_like(dkx_ref)
            dvx_ref[...] = jnp.zeros_like(dvx_ref)
            dsk_ref[...] = jnp.zeros_like(dsk_ref)

        for h, sl in enumerate(_HEAD_SLICES):
            dq_ref[:, sl] = dqs[h]
            dsk_ref[h // GRP, (h % GRP):(h % GRP) + 1, :] += dsinks[h]
        for g in range(KVH):
            sl = slice(g * HD, (g + 1) * HD)
            for t in range(3):
                dkp_ref[t, :, sl] = dks[g][t * CB:(t + 1) * CB]
                dvp_ref[t, :, sl] = dvs[g][t * CB:(t + 1) * CB]
            dkx_ref[:, sl] += dks[g][3 * CB:]
            dvx_ref[:, sl] += dvs[g][3 * CB:]

    dys = qs
    parts = pl.BlockSpec((3, CB, KV), lambda i: (i, 0, 0))
    ctxo = pl.BlockSpec((lc, KV), lambda i: (0, 0))
    return pl.pallas_call(
        body, grid=(nqb,), name="attn_bwd",
        in_specs=[qs, ka, kb, kc, kx, ka, kb, kc, kx, sk, dys, bs],
        out_specs=[dys, parts, parts, ctxo, ctxo, sk],
        out_shape=[jax.ShapeDtypeStruct((nlat, D), F32), jax.ShapeDtypeStruct((3 * nqb, CB, KV), F32),
                   jax.ShapeDtypeStruct((3 * nqb, CB, KV), F32), jax.ShapeDtypeStruct((lc, KV), F32),
                   jax.ShapeDtypeStruct((lc, KV), F32), jax.ShapeDtypeStruct((KVH, 8, 128), F32)],
        compiler_params=_cp(),
    )(qr, kr, kr, kr, kr, vb, vb, vb, vb, sink, dy, _attn_bias(lc))


def _merge_fn(z_dn, z_at, g_dn, g_at):
    return jax.nn.sigmoid(g_dn) * z_dn + jax.nn.sigmoid(g_at) * z_at


def _merge_fwd(z_dn, z_at, p, nlat):
    tb = _tile(nlat, (256, 128))

    def body(zd_ref, za_ref, gd_ref, ga_ref, o_ref):
        o_ref[...] = _merge_fn(zd_ref[...], za_ref[...], gd_ref[...], ga_ref[...]).astype(BF)

    row = pl.BlockSpec((tb, D), lambda i: (i, 0))
    return pl.pallas_call(
        body, grid=(nlat // tb,), name="merge_fwd",
        in_specs=[row, row, pl.BlockSpec((tb, D), lambda i: (i, O_MG // D)), pl.BlockSpec((tb, D), lambda i: (i, O_MG // D + 1))],
        out_specs=row, out_shape=jax.ShapeDtypeStruct((nlat, D), BF),
    )(z_dn, z_at, p, p)


def _merge_bwd(z_dn, z_at, p, dm, nlat):
    T = p.shape[0]
    tb = _tile(nlat, (256, 128))
    nlb = nlat // tb

    def body(zd_ref, za_ref, gd_ref, ga_ref, dm_ref, dzd_ref, dza_ref, dg_ref):
        is_lat = pl.program_id(0) < nlb
        _, vjp = jax.vjp(_merge_fn, zd_ref[...], za_ref[...], gd_ref[...], ga_ref[...])
        dzd, dza, dgd, dga = vjp(dm_ref[...])
        dzd_ref[...] = dzd.astype(BF)
        dza_ref[...] = dza.astype(BF)
        dg_ref[:, :D] = jnp.where(is_lat, dgd, 0.0).astype(BF)
        dg_ref[:, D:] = jnp.where(is_lat, dga, 0.0).astype(BF)

    lat = lambda i: jnp.minimum(i, nlb - 1)
    row = pl.BlockSpec((tb, D), lambda i: (lat(i), 0))
    return pl.pallas_call(
        body, grid=(T // tb,), name="merge_bwd",
        in_specs=[row, row, pl.BlockSpec((tb, D), lambda i: (lat(i), O_MG // D)),
                  pl.BlockSpec((tb, D), lambda i: (lat(i), O_MG // D + 1)), row],
        out_specs=[row, row, pl.BlockSpec((tb, 2 * D), lambda i: (i, 0))],
        out_shape=[jax.ShapeDtypeStruct((nlat, D), BF), jax.ShapeDtypeStruct((nlat, D), BF), jax.ShapeDtypeStruct((T, 2 * D), BF)],
    )(z_dn, z_at, p, p, dm)


def _swiglu_fn(ug, uv):
    return jax.nn.silu(ug) * uv


FFN_GROUP = 256


def _ffn_mid_fwd(u, w8, bias):
    n = u.shape[0]
    tb = _tile(n, (256, 128))
    starts, ends = _segment_edges((n,), tb)

    def body(cur_ref, prev_ref, next_ref, w_ref, b_ref, o_ref):
        keep = _keep_halos(pl.program_id(0), starts, ends)
        for c0 in range(0, DFF, FFN_GROUP):
            halves = []
            for cols in (slice(c0, c0 + FFN_GROUP), slice(DFF + c0, DFF + c0 + FFN_GROUP)):
                xe = _ext_rows((cur_ref, prev_ref, next_ref), cols, keep)
                halves.append(_conv_rows(_shifted_rows(xe, FFN_TAPS), w_ref, cols)[HALO:HALO + tb] + b_ref[:, cols])
            o_ref[:, c0:c0 + FFN_GROUP] = _swiglu_fn(*halves).astype(BF)

    return pl.pallas_call(
        body, grid=(n // tb,), name="ffn_mid_fwd",
        in_specs=_halo_specs(tb, 2 * DFF, n) + [pl.BlockSpec((8, 2 * DFF), lambda i: (0, 0)), pl.BlockSpec((1, 2 * DFF), lambda i: (0, 0))],
        out_specs=pl.BlockSpec((tb, DFF), lambda i: (i, 0)), out_shape=jax.ShapeDtypeStruct((n, DFF), BF),
        compiler_params=_cp(),
    )(u, u, u, w8, bias)


def _ffn_mid_bwd(u, w8, bias, da):
    n = u.shape[0]
    tb = _tile(n, (256, 128))
    starts, ends = _segment_edges((n,), tb)

    def body(cur_ref, prev_ref, next_ref, w_ref, b_ref, da_c, da_p, da_n, du_ref, dw_ref, db_ref):
        i = pl.program_id(0)
        keep = _keep_halos(i, starts, ends)

        @pl.when(i == 0)
        def _():
            dw_ref[...] = jnp.zeros_like(dw_ref)
            db_ref[...] = jnp.zeros_like(db_ref)

        for c0 in range(0, DFF, FFN_GROUP):
            col_pair = (slice(c0, c0 + FFN_GROUP), slice(DFF + c0, DFF + c0 + FFN_GROUP))
            shifts = [_shifted_rows(_ext_rows((cur_ref, prev_ref, next_ref), cols, keep), FFN_TAPS) for cols in col_pair]
            convs = [_conv_rows(shifted, w_ref, cols) + b_ref[:, cols] for shifted, cols in zip(shifts, col_pair)]
            dae = _ext_rows((da_c, da_p, da_n), col_pair[0], keep)
            _, vjp = jax.vjp(_swiglu_fn, *convs)
            for shifted, cols, dce in zip(shifts, col_pair, vjp(dae)):
                du_ref[:, cols] = _conv_rows(_shifted_rows(dce, FFN_TAPS, transpose=True), w_ref, cols)[HALO:HALO + tb].astype(BF)
                dcur = dce[HALO:HALO + tb]
                for j, g in enumerate(_tap_grads(dcur, shifted, tb)):
                    dw_ref[j:j + 1, cols] += g
                db_ref[:, cols] += jnp.sum(dcur, axis=0, keepdims=True)

    wspec = pl.BlockSpec((8, 2 * DFF), lambda i: (0, 0))
    bspec = pl.BlockSpec((1, 2 * DFF), lambda i: (0, 0))
    return pl.pallas_call(
        body, grid=(n // tb,), name="ffn_mid_bwd",
        in_specs=_halo_specs(tb, 2 * DFF, n) + [wspec, bspec] + _halo_specs(tb, DFF, n),
        out_specs=[pl.BlockSpec((tb, 2 * DFF), lambda i: (i, 0)), wspec, bspec],
        out_shape=[jax.ShapeDtypeStruct((n, 2 * DFF), BF), jax.ShapeDtypeStruct((8, 2 * DFF), F32), jax.ShapeDtypeStruct((1, 2 * DFF), F32)],
        compiler_params=_cp(),
    )(u, u, u, w8, bias, da, da, da)


def _loss_kernel(x1, gate, ff, target):
    n = x1.shape[0]
    tb = _tile(n, (256, 128))

    def body(x_ref, g_ref, f_ref, t_ref, loss_ref, dy_ref, dff_ref, dg_ref):
        err = x_ref[...] + g_ref[...] * f_ref[...] - t_ref[...]
        dy = err * (1.0 / D)
        dy_ref[...] = dy
        dff_ref[...] = (g_ref[...] * dy).astype(BF)

        @pl.when(pl.program_id(0) == 0)
        def _():
            loss_ref[...] = jnp.zeros_like(loss_ref)
            dg_ref[...] = jnp.zeros_like(dg_ref)
        part = 0.5 * jnp.sum(jnp.sum(err * err, axis=1, keepdims=True) * (1.0 / D), axis=0, keepdims=True)
        loss_ref[...] += jnp.broadcast_to(part, (1, 128))
        dg_ref[...] += jnp.sum(dy * f_ref[...], axis=0, keepdims=True)

    row = pl.BlockSpec((tb, D), lambda i: (i, 0))
    one = pl.BlockSpec((1, D), lambda i: (0, 0))
    return pl.pallas_call(
        body, grid=(n // tb,), name="loss",
        in_specs=[row, one, row, row], out_specs=[pl.BlockSpec((1, 128), lambda i: (0, 0)), row, row, one],
        out_shape=[jax.ShapeDtypeStruct((1, 128), F32), jax.ShapeDtypeStruct((n, D), F32),
                   jax.ShapeDtypeStruct((n, D), BF), jax.ShapeDtypeStruct((1, D), F32)],
    )(x1, gate, ff, target)


def _rope_tables(nlat, lc):
    t = jnp.arange(nlat)
    row = (t // GRID_W).astype(F32)
    col = (t % GRID_W).astype(F32)
    inv_freq = ROPE_BASE ** (-jnp.arange(32, dtype=F32) / 32)
    ar, ac = row[:, None] * inv_freq, col[:, None] * inv_freq
    cos = jnp.concatenate([jnp.cos(ar), jnp.cos(ar), jnp.cos(ac), jnp.cos(ac)], axis=1)
    sin = jnp.concatenate([-jnp.sin(ar), jnp.sin(ar), -jnp.sin(ac), jnp.sin(ac)], axis=1)
    cos = jnp.concatenate([cos, jnp.ones((lc, HD), F32)], axis=0)
    sin = jnp.concatenate([sin, jnp.zeros((lc, HD), F32)], axis=0)
    return cos, sin


def _pad_rows8(w):
    return jnp.concatenate([w, jnp.zeros((8 - w.shape[0], w.shape[1]), w.dtype)], axis=0)


def _pack_w_in(w):
    cuts = [sum(IN_SIZES[:i]) for i in range(len(IN_SIZES) + 1)]
    qkv, gt, b, a, q, k, v, mg = [w[:, cuts[i]:cuts[i + 1]] for i in range(len(IN_SIZES))]
    return jnp.concatenate([qkv, gt, q, mg, k, v, b, a, jnp.zeros((w.shape[0], PW - O_BA - 32), w.dtype)], axis=1)


def _unpack_w_in(g):
    return jnp.concatenate([g[:, O_QKV:O_GT], g[:, O_GT:O_Q], g[:, O_BA:O_BA + 32], g[:, O_Q:O_MG], g[:, O_K:O_V],
                            g[:, O_V:O_BA], g[:, O_MG:O_K]], axis=1)


def _local_step(x, ctx, mod_x, mod_c, target, project_in, project_back,
                norm_mix, norm_ffn, dn_conv, a_log, dt_bias, dn_norm, q_norm, k_norm, sink, ffn_conv, ffn_conv_b):
    L, LC = x.shape[0], ctx.shape[0]
    T = L + LC
    seg = lambda r: jnp.stack([mod_x[r], mod_c[r]])[:, None, :]
    sh_a, sc_a = seg(0), seg(1)
    g_a, g_f = mod_x[2][None], mod_x[5][None]
    sh_f, sc_f = mod_x[3][None], mod_x[4][None]
    cos, sin = _rope_tables(L, LC)
    dnc8 = _pad_rows8(dn_conv)
    ffc8 = _pad_rows8(ffn_conv)
    gate_row = lambda a: jnp.concatenate([jnp.zeros((1, 16), F32), a.reshape(1, 16), jnp.zeros((1, 96), F32)], axis=1)
    alog_row, dt_row = gate_row(a_log), gate_row(dt_bias)
    sinkb = jnp.concatenate([jnp.broadcast_to(sink.reshape(KVH, GRP, 1), (KVH, GRP, 128)), jnp.zeros((KVH, 8 - GRP, 128), F32)], axis=1)

    h1 = _norm_mod_fwd(x, ctx, norm_mix, sh_a, sc_a, "norm_mix_fwd")
    p, (w_in_p, w_bdn, w_bat, w_out, w_up, w_down) = project_in(h1)
    q, k, v, gb = _dn_pre_fwd(p, dnc8, alog_row, dt_row, (L, LC))
    wy = _dn1_fwd(q, k, v, gb)
    scans = _dn2_fwd([t[:6] for t in wy], L)
    o_dir = [s[0] for s in scans]
    y_dn = _ghn_fwd(o_dir[0], o_dir[1], p, dn_norm, L)
    qr, kr, vb = _attn_prep_fwd(p, q_norm, k_norm, cos, sin)
    y_at = _attn_fwd(qr, kr, vb, sinkb, L)
    z_dn = _mm(y_dn, w_bdn, form="nn", out_dtype=F32, name="branch_dn")
    z_at = _mm(y_at, w_bat, form="nn", out_dtype=F32, name="branch_at")
    merged = _merge_fwd(z_dn, z_at, p, L)
    mix = _mm(merged, w_out, form="nn", out_dtype=F32, name="out_proj")
    x1, h2 = _resid_norm_fwd(x, g_a, mix, norm_ffn, sh_f, sc_f)
    u_raw = _mm(h2, w_up, form="nn", out_dtype=F32, name="ffn_up")
    act = _ffn_mid_fwd(u_raw, ffc8, ffn_conv_b)
    ff = _mm(act, w_down, form="nn", out_dtype=F32, name="ffn_down")
    loss_row, dy, dff, dg_f = _loss_kernel(x1, g_f, ff, target)

    g_down = _mm(act, dff, form="tn", out_dtype=BF, name="g_ffn_down")
    dact = _mm(dff, w_down, form="nt", out_dtype=F32, name="d_act")
    du_raw, g_ffc8, g_ffb = _ffn_mid_bwd(u_raw, ffc8, ffn_conv_b, dact)
    g_up = _mm(h2, du_raw, form="tn", out_dtype=BF, name="g_ffn_up")
    dh2 = _mm(du_raw, w_up, form="nt", out_dtype=F32, name="d_h2")
    dx1, dmix, dg_a, g_nffn, dsh_f, dsc_f = _resid_norm_bwd(x1, g_a, mix, norm_ffn, sh_f, sc_f, dh2, dy)

    g_out = _mm(merged, dmix, form="tn", out_dtype=BF, name="g_w_out")
    dmerged = _mm(dmix, w_out, form="nt", out_dtype=F32, name="d_merged")
    dz_dn, dz_at, dmg = _merge_bwd(z_dn, z_at, p, dmerged, L)
    g_bdn = _mm(y_dn, dz_dn, form="tn", out_dtype=BF, name="g_branch_dn")
    g_bat = _mm(y_at, dz_at, form="tn", out_dtype=BF, name="g_branch_at")
    dy_dn = _mm(dz_dn, w_bdn, form="nt", out_dtype=F32, name="d_y_dn")
    dy_at = _mm(dz_at, w_bat, form="nt", out_dtype=F32, name="d_y_at")
    dqr, dkp, dvp, dkx, dvx, dsink = _attn_bwd(qr, kr, vb, sinkb, dy_at, L)
    dq_raw, dk_raw, dv_raw, g_qn, g_kn = _attn_prep_bwd(p, q_norm, k_norm, cos, sin, dqr, dkp, dvp, dkx, dvx, L)
    do, dgt, g_dnn = _ghn_bwd(o_dir[0], o_dir[1], p, dn_norm, dy_dn, L)
    cots = _dn2_bwd([wy[d][:6] + (scans[d][1],) for d in (0, 1)], do, L)
    dq, dk, dv, dgb = _dn1_bwd(q, k, v, gb, [t[6] for t in wy], cots)
    dqkv_raw, dba, g_dnc8, g_alog, g_dt = _dn_pre_bwd(p, dnc8, alog_row, dt_row, dq, dk, dv, dgb, (L, LC))
    dp = jnp.concatenate([dqkv_raw, dgt, dq_raw, dmg, dk_raw, dv_raw, dba, jnp.zeros((T, PW - O_BA - 128), BF)], axis=1)
    big, dh1 = project_back(h1, dp, w_in_p, (g_bdn, g_bat, g_out, g_up, g_down))
    grad_x, g_nmix_x, dsh_a, dsc_a = _norm_mod_bwd(x, norm_mix, mod_x[0][None], mod_x[1][None], dh1, row0=0,
                                                   name="norm_mix_bwd", residual=dx1)
    g_nmix_c, dsh_c, dsc_c = _norm_mod_bwd(ctx, norm_mix, mod_c[0][None], mod_c[1][None], dh1, row0=L, name="norm_mix_bwd_ctx")
    g_nmix = g_nmix_x + g_nmix_c

    zero = jnp.zeros((D,), F32)
    dmod_x = jnp.stack([dsh_a[0], dsc_a[0], dg_a[0], dsh_f[0], dsc_f[0], dg_f[0]])
    dmod_c = jnp.stack([dsh_c[0], dsc_c[0], zero, zero, zero, zero])
    small = dict(
        dmod_x=dmod_x, dmod_c=dmod_c, norm_mix=g_nmix, norm_ffn=g_nffn, dn_conv=g_dnc8[:5], dn_a_log=g_alog[0, 16:32].reshape(2, 8),
        dn_dt_bias=g_dt[0, 16:32].reshape(2, 8), dn_norm=g_dnn, q_norm=g_qn, k_norm=g_kn,
        attn_sink=jnp.sum(dsink[:, :GRP, :], axis=2).reshape(1, NH), ffn_conv=g_ffc8[:3], ffn_conv_b=g_ffb)
    return loss_row[0, 0], grad_x, big, small


def _exchange(arrays, scatter, name):
    n = len(arrays)

    def body(*refs):
        args = (refs[:n], refs[n:2 * n], *refs[2 * n:], scatter)
        _exchange_start(*args)
        _exchange_wait(*args)

    hbm = pl.BlockSpec(memory_space=pl.ANY)
    out_shape, sems = _exchange_shapes(arrays, scatter)
    return pl.pallas_call(body, name=name, in_specs=[hbm] * n, out_specs=[hbm] * n, out_shape=out_shape,
                          scratch_shapes=sems)(*arrays)


def _ada_fwd(c16, w_ada, b_ada):
    def body(c_ref, w_ref, b_ref, o_ref):
        o_ref[...] = _dot_hi(jax.nn.silu(c_ref[...]), w_ref[...]) + b_ref[...]

    return pl.pallas_call(body, name="ada_fwd", out_shape=jax.ShapeDtypeStruct((16, w_ada.shape[1]), F32))(c16, w_ada, b_ada)


def _ada_bwd(c16, w_ada, dmx, dmc):
    def body(c_ref, w_ref, dmx_ref, dmc_ref, gw_ref, pc_ref):
        dmc_tot = dmc_ref[0:1, :]
        for d in range(1, N_DEV):
            dmc_tot = dmc_tot + dmc_ref[d:d + 1, :]
        dm16 = jnp.concatenate([dmx_ref[...], jnp.broadcast_to(dmc_tot, (8, dmc_tot.shape[1]))], axis=0)
        row = lax.broadcasted_iota(jnp.int32, dm16.shape, 0)
        dm16 = jnp.where(row <= 8, dm16, 0.0)
        s = jax.nn.silu(c_ref[...])
        gw_ref[...] = lax.dot_general(s, dm16, (_DIMS["tn"], ((), ())), precision=HI, preferred_element_type=F32)
        pc = lax.dot_general(dm16, w_ref[...], (_DIMS["nt"], ((), ())), precision=HI, preferred_element_type=F32)
        pc_ref[...] = pc[8:9, :]

    return pl.pallas_call(body, name="ada_bwd", out_shape=[jax.ShapeDtypeStruct(w_ada.shape, F32), jax.ShapeDtypeStruct((1, D), F32)],
                          compiler_params=_cp())(c16, w_ada, dmx, dmc)


def _cctx_grad(pc_all, c_ctx_row):
    def body(pc_ref, c_ref, g_ref):
        tot = pc_ref[0]
        for d in range(1, N_DEV):
            tot = tot + pc_ref[d]
        _, vjp = jax.vjp(jax.nn.silu, c_ref[...])
        g_ref[...] = vjp(tot)[0]

    return pl.pallas_call(body, name="cctx_grad", out_shape=jax.ShapeDtypeStruct((1, D), F32))(pc_all, c_ctx_row)


def _adamw(parts, w, m, v, name):
    ns, R, C = parts.shape
    tb = _tile(R, (128, 64, 32, 16, 8))

    def body(p_ref, w_ref, m_ref, v_ref, g_ref, d_ref, mo_ref, vo_ref):
        g = p_ref[0].astype(F32)
        for s in range(1, ns):
            g = g + p_ref[s].astype(F32)
        m2 = ADAM_B1 * m_ref[...] + (1.0 - ADAM_B1) * g
        v2 = ADAM_B2 * v_ref[...] + (1.0 - ADAM_B2) * jnp.square(g)
        m_hat = m2 / (1.0 - ADAM_B1 ** ADAM_STEP)
        v_hat = v2 / (1.0 - ADAM_B2 ** ADAM_STEP)
        g_ref[...] = g
        d_ref[...] = -ADAM_LR * (m_hat / (jnp.sqrt(v_hat) + ADAM_EPS) + ADAM_WD * w_ref[...])
        mo_ref[...] = m2
        vo_ref[...] = v2

    row = pl.BlockSpec((tb, C), lambda i: (i, 0))
    return pl.pallas_call(
        body, grid=(R // tb,), name=name,
        in_specs=[pl.BlockSpec((ns, tb, C), lambda i: (0, i, 0)), row, row, row], out_specs=[row] * 4,
        out_shape=[jax.ShapeDtypeStruct((R, C), F32)] * 4, compiler_params=_cp(),
    )(parts, w, m, v)


_SMALL = (("dmod_x", 6 * D), ("dmod_c", 6 * D), ("b_ada", 6 * D), ("norm_mix", D), ("norm_ffn", D), ("dn_a_log", 16),
          ("dn_dt_bias", 16), ("dn_norm", HD), ("q_norm", HD), ("k_norm", HD), ("attn_sink", NH), ("ffn_conv_b", 2 * DFF),
          ("dn_conv", 5 * 3 * D), ("ffn_conv", 3 * 2 * DFF))
_SMALL_ROWS = -(-sum(n for _, n in _SMALL) // 1024) * 8


def _pack_small(d):
    flat = jnp.concatenate([d[k].reshape(-1).astype(F32) if k in d else jnp.zeros((n,), F32) for k, n in _SMALL])
    return jnp.concatenate([flat, jnp.zeros((_SMALL_ROWS * 128 - flat.shape[0],), F32)]).reshape(_SMALL_ROWS, 128)


def _unpack_small(a):
    flat = a.reshape(a.shape[:-2] + (-1,))
    out, off = {}, 0
    for k, n in _SMALL:
        out[k] = flat[..., off:off + n]
        off += n
    return out


def kernel(x, c, ctx, c_ctx, w_ada, b_ada, norm_mix, norm_ffn, w_in, dn_conv, dn_a_log, dn_dt_bias, dn_norm, q_norm, k_norm, attn_sink, w_branch_dn, w_branch_attn, w_out, ffn_up, ffn_conv, ffn_conv_b, ffn_down, loss_target, m_c_ctx, m_w_ada, m_b_ada, m_norm_mix, m_norm_ffn, m_w_in, m_dn_conv, m_dn_a_log, m_dn_dt_bias, m_dn_norm, m_q_norm, m_k_norm, m_attn_sink, m_w_branch_dn, m_w_branch_attn, m_w_out, m_ffn_up, m_ffn_conv, m_ffn_conv_b, m_ffn_down, v_c_ctx, v_w_ada, v_b_ada, v_norm_mix, v_norm_ffn, v_w_in, v_dn_conv, v_dn_a_log, v_dn_dt_bias, v_dn_norm, v_q_norm, v_k_norm, v_attn_sink, v_w_branch_dn, v_w_branch_attn, v_w_out, v_ffn_up, v_ffn_conv, v_ffn_conv_b, v_ffn_down):
    me = 4 * lax.axis_index("x") + 2 * lax.axis_index("y") + lax.axis_index("c")
    ada_cols = w_ada.shape[2]

    cols = lambda a: jnp.swapaxes(a, 0, 1).reshape(a.shape[1], -1)
    rows = lambda a: a.reshape(-1, a.shape[2])
    col_blocks = lambda g: jnp.swapaxes(g.reshape(g.shape[0], N_DEV, -1), 0, 1)
    row_blocks = lambda g: g.reshape(N_DEV, -1, g.shape[1])

    gathered = _exchange([w_in[0].astype(BF), c, dn_conv[0], ffn_conv[0]], scatter=False, name="gather_first")
    w_in_packed = _pack_w_in(cols(gathered[0]))
    c_all = gathered[1][:, 0, :]

    def project_in(h1):
        p, rest = _mm(h1, w_in_packed, form="nn", out_dtype=F32, name="in_proj",
                      exchange=([w_branch_dn[0].astype(BF), w_branch_attn[0].astype(BF), w_out[0].astype(BF),
                                 ffn_up[0].astype(BF), ffn_down[0].astype(BF)], False))
        return p, (w_in_packed, rows(rest[0]), rows(rest[1]), rows(rest[2]), cols(rest[3]), rows(rest[4]))

    def project_back(h1, dp, w_in_p, grads):
        g_bdn, g_bat, g_out, g_up, g_down = grads
        g_in, landed_rest = _mm(h1, dp, form="tn", out_dtype=BF, name="g_w_in",
                                exchange=([row_blocks(g_bdn), row_blocks(g_bat), row_blocks(g_out), col_blocks(g_up),
                                           row_blocks(g_down)], True))
        dh1, landed_in = _mm(dp, w_in_p, form="nt", out_dtype=F32, name="d_h1",
                             exchange=([col_blocks(_unpack_w_in(g_in))], True))
        return [landed_in[0]] + landed_rest, dh1

    c16 = jnp.concatenate([c_all, c_ctx[None], jnp.zeros((7, D), F32)], axis=0)
    b_loc = lax.dynamic_slice_in_dim(b_ada, me * ada_cols, ada_cols, axis=1)
    mod_part = _ada_fwd(c16, w_ada[0], b_loc)
    mod_all = cols(_exchange([mod_part], scatter=False, name="gather_mod")[0])
    mod_x = lax.dynamic_slice_in_dim(mod_all, me, 1, axis=0).reshape(6, D)
    mod_c = mod_all[8].reshape(6, D)

    loss_loc, grad_x, landed, small = _local_step(
        x[0], ctx[0], mod_x, mod_c, loss_target[0], project_in, project_back,
        norm_mix, norm_ffn, cols(gathered[2]), dn_a_log[0], dn_dt_bias[0], dn_norm, q_norm, k_norm, attn_sink[0], cols(gathered[3]),
        ffn_conv_b)
    loss = lax.psum(loss_loc, ("x", "y", "c"))

    res = {}
    res["w_in"] = _adamw(landed[0], w_in[0], m_w_in[0], v_w_in[0], "adamw_w_in")
    res["w_branch_dn"] = _adamw(landed[1], w_branch_dn[0], m_w_branch_dn[0], v_w_branch_dn[0], "adamw_w_branch_dn")
    res["w_branch_attn"] = _adamw(landed[2], w_branch_attn[0], m_w_branch_attn[0], v_w_branch_attn[0], "adamw_w_branch_attn")
    res["w_out"] = _adamw(landed[3], w_out[0], m_w_out[0], v_w_out[0], "adamw_w_out")
    res["ffn_up"] = _adamw(landed[4], ffn_up[0], m_ffn_up[0], v_ffn_up[0], "adamw_ffn_up")
    res["ffn_down"] = _adamw(landed[5], ffn_down[0], m_ffn_down[0], v_ffn_down[0], "adamw_ffn_down")

    small = dict(small)
    small["b_ada"] = small["dmod_x"] + small["dmod_c"]
    parts = _exchange([_pack_small(small)], scatter=False, name="gather_small")[0]
    per_dev = _unpack_small(parts)
    given = dict(b_ada=(b_ada, m_b_ada, v_b_ada), norm_mix=(norm_mix, m_norm_mix, v_norm_mix), norm_ffn=(norm_ffn, m_norm_ffn, v_norm_ffn),
                 dn_a_log=(dn_a_log, m_dn_a_log, v_dn_a_log), dn_dt_bias=(dn_dt_bias, m_dn_dt_bias, v_dn_dt_bias),
                 dn_norm=(dn_norm, m_dn_norm, v_dn_norm), q_norm=(q_norm, m_q_norm, v_q_norm), k_norm=(k_norm, m_k_norm, v_k_norm),
                 attn_sink=(attn_sink, m_attn_sink, v_attn_sink), ffn_conv_b=(ffn_conv_b, m_ffn_conv_b, v_ffn_conv_b))
    packs = [_pack_small({k: t[j] for k, t in given.items()}) for j in range(3)]
    upd = [_unpack_small(a) for a in _adamw(parts, packs[0], packs[1], packs[2], "adamw_small")]
    for k, t in given.items():
        res[k] = tuple(u[k].reshape(t[0].shape) for u in upd)
    dnc = lax.dynamic_slice_in_dim(upd[0]["dn_conv"].reshape(5, 3 * D), me * dn_conv.shape[2], dn_conv.shape[2], axis=1)
    ffc = lax.dynamic_slice_in_dim(upd[0]["ffn_conv"].reshape(3, 2 * DFF), me * ffn_conv.shape[2], ffn_conv.shape[2], axis=1)
    r8 = lambda a: _pad_rows8(a)
    t = _adamw(r8(dnc)[None], r8(dn_conv[0]), r8(m_dn_conv[0]), r8(v_dn_conv[0]), "adamw_dn_conv")
    res["dn_conv"] = tuple(a[:5][None] for a in t)
    t = _adamw(r8(ffc)[None], r8(ffn_conv[0]), r8(m_ffn_conv[0]), r8(v_ffn_conv[0]), "adamw_ffn_conv")
    res["ffn_conv"] = tuple(a[:3][None] for a in t)

    dmx = lax.dynamic_slice_in_dim(per_dev["dmod_x"], me * ada_cols, ada_cols, axis=1)
    dmc = lax.dynamic_slice_in_dim(per_dev["dmod_c"], me * ada_cols, ada_cols, axis=1)
    g_ada, pc = _ada_bwd(c16, w_ada[0], dmx, dmc)
    res["w_ada"] = _adamw(g_ada[None], w_ada[0], m_w_ada[0], v_w_ada[0], "adamw_w_ada")
    pc_all = _exchange([pc], scatter=False, name="gather_cctx")[0]
    g_cctx = _cctx_grad(pc_all, c_ctx[None])
    r8b = lambda a: jnp.broadcast_to(a, (8, D))
    t = _adamw(r8b(g_cctx)[None], r8b(c_ctx[None]), r8b(m_c_ctx[None]), r8b(v_c_ctx[None]), "adamw_c_ctx")
    res["c_ctx"] = tuple(a[0] for a in t)

    names = ("c_ctx", "w_ada", "b_ada", "norm_mix", "norm_ffn", "w_in", "dn_conv", "dn_a_log", "dn_dt_bias", "dn_norm", "q_norm",
             "k_norm", "attn_sink", "w_branch_dn", "w_branch_attn", "w_out", "ffn_up", "ffn_conv", "ffn_conv_b", "ffn_down")
    lead = ("w_ada", "w_in", "w_branch_dn", "w_branch_attn", "w_out", "ffn_up", "ffn_down")
    fix = lambda k, a: a[None] if k in lead else a
    outs = [loss, grad_x[None]]
    for j in range(4):
        outs += [fix(k, res[k][j]) for k in names]
    return tuple(outs)
```

```python
import functools

import jax
import jax.numpy as jnp
from jax import lax
from jax.experimental import pallas as pl
from jax.experimental.pallas import tpu as pltpu

F32 = jnp.float32
BF = jnp.bfloat16
HI = lax.Precision.HIGHEST
MESH = pl.DeviceIdType.MESH

D = 1024
NH = 8
HD = 128
KVH = 2
GRP = 4
KV = KVH * HD
DFF = 2816
CB = 128
GRID_W = 64
ROPE_BASE = 10000.0
EPS = 1e-6
N_DEV = 8
PW = 8192
O_QKV, O_GT, O_Q, O_MG, O_K, O_V, O_BA = 0, 3072, 4096, 5120, 7168, 7424, 7680
IN_SIZES = (3072, 1024, 16, 16, 1024, 256, 256, 2048)
IN_DIM = sum(IN_SIZES)
ADAM_LR, ADAM_B1, ADAM_B2, ADAM_EPS, ADAM_WD, ADAM_STEP = 0.001, 0.9, 0.999, 1e-08, 0.01, 10
VMEM_LIMIT = 56 * 1024 * 1024


def _cp():
    return pltpu.CompilerParams(vmem_limit_bytes=VMEM_LIMIT)


def _tile(n, cands):
    for c in cands:
        if n % c == 0:
            return c
    return n


def _iota2(shape):
    return lax.broadcasted_iota(jnp.int32, shape, 0), lax.broadcasted_iota(jnp.int32, shape, 1)


_DIMS = {"nn": ((1,), (0,)), "nt": ((1,), (1,)), "tn": ((0,), (0,))}


def _exchange_copies(ins, outs, send_sems, recv_sems, local_sems, scatter, landings):
    x, y, c = lax.axis_index("x"), lax.axis_index("y"), lax.axis_index("c")
    me = 4 * x + 2 * y + c
    local, remote = [], []
    for k in range(len(ins)):
        local.append(pltpu.make_async_copy(ins[k].at[me] if scatter else ins[k], outs[k].at[me], local_sems.at[k]))
        for m in range(1, N_DEV):
            px = 1 - x if m & 4 else x
            py = 1 - y if m & 2 else y
            pc = 1 - c if m & 1 else c
            peer = 4 * px + 2 * py + pc
            src = ins[k].at[peer] if scatter else ins[k]
            sem = k * (N_DEV - 1) + m - 1
            push = pltpu.make_async_remote_copy(src_ref=src, dst_ref=outs[k].at[me], send_sem=send_sems.at[sem],
                                                recv_sem=recv_sems.at[sem], device_id=(px, py, pc), device_id_type=MESH)
            landing = None
            if landings:
                landing = pltpu.make_async_remote_copy(src_ref=src, dst_ref=outs[k].at[peer], send_sem=send_sems.at[sem],
                                                       recv_sem=recv_sems.at[sem], device_id=(px, py, pc), device_id_type=MESH)
            remote.append((push, landing))
    return local, remote


def _exchange_start(*args):
    local, remote = _exchange_copies(*args, landings=False)
    for cp in local:
        cp.start()
    for push, _ in remote:
        push.start()


def _exchange_wait(*args):
    local, remote = _exchange_copies(*args, landings=True)
    for _, landing in remote:
        landing.wait_recv()
    for push, _ in remote:
        push.wait_send()
    for cp in local:
        cp.wait()


def _exchange_shapes(arrays, scatter):
    out_shape = [jax.ShapeDtypeStruct(a.shape if scatter else (N_DEV,) + a.shape, a.dtype) for a in arrays]
    n = len(arrays)
    sems = [pltpu.SemaphoreType.DMA((n * (N_DEV - 1),)), pltpu.SemaphoreType.DMA((n * (N_DEV - 1),)), pltpu.SemaphoreType.DMA((n,))]
    return out_shape, sems


def _mm(a, b, *, form, out_dtype, name, tm=None, tn=None, tk=None, exchange=None):
    if form == "tn":
        K, M = a.shape
        N = b.shape[1]
    else:
        M, K = a.shape
        N = b.shape[0] if form == "nt" else b.shape[1]
    tm = tm or _tile(M, (1408, 1280, 1024, 640, 512, 256, 128))
    tn = tn or _tile(N, (1408, 1024, 512, 256, 128))
    tk = tk or _tile(K, (2048, 1408, 1280, 1024, 640, 512, 256, 128))
    ni, nj, nk = M // tm, N // tn, K // tk
    dims = (_DIMS[form], ((), ()))
    ex_arrays, scatter = exchange if exchange else ([], False)
    nx = len(ex_arrays)

    def body(a_ref, b_ref, *refs):
        ex_in, o_ref, ex_out, scratch = refs[:nx], refs[nx], refs[nx + 1:2 * nx + 1], refs[2 * nx + 1:]
        i, j, k = pl.program_id(0), pl.program_id(1), pl.program_id(2)
        if nx:
            sems = scratch[-3:]

            @pl.when((i == 0) & (j == 0) & (k == 0))
            def _():
                _exchange_start(ex_in, ex_out, *sems, scatter)

        part = lax.dot_general(a_ref[...].astype(BF), b_ref[...].astype(BF), dims, preferred_element_type=F32)
        if nk == 1:
            o_ref[...] = part.astype(out_dtype)
        else:
            acc_ref = scratch[0]

            @pl.when(k == 0)
            def _():
                acc_ref[...] = part

            @pl.when(k > 0)
            def _():
                acc_ref[...] += part

            @pl.when(k == nk - 1)
            def _():
                o_ref[...] = acc_ref[...].astype(out_dtype)

        if nx:
            @pl.when((i == ni - 1) & (j == nj - 1) & (k == nk - 1))
            def _():
                _exchange_wait(ex_in, ex_out, *sems, scatter)

    if form == "tn":
        a_spec = pl.BlockSpec((tk, tm), lambda i, j, k: (k, i))
    else:
        a_spec = pl.BlockSpec((tm, tk), lambda i, j, k: (i, k))
    if form == "nt":
        b_spec = pl.BlockSpec((tn, tk), lambda i, j, k: (j, k))
    else:
        b_spec = pl.BlockSpec((tk, tn), lambda i, j, k: (k, j))
    hbm = pl.BlockSpec(memory_space=pl.ANY)
    ex_shapes, ex_sems = _exchange_shapes(ex_arrays, scatter) if nx else ([], [])
    outs = pl.pallas_call(
        body, grid=(ni, nj, nk), name=name,
        in_specs=[a_spec, b_spec] + [hbm] * nx, out_specs=[pl.BlockSpec((tm, tn), lambda i, j, k: (i, j))] + [hbm] * nx,
        out_shape=[jax.ShapeDtypeStruct((M, N), out_dtype)] + ex_shapes,
        scratch_shapes=([] if nk == 1 else [pltpu.VMEM((tm, tn), F32)]) + ex_sems,
        compiler_params=_cp(),
    )(a, b, *ex_arrays)
    return (outs[0], list(outs[1:])) if nx else outs[0]


def _norm_mod_fn(x, nw, sh, sc):
    y = x * lax.rsqrt(jnp.mean(x * x, axis=-1, keepdims=True) + EPS)
    return (y * nw) * (1.0 + sc) + sh


def _norm_mod_fwd(x, ctx, nw, sh, sc, name):
    nlat = x.shape[0]
    T = nlat + ctx.shape[0]
    tb = _tile(ctx.shape[0], (256, 128))
    nlb = nlat // tb

    def body(x_ref, c_ref, nw_ref, sh_ref, sc_ref, h_ref):
        rows = jnp.where(pl.program_id(0) < nlb, x_ref[...], c_ref[...])
        h_ref[...] = _norm_mod_fn(rows, nw_ref[...], sh_ref[0], sc_ref[0]).astype(BF)

    seg = pl.BlockSpec((1, 1, D), lambda i: (jnp.where(i >= nlb, 1, 0), 0, 0))
    return pl.pallas_call(
        body, grid=(T // tb,), name=name,
        in_specs=[pl.BlockSpec((tb, D), lambda i: (jnp.minimum(i, nlb - 1), 0)),
                  pl.BlockSpec((tb, D), lambda i: (jnp.maximum(i - nlb, 0), 0)), pl.BlockSpec((1, D), lambda i: (0, 0)), seg, seg],
        out_specs=pl.BlockSpec((tb, D), lambda i: (i, 0)),
        out_shape=jax.ShapeDtypeStruct((T, D), BF),
    )(x, ctx, nw, sh, sc)


def _norm_mod_bwd(x, nw, sh, sc, dh, *, row0, name, residual=None):
    nrows = x.shape[0]
    tb = _tile(nrows, (256, 128))
    b0 = row0 // tb

    def body(x_ref, nw_ref, sh_ref, sc_ref, dh_ref, *refs):
        dnw_ref, dsh_ref, dsc_ref = refs[-3:]
        _, vjp = jax.vjp(_norm_mod_fn, x_ref[...], nw_ref[...], sh_ref[...], sc_ref[...])
        dx, dnw, dsh, dsc = vjp(dh_ref[...])
        if residual is not None:
            refs[1][...] = dx + refs[0][...]

        @pl.when(pl.program_id(0) == 0)
        def _():
            dnw_ref[...] = jnp.zeros_like(dnw_ref)
            dsh_ref[...] = jnp.zeros_like(dsh_ref)
            dsc_ref[...] = jnp.zeros_like(dsc_ref)

        dnw_ref[...] += dnw
        dsh_ref[...] += dsh
        dsc_ref[...] += dsc

    dh_row = pl.BlockSpec((tb, D), lambda i: (b0 + i, 0))
    out_row = pl.BlockSpec((tb, D), lambda i: (i, 0))
    one = pl.BlockSpec((1, D), lambda i: (0, 0))
    with_dx = residual is not None
    return pl.pallas_call(
        body, grid=(nrows // tb,), name=name,
        in_specs=[out_row, one, one, one, dh_row] + [out_row] * with_dx, out_specs=[out_row] * with_dx + [one] * 3,
        out_shape=[jax.ShapeDtypeStruct((nrows, D), F32)] * with_dx + [jax.ShapeDtypeStruct((1, D), F32)] * 3,
    )(x, nw, sh, sc, dh, *([residual] if with_dx else []))


def _resid_norm_fwd(x, gate, y, nw, sh, sc):
    n = y.shape[0]
    tb = _tile(n, (256, 128))

    def body(x_ref, g_ref, y_ref, nw_ref, sh_ref, sc_ref, x1_ref, h_ref):
        x1 = x_ref[...] + g_ref[...] * y_ref[...]
        x1_ref[...] = x1
        h_ref[...] = _norm_mod_fn(x1, nw_ref[...], sh_ref[...], sc_ref[...]).astype(BF)

    row = pl.BlockSpec((tb, D), lambda i: (i, 0))
    one = pl.BlockSpec((1, D), lambda i: (0, 0))
    return pl.pallas_call(
        body, grid=(n // tb,), name="resid_norm_fwd",
        in_specs=[row, one, row, one, one, one], out_specs=[row, row],
        out_shape=[jax.ShapeDtypeStruct((n, D), F32), jax.ShapeDtypeStruct((n, D), BF)],
    )(x, gate, y, nw, sh, sc)


def _resid_norm_bwd(x1, gate, y, nw, sh, sc, dh, dx1_direct, w_out):
    n = y.shape[0]
    tb = _tile(n, (256, 128))

    def body(x1_ref, g_ref, y_ref, nw_ref, sh_ref, sc_ref, dh_ref, dd_ref, wo_ref,
             dx_ref, dy_ref, dg_ref, dnw_ref, dsh_ref, dsc_ref, dm_ref):
        _, vjp = jax.vjp(_norm_mod_fn, x1_ref[...], nw_ref[...], sh_ref[...], sc_ref[...])
        dxn, dnw, dsh, dsc = vjp(dh_ref[...])
        dx = dxn + dd_ref[...]
        dx_ref[...] = dx
        dy_ref[...] = (g_ref[...] * dx).astype(BF)
        dm_ref[...] = lax.dot_general(dy_ref[...], wo_ref[...], (_DIMS["nt"], ((), ())), preferred_element_type=F32)

        @pl.when(pl.program_id(0) == 0)
        def _():
            for r in (dg_ref, dnw_ref, dsh_ref, dsc_ref):
                r[...] = jnp.zeros_like(r)

        dg_ref[...] += jnp.sum(dx * y_ref[...], axis=0, keepdims=True)
        dnw_ref[...] += dnw
        dsh_ref[...] += dsh
        dsc_ref[...] += dsc

    row = pl.BlockSpec((tb, D), lambda i: (i, 0))
    one = pl.BlockSpec((1, D), lambda i: (0, 0))
    return pl.pallas_call(
        body, grid=(n // tb,), name="resid_norm_bwd",
        in_specs=[row, one, row, one, one, one, row, row, _resident((D, D))], out_specs=[row, row] + [one] * 4 + [row],
        out_shape=[jax.ShapeDtypeStruct((n, D), F32), jax.ShapeDtypeStruct((n, D), BF)] + [jax.ShapeDtypeStruct((1, D), F32)] * 4
        + [jax.ShapeDtypeStruct((n, D), F32)],
    )(x1, gate, y, nw, sh, sc, dh, dx1_direct, w_out)


HALO = 8


def _halo_specs(tb, width, nrows, col=0):
    r8 = tb // HALO
    cur = pl.BlockSpec((tb, width), lambda i: (i, col))
    prev = pl.BlockSpec((HALO, width), lambda i: (jnp.maximum(i * r8 - 1, 0), col))
    nxt = pl.BlockSpec((HALO, width), lambda i: (jnp.minimum((i + 1) * r8, nrows // HALO - 1), col))
    return [cur, prev, nxt]


def _segment_edges(seg_rows, tb):
    bounds = [0]
    for s in seg_rows:
        bounds.append(bounds[-1] + s // tb)
    return bounds[:-1], [b - 1 for b in bounds[1:]]


def _keep_halos(i, starts, ends):
    keep_p = functools.reduce(lambda a, b: a & b, [i != s for s in starts])
    keep_n = functools.reduce(lambda a, b: a & b, [i != e for e in ends])
    return keep_p, keep_n


def _ext_rows(refs, cols, keep):
    cur_ref, prev_ref, next_ref = refs
    p = jnp.where(keep[0], prev_ref[:, cols].astype(F32), 0.0)
    n = jnp.where(keep[1], next_ref[:, cols].astype(F32), 0.0)
    return jnp.concatenate([p, cur_ref[:, cols].astype(F32), n], axis=0)


def _shifted_rows(xe, width, transpose=False):
    r = width // 2
    n = xe.shape[0]
    out = []
    for j in range(width):
        s = ((j - r) if transpose else (r - j)) % n
        out.append(xe if s == 0 else pltpu.roll(xe, s, 0))
    return out


def _conv_rows(shifted, w_ref, cols):
    acc = None
    for j, xs in enumerate(shifted):
        term = xs * w_ref[j:j + 1, cols]
        acc = term if acc is None else acc + term
    return acc


def _tap_grads(dcur, shifted, tb):
    return [jnp.sum(dcur * xs[HALO:HALO + tb], axis=0, keepdims=True) for xs in shifted]


def _softplus(x):
    return jnp.maximum(x, 0.0) + jnp.log(1.0 + jnp.exp(-jnp.abs(x)))


def _gates_fn(ba, alog_row, dt_row):
    col = lax.broadcasted_iota(jnp.int32, ba.shape, 1)
    beta = jax.nn.sigmoid(ba)
    g = -jnp.exp(alog_row) * _softplus(ba + dt_row)
    return jnp.where(col < 16, beta, jnp.where(col < 32, g, 0.0))


def _qkv_post_fn(c, kind):
    y = jax.nn.silu(c)
    if kind == 2:
        return y
    n = y * lax.rsqrt(jnp.sum(y * y, axis=-1, keepdims=True) + EPS)
    return n * (HD ** -0.5) if kind == 0 else n


DN_TAPS = 5
FFN_TAPS = 3


def _dn_pre_fwd(p, w8, alog_row, dt_row, seg_rows):
    T = p.shape[0]
    tb = _tile(T, (256, 128))
    starts, ends = _segment_edges(seg_rows, tb)

    def body(cur_ref, prev_ref, next_ref, ba_ref, w_ref, al_ref, dt_ref, q_ref, k_ref, v_ref, gb_ref):
        keep = _keep_halos(pl.program_id(0), starts, ends)
        outs = (q_ref, k_ref, v_ref)
        for kind in range(3):
            for h in range(NH):
                cols = slice(kind * D + h * HD, kind * D + (h + 1) * HD)
                xe = _ext_rows((cur_ref, prev_ref, next_ref), cols, keep)
                conv = _conv_rows(_shifted_rows(xe, DN_TAPS), w_ref, cols)[HALO:HALO + tb]
                outs[kind][:, h * HD:(h + 1) * HD] = _qkv_post_fn(conv, kind)
        gb_ref[...] = _gates_fn(ba_ref[...], al_ref[...], dt_ref[...])

    row = pl.BlockSpec((tb, D), lambda i: (i, 0))
    one = pl.BlockSpec((1, 128), lambda i: (0, 0))
    return pl.pallas_call(
        body, grid=(T // tb,), name="dn_pre_fwd",
        in_specs=_halo_specs(tb, 3 * D, T) + [pl.BlockSpec((tb, 128), lambda i: (i, O_BA // 128)),
                                              pl.BlockSpec((8, 3 * D), lambda i: (0, 0)), one, one],
        out_specs=[row, row, row, pl.BlockSpec((tb, 128), lambda i: (i, 0))],
        out_shape=[jax.ShapeDtypeStruct((T, D), F32)] * 3 + [jax.ShapeDtypeStruct((T, 128), F32)],
        compiler_params=_cp(),
    )(p, p, p, p, w8, alog_row, dt_row)


def _dn_pre_bwd(p, w8, alog_row, dt_row, dq, dk, dv, dgb, seg_rows):
    T = p.shape[0]
    tb = _tile(T, (256, 128))
    starts, ends = _segment_edges(seg_rows, tb)

    def body(cur_ref, prev_ref, next_ref, ba_ref, w_ref, al_ref, dt_ref,
             dq_c, dq_p, dq_n, dk_c, dk_p, dk_n, dv_c, dv_p, dv_n, dgb_ref, dx_ref, dba_ref, dw_ref, dal_ref, ddt_ref):
        i = pl.program_id(0)
        keep = _keep_halos(i, starts, ends)

        @pl.when(i == 0)
        def _():
            dw_ref[...] = jnp.zeros_like(dw_ref)
            dal_ref[...] = jnp.zeros_like(dal_ref)
            ddt_ref[...] = jnp.zeros_like(ddt_ref)

        douts = ((dq_c, dq_p, dq_n), (dk_c, dk_p, dk_n), (dv_c, dv_p, dv_n))
        for kind in range(3):
            for h in range(NH):
                cols = slice(kind * D + h * HD, kind * D + (h + 1) * HD)
                xe = _ext_rows((cur_ref, prev_ref, next_ref), cols, keep)
                shifted = _shifted_rows(xe, DN_TAPS)
                conv = _conv_rows(shifted, w_ref, cols)
                dye = _ext_rows(douts[kind], slice(h * HD, (h + 1) * HD), keep)
                _, vjp = jax.vjp(functools.partial(_qkv_post_fn, kind=kind), conv)
                dce = vjp(dye)[0]
                dx_ref[:, cols] = _conv_rows(_shifted_rows(dce, DN_TAPS, transpose=True), w_ref, cols)[HALO:HALO + tb].astype(BF)
                for j, g in enumerate(_tap_grads(dce[HALO:HALO + tb], shifted, tb)):
                    dw_ref[j:j + 1, cols] += g
        _, vjp = jax.vjp(_gates_fn, ba_ref[...], al_ref[...], dt_ref[...])
        dba, dal, ddt = vjp(dgb_ref[...])
        dba_ref[...] = dba.astype(BF)
        dal_ref[...] += dal
        ddt_ref[...] += ddt

    one = pl.BlockSpec((1, 128), lambda i: (0, 0))
    nar = pl.BlockSpec((tb, 128), lambda i: (i, 0))
    wspec = pl.BlockSpec((8, 3 * D), lambda i: (0, 0))
    return pl.pallas_call(
        body, grid=(T // tb,), name="dn_pre_bwd",
        in_specs=_halo_specs(tb, 3 * D, T) + [pl.BlockSpec((tb, 128), lambda i: (i, O_BA // 128)), wspec, one, one]
        + _halo_specs(tb, D, T) * 3 + [nar],
        out_specs=[pl.BlockSpec((tb, 3 * D), lambda i: (i, 0)), nar, wspec, one, one],
        out_shape=[jax.ShapeDtypeStruct((T, 3 * D), BF), jax.ShapeDtypeStruct((T, 128), BF), jax.ShapeDtypeStruct((8, 3 * D), F32),
                   jax.ShapeDtypeStruct((1, 128), F32), jax.ShapeDtypeStruct((1, 128), F32)],
        compiler_params=_cp(),
    )(p, p, p, p, w8, alog_row, dt_row, dq, dq, dq, dk, dk, dk, dv, dv, dv, dgb)


def _dot_hi(a, b):
    return jnp.dot(a, b, precision=HI, preferred_element_type=F32)


def _dot_bf(a, b):
    return jnp.dot(a.astype(BF), b.astype(BF), preferred_element_type=F32)


def _dot_nt_bf(a, b):
    return lax.dot_general(a.astype(BF), b.astype(BF), (_DIMS["nt"], ((), ())), preferred_element_type=F32)


def _dot_tn_bf(a, b):
    return lax.dot_general(a.astype(BF), b.astype(BF), (_DIMS["tn"], ((), ())), preferred_element_type=F32)


def _dot_h3(a, b):
    return jnp.dot(a, b, precision=lax.Precision.HIGH, preferred_element_type=F32)


def _dot_split(fine, coarse, form):
    hi = fine.astype(BF)
    lo = (fine - hi.astype(F32)).astype(BF)
    cb = coarse.astype(BF)
    if form == "tn":
        return lax.dot_general(jnp.concatenate([cb, cb], axis=0), jnp.concatenate([hi, lo], axis=0),
                               (_DIMS["tn"], ((), ())), preferred_element_type=F32)
    parts = jnp.concatenate([hi, lo], axis=1)
    if form == "nt":
        return lax.dot_general(parts, jnp.concatenate([cb, cb], axis=1), (_DIMS["nt"], ((), ())), preferred_element_type=F32)
    return jnp.dot(parts, jnp.concatenate([cb, cb], axis=0), preferred_element_type=F32)


@jax.custom_vjp
def _mm_split(a, b):
    return _dot_split(a, b, "nn")


_mm_split.defvjp(lambda a, b: (_dot_split(a, b, "nn"), (a, b)),
                 lambda res, dc: (_dot_split(dc, res[1], "nt"), _dot_split(dc, res[0], "tn")))


def _unit_tri_inverses(mats):
    r, c = _iota2((CB, CB))
    eye = (r == c).astype(F32)
    a8 = [jnp.where((r // 8) == (c // 8), a, 0.0) for a in mats]
    a2 = [_dot_split(x, x, "nn") for x in a8]
    a4 = [_dot_split(x, x, "nn") for x in a2]
    t = [_dot_split(eye - x, eye + y, "nn") for x, y in zip(a8, a2)]
    t = [_dot_split(x, eye + y, "nn") for x, y in zip(t, a4)]
    b = 8
    while b < CB:
        mask = ((r // (2 * b)) == (c // (2 * b))) & ((r // b) != (c // b))
        te = [_dot_split(x, jnp.where(mask, a, 0.0), "nn") for x, a in zip(t, mats)]
        t = [x - _dot_split(y, x, "nn") for x, y in zip(t, te)]
        b *= 2
    return t


@jax.custom_vjp
def _saved_inverse(a, t):
    return t


_saved_inverse.defvjp(lambda a, t: (t, t),
                      lambda t, dt: (-_dot_split(_dot_split(dt, t, "nt"), t, "tn"), jnp.zeros_like(t)))


def _dn1_decay(gc, reverse):
    r, c = _iota2((CB, CB))
    incl = (c >= r) if reverse else (c <= r)
    return jnp.where(incl, jnp.exp(jnp.where(incl, gc - gc.T, 0.0)), 0.0)


def _dn1_heads(qs, ks, vs, betas, gcs, ts_saved, reverse, kks=None, qks=None):
    r, c = _iota2((CB, CB))
    strict = (c > r) if reverse else (c < r)
    decays = [_dn1_decay(gc, reverse) for gc in gcs]
    kks = kks or [_dot_nt_bf(k, k) for k in ks]
    systems = [jnp.where(strict, b * kk * dc, 0.0) for b, kk, dc in zip(betas, kks, decays)]
    if ts_saved is None:
        ts = _unit_tri_inverses(systems)
    else:
        ts = [_saved_inverse(a, t) for a, t in zip(systems, ts_saved)]
    egs = [jnp.exp(gc) for gc in gcs]
    us = [_mm_split(t, v * b) for t, v, b in zip(ts, vs, betas)]
    ws = [_mm_split(t, k * (b * eg)) for t, k, b, eg in zip(ts, ks, betas, egs)]
    qks = qks or [_dot_nt_bf(q, k) for q, k in zip(qs, ks)]
    last = 0 if reverse else CB - 1
    glogs = [jnp.sum(jnp.where(r == last, gc, 0.0), axis=0, keepdims=True) for gc in gcs]
    outs = [(u, w, q * eg, k * jnp.exp(gl - gc), qk * dc, jnp.exp(gl))
            for u, w, q, k, eg, gl, gc, qk, dc in zip(us, ws, qs, ks, egs, glogs, gcs, qks, decays)]
    return outs, ts


def _cum_matrix(upper):
    r, c = _iota2((CB, CB))
    return ((c >= r) if upper else (c <= r)).astype(F32)


def _lane_bcast(x, col):
    return jnp.broadcast_to(x[:, col:col + 1], x.shape)


_HEAD_SLICES = [slice(h * HD, (h + 1) * HD) for h in range(NH)]


def _dn1_fwd(q, k, v, gb):
    T = q.shape[0]
    nb = T // CB

    def body(q_ref, k_ref, v_ref, gb_ref, *out_refs):
        gbv = gb_ref[...]
        qs = [q_ref[:, sl] for sl in _HEAD_SLICES]
        ks = [k_ref[:, sl] for sl in _HEAD_SLICES]
        vs = [v_ref[:, sl] for sl in _HEAD_SLICES]
        kks = [_dot_nt_bf(x, x) for x in ks]
        qks = [_dot_nt_bf(x, y) for x, y in zip(qs, ks)]
        for d in (0, 1):
            u_ref, w_ref, qg_ref, kd_ref, qkd_ref, gl_ref, t_ref = out_refs[7 * d:7 * d + 7]
            gcum = _dot_h3(_cum_matrix(d == 1), gbv)
            betas = [_lane_bcast(gbv, d * NH + h) for h in range(NH)]
            gcs = [_lane_bcast(gcum, 16 + d * NH + h) for h in range(NH)]
            outs, ts = _dn1_heads(qs, ks, vs, betas, gcs, None, d == 1, kks, qks)
            for h, sl in enumerate(_HEAD_SLICES):
                u, w, qg, kd, qkd, gl = outs[h]
                u_ref[:, sl] = u
                w_ref[:, sl] = w.astype(BF)
                qg_ref[:, sl] = qg.astype(BF)
                kd_ref[:, sl] = kd.astype(BF)
                qkd_ref[:, sl] = qkd.astype(BF)
                gl_ref[h] = gl
                t_ref[:, sl] = ts[h]

    tb = pl.BlockSpec((CB, D), lambda i: (i, 0))
    one_dir_specs = [tb, tb, tb, tb, tb, pl.BlockSpec((NH, 1, 128), lambda i: (i, 0, 0)), tb]
    one_dir_shapes = ([jax.ShapeDtypeStruct((T, D), F32)] + [jax.ShapeDtypeStruct((T, D), BF)] * 4
                      + [jax.ShapeDtypeStruct((nb * NH, 1, 128), F32), jax.ShapeDtypeStruct((T, D), F32)])
    outs = pl.pallas_call(
        body, grid=(nb,), name="dn1_fwd",
        in_specs=[tb, tb, tb, pl.BlockSpec((CB, 128), lambda i: (i, 0))],
        out_specs=one_dir_specs * 2, out_shape=one_dir_shapes * 2, compiler_params=_cp(),
    )(q, k, v, gb)
    return [tuple(outs[:7]), tuple(outs[7:])]


def _dn1_bwd(q, k, v, gb, tinvs, cots):
    T = q.shape[0]
    nb = T // CB

    def body(q_ref, k_ref, v_ref, gb_ref, *refs):
        dir_refs, (dq_ref, dk_ref, dv_ref, dgb_ref) = refs[:14], refs[14:]
        gbv = gb_ref[...]
        qs = [q_ref[:, sl] for sl in _HEAD_SLICES]
        ks = [k_ref[:, sl] for sl in _HEAD_SLICES]
        vs = [v_ref[:, sl] for sl in _HEAD_SLICES]
        lane = lax.broadcasted_iota(jnp.int32, (CB, 128), 1)
        dgb = jnp.zeros((CB, 128), F32)
        for d in (0, 1):
            t_ref, du_ref, dw_ref, dqg_ref, dkd_ref, dqkd_ref, dgl_ref = dir_refs[7 * d:7 * d + 7]
            gcum = _dot_h3(_cum_matrix(d == 1), gbv)
            betas = [_lane_bcast(gbv, d * NH + h) for h in range(NH)]
            gcs = [_lane_bcast(gcum, 16 + d * NH + h) for h in range(NH)]
            ts = [t_ref[:, sl] for sl in _HEAD_SLICES]
            f = lambda qs, ks, vs, betas, gcs: _dn1_heads(qs, ks, vs, betas, gcs, ts, d == 1)[0]
            _, vjp = jax.vjp(f, qs, ks, vs, betas, gcs)
            cot = [(du_ref[:, sl], dw_ref[:, sl], dqg_ref[:, sl], dkd_ref[:, sl], dqkd_ref[:, sl], dgl_ref[h])
                   for h, sl in enumerate(_HEAD_SLICES)]
            dqs, dks, dvs, dbetas, dgcs = vjp(cot)
            dgcum = jnp.zeros((CB, 128), F32)
            for h, sl in enumerate(_HEAD_SLICES):
                if d == 0:
                    dq_ref[:, sl] = dqs[h]
                    dk_ref[:, sl] = dks[h]
                    dv_ref[:, sl] = dvs[h]
                else:
                    dq_ref[:, sl] += dqs[h]
                    dk_ref[:, sl] += dks[h]
                    dv_ref[:, sl] += dvs[h]
                dgb = dgb + jnp.where(lane == d * NH + h, jnp.sum(dbetas[h], axis=1, keepdims=True), 0.0)
                dgcum = dgcum + jnp.where(lane == 16 + d * NH + h, jnp.sum(dgcs[h], axis=1, keepdims=True), 0.0)
            dgb = dgb + _dot_h3(_cum_matrix(d == 0), dgcum)
        dgb_ref[...] = dgb

    tb = pl.BlockSpec((CB, D), lambda i: (i, 0))
    gbs = pl.BlockSpec((CB, 128), lambda i: (i, 0))
    gls = pl.BlockSpec((NH, 1, 128), lambda i: (i, 0, 0))
    args = []
    for d in (0, 1):
        args += [tinvs[d], *cots[d]]
    return pl.pallas_call(
        body, grid=(nb,), name="dn1_bwd",
        in_specs=[tb, tb, tb, gbs] + [tb, tb, tb, tb, tb, tb, gls] * 2, out_specs=[tb, tb, tb, gbs],
        out_shape=[jax.ShapeDtypeStruct((T, D), F32)] * 3 + [jax.ShapeDtypeStruct((T, 128), F32)],
        compiler_params=_cp(),
    )(q, k, v, gb, *args)


def _dn2_steps(chains):
    ws = [_dot_bf(w, s) for _, w, _, _, _, _, s in chains]
    v_new = [c[0] - x for c, x in zip(chains, ws)]
    o_state = [_dot_bf(c[2], c[6]) for c in chains]
    o_local = [_dot_bf(c[4], vn) for c, vn in zip(chains, v_new)]
    grow = [_dot_tn_bf(c[3], vn) for c, vn in zip(chains, v_new)]
    return [a + b for a, b in zip(o_state, o_local)], [c[6] * c[5] + g for c, g in zip(chains, grow)]


def _scan_order(direction, nlat_b, nall_b):
    if direction == 0:
        return lambda i: (i + nlat_b) % nall_b
    return lambda i: nall_b - 1 - i


def _dn2_fwd(per_dir, nlat):
    T = per_dir[0][0].shape[0]
    nb = T // CB
    blks = [_scan_order(d, nlat // CB, nb) for d in (0, 1)]

    def body(*refs):
        ins, outs, s_scr = refs[:12], refs[12:16], refs[16]

        @pl.when(pl.program_id(0) == 0)
        def _():
            s_scr[...] = jnp.zeros_like(s_scr)
        for d in (0, 1):
            outs[2 * d + 1][0] = s_scr[d]
        where = [(d, h, sl) for h, sl in enumerate(_HEAD_SLICES) for d in (0, 1)]
        chains = []
        for d, h, sl in where:
            u_ref, w_ref, qg_ref, kd_ref, qkd_ref, gl_ref = ins[6 * d:6 * d + 6]
            chains.append((u_ref[:, sl], w_ref[:, sl], qg_ref[:, sl], kd_ref[:, sl], qkd_ref[:, sl], gl_ref[h], s_scr[d, h]))
        os, states = _dn2_steps(chains)
        for (d, h, sl), o, s_next in zip(where, os, states):
            outs[2 * d][:, sl] = o
            s_scr[d, h] = s_next

    in_specs, out_specs, args = [], [], []
    for d in (0, 1):
        blk = blks[d]
        tb = pl.BlockSpec((CB, D), lambda i, blk=blk: (blk(i), 0))
        in_specs += [tb] * 5 + [pl.BlockSpec((NH, 1, 128), lambda i, blk=blk: (blk(i), 0, 0))]
        out_specs += [tb, pl.BlockSpec((1, NH, HD, HD), lambda i, blk=blk: (blk(i), 0, 0, 0))]
        args += list(per_dir[d])
    outs = pl.pallas_call(
        body, grid=(nb,), name="dn2_fwd", in_specs=in_specs, out_specs=out_specs,
        out_shape=[jax.ShapeDtypeStruct((T, D), F32), jax.ShapeDtypeStruct((nb, NH, HD, HD), F32)] * 2,
        scratch_shapes=[pltpu.VMEM((2, NH, HD, HD), F32)], compiler_params=_cp(),
    )(*args)
    return [tuple(outs[:2]), tuple(outs[2:])]


def _dn2_bwd(per_dir, do, nlat):
    T = per_dir[0][0].shape[0]
    nb = T // CB
    nlat_b = nlat // CB
    fwd = [_scan_order(d, nlat_b, nb) for d in (0, 1)]
    blks = [lambda i, f=f: f(nb - 1 - i) for f in fwd]

    def body(*refs):
        ins, outs, ds_scr = refs[:16], refs[16:28], refs[28]
        i = pl.program_id(0)

        @pl.when(i == 0)
        def _():
            ds_scr[...] = jnp.zeros_like(ds_scr)
        where = [(d, h, sl) for h, sl in enumerate(_HEAD_SLICES) for d in (0, 1)]
        chains, cot_o, cot_s = [], [], []
        for d, h, sl in where:
            u_ref, w_ref, qg_ref, kd_ref, qkd_ref, gl_ref, sall_ref, do_ref = ins[8 * d:8 * d + 8]
            chains.append((u_ref[:, sl], w_ref[:, sl].astype(F32), qg_ref[:, sl].astype(F32), kd_ref[:, sl].astype(F32),
                           qkd_ref[:, sl].astype(F32), gl_ref[h], sall_ref[0, h]))
            cot_o.append(jnp.where(blks[d](i) < nlat_b, do_ref[:, sl], 0.0))
            cot_s.append(ds_scr[d, h])
        _, vjp = jax.vjp(_dn2_steps, chains)
        for (d, h, sl), (du, dw, dqg, dkd, dqkd, dgl, ds) in zip(where, vjp((cot_o, cot_s))[0]):
            du_ref, dw_ref, dqg_ref, dkd_ref, dqkd_ref, dgl_ref = outs[6 * d:6 * d + 6]
            du_ref[:, sl] = du
            dw_ref[:, sl] = dw
            dqg_ref[:, sl] = dqg
            dkd_ref[:, sl] = dkd
            dqkd_ref[:, sl] = dqkd
            dgl_ref[h] = dgl
            ds_scr[d, h] = ds

    in_specs, out_specs, args = [], [], []
    for d in (0, 1):
        blk = blks[d]
        tb = pl.BlockSpec((CB, D), lambda i, blk=blk: (blk(i), 0))
        gls = pl.BlockSpec((NH, 1, 128), lambda i, blk=blk: (blk(i), 0, 0))
        in_specs += [tb] * 5 + [gls, pl.BlockSpec((1, NH, HD, HD), lambda i, blk=blk: (blk(i), 0, 0, 0)),
                                pl.BlockSpec((CB, D), lambda i, blk=blk: (jnp.minimum(blk(i), nlat_b - 1), 0))]
        out_specs += [tb] * 5 + [gls]
        args += list(per_dir[d]) + [do]
    outs = pl.pallas_call(
        body, grid=(nb,), name="dn2_bwd", in_specs=in_specs, out_specs=out_specs,
        out_shape=([jax.ShapeDtypeStruct((T, D), F32)] * 5 + [jax.ShapeDtypeStruct((nb * NH, 1, 128), F32)]) * 2,
        scratch_shapes=[pltpu.VMEM((2, NH, HD, HD), F32)], compiler_params=_cp(),
    )(*args)
    return [tuple(outs[:6]), tuple(outs[6:])]


def _ghn_fn(o, gt, w):
    y = o * lax.rsqrt(jnp.mean(o * o, axis=-1, keepdims=True) + EPS)
    return (y * w) * jax.nn.silu(gt)


def _ghn_fwd(o_f, o_b, p, w, w_branch, nlat):
    tb = _tile(nlat, (256, 128))

    def body(of_ref, ob_ref, gt_ref, w_ref, wb_ref, y_ref, z_ref):
        for h in range(NH):
            sl = slice(h * HD, (h + 1) * HD)
            y_ref[:, sl] = _ghn_fn(of_ref[:, sl] + ob_ref[:, sl], gt_ref[:, sl], w_ref[...]).astype(BF)
        z_ref[...] = jnp.dot(y_ref[...], wb_ref[...], preferred_element_type=F32)

    row = pl.BlockSpec((tb, D), lambda i: (i, 0))
    return pl.pallas_call(
        body, grid=(nlat // tb,), name="ghn_fwd",
        in_specs=[row, row, pl.BlockSpec((tb, D), lambda i: (i, O_GT // D)), pl.BlockSpec((1, HD), lambda i: (0, 0)), _resident((D, D))],
        out_specs=[row, row], out_shape=[jax.ShapeDtypeStruct((nlat, D), BF), jax.ShapeDtypeStruct((nlat, D), F32)],
    )(o_f, o_b, p, w, w_branch)


def _ghn_bwd(o_f, o_b, p, w, dy, nlat):
    T = p.shape[0]
    tb = _tile(nlat, (256, 128))
    nlb = nlat // tb

    def body(of_ref, ob_ref, gt_ref, w_ref, dy_ref, do_ref, dgt_ref, dw_ref):
        is_lat = pl.program_id(0) < nlb

        @pl.when(pl.program_id(0) == 0)
        def _():
            dw_ref[...] = jnp.zeros_like(dw_ref)
        for h in range(NH):
            sl = slice(h * HD, (h + 1) * HD)
            _, vjp = jax.vjp(_ghn_fn, of_ref[:, sl] + ob_ref[:, sl], gt_ref[:, sl], w_ref[...])
            do, dgt, dw = vjp(dy_ref[:, sl])
            do_ref[:, sl] = do
            dgt_ref[:, sl] = jnp.where(is_lat, dgt, 0.0).astype(BF)
            dw_ref[...] += jnp.where(is_lat, dw, 0.0)

    lat = lambda i: jnp.minimum(i, nlb - 1)
    row = pl.BlockSpec((tb, D), lambda i: (lat(i), 0))
    one = pl.BlockSpec((1, HD), lambda i: (0, 0))
    return pl.pallas_call(
        body, grid=(T // tb,), name="ghn_bwd",
        in_specs=[row, row, pl.BlockSpec((tb, D), lambda i: (lat(i), O_GT // D)), one, row],
        out_specs=[row, pl.BlockSpec((tb, D), lambda i: (i, 0)), one],
        out_shape=[jax.ShapeDtypeStruct((nlat, D), F32), jax.ShapeDtypeStruct((T, D), BF), jax.ShapeDtypeStruct((1, HD), F32)],
    )(o_f, o_b, p, w, dy)


@jax.custom_vjp
def _swap32(x):
    lane = lax.broadcasted_iota(jnp.int32, x.shape, 1)
    return jnp.where((lane & 32) == 0, pltpu.roll(x, 96, 1), pltpu.roll(x, 32, 1))


_swap32.defvjp(lambda x: (_swap32(x), None), lambda _, g: (_swap32(g),))


def _qk_post_fn(x, w, cos, sin):
    y = (x * lax.rsqrt(jnp.mean(x * x, axis=-1, keepdims=True) + EPS)) * w
    return y * cos + _swap32(y) * sin


def _attn_prep_fwd(p, qn, kn, cos, sin):
    T = p.shape[0]
    tb = _tile(T, (256, 128))

    def body(q_ref, k_ref, v_ref, qn_ref, kn_ref, cos_ref, sin_ref, qr_ref, kr_ref, vb_ref):
        cos_v, sin_v = cos_ref[...], sin_ref[...]
        for h in range(NH):
            sl = slice(h * HD, (h + 1) * HD)
            qr_ref[:, sl] = _qk_post_fn(q_ref[:, sl], qn_ref[...], cos_v, sin_v).astype(BF)
        for h in range(KVH):
            sl = slice(h * HD, (h + 1) * HD)
            kr_ref[:, sl] = _qk_post_fn(k_ref[:, sl], kn_ref[...], cos_v, sin_v).astype(BF)
        vb_ref[...] = v_ref[...].astype(BF)

    one = pl.BlockSpec((1, HD), lambda i: (0, 0))
    tab = pl.BlockSpec((tb, HD), lambda i: (i, 0))
    return pl.pallas_call(
        body, grid=(T // tb,), name="attn_prep_fwd",
        in_specs=[pl.BlockSpec((tb, D), lambda i: (i, O_Q // D)), pl.BlockSpec((tb, KV), lambda i: (i, O_K // KV)),
                  pl.BlockSpec((tb, KV), lambda i: (i, O_V // KV)), one, one, tab, tab],
        out_specs=[pl.BlockSpec((tb, D), lambda i: (i, 0)), pl.BlockSpec((tb, KV), lambda i: (i, 0)),
                   pl.BlockSpec((tb, KV), lambda i: (i, 0))],
        out_shape=[jax.ShapeDtypeStruct((T, D), BF), jax.ShapeDtypeStruct((T, KV), BF), jax.ShapeDtypeStruct((T, KV), BF)],
    )(p, p, p, qn, kn, cos, sin)


def _attn_prep_bwd(p, qn, kn, cos, sin, dqr, dkp, dvp, dkc, dvc, nlat):
    T = p.shape[0]
    nqb = nlat // CB
    ncb = (T - nlat) // CB

    def body(q_ref, k_ref, v_ref, qn_ref, kn_ref, cos_ref, sin_ref, dqr_ref, dka_ref, dkb_ref, dkc3_ref, dva_ref, dvb_ref, dvc3_ref,
             dkctx_ref, dvctx_ref, dq_ref, dk_ref, dv_ref, dqn_ref, dkn_ref):
        i = pl.program_id(0)
        is_lat = i < nqb
        cos_v, sin_v = cos_ref[...], sin_ref[...]

        @pl.when(i == 0)
        def _():
            dqn_ref[...] = jnp.zeros_like(dqn_ref)
            dkn_ref[...] = jnp.zeros_like(dkn_ref)

        def band_sum(a_ref, b_ref, c_ref, ctx_ref):
            s = b_ref[0] + jnp.where(i > 0, a_ref[0], 0.0) + jnp.where(i < nqb - 1, c_ref[0], 0.0)
            return jnp.where(is_lat, s, ctx_ref[...])

        dkr = band_sum(dka_ref, dkb_ref, dkc3_ref, dkctx_ref)
        dv_ref[...] = band_sum(dva_ref, dvb_ref, dvc3_ref, dvctx_ref).astype(BF)
        for h in range(NH):
            sl = slice(h * HD, (h + 1) * HD)
            _, vjp = jax.vjp(_qk_post_fn, q_ref[:, sl], qn_ref[...], cos_v, sin_v)
            dq, dqn, _, _ = vjp(jnp.where(is_lat, dqr_ref[:, sl], 0.0))
            dq_ref[:, sl] = dq.astype(BF)
            dqn_ref[...] += dqn
        for h in range(KVH):
            sl = slice(h * HD, (h + 1) * HD)
            _, vjp = jax.vjp(_qk_post_fn, k_ref[:, sl], kn_ref[...], cos_v, sin_v)
            dk, dkn, _, _ = vjp(dkr[:, sl])
            dk_ref[:, sl] = dk.astype(BF)
            dkn_ref[...] += dkn

    one = pl.BlockSpec((1, HD), lambda i: (0, 0))
    tab = pl.BlockSpec((CB, HD), lambda i: (i, 0))
    lat = lambda i: jnp.minimum(i, nqb - 1)

    def part(off, slot):
        return pl.BlockSpec((1, CB, KV), lambda i: (jnp.clip(lat(i) + off, 0, nqb - 1) * 3 + slot, 0, 0))

    ctxs = pl.BlockSpec((CB, KV), lambda i: (jnp.clip(i - nqb, 0, ncb - 1), 0))
    kvs = pl.BlockSpec((CB, KV), lambda i: (i, 0))
    return pl.pallas_call(
        body, grid=(T // CB,), name="attn_prep_bwd",
        in_specs=[pl.BlockSpec((CB, D), lambda i: (i, O_Q // D)), pl.BlockSpec((CB, KV), lambda i: (i, O_K // KV)),
                  pl.BlockSpec((CB, KV), lambda i: (i, O_V // KV)), one, one, tab, tab,
                  pl.BlockSpec((CB, D), lambda i: (lat(i), 0)),
                  part(-1, 2), part(0, 1), part(1, 0), part(-1, 2), part(0, 1), part(1, 0), ctxs, ctxs],
        out_specs=[pl.BlockSpec((CB, D), lambda i: (i, 0)), kvs, kvs, one, one],
        out_shape=[jax.ShapeDtypeStruct((T, D), BF), jax.ShapeDtypeStruct((T, KV), BF), jax.ShapeDtypeStruct((T, KV), BF),
                   jax.ShapeDtypeStruct((1, HD), F32), jax.ShapeDtypeStruct((1, HD), F32)],
    )(p, p, p, qn, kn, cos, sin, dqr, dkp, dkp, dkp, dvp, dvp, dvp, dkc, dvc)


def _attn_groups_fn(qs, kalls, valls, sinks, bias):
    groups = range(KVH)
    q = [jnp.concatenate(qs[GRP * g:GRP * (g + 1)], axis=0) for g in groups]
    s = [_dot_nt_bf(q[g], kalls[g]) * (HD ** -0.5) + bias for g in groups]
    sk = [jnp.concatenate([jnp.broadcast_to(jnp.mean(t, axis=1, keepdims=True), (CB, 1)) for t in sinks[GRP * g:GRP * (g + 1)]],
                          axis=0) for g in groups]
    m = [lax.stop_gradient(jnp.maximum(jnp.max(s[g], axis=1, keepdims=True), sk[g])) for g in groups]
    e = [jnp.exp(s[g] - m[g]) for g in groups]
    den = [jnp.sum(e[g], axis=1, keepdims=True) + jnp.exp(sk[g] - m[g]) for g in groups]
    return [_dot_bf(e[g] / den[g], valls[g]) for g in groups]


def _attn_bias(lc):
    r, c = _iota2((GRP * CB, 3 * CB + lc))
    rel = c - (r & (CB - 1))
    win = (rel >= 0) & (rel <= 2 * CB)
    ctx = c >= 3 * CB
    seen = [(win & (c >= CB)) | ctx, win | ctx, (win & (c < 2 * CB)) | ctx]
    return jnp.stack([jnp.where(s, 0.0, -1e30) for s in seen]).astype(F32)


def _attn_specs(nqb, lc, nlat):
    assert nqb >= 2
    qs = pl.BlockSpec((CB, D), lambda i: (i, 0))
    ka = pl.BlockSpec((CB, KV), lambda i: (jnp.maximum(i - 1, 0), 0))
    kb = pl.BlockSpec((CB, KV), lambda i: (i, 0))
    kc = pl.BlockSpec((CB, KV), lambda i: (jnp.minimum(i + 1, nqb - 1), 0))
    kx = pl.BlockSpec((lc, KV), lambda i: (nlat // lc, 0))
    sk = pl.BlockSpec((KVH, 8, 128), lambda i: (0, 0, 0))
    bs = pl.BlockSpec((1, GRP * CB, 3 * CB + lc), lambda i: (jnp.where(i == 0, 0, jnp.where(i == nqb - 1, 2, 1)), 0, 0))
    return qs, ka, kb, kc, kx, sk, bs


def _attn_operands(q_ref, k_refs, v_refs, sk_ref, dtype):
    sls = [slice(g * HD, (g + 1) * HD) for g in range(KVH)]
    kalls = [jnp.concatenate([r[:, sl] for r in k_refs], axis=0).astype(dtype) for sl in sls]
    valls = [jnp.concatenate([r[:, sl] for r in v_refs], axis=0).astype(dtype) for sl in sls]
    qs = [q_ref[:, sl].astype(dtype) for sl in _HEAD_SLICES]
    sinks = [sk_ref[h // GRP, (h % GRP):(h % GRP) + 1, :] for h in range(NH)]
    return qs, kalls, valls, sinks


def _attn_fwd(qr, kr, vb, sink, w_branch, nlat):
    lc = kr.shape[0] - nlat
    nqb = nlat // CB
    qs, ka, kb, kc, kx, sk, bs = _attn_specs(nqb, lc, nlat)

    def body(q_ref, ka_ref, kb_ref, kc_ref, kx_ref, va_ref, vb_ref, vc_ref, vx_ref, sk_ref, bias_ref, wb_ref, o_ref, z_ref):
        operands = _attn_operands(q_ref, (ka_ref, kb_ref, kc_ref, kx_ref), (va_ref, vb_ref, vc_ref, vx_ref), sk_ref, BF)
        outs = _attn_groups_fn(*operands, bias_ref[0])
        for h, sl in enumerate(_HEAD_SLICES):
            o_ref[:, sl] = outs[h // GRP][(h % GRP) * CB:(h % GRP + 1) * CB].astype(BF)
        z_ref[...] = jnp.dot(o_ref[...], wb_ref[...], preferred_element_type=F32)

    return pl.pallas_call(
        body, grid=(nqb,), name="attn_fwd",
        in_specs=[qs, ka, kb, kc, kx, ka, kb, kc, kx, sk, bs, _resident((D, D))], out_specs=[qs, qs],
        out_shape=[jax.ShapeDtypeStruct((nlat, D), BF), jax.ShapeDtypeStruct((nlat, D), F32)], compiler_params=_cp(),
    )(qr, kr, kr, kr, kr, vb, vb, vb, vb, sink, _attn_bias(lc), w_branch)


def _attn_bwd(qr, kr, vb, sink, dy, nlat):
    lc = kr.shape[0] - nlat
    nqb = nlat // CB
    qs, ka, kb, kc, kx, sk, bs = _attn_specs(nqb, lc, nlat)

    def body(q_ref, ka_ref, kb_ref, kc_ref, kx_ref, va_ref, vb_ref, vc_ref, vx_ref, sk_ref, dy_ref, bias_ref,
             dq_ref, dkp_ref, dvp_ref, dkx_ref, dvx_ref, dsk_ref):
        operands = _attn_operands(q_ref, (ka_ref, kb_ref, kc_ref, kx_ref), (va_ref, vb_ref, vc_ref, vx_ref), sk_ref, F32)
        _, vjp = jax.vjp(functools.partial(_attn_groups_fn, bias=bias_ref[0]), *operands)
        dys_g = [jnp.concatenate([dy_ref[:, sl] for sl in _HEAD_SLICES[GRP * g:GRP * (g + 1)]], axis=0) for g in range(KVH)]
        dqs, dks, dvs, dsinks = vjp(dys_g)

        @pl.when(pl.program_id(0) == 0)
        def _():
            dkx_ref[...] = jnp.zeros_like(dkx_ref)
            dvx_ref[...] = jnp.zeros_like(dvx_ref)
            dsk_ref[...] = jnp.zeros_like(dsk_ref)

        for h, sl in enumerate(_HEAD_SLICES):
            dq_ref[:, sl] = dqs[h]
            dsk_ref[h // GRP, (h % GRP):(h % GRP) + 1, :] += dsinks[h]
        for g in range(KVH):
            sl = slice(g * HD, (g + 1) * HD)
            for t in range(3):
                dkp_ref[t, :, sl] = dks[g][t * CB:(t + 1) * CB]
                dvp_ref[t, :, sl] = dvs[g][t * CB:(t + 1) * CB]
            dkx_ref[:, sl] += dks[g][3 * CB:]
            dvx_ref[:, sl] += dvs[g][3 * CB:]

    dys = qs
    parts = pl.BlockSpec((3, CB, KV), lambda i: (i, 0, 0))
    ctxo = pl.BlockSpec((lc, KV), lambda i: (0, 0))
    return pl.pallas_call(
        body, grid=(nqb,), name="attn_bwd",
        in_specs=[qs, ka, kb, kc, kx, ka, kb, kc, kx, sk, dys, bs],
        out_specs=[dys, parts, parts, ctxo, ctxo, sk],
        out_shape=[jax.ShapeDtypeStruct((nlat, D), F32), jax.ShapeDtypeStruct((3 * nqb, CB, KV), F32),
                   jax.ShapeDtypeStruct((3 * nqb, CB, KV), F32), jax.ShapeDtypeStruct((lc, KV), F32),
                   jax.ShapeDtypeStruct((lc, KV), F32), jax.ShapeDtypeStruct((KVH, 8, 128), F32)],
        compiler_params=_cp(),
    )(qr, kr, kr, kr, kr, vb, vb, vb, vb, sink, dy, _attn_bias(lc))


def _merge_fn(z_dn, z_at, g_dn, g_at):
    return jax.nn.sigmoid(g_dn) * z_dn + jax.nn.sigmoid(g_at) * z_at


def _merge_fwd(z_dn, z_at, p, w_out, nlat):
    tb = _tile(nlat, (256, 128))

    def body(zd_ref, za_ref, gd_ref, ga_ref, wo_ref, o_ref, mix_ref):
        o_ref[...] = _merge_fn(zd_ref[...], za_ref[...], gd_ref[...], ga_ref[...]).astype(BF)
        mix_ref[...] = jnp.dot(o_ref[...], wo_ref[...], preferred_element_type=F32)

    row = pl.BlockSpec((tb, D), lambda i: (i, 0))
    return pl.pallas_call(
        body, grid=(nlat // tb,), name="merge_fwd",
        in_specs=[row, row, pl.BlockSpec((tb, D), lambda i: (i, O_MG // D)), pl.BlockSpec((tb, D), lambda i: (i, O_MG // D + 1)),
                  _resident((D, D))],
        out_specs=[row, row], out_shape=[jax.ShapeDtypeStruct((nlat, D), BF), jax.ShapeDtypeStruct((nlat, D), F32)],
    )(z_dn, z_at, p, p, w_out)


def _merge_bwd(z_dn, z_at, p, dm, w_bdn, w_bat, nlat):
    T = p.shape[0]
    tb = _tile(nlat, (256, 128))
    nlb = nlat // tb

    def body(zd_ref, za_ref, gd_ref, ga_ref, dm_ref, wd_ref, wa_ref, dzd_ref, dza_ref, dg_ref, dyd_ref, dya_ref):
        is_lat = pl.program_id(0) < nlb
        _, vjp = jax.vjp(_merge_fn, zd_ref[...], za_ref[...], gd_ref[...], ga_ref[...])
        dzd, dza, dgd, dga = vjp(dm_ref[...])
        dzd_ref[...] = dzd.astype(BF)
        dza_ref[...] = dza.astype(BF)
        dg_ref[:, :D] = jnp.where(is_lat, dgd, 0.0).astype(BF)
        dg_ref[:, D:] = jnp.where(is_lat, dga, 0.0).astype(BF)
        dyd_ref[...] = lax.dot_general(dzd_ref[...], wd_ref[...], (_DIMS["nt"], ((), ())), preferred_element_type=F32)
        dya_ref[...] = lax.dot_general(dza_ref[...], wa_ref[...], (_DIMS["nt"], ((), ())), preferred_element_type=F32)

    lat = lambda i: jnp.minimum(i, nlb - 1)
    row = pl.BlockSpec((tb, D), lambda i: (lat(i), 0))
    return pl.pallas_call(
        body, grid=(T // tb,), name="merge_bwd",
        in_specs=[row, row, pl.BlockSpec((tb, D), lambda i: (lat(i), O_MG // D)),
                  pl.BlockSpec((tb, D), lambda i: (lat(i), O_MG // D + 1)), row, _resident((D, D)), _resident((D, D))],
        out_specs=[row, row, pl.BlockSpec((tb, 2 * D), lambda i: (i, 0)), row, row],
        out_shape=[jax.ShapeDtypeStruct((nlat, D), BF), jax.ShapeDtypeStruct((nlat, D), BF), jax.ShapeDtypeStruct((T, 2 * D), BF),
                   jax.ShapeDtypeStruct((nlat, D), F32), jax.ShapeDtypeStruct((nlat, D), F32)],
    )(z_dn, z_at, p, p, dm, w_bdn, w_bat)


def _swiglu_fn(ug, uv):
    return jax.nn.silu(ug) * uv


FFN_GROUP = 256


def _resident(shape):
    return pl.BlockSpec(shape, lambda i: (0,) * len(shape), pipeline_mode=pl.Buffered(1))


def _ffn_mid_fwd(u, w8, bias, w_down):
    n = u.shape[0]
    tb = _tile(n, (256, 128))
    starts, ends = _segment_edges((n,), tb)

    def body(cur_ref, prev_ref, next_ref, w_ref, b_ref, wd_ref, o_ref, ff_ref):
        keep = _keep_halos(pl.program_id(0), starts, ends)
        for c0 in range(0, DFF, FFN_GROUP):
            halves = []
            for cols in (slice(c0, c0 + FFN_GROUP), slice(DFF + c0, DFF + c0 + FFN_GROUP)):
                xe = _ext_rows((cur_ref, prev_ref, next_ref), cols, keep)
                halves.append(_conv_rows(_shifted_rows(xe, FFN_TAPS), w_ref, cols)[HALO:HALO + tb] + b_ref[:, cols])
            o_ref[:, c0:c0 + FFN_GROUP] = _swiglu_fn(*halves).astype(BF)
        ff_ref[...] = jnp.dot(o_ref[...], wd_ref[...], preferred_element_type=F32)

    return pl.pallas_call(
        body, grid=(n // tb,), name="ffn_mid_fwd",
        in_specs=_halo_specs(tb, 2 * DFF, n) + [pl.BlockSpec((8, 2 * DFF), lambda i: (0, 0)), pl.BlockSpec((1, 2 * DFF), lambda i: (0, 0)),
                                               _resident((DFF, D))],
        out_specs=[pl.BlockSpec((tb, DFF), lambda i: (i, 0)), pl.BlockSpec((tb, D), lambda i: (i, 0))],
        out_shape=[jax.ShapeDtypeStruct((n, DFF), BF), jax.ShapeDtypeStruct((n, D), F32)],
        compiler_params=_cp(),
    )(u, u, u, w8, bias, w_down)


def _ffn_mid_bwd(u, w8, bias, da, w_up):
    n = u.shape[0]
    tb = _tile(n, (256, 128))
    starts, ends = _segment_edges((n,), tb)

    def body(cur_ref, prev_ref, next_ref, w_ref, b_ref, da_c, da_p, da_n, wu_ref, du_ref, dw_ref, db_ref, dh_ref):
        i = pl.program_id(0)
        keep = _keep_halos(i, starts, ends)

        @pl.when(i == 0)
        def _():
            dw_ref[...] = jnp.zeros_like(dw_ref)
            db_ref[...] = jnp.zeros_like(db_ref)

        for c0 in range(0, DFF, FFN_GROUP):
            col_pair = (slice(c0, c0 + FFN_GROUP), slice(DFF + c0, DFF + c0 + FFN_GROUP))
            shifts = [_shifted_rows(_ext_rows((cur_ref, prev_ref, next_ref), cols, keep), FFN_TAPS) for cols in col_pair]
            convs = [_conv_rows(shifted, w_ref, cols) + b_ref[:, cols] for shifted, cols in zip(shifts, col_pair)]
            dae = _ext_rows((da_c, da_p, da_n), col_pair[0], keep)
            _, vjp = jax.vjp(_swiglu_fn, *convs)
            for shifted, cols, dce in zip(shifts, col_pair, vjp(dae)):
                du_ref[:, cols] = _conv_rows(_shifted_rows(dce, FFN_TAPS, transpose=True), w_ref, cols)[HALO:HALO + tb].astype(BF)
                dcur = dce[HALO:HALO + tb]
                for j, g in enumerate(_tap_grads(dcur, shifted, tb)):
                    dw_ref[j:j + 1, cols] += g
                db_ref[:, cols] += jnp.sum(dcur, axis=0, keepdims=True)
        dh_ref[...] = lax.dot_general(du_ref[...], wu_ref[...], (_DIMS["nt"], ((), ())), preferred_element_type=F32)

    wspec = pl.BlockSpec((8, 2 * DFF), lambda i: (0, 0))
    bspec = pl.BlockSpec((1, 2 * DFF), lambda i: (0, 0))
    return pl.pallas_call(
        body, grid=(n // tb,), name="ffn_mid_bwd",
        in_specs=_halo_specs(tb, 2 * DFF, n) + [wspec, bspec] + _halo_specs(tb, DFF, n) + [_resident((D, 2 * DFF))],
        out_specs=[pl.BlockSpec((tb, 2 * DFF), lambda i: (i, 0)), wspec, bspec, pl.BlockSpec((tb, D), lambda i: (i, 0))],
        out_shape=[jax.ShapeDtypeStruct((n, 2 * DFF), BF), jax.ShapeDtypeStruct((8, 2 * DFF), F32), jax.ShapeDtypeStruct((1, 2 * DFF), F32),
                   jax.ShapeDtypeStruct((n, D), F32)],
        compiler_params=_cp(),
    )(u, u, u, w8, bias, da, da, da, w_up)


def _loss_kernel(x1, gate, ff, target, w_down):
    n = x1.shape[0]
    tb = _tile(n, (256, 128))

    def body(x_ref, g_ref, f_ref, t_ref, wd_ref, loss_ref, dy_ref, dff_ref, dg_ref, da_ref):
        err = x_ref[...] + g_ref[...] * f_ref[...] - t_ref[...]
        dy = err * (1.0 / D)
        dy_ref[...] = dy
        dff_ref[...] = (g_ref[...] * dy).astype(BF)
        da_ref[...] = lax.dot_general(dff_ref[...], wd_ref[...], (_DIMS["nt"], ((), ())), preferred_element_type=F32)

        @pl.when(pl.program_id(0) == 0)
        def _():
            loss_ref[...] = jnp.zeros_like(loss_ref)
            dg_ref[...] = jnp.zeros_like(dg_ref)
        part = 0.5 * jnp.sum(jnp.sum(err * err, axis=1, keepdims=True) * (1.0 / D), axis=0, keepdims=True)
        loss_ref[...] += jnp.broadcast_to(part, (1, 128))
        dg_ref[...] += jnp.sum(dy * f_ref[...], axis=0, keepdims=True)

    row = pl.BlockSpec((tb, D), lambda i: (i, 0))
    one = pl.BlockSpec((1, D), lambda i: (0, 0))
    return pl.pallas_call(
        body, grid=(n // tb,), name="loss",
        in_specs=[row, one, row, row, _resident((DFF, D))],
        out_specs=[pl.BlockSpec((1, 128), lambda i: (0, 0)), row, row, one, pl.BlockSpec((tb, DFF), lambda i: (i, 0))],
        out_shape=[jax.ShapeDtypeStruct((1, 128), F32), jax.ShapeDtypeStruct((n, D), F32),
                   jax.ShapeDtypeStruct((n, D), BF), jax.ShapeDtypeStruct((1, D), F32), jax.ShapeDtypeStruct((n, DFF), F32)],
        compiler_params=_cp(),
    )(x1, gate, ff, target, w_down)


def _rope_tables(nlat, lc):
    t = jnp.arange(nlat)
    row = (t // GRID_W).astype(F32)
    col = (t % GRID_W).astype(F32)
    inv_freq = ROPE_BASE ** (-jnp.arange(32, dtype=F32) / 32)
    ar, ac = row[:, None] * inv_freq, col[:, None] * inv_freq
    cos = jnp.concatenate([jnp.cos(ar), jnp.cos(ar), jnp.cos(ac), jnp.cos(ac)], axis=1)
    sin = jnp.concatenate([-jnp.sin(ar), jnp.sin(ar), -jnp.sin(ac), jnp.sin(ac)], axis=1)
    cos = jnp.concatenate([cos, jnp.ones((lc, HD), F32)], axis=0)
    sin = jnp.concatenate([sin, jnp.zeros((lc, HD), F32)], axis=0)
    return cos, sin


def _pad_rows8(w):
    return jnp.concatenate([w, jnp.zeros((8 - w.shape[0], w.shape[1]), w.dtype)], axis=0)


def _pack_w_in(w):
    cuts = [sum(IN_SIZES[:i]) for i in range(len(IN_SIZES) + 1)]
    qkv, gt, b, a, q, k, v, mg = [w[:, cuts[i]:cuts[i + 1]] for i in range(len(IN_SIZES))]
    return jnp.concatenate([qkv, gt, q, mg, k, v, b, a, jnp.zeros((w.shape[0], PW - O_BA - 32), w.dtype)], axis=1)


def _unpack_w_in(g):
    return jnp.concatenate([g[:, O_QKV:O_GT], g[:, O_GT:O_Q], g[:, O_BA:O_BA + 32], g[:, O_Q:O_MG], g[:, O_K:O_V],
                            g[:, O_V:O_BA], g[:, O_MG:O_K]], axis=1)


def _local_step(x, ctx, mod_x, mod_c, target, project_in, project_back,
                norm_mix, norm_ffn, dn_conv, a_log, dt_bias, dn_norm, q_norm, k_norm, sink, ffn_conv, ffn_conv_b):
    L, LC = x.shape[0], ctx.shape[0]
    T = L + LC
    seg = lambda r: jnp.stack([mod_x[r], mod_c[r]])[:, None, :]
    sh_a, sc_a = seg(0), seg(1)
    g_a, g_f = mod_x[2][None], mod_x[5][None]
    sh_f, sc_f = mod_x[3][None], mod_x[4][None]
    cos, sin = _rope_tables(L, LC)
    dnc8 = _pad_rows8(dn_conv)
    ffc8 = _pad_rows8(ffn_conv)
    gate_row = lambda a: jnp.concatenate([jnp.zeros((1, 16), F32), a.reshape(1, 16), jnp.zeros((1, 96), F32)], axis=1)
    alog_row, dt_row = gate_row(a_log), gate_row(dt_bias)
    sinkb = jnp.concatenate([jnp.broadcast_to(sink.reshape(KVH, GRP, 1), (KVH, GRP, 128)), jnp.zeros((KVH, 8 - GRP, 128), F32)], axis=1)

    h1 = _norm_mod_fwd(x, ctx, norm_mix, sh_a, sc_a, "norm_mix_fwd")
    p, (w_in_p, w_bdn, w_bat, w_out, w_up, w_down) = project_in(h1)
    q, k, v, gb = _dn_pre_fwd(p, dnc8, alog_row, dt_row, (L, LC))
    wy = _dn1_fwd(q, k, v, gb)
    scans = _dn2_fwd([t[:6] for t in wy], L)
    o_dir = [s[0] for s in scans]
    y_dn, z_dn = _ghn_fwd(o_dir[0], o_dir[1], p, dn_norm, w_bdn, L)
    qr, kr, vb = _attn_prep_fwd(p, q_norm, k_norm, cos, sin)
    y_at, z_at = _attn_fwd(qr, kr, vb, sinkb, w_bat, L)
    merged, mix = _merge_fwd(z_dn, z_at, p, w_out, L)
    x1, h2 = _resid_norm_fwd(x, g_a, mix, norm_ffn, sh_f, sc_f)
    u_raw = _mm(h2, w_up, form="nn", out_dtype=F32, name="ffn_up")
    act, ff = _ffn_mid_fwd(u_raw, ffc8, ffn_conv_b, w_down)
    loss_row, dy, dff, dg_f, dact = _loss_kernel(x1, g_f, ff, target, w_down)

    g_down = _mm(act, dff, form="tn", out_dtype=BF, name="g_ffn_down")
    du_raw, g_ffc8, g_ffb, dh2 = _ffn_mid_bwd(u_raw, ffc8, ffn_conv_b, dact, w_up)
    g_up = _mm(h2, du_raw, form="tn", out_dtype=BF, name="g_ffn_up")
    dx1, dmix, dg_a, g_nffn, dsh_f, dsc_f, dmerged = _resid_norm_bwd(x1, g_a, mix, norm_ffn, sh_f, sc_f, dh2, dy, w_out)

    g_out = _mm(merged, dmix, form="tn", out_dtype=BF, name="g_w_out")
    dz_dn, dz_at, dmg, dy_dn, dy_at = _merge_bwd(z_dn, z_at, p, dmerged, w_bdn, w_bat, L)
    g_bdn = _mm(y_dn, dz_dn, form="tn", out_dtype=BF, name="g_branch_dn")
    g_bat = _mm(y_at, dz_at, form="tn", out_dtype=BF, name="g_branch_at")
    dqr, dkp, dvp, dkx, dvx, dsink = _attn_bwd(qr, kr, vb, sinkb, dy_at, L)
    dq_raw, dk_raw, dv_raw, g_qn, g_kn = _attn_prep_bwd(p, q_norm, k_norm, cos, sin, dqr, dkp, dvp, dkx, dvx, L)
    do, dgt, g_dnn = _ghn_bwd(o_dir[0], o_dir[1], p, dn_norm, dy_dn, L)
    cots = _dn2_bwd([wy[d][:6] + (scans[d][1],) for d in (0, 1)], do, L)
    dq, dk, dv, dgb = _dn1_bwd(q, k, v, gb, [t[6] for t in wy], cots)
    dqkv_raw, dba, g_dnc8, g_alog, g_dt = _dn_pre_bwd(p, dnc8, alog_row, dt_row, dq, dk, dv, dgb, (L, LC))
    dp = jnp.concatenate([dqkv_raw, dgt, dq_raw, dmg, dk_raw, dv_raw, dba, jnp.zeros((T, PW - O_BA - 128), BF)], axis=1)
    big, dh1 = project_back(h1, dp, w_in_p, (g_bdn, g_bat, g_out, g_up, g_down))
    grad_x, g_nmix_x, dsh_a, dsc_a = _norm_mod_bwd(x, norm_mix, mod_x[0][None], mod_x[1][None], dh1, row0=0,
                                                   name="norm_mix_bwd", residual=dx1)
    g_nmix_c, dsh_c, dsc_c = _norm_mod_bwd(ctx, norm_mix, mod_c[0][None], mod_c[1][None], dh1, row0=L, name="norm_mix_bwd_ctx")
    g_nmix = g_nmix_x + g_nmix_c

    zero = jnp.zeros((D,), F32)
    dmod_x = jnp.stack([dsh_a[0], dsc_a[0], dg_a[0], dsh_f[0], dsc_f[0], dg_f[0]])
    dmod_c = jnp.stack([dsh_c[0], dsc_c[0], zero, zero, zero, zero])
    small = dict(
        dmod_x=dmod_x, dmod_c=dmod_c, norm_mix=g_nmix, norm_ffn=g_nffn, dn_conv=g_dnc8[:5], dn_a_log=g_alog[0, 16:32].reshape(2, 8),
        dn_dt_bias=g_dt[0, 16:32].reshape(2, 8), dn_norm=g_dnn, q_norm=g_qn, k_norm=g_kn,
        attn_sink=jnp.sum(dsink[:, :GRP, :], axis=2).reshape(1, NH), ffn_conv=g_ffc8[:3], ffn_conv_b=g_ffb)
    return loss_row[0, 0], grad_x, big, small


def _exchange(arrays, scatter, name):
    n = len(arrays)

    def body(*refs):
        args = (refs[:n], refs[n:2 * n], *refs[2 * n:], scatter)
        _exchange_start(*args)
        _exchange_wait(*args)

    hbm = pl.BlockSpec(memory_space=pl.ANY)
    out_shape, sems = _exchange_shapes(arrays, scatter)
    return pl.pallas_call(body, name=name, in_specs=[hbm] * n, out_specs=[hbm] * n, out_shape=out_shape,
                          scratch_shapes=sems)(*arrays)


def _ada_fwd(c16, w_ada, b_ada):
    def body(c_ref, w_ref, b_ref, o_ref):
        o_ref[...] = _dot_hi(jax.nn.silu(c_ref[...]), w_ref[...]) + b_ref[...]

    return pl.pallas_call(body, name="ada_fwd", out_shape=jax.ShapeDtypeStruct((16, w_ada.shape[1]), F32))(c16, w_ada, b_ada)


def _ada_bwd(c16, w_ada, dmx, dmc):
    def body(c_ref, w_ref, dmx_ref, dmc_ref, gw_ref, pc_ref):
        dmc_tot = dmc_ref[0:1, :]
        for d in range(1, N_DEV):
            dmc_tot = dmc_tot + dmc_ref[d:d + 1, :]
        dm16 = jnp.concatenate([dmx_ref[...], jnp.broadcast_to(dmc_tot, (8, dmc_tot.shape[1]))], axis=0)
        row = lax.broadcasted_iota(jnp.int32, dm16.shape, 0)
        dm16 = jnp.where(row <= 8, dm16, 0.0)
        s = jax.nn.silu(c_ref[...])
        gw_ref[...] = lax.dot_general(s, dm16, (_DIMS["tn"], ((), ())), precision=HI, preferred_element_type=F32)
        pc = lax.dot_general(dm16, w_ref[...], (_DIMS["nt"], ((), ())), precision=HI, preferred_element_type=F32)
        pc_ref[...] = pc[8:9, :]

    return pl.pallas_call(body, name="ada_bwd", out_shape=[jax.ShapeDtypeStruct(w_ada.shape, F32), jax.ShapeDtypeStruct((1, D), F32)],
                          compiler_params=_cp())(c16, w_ada, dmx, dmc)


def _cctx_grad(pc_all, c_ctx_row):
    def body(pc_ref, c_ref, g_ref):
        tot = pc_ref[0]
        for d in range(1, N_DEV):
            tot = tot + pc_ref[d]
        _, vjp = jax.vjp(jax.nn.silu, c_ref[...])
        g_ref[...] = vjp(tot)[0]

    return pl.pallas_call(body, name="cctx_grad", out_shape=jax.ShapeDtypeStruct((1, D), F32))(pc_all, c_ctx_row)


def _adamw(parts, w, m, v, name):
    ns, R, C = parts.shape
    tb = _tile(R, (128, 64, 32, 16, 8))

    def body(p_ref, w_ref, m_ref, v_ref, g_ref, d_ref, mo_ref, vo_ref):
        g = p_ref[0].astype(F32)
        for s in range(1, ns):
            g = g + p_ref[s].astype(F32)
        m2 = ADAM_B1 * m_ref[...] + (1.0 - ADAM_B1) * g
        v2 = ADAM_B2 * v_ref[...] + (1.0 - ADAM_B2) * jnp.square(g)
        m_hat = m2 / (1.0 - ADAM_B1 ** ADAM_STEP)
        v_hat = v2 / (1.0 - ADAM_B2 ** ADAM_STEP)
        g_ref[...] = g
        d_ref[...] = -ADAM_LR * (m_hat / (jnp.sqrt(v_hat) + ADAM_EPS) + ADAM_WD * w_ref[...])
        mo_ref[...] = m2
        vo_ref[...] = v2

    row = pl.BlockSpec((tb, C), lambda i: (i, 0))
    return pl.pallas_call(
        body, grid=(R // tb,), name=name,
        in_specs=[pl.BlockSpec((ns, tb, C), lambda i: (0, i, 0)), row, row, row], out_specs=[row] * 4,
        out_shape=[jax.ShapeDtypeStruct((R, C), F32)] * 4, compiler_params=_cp(),
    )(parts, w, m, v)


_SMALL = (("dmod_x", 6 * D), ("dmod_c", 6 * D), ("b_ada", 6 * D), ("norm_mix", D), ("norm_ffn", D), ("dn_a_log", 16),
          ("dn_dt_bias", 16), ("dn_norm", HD), ("q_norm", HD), ("k_norm", HD), ("attn_sink", NH), ("ffn_conv_b", 2 * DFF),
          ("dn_conv", 5 * 3 * D), ("ffn_conv", 3 * 2 * DFF))
_SMALL_ROWS = -(-sum(n for _, n in _SMALL) // 1024) * 8


def _pack_small(d):
    flat = jnp.concatenate([d[k].reshape(-1).astype(F32) if k in d else jnp.zeros((n,), F32) for k, n in _SMALL])
    return jnp.concatenate([flat, jnp.zeros((_SMALL_ROWS * 128 - flat.shape[0],), F32)]).reshape(_SMALL_ROWS, 128)


def _unpack_small(a):
    flat = a.reshape(a.shape[:-2] + (-1,))
    out, off = {}, 0
    for k, n in _SMALL:
        out[k] = flat[..., off:off + n]
        off += n
    return out


def kernel(x, c, ctx, c_ctx, w_ada, b_ada, norm_mix, norm_ffn, w_in, dn_conv, dn_a_log, dn_dt_bias, dn_norm, q_norm, k_norm, attn_sink, w_branch_dn, w_branch_attn, w_out, ffn_up, ffn_conv, ffn_conv_b, ffn_down, loss_target, m_c_ctx, m_w_ada, m_b_ada, m_norm_mix, m_norm_ffn, m_w_in, m_dn_conv, m_dn_a_log, m_dn_dt_bias, m_dn_norm, m_q_norm, m_k_norm, m_attn_sink, m_w_branch_dn, m_w_branch_attn, m_w_out, m_ffn_up, m_ffn_conv, m_ffn_conv_b, m_ffn_down, v_c_ctx, v_w_ada, v_b_ada, v_norm_mix, v_norm_ffn, v_w_in, v_dn_conv, v_dn_a_log, v_dn_dt_bias, v_dn_norm, v_q_norm, v_k_norm, v_attn_sink, v_w_branch_dn, v_w_branch_attn, v_w_out, v_ffn_up, v_ffn_conv, v_ffn_conv_b, v_ffn_down):
    me = 4 * lax.axis_index("x") + 2 * lax.axis_index("y") + lax.axis_index("c")
    ada_cols = w_ada.shape[2]

    cols = lambda a: jnp.swapaxes(a, 0, 1).reshape(a.shape[1], -1)
    rows = lambda a: a.reshape(-1, a.shape[2])
    col_blocks = lambda g: jnp.swapaxes(g.reshape(g.shape[0], N_DEV, -1), 0, 1)
    row_blocks = lambda g: g.reshape(N_DEV, -1, g.shape[1])

    gathered = _exchange([w_in[0].astype(BF), c, dn_conv[0], ffn_conv[0]], scatter=False, name="gather_first")
    w_in_packed = _pack_w_in(cols(gathered[0]))
    c_all = gathered[1][:, 0, :]

    def project_in(h1):
        p, rest = _mm(h1, w_in_packed, form="nn", out_dtype=F32, name="in_proj",
                      exchange=([w_branch_dn[0].astype(BF), w_branch_attn[0].astype(BF), w_out[0].astype(BF),
                                 ffn_up[0].astype(BF), ffn_down[0].astype(BF)], False))
        return p, (w_in_packed, rows(rest[0]), rows(rest[1]), rows(rest[2]), cols(rest[3]), rows(rest[4]))

    def project_back(h1, dp, w_in_p, grads):
        g_bdn, g_bat, g_out, g_up, g_down = grads
        g_in, landed_rest = _mm(h1, dp, form="tn", out_dtype=BF, name="g_w_in",
                                exchange=([row_blocks(g_bdn), row_blocks(g_bat), row_blocks(g_out), col_blocks(g_up),
                                           row_blocks(g_down)], True))
        dh1, landed_in = _mm(dp, w_in_p, form="nt", out_dtype=F32, name="d_h1",
                             exchange=([col_blocks(_unpack_w_in(g_in))], True))
        return [landed_in[0]] + landed_rest, dh1

    c16 = jnp.concatenate([c_all, c_ctx[None], jnp.zeros((7, D), F32)], axis=0)
    b_loc = lax.dynamic_slice_in_dim(b_ada, me * ada_cols, ada_cols, axis=1)
    mod_part = _ada_fwd(c16, w_ada[0], b_loc)
    mod_all = cols(_exchange([mod_part], scatter=False, name="gather_mod")[0])
    mod_x = lax.dynamic_slice_in_dim(mod_all, me, 1, axis=0).reshape(6, D)
    mod_c = mod_all[8].reshape(6, D)

    loss_loc, grad_x, landed, small = _local_step(
        x[0], ctx[0], mod_x, mod_c, loss_target[0], project_in, project_back,
        norm_mix, norm_ffn, cols(gathered[2]), dn_a_log[0], dn_dt_bias[0], dn_norm, q_norm, k_norm, attn_sink[0], cols(gathered[3]),
        ffn_conv_b)
    loss = lax.psum(loss_loc, ("x", "y", "c"))

    res = {}
    res["w_in"] = _adamw(landed[0], w_in[0], m_w_in[0], v_w_in[0], "adamw_w_in")
    res["w_branch_dn"] = _adamw(landed[1], w_branch_dn[0], m_w_branch_dn[0], v_w_branch_dn[0], "adamw_w_branch_dn")
    res["w_branch_attn"] = _adamw(landed[2], w_branch_attn[0], m_w_branch_attn[0], v_w_branch_attn[0], "adamw_w_branch_attn")
    res["w_out"] = _adamw(landed[3], w_out[0], m_w_out[0], v_w_out[0], "adamw_w_out")
    res["ffn_up"] = _adamw(landed[4], ffn_up[0], m_ffn_up[0], v_ffn_up[0], "adamw_ffn_up")
    res["ffn_down"] = _adamw(landed[5], ffn_down[0], m_ffn_down[0], v_ffn_down[0], "adamw_ffn_down")

    small = dict(small)
    small["b_ada"] = small["dmod_x"] + small["dmod_c"]
    parts = _exchange([_pack_small(small)], scatter=False, name="gather_small")[0]
    per_dev = _unpack_small(parts)
    given = dict(b_ada=(b_ada, m_b_ada, v_b_ada), norm_mix=(norm_mix, m_norm_mix, v_norm_mix), norm_ffn=(norm_ffn, m_norm_ffn, v_norm_ffn),
                 dn_a_log=(dn_a_log, m_dn_a_log, v_dn_a_log), dn_dt_bias=(dn_dt_bias, m_dn_dt_bias, v_dn_dt_bias),
                 dn_norm=(dn_norm, m_dn_norm, v_dn_norm), q_norm=(q_norm, m_q_norm, v_q_norm), k_norm=(k_norm, m_k_norm, v_k_norm),
                 attn_sink=(attn_sink, m_attn_sink, v_attn_sink), ffn_conv_b=(ffn_conv_b, m_ffn_conv_b, v_ffn_conv_b))
    packs = [_pack_small({k: t[j] for k, t in given.items()}) for j in range(3)]
    upd = [_unpack_small(a) for a in _adamw(parts, packs[0], packs[1], packs[2], "adamw_small")]
    for k, t in given.items():
        res[k] = tuple(u[k].reshape(t[0].shape) for u in upd)
    dnc = lax.dynamic_slice_in_dim(upd[0]["dn_conv"].reshape(5, 3 * D), me * dn_conv.shape[2], dn_conv.shape[2], axis=1)
    ffc = lax.dynamic_slice_in_dim(upd[0]["ffn_conv"].reshape(3, 2 * DFF), me * ffn_conv.shape[2], ffn_conv.shape[2], axis=1)
    r8 = lambda a: _pad_rows8(a)
    t = _adamw(r8(dnc)[None], r8(dn_conv[0]), r8(m_dn_conv[0]), r8(v_dn_conv[0]), "adamw_dn_conv")
    res["dn_conv"] = tuple(a[:5][None] for a in t)
    t = _adamw(r8(ffc)[None], r8(ffn_conv[0]), r8(m_ffn_conv[0]), r8(v_ffn_conv[0]), "adamw_ffn_conv")
    res["ffn_conv"] = tuple(a[:3][None] for a in t)

    dmx = lax.dynamic_slice_in_dim(per_dev["dmod_x"], me * ada_cols, ada_cols, axis=1)
    dmc = lax.dynamic_slice_in_dim(per_dev["dmod_c"], me * ada_cols, ada_cols, axis=1)
    g_ada, pc = _ada_bwd(c16, w_ada[0], dmx, dmc)
    res["w_ada"] = _adamw(g_ada[None], w_ada[0], m_w_ada[0], v_w_ada[0], "adamw_w_ada")
    pc_all = _exchange([pc], scatter=False, name="gather_cctx")[0]
    g_cctx = _cctx_grad(pc_all, c_ctx[None])
    r8b = lambda a: jnp.broadcast_to(a, (8, D))
    t = _adamw(r8b(g_cctx)[None], r8b(c_ctx[None]), r8b(m_c_ctx[None]), r8b(v_c_ctx[None]), "adamw_c_ctx")
    res["c_ctx"] = tuple(a[0] for a in t)

    names = ("c_ctx", "w_ada", "b_ada", "norm_mix", "norm_ffn", "w_in", "dn_conv", "dn_a_log", "dn_dt_bias", "dn_norm", "q_norm",
             "k_norm", "attn_sink", "w_branch_dn", "w_branch_attn", "w_out", "ffn_up", "ffn_conv", "ffn_conv_b", "ffn_down")
    lead = ("w_ada", "w_in", "w_branch_dn", "w_branch_attn", "w_out", "ffn_up", "ffn_down")
    fix = lambda k, a: a[None] if k in lead else a
    outs = [loss, grad_x[None]]
    for j in range(4):
        outs += [fix(k, res[k][j]) for k in names]
    return tuple(outs)
```

```python
import functools

import jax
import jax.numpy as jnp
from jax import lax
from jax.experimental import pallas as pl
from jax.experimental.pallas import tpu as pltpu

F32 = jnp.float32
BF = jnp.bfloat16
HI = lax.Precision.HIGHEST
MESH = pl.DeviceIdType.MESH

D = 1024
NH = 8
HD = 128
KVH = 2
GRP = 4
KV = KVH * HD
DFF = 2816
CB = 128
GRID_W = 64
ROPE_BASE = 10000.0
EPS = 1e-6
N_DEV = 8
PW = 8192
O_QKV, O_GT, O_Q, O_MG, O_K, O_V, O_BA = 0, 3072, 4096, 5120, 7168, 7424, 7680
IN_SIZES = (3072, 1024, 16, 16, 1024, 256, 256, 2048)
IN_DIM = sum(IN_SIZES)
ADAM_LR, ADAM_B1, ADAM_B2, ADAM_EPS, ADAM_WD, ADAM_STEP = 0.001, 0.9, 0.999, 1e-08, 0.01, 10
VMEM_LIMIT = 56 * 1024 * 1024


def _cp():
    return pltpu.CompilerParams(vmem_limit_bytes=VMEM_LIMIT)


def _tile(n, cands):
    for c in cands:
        if n % c == 0:
            return c
    return n


def _iota2(shape):
    return lax.broadcasted_iota(jnp.int32, shape, 0), lax.broadcasted_iota(jnp.int32, shape, 1)


_DIMS = {"nn": ((1,), (0,)), "nt": ((1,), (1,)), "tn": ((0,), (0,))}


def _exchange_copies(ins, outs, send_sems, recv_sems, local_sems, scatter, landings):
    x, y, c = lax.axis_index("x"), lax.axis_index("y"), lax.axis_index("c")
    me = 4 * x + 2 * y + c
    local, remote = [], []
    for k in range(len(ins)):
        local.append(pltpu.make_async_copy(ins[k].at[me] if scatter else ins[k], outs[k].at[me], local_sems.at[k]))
        for m in range(1, N_DEV):
            px = 1 - x if m & 4 else x
            py = 1 - y if m & 2 else y
            pc = 1 - c if m & 1 else c
            peer = 4 * px + 2 * py + pc
            src = ins[k].at[peer] if scatter else ins[k]
            sem = k * (N_DEV - 1) + m - 1
            push = pltpu.make_async_remote_copy(src_ref=src, dst_ref=outs[k].at[me], send_sem=send_sems.at[sem],
                                                recv_sem=recv_sems.at[sem], device_id=(px, py, pc), device_id_type=MESH)
            landing = None
            if landings:
                landing = pltpu.make_async_remote_copy(src_ref=src, dst_ref=outs[k].at[peer], send_sem=send_sems.at[sem],
                                                       recv_sem=recv_sems.at[sem], device_id=(px, py, pc), device_id_type=MESH)
            remote.append((push, landing))
    return local, remote


def _exchange_start(*args):
    local, remote = _exchange_copies(*args, landings=False)
    for cp in local:
        cp.start()
    for push, _ in remote:
        push.start()


def _exchange_wait(*args):
    local, remote = _exchange_copies(*args, landings=True)
    for _, landing in remote:
        landing.wait_recv()
    for push, _ in remote:
        push.wait_send()
    for cp in local:
        cp.wait()


def _exchange_shapes(arrays, scatter):
    out_shape = [jax.ShapeDtypeStruct(a.shape if scatter else (N_DEV,) + a.shape, a.dtype) for a in arrays]
    n = len(arrays)
    sems = [pltpu.SemaphoreType.DMA((n * (N_DEV - 1),)), pltpu.SemaphoreType.DMA((n * (N_DEV - 1),)), pltpu.SemaphoreType.DMA((n,))]
    return out_shape, sems


def _mm(a, b, *, form, out_dtype, name, tm=None, tn=None, tk=None, exchange=None):
    if form == "tn":
        K, M = a.shape
        N = b.shape[1]
    else:
        M, K = a.shape
        N = b.shape[0] if form == "nt" else b.shape[1]
    tm = tm or _tile(M, (1408, 1280, 1024, 640, 512, 256, 128))
    tn = tn or _tile(N, (1408, 1024, 512, 256, 128))
    tk = tk or _tile(K, (2048, 1408, 1280, 1024, 640, 512, 256, 128))
    ni, nj, nk = M // tm, N // tn, K // tk
    dims = (_DIMS[form], ((), ()))
    ex_arrays, scatter = exchange if exchange else ([], False)
    nx = len(ex_arrays)

    def body(a_ref, b_ref, *refs):
        ex_in, o_ref, ex_out, scratch = refs[:nx], refs[nx], refs[nx + 1:2 * nx + 1], refs[2 * nx + 1:]
        i, j, k = pl.program_id(0), pl.program_id(1), pl.program_id(2)
        if nx:
            sems = scratch[-3:]

            @pl.when((i == 0) & (j == 0) & (k == 0))
            def _():
                _exchange_start(ex_in, ex_out, *sems, scatter)

        part = lax.dot_general(a_ref[...].astype(BF), b_ref[...].astype(BF), dims, preferred_element_type=F32)
        if nk == 1:
            o_ref[...] = part.astype(out_dtype)
        else:
            acc_ref = scratch[0]

            @pl.when(k == 0)
            def _():
                acc_ref[...] = part

            @pl.when(k > 0)
            def _():
                acc_ref[...] += part

            @pl.when(k == nk - 1)
            def _():
                o_ref[...] = acc_ref[...].astype(out_dtype)

        if nx:
            @pl.when((i == ni - 1) & (j == nj - 1) & (k == nk - 1))
            def _():
                _exchange_wait(ex_in, ex_out, *sems, scatter)

    if form == "tn":
        a_spec = pl.BlockSpec((tk, tm), lambda i, j, k: (k, i))
    else:
        a_spec = pl.BlockSpec((tm, tk), lambda i, j, k: (i, k))
    if form == "nt":
        b_spec = pl.BlockSpec((tn, tk), lambda i, j, k: (j, k))
    else:
        b_spec = pl.BlockSpec((tk, tn), lambda i, j, k: (k, j))
    hbm = pl.BlockSpec(memory_space=pl.ANY)
    ex_shapes, ex_sems = _exchange_shapes(ex_arrays, scatter) if nx else ([], [])
    outs = pl.pallas_call(
        body, grid=(ni, nj, nk), name=name,
        in_specs=[a_spec, b_spec] + [hbm] * nx, out_specs=[pl.BlockSpec((tm, tn), lambda i, j, k: (i, j))] + [hbm] * nx,
        out_shape=[jax.ShapeDtypeStruct((M, N), out_dtype)] + ex_shapes,
        scratch_shapes=([] if nk == 1 else [pltpu.VMEM((tm, tn), F32)]) + ex_sems,
        compiler_params=_cp(),
    )(a, b, *ex_arrays)
    return (outs[0], list(outs[1:])) if nx else outs[0]


def _norm_mod_fn(x, nw, sh, sc):
    y = x * lax.rsqrt(jnp.mean(x * x, axis=-1, keepdims=True) + EPS)
    return (y * nw) * (1.0 + sc) + sh


def _norm_mod_fwd(x, ctx, nw, sh, sc, name):
    nlat = x.shape[0]
    T = nlat + ctx.shape[0]
    tb = _tile(ctx.shape[0], (256, 128))
    nlb = nlat // tb

    def body(x_ref, c_ref, nw_ref, sh_ref, sc_ref, h_ref):
        rows = jnp.where(pl.program_id(0) < nlb, x_ref[...], c_ref[...])
        h_ref[...] = _norm_mod_fn(rows, nw_ref[...], sh_ref[0], sc_ref[0]).astype(BF)

    seg = pl.BlockSpec((1, 1, D), lambda i: (jnp.where(i >= nlb, 1, 0), 0, 0))
    return pl.pallas_call(
        body, grid=(T // tb,), name=name,
        in_specs=[pl.BlockSpec((tb, D), lambda i: (jnp.minimum(i, nlb - 1), 0)),
                  pl.BlockSpec((tb, D), lambda i: (jnp.maximum(i - nlb, 0), 0)), pl.BlockSpec((1, D), lambda i: (0, 0)), seg, seg],
        out_specs=pl.BlockSpec((tb, D), lambda i: (i, 0)),
        out_shape=jax.ShapeDtypeStruct((T, D), BF),
    )(x, ctx, nw, sh, sc)


def _norm_mod_bwd(x, nw, sh, sc, dh, *, row0, name, residual=None):
    nrows = x.shape[0]
    tb = _tile(nrows, (256, 128))
    b0 = row0 // tb

    def body(x_ref, nw_ref, sh_ref, sc_ref, dh_ref, *refs):
        dnw_ref, dsh_ref, dsc_ref = refs[-3:]
        _, vjp = jax.vjp(_norm_mod_fn, x_ref[...], nw_ref[...], sh_ref[...], sc_ref[...])
        dx, dnw, dsh, dsc = vjp(dh_ref[...])
        if residual is not None:
            refs[1][...] = dx + refs[0][...]

        @pl.when(pl.program_id(0) == 0)
        def _():
            dnw_ref[...] = jnp.zeros_like(dnw_ref)
            dsh_ref[...] = jnp.zeros_like(dsh_ref)
            dsc_ref[...] = jnp.zeros_like(dsc_ref)

        dnw_ref[...] += dnw
        dsh_ref[...] += dsh
        dsc_ref[...] += dsc

    dh_row = pl.BlockSpec((tb, D), lambda i: (b0 + i, 0))
    out_row = pl.BlockSpec((tb, D), lambda i: (i, 0))
    one = pl.BlockSpec((1, D), lambda i: (0, 0))
    with_dx = residual is not None
    return pl.pallas_call(
        body, grid=(nrows // tb,), name=name,
        in_specs=[out_row, one, one, one, dh_row] + [out_row] * with_dx, out_specs=[out_row] * with_dx + [one] * 3,
        out_shape=[jax.ShapeDtypeStruct((nrows, D), F32)] * with_dx + [jax.ShapeDtypeStruct((1, D), F32)] * 3,
    )(x, nw, sh, sc, dh, *([residual] if with_dx else []))


def _resid_norm_fwd(x, gate, y, nw, sh, sc):
    n = y.shape[0]
    tb = _tile(n, (256, 128))

    def body(x_ref, g_ref, y_ref, nw_ref, sh_ref, sc_ref, x1_ref, h_ref):
        x1 = x_ref[...] + g_ref[...] * y_ref[...]
        x1_ref[...] = x1
        h_ref[...] = _norm_mod_fn(x1, nw_ref[...], sh_ref[...], sc_ref[...]).astype(BF)

    row = pl.BlockSpec((tb, D), lambda i: (i, 0))
    one = pl.BlockSpec((1, D), lambda i: (0, 0))
    return pl.pallas_call(
        body, grid=(n // tb,), name="resid_norm_fwd",
        in_specs=[row, one, row, one, one, one], out_specs=[row, row],
        out_shape=[jax.ShapeDtypeStruct((n, D), F32), jax.ShapeDtypeStruct((n, D), BF)],
    )(x, gate, y, nw, sh, sc)


def _resid_norm_bwd(x1, gate, y, nw, sh, sc, dh, dx1_direct, w_out):
    n = y.shape[0]
    tb = _tile(n, (256, 128))

    def body(x1_ref, g_ref, y_ref, nw_ref, sh_ref, sc_ref, dh_ref, dd_ref, wo_ref,
             dx_ref, dy_ref, dg_ref, dnw_ref, dsh_ref, dsc_ref, dm_ref):
        _, vjp = jax.vjp(_norm_mod_fn, x1_ref[...], nw_ref[...], sh_ref[...], sc_ref[...])
        dxn, dnw, dsh, dsc = vjp(dh_ref[...])
        dx = dxn + dd_ref[...]
        dx_ref[...] = dx
        dy_ref[...] = (g_ref[...] * dx).astype(BF)
        dm_ref[...] = lax.dot_general(dy_ref[...], wo_ref[...], (_DIMS["nt"], ((), ())), preferred_element_type=F32)

        @pl.when(pl.program_id(0) == 0)
        def _():
            for r in (dg_ref, dnw_ref, dsh_ref, dsc_ref):
                r[...] = jnp.zeros_like(r)

        dg_ref[...] += jnp.sum(dx * y_ref[...], axis=0, keepdims=True)
        dnw_ref[...] += dnw
        dsh_ref[...] += dsh
        dsc_ref[...] += dsc

    row = pl.BlockSpec((tb, D), lambda i: (i, 0))
    one = pl.BlockSpec((1, D), lambda i: (0, 0))
    return pl.pallas_call(
        body, grid=(n // tb,), name="resid_norm_bwd",
        in_specs=[row, one, row, one, one, one, row, row, _resident((D, D))], out_specs=[row, row] + [one] * 4 + [row],
        out_shape=[jax.ShapeDtypeStruct((n, D), F32), jax.ShapeDtypeStruct((n, D), BF)] + [jax.ShapeDtypeStruct((1, D), F32)] * 4
        + [jax.ShapeDtypeStruct((n, D), F32)],
    )(x1, gate, y, nw, sh, sc, dh, dx1_direct, w_out)


HALO = 8


def _halo_specs(tb, width, nrows, col=0, halo=HALO):
    r8 = tb // halo
    cur = pl.BlockSpec((tb, width), lambda i: (i, col))
    prev = pl.BlockSpec((halo, width), lambda i: (jnp.maximum(i * r8 - 1, 0), col))
    nxt = pl.BlockSpec((halo, width), lambda i: (jnp.minimum((i + 1) * r8, nrows // halo - 1), col))
    return [cur, prev, nxt]


def _segment_edges(seg_rows, tb):
    bounds = [0]
    for s in seg_rows:
        bounds.append(bounds[-1] + s // tb)
    return bounds[:-1], [b - 1 for b in bounds[1:]]


def _keep_halos(i, starts, ends):
    keep_p = functools.reduce(lambda a, b: a & b, [i != s for s in starts])
    keep_n = functools.reduce(lambda a, b: a & b, [i != e for e in ends])
    return keep_p, keep_n


def _ext_rows(refs, cols, keep):
    cur_ref, prev_ref, next_ref = refs
    p = jnp.where(keep[0], prev_ref[:, cols].astype(F32), 0.0)
    n = jnp.where(keep[1], next_ref[:, cols].astype(F32), 0.0)
    return jnp.concatenate([p, cur_ref[:, cols].astype(F32), n], axis=0)


def _shifted_rows(xe, width, transpose=False):
    r = width // 2
    n = xe.shape[0]
    out = []
    for j in range(width):
        s = ((j - r) if transpose else (r - j)) % n
        out.append(xe if s == 0 else pltpu.roll(xe, s, 0))
    return out


def _conv_rows(shifted, w_ref, cols):
    acc = None
    for j, xs in enumerate(shifted):
        term = xs * w_ref[j:j + 1, cols]
        acc = term if acc is None else acc + term
    return acc


def _tap_grads(dcur, shifted, tb):
    return [jnp.sum(dcur * xs[HALO:HALO + tb], axis=0, keepdims=True) for xs in shifted]


def _softplus(x):
    return jnp.maximum(x, 0.0) + jnp.log(1.0 + jnp.exp(-jnp.abs(x)))


def _gates_fn(ba, alog_row, dt_row):
    col = lax.broadcasted_iota(jnp.int32, ba.shape, 1)
    beta = jax.nn.sigmoid(ba)
    g = -jnp.exp(alog_row) * _softplus(ba + dt_row)
    return jnp.where(col < 16, beta, jnp.where(col < 32, g, 0.0))


def _qkv_post_fn(c, kind):
    y = jax.nn.silu(c)
    if kind == 2:
        return y
    n = y * lax.rsqrt(jnp.sum(y * y, axis=-1, keepdims=True) + EPS)
    return n * (HD ** -0.5) if kind == 0 else n


DN_TAPS = 5
FFN_TAPS = 3


def _dn_pre_fwd(p, w8, alog_row, dt_row, seg_rows):
    T = p.shape[0]
    tb = _tile(T, (256, 128))
    starts, ends = _segment_edges(seg_rows, tb)

    def body(cur_ref, prev_ref, next_ref, ba_ref, w_ref, al_ref, dt_ref, q_ref, k_ref, v_ref, gb_ref):
        keep = _keep_halos(pl.program_id(0), starts, ends)
        outs = (q_ref, k_ref, v_ref)
        for kind in range(3):
            for h in range(NH):
                cols = slice(kind * D + h * HD, kind * D + (h + 1) * HD)
                xe = _ext_rows((cur_ref, prev_ref, next_ref), cols, keep)
                conv = _conv_rows(_shifted_rows(xe, DN_TAPS), w_ref, cols)[HALO:HALO + tb]
                outs[kind][:, h * HD:(h + 1) * HD] = _qkv_post_fn(conv, kind)
        gb_ref[...] = _gates_fn(ba_ref[...], al_ref[...], dt_ref[...])

    row = pl.BlockSpec((tb, D), lambda i: (i, 0))
    one = pl.BlockSpec((1, 128), lambda i: (0, 0))
    return pl.pallas_call(
        body, grid=(T // tb,), name="dn_pre_fwd",
        in_specs=_halo_specs(tb, 3 * D, T) + [pl.BlockSpec((tb, 128), lambda i: (i, O_BA // 128)),
                                              pl.BlockSpec((8, 3 * D), lambda i: (0, 0)), one, one],
        out_specs=[row, row, row, pl.BlockSpec((tb, 128), lambda i: (i, 0))],
        out_shape=[jax.ShapeDtypeStruct((T, D), F32)] * 3 + [jax.ShapeDtypeStruct((T, 128), F32)],
        compiler_params=_cp(),
    )(p, p, p, p, w8, alog_row, dt_row)


def _dn_pre_bwd(p, w8, alog_row, dt_row, dq, dk, dv, dgb, seg_rows):
    T = p.shape[0]
    tb = _tile(T, (256, 128))
    starts, ends = _segment_edges(seg_rows, tb)

    def body(cur_ref, prev_ref, next_ref, ba_ref, w_ref, al_ref, dt_ref,
             dq_c, dq_p, dq_n, dk_c, dk_p, dk_n, dv_c, dv_p, dv_n, dgb_ref, dx_ref, dba_ref, dw_ref, dal_ref, ddt_ref):
        i = pl.program_id(0)
        keep = _keep_halos(i, starts, ends)

        @pl.when(i == 0)
        def _():
            dw_ref[...] = jnp.zeros_like(dw_ref)
            dal_ref[...] = jnp.zeros_like(dal_ref)
            ddt_ref[...] = jnp.zeros_like(ddt_ref)

        douts = ((dq_c, dq_p, dq_n), (dk_c, dk_p, dk_n), (dv_c, dv_p, dv_n))
        for kind in range(3):
            for h in range(NH):
                cols = slice(kind * D + h * HD, kind * D + (h + 1) * HD)
                xe = _ext_rows((cur_ref, prev_ref, next_ref), cols, keep)
                shifted = _shifted_rows(xe, DN_TAPS)
                conv = _conv_rows(shifted, w_ref, cols)
                dye = _ext_rows(douts[kind], slice(h * HD, (h + 1) * HD), keep)
                _, vjp = jax.vjp(functools.partial(_qkv_post_fn, kind=kind), conv)
                dce = vjp(dye)[0]
                dx_ref[:, cols] = _conv_rows(_shifted_rows(dce, DN_TAPS, transpose=True), w_ref, cols)[HALO:HALO + tb].astype(BF)
                for j, g in enumerate(_tap_grads(dce[HALO:HALO + tb], shifted, tb)):
                    dw_ref[j:j + 1, cols] += g
        _, vjp = jax.vjp(_gates_fn, ba_ref[...], al_ref[...], dt_ref[...])
        dba, dal, ddt = vjp(dgb_ref[...])
        dba_ref[...] = dba.astype(BF)
        dal_ref[...] += dal
        ddt_ref[...] += ddt

    one = pl.BlockSpec((1, 128), lambda i: (0, 0))
    nar = pl.BlockSpec((tb, 128), lambda i: (i, 0))
    wspec = pl.BlockSpec((8, 3 * D), lambda i: (0, 0))
    return pl.pallas_call(
        body, grid=(T // tb,), name="dn_pre_bwd",
        in_specs=_halo_specs(tb, 3 * D, T) + [pl.BlockSpec((tb, 128), lambda i: (i, O_BA // 128)), wspec, one, one]
        + _halo_specs(tb, D, T) * 3 + [nar],
        out_specs=[pl.BlockSpec((tb, 3 * D), lambda i: (i, 0)), nar, wspec, one, one],
        out_shape=[jax.ShapeDtypeStruct((T, 3 * D), BF), jax.ShapeDtypeStruct((T, 128), BF), jax.ShapeDtypeStruct((8, 3 * D), F32),
                   jax.ShapeDtypeStruct((1, 128), F32), jax.ShapeDtypeStruct((1, 128), F32)],
        compiler_params=_cp(),
    )(p, p, p, p, w8, alog_row, dt_row, dq, dq, dq, dk, dk, dk, dv, dv, dv, dgb)


def _dot_hi(a, b):
    return jnp.dot(a, b, precision=HI, preferred_element_type=F32)


def _dot_bf(a, b):
    return jnp.dot(a.astype(BF), b.astype(BF), preferred_element_type=F32)


def _dot_nt_bf(a, b):
    return lax.dot_general(a.astype(BF), b.astype(BF), (_DIMS["nt"], ((), ())), preferred_element_type=F32)


def _dot_tn_bf(a, b):
    return lax.dot_general(a.astype(BF), b.astype(BF), (_DIMS["tn"], ((), ())), preferred_element_type=F32)


def _dot_h3(a, b):
    return jnp.dot(a, b, precision=lax.Precision.HIGH, preferred_element_type=F32)


def _dot_split(fine, coarse, form):
    hi = fine.astype(BF)
    lo = (fine - hi.astype(F32)).astype(BF)
    cb = coarse.astype(BF)
    if form == "tn":
        return lax.dot_general(jnp.concatenate([cb, cb], axis=0), jnp.concatenate([hi, lo], axis=0),
                               (_DIMS["tn"], ((), ())), preferred_element_type=F32)
    parts = jnp.concatenate([hi, lo], axis=1)
    if form == "nt":
        return lax.dot_general(parts, jnp.concatenate([cb, cb], axis=1), (_DIMS["nt"], ((), ())), preferred_element_type=F32)
    return jnp.dot(parts, jnp.concatenate([cb, cb], axis=0), preferred_element_type=F32)


@jax.custom_vjp
def _mm_split(a, b):
    return _dot_split(a, b, "nn")


_mm_split.defvjp(lambda a, b: (_dot_split(a, b, "nn"), (a, b)),
                 lambda res, dc: (_dot_split(dc, res[1], "nt"), _dot_split(dc, res[0], "tn")))


def _unit_tri_inverses(mats):
    r, c = _iota2((CB, CB))
    eye = (r == c).astype(F32)
    a8 = [jnp.where((r // 8) == (c // 8), a, 0.0) for a in mats]
    a2 = [_dot_split(x, x, "nn") for x in a8]
    a4 = [_dot_split(x, x, "nn") for x in a2]
    t = [_dot_split(eye - x, eye + y, "nn") for x, y in zip(a8, a2)]
    t = [_dot_split(x, eye + y, "nn") for x, y in zip(t, a4)]
    b = 8
    while b < CB:
        mask = ((r // (2 * b)) == (c // (2 * b))) & ((r // b) != (c // b))
        te = [_dot_split(x, jnp.where(mask, a, 0.0), "nn") for x, a in zip(t, mats)]
        t = [x - _dot_split(y, x, "nn") for x, y in zip(t, te)]
        b *= 2
    return t


@jax.custom_vjp
def _saved_inverse(a, t):
    return t


_saved_inverse.defvjp(lambda a, t: (t, t),
                      lambda t, dt: (-_dot_split(_dot_split(dt, t, "nt"), t, "tn"), jnp.zeros_like(t)))


def _dn1_decay(gc, reverse):
    r, c = _iota2((CB, CB))
    incl = (c >= r) if reverse else (c <= r)
    return jnp.where(incl, jnp.exp(jnp.where(incl, gc - gc.T, 0.0)), 0.0)


def _dn1_heads(qs, ks, vs, betas, gcs, ts_saved, reverse, kks=None, qks=None):
    r, c = _iota2((CB, CB))
    strict = (c > r) if reverse else (c < r)
    decays = [_dn1_decay(gc, reverse) for gc in gcs]
    kks = kks or [_dot_nt_bf(k, k) for k in ks]
    systems = [jnp.where(strict, b * kk * dc, 0.0) for b, kk, dc in zip(betas, kks, decays)]
    if ts_saved is None:
        ts = _unit_tri_inverses(systems)
    else:
        ts = [_saved_inverse(a, t) for a, t in zip(systems, ts_saved)]
    egs = [jnp.exp(gc) for gc in gcs]
    us = [_mm_split(t, v * b) for t, v, b in zip(ts, vs, betas)]
    ws = [_mm_split(t, k * (b * eg)) for t, k, b, eg in zip(ts, ks, betas, egs)]
    qks = qks or [_dot_nt_bf(q, k) for q, k in zip(qs, ks)]
    last = 0 if reverse else CB - 1
    glogs = [jnp.sum(jnp.where(r == last, gc, 0.0), axis=0, keepdims=True) for gc in gcs]
    outs = [(u, w, q * eg, k * jnp.exp(gl - gc), qk * dc, jnp.exp(gl))
            for u, w, q, k, eg, gl, gc, qk, dc in zip(us, ws, qs, ks, egs, glogs, gcs, qks, decays)]
    return outs, ts


def _cum_matrix(upper):
    r, c = _iota2((CB, CB))
    return ((c >= r) if upper else (c <= r)).astype(F32)


def _lane_bcast(x, col):
    return jnp.broadcast_to(x[:, col:col + 1], x.shape)


_HEAD_SLICES = [slice(h * HD, (h + 1) * HD) for h in range(NH)]


def _dn1_fwd(q, k, v, gb):
    T = q.shape[0]
    nb = T // CB

    def body(q_ref, k_ref, v_ref, gb_ref, *out_refs):
        gbv = gb_ref[...]
        qs = [q_ref[:, sl] for sl in _HEAD_SLICES]
        ks = [k_ref[:, sl] for sl in _HEAD_SLICES]
        vs = [v_ref[:, sl] for sl in _HEAD_SLICES]
        kks = [_dot_nt_bf(x, x) for x in ks]
        qks = [_dot_nt_bf(x, y) for x, y in zip(qs, ks)]
        for d in (0, 1):
            u_ref, w_ref, qg_ref, kd_ref, qkd_ref, gl_ref, t_ref = out_refs[7 * d:7 * d + 7]
            gcum = _dot_h3(_cum_matrix(d == 1), gbv)
            betas = [_lane_bcast(gbv, d * NH + h) for h in range(NH)]
            gcs = [_lane_bcast(gcum, 16 + d * NH + h) for h in range(NH)]
            outs, ts = _dn1_heads(qs, ks, vs, betas, gcs, None, d == 1, kks, qks)
            for h, sl in enumerate(_HEAD_SLICES):
                u, w, qg, kd, qkd, gl = outs[h]
                u_ref[:, sl] = u
                w_ref[:, sl] = w.astype(BF)
                qg_ref[:, sl] = qg.astype(BF)
                kd_ref[:, sl] = kd.astype(BF)
                qkd_ref[:, sl] = qkd.astype(BF)
                gl_ref[h] = gl
                t_ref[:, sl] = ts[h]

    tb = pl.BlockSpec((CB, D), lambda i: (i, 0))
    one_dir_specs = [tb, tb, tb, tb, tb, pl.BlockSpec((NH, 1, 128), lambda i: (i, 0, 0)), tb]
    one_dir_shapes = ([jax.ShapeDtypeStruct((T, D), F32)] + [jax.ShapeDtypeStruct((T, D), BF)] * 4
                      + [jax.ShapeDtypeStruct((nb * NH, 1, 128), F32), jax.ShapeDtypeStruct((T, D), F32)])
    outs = pl.pallas_call(
        body, grid=(nb,), name="dn1_fwd",
        in_specs=[tb, tb, tb, pl.BlockSpec((CB, 128), lambda i: (i, 0))],
        out_specs=one_dir_specs * 2, out_shape=one_dir_shapes * 2, compiler_params=_cp(),
    )(q, k, v, gb)
    return [tuple(outs[:7]), tuple(outs[7:])]


def _dn1_bwd(q, k, v, gb, tinvs, cots):
    T = q.shape[0]
    nb = T // CB

    def body(q_ref, k_ref, v_ref, gb_ref, *refs):
        dir_refs, (dq_ref, dk_ref, dv_ref, dgb_ref) = refs[:14], refs[14:]
        gbv = gb_ref[...]
        qs = [q_ref[:, sl] for sl in _HEAD_SLICES]
        ks = [k_ref[:, sl] for sl in _HEAD_SLICES]
        vs = [v_ref[:, sl] for sl in _HEAD_SLICES]
        lane = lax.broadcasted_iota(jnp.int32, (CB, 128), 1)
        dgb = jnp.zeros((CB, 128), F32)
        for d in (0, 1):
            t_ref, du_ref, dw_ref, dqg_ref, dkd_ref, dqkd_ref, dgl_ref = dir_refs[7 * d:7 * d + 7]
            gcum = _dot_h3(_cum_matrix(d == 1), gbv)
            betas = [_lane_bcast(gbv, d * NH + h) for h in range(NH)]
            gcs = [_lane_bcast(gcum, 16 + d * NH + h) for h in range(NH)]
            ts = [t_ref[:, sl] for sl in _HEAD_SLICES]
            f = lambda qs, ks, vs, betas, gcs: _dn1_heads(qs, ks, vs, betas, gcs, ts, d == 1)[0]
            _, vjp = jax.vjp(f, qs, ks, vs, betas, gcs)
            cot = [(du_ref[:, sl], dw_ref[:, sl], dqg_ref[:, sl], dkd_ref[:, sl], dqkd_ref[:, sl], dgl_ref[h])
                   for h, sl in enumerate(_HEAD_SLICES)]
            dqs, dks, dvs, dbetas, dgcs = vjp(cot)
            dgcum = jnp.zeros((CB, 128), F32)
            for h, sl in enumerate(_HEAD_SLICES):
                if d == 0:
                    dq_ref[:, sl] = dqs[h]
                    dk_ref[:, sl] = dks[h]
                    dv_ref[:, sl] = dvs[h]
                else:
                    dq_ref[:, sl] += dqs[h]
                    dk_ref[:, sl] += dks[h]
                    dv_ref[:, sl] += dvs[h]
                dgb = dgb + jnp.where(lane == d * NH + h, jnp.sum(dbetas[h], axis=1, keepdims=True), 0.0)
                dgcum = dgcum + jnp.where(lane == 16 + d * NH + h, jnp.sum(dgcs[h], axis=1, keepdims=True), 0.0)
            dgb = dgb + _dot_h3(_cum_matrix(d == 0), dgcum)
        dgb_ref[...] = dgb

    tb = pl.BlockSpec((CB, D), lambda i: (i, 0))
    gbs = pl.BlockSpec((CB, 128), lambda i: (i, 0))
    gls = pl.BlockSpec((NH, 1, 128), lambda i: (i, 0, 0))
    args = []
    for d in (0, 1):
        args += [tinvs[d], *cots[d]]
    return pl.pallas_call(
        body, grid=(nb,), name="dn1_bwd",
        in_specs=[tb, tb, tb, gbs] + [tb, tb, tb, tb, tb, tb, gls] * 2, out_specs=[tb, tb, tb, gbs],
        out_shape=[jax.ShapeDtypeStruct((T, D), F32)] * 3 + [jax.ShapeDtypeStruct((T, 128), F32)],
        compiler_params=_cp(),
    )(q, k, v, gb, *args)


def _dn2_steps(chains):
    ws = [_dot_bf(w, s) for _, w, _, _, _, _, s in chains]
    v_new = [c[0] - x for c, x in zip(chains, ws)]
    o_state = [_dot_bf(c[2], c[6]) for c in chains]
    o_local = [_dot_bf(c[4], vn) for c, vn in zip(chains, v_new)]
    grow = [_dot_tn_bf(c[3], vn) for c, vn in zip(chains, v_new)]
    return [a + b for a, b in zip(o_state, o_local)], [c[6] * c[5] + g for c, g in zip(chains, grow)]


def _scan_order(direction, nlat_b, nall_b):
    if direction == 0:
        return lambda i: (i + nlat_b) % nall_b
    return lambda i: nall_b - 1 - i


def _dn2_fwd(per_dir, nlat):
    T = per_dir[0][0].shape[0]
    nb = T // CB
    blks = [_scan_order(d, nlat // CB, nb) for d in (0, 1)]

    def body(*refs):
        ins, outs, s_scr = refs[:12], refs[12:16], refs[16]

        @pl.when(pl.program_id(0) == 0)
        def _():
            s_scr[...] = jnp.zeros_like(s_scr)
        for d in (0, 1):
            outs[2 * d + 1][0] = s_scr[d]
        where = [(d, h, sl) for h, sl in enumerate(_HEAD_SLICES) for d in (0, 1)]
        chains = []
        for d, h, sl in where:
            u_ref, w_ref, qg_ref, kd_ref, qkd_ref, gl_ref = ins[6 * d:6 * d + 6]
            chains.append((u_ref[:, sl], w_ref[:, sl], qg_ref[:, sl], kd_ref[:, sl], qkd_ref[:, sl], gl_ref[h], s_scr[d, h]))
        os, states = _dn2_steps(chains)
        for (d, h, sl), o, s_next in zip(where, os, states):
            outs[2 * d][:, sl] = o
            s_scr[d, h] = s_next

    in_specs, out_specs, args = [], [], []
    for d in (0, 1):
        blk = blks[d]
        tb = pl.BlockSpec((CB, D), lambda i, blk=blk: (blk(i), 0))
        in_specs += [tb] * 5 + [pl.BlockSpec((NH, 1, 128), lambda i, blk=blk: (blk(i), 0, 0))]
        out_specs += [tb, pl.BlockSpec((1, NH, HD, HD), lambda i, blk=blk: (blk(i), 0, 0, 0))]
        args += list(per_dir[d])
    outs = pl.pallas_call(
        body, grid=(nb,), name="dn2_fwd", in_specs=in_specs, out_specs=out_specs,
        out_shape=[jax.ShapeDtypeStruct((T, D), F32), jax.ShapeDtypeStruct((nb, NH, HD, HD), F32)] * 2,
        scratch_shapes=[pltpu.VMEM((2, NH, HD, HD), F32)], compiler_params=_cp(),
    )(*args)
    return [tuple(outs[:2]), tuple(outs[2:])]


def _dn2_bwd(per_dir, do, nlat):
    T = per_dir[0][0].shape[0]
    nb = T // CB
    nlat_b = nlat // CB
    fwd = [_scan_order(d, nlat_b, nb) for d in (0, 1)]
    blks = [lambda i, f=f: f(nb - 1 - i) for f in fwd]

    def body(*refs):
        ins, outs, ds_scr = refs[:16], refs[16:28], refs[28]
        i = pl.program_id(0)

        @pl.when(i == 0)
        def _():
            ds_scr[...] = jnp.zeros_like(ds_scr)
        where = [(d, h, sl) for h, sl in enumerate(_HEAD_SLICES) for d in (0, 1)]
        chains, cot_o, cot_s = [], [], []
        for d, h, sl in where:
            u_ref, w_ref, qg_ref, kd_ref, qkd_ref, gl_ref, sall_ref, do_ref = ins[8 * d:8 * d + 8]
            chains.append((u_ref[:, sl], w_ref[:, sl].astype(F32), qg_ref[:, sl].astype(F32), kd_ref[:, sl].astype(F32),
                           qkd_ref[:, sl].astype(F32), gl_ref[h], sall_ref[0, h]))
            cot_o.append(jnp.where(blks[d](i) < nlat_b, do_ref[:, sl], 0.0))
            cot_s.append(ds_scr[d, h])
        _, vjp = jax.vjp(_dn2_steps, chains)
        for (d, h, sl), (du, dw, dqg, dkd, dqkd, dgl, ds) in zip(where, vjp((cot_o, cot_s))[0]):
            du_ref, dw_ref, dqg_ref, dkd_ref, dqkd_ref, dgl_ref = outs[6 * d:6 * d + 6]
            du_ref[:, sl] = du
            dw_ref[:, sl] = dw
            dqg_ref[:, sl] = dqg
            dkd_ref[:, sl] = dkd
            dqkd_ref[:, sl] = dqkd
            dgl_ref[h] = dgl
            ds_scr[d, h] = ds

    in_specs, out_specs, args = [], [], []
    for d in (0, 1):
        blk = blks[d]
        tb = pl.BlockSpec((CB, D), lambda i, blk=blk: (blk(i), 0))
        gls = pl.BlockSpec((NH, 1, 128), lambda i, blk=blk: (blk(i), 0, 0))
        in_specs += [tb] * 5 + [gls, pl.BlockSpec((1, NH, HD, HD), lambda i, blk=blk: (blk(i), 0, 0, 0)),
                                pl.BlockSpec((CB, D), lambda i, blk=blk: (jnp.minimum(blk(i), nlat_b - 1), 0))]
        out_specs += [tb] * 5 + [gls]
        args += list(per_dir[d]) + [do]
    outs = pl.pallas_call(
        body, grid=(nb,), name="dn2_bwd", in_specs=in_specs, out_specs=out_specs,
        out_shape=([jax.ShapeDtypeStruct((T, D), F32)] * 5 + [jax.ShapeDtypeStruct((nb * NH, 1, 128), F32)]) * 2,
        scratch_shapes=[pltpu.VMEM((2, NH, HD, HD), F32)], compiler_params=_cp(),
    )(*args)
    return [tuple(outs[:6]), tuple(outs[6:])]


def _ghn_fn(o, gt, w):
    y = o * lax.rsqrt(jnp.mean(o * o, axis=-1, keepdims=True) + EPS)
    return (y * w) * jax.nn.silu(gt)


def _ghn_fwd(o_f, o_b, p, w, w_branch, nlat):
    tb = _tile(nlat, (256, 128))

    def body(of_ref, ob_ref, gt_ref, w_ref, wb_ref, y_ref, z_ref):
        for h in range(NH):
            sl = slice(h * HD, (h + 1) * HD)
            y_ref[:, sl] = _ghn_fn(of_ref[:, sl] + ob_ref[:, sl], gt_ref[:, sl], w_ref[...]).astype(BF)
        z_ref[...] = jnp.dot(y_ref[...], wb_ref[...], preferred_element_type=F32)

    row = pl.BlockSpec((tb, D), lambda i: (i, 0))
    return pl.pallas_call(
        body, grid=(nlat // tb,), name="ghn_fwd",
        in_specs=[row, row, pl.BlockSpec((tb, D), lambda i: (i, O_GT // D)), pl.BlockSpec((1, HD), lambda i: (0, 0)), _resident((D, D))],
        out_specs=[row, row], out_shape=[jax.ShapeDtypeStruct((nlat, D), BF), jax.ShapeDtypeStruct((nlat, D), F32)],
    )(o_f, o_b, p, w, w_branch)


def _ghn_bwd(o_f, o_b, p, w, dy, nlat):
    T = p.shape[0]
    tb = _tile(nlat, (256, 128))
    nlb = nlat // tb

    def body(of_ref, ob_ref, gt_ref, w_ref, dy_ref, do_ref, dgt_ref, dw_ref):
        is_lat = pl.program_id(0) < nlb

        @pl.when(pl.program_id(0) == 0)
        def _():
            dw_ref[...] = jnp.zeros_like(dw_ref)
        for h in range(NH):
            sl = slice(h * HD, (h + 1) * HD)
            _, vjp = jax.vjp(_ghn_fn, of_ref[:, sl] + ob_ref[:, sl], gt_ref[:, sl], w_ref[...])
            do, dgt, dw = vjp(dy_ref[:, sl])
            do_ref[:, sl] = do
            dgt_ref[:, sl] = jnp.where(is_lat, dgt, 0.0).astype(BF)
            dw_ref[...] += jnp.where(is_lat, dw, 0.0)

    lat = lambda i: jnp.minimum(i, nlb - 1)
    row = pl.BlockSpec((tb, D), lambda i: (lat(i), 0))
    one = pl.BlockSpec((1, HD), lambda i: (0, 0))
    return pl.pallas_call(
        body, grid=(T // tb,), name="ghn_bwd",
        in_specs=[row, row, pl.BlockSpec((tb, D), lambda i: (lat(i), O_GT // D)), one, row],
        out_specs=[row, pl.BlockSpec((tb, D), lambda i: (i, 0)), one],
        out_shape=[jax.ShapeDtypeStruct((nlat, D), F32), jax.ShapeDtypeStruct((T, D), BF), jax.ShapeDtypeStruct((1, HD), F32)],
    )(o_f, o_b, p, w, dy)


@jax.custom_vjp
def _swap32(x):
    lane = lax.broadcasted_iota(jnp.int32, x.shape, 1)
    return jnp.where((lane & 32) == 0, pltpu.roll(x, 96, 1), pltpu.roll(x, 32, 1))


_swap32.defvjp(lambda x: (_swap32(x), None), lambda _, g: (_swap32(g),))


def _qk_post_fn(x, w, cos, sin):
    y = (x * lax.rsqrt(jnp.mean(x * x, axis=-1, keepdims=True) + EPS)) * w
    return y * cos + _swap32(y) * sin


def _attn_prep_fwd(p, qn, kn, cos, sin):
    T = p.shape[0]
    tb = _tile(T, (256, 128))

    def body(q_ref, k_ref, v_ref, qn_ref, kn_ref, cos_ref, sin_ref, qr_ref, kr_ref, vb_ref):
        cos_v, sin_v = cos_ref[...], sin_ref[...]
        for h in range(NH):
            sl = slice(h * HD, (h + 1) * HD)
            qr_ref[:, sl] = _qk_post_fn(q_ref[:, sl], qn_ref[...], cos_v, sin_v).astype(BF)
        for h in range(KVH):
            sl = slice(h * HD, (h + 1) * HD)
            kr_ref[:, sl] = _qk_post_fn(k_ref[:, sl], kn_ref[...], cos_v, sin_v).astype(BF)
        vb_ref[...] = v_ref[...].astype(BF)

    one = pl.BlockSpec((1, HD), lambda i: (0, 0))
    tab = pl.BlockSpec((tb, HD), lambda i: (i, 0))
    return pl.pallas_call(
        body, grid=(T // tb,), name="attn_prep_fwd",
        in_specs=[pl.BlockSpec((tb, D), lambda i: (i, O_Q // D)), pl.BlockSpec((tb, KV), lambda i: (i, O_K // KV)),
                  pl.BlockSpec((tb, KV), lambda i: (i, O_V // KV)), one, one, tab, tab],
        out_specs=[pl.BlockSpec((tb, D), lambda i: (i, 0)), pl.BlockSpec((tb, KV), lambda i: (i, 0)),
                   pl.BlockSpec((tb, KV), lambda i: (i, 0))],
        out_shape=[jax.ShapeDtypeStruct((T, D), BF), jax.ShapeDtypeStruct((T, KV), BF), jax.ShapeDtypeStruct((T, KV), BF)],
    )(p, p, p, qn, kn, cos, sin)


def _attn_prep_bwd(p, qn, kn, cos, sin, dqr, dkp, dvp, dkc, dvc, nlat):
    T = p.shape[0]
    nqb = nlat // CB
    ncb = (T - nlat) // CB

    def body(q_ref, k_ref, v_ref, qn_ref, kn_ref, cos_ref, sin_ref, dqr_ref, dka_ref, dkb_ref, dkc3_ref, dva_ref, dvb_ref, dvc3_ref,
             dkctx_ref, dvctx_ref, dq_ref, dk_ref, dv_ref, dqn_ref, dkn_ref):
        i = pl.program_id(0)
        is_lat = i < nqb
        cos_v, sin_v = cos_ref[...], sin_ref[...]

        @pl.when(i == 0)
        def _():
            dqn_ref[...] = jnp.zeros_like(dqn_ref)
            dkn_ref[...] = jnp.zeros_like(dkn_ref)

        def band_sum(a_ref, b_ref, c_ref, ctx_ref):
            s = b_ref[0] + jnp.where(i > 0, a_ref[0], 0.0) + jnp.where(i < nqb - 1, c_ref[0], 0.0)
            return jnp.where(is_lat, s, ctx_ref[...])

        dkr = band_sum(dka_ref, dkb_ref, dkc3_ref, dkctx_ref)
        dv_ref[...] = band_sum(dva_ref, dvb_ref, dvc3_ref, dvctx_ref).astype(BF)
        for h in range(NH):
            sl = slice(h * HD, (h + 1) * HD)
            _, vjp = jax.vjp(_qk_post_fn, q_ref[:, sl], qn_ref[...], cos_v, sin_v)
            dq, dqn, _, _ = vjp(jnp.where(is_lat, dqr_ref[:, sl], 0.0))
            dq_ref[:, sl] = dq.astype(BF)
            dqn_ref[...] += dqn
        for h in range(KVH):
            sl = slice(h * HD, (h + 1) * HD)
            _, vjp = jax.vjp(_qk_post_fn, k_ref[:, sl], kn_ref[...], cos_v, sin_v)
            dk, dkn, _, _ = vjp(dkr[:, sl])
            dk_ref[:, sl] = dk.astype(BF)
            dkn_ref[...] += dkn

    one = pl.BlockSpec((1, HD), lambda i: (0, 0))
    tab = pl.BlockSpec((CB, HD), lambda i: (i, 0))
    lat = lambda i: jnp.minimum(i, nqb - 1)

    def part(off, slot):
        return pl.BlockSpec((1, CB, KV), lambda i: (jnp.clip(lat(i) + off, 0, nqb - 1) * 3 + slot, 0, 0))

    ctxs = pl.BlockSpec((CB, KV), lambda i: (jnp.clip(i - nqb, 0, ncb - 1), 0))
    kvs = pl.BlockSpec((CB, KV), lambda i: (i, 0))
    return pl.pallas_call(
        body, grid=(T // CB,), name="attn_prep_bwd",
        in_specs=[pl.BlockSpec((CB, D), lambda i: (i, O_Q // D)), pl.BlockSpec((CB, KV), lambda i: (i, O_K // KV)),
                  pl.BlockSpec((CB, KV), lambda i: (i, O_V // KV)), one, one, tab, tab,
                  pl.BlockSpec((CB, D), lambda i: (lat(i), 0)),
                  part(-1, 2), part(0, 1), part(1, 0), part(-1, 2), part(0, 1), part(1, 0), ctxs, ctxs],
        out_specs=[pl.BlockSpec((CB, D), lambda i: (i, 0)), kvs, kvs, one, one],
        out_shape=[jax.ShapeDtypeStruct((T, D), BF), jax.ShapeDtypeStruct((T, KV), BF), jax.ShapeDtypeStruct((T, KV), BF),
                   jax.ShapeDtypeStruct((1, HD), F32), jax.ShapeDtypeStruct((1, HD), F32)],
    )(p, p, p, qn, kn, cos, sin, dqr, dkp, dkp, dkp, dvp, dvp, dvp, dkc, dvc)


def _attn_groups_fn(qs, kalls, valls, sinks, bias):
    groups = range(KVH)
    q = [jnp.concatenate(qs[GRP * g:GRP * (g + 1)], axis=0) for g in groups]
    s = [_dot_nt_bf(q[g], kalls[g]) * (HD ** -0.5) + bias for g in groups]
    sk = [jnp.concatenate([jnp.broadcast_to(jnp.mean(t, axis=1, keepdims=True), (CB, 1)) for t in sinks[GRP * g:GRP * (g + 1)]],
                          axis=0) for g in groups]
    m = [lax.stop_gradient(jnp.maximum(jnp.max(s[g], axis=1, keepdims=True), sk[g])) for g in groups]
    e = [jnp.exp(s[g] - m[g]) for g in groups]
    den = [jnp.sum(e[g], axis=1, keepdims=True) + jnp.exp(sk[g] - m[g]) for g in groups]
    return [_dot_bf(e[g] / den[g], valls[g]) for g in groups]


def _attn_bias(lc):
    r, c = _iota2((GRP * CB, 3 * CB + lc))
    rel = c - (r & (CB - 1))
    win = (rel >= 0) & (rel <= 2 * CB)
    ctx = c >= 3 * CB
    seen = [(win & (c >= CB)) | ctx, win | ctx, (win & (c < 2 * CB)) | ctx]
    return jnp.stack([jnp.where(s, 0.0, -1e30) for s in seen]).astype(F32)


def _attn_specs(nqb, lc, nlat):
    assert nqb >= 2
    qs = pl.BlockSpec((CB, D), lambda i: (i, 0))
    ka = pl.BlockSpec((CB, KV), lambda i: (jnp.maximum(i - 1, 0), 0))
    kb = pl.BlockSpec((CB, KV), lambda i: (i, 0))
    kc = pl.BlockSpec((CB, KV), lambda i: (jnp.minimum(i + 1, nqb - 1), 0))
    kx = pl.BlockSpec((lc, KV), lambda i: (nlat // lc, 0))
    sk = pl.BlockSpec((KVH, 8, 128), lambda i: (0, 0, 0))
    bs = pl.BlockSpec((1, GRP * CB, 3 * CB + lc), lambda i: (jnp.where(i == 0, 0, jnp.where(i == nqb - 1, 2, 1)), 0, 0))
    return qs, ka, kb, kc, kx, sk, bs


def _attn_operands(q_ref, k_refs, v_refs, sk_ref, dtype):
    sls = [slice(g * HD, (g + 1) * HD) for g in range(KVH)]
    kalls = [jnp.concatenate([r[:, sl] for r in k_refs], axis=0).astype(dtype) for sl in sls]
    valls = [jnp.concatenate([r[:, sl] for r in v_refs], axis=0).astype(dtype) for sl in sls]
    qs = [q_ref[:, sl].astype(dtype) for sl in _HEAD_SLICES]
    sinks = [sk_ref[h // GRP, (h % GRP):(h % GRP) + 1, :] for h in range(NH)]
    return qs, kalls, valls, sinks


def _attn_fwd(qr, kr, vb, sink, w_branch, nlat):
    lc = kr.shape[0] - nlat
    nqb = nlat // CB
    qs, ka, kb, kc, kx, sk, bs = _attn_specs(nqb, lc, nlat)

    def body(q_ref, ka_ref, kb_ref, kc_ref, kx_ref, va_ref, vb_ref, vc_ref, vx_ref, sk_ref, bias_ref, wb_ref, o_ref, z_ref):
        operands = _attn_operands(q_ref, (ka_ref, kb_ref, kc_ref, kx_ref), (va_ref, vb_ref, vc_ref, vx_ref), sk_ref, BF)
        outs = _attn_groups_fn(*operands, bias_ref[0])
        for h, sl in enumerate(_HEAD_SLICES):
            o_ref[:, sl] = outs[h // GRP][(h % GRP) * CB:(h % GRP + 1) * CB].astype(BF)
        z_ref[...] = jnp.dot(o_ref[...], wb_ref[...], preferred_element_type=F32)

    return pl.pallas_call(
        body, grid=(nqb,), name="attn_fwd",
        in_specs=[qs, ka, kb, kc, kx, ka, kb, kc, kx, sk, bs, _resident((D, D))], out_specs=[qs, qs],
        out_shape=[jax.ShapeDtypeStruct((nlat, D), BF), jax.ShapeDtypeStruct((nlat, D), F32)], compiler_params=_cp(),
    )(qr, kr, kr, kr, kr, vb, vb, vb, vb, sink, _attn_bias(lc), w_branch)


def _attn_bwd(qr, kr, vb, sink, dy, nlat):
    lc = kr.shape[0] - nlat
    nqb = nlat // CB
    qs, ka, kb, kc, kx, sk, bs = _attn_specs(nqb, lc, nlat)

    def body(q_ref, ka_ref, kb_ref, kc_ref, kx_ref, va_ref, vb_ref, vc_ref, vx_ref, sk_ref, dy_ref, bias_ref,
             dq_ref, dkp_ref, dvp_ref, dkx_ref, dvx_ref, dsk_ref):
        operands = _attn_operands(q_ref, (ka_ref, kb_ref, kc_ref, kx_ref), (va_ref, vb_ref, vc_ref, vx_ref), sk_ref, F32)
        _, vjp = jax.vjp(functools.partial(_attn_groups_fn, bias=bias_ref[0]), *operands)
        dys_g = [jnp.concatenate([dy_ref[:, sl] for sl in _HEAD_SLICES[GRP * g:GRP * (g + 1)]], axis=0) for g in range(KVH)]
        dqs, dks, dvs, dsinks = vjp(dys_g)

        @pl.when(pl.program_id(0) == 0)
        def _():
            dkx_ref[...] = jnp.zeros_like(dkx_ref)
            dvx_ref[...] = jnp.zeros_like(dvx_ref)
            dsk_ref[...] = jnp.zeros_like(dsk_ref)

        for h, sl in enumerate(_HEAD_SLICES):
            dq_ref[:, sl] = dqs[h]
            dsk_ref[h // GRP, (h % GRP):(h % GRP) + 1, :] += dsinks[h]
        for g in range(KVH):
            sl = slice(g * HD, (g + 1) * HD)
            for t in range(3):
                dkp_ref[t, :, sl] = dks[g][t * CB:(t + 1) * CB]
                dvp_ref[t, :, sl] = dvs[g][t * CB:(t + 1) * CB]
            dkx_ref[:, sl] += dks[g][3 * CB:]
            dvx_ref[:, sl] += dvs[g][3 * CB:]

    dys = qs
    parts = pl.BlockSpec((3, CB, KV), lambda i: (i, 0, 0))
    ctxo = pl.BlockSpec((lc, KV), lambda i: (0, 0))
    return pl.pallas_call(
        body, grid=(nqb,), name="attn_bwd",
        in_specs=[qs, ka, kb, kc, kx, ka, kb, kc, kx, sk, dys, bs],
        out_specs=[dys, parts, parts, ctxo, ctxo, sk],
        out_shape=[jax.ShapeDtypeStruct((nlat, D), F32), jax.ShapeDtypeStruct((3 * nqb, CB, KV), F32),
                   jax.ShapeDtypeStruct((3 * nqb, CB, KV), F32), jax.ShapeDtypeStruct((lc, KV), F32),
                   jax.ShapeDtypeStruct((lc, KV), F32), jax.ShapeDtypeStruct((KVH, 8, 128), F32)],
        compiler_params=_cp(),
    )(qr, kr, kr, kr, kr, vb, vb, vb, vb, sink, dy, _attn_bias(lc))


def _merge_fn(z_dn, z_at, g_dn, g_at):
    return jax.nn.sigmoid(g_dn) * z_dn + jax.nn.sigmoid(g_at) * z_at


def _merge_fwd(z_dn, z_at, p, w_out, nlat):
    tb = _tile(nlat, (256, 128))

    def body(zd_ref, za_ref, gd_ref, ga_ref, wo_ref, o_ref, mix_ref):
        o_ref[...] = _merge_fn(zd_ref[...], za_ref[...], gd_ref[...], ga_ref[...]).astype(BF)
        mix_ref[...] = jnp.dot(o_ref[...], wo_ref[...], preferred_element_type=F32)

    row = pl.BlockSpec((tb, D), lambda i: (i, 0))
    return pl.pallas_call(
        body, grid=(nlat // tb,), name="merge_fwd",
        in_specs=[row, row, pl.BlockSpec((tb, D), lambda i: (i, O_MG // D)), pl.BlockSpec((tb, D), lambda i: (i, O_MG // D + 1)),
                  _resident((D, D))],
        out_specs=[row, row], out_shape=[jax.ShapeDtypeStruct((nlat, D), BF), jax.ShapeDtypeStruct((nlat, D), F32)],
    )(z_dn, z_at, p, p, w_out)


def _merge_bwd(z_dn, z_at, p, dm, w_bdn, w_bat, nlat):
    T = p.shape[0]
    tb = _tile(nlat, (256, 128))
    nlb = nlat // tb

    def body(zd_ref, za_ref, gd_ref, ga_ref, dm_ref, wd_ref, wa_ref, dzd_ref, dza_ref, dg_ref, dyd_ref, dya_ref):
        is_lat = pl.program_id(0) < nlb
        _, vjp = jax.vjp(_merge_fn, zd_ref[...], za_ref[...], gd_ref[...], ga_ref[...])
        dzd, dza, dgd, dga = vjp(dm_ref[...])
        dzd_ref[...] = dzd.astype(BF)
        dza_ref[...] = dza.astype(BF)
        dg_ref[:, :D] = jnp.where(is_lat, dgd, 0.0).astype(BF)
        dg_ref[:, D:] = jnp.where(is_lat, dga, 0.0).astype(BF)
        dyd_ref[...] = lax.dot_general(dzd_ref[...], wd_ref[...], (_DIMS["nt"], ((), ())), preferred_element_type=F32)
        dya_ref[...] = lax.dot_general(dza_ref[...], wa_ref[...], (_DIMS["nt"], ((), ())), preferred_element_type=F32)

    lat = lambda i: jnp.minimum(i, nlb - 1)
    row = pl.BlockSpec((tb, D), lambda i: (lat(i), 0))
    return pl.pallas_call(
        body, grid=(T // tb,), name="merge_bwd",
        in_specs=[row, row, pl.BlockSpec((tb, D), lambda i: (lat(i), O_MG // D)),
                  pl.BlockSpec((tb, D), lambda i: (lat(i), O_MG // D + 1)), row, _resident((D, D)), _resident((D, D))],
        out_specs=[row, row, pl.BlockSpec((tb, 2 * D), lambda i: (i, 0)), row, row],
        out_shape=[jax.ShapeDtypeStruct((nlat, D), BF), jax.ShapeDtypeStruct((nlat, D), BF), jax.ShapeDtypeStruct((T, 2 * D), BF),
                   jax.ShapeDtypeStruct((nlat, D), F32), jax.ShapeDtypeStruct((nlat, D), F32)],
    )(z_dn, z_at, p, p, dm, w_bdn, w_bat)


def _swiglu_fn(ug, uv):
    return jax.nn.silu(ug) * uv


FFN_GROUP = 256


def _resident(shape):
    return pl.BlockSpec(shape, lambda i: (0,) * len(shape), pipeline_mode=pl.Buffered(1))


H_HALO = 16


def _up_project(h_refs, wu_ref, u_scr):
    cur_ref, prev_ref, next_ref = h_refs
    rows = jnp.concatenate([prev_ref[...], cur_ref[...], next_ref[...]], axis=0)
    u_scr[...] = jnp.dot(rows, wu_ref[...], preferred_element_type=F32)


def _up_ext_rows(u_scr, cols, keep, tb):
    xe = u_scr[H_HALO - HALO:H_HALO + tb + HALO, cols]
    r = lax.broadcasted_iota(jnp.int32, (tb + 2 * HALO, 1), 0)
    inside = ((r >= HALO) | keep[0]) & ((r < HALO + tb) | keep[1])
    return jnp.where(inside, xe, 0.0)


def _ffn_fwd(h, w_up, w8, bias, w_down):
    n = h.shape[0]
    tb = _tile(n, (256, 128))
    starts, ends = _segment_edges((n,), tb)

    def body(cur_ref, prev_ref, next_ref, wu_ref, w_ref, b_ref, wd_ref, o_ref, ff_ref, u_scr):
        keep = _keep_halos(pl.program_id(0), starts, ends)
        _up_project((cur_ref, prev_ref, next_ref), wu_ref, u_scr)
        for c0 in range(0, DFF, FFN_GROUP):
            halves = []
            for cols in (slice(c0, c0 + FFN_GROUP), slice(DFF + c0, DFF + c0 + FFN_GROUP)):
                xe = _up_ext_rows(u_scr, cols, keep, tb)
                halves.append(_conv_rows(_shifted_rows(xe, FFN_TAPS), w_ref, cols)[HALO:HALO + tb] + b_ref[:, cols])
            o_ref[:, c0:c0 + FFN_GROUP] = _swiglu_fn(*halves).astype(BF)
        ff_ref[...] = jnp.dot(o_ref[...], wd_ref[...], preferred_element_type=F32)

    return pl.pallas_call(
        body, grid=(n // tb,), name="ffn_fwd",
        in_specs=_halo_specs(tb, D, n, halo=H_HALO) + [_resident((D, 2 * DFF)), pl.BlockSpec((8, 2 * DFF), lambda i: (0, 0)),
                                                        pl.BlockSpec((1, 2 * DFF), lambda i: (0, 0)), _resident((DFF, D))],
        out_specs=[pl.BlockSpec((tb, DFF), lambda i: (i, 0)), pl.BlockSpec((tb, D), lambda i: (i, 0))],
        out_shape=[jax.ShapeDtypeStruct((n, DFF), BF), jax.ShapeDtypeStruct((n, D), F32)],
        scratch_shapes=[pltpu.VMEM((tb + 2 * H_HALO, 2 * DFF), F32)],
        compiler_params=_cp(),
    )(h, h, h, w_up, w8, bias, w_down)


def _ffn_bwd(h, w_up, w8, bias, da):
    n = h.shape[0]
    tb = _tile(n, (256, 128))
    starts, ends = _segment_edges((n,), tb)

    def body(cur_ref, prev_ref, next_ref, wu_ref, w_ref, b_ref, da_c, da_p, da_n, du_ref, dw_ref, db_ref, dh_ref, u_scr):
        i = pl.program_id(0)
        keep = _keep_halos(i, starts, ends)
        _up_project((cur_ref, prev_ref, next_ref), wu_ref, u_scr)

        @pl.when(i == 0)
        def _():
            dw_ref[...] = jnp.zeros_like(dw_ref)
            db_ref[...] = jnp.zeros_like(db_ref)

        for c0 in range(0, DFF, FFN_GROUP):
            col_pair = (slice(c0, c0 + FFN_GROUP), slice(DFF + c0, DFF + c0 + FFN_GROUP))
            shifts = [_shifted_rows(_up_ext_rows(u_scr, cols, keep, tb), FFN_TAPS) for cols in col_pair]
            convs = [_conv_rows(shifted, w_ref, cols) + b_ref[:, cols] for shifted, cols in zip(shifts, col_pair)]
            dae = _ext_rows((da_c, da_p, da_n), col_pair[0], keep)
            _, vjp = jax.vjp(_swiglu_fn, *convs)
            for shifted, cols, dce in zip(shifts, col_pair, vjp(dae)):
                du_ref[:, cols] = _conv_rows(_shifted_rows(dce, FFN_TAPS, transpose=True), w_ref, cols)[HALO:HALO + tb].astype(BF)
                dcur = dce[HALO:HALO + tb]
                for j, g in enumerate(_tap_grads(dcur, shifted, tb)):
                    dw_ref[j:j + 1, cols] += g
                db_ref[:, cols] += jnp.sum(dcur, axis=0, keepdims=True)
        dh_ref[...] = lax.dot_general(du_ref[...], wu_ref[...], (_DIMS["nt"], ((), ())), preferred_element_type=F32)

    wspec = pl.BlockSpec((8, 2 * DFF), lambda i: (0, 0))
    bspec = pl.BlockSpec((1, 2 * DFF), lambda i: (0, 0))
    return pl.pallas_call(
        body, grid=(n // tb,), name="ffn_bwd",
        in_specs=_halo_specs(tb, D, n, halo=H_HALO) + [_resident((D, 2 * DFF)), wspec, bspec] + _halo_specs(tb, DFF, n),
        out_specs=[pl.BlockSpec((tb, 2 * DFF), lambda i: (i, 0)), wspec, bspec, pl.BlockSpec((tb, D), lambda i: (i, 0))],
        out_shape=[jax.ShapeDtypeStruct((n, 2 * DFF), BF), jax.ShapeDtypeStruct((8, 2 * DFF), F32), jax.ShapeDtypeStruct((1, 2 * DFF), F32),
                   jax.ShapeDtypeStruct((n, D), F32)],
        scratch_shapes=[pltpu.VMEM((tb + 2 * H_HALO, 2 * DFF), F32)],
        compiler_params=_cp(),
    )(h, h, h, w_up, w8, bias, da, da, da)


def _loss_kernel(x1, gate, ff, target, w_down):
    n = x1.shape[0]
    tb = _tile(n, (256, 128))

    def body(x_ref, g_ref, f_ref, t_ref, wd_ref, loss_ref, dy_ref, dff_ref, dg_ref, da_ref):
        err = x_ref[...] + g_ref[...] * f_ref[...] - t_ref[...]
        dy = err * (1.0 / D)
        dy_ref[...] = dy
        dff_ref[...] = (g_ref[...] * dy).astype(BF)
        da_ref[...] = lax.dot_general(dff_ref[...], wd_ref[...], (_DIMS["nt"], ((), ())), preferred_element_type=F32)

        @pl.when(pl.program_id(0) == 0)
        def _():
            loss_ref[...] = jnp.zeros_like(loss_ref)
            dg_ref[...] = jnp.zeros_like(dg_ref)
        part = 0.5 * jnp.sum(jnp.sum(err * err, axis=1, keepdims=True) * (1.0 / D), axis=0, keepdims=True)
        loss_ref[...] += jnp.broadcast_to(part, (1, 128))
        dg_ref[...] += jnp.sum(dy * f_ref[...], axis=0, keepdims=True)

    row = pl.BlockSpec((tb, D), lambda i: (i, 0))
    one = pl.BlockSpec((1, D), lambda i: (0, 0))
    return pl.pallas_call(
        body, grid=(n // tb,), name="loss",
        in_specs=[row, one, row, row, _resident((DFF, D))],
        out_specs=[pl.BlockSpec((1, 128), lambda i: (0, 0)), row, row, one, pl.BlockSpec((tb, DFF), lambda i: (i, 0))],
        out_shape=[jax.ShapeDtypeStruct((1, 128), F32), jax.ShapeDtypeStruct((n, D), F32),
                   jax.ShapeDtypeStruct((n, D), BF), jax.ShapeDtypeStruct((1, D), F32), jax.ShapeDtypeStruct((n, DFF), F32)],
        compiler_params=_cp(),
    )(x1, gate, ff, target, w_down)


def _rope_tables(nlat, lc):
    t = jnp.arange(nlat)
    row = (t // GRID_W).astype(F32)
    col = (t % GRID_W).astype(F32)
    inv_freq = ROPE_BASE ** (-jnp.arange(32, dtype=F32) / 32)
    ar, ac = row[:, None] * inv_freq, col[:, None] * inv_freq
    cos = jnp.concatenate([jnp.cos(ar), jnp.cos(ar), jnp.cos(ac), jnp.cos(ac)], axis=1)
    sin = jnp.concatenate([-jnp.sin(ar), jnp.sin(ar), -jnp.sin(ac), jnp.sin(ac)], axis=1)
    cos = jnp.concatenate([cos, jnp.ones((lc, HD), F32)], axis=0)
    sin = jnp.concatenate([sin, jnp.zeros((lc, HD), F32)], axis=0)
    return cos, sin


def _pad_rows8(w):
    return jnp.concatenate([w, jnp.zeros((8 - w.shape[0], w.shape[1]), w.dtype)], axis=0)


def _pack_w_in(w):
    cuts = [sum(IN_SIZES[:i]) for i in range(len(IN_SIZES) + 1)]
    qkv, gt, b, a, q, k, v, mg = [w[:, cuts[i]:cuts[i + 1]] for i in range(len(IN_SIZES))]
    return jnp.concatenate([qkv, gt, q, mg, k, v, b, a, jnp.zeros((w.shape[0], PW - O_BA - 32), w.dtype)], axis=1)


def _unpack_w_in(g):
    return jnp.concatenate([g[:, O_QKV:O_GT], g[:, O_GT:O_Q], g[:, O_BA:O_BA + 32], g[:, O_Q:O_MG], g[:, O_K:O_V],
                            g[:, O_V:O_BA], g[:, O_MG:O_K]], axis=1)


def _local_step(x, ctx, mod_x, mod_c, target, project_in, project_back,
                norm_mix, norm_ffn, dn_conv, a_log, dt_bias, dn_norm, q_norm, k_norm, sink, ffn_conv, ffn_conv_b):
    L, LC = x.shape[0], ctx.shape[0]
    T = L + LC
    seg = lambda r: jnp.stack([mod_x[r], mod_c[r]])[:, None, :]
    sh_a, sc_a = seg(0), seg(1)
    g_a, g_f = mod_x[2][None], mod_x[5][None]
    sh_f, sc_f = mod_x[3][None], mod_x[4][None]
    cos, sin = _rope_tables(L, LC)
    dnc8 = _pad_rows8(dn_conv)
    ffc8 = _pad_rows8(ffn_conv)
    gate_row = lambda a: jnp.concatenate([jnp.zeros((1, 16), F32), a.reshape(1, 16), jnp.zeros((1, 96), F32)], axis=1)
    alog_row, dt_row = gate_row(a_log), gate_row(dt_bias)
    sinkb = jnp.concatenate([jnp.broadcast_to(sink.reshape(KVH, GRP, 1), (KVH, GRP, 128)), jnp.zeros((KVH, 8 - GRP, 128), F32)], axis=1)

    h1 = _norm_mod_fwd(x, ctx, norm_mix, sh_a, sc_a, "norm_mix_fwd")
    p, (w_in_p, w_bdn, w_bat, w_out, w_up, w_down) = project_in(h1)
    q, k, v, gb = _dn_pre_fwd(p, dnc8, alog_row, dt_row, (L, LC))
    wy = _dn1_fwd(q, k, v, gb)
    scans = _dn2_fwd([t[:6] for t in wy], L)
    o_dir = [s[0] for s in scans]
    y_dn, z_dn = _ghn_fwd(o_dir[0], o_dir[1], p, dn_norm, w_bdn, L)
    qr, kr, vb = _attn_prep_fwd(p, q_norm, k_norm, cos, sin)
    y_at, z_at = _attn_fwd(qr, kr, vb, sinkb, w_bat, L)
    merged, mix = _merge_fwd(z_dn, z_at, p, w_out, L)
    x1, h2 = _resid_norm_fwd(x, g_a, mix, norm_ffn, sh_f, sc_f)
    act, ff = _ffn_fwd(h2, w_up, ffc8, ffn_conv_b, w_down)
    loss_row, dy, dff, dg_f, dact = _loss_kernel(x1, g_f, ff, target, w_down)

    g_down = _mm(act, dff, form="tn", out_dtype=BF, name="g_ffn_down")
    du_raw, g_ffc8, g_ffb, dh2 = _ffn_bwd(h2, w_up, ffc8, ffn_conv_b, dact)
    g_up = _mm(h2, du_raw, form="tn", out_dtype=BF, name="g_ffn_up")
    dx1, dmix, dg_a, g_nffn, dsh_f, dsc_f, dmerged = _resid_norm_bwd(x1, g_a, mix, norm_ffn, sh_f, sc_f, dh2, dy, w_out)

    g_out = _mm(merged, dmix, form="tn", out_dtype=BF, name="g_w_out")
    dz_dn, dz_at, dmg, dy_dn, dy_at = _merge_bwd(z_dn, z_at, p, dmerged, w_bdn, w_bat, L)
    g_bdn = _mm(y_dn, dz_dn, form="tn", out_dtype=BF, name="g_branch_dn")
    g_bat = _mm(y_at, dz_at, form="tn", out_dtype=BF, name="g_branch_at")
    dqr, dkp, dvp, dkx, dvx, dsink = _attn_bwd(qr, kr, vb, sinkb, dy_at, L)
    dq_raw, dk_raw, dv_raw, g_qn, g_kn = _attn_prep_bwd(p, q_norm, k_norm, cos, sin, dqr, dkp, dvp, dkx, dvx, L)
    do, dgt, g_dnn = _ghn_bwd(o_dir[0], o_dir[1], p, dn_norm, dy_dn, L)
    cots = _dn2_bwd([wy[d][:6] + (scans[d][1],) for d in (0, 1)], do, L)
    dq, dk, dv, dgb = _dn1_bwd(q, k, v, gb, [t[6] for t in wy], cots)
    dqkv_raw, dba, g_dnc8, g_alog, g_dt = _dn_pre_bwd(p, dnc8, alog_row, dt_row, dq, dk, dv, dgb, (L, LC))
    dp = jnp.concatenate([dqkv_raw, dgt, dq_raw, dmg, dk_raw, dv_raw, dba, jnp.zeros((T, PW - O_BA - 128), BF)], axis=1)
    big, dh1 = project_back(h1, dp, w_in_p, (g_bdn, g_bat, g_out, g_up, g_down))
    grad_x, g_nmix_x, dsh_a, dsc_a = _norm_mod_bwd(x, norm_mix, mod_x[0][None], mod_x[1][None], dh1, row0=0,
                                                   name="norm_mix_bwd", residual=dx1)
    g_nmix_c, dsh_c, dsc_c = _norm_mod_bwd(ctx, norm_mix, mod_c[0][None], mod_c[1][None], dh1, row0=L, name="norm_mix_bwd_ctx")
    g_nmix = g_nmix_x + g_nmix_c

    zero = jnp.zeros((D,), F32)
    dmod_x = jnp.stack([dsh_a[0], dsc_a[0], dg_a[0], dsh_f[0], dsc_f[0], dg_f[0]])
    dmod_c = jnp.stack([dsh_c[0], dsc_c[0], zero, zero, zero, zero])
    small = dict(
        dmod_x=dmod_x, dmod_c=dmod_c, norm_mix=g_nmix, norm_ffn=g_nffn, dn_conv=g_dnc8[:5], dn_a_log=g_alog[0, 16:32].reshape(2, 8),
        dn_dt_bias=g_dt[0, 16:32].reshape(2, 8), dn_norm=g_dnn, q_norm=g_qn, k_norm=g_kn,
        attn_sink=jnp.sum(dsink[:, :GRP, :], axis=2).reshape(1, NH), ffn_conv=g_ffc8[:3], ffn_conv_b=g_ffb)
    return loss_row[0, 0], grad_x, big, small


def _exchange(arrays, scatter, name):
    n = len(arrays)

    def body(*refs):
        args = (refs[:n], refs[n:2 * n], *refs[2 * n:], scatter)
        _exchange_start(*args)
        _exchange_wait(*args)

    hbm = pl.BlockSpec(memory_space=pl.ANY)
    out_shape, sems = _exchange_shapes(arrays, scatter)
    return pl.pallas_call(body, name=name, in_specs=[hbm] * n, out_specs=[hbm] * n, out_shape=out_shape,
                          scratch_shapes=sems)(*arrays)


def _ada_fwd(c16, w_ada, b_ada):
    def body(c_ref, w_ref, b_ref, o_ref):
        o_ref[...] = _dot_hi(jax.nn.silu(c_ref[...]), w_ref[...]) + b_ref[...]

    return pl.pallas_call(body, name="ada_fwd", out_shape=jax.ShapeDtypeStruct((16, w_ada.shape[1]), F32))(c16, w_ada, b_ada)


def _ada_bwd(c16, w_ada, dmx, dmc):
    def body(c_ref, w_ref, dmx_ref, dmc_ref, gw_ref, pc_ref):
        dmc_tot = dmc_ref[0:1, :]
        for d in range(1, N_DEV):
            dmc_tot = dmc_tot + dmc_ref[d:d + 1, :]
        dm16 = jnp.concatenate([dmx_ref[...], jnp.broadcast_to(dmc_tot, (8, dmc_tot.shape[1]))], axis=0)
        row = lax.broadcasted_iota(jnp.int32, dm16.shape, 0)
        dm16 = jnp.where(row <= 8, dm16, 0.0)
        s = jax.nn.silu(c_ref[...])
        gw_ref[...] = lax.dot_general(s, dm16, (_DIMS["tn"], ((), ())), precision=HI, preferred_element_type=F32)
        pc = lax.dot_general(dm16, w_ref[...], (_DIMS["nt"], ((), ())), precision=HI, preferred_element_type=F32)
        pc_ref[...] = pc[8:9, :]

    return pl.pallas_call(body, name="ada_bwd", out_shape=[jax.ShapeDtypeStruct(w_ada.shape, F32), jax.ShapeDtypeStruct((1, D), F32)],
                          compiler_params=_cp())(c16, w_ada, dmx, dmc)


def _cctx_grad(pc_all, c_ctx_row):
    def body(pc_ref, c_ref, g_ref):
        tot = pc_ref[0]
        for d in range(1, N_DEV):
            tot = tot + pc_ref[d]
        _, vjp = jax.vjp(jax.nn.silu, c_ref[...])
        g_ref[...] = vjp(tot)[0]

    return pl.pallas_call(body, name="cctx_grad", out_shape=jax.ShapeDtypeStruct((1, D), F32))(pc_all, c_ctx_row)


def _adamw(parts, w, m, v, name):
    ns, R, C = parts.shape
    tb = _tile(R, (128, 64, 32, 16, 8))

    def body(p_ref, w_ref, m_ref, v_ref, g_ref, d_ref, mo_ref, vo_ref):
        g = p_ref[0].astype(F32)
        for s in range(1, ns):
            g = g + p_ref[s].astype(F32)
        m2 = ADAM_B1 * m_ref[...] + (1.0 - ADAM_B1) * g
        v2 = ADAM_B2 * v_ref[...] + (1.0 - ADAM_B2) * jnp.square(g)
        m_hat = m2 / (1.0 - ADAM_B1 ** ADAM_STEP)
        v_hat = v2 / (1.0 - ADAM_B2 ** ADAM_STEP)
        g_ref[...] = g
        d_ref[...] = -ADAM_LR * (m_hat / (jnp.sqrt(v_hat) + ADAM_EPS) + ADAM_WD * w_ref[...])
        mo_ref[...] = m2
        vo_ref[...] = v2

    row = pl.BlockSpec((tb, C), lambda i: (i, 0))
    return pl.pallas_call(
        body, grid=(R // tb,), name=name,
        in_specs=[pl.BlockSpec((ns, tb, C), lambda i: (0, i, 0)), row, row, row], out_specs=[row] * 4,
        out_shape=[jax.ShapeDtypeStruct((R, C), F32)] * 4, compiler_params=_cp(),
    )(parts, w, m, v)


_SMALL = (("dmod_x", 6 * D), ("dmod_c", 6 * D), ("b_ada", 6 * D), ("norm_mix", D), ("norm_ffn", D), ("dn_a_log", 16),
          ("dn_dt_bias", 16), ("dn_norm", HD), ("q_norm", HD), ("k_norm", HD), ("attn_sink", NH), ("ffn_conv_b", 2 * DFF),
          ("dn_conv", 5 * 3 * D), ("ffn_conv", 3 * 2 * DFF))
_SMALL_ROWS = -(-sum(n for _, n in _SMALL) // 1024) * 8


def _pack_small(d):
    flat = jnp.concatenate([d[k].reshape(-1).astype(F32) if k in d else jnp.zeros((n,), F32) for k, n in _SMALL])
    return jnp.concatenate([flat, jnp.zeros((_SMALL_ROWS * 128 - flat.shape[0],), F32)]).reshape(_SMALL_ROWS, 128)


def _unpack_small(a):
    flat = a.reshape(a.shape[:-2] + (-1,))
    out, off = {}, 0
    for k, n in _SMALL:
        out[k] = flat[..., off:off + n]
        off += n
    return out


def kernel(x, c, ctx, c_ctx, w_ada, b_ada, norm_mix, norm_ffn, w_in, dn_conv, dn_a_log, dn_dt_bias, dn_norm, q_norm, k_norm, attn_sink, w_branch_dn, w_branch_attn, w_out, ffn_up, ffn_conv, ffn_conv_b, ffn_down, loss_target, m_c_ctx, m_w_ada, m_b_ada, m_norm_mix, m_norm_ffn, m_w_in, m_dn_conv, m_dn_a_log, m_dn_dt_bias, m_dn_norm, m_q_norm, m_k_norm, m_attn_sink, m_w_branch_dn, m_w_branch_attn, m_w_out, m_ffn_up, m_ffn_conv, m_ffn_conv_b, m_ffn_down, v_c_ctx, v_w_ada, v_b_ada, v_norm_mix, v_norm_ffn, v_w_in, v_dn_conv, v_dn_a_log, v_dn_dt_bias, v_dn_norm, v_q_norm, v_k_norm, v_attn_sink, v_w_branch_dn, v_w_branch_attn, v_w_out, v_ffn_up, v_ffn_conv, v_ffn_conv_b, v_ffn_down):
    me = 4 * lax.axis_index("x") + 2 * lax.axis_index("y") + lax.axis_index("c")
    ada_cols = w_ada.shape[2]

    cols = lambda a: jnp.swapaxes(a, 0, 1).reshape(a.shape[1], -1)
    rows = lambda a: a.reshape(-1, a.shape[2])
    col_blocks = lambda g: jnp.swapaxes(g.reshape(g.shape[0], N_DEV, -1), 0, 1)
    row_blocks = lambda g: g.reshape(N_DEV, -1, g.shape[1])

    gathered = _exchange([w_in[0].astype(BF), c, dn_conv[0], ffn_conv[0]], scatter=False, name="gather_first")
    w_in_packed = _pack_w_in(cols(gathered[0]))
    c_all = gathered[1][:, 0, :]

    def project_in(h1):
        p, rest = _mm(h1, w_in_packed, form="nn", out_dtype=F32, name="in_proj",
                      exchange=([w_branch_dn[0].astype(BF), w_branch_attn[0].astype(BF), w_out[0].astype(BF),
                                 ffn_up[0].astype(BF), ffn_down[0].astype(BF)], False))
        return p, (w_in_packed, rows(rest[0]), rows(rest[1]), rows(rest[2]), cols(rest[3]), rows(rest[4]))

    def project_back(h1, dp, w_in_p, grads):
        g_bdn, g_bat, g_out, g_up, g_down = grads
        g_in, landed_rest = _mm(h1, dp, form="tn", out_dtype=BF, name="g_w_in",
                                exchange=([row_blocks(g_bdn), row_blocks(g_bat), row_blocks(g_out), col_blocks(g_up),
                                           row_blocks(g_down)], True))
        dh1, landed_in = _mm(dp, w_in_p, form="nt", out_dtype=F32, name="d_h1",
                             exchange=([col_blocks(_unpack_w_in(g_in))], True))
        return [landed_in[0]] + landed_rest, dh1

    c16 = jnp.concatenate([c_all, c_ctx[None], jnp.zeros((7, D), F32)], axis=0)
    b_loc = lax.dynamic_slice_in_dim(b_ada, me * ada_cols, ada_cols, axis=1)
    mod_part = _ada_fwd(c16, w_ada[0], b_loc)
    mod_all = cols(_exchange([mod_part], scatter=False, name="gather_mod")[0])
    mod_x = lax.dynamic_slice_in_dim(mod_all, me, 1, axis=0).reshape(6, D)
    mod_c = mod_all[8].reshape(6, D)

    loss_loc, grad_x, landed, small = _local_step(
        x[0], ctx[0], mod_x, mod_c, loss_target[0], project_in, project_back,
        norm_mix, norm_ffn, cols(gathered[2]), dn_a_log[0], dn_dt_bias[0], dn_norm, q_norm, k_norm, attn_sink[0], cols(gathered[3]),
        ffn_conv_b)
    loss = lax.psum(loss_loc, ("x", "y", "c"))

    res = {}
    res["w_in"] = _adamw(landed[0], w_in[0], m_w_in[0], v_w_in[0], "adamw_w_in")
    res["w_branch_dn"] = _adamw(landed[1], w_branch_dn[0], m_w_branch_dn[0], v_w_branch_dn[0], "adamw_w_branch_dn")
    res["w_branch_attn"] = _adamw(landed[2], w_branch_attn[0], m_w_branch_attn[0], v_w_branch_attn[0], "adamw_w_branch_attn")
    res["w_out"] = _adamw(landed[3], w_out[0], m_w_out[0], v_w_out[0], "adamw_w_out")
    res["ffn_up"] = _adamw(landed[4], ffn_up[0], m_ffn_up[0], v_ffn_up[0], "adamw_ffn_up")
    res["ffn_down"] = _adamw(landed[5], ffn_down[0], m_ffn_down[0], v_ffn_down[0], "adamw_ffn_down")

    small = dict(small)
    small["b_ada"] = small["dmod_x"] + small["dmod_c"]
    parts = _exchange([_pack_small(small)], scatter=False, name="gather_small")[0]
    per_dev = _unpack_small(parts)
    given = dict(b_ada=(b_ada, m_b_ada, v_b_ada), norm_mix=(norm_mix, m_norm_mix, v_norm_mix), norm_ffn=(norm_ffn, m_norm_ffn, v_norm_ffn),
                 dn_a_log=(dn_a_log, m_dn_a_log, v_dn_a_log), dn_dt_bias=(dn_dt_bias, m_dn_dt_bias, v_dn_dt_bias),
                 dn_norm=(dn_norm, m_dn_norm, v_dn_norm), q_norm=(q_norm, m_q_norm, v_q_norm), k_norm=(k_norm, m_k_norm, v_k_norm),
                 attn_sink=(attn_sink, m_attn_sink, v_attn_sink), ffn_conv_b=(ffn_conv_b, m_ffn_conv_b, v_ffn_conv_b))
    packs = [_pack_small({k: t[j] for k, t in given.items()}) for j in range(3)]
    upd = [_unpack_small(a) for a in _adamw(parts, packs[0], packs[1], packs[2], "adamw_small")]
    for k, t in given.items():
        res[k] = tuple(u[k].reshape(t[0].shape) for u in upd)
    dnc = lax.dynamic_slice_in_dim(upd[0]["dn_conv"].reshape(5, 3 * D), me * dn_conv.shape[2], dn_conv.shape[2], axis=1)
    ffc = lax.dynamic_slice_in_dim(upd[0]["ffn_conv"].reshape(3, 2 * DFF), me * ffn_conv.shape[2], ffn_conv.shape[2], axis=1)
    r8 = lambda a: _pad_rows8(a)
    t = _adamw(r8(dnc)[None], r8(dn_conv[0]), r8(m_dn_conv[0]), r8(v_dn_conv[0]), "adamw_dn_conv")
    res["dn_conv"] = tuple(a[:5][None] for a in t)
    t = _adamw(r8(ffc)[None], r8(ffn_conv[0]), r8(m_ffn_conv[0]), r8(v_ffn_conv[0]), "adamw_ffn_conv")
    res["ffn_conv"] = tuple(a[:3][None] for a in t)

    dmx = lax.dynamic_slice_in_dim(per_dev["dmod_x"], me * ada_cols, ada_cols, axis=1)
    dmc = lax.dynamic_slice_in_dim(per_dev["dmod_c"], me * ada_cols, ada_cols, axis=1)
    g_ada, pc = _ada_bwd(c16, w_ada[0], dmx, dmc)
    res["w_ada"] = _adamw(g_ada[None], w_ada[0], m_w_ada[0], v_w_ada[0], "adamw_w_ada")
    pc_all = _exchange([pc], scatter=False, name="gather_cctx")[0]
    g_cctx = _cctx_grad(pc_all, c_ctx[None])
    r8b = lambda a: jnp.broadcast_to(a, (8, D))
    t = _adamw(r8b(g_cctx)[None], r8b(c_ctx[None]), r8b(m_c_ctx[None]), r8b(v_c_ctx[None]), "adamw_c_ctx")
    res["c_ctx"] = tuple(a[0] for a in t)

    names = ("c_ctx", "w_ada", "b_ada", "norm_mix", "norm_ffn", "w_in", "dn_conv", "dn_a_log", "dn_dt_bias", "dn_norm", "q_norm",
             "k_norm", "attn_sink", "w_branch_dn", "w_branch_attn", "w_out", "ffn_up", "ffn_conv", "ffn_conv_b", "ffn_down")
    lead = ("w_ada", "w_in", "w_branch_dn", "w_branch_attn", "w_out", "ffn_up", "ffn_down")
    fix = lambda k, a: a[None] if k in lead else a
    outs = [loss, grad_x[None]]
    for j in range(4):
        outs += [fix(k, res[k][j]) for k in names]
    return tuple(outs)
```

```python
import functools

import jax
import jax.numpy as jnp
from jax import lax
from jax.experimental import pallas as pl
from jax.experimental.pallas import tpu as pltpu

F32 = jnp.float32
BF = jnp.bfloat16
HI = lax.Precision.HIGHEST
MESH = pl.DeviceIdType.MESH

D = 1024
NH = 8
HD = 128
KVH = 2
GRP = 4
KV = KVH * HD
DFF = 2816
CB = 128
GRID_W = 64
ROPE_BASE = 10000.0
EPS = 1e-6
N_DEV = 8
PW = 8192
O_QKV, O_GT, O_Q, O_MG, O_K, O_V, O_BA = 0, 3072, 4096, 5120, 7168, 7424, 7680
IN_SIZES = (3072, 1024, 16, 16, 1024, 256, 256, 2048)
IN_DIM = sum(IN_SIZES)
ADAM_LR, ADAM_B1, ADAM_B2, ADAM_EPS, ADAM_WD, ADAM_STEP = 0.001, 0.9, 0.999, 1e-08, 0.01, 10
VMEM_LIMIT = 56 * 1024 * 1024


def _cp():
    return pltpu.CompilerParams(vmem_limit_bytes=VMEM_LIMIT)


def _tile(n, cands):
    for c in cands:
        if n % c == 0:
            return c
    return n


def _iota2(shape):
    return lax.broadcasted_iota(jnp.int32, shape, 0), lax.broadcasted_iota(jnp.int32, shape, 1)


_DIMS = {"nn": ((1,), (0,)), "nt": ((1,), (1,)), "tn": ((0,), (0,))}


def _exchange_copies(ins, outs, send_sems, recv_sems, local_sems, scatter, landings):
    x, y, c = lax.axis_index("x"), lax.axis_index("y"), lax.axis_index("c")
    me = 4 * x + 2 * y + c
    local, remote = [], []
    for k in range(len(ins)):
        local.append(pltpu.make_async_copy(ins[k].at[me] if scatter else ins[k], outs[k].at[me], local_sems.at[k]))
        for m in range(1, N_DEV):
            px = 1 - x if m & 4 else x
            py = 1 - y if m & 2 else y
            pc = 1 - c if m & 1 else c
            peer = 4 * px + 2 * py + pc
            src = ins[k].at[peer] if scatter else ins[k]
            sem = k * (N_DEV - 1) + m - 1
            push = pltpu.make_async_remote_copy(src_ref=src, dst_ref=outs[k].at[me], send_sem=send_sems.at[sem],
                                                recv_sem=recv_sems.at[sem], device_id=(px, py, pc), device_id_type=MESH)
            landing = None
            if landings:
                landing = pltpu.make_async_remote_copy(src_ref=src, dst_ref=outs[k].at[peer], send_sem=send_sems.at[sem],
                                                       recv_sem=recv_sems.at[sem], device_id=(px, py, pc), device_id_type=MESH)
            remote.append((push, landing))
    return local, remote


def _exchange_start(*args):
    local, remote = _exchange_copies(*args, landings=False)
    for cp in local:
        cp.start()
    for push, _ in remote:
        push.start()


def _exchange_wait(*args):
    local, remote = _exchange_copies(*args, landings=True)
    for _, landing in remote:
        landing.wait_recv()
    for push, _ in remote:
        push.wait_send()
    for cp in local:
        cp.wait()


def _exchange_shapes(arrays, scatter):
    out_shape = [jax.ShapeDtypeStruct(a.shape if scatter else (N_DEV,) + a.shape, a.dtype) for a in arrays]
    n = len(arrays)
    sems = [pltpu.SemaphoreType.DMA((n * (N_DEV - 1),)), pltpu.SemaphoreType.DMA((n * (N_DEV - 1),)), pltpu.SemaphoreType.DMA((n,))]
    return out_shape, sems


def _mm(a, b, *, form, out_dtype, name, tm=None, tn=None, tk=None, exchange=None):
    if form == "tn":
        K, M = a.shape
        N = b.shape[1]
    else:
        M, K = a.shape
        N = b.shape[0] if form == "nt" else b.shape[1]
    tm = tm or _tile(M, (1408, 1280, 1024, 640, 512, 256, 128))
    tn = tn or _tile(N, (1408, 1024, 512, 256, 128))
    tk = tk or _tile(K, (2048, 1408, 1280, 1024, 640, 512, 256, 128))
    ni, nj, nk = M // tm, N // tn, K // tk
    dims = (_DIMS[form], ((), ()))
    ex_arrays, scatter = exchange if exchange else ([], False)
    nx = len(ex_arrays)

    def body(a_ref, b_ref, *refs):
        ex_in, o_ref, ex_out, scratch = refs[:nx], refs[nx], refs[nx + 1:2 * nx + 1], refs[2 * nx + 1:]
        i, j, k = pl.program_id(0), pl.program_id(1), pl.program_id(2)
        if nx:
            sems = scratch[-3:]

            @pl.when((i == 0) & (j == 0) & (k == 0))
            def _():
                _exchange_start(ex_in, ex_out, *sems, scatter)

        part = lax.dot_general(a_ref[...].astype(BF), b_ref[...].astype(BF), dims, preferred_element_type=F32)
        if nk == 1:
            o_ref[...] = part.astype(out_dtype)
        else:
            acc_ref = scratch[0]

            @pl.when(k == 0)
            def _():
                acc_ref[...] = part

            @pl.when(k > 0)
            def _():
                acc_ref[...] += part

            @pl.when(k == nk - 1)
            def _():
                o_ref[...] = acc_ref[...].astype(out_dtype)

        if nx:
            @pl.when((i == ni - 1) & (j == nj - 1) & (k == nk - 1))
            def _():
                _exchange_wait(ex_in, ex_out, *sems, scatter)

    if form == "tn":
        a_spec = pl.BlockSpec((tk, tm), lambda i, j, k: (k, i))
    else:
        a_spec = pl.BlockSpec((tm, tk), lambda i, j, k: (i, k))
    if form == "nt":
        b_spec = pl.BlockSpec((tn, tk), lambda i, j, k: (j, k))
    else:
        b_spec = pl.BlockSpec((tk, tn), lambda i, j, k: (k, j))
    hbm = pl.BlockSpec(memory_space=pl.ANY)
    ex_shapes, ex_sems = _exchange_shapes(ex_arrays, scatter) if nx else ([], [])
    outs = pl.pallas_call(
        body, grid=(ni, nj, nk), name=name,
        in_specs=[a_spec, b_spec] + [hbm] * nx, out_specs=[pl.BlockSpec((tm, tn), lambda i, j, k: (i, j))] + [hbm] * nx,
        out_shape=[jax.ShapeDtypeStruct((M, N), out_dtype)] + ex_shapes,
        scratch_shapes=([] if nk == 1 else [pltpu.VMEM((tm, tn), F32)]) + ex_sems,
        compiler_params=_cp(),
    )(a, b, *ex_arrays)
    return (outs[0], list(outs[1:])) if nx else outs[0]


def _norm_mod_fn(x, nw, sh, sc):
    y = x * lax.rsqrt(jnp.mean(x * x, axis=-1, keepdims=True) + EPS)
    return (y * nw) * (1.0 + sc) + sh


def _norm_mod_fwd(x, ctx, nw, sh, sc, name):
    nlat = x.shape[0]
    T = nlat + ctx.shape[0]
    tb = _tile(ctx.shape[0], (256, 128))
    nlb = nlat // tb

    def body(x_ref, c_ref, nw_ref, sh_ref, sc_ref, h_ref):
        rows = jnp.where(pl.program_id(0) < nlb, x_ref[...], c_ref[...])
        h_ref[...] = _norm_mod_fn(rows, nw_ref[...], sh_ref[0], sc_ref[0]).astype(BF)

    seg = pl.BlockSpec((1, 1, D), lambda i: (jnp.where(i >= nlb, 1, 0), 0, 0))
    return pl.pallas_call(
        body, grid=(T // tb,), name=name,
        in_specs=[pl.BlockSpec((tb, D), lambda i: (jnp.minimum(i, nlb - 1), 0)),
                  pl.BlockSpec((tb, D), lambda i: (jnp.maximum(i - nlb, 0), 0)), pl.BlockSpec((1, D), lambda i: (0, 0)), seg, seg],
        out_specs=pl.BlockSpec((tb, D), lambda i: (i, 0)),
        out_shape=jax.ShapeDtypeStruct((T, D), BF),
    )(x, ctx, nw, sh, sc)


def _norm_mod_bwd(x, nw, sh, sc, dh, *, row0, name, residual=None):
    nrows = x.shape[0]
    tb = _tile(nrows, (256, 128))
    b0 = row0 // tb

    def body(x_ref, nw_ref, sh_ref, sc_ref, dh_ref, *refs):
        dnw_ref, dsh_ref, dsc_ref = refs[-3:]
        _, vjp = jax.vjp(_norm_mod_fn, x_ref[...], nw_ref[...], sh_ref[...], sc_ref[...])
        dx, dnw, dsh, dsc = vjp(dh_ref[...])
        if residual is not None:
            refs[1][...] = dx + refs[0][...]

        @pl.when(pl.program_id(0) == 0)
        def _():
            dnw_ref[...] = jnp.zeros_like(dnw_ref)
            dsh_ref[...] = jnp.zeros_like(dsh_ref)
            dsc_ref[...] = jnp.zeros_like(dsc_ref)

        dnw_ref[...] += dnw
        dsh_ref[...] += dsh
        dsc_ref[...] += dsc

    dh_row = pl.BlockSpec((tb, D), lambda i: (b0 + i, 0))
    out_row = pl.BlockSpec((tb, D), lambda i: (i, 0))
    one = pl.BlockSpec((1, D), lambda i: (0, 0))
    with_dx = residual is not None
    return pl.pallas_call(
        body, grid=(nrows // tb,), name=name,
        in_specs=[out_row, one, one, one, dh_row] + [out_row] * with_dx, out_specs=[out_row] * with_dx + [one] * 3,
        out_shape=[jax.ShapeDtypeStruct((nrows, D), F32)] * with_dx + [jax.ShapeDtypeStruct((1, D), F32)] * 3,
    )(x, nw, sh, sc, dh, *([residual] if with_dx else []))


def _resid_norm_fwd(x, gate, y, nw, sh, sc):
    n = y.shape[0]
    tb = _tile(n, (256, 128))

    def body(x_ref, g_ref, y_ref, nw_ref, sh_ref, sc_ref, x1_ref, h_ref):
        x1 = x_ref[...] + g_ref[...] * y_ref[...]
        x1_ref[...] = x1
        h_ref[...] = _norm_mod_fn(x1, nw_ref[...], sh_ref[...], sc_ref[...]).astype(BF)

    row = pl.BlockSpec((tb, D), lambda i: (i, 0))
    one = pl.BlockSpec((1, D), lambda i: (0, 0))
    return pl.pallas_call(
        body, grid=(n // tb,), name="resid_norm_fwd",
        in_specs=[row, one, row, one, one, one], out_specs=[row, row],
        out_shape=[jax.ShapeDtypeStruct((n, D), F32), jax.ShapeDtypeStruct((n, D), BF)],
    )(x, gate, y, nw, sh, sc)


def _resid_norm_bwd(x1, gate, y, nw, sh, sc, dh, dx1_direct, w_out):
    n = y.shape[0]
    tb = _tile(n, (256, 128))

    def body(x1_ref, g_ref, y_ref, nw_ref, sh_ref, sc_ref, dh_ref, dd_ref, wo_ref,
             dx_ref, dy_ref, dg_ref, dnw_ref, dsh_ref, dsc_ref, dm_ref):
        _, vjp = jax.vjp(_norm_mod_fn, x1_ref[...], nw_ref[...], sh_ref[...], sc_ref[...])
        dxn, dnw, dsh, dsc = vjp(dh_ref[...])
        dx = dxn + dd_ref[...]
        dx_ref[...] = dx
        dy_ref[...] = (g_ref[...] * dx).astype(BF)
        dm_ref[...] = lax.dot_general(dy_ref[...], wo_ref[...], (_DIMS["nt"], ((), ())), preferred_element_type=F32)

        @pl.when(pl.program_id(0) == 0)
        def _():
            for r in (dg_ref, dnw_ref, dsh_ref, dsc_ref):
                r[...] = jnp.zeros_like(r)

        dg_ref[...] += jnp.sum(dx * y_ref[...], axis=0, keepdims=True)
        dnw_ref[...] += dnw
        dsh_ref[...] += dsh
        dsc_ref[...] += dsc

    row = pl.BlockSpec((tb, D), lambda i: (i, 0))
    one = pl.BlockSpec((1, D), lambda i: (0, 0))
    return pl.pallas_call(
        body, grid=(n // tb,), name="resid_norm_bwd",
        in_specs=[row, one, row, one, one, one, row, row, _resident((D, D))], out_specs=[row, row] + [one] * 4 + [row],
        out_shape=[jax.ShapeDtypeStruct((n, D), F32), jax.ShapeDtypeStruct((n, D), BF)] + [jax.ShapeDtypeStruct((1, D), F32)] * 4
        + [jax.ShapeDtypeStruct((n, D), F32)],
    )(x1, gate, y, nw, sh, sc, dh, dx1_direct, w_out)


HALO = 8


def _halo_specs(tb, width, nrows, col=0, halo=HALO):
    r8 = tb // halo
    cur = pl.BlockSpec((tb, width), lambda i: (i, col))
    prev = pl.BlockSpec((halo, width), lambda i: (jnp.maximum(i * r8 - 1, 0), col))
    nxt = pl.BlockSpec((halo, width), lambda i: (jnp.minimum((i + 1) * r8, nrows // halo - 1), col))
    return [cur, prev, nxt]


def _segment_edges(seg_rows, tb):
    bounds = [0]
    for s in seg_rows:
        bounds.append(bounds[-1] + s // tb)
    return bounds[:-1], [b - 1 for b in bounds[1:]]


def _keep_halos(i, starts, ends):
    keep_p = functools.reduce(lambda a, b: a & b, [i != s for s in starts])
    keep_n = functools.reduce(lambda a, b: a & b, [i != e for e in ends])
    return keep_p, keep_n


def _ext_rows(refs, cols, keep):
    cur_ref, prev_ref, next_ref = refs
    p = jnp.where(keep[0], prev_ref[:, cols].astype(F32), 0.0)
    n = jnp.where(keep[1], next_ref[:, cols].astype(F32), 0.0)
    return jnp.concatenate([p, cur_ref[:, cols].astype(F32), n], axis=0)


def _shifted_rows(xe, width, transpose=False):
    r = width // 2
    n = xe.shape[0]
    out = []
    for j in range(width):
        s = ((j - r) if transpose else (r - j)) % n
        out.append(xe if s == 0 else pltpu.roll(xe, s, 0))
    return out


def _conv_rows(shifted, w_ref, cols):
    acc = None
    for j, xs in enumerate(shifted):
        term = xs * w_ref[j:j + 1, cols]
        acc = term if acc is None else acc + term
    return acc


def _tap_grads(dcur, shifted, tb):
    return [jnp.sum(dcur * xs[HALO:HALO + tb], axis=0, keepdims=True) for xs in shifted]


def _softplus(x):
    return jnp.maximum(x, 0.0) + jnp.log(1.0 + jnp.exp(-jnp.abs(x)))


def _gates_fn(ba, alog_row, dt_row):
    col = lax.broadcasted_iota(jnp.int32, ba.shape, 1)
    beta = jax.nn.sigmoid(ba)
    g = -jnp.exp(alog_row) * _softplus(ba + dt_row)
    return jnp.where(col < 16, beta, jnp.where(col < 32, g, 0.0))


def _qkv_post_fn(c, kind):
    y = jax.nn.silu(c)
    if kind == 2:
        return y
    n = y * lax.rsqrt(jnp.sum(y * y, axis=-1, keepdims=True) + EPS)
    return n * (HD ** -0.5) if kind == 0 else n


DN_TAPS = 5
FFN_TAPS = 3


def _dn_pre_fwd(p, w8, alog_row, dt_row, seg_rows):
    T = p.shape[0]
    tb = _tile(T, (256, 128))
    starts, ends = _segment_edges(seg_rows, tb)

    def body(cur_ref, prev_ref, next_ref, ba_ref, w_ref, al_ref, dt_ref, q_ref, k_ref, v_ref, gb_ref):
        keep = _keep_halos(pl.program_id(0), starts, ends)
        outs = (q_ref, k_ref, v_ref)
        for kind in range(3):
            for h in range(NH):
                cols = slice(kind * D + h * HD, kind * D + (h + 1) * HD)
                xe = _ext_rows((cur_ref, prev_ref, next_ref), cols, keep)
                conv = _conv_rows(_shifted_rows(xe, DN_TAPS), w_ref, cols)[HALO:HALO + tb]
                outs[kind][:, h * HD:(h + 1) * HD] = _qkv_post_fn(conv, kind)
        gb_ref[...] = _gates_fn(ba_ref[...], al_ref[...], dt_ref[...])

    row = pl.BlockSpec((tb, D), lambda i: (i, 0))
    one = pl.BlockSpec((1, 128), lambda i: (0, 0))
    return pl.pallas_call(
        body, grid=(T // tb,), name="dn_pre_fwd",
        in_specs=_halo_specs(tb, 3 * D, T) + [pl.BlockSpec((tb, 128), lambda i: (i, O_BA // 128)),
                                              pl.BlockSpec((8, 3 * D), lambda i: (0, 0)), one, one],
        out_specs=[row, row, row, pl.BlockSpec((tb, 128), lambda i: (i, 0))],
        out_shape=[jax.ShapeDtypeStruct((T, D), F32)] * 3 + [jax.ShapeDtypeStruct((T, 128), F32)],
        compiler_params=_cp(),
    )(p, p, p, p, w8, alog_row, dt_row)


def _dn_pre_bwd(p, w8, alog_row, dt_row, dq, dk, dv, dgb, seg_rows):
    T = p.shape[0]
    tb = _tile(T, (256, 128))
    starts, ends = _segment_edges(seg_rows, tb)

    def body(cur_ref, prev_ref, next_ref, ba_ref, w_ref, al_ref, dt_ref,
             dq_c, dq_p, dq_n, dk_c, dk_p, dk_n, dv_c, dv_p, dv_n, dgb_ref, dx_ref, dba_ref, dw_ref, dal_ref, ddt_ref):
        i = pl.program_id(0)
        keep = _keep_halos(i, starts, ends)

        @pl.when(i == 0)
        def _():
            dw_ref[...] = jnp.zeros_like(dw_ref)
            dal_ref[...] = jnp.zeros_like(dal_ref)
            ddt_ref[...] = jnp.zeros_like(ddt_ref)

        douts = ((dq_c, dq_p, dq_n), (dk_c, dk_p, dk_n), (dv_c, dv_p, dv_n))
        for kind in range(3):
            for h in range(NH):
                cols = slice(kind * D + h * HD, kind * D + (h + 1) * HD)
                xe = _ext_rows((cur_ref, prev_ref, next_ref), cols, keep)
                shifted = _shifted_rows(xe, DN_TAPS)
                conv = _conv_rows(shifted, w_ref, cols)
                dye = _ext_rows(douts[kind], slice(h * HD, (h + 1) * HD), keep)
                _, vjp = jax.vjp(functools.partial(_qkv_post_fn, kind=kind), conv)
                dce = vjp(dye)[0]
                dx_ref[:, cols] = _conv_rows(_shifted_rows(dce, DN_TAPS, transpose=True), w_ref, cols)[HALO:HALO + tb].astype(BF)
                for j, g in enumerate(_tap_grads(dce[HALO:HALO + tb], shifted, tb)):
                    dw_ref[j:j + 1, cols] += g
        _, vjp = jax.vjp(_gates_fn, ba_ref[...], al_ref[...], dt_ref[...])
        dba, dal, ddt = vjp(dgb_ref[...])
        dba_ref[...] = dba.astype(BF)
        dal_ref[...] += dal
        ddt_ref[...] += ddt

    one = pl.BlockSpec((1, 128), lambda i: (0, 0))
    nar = pl.BlockSpec((tb, 128), lambda i: (i, 0))
    wspec = pl.BlockSpec((8, 3 * D), lambda i: (0, 0))
    return pl.pallas_call(
        body, grid=(T // tb,), name="dn_pre_bwd",
        in_specs=_halo_specs(tb, 3 * D, T) + [pl.BlockSpec((tb, 128), lambda i: (i, O_BA // 128)), wspec, one, one]
        + _halo_specs(tb, D, T) * 3 + [nar],
        out_specs=[pl.BlockSpec((tb, 3 * D), lambda i: (i, 0)), nar, wspec, one, one],
        out_shape=[jax.ShapeDtypeStruct((T, 3 * D), BF), jax.ShapeDtypeStruct((T, 128), BF), jax.ShapeDtypeStruct((8, 3 * D), F32),
                   jax.ShapeDtypeStruct((1, 128), F32), jax.ShapeDtypeStruct((1, 128), F32)],
        compiler_params=_cp(),
    )(p, p, p, p, w8, alog_row, dt_row, dq, dq, dq, dk, dk, dk, dv, dv, dv, dgb)


def _dot_hi(a, b):
    return jnp.dot(a, b, precision=HI, preferred_element_type=F32)


def _dot_bf(a, b):
    return jnp.dot(a.astype(BF), b.astype(BF), preferred_element_type=F32)


def _dot_nt_bf(a, b):
    return lax.dot_general(a.astype(BF), b.astype(BF), (_DIMS["nt"], ((), ())), preferred_element_type=F32)


def _dot_tn_bf(a, b):
    return lax.dot_general(a.astype(BF), b.astype(BF), (_DIMS["tn"], ((), ())), preferred_element_type=F32)


def _dot_h3(a, b):
    return jnp.dot(a, b, precision=lax.Precision.HIGH, preferred_element_type=F32)


def _dot_split(fine, coarse, form):
    hi = fine.astype(BF)
    lo = (fine - hi.astype(F32)).astype(BF)
    cb = coarse.astype(BF)
    if form == "tn":
        return lax.dot_general(jnp.concatenate([cb, cb], axis=0), jnp.concatenate([hi, lo], axis=0),
                               (_DIMS["tn"], ((), ())), preferred_element_type=F32)
    parts = jnp.concatenate([hi, lo], axis=1)
    if form == "nt":
        return lax.dot_general(parts, jnp.concatenate([cb, cb], axis=1), (_DIMS["nt"], ((), ())), preferred_element_type=F32)
    return jnp.dot(parts, jnp.concatenate([cb, cb], axis=0), preferred_element_type=F32)


@jax.custom_vjp
def _mm_split(a, b):
    return _dot_split(a, b, "nn")


_mm_split.defvjp(lambda a, b: (_dot_split(a, b, "nn"), (a, b)),
                 lambda res, dc: (_dot_split(dc, res[1], "nt"), _dot_split(dc, res[0], "tn")))


def _unit_tri_inverses(mats):
    r, c = _iota2((CB, CB))
    eye = (r == c).astype(F32)
    a8 = [jnp.where((r // 8) == (c // 8), a, 0.0) for a in mats]
    a2 = [_dot_split(x, x, "nn") for x in a8]
    a4 = [_dot_split(x, x, "nn") for x in a2]
    t = [_dot_split(eye - x, eye + y, "nn") for x, y in zip(a8, a2)]
    t = [_dot_split(x, eye + y, "nn") for x, y in zip(t, a4)]
    b = 8
    while b < CB:
        mask = ((r // (2 * b)) == (c // (2 * b))) & ((r // b) != (c // b))
        te = [_dot_split(x, jnp.where(mask, a, 0.0), "nn") for x, a in zip(t, mats)]
        t = [x - _dot_split(y, x, "nn") for x, y in zip(t, te)]
        b *= 2
    return t


@jax.custom_vjp
def _saved_inverse(a, t):
    return t


_saved_inverse.defvjp(lambda a, t: (t, t),
                      lambda t, dt: (-_dot_split(_dot_split(dt, t, "nt"), t, "tn"), jnp.zeros_like(t)))


def _dn1_decay(gc, reverse):
    r, c = _iota2((CB, CB))
    incl = (c >= r) if reverse else (c <= r)
    return jnp.where(incl, jnp.exp(jnp.where(incl, gc - gc.T, 0.0)), 0.0)


def _dn1_heads(qs, ks, vs, betas, gcs, ts_saved, reverse, kks=None, qks=None):
    r, c = _iota2((CB, CB))
    strict = (c > r) if reverse else (c < r)
    decays = [_dn1_decay(gc, reverse) for gc in gcs]
    kks = kks or [_dot_nt_bf(k, k) for k in ks]
    systems = [jnp.where(strict, b * kk * dc, 0.0) for b, kk, dc in zip(betas, kks, decays)]
    if ts_saved is None:
        ts = _unit_tri_inverses(systems)
    else:
        ts = [_saved_inverse(a, t) for a, t in zip(systems, ts_saved)]
    egs = [jnp.exp(gc) for gc in gcs]
    us = [_mm_split(t, v * b) for t, v, b in zip(ts, vs, betas)]
    ws = [_mm_split(t, k * (b * eg)) for t, k, b, eg in zip(ts, ks, betas, egs)]
    qks = qks or [_dot_nt_bf(q, k) for q, k in zip(qs, ks)]
    last = 0 if reverse else CB - 1
    glogs = [jnp.sum(jnp.where(r == last, gc, 0.0), axis=0, keepdims=True) for gc in gcs]
    outs = [(u, w, q * eg, k * jnp.exp(gl - gc), qk * dc, jnp.exp(gl))
            for u, w, q, k, eg, gl, gc, qk, dc in zip(us, ws, qs, ks, egs, glogs, gcs, qks, decays)]
    return outs, ts


def _cum_matrix(upper):
    r, c = _iota2((CB, CB))
    return ((c >= r) if upper else (c <= r)).astype(F32)


def _lane_bcast(x, col):
    return jnp.broadcast_to(x[:, col:col + 1], x.shape)


_HEAD_SLICES = [slice(h * HD, (h + 1) * HD) for h in range(NH)]


def _dn1_fwd(q, k, v, gb):
    T = q.shape[0]
    nb = T // CB

    def body(q_ref, k_ref, v_ref, gb_ref, *out_refs):
        gbv = gb_ref[...]
        qs = [q_ref[:, sl] for sl in _HEAD_SLICES]
        ks = [k_ref[:, sl] for sl in _HEAD_SLICES]
        vs = [v_ref[:, sl] for sl in _HEAD_SLICES]
        kks = [_dot_nt_bf(x, x) for x in ks]
        qks = [_dot_nt_bf(x, y) for x, y in zip(qs, ks)]
        for d in (0, 1):
            u_ref, w_ref, qg_ref, kd_ref, qkd_ref, gl_ref, t_ref = out_refs[7 * d:7 * d + 7]
            gcum = _dot_h3(_cum_matrix(d == 1), gbv)
            betas = [_lane_bcast(gbv, d * NH + h) for h in range(NH)]
            gcs = [_lane_bcast(gcum, 16 + d * NH + h) for h in range(NH)]
            outs, ts = _dn1_heads(qs, ks, vs, betas, gcs, None, d == 1, kks, qks)
            for h, sl in enumerate(_HEAD_SLICES):
                u, w, qg, kd, qkd, gl = outs[h]
                u_ref[:, sl] = u
                w_ref[:, sl] = w.astype(BF)
                qg_ref[:, sl] = qg.astype(BF)
                kd_ref[:, sl] = kd.astype(BF)
                qkd_ref[:, sl] = qkd.astype(BF)
                gl_ref[h] = gl
                t_ref[:, sl] = ts[h]

    tb = pl.BlockSpec((CB, D), lambda i: (i, 0))
    one_dir_specs = [tb, tb, tb, tb, tb, pl.BlockSpec((NH, 1, 128), lambda i: (i, 0, 0)), tb]
    one_dir_shapes = ([jax.ShapeDtypeStruct((T, D), F32)] + [jax.ShapeDtypeStruct((T, D), BF)] * 4
                      + [jax.ShapeDtypeStruct((nb * NH, 1, 128), F32), jax.ShapeDtypeStruct((T, D), F32)])
    outs = pl.pallas_call(
        body, grid=(nb,), name="dn1_fwd",
        in_specs=[tb, tb, tb, pl.BlockSpec((CB, 128), lambda i: (i, 0))],
        out_specs=one_dir_specs * 2, out_shape=one_dir_shapes * 2, compiler_params=_cp(),
    )(q, k, v, gb)
    return [tuple(outs[:7]), tuple(outs[7:])]


def _dn1_bwd(q, k, v, gb, tinvs, cots):
    T = q.shape[0]
    nb = T // CB

    def body(q_ref, k_ref, v_ref, gb_ref, *refs):
        dir_refs, (dq_ref, dk_ref, dv_ref, dgb_ref) = refs[:14], refs[14:]
        gbv = gb_ref[...]
        qs = [q_ref[:, sl] for sl in _HEAD_SLICES]
        ks = [k_ref[:, sl] for sl in _HEAD_SLICES]
        vs = [v_ref[:, sl] for sl in _HEAD_SLICES]
        lane = lax.broadcasted_iota(jnp.int32, (CB, 128), 1)
        dgb = jnp.zeros((CB, 128), F32)
        for d in (0, 1):
            t_ref, du_ref, dw_ref, dqg_ref, dkd_ref, dqkd_ref, dgl_ref = dir_refs[7 * d:7 * d + 7]
            gcum = _dot_h3(_cum_matrix(d == 1), gbv)
            betas = [_lane_bcast(gbv, d * NH + h) for h in range(NH)]
            gcs = [_lane_bcast(gcum, 16 + d * NH + h) for h in range(NH)]
            ts = [t_ref[:, sl] for sl in _HEAD_SLICES]
            f = lambda qs, ks, vs, betas, gcs: _dn1_heads(qs, ks, vs, betas, gcs, ts, d == 1)[0]
            _, vjp = jax.vjp(f, qs, ks, vs, betas, gcs)
            cot = [(du_ref[:, sl], dw_ref[:, sl], dqg_ref[:, sl], dkd_ref[:, sl], dqkd_ref[:, sl], dgl_ref[h])
                   for h, sl in enumerate(_HEAD_SLICES)]
            dqs, dks, dvs, dbetas, dgcs = vjp(cot)
            dgcum = jnp.zeros((CB, 128), F32)
            for h, sl in enumerate(_HEAD_SLICES):
                if d == 0:
                    dq_ref[:, sl] = dqs[h]
                    dk_ref[:, sl] = dks[h]
                    dv_ref[:, sl] = dvs[h]
                else:
                    dq_ref[:, sl] += dqs[h]
                    dk_ref[:, sl] += dks[h]
                    dv_ref[:, sl] += dvs[h]
                dgb = dgb + jnp.where(lane == d * NH + h, jnp.sum(dbetas[h], axis=1, keepdims=True), 0.0)
                dgcum = dgcum + jnp.where(lane == 16 + d * NH + h, jnp.sum(dgcs[h], axis=1, keepdims=True), 0.0)
            dgb = dgb + _dot_h3(_cum_matrix(d == 0), dgcum)
        dgb_ref[...] = dgb

    tb = pl.BlockSpec((CB, D), lambda i: (i, 0))
    gbs = pl.BlockSpec((CB, 128), lambda i: (i, 0))
    gls = pl.BlockSpec((NH, 1, 128), lambda i: (i, 0, 0))
    args = []
    for d in (0, 1):
        args += [tinvs[d], *cots[d]]
    return pl.pallas_call(
        body, grid=(nb,), name="dn1_bwd",
        in_specs=[tb, tb, tb, gbs] + [tb, tb, tb, tb, tb, tb, gls] * 2, out_specs=[tb, tb, tb, gbs],
        out_shape=[jax.ShapeDtypeStruct((T, D), F32)] * 3 + [jax.ShapeDtypeStruct((T, 128), F32)],
        compiler_params=_cp(),
    )(q, k, v, gb, *args)


def _dn2_steps(chains):
    ws = [_dot_bf(w, s) for _, w, _, _, _, _, s in chains]
    v_new = [c[0] - x for c, x in zip(chains, ws)]
    o_state = [_dot_bf(c[2], c[6]) for c in chains]
    o_local = [_dot_bf(c[4], vn) for c, vn in zip(chains, v_new)]
    grow = [_dot_tn_bf(c[3], vn) for c, vn in zip(chains, v_new)]
    return [a + b for a, b in zip(o_state, o_local)], [c[6] * c[5] + g for c, g in zip(chains, grow)]


def _scan_order(direction, nlat_b, nall_b):
    if direction == 0:
        return lambda i: (i + nlat_b) % nall_b
    return lambda i: nall_b - 1 - i


def _dn2_fwd(per_dir, nlat):
    T = per_dir[0][0].shape[0]
    nb = T // CB
    blks = [_scan_order(d, nlat // CB, nb) for d in (0, 1)]

    def body(*refs):
        ins, outs, s_scr = refs[:12], refs[12:16], refs[16]

        @pl.when(pl.program_id(0) == 0)
        def _():
            s_scr[...] = jnp.zeros_like(s_scr)
        for d in (0, 1):
            outs[2 * d + 1][0] = s_scr[d]
        where = [(d, h, sl) for h, sl in enumerate(_HEAD_SLICES) for d in (0, 1)]
        chains = []
        for d, h, sl in where:
            u_ref, w_ref, qg_ref, kd_ref, qkd_ref, gl_ref = ins[6 * d:6 * d + 6]
            chains.append((u_ref[:, sl], w_ref[:, sl], qg_ref[:, sl], kd_ref[:, sl], qkd_ref[:, sl], gl_ref[h], s_scr[d, h]))
        os, states = _dn2_steps(chains)
        for (d, h, sl), o, s_next in zip(where, os, states):
            outs[2 * d][:, sl] = o
            s_scr[d, h] = s_next

    in_specs, out_specs, args = [], [], []
    for d in (0, 1):
        blk = blks[d]
        tb = pl.BlockSpec((CB, D), lambda i, blk=blk: (blk(i), 0))
        in_specs += [tb] * 5 + [pl.BlockSpec((NH, 1, 128), lambda i, blk=blk: (blk(i), 0, 0))]
        out_specs += [tb, pl.BlockSpec((1, NH, HD, HD), lambda i, blk=blk: (blk(i), 0, 0, 0))]
        args += list(per_dir[d])
    outs = pl.pallas_call(
        body, grid=(nb,), name="dn2_fwd", in_specs=in_specs, out_specs=out_specs,
        out_shape=[jax.ShapeDtypeStruct((T, D), F32), jax.ShapeDtypeStruct((nb, NH, HD, HD), F32)] * 2,
        scratch_shapes=[pltpu.VMEM((2, NH, HD, HD), F32)], compiler_params=_cp(),
    )(*args)
    return [tuple(outs[:2]), tuple(outs[2:])]


def _dn2_bwd(per_dir, do, nlat):
    T = per_dir[0][0].shape[0]
    nb = T // CB
    nlat_b = nlat // CB
    fwd = [_scan_order(d, nlat_b, nb) for d in (0, 1)]
    blks = [lambda i, f=f: f(nb - 1 - i) for f in fwd]

    def body(*refs):
        ins, outs, ds_scr = refs[:16], refs[16:28], refs[28]
        i = pl.program_id(0)

        @pl.when(i == 0)
        def _():
            ds_scr[...] = jnp.zeros_like(ds_scr)
        where = [(d, h, sl) for h, sl in enumerate(_HEAD_SLICES) for d in (0, 1)]
        chains, cot_o, cot_s = [], [], []
        for d, h, sl in where:
            u_ref, w_ref, qg_ref, kd_ref, qkd_ref, gl_ref, sall_ref, do_ref = ins[8 * d:8 * d + 8]
            chains.append((u_ref[:, sl], w_ref[:, sl].astype(F32), qg_ref[:, sl].astype(F32), kd_ref[:, sl].astype(F32),
                           qkd_ref[:, sl].astype(F32), gl_ref[h], sall_ref[0, h]))
            cot_o.append(jnp.where(blks[d](i) < nlat_b, do_ref[:, sl], 0.0))
            cot_s.append(ds_scr[d, h])
        _, vjp = jax.vjp(_dn2_steps, chains)
        for (d, h, sl), (du, dw, dqg, dkd, dqkd, dgl, ds) in zip(where, vjp((cot_o, cot_s))[0]):
            du_ref, dw_ref, dqg_ref, dkd_ref, dqkd_ref, dgl_ref = outs[6 * d:6 * d + 6]
            du_ref[:, sl] = du
            dw_ref[:, sl] = dw
            dqg_ref[:, sl] = dqg
            dkd_ref[:, sl] = dkd
            dqkd_ref[:, sl] = dqkd
            dgl_ref[h] = dgl
            ds_scr[d, h] = ds

    in_specs, out_specs, args = [], [], []
    for d in (0, 1):
        blk = blks[d]
        tb = pl.BlockSpec((CB, D), lambda i, blk=blk: (blk(i), 0))
        gls = pl.BlockSpec((NH, 1, 128), lambda i, blk=blk: (blk(i), 0, 0))
        in_specs += [tb] * 5 + [gls, pl.BlockSpec((1, NH, HD, HD), lambda i, blk=blk: (blk(i), 0, 0, 0)),
                                pl.BlockSpec((CB, D), lambda i, blk=blk: (jnp.minimum(blk(i), nlat_b - 1), 0))]
        out_specs += [tb] * 5 + [gls]
        args += list(per_dir[d]) + [do]
    outs = pl.pallas_call(
        body, grid=(nb,), name="dn2_bwd", in_specs=in_specs, out_specs=out_specs,
        out_shape=([jax.ShapeDtypeStruct((T, D), F32)] * 5 + [jax.ShapeDtypeStruct((nb * NH, 1, 128), F32)]) * 2,
        scratch_shapes=[pltpu.VMEM((2, NH, HD, HD), F32)], compiler_params=_cp(),
    )(*args)
    return [tuple(outs[:6]), tuple(outs[6:])]


def _ghn_fn(o, gt, w):
    y = o * lax.rsqrt(jnp.mean(o * o, axis=-1, keepdims=True) + EPS)
    return (y * w) * jax.nn.silu(gt)


def _ghn_fwd(o_f, o_b, p, w, w_branch, nlat):
    tb = _tile(nlat, (256, 128))

    def body(of_ref, ob_ref, gt_ref, w_ref, wb_ref, y_ref, z_ref):
        for h in range(NH):
            sl = slice(h * HD, (h + 1) * HD)
            y_ref[:, sl] = _ghn_fn(of_ref[:, sl] + ob_ref[:, sl], gt_ref[:, sl], w_ref[...]).astype(BF)
        z_ref[...] = jnp.dot(y_ref[...], wb_ref[...], preferred_element_type=F32)

    row = pl.BlockSpec((tb, D), lambda i: (i, 0))
    return pl.pallas_call(
        body, grid=(nlat // tb,), name="ghn_fwd",
        in_specs=[row, row, pl.BlockSpec((tb, D), lambda i: (i, O_GT // D)), pl.BlockSpec((1, HD), lambda i: (0, 0)), _resident((D, D))],
        out_specs=[row, row], out_shape=[jax.ShapeDtypeStruct((nlat, D), BF), jax.ShapeDtypeStruct((nlat, D), F32)],
    )(o_f, o_b, p, w, w_branch)


def _ghn_bwd(o_f, o_b, p, w, dy, nlat):
    T = p.shape[0]
    tb = _tile(nlat, (256, 128))
    nlb = nlat // tb

    def body(of_ref, ob_ref, gt_ref, w_ref, dy_ref, do_ref, dgt_ref, dw_ref):
        is_lat = pl.program_id(0) < nlb

        @pl.when(pl.program_id(0) == 0)
        def _():
            dw_ref[...] = jnp.zeros_like(dw_ref)
        for h in range(NH):
            sl = slice(h * HD, (h + 1) * HD)
            _, vjp = jax.vjp(_ghn_fn, of_ref[:, sl] + ob_ref[:, sl], gt_ref[:, sl], w_ref[...])
            do, dgt, dw = vjp(dy_ref[:, sl])
            do_ref[:, sl] = do
            dgt_ref[:, sl] = jnp.where(is_lat, dgt, 0.0).astype(BF)
            dw_ref[...] += jnp.where(is_lat, dw, 0.0)

    lat = lambda i: jnp.minimum(i, nlb - 1)
    row = pl.BlockSpec((tb, D), lambda i: (lat(i), 0))
    one = pl.BlockSpec((1, HD), lambda i: (0, 0))
    return pl.pallas_call(
        body, grid=(T // tb,), name="ghn_bwd",
        in_specs=[row, row, pl.BlockSpec((tb, D), lambda i: (lat(i), O_GT // D)), one, row],
        out_specs=[row, pl.BlockSpec((tb, D), lambda i: (i, 0)), one],
        out_shape=[jax.ShapeDtypeStruct((nlat, D), F32), jax.ShapeDtypeStruct((T, D), BF), jax.ShapeDtypeStruct((1, HD), F32)],
    )(o_f, o_b, p, w, dy)


@jax.custom_vjp
def _swap32(x):
    lane = lax.broadcasted_iota(jnp.int32, x.shape, 1)
    return jnp.where((lane & 32) == 0, pltpu.roll(x, 96, 1), pltpu.roll(x, 32, 1))


_swap32.defvjp(lambda x: (_swap32(x), None), lambda _, g: (_swap32(g),))


def _qk_post_fn(xs, w, cos, sin):
    inv = [lax.rsqrt(jnp.mean(x * x, axis=-1, keepdims=True) + EPS) for x in xs]
    ys = [(x * r) * w for x, r in zip(xs, inv)]
    return [y * cos + _swap32(y) * sin for y in ys]


def _attn_prep_fwd(p, qn, kn, cos, sin):
    T = p.shape[0]
    tb = _tile(T, (256, 128))

    def body(q_ref, k_ref, v_ref, qn_ref, kn_ref, cos_ref, sin_ref, qr_ref, kr_ref, vb_ref):
        cos_v, sin_v = cos_ref[...], sin_ref[...]
        for sl, y in zip(_HEAD_SLICES, _qk_post_fn([q_ref[:, sl] for sl in _HEAD_SLICES], qn_ref[...], cos_v, sin_v)):
            qr_ref[:, sl] = y.astype(BF)
        for sl, y in zip(_HEAD_SLICES, _qk_post_fn([k_ref[:, sl] for sl in _HEAD_SLICES[:KVH]], kn_ref[...], cos_v, sin_v)):
            kr_ref[:, sl] = y.astype(BF)
        vb_ref[...] = v_ref[...].astype(BF)

    one = pl.BlockSpec((1, HD), lambda i: (0, 0))
    tab = pl.BlockSpec((tb, HD), lambda i: (i, 0))
    return pl.pallas_call(
        body, grid=(T // tb,), name="attn_prep_fwd",
        in_specs=[pl.BlockSpec((tb, D), lambda i: (i, O_Q // D)), pl.BlockSpec((tb, KV), lambda i: (i, O_K // KV)),
                  pl.BlockSpec((tb, KV), lambda i: (i, O_V // KV)), one, one, tab, tab],
        out_specs=[pl.BlockSpec((tb, D), lambda i: (i, 0)), pl.BlockSpec((tb, KV), lambda i: (i, 0)),
                   pl.BlockSpec((tb, KV), lambda i: (i, 0))],
        out_shape=[jax.ShapeDtypeStruct((T, D), BF), jax.ShapeDtypeStruct((T, KV), BF), jax.ShapeDtypeStruct((T, KV), BF)],
    )(p, p, p, qn, kn, cos, sin)


def _attn_prep_bwd(p, qn, kn, cos, sin, dqr, dkp, dvp, dkc, dvc, nlat):
    T = p.shape[0]
    nqb = nlat // CB
    ncb = (T - nlat) // CB

    def body(q_ref, k_ref, v_ref, qn_ref, kn_ref, cos_ref, sin_ref, dqr_ref, dka_ref, dkb_ref, dkc3_ref, dva_ref, dvb_ref, dvc3_ref,
             dkctx_ref, dvctx_ref, dq_ref, dk_ref, dv_ref, dqn_ref, dkn_ref):
        i = pl.program_id(0)
        is_lat = i < nqb
        cos_v, sin_v = cos_ref[...], sin_ref[...]

        @pl.when(i == 0)
        def _():
            dqn_ref[...] = jnp.zeros_like(dqn_ref)
            dkn_ref[...] = jnp.zeros_like(dkn_ref)

        def band_sum(a_ref, b_ref, c_ref, ctx_ref):
            s = b_ref[0] + jnp.where(i > 0, a_ref[0], 0.0) + jnp.where(i < nqb - 1, c_ref[0], 0.0)
            return jnp.where(is_lat, s, ctx_ref[...])

        dkr = band_sum(dka_ref, dkb_ref, dkc3_ref, dkctx_ref)
        dv_ref[...] = band_sum(dva_ref, dvb_ref, dvc3_ref, dvctx_ref).astype(BF)
        post = lambda xs, w: _qk_post_fn(xs, w, cos_v, sin_v)
        _, vjp = jax.vjp(post, [q_ref[:, sl] for sl in _HEAD_SLICES], qn_ref[...])
        dqs, dqn = vjp([jnp.where(is_lat, dqr_ref[:, sl], 0.0) for sl in _HEAD_SLICES])
        for sl, dq in zip(_HEAD_SLICES, dqs):
            dq_ref[:, sl] = dq.astype(BF)
        dqn_ref[...] += dqn
        _, vjp = jax.vjp(post, [k_ref[:, sl] for sl in _HEAD_SLICES[:KVH]], kn_ref[...])
        dks, dkn = vjp([dkr[:, sl] for sl in _HEAD_SLICES[:KVH]])
        for sl, dk in zip(_HEAD_SLICES, dks):
            dk_ref[:, sl] = dk.astype(BF)
        dkn_ref[...] += dkn

    one = pl.BlockSpec((1, HD), lambda i: (0, 0))
    tab = pl.BlockSpec((CB, HD), lambda i: (i, 0))
    lat = lambda i: jnp.minimum(i, nqb - 1)

    def part(off, slot):
        return pl.BlockSpec((1, CB, KV), lambda i: (jnp.clip(lat(i) + off, 0, nqb - 1) * 3 + slot, 0, 0))

    ctxs = pl.BlockSpec((CB, KV), lambda i: (jnp.clip(i - nqb, 0, ncb - 1), 0))
    kvs = pl.BlockSpec((CB, KV), lambda i: (i, 0))
    return pl.pallas_call(
        body, grid=(T // CB,), name="attn_prep_bwd",
        in_specs=[pl.BlockSpec((CB, D), lambda i: (i, O_Q // D)), pl.BlockSpec((CB, KV), lambda i: (i, O_K // KV)),
                  pl.BlockSpec((CB, KV), lambda i: (i, O_V // KV)), one, one, tab, tab,
                  pl.BlockSpec((CB, D), lambda i: (lat(i), 0)),
                  part(-1, 2), part(0, 1), part(1, 0), part(-1, 2), part(0, 1), part(1, 0), ctxs, ctxs],
        out_specs=[pl.BlockSpec((CB, D), lambda i: (i, 0)), kvs, kvs, one, one],
        out_shape=[jax.ShapeDtypeStruct((T, D), BF), jax.ShapeDtypeStruct((T, KV), BF), jax.ShapeDtypeStruct((T, KV), BF),
                   jax.ShapeDtypeStruct((1, HD), F32), jax.ShapeDtypeStruct((1, HD), F32)],
    )(p, p, p, qn, kn, cos, sin, dqr, dkp, dkp, dkp, dvp, dvp, dvp, dkc, dvc)


def _attn_groups_fn(qs, kalls, valls, sinks, bias):
    groups = range(KVH)
    q = [jnp.concatenate(qs[GRP * g:GRP * (g + 1)], axis=0) for g in groups]
    s = [_dot_nt_bf(q[g], kalls[g]) * (HD ** -0.5) + bias for g in groups]
    sk = [jnp.concatenate([jnp.broadcast_to(jnp.mean(t, axis=1, keepdims=True), (CB, 1)) for t in sinks[GRP * g:GRP * (g + 1)]],
                          axis=0) for g in groups]
    m = [lax.stop_gradient(jnp.maximum(jnp.max(s[g], axis=1, keepdims=True), sk[g])) for g in groups]
    e = [jnp.exp(s[g] - m[g]) for g in groups]
    den = [jnp.sum(e[g], axis=1, keepdims=True) + jnp.exp(sk[g] - m[g]) for g in groups]
    return [_dot_bf(e[g] / den[g], valls[g]) for g in groups]


def _attn_bias(lc):
    r, c = _iota2((GRP * CB, 3 * CB + lc))
    rel = c - (r & (CB - 1))
    win = (rel >= 0) & (rel <= 2 * CB)
    ctx = c >= 3 * CB
    seen = [(win & (c >= CB)) | ctx, win | ctx, (win & (c < 2 * CB)) | ctx]
    return jnp.stack([jnp.where(s, 0.0, -1e30) for s in seen]).astype(F32)


def _attn_specs(nqb, lc, nlat):
    assert nqb >= 2
    qs = pl.BlockSpec((CB, D), lambda i: (i, 0))
    ka = pl.BlockSpec((CB, KV), lambda i: (jnp.maximum(i - 1, 0), 0))
    kb = pl.BlockSpec((CB, KV), lambda i: (i, 0))
    kc = pl.BlockSpec((CB, KV), lambda i: (jnp.minimum(i + 1, nqb - 1), 0))
    kx = pl.BlockSpec((lc, KV), lambda i: (nlat // lc, 0))
    sk = pl.BlockSpec((KVH, 8, 128), lambda i: (0, 0, 0))
    bs = pl.BlockSpec((1, GRP * CB, 3 * CB + lc), lambda i: (jnp.where(i == 0, 0, jnp.where(i == nqb - 1, 2, 1)), 0, 0))
    return qs, ka, kb, kc, kx, sk, bs


def _attn_operands(q_ref, k_refs, v_refs, sk_ref, dtype):
    sls = [slice(g * HD, (g + 1) * HD) for g in range(KVH)]
    kalls = [jnp.concatenate([r[:, sl] for r in k_refs], axis=0).astype(dtype) for sl in sls]
    valls = [jnp.concatenate([r[:, sl] for r in v_refs], axis=0).astype(dtype) for sl in sls]
    qs = [q_ref[:, sl].astype(dtype) for sl in _HEAD_SLICES]
    sinks = [sk_ref[h // GRP, (h % GRP):(h % GRP) + 1, :] for h in range(NH)]
    return qs, kalls, valls, sinks


def _attn_fwd(qr, kr, vb, sink, w_branch, nlat):
    lc = kr.shape[0] - nlat
    nqb = nlat // CB
    qs, ka, kb, kc, kx, sk, bs = _attn_specs(nqb, lc, nlat)

    def body(q_ref, ka_ref, kb_ref, kc_ref, kx_ref, va_ref, vb_ref, vc_ref, vx_ref, sk_ref, bias_ref, wb_ref, o_ref, z_ref):
        operands = _attn_operands(q_ref, (ka_ref, kb_ref, kc_ref, kx_ref), (va_ref, vb_ref, vc_ref, vx_ref), sk_ref, BF)
        outs = _attn_groups_fn(*operands, bias_ref[0])
        for h, sl in enumerate(_HEAD_SLICES):
            o_ref[:, sl] = outs[h // GRP][(h % GRP) * CB:(h % GRP + 1) * CB].astype(BF)
        z_ref[...] = jnp.dot(o_ref[...], wb_ref[...], preferred_element_type=F32)

    return pl.pallas_call(
        body, grid=(nqb,), name="attn_fwd",
        in_specs=[qs, ka, kb, kc, kx, ka, kb, kc, kx, sk, bs, _resident((D, D))], out_specs=[qs, qs],
        out_shape=[jax.ShapeDtypeStruct((nlat, D), BF), jax.ShapeDtypeStruct((nlat, D), F32)], compiler_params=_cp(),
    )(qr, kr, kr, kr, kr, vb, vb, vb, vb, sink, _attn_bias(lc), w_branch)


def _attn_bwd(qr, kr, vb, sink, dy, nlat):
    lc = kr.shape[0] - nlat
    nqb = nlat // CB
    qs, ka, kb, kc, kx, sk, bs = _attn_specs(nqb, lc, nlat)

    def body(q_ref, ka_ref, kb_ref, kc_ref, kx_ref, va_ref, vb_ref, vc_ref, vx_ref, sk_ref, dy_ref, bias_ref,
             dq_ref, dkp_ref, dvp_ref, dkx_ref, dvx_ref, dsk_ref):
        operands = _attn_operands(q_ref, (ka_ref, kb_ref, kc_ref, kx_ref), (va_ref, vb_ref, vc_ref, vx_ref), sk_ref, F32)
        _, vjp = jax.vjp(functools.partial(_attn_groups_fn, bias=bias_ref[0]), *operands)
        dys_g = [jnp.concatenate([dy_ref[:, sl] for sl in _HEAD_SLICES[GRP * g:GRP * (g + 1)]], axis=0) for g in range(KVH)]
        dqs, dks, dvs, dsinks = vjp(dys_g)

        @pl.when(pl.program_id(0) == 0)
        def _():
            dkx_ref[...] = jnp.zeros_like(dkx_ref)
            dvx_ref[...] = jnp.zeros_like(dvx_ref)
            dsk_ref[...] = jnp.zeros_like(dsk_ref)

        for h, sl in enumerate(_HEAD_SLICES):
            dq_ref[:, sl] = dqs[h]
            dsk_ref[h // GRP, (h % GRP):(h % GRP) + 1, :] += dsinks[h]
        for g in range(KVH):
            sl = slice(g * HD, (g + 1) * HD)
            for t in range(3):
                dkp_ref[t, :, sl] = dks[g][t * CB:(t + 1) * CB]
                dvp_ref[t, :, sl] = dvs[g][t * CB:(t + 1) * CB]
            dkx_ref[:, sl] += dks[g][3 * CB:]
            dvx_ref[:, sl] += dvs[g][3 * CB:]

    dys = qs
    parts = pl.BlockSpec((3, CB, KV), lambda i: (i, 0, 0))
    ctxo = pl.BlockSpec((lc, KV), lambda i: (0, 0))
    return pl.pallas_call(
        body, grid=(nqb,), name="attn_bwd",
        in_specs=[qs, ka, kb, kc, kx, ka, kb, kc, kx, sk, dys, bs],
        out_specs=[dys, parts, parts, ctxo, ctxo, sk],
        out_shape=[jax.ShapeDtypeStruct((nlat, D), F32), jax.ShapeDtypeStruct((3 * nqb, CB, KV), F32),
                   jax.ShapeDtypeStruct((3 * nqb, CB, KV), F32), jax.ShapeDtypeStruct((lc, KV), F32),
                   jax.ShapeDtypeStruct((lc, KV), F32), jax.ShapeDtypeStruct((KVH, 8, 128), F32)],
        compiler_params=_cp(),
    )(qr, kr, kr, kr, kr, vb, vb, vb, vb, sink, dy, _attn_bias(lc))


def _merge_fn(z_dn, z_at, g_dn, g_at):
    return jax.nn.sigmoid(g_dn) * z_dn + jax.nn.sigmoid(g_at) * z_at


def _merge_fwd(z_dn, z_at, p, w_out, nlat):
    tb = _tile(nlat, (256, 128))

    def body(zd_ref, za_ref, gd_ref, ga_ref, wo_ref, o_ref, mix_ref):
        o_ref[...] = _merge_fn(zd_ref[...], za_ref[...], gd_ref[...], ga_ref[...]).astype(BF)
        mix_ref[...] = jnp.dot(o_ref[...], wo_ref[...], preferred_element_type=F32)

    row = pl.BlockSpec((tb, D), lambda i: (i, 0))
    return pl.pallas_call(
        body, grid=(nlat // tb,), name="merge_fwd",
        in_specs=[row, row, pl.BlockSpec((tb, D), lambda i: (i, O_MG // D)), pl.BlockSpec((tb, D), lambda i: (i, O_MG // D + 1)),
                  _resident((D, D))],
        out_specs=[row, row], out_shape=[jax.ShapeDtypeStruct((nlat, D), BF), jax.ShapeDtypeStruct((nlat, D), F32)],
    )(z_dn, z_at, p, p, w_out)


def _merge_bwd(z_dn, z_at, p, dm, w_bdn, w_bat, nlat):
    T = p.shape[0]
    tb = _tile(nlat, (256, 128))
    nlb = nlat // tb

    def body(zd_ref, za_ref, gd_ref, ga_ref, dm_ref, wd_ref, wa_ref, dzd_ref, dza_ref, dg_ref, dyd_ref, dya_ref):
        is_lat = pl.program_id(0) < nlb
        _, vjp = jax.vjp(_merge_fn, zd_ref[...], za_ref[...], gd_ref[...], ga_ref[...])
        dzd, dza, dgd, dga = vjp(dm_ref[...])
        dzd_ref[...] = dzd.astype(BF)
        dza_ref[...] = dza.astype(BF)
        dg_ref[:, :D] = jnp.where(is_lat, dgd, 0.0).astype(BF)
        dg_ref[:, D:] = jnp.where(is_lat, dga, 0.0).astype(BF)
        dyd_ref[...] = lax.dot_general(dzd_ref[...], wd_ref[...], (_DIMS["nt"], ((), ())), preferred_element_type=F32)
        dya_ref[...] = lax.dot_general(dza_ref[...], wa_ref[...], (_DIMS["nt"], ((), ())), preferred_element_type=F32)

    lat = lambda i: jnp.minimum(i, nlb - 1)
    row = pl.BlockSpec((tb, D), lambda i: (lat(i), 0))
    return pl.pallas_call(
        body, grid=(T // tb,), name="merge_bwd",
        in_specs=[row, row, pl.BlockSpec((tb, D), lambda i: (lat(i), O_MG // D)),
                  pl.BlockSpec((tb, D), lambda i: (lat(i), O_MG // D + 1)), row, _resident((D, D)), _resident((D, D))],
        out_specs=[row, row, pl.BlockSpec((tb, 2 * D), lambda i: (i, 0)), row, row],
        out_shape=[jax.ShapeDtypeStruct((nlat, D), BF), jax.ShapeDtypeStruct((nlat, D), BF), jax.ShapeDtypeStruct((T, 2 * D), BF),
                   jax.ShapeDtypeStruct((nlat, D), F32), jax.ShapeDtypeStruct((nlat, D), F32)],
    )(z_dn, z_at, p, p, dm, w_bdn, w_bat)


def _swiglu_fn(ug, uv):
    return jax.nn.silu(ug) * uv


FFN_GROUP = 256


def _resident(shape):
    return pl.BlockSpec(shape, lambda i: (0,) * len(shape), pipeline_mode=pl.Buffered(1))


H_HALO = 16


def _up_project(h_refs, wu_ref, u_scr):
    cur_ref, prev_ref, next_ref = h_refs
    rows = jnp.concatenate([prev_ref[...], cur_ref[...], next_ref[...]], axis=0)
    u_scr[...] = jnp.dot(rows, wu_ref[...], preferred_element_type=F32)


def _up_ext_rows(u_scr, cols, keep, tb):
    xe = u_scr[H_HALO - HALO:H_HALO + tb + HALO, cols]
    r = lax.broadcasted_iota(jnp.int32, (tb + 2 * HALO, 1), 0)
    inside = ((r >= HALO) | keep[0]) & ((r < HALO + tb) | keep[1])
    return jnp.where(inside, xe, 0.0)


def _ffn_fwd(h, w_up, w8, bias, w_down):
    n = h.shape[0]
    tb = _tile(n, (256, 128))
    starts, ends = _segment_edges((n,), tb)

    def body(cur_ref, prev_ref, next_ref, wu_ref, w_ref, b_ref, wd_ref, u_ref, o_ref, ff_ref, u_scr):
        keep = _keep_halos(pl.program_id(0), starts, ends)
        _up_project((cur_ref, prev_ref, next_ref), wu_ref, u_scr)
        u_ref[...] = u_scr[H_HALO:H_HALO + tb, :]
        for c0 in range(0, DFF, FFN_GROUP):
            halves = []
            for cols in (slice(c0, c0 + FFN_GROUP), slice(DFF + c0, DFF + c0 + FFN_GROUP)):
                xe = _up_ext_rows(u_scr, cols, keep, tb)
                halves.append(_conv_rows(_shifted_rows(xe, FFN_TAPS), w_ref, cols)[HALO:HALO + tb] + b_ref[:, cols])
            o_ref[:, c0:c0 + FFN_GROUP] = _swiglu_fn(*halves).astype(BF)
        ff_ref[...] = jnp.dot(o_ref[...], wd_ref[...], preferred_element_type=F32)

    return pl.pallas_call(
        body, grid=(n // tb,), name="ffn_fwd",
        in_specs=_halo_specs(tb, D, n, halo=H_HALO) + [_resident((D, 2 * DFF)), pl.BlockSpec((8, 2 * DFF), lambda i: (0, 0)),
                                                        pl.BlockSpec((1, 2 * DFF), lambda i: (0, 0)), _resident((DFF, D))],
        out_specs=[pl.BlockSpec((tb, 2 * DFF), lambda i: (i, 0)), pl.BlockSpec((tb, DFF), lambda i: (i, 0)),
                   pl.BlockSpec((tb, D), lambda i: (i, 0))],
        out_shape=[jax.ShapeDtypeStruct((n, 2 * DFF), F32), jax.ShapeDtypeStruct((n, DFF), BF), jax.ShapeDtypeStruct((n, D), F32)],
        scratch_shapes=[pltpu.VMEM((tb + 2 * H_HALO, 2 * DFF), F32)],
        compiler_params=_cp(),
    )(h, h, h, w_up, w8, bias, w_down)


def _ffn_bwd(u, w_up, w8, bias, da):
    n = u.shape[0]
    tb = _tile(n, (256, 128))
    starts, ends = _segment_edges((n,), tb)

    def body(cur_ref, prev_ref, next_ref, wu_ref, w_ref, b_ref, da_c, da_p, da_n, du_ref, dw_ref, db_ref, dh_ref):
        i = pl.program_id(0)
        keep = _keep_halos(i, starts, ends)

        @pl.when(i == 0)
        def _():
            dw_ref[...] = jnp.zeros_like(dw_ref)
            db_ref[...] = jnp.zeros_like(db_ref)

        for c0 in range(0, DFF, FFN_GROUP):
            col_pair = (slice(c0, c0 + FFN_GROUP), slice(DFF + c0, DFF + c0 + FFN_GROUP))
            shifts = [_shifted_rows(_ext_rows((cur_ref, prev_ref, next_ref), cols, keep), FFN_TAPS) for cols in col_pair]
            convs = [_conv_rows(shifted, w_ref, cols) + b_ref[:, cols] for shifted, cols in zip(shifts, col_pair)]
            dae = _ext_rows((da_c, da_p, da_n), col_pair[0], keep)
            _, vjp = jax.vjp(_swiglu_fn, *convs)
            for shifted, cols, dce in zip(shifts, col_pair, vjp(dae)):
                du_ref[:, cols] = _conv_rows(_shifted_rows(dce, FFN_TAPS, transpose=True), w_ref, cols)[HALO:HALO + tb].astype(BF)
                dcur = dce[HALO:HALO + tb]
                for j, g in enumerate(_tap_grads(dcur, shifted, tb)):
                    dw_ref[j:j + 1, cols] += g
                db_ref[:, cols] += jnp.sum(dcur, axis=0, keepdims=True)
        dh_ref[...] = lax.dot_general(du_ref[...], wu_ref[...], (_DIMS["nt"], ((), ())), preferred_element_type=F32)

    wspec = pl.BlockSpec((8, 2 * DFF), lambda i: (0, 0))
    bspec = pl.BlockSpec((1, 2 * DFF), lambda i: (0, 0))
    return pl.pallas_call(
        body, grid=(n // tb,), name="ffn_bwd",
        in_specs=_halo_specs(tb, 2 * DFF, n) + [_resident((D, 2 * DFF)), wspec, bspec] + _halo_specs(tb, DFF, n),
        out_specs=[pl.BlockSpec((tb, 2 * DFF), lambda i: (i, 0)), wspec, bspec, pl.BlockSpec((tb, D), lambda i: (i, 0))],
        out_shape=[jax.ShapeDtypeStruct((n, 2 * DFF), BF), jax.ShapeDtypeStruct((8, 2 * DFF), F32), jax.ShapeDtypeStruct((1, 2 * DFF), F32),
                   jax.ShapeDtypeStruct((n, D), F32)],
        compiler_params=_cp(),
    )(u, u, u, w_up, w8, bias, da, da, da)


def _loss_kernel(x1, gate, ff, target, w_down):
    n = x1.shape[0]
    tb = _tile(n, (256, 128))

    def body(x_ref, g_ref, f_ref, t_ref, wd_ref, loss_ref, dy_ref, dff_ref, dg_ref, da_ref):
        err = x_ref[...] + g_ref[...] * f_ref[...] - t_ref[...]
        dy = err * (1.0 / D)
        dy_ref[...] = dy
        dff_ref[...] = (g_ref[...] * dy).astype(BF)
        da_ref[...] = lax.dot_general(dff_ref[...], wd_ref[...], (_DIMS["nt"], ((), ())), preferred_element_type=F32)

        @pl.when(pl.program_id(0) == 0)
        def _():
            loss_ref[...] = jnp.zeros_like(loss_ref)
            dg_ref[...] = jnp.zeros_like(dg_ref)
        part = 0.5 * jnp.sum(jnp.sum(err * err, axis=1, keepdims=True) * (1.0 / D), axis=0, keepdims=True)
        loss_ref[...] += jnp.broadcast_to(part, (1, 128))
        dg_ref[...] += jnp.sum(dy * f_ref[...], axis=0, keepdims=True)

    row = pl.BlockSpec((tb, D), lambda i: (i, 0))
    one = pl.BlockSpec((1, D), lambda i: (0, 0))
    return pl.pallas_call(
        body, grid=(n // tb,), name="loss",
        in_specs=[row, one, row, row, _resident((DFF, D))],
        out_specs=[pl.BlockSpec((1, 128), lambda i: (0, 0)), row, row, one, pl.BlockSpec((tb, DFF), lambda i: (i, 0))],
        out_shape=[jax.ShapeDtypeStruct((1, 128), F32), jax.ShapeDtypeStruct((n, D), F32),
                   jax.ShapeDtypeStruct((n, D), BF), jax.ShapeDtypeStruct((1, D), F32), jax.ShapeDtypeStruct((n, DFF), F32)],
        compiler_params=_cp(),
    )(x1, gate, ff, target, w_down)


def _rope_tables(nlat, lc):
    t = jnp.arange(nlat)
    row = (t // GRID_W).astype(F32)
    col = (t % GRID_W).astype(F32)
    inv_freq = ROPE_BASE ** (-jnp.arange(32, dtype=F32) / 32)
    ar, ac = row[:, None] * inv_freq, col[:, None] * inv_freq
    cos = jnp.concatenate([jnp.cos(ar), jnp.cos(ar), jnp.cos(ac), jnp.cos(ac)], axis=1)
    sin = jnp.concatenate([-jnp.sin(ar), jnp.sin(ar), -jnp.sin(ac), jnp.sin(ac)], axis=1)
    cos = jnp.concatenate([cos, jnp.ones((lc, HD), F32)], axis=0)
    sin = jnp.concatenate([sin, jnp.zeros((lc, HD), F32)], axis=0)
    return cos, sin


def _pad_rows8(w):
    return jnp.concatenate([w, jnp.zeros((8 - w.shape[0], w.shape[1]), w.dtype)], axis=0)


def _pack_w_in(w):
    cuts = [sum(IN_SIZES[:i]) for i in range(len(IN_SIZES) + 1)]
    qkv, gt, b, a, q, k, v, mg = [w[:, cuts[i]:cuts[i + 1]] for i in range(len(IN_SIZES))]
    return jnp.concatenate([qkv, gt, q, mg, k, v, b, a, jnp.zeros((w.shape[0], PW - O_BA - 32), w.dtype)], axis=1)


def _unpack_w_in(g):
    return jnp.concatenate([g[:, O_QKV:O_GT], g[:, O_GT:O_Q], g[:, O_BA:O_BA + 32], g[:, O_Q:O_MG], g[:, O_K:O_V],
                            g[:, O_V:O_BA], g[:, O_MG:O_K]], axis=1)


def _local_step(x, ctx, mod_x, mod_c, target, project_in, project_back,
                norm_mix, norm_ffn, dn_conv, a_log, dt_bias, dn_norm, q_norm, k_norm, sink, ffn_conv, ffn_conv_b):
    L, LC = x.shape[0], ctx.shape[0]
    T = L + LC
    seg = lambda r: jnp.stack([mod_x[r], mod_c[r]])[:, None, :]
    sh_a, sc_a = seg(0), seg(1)
    g_a, g_f = mod_x[2][None], mod_x[5][None]
    sh_f, sc_f = mod_x[3][None], mod_x[4][None]
    cos, sin = _rope_tables(L, LC)
    dnc8 = _pad_rows8(dn_conv)
    ffc8 = _pad_rows8(ffn_conv)
    gate_row = lambda a: jnp.concatenate([jnp.zeros((1, 16), F32), a.reshape(1, 16), jnp.zeros((1, 96), F32)], axis=1)
    alog_row, dt_row = gate_row(a_log), gate_row(dt_bias)
    sinkb = jnp.concatenate([jnp.broadcast_to(sink.reshape(KVH, GRP, 1), (KVH, GRP, 128)), jnp.zeros((KVH, 8 - GRP, 128), F32)], axis=1)

    h1 = _norm_mod_fwd(x, ctx, norm_mix, sh_a, sc_a, "norm_mix_fwd")
    p, (w_in_p, w_bdn, w_bat, w_out, w_up, w_down) = project_in(h1)
    q, k, v, gb = _dn_pre_fwd(p, dnc8, alog_row, dt_row, (L, LC))
    wy = _dn1_fwd(q, k, v, gb)
    scans = _dn2_fwd([t[:6] for t in wy], L)
    o_dir = [s[0] for s in scans]
    y_dn, z_dn = _ghn_fwd(o_dir[0], o_dir[1], p, dn_norm, w_bdn, L)
    qr, kr, vb = _attn_prep_fwd(p, q_norm, k_norm, cos, sin)
    y_at, z_at = _attn_fwd(qr, kr, vb, sinkb, w_bat, L)
    merged, mix = _merge_fwd(z_dn, z_at, p, w_out, L)
    x1, h2 = _resid_norm_fwd(x, g_a, mix, norm_ffn, sh_f, sc_f)
    u_raw, act, ff = _ffn_fwd(h2, w_up, ffc8, ffn_conv_b, w_down)
    loss_row, dy, dff, dg_f, dact = _loss_kernel(x1, g_f, ff, target, w_down)

    g_down = _mm(act, dff, form="tn", out_dtype=BF, name="g_ffn_down")
    du_raw, g_ffc8, g_ffb, dh2 = _ffn_bwd(u_raw, w_up, ffc8, ffn_conv_b, dact)
    g_up = _mm(h2, du_raw, form="tn", out_dtype=BF, name="g_ffn_up")
    dx1, dmix, dg_a, g_nffn, dsh_f, dsc_f, dmerged = _resid_norm_bwd(x1, g_a, mix, norm_ffn, sh_f, sc_f, dh2, dy, w_out)

    g_out = _mm(merged, dmix, form="tn", out_dtype=BF, name="g_w_out")
    dz_dn, dz_at, dmg, dy_dn, dy_at = _merge_bwd(z_dn, z_at, p, dmerged, w_bdn, w_bat, L)
    g_bdn = _mm(y_dn, dz_dn, form="tn", out_dtype=BF, name="g_branch_dn")
    g_bat = _mm(y_at, dz_at, form="tn", out_dtype=BF, name="g_branch_at")
    dqr, dkp, dvp, dkx, dvx, dsink = _attn_bwd(qr, kr, vb, sinkb, dy_at, L)
    dq_raw, dk_raw, dv_raw, g_qn, g_kn = _attn_prep_bwd(p, q_norm, k_norm, cos, sin, dqr, dkp, dvp, dkx, dvx, L)
    do, dgt, g_dnn = _ghn_bwd(o_dir[0], o_dir[1], p, dn_norm, dy_dn, L)
    cots = _dn2_bwd([wy[d][:6] + (scans[d][1],) for d in (0, 1)], do, L)
    dq, dk, dv, dgb = _dn1_bwd(q, k, v, gb, [t[6] for t in wy], cots)
    dqkv_raw, dba, g_dnc8, g_alog, g_dt = _dn_pre_bwd(p, dnc8, alog_row, dt_row, dq, dk, dv, dgb, (L, LC))
    dp = jnp.concatenate([dqkv_raw, dgt, dq_raw, dmg, dk_raw, dv_raw, dba, jnp.zeros((T, PW - O_BA - 128), BF)], axis=1)
    big, dh1 = project_back(h1, dp, w_in_p, (g_bdn, g_bat, g_out, g_up, g_down))
    grad_x, g_nmix_x, dsh_a, dsc_a = _norm_mod_bwd(x, norm_mix, mod_x[0][None], mod_x[1][None], dh1, row0=0,
                                                   name="norm_mix_bwd", residual=dx1)
    g_nmix_c, dsh_c, dsc_c = _norm_mod_bwd(ctx, norm_mix, mod_c[0][None], mod_c[1][None], dh1, row0=L, name="norm_mix_bwd_ctx")
    g_nmix = g_nmix_x + g_nmix_c

    zero = jnp.zeros((D,), F32)
    dmod_x = jnp.stack([dsh_a[0], dsc_a[0], dg_a[0], dsh_f[0], dsc_f[0], dg_f[0]])
    dmod_c = jnp.stack([dsh_c[0], dsc_c[0], zero, zero, zero, zero])
    small = dict(
        dmod_x=dmod_x, dmod_c=dmod_c, norm_mix=g_nmix, norm_ffn=g_nffn, dn_conv=g_dnc8[:5], dn_a_log=g_alog[0, 16:32].reshape(2, 8),
        dn_dt_bias=g_dt[0, 16:32].reshape(2, 8), dn_norm=g_dnn, q_norm=g_qn, k_norm=g_kn,
        attn_sink=jnp.sum(dsink[:, :GRP, :], axis=2).reshape(1, NH), ffn_conv=g_ffc8[:3], ffn_conv_b=g_ffb)
    return loss_row[0, 0], grad_x, big, small


def _exchange(arrays, scatter, name):
    n = len(arrays)

    def body(*refs):
        args = (refs[:n], refs[n:2 * n], *refs[2 * n:], scatter)
        _exchange_start(*args)
        _exchange_wait(*args)

    hbm = pl.BlockSpec(memory_space=pl.ANY)
    out_shape, sems = _exchange_shapes(arrays, scatter)
    return pl.pallas_call(body, name=name, in_specs=[hbm] * n, out_specs=[hbm] * n, out_shape=out_shape,
                          scratch_shapes=sems)(*arrays)


def _ada_fwd(c16, w_ada, b_ada):
    def body(c_ref, w_ref, b_ref, o_ref):
        o_ref[...] = _dot_hi(jax.nn.silu(c_ref[...]), w_ref[...]) + b_ref[...]

    return pl.pallas_call(body, name="ada_fwd", out_shape=jax.ShapeDtypeStruct((16, w_ada.shape[1]), F32))(c16, w_ada, b_ada)


def _ada_bwd(c16, w_ada, dmx, dmc):
    def body(c_ref, w_ref, dmx_ref, dmc_ref, gw_ref, pc_ref):
        dmc_tot = dmc_ref[0:1, :]
        for d in range(1, N_DEV):
            dmc_tot = dmc_tot + dmc_ref[d:d + 1, :]
        dm16 = jnp.concatenate([dmx_ref[...], jnp.broadcast_to(dmc_tot, (8, dmc_tot.shape[1]))], axis=0)
        row = lax.broadcasted_iota(jnp.int32, dm16.shape, 0)
        dm16 = jnp.where(row <= 8, dm16, 0.0)
        s = jax.nn.silu(c_ref[...])
        gw_ref[...] = lax.dot_general(s, dm16, (_DIMS["tn"], ((), ())), precision=HI, preferred_element_type=F32)
        pc = lax.dot_general(dm16, w_ref[...], (_DIMS["nt"], ((), ())), precision=HI, preferred_element_type=F32)
        pc_ref[...] = pc[8:9, :]

    return pl.pallas_call(body, name="ada_bwd", out_shape=[jax.ShapeDtypeStruct(w_ada.shape, F32), jax.ShapeDtypeStruct((1, D), F32)],
                          compiler_params=_cp())(c16, w_ada, dmx, dmc)


def _cctx_grad(pc_all, c_ctx_row):
    def body(pc_ref, c_ref, g_ref):
        tot = pc_ref[0]
        for d in range(1, N_DEV):
            tot = tot + pc_ref[d]
        _, vjp = jax.vjp(jax.nn.silu, c_ref[...])
        g_ref[...] = vjp(tot)[0]

    return pl.pallas_call(body, name="cctx_grad", out_shape=jax.ShapeDtypeStruct((1, D), F32))(pc_all, c_ctx_row)


def _adamw(parts, w, m, v, name):
    ns, R, C = parts.shape
    tb = _tile(R, (128, 64, 32, 16, 8))

    def body(p_ref, w_ref, m_ref, v_ref, g_ref, d_ref, mo_ref, vo_ref):
        g = p_ref[0].astype(F32)
        for s in range(1, ns):
            g = g + p_ref[s].astype(F32)
        m2 = ADAM_B1 * m_ref[...] + (1.0 - ADAM_B1) * g
        v2 = ADAM_B2 * v_ref[...] + (1.0 - ADAM_B2) * jnp.square(g)
        m_hat = m2 / (1.0 - ADAM_B1 ** ADAM_STEP)
        v_hat = v2 / (1.0 - ADAM_B2 ** ADAM_STEP)
        g_ref[...] = g
        d_ref[...] = -ADAM_LR * (m_hat / (jnp.sqrt(v_hat) + ADAM_EPS) + ADAM_WD * w_ref[...])
        mo_ref[...] = m2
        vo_ref[...] = v2

    row = pl.BlockSpec((tb, C), lambda i: (i, 0))
    return pl.pallas_call(
        body, grid=(R // tb,), name=name,
        in_specs=[pl.BlockSpec((ns, tb, C), lambda i: (0, i, 0)), row, row, row], out_specs=[row] * 4,
        out_shape=[jax.ShapeDtypeStruct((R, C), F32)] * 4, compiler_params=_cp(),
    )(parts, w, m, v)


_SMALL = (("dmod_x", 6 * D), ("dmod_c", 6 * D), ("b_ada", 6 * D), ("norm_mix", D), ("norm_ffn", D), ("dn_a_log", 16),
          ("dn_dt_bias", 16), ("dn_norm", HD), ("q_norm", HD), ("k_norm", HD), ("attn_sink", NH), ("ffn_conv_b", 2 * DFF),
          ("dn_conv", 5 * 3 * D), ("ffn_conv", 3 * 2 * DFF))
_SMALL_ROWS = -(-sum(n for _, n in _SMALL) // 1024) * 8


def _pack_small(d):
    flat = jnp.concatenate([d[k].reshape(-1).astype(F32) if k in d else jnp.zeros((n,), F32) for k, n in _SMALL])
    return jnp.concatenate([flat, jnp.zeros((_SMALL_ROWS * 128 - flat.shape[0],), F32)]).reshape(_SMALL_ROWS, 128)


def _unpack_small(a):
    flat = a.reshape(a.shape[:-2] + (-1,))
    out, off = {}, 0
    for k, n in _SMALL:
        out[k] = flat[..., off:off + n]
        off += n
    return out


def kernel(x, c, ctx, c_ctx, w_ada, b_ada, norm_mix, norm_ffn, w_in, dn_conv, dn_a_log, dn_dt_bias, dn_norm, q_norm, k_norm, attn_sink, w_branch_dn, w_branch_attn, w_out, ffn_up, ffn_conv, ffn_conv_b, ffn_down, loss_target, m_c_ctx, m_w_ada, m_b_ada, m_norm_mix, m_norm_ffn, m_w_in, m_dn_conv, m_dn_a_log, m_dn_dt_bias, m_dn_norm, m_q_norm, m_k_norm, m_attn_sink, m_w_branch_dn, m_w_branch_attn, m_w_out, m_ffn_up, m_ffn_conv, m_ffn_conv_b, m_ffn_down, v_c_ctx, v_w_ada, v_b_ada, v_norm_mix, v_norm_ffn, v_w_in, v_dn_conv, v_dn_a_log, v_dn_dt_bias, v_dn_norm, v_q_norm, v_k_norm, v_attn_sink, v_w_branch_dn, v_w_branch_attn, v_w_out, v_ffn_up, v_ffn_conv, v_ffn_conv_b, v_ffn_down):
    me = 4 * lax.axis_index("x") + 2 * lax.axis_index("y") + lax.axis_index("c")
    ada_cols = w_ada.shape[2]

    cols = lambda a: jnp.swapaxes(a, 0, 1).reshape(a.shape[1], -1)
    rows = lambda a: a.reshape(-1, a.shape[2])
    col_blocks = lambda g: jnp.swapaxes(g.reshape(g.shape[0], N_DEV, -1), 0, 1)
    row_blocks = lambda g: g.reshape(N_DEV, -1, g.shape[1])

    gathered = _exchange([w_in[0].astype(BF), c, dn_conv[0], ffn_conv[0]], scatter=False, name="gather_first")
    w_in_packed = _pack_w_in(cols(gathered[0]))
    c_all = gathered[1][:, 0, :]

    def project_in(h1):
        p, rest = _mm(h1, w_in_packed, form="nn", out_dtype=F32, name="in_proj",
                      exchange=([w_branch_dn[0].astype(BF), w_branch_attn[0].astype(BF), w_out[0].astype(BF),
                                 ffn_up[0].astype(BF), ffn_down[0].astype(BF)], False))
        return p, (w_in_packed, rows(rest[0]), rows(rest[1]), rows(rest[2]), cols(rest[3]), rows(rest[4]))

    def project_back(h1, dp, w_in_p, grads):
        g_bdn, g_bat, g_out, g_up, g_down = grads
        g_in, landed_rest = _mm(h1, dp, form="tn", out_dtype=BF, name="g_w_in",
                                exchange=([row_blocks(g_bdn), row_blocks(g_bat), row_blocks(g_out), col_blocks(g_up),
                                           row_blocks(g_down)], True))
        dh1, landed_in = _mm(dp, w_in_p, form="nt", out_dtype=F32, name="d_h1",
                             exchange=([col_blocks(_unpack_w_in(g_in))], True))
        return [landed_in[0]] + landed_rest, dh1

    c16 = jnp.concatenate([c_all, c_ctx[None], jnp.zeros((7, D), F32)], axis=0)
    b_loc = lax.dynamic_slice_in_dim(b_ada, me * ada_cols, ada_cols, axis=1)
    mod_part = _ada_fwd(c16, w_ada[0], b_loc)
    mod_all = cols(_exchange([mod_part], scatter=False, name="gather_mod")[0])
    mod_x = lax.dynamic_slice_in_dim(mod_all, me, 1, axis=0).reshape(6, D)
    mod_c = mod_all[8].reshape(6, D)

    loss_loc, grad_x, landed, small = _local_step(
        x[0], ctx[0], mod_x, mod_c, loss_target[0], project_in, project_back,
        norm_mix, norm_ffn, cols(gathered[2]), dn_a_log[0], dn_dt_bias[0], dn_norm, q_norm, k_norm, attn_sink[0], cols(gathered[3]),
        ffn_conv_b)
    loss = lax.psum(loss_loc, ("x", "y", "c"))

    res = {}
    res["w_in"] = _adamw(landed[0], w_in[0], m_w_in[0], v_w_in[0], "adamw_w_in")
    res["w_branch_dn"] = _adamw(landed[1], w_branch_dn[0], m_w_branch_dn[0], v_w_branch_dn[0], "adamw_w_branch_dn")
    res["w_branch_attn"] = _adamw(landed[2], w_branch_attn[0], m_w_branch_attn[0], v_w_branch_attn[0], "adamw_w_branch_attn")
    res["w_out"] = _adamw(landed[3], w_out[0], m_w_out[0], v_w_out[0], "adamw_w_out")
    res["ffn_up"] = _adamw(landed[4], ffn_up[0], m_ffn_up[0], v_ffn_up[0], "adamw_ffn_up")
    res["ffn_down"] = _adamw(landed[5], ffn_down[0], m_ffn_down[0], v_ffn_down[0], "adamw_ffn_down")

    small = dict(small)
    small["b_ada"] = small["dmod_x"] + small["dmod_c"]
    parts = _exchange([_pack_small(small)], scatter=False, name="gather_small")[0]
    per_dev = _unpack_small(parts)
    given = dict(b_ada=(b_ada, m_b_ada, v_b_ada), norm_mix=(norm_mix, m_norm_mix, v_norm_mix), norm_ffn=(norm_ffn, m_norm_ffn, v_norm_ffn),
                 dn_a_log=(dn_a_log, m_dn_a_log, v_dn_a_log), dn_dt_bias=(dn_dt_bias, m_dn_dt_bias, v_dn_dt_bias),
                 dn_norm=(dn_norm, m_dn_norm, v_dn_norm), q_norm=(q_norm, m_q_norm, v_q_norm), k_norm=(k_norm, m_k_norm, v_k_norm),
                 attn_sink=(attn_sink, m_attn_sink, v_attn_sink), ffn_conv_b=(ffn_conv_b, m_ffn_conv_b, v_ffn_conv_b))
    packs = [_pack_small({k: t[j] for k, t in given.items()}) for j in range(3)]
    upd = [_unpack_small(a) for a in _adamw(parts, packs[0], packs[1], packs[2], "adamw_small")]
    for k, t in given.items():
        res[k] = tuple(u[k].reshape(t[0].shape) for u in upd)
    dnc = lax.dynamic_slice_in_dim(upd[0]["dn_conv"].reshape(5, 3 * D), me * dn_conv.shape[2], dn_conv.shape[2], axis=1)
    ffc = lax.dynamic_slice_in_dim(upd[0]["ffn_conv"].reshape(3, 2 * DFF), me * ffn_conv.shape[2], ffn_conv.shape[2], axis=1)
    r8 = lambda a: _pad_rows8(a)
    t = _adamw(r8(dnc)[None], r8(dn_conv[0]), r8(m_dn_conv[0]), r8(v_dn_conv[0]), "adamw_dn_conv")
    res["dn_conv"] = tuple(a[:5][None] for a in t)
    t = _adamw(r8(ffc)[None], r8(ffn_conv[0]), r8(m_ffn_conv[0]), r8(v_ffn_conv[0]), "adamw_ffn_conv")
    res["ffn_conv"] = tuple(a[:3][None] for a in t)

    dmx = lax.dynamic_slice_in_dim(per_dev["dmod_x"], me * ada_cols, ada_cols, axis=1)
    dmc = lax.dynamic_slice_in_dim(per_dev["dmod_c"], me * ada_cols, ada_cols, axis=1)
    g_ada, pc = _ada_bwd(c16, w_ada[0], dmx, dmc)
    res["w_ada"] = _adamw(g_ada[None], w_ada[0], m_w_ada[0], v_w_ada[0], "adamw_w_ada")
    pc_all = _exchange([pc], scatter=False, name="gather_cctx")[0]
    g_cctx = _cctx_grad(pc_all, c_ctx[None])
    r8b = lambda a: jnp.broadcast_to(a, (8, D))
    t = _adamw(r8b(g_cctx)[None], r8b(c_ctx[None]), r8b(m_c_ctx[None]), r8b(v_c_ctx[None]), "adamw_c_ctx")
    res["c_ctx"] = tuple(a[0] for a in t)

    names = ("c_ctx", "w_ada", "b_ada", "norm_mix", "norm_ffn", "w_in", "dn_conv", "dn_a_log", "dn_dt_bias", "dn_norm", "q_norm",
             "k_norm", "attn_sink", "w_branch_dn", "w_branch_attn", "w_out", "ffn_up", "ffn_conv", "ffn_conv_b", "ffn_down")
    lead = ("w_ada", "w_in", "w_branch_dn", "w_branch_attn", "w_out", "ffn_up", "ffn_down")
    fix = lambda k, a: a[None] if k in lead else a
    outs = [loss, grad_x[None]]
    for j in range(4):
        outs += [fix(k, res[k][j]) for k in names]
    return tuple(outs)
```

```python
import functools

import jax
import jax.numpy as jnp
from jax import lax
from jax.experimental import pallas as pl
from jax.experimental.pallas import tpu as pltpu

F32 = jnp.float32
BF = jnp.bfloat16
HI = lax.Precision.HIGHEST
MESH = pl.DeviceIdType.MESH

D = 1024
NH = 8
HD = 128
KVH = 2
GRP = 4
KV = KVH * HD
DFF = 2816
CB = 128
GRID_W = 64
ROPE_BASE = 10000.0
EPS = 1e-6
N_DEV = 8
PW = 8192
O_QKV, O_GT, O_Q, O_MG, O_K, O_V, O_BA = 0, 3072, 4096, 5120, 7168, 7424, 7680
IN_SIZES = (3072, 1024, 16, 16, 1024, 256, 256, 2048)
IN_DIM = sum(IN_SIZES)
ADAM_LR, ADAM_B1, ADAM_B2, ADAM_EPS, ADAM_WD, ADAM_STEP = 0.001, 0.9, 0.999, 1e-08, 0.01, 10
VMEM_LIMIT = 56 * 1024 * 1024


def _cp():
    return pltpu.CompilerParams(vmem_limit_bytes=VMEM_LIMIT)


def _tile(n, cands):
    for c in cands:
        if n % c == 0:
            return c
    return n


def _iota2(shape):
    return lax.broadcasted_iota(jnp.int32, shape, 0), lax.broadcasted_iota(jnp.int32, shape, 1)


_DIMS = {"nn": ((1,), (0,)), "nt": ((1,), (1,)), "tn": ((0,), (0,))}


def _exchange_copies(ins, outs, send_sems, recv_sems, local_sems, scatter, landings):
    x, y, c = lax.axis_index("x"), lax.axis_index("y"), lax.axis_index("c")
    me = 4 * x + 2 * y + c
    local, remote = [], []
    for k in range(len(ins)):
        local.append(pltpu.make_async_copy(ins[k].at[me] if scatter else ins[k], outs[k].at[me], local_sems.at[k]))
        for m in range(1, N_DEV):
            px = 1 - x if m & 4 else x
            py = 1 - y if m & 2 else y
            pc = 1 - c if m & 1 else c
            peer = 4 * px + 2 * py + pc
            src = ins[k].at[peer] if scatter else ins[k]
            sem = k * (N_DEV - 1) + m - 1
            push = pltpu.make_async_remote_copy(src_ref=src, dst_ref=outs[k].at[me], send_sem=send_sems.at[sem],
                                                recv_sem=recv_sems.at[sem], device_id=(px, py, pc), device_id_type=MESH)
            landing = None
            if landings:
                landing = pltpu.make_async_remote_copy(src_ref=src, dst_ref=outs[k].at[peer], send_sem=send_sems.at[sem],
                                                       recv_sem=recv_sems.at[sem], device_id=(px, py, pc), device_id_type=MESH)
            remote.append((push, landing))
    return local, remote


def _exchange_start(*args):
    local, remote = _exchange_copies(*args, landings=False)
    for cp in local:
        cp.start()
    for push, _ in remote:
        push.start()


def _exchange_wait(*args):
    local, remote = _exchange_copies(*args, landings=True)
    for _, landing in remote:
        landing.wait_recv()
    for push, _ in remote:
        push.wait_send()
    for cp in local:
        cp.wait()


def _exchange_shapes(arrays, scatter):
    out_shape = [jax.ShapeDtypeStruct(a.shape if scatter else (N_DEV,) + a.shape, a.dtype) for a in arrays]
    n = len(arrays)
    sems = [pltpu.SemaphoreType.DMA((n * (N_DEV - 1),)), pltpu.SemaphoreType.DMA((n * (N_DEV - 1),)), pltpu.SemaphoreType.DMA((n,))]
    return out_shape, sems


def _mm(a, b, *, form, out_dtype, name, tm=None, tn=None, tk=None, exchange=None):
    if form == "tn":
        K, M = a.shape
        N = b.shape[1]
    else:
        M, K = a.shape
        N = b.shape[0] if form == "nt" else b.shape[1]
    tm = tm or _tile(M, (1408, 1280, 1024, 640, 512, 256, 128))
    tn = tn or _tile(N, (1408, 1024, 512, 256, 128))
    tk = tk or _tile(K, (2048, 1408, 1280, 1024, 640, 512, 256, 128))
    ni, nj, nk = M // tm, N // tn, K // tk
    dims = (_DIMS[form], ((), ()))
    ex_arrays, scatter = exchange if exchange else ([], False)
    nx = len(ex_arrays)

    def body(a_ref, b_ref, *refs):
        ex_in, o_ref, ex_out, scratch = refs[:nx], refs[nx], refs[nx + 1:2 * nx + 1], refs[2 * nx + 1:]
        i, j, k = pl.program_id(0), pl.program_id(1), pl.program_id(2)
        if nx:
            sems = scratch[-3:]

            @pl.when((i == 0) & (j == 0) & (k == 0))
            def _():
                _exchange_start(ex_in, ex_out, *sems, scatter)

        part = lax.dot_general(a_ref[...].astype(BF), b_ref[...].astype(BF), dims, preferred_element_type=F32)
        if nk == 1:
            o_ref[...] = part.astype(out_dtype)
        else:
            acc_ref = scratch[0]

            @pl.when(k == 0)
            def _():
                acc_ref[...] = part

            @pl.when(k > 0)
            def _():
                acc_ref[...] += part

            @pl.when(k == nk - 1)
            def _():
                o_ref[...] = acc_ref[...].astype(out_dtype)

        if nx:
            @pl.when((i == ni - 1) & (j == nj - 1) & (k == nk - 1))
            def _():
                _exchange_wait(ex_in, ex_out, *sems, scatter)

    if form == "tn":
        a_spec = pl.BlockSpec((tk, tm), lambda i, j, k: (k, i))
    else:
        a_spec = pl.BlockSpec((tm, tk), lambda i, j, k: (i, k))
    if form == "nt":
        b_spec = pl.BlockSpec((tn, tk), lambda i, j, k: (j, k))
    else:
        b_spec = pl.BlockSpec((tk, tn), lambda i, j, k: (k, j))
    hbm = pl.BlockSpec(memory_space=pl.ANY)
    ex_shapes, ex_sems = _exchange_shapes(ex_arrays, scatter) if nx else ([], [])
    outs = pl.pallas_call(
        body, grid=(ni, nj, nk), name=name,
        in_specs=[a_spec, b_spec] + [hbm] * nx, out_specs=[pl.BlockSpec((tm, tn), lambda i, j, k: (i, j))] + [hbm] * nx,
        out_shape=[jax.ShapeDtypeStruct((M, N), out_dtype)] + ex_shapes,
        scratch_shapes=([] if nk == 1 else [pltpu.VMEM((tm, tn), F32)]) + ex_sems,
        compiler_params=_cp(),
    )(a, b, *ex_arrays)
    return (outs[0], list(outs[1:])) if nx else outs[0]


def _norm_mod_fn(x, nw, sh, sc):
    y = x * lax.rsqrt(jnp.mean(x * x, axis=-1, keepdims=True) + EPS)
    return (y * nw) * (1.0 + sc) + sh


def _norm_mod_fwd(x, ctx, nw, sh, sc, name):
    nlat = x.shape[0]
    T = nlat + ctx.shape[0]
    tb = _tile(ctx.shape[0], (256, 128))
    nlb = nlat // tb

    def body(x_ref, c_ref, nw_ref, sh_ref, sc_ref, h_ref):
        rows = jnp.where(pl.program_id(0) < nlb, x_ref[...], c_ref[...])
        h_ref[...] = _norm_mod_fn(rows, nw_ref[...], sh_ref[0], sc_ref[0]).astype(BF)

    seg = pl.BlockSpec((1, 1, D), lambda i: (jnp.where(i >= nlb, 1, 0), 0, 0))
    return pl.pallas_call(
        body, grid=(T // tb,), name=name,
        in_specs=[pl.BlockSpec((tb, D), lambda i: (jnp.minimum(i, nlb - 1), 0)),
                  pl.BlockSpec((tb, D), lambda i: (jnp.maximum(i - nlb, 0), 0)), pl.BlockSpec((1, D), lambda i: (0, 0)), seg, seg],
        out_specs=pl.BlockSpec((tb, D), lambda i: (i, 0)),
        out_shape=jax.ShapeDtypeStruct((T, D), BF),
    )(x, ctx, nw, sh, sc)


def _norm_mod_bwd(x, nw, sh, sc, dh, *, row0, name, residual=None):
    nrows = x.shape[0]
    tb = _tile(nrows, (256, 128))
    b0 = row0 // tb

    def body(x_ref, nw_ref, sh_ref, sc_ref, dh_ref, *refs):
        dnw_ref, dsh_ref, dsc_ref = refs[-3:]
        _, vjp = jax.vjp(_norm_mod_fn, x_ref[...], nw_ref[...], sh_ref[...], sc_ref[...])
        dx, dnw, dsh, dsc = vjp(dh_ref[...])
        if residual is not None:
            refs[1][...] = dx + refs[0][...]

        @pl.when(pl.program_id(0) == 0)
        def _():
            dnw_ref[...] = jnp.zeros_like(dnw_ref)
            dsh_ref[...] = jnp.zeros_like(dsh_ref)
            dsc_ref[...] = jnp.zeros_like(dsc_ref)

        dnw_ref[...] += dnw
        dsh_ref[...] += dsh
        dsc_ref[...] += dsc

    dh_row = pl.BlockSpec((tb, D), lambda i: (b0 + i, 0))
    out_row = pl.BlockSpec((tb, D), lambda i: (i, 0))
    one = pl.BlockSpec((1, D), lambda i: (0, 0))
    with_dx = residual is not None
    return pl.pallas_call(
        body, grid=(nrows // tb,), name=name,
        in_specs=[out_row, one, one, one, dh_row] + [out_row] * with_dx, out_specs=[out_row] * with_dx + [one] * 3,
        out_shape=[jax.ShapeDtypeStruct((nrows, D), F32)] * with_dx + [jax.ShapeDtypeStruct((1, D), F32)] * 3,
    )(x, nw, sh, sc, dh, *([residual] if with_dx else []))


def _resid_norm_fwd(x, gate, y, nw, sh, sc):
    n = y.shape[0]
    tb = _tile(n, (256, 128))

    def body(x_ref, g_ref, y_ref, nw_ref, sh_ref, sc_ref, x1_ref, h_ref):
        x1 = x_ref[...] + g_ref[...] * y_ref[...]
        x1_ref[...] = x1
        h_ref[...] = _norm_mod_fn(x1, nw_ref[...], sh_ref[...], sc_ref[...]).astype(BF)

    row = pl.BlockSpec((tb, D), lambda i: (i, 0))
    one = pl.BlockSpec((1, D), lambda i: (0, 0))
    return pl.pallas_call(
        body, grid=(n // tb,), name="resid_norm_fwd",
        in_specs=[row, one, row, one, one, one], out_specs=[row, row],
        out_shape=[jax.ShapeDtypeStruct((n, D), F32), jax.ShapeDtypeStruct((n, D), BF)],
    )(x, gate, y, nw, sh, sc)


def _resid_norm_bwd(x1, gate, y, nw, sh, sc, dh, dx1_direct, w_out):
    n = y.shape[0]
    tb = _tile(n, (256, 128))

    def body(x1_ref, g_ref, y_ref, nw_ref, sh_ref, sc_ref, dh_ref, dd_ref, wo_ref,
             dx_ref, dy_ref, dg_ref, dnw_ref, dsh_ref, dsc_ref, dm_ref):
        _, vjp = jax.vjp(_norm_mod_fn, x1_ref[...], nw_ref[...], sh_ref[...], sc_ref[...])
        dxn, dnw, dsh, dsc = vjp(dh_ref[...])
        dx = dxn + dd_ref[...]
        dx_ref[...] = dx
        dy_ref[...] = (g_ref[...] * dx).astype(BF)
        dm_ref[...] = lax.dot_general(dy_ref[...], wo_ref[...], (_DIMS["nt"], ((), ())), preferred_element_type=F32)

        @pl.when(pl.program_id(0) == 0)
        def _():
            for r in (dg_ref, dnw_ref, dsh_ref, dsc_ref):
                r[...] = jnp.zeros_like(r)

        dg_ref[...] += jnp.sum(dx * y_ref[...], axis=0, keepdims=True)
        dnw_ref[...] += dnw
        dsh_ref[...] += dsh
        dsc_ref[...] += dsc

    row = pl.BlockSpec((tb, D), lambda i: (i, 0))
    one = pl.BlockSpec((1, D), lambda i: (0, 0))
    return pl.pallas_call(
        body, grid=(n // tb,), name="resid_norm_bwd",
        in_specs=[row, one, row, one, one, one, row, row, _resident((D, D))], out_specs=[row, row] + [one] * 4 + [row],
        out_shape=[jax.ShapeDtypeStruct((n, D), F32), jax.ShapeDtypeStruct((n, D), BF)] + [jax.ShapeDtypeStruct((1, D), F32)] * 4
        + [jax.ShapeDtypeStruct((n, D), F32)],
    )(x1, gate, y, nw, sh, sc, dh, dx1_direct, w_out)


HALO = 8


def _halo_specs(tb, width, nrows, col=0, halo=HALO):
    r8 = tb // halo
    cur = pl.BlockSpec((tb, width), lambda i: (i, col))
    prev = pl.BlockSpec((halo, width), lambda i: (jnp.maximum(i * r8 - 1, 0), col))
    nxt = pl.BlockSpec((halo, width), lambda i: (jnp.minimum((i + 1) * r8, nrows // halo - 1), col))
    return [cur, prev, nxt]


def _segment_edges(seg_rows, tb):
    bounds = [0]
    for s in seg_rows:
        bounds.append(bounds[-1] + s // tb)
    return bounds[:-1], [b - 1 for b in bounds[1:]]


def _keep_halos(i, starts, ends):
    keep_p = functools.reduce(lambda a, b: a & b, [i != s for s in starts])
    keep_n = functools.reduce(lambda a, b: a & b, [i != e for e in ends])
    return keep_p, keep_n


def _ext_rows(refs, cols, keep):
    cur_ref, prev_ref, next_ref = refs
    p = jnp.where(keep[0], prev_ref[:, cols].astype(F32), 0.0)
    n = jnp.where(keep[1], next_ref[:, cols].astype(F32), 0.0)
    return jnp.concatenate([p, cur_ref[:, cols].astype(F32), n], axis=0)


def _shifted_rows(xe, width, transpose=False):
    r = width // 2
    n = xe.shape[0]
    out = []
    for j in range(width):
        s = ((j - r) if transpose else (r - j)) % n
        out.append(xe if s == 0 else pltpu.roll(xe, s, 0))
    return out


def _conv_rows(shifted, w_ref, cols):
    acc = None
    for j, xs in enumerate(shifted):
        term = xs * w_ref[j:j + 1, cols]
        acc = term if acc is None else acc + term
    return acc


def _tap_grads(dcur, shifted, tb):
    return [jnp.sum(dcur * xs[HALO:HALO + tb], axis=0, keepdims=True) for xs in shifted]


def _softplus(x):
    return jnp.maximum(x, 0.0) + jnp.log(1.0 + jnp.exp(-jnp.abs(x)))


def _gates_fn(ba, alog_row, dt_row):
    col = lax.broadcasted_iota(jnp.int32, ba.shape, 1)
    beta = jax.nn.sigmoid(ba)
    g = -jnp.exp(alog_row) * _softplus(ba + dt_row)
    return jnp.where(col < 16, beta, jnp.where(col < 32, g, 0.0))


def _qkv_post_fn(c, kind):
    y = jax.nn.silu(c)
    if kind == 2:
        return y
    n = y * lax.rsqrt(jnp.sum(y * y, axis=-1, keepdims=True) + EPS)
    return n * (HD ** -0.5) if kind == 0 else n


DN_TAPS = 5
FFN_TAPS = 3


def _dn_pre_fwd(p, w8, alog_row, dt_row, seg_rows):
    T = p.shape[0]
    tb = _tile(T, (256, 128))
    starts, ends = _segment_edges(seg_rows, tb)

    def body(cur_ref, prev_ref, next_ref, ba_ref, w_ref, al_ref, dt_ref, q_ref, k_ref, v_ref, gb_ref):
        keep = _keep_halos(pl.program_id(0), starts, ends)
        outs = (q_ref, k_ref, v_ref)
        for kind in range(3):
            for h in range(NH):
                cols = slice(kind * D + h * HD, kind * D + (h + 1) * HD)
                xe = _ext_rows((cur_ref, prev_ref, next_ref), cols, keep)
                conv = _conv_rows(_shifted_rows(xe, DN_TAPS), w_ref, cols)[HALO:HALO + tb]
                outs[kind][:, h * HD:(h + 1) * HD] = _qkv_post_fn(conv, kind)
        gb_ref[...] = _gates_fn(ba_ref[...], al_ref[...], dt_ref[...])

    row = pl.BlockSpec((tb, D), lambda i: (i, 0))
    one = pl.BlockSpec((1, 128), lambda i: (0, 0))
    return pl.pallas_call(
        body, grid=(T // tb,), name="dn_pre_fwd",
        in_specs=_halo_specs(tb, 3 * D, T) + [pl.BlockSpec((tb, 128), lambda i: (i, O_BA // 128)),
                                              pl.BlockSpec((8, 3 * D), lambda i: (0, 0)), one, one],
        out_specs=[row, row, row, pl.BlockSpec((tb, 128), lambda i: (i, 0))],
        out_shape=[jax.ShapeDtypeStruct((T, D), F32)] * 3 + [jax.ShapeDtypeStruct((T, 128), F32)],
        compiler_params=_cp(),
    )(p, p, p, p, w8, alog_row, dt_row)


def _dn_pre_bwd(p, w8, alog_row, dt_row, dq, dk, dv, dgb, others, seg_rows):
    T = p.shape[0]
    tb = _tile(T, (256, 128))
    starts, ends = _segment_edges(seg_rows, tb)
    other_cols = (O_GT, O_Q, O_MG, O_K, O_V)
    assert [o.shape[1] for o in others] == [O_Q - O_GT, O_MG - O_Q, O_K - O_MG, O_V - O_K, O_BA - O_V]

    def body(cur_ref, prev_ref, next_ref, ba_ref, w_ref, al_ref, dt_ref,
             dq_c, dq_p, dq_n, dk_c, dk_p, dk_n, dv_c, dv_p, dv_n, dgb_ref, gt_ref, q_ref, mg_ref, k_ref, v_ref,
             dx_ref, dw_ref, dal_ref, ddt_ref):
        i = pl.program_id(0)
        for c0, ref in zip(other_cols, (gt_ref, q_ref, mg_ref, k_ref, v_ref)):
            dx_ref[:, c0:c0 + ref.shape[1]] = ref[...]
        dx_ref[:, O_BA + 128:] = jnp.zeros((tb, PW - O_BA - 128), BF)
        keep = _keep_halos(i, starts, ends)

        @pl.when(i == 0)
        def _():
            dw_ref[...] = jnp.zeros_like(dw_ref)
            dal_ref[...] = jnp.zeros_like(dal_ref)
            ddt_ref[...] = jnp.zeros_like(ddt_ref)

        douts = ((dq_c, dq_p, dq_n), (dk_c, dk_p, dk_n), (dv_c, dv_p, dv_n))
        for kind in range(3):
            for h in range(NH):
                cols = slice(kind * D + h * HD, kind * D + (h + 1) * HD)
                xe = _ext_rows((cur_ref, prev_ref, next_ref), cols, keep)
                shifted = _shifted_rows(xe, DN_TAPS)
                conv = _conv_rows(shifted, w_ref, cols)
                dye = _ext_rows(douts[kind], slice(h * HD, (h + 1) * HD), keep)
                _, vjp = jax.vjp(functools.partial(_qkv_post_fn, kind=kind), conv)
                dce = vjp(dye)[0]
                dx_ref[:, cols] = _conv_rows(_shifted_rows(dce, DN_TAPS, transpose=True), w_ref, cols)[HALO:HALO + tb].astype(BF)
                for j, g in enumerate(_tap_grads(dce[HALO:HALO + tb], shifted, tb)):
                    dw_ref[j:j + 1, cols] += g
        _, vjp = jax.vjp(_gates_fn, ba_ref[...], al_ref[...], dt_ref[...])
        dba, dal, ddt = vjp(dgb_ref[...])
        dx_ref[:, O_BA:O_BA + 128] = dba.astype(BF)
        dal_ref[...] += dal
        ddt_ref[...] += ddt

    one = pl.BlockSpec((1, 128), lambda i: (0, 0))
    nar = pl.BlockSpec((tb, 128), lambda i: (i, 0))
    wspec = pl.BlockSpec((8, 3 * D), lambda i: (0, 0))
    return pl.pallas_call(
        body, grid=(T // tb,), name="dn_pre_bwd",
        in_specs=_halo_specs(tb, 3 * D, T) + [pl.BlockSpec((tb, 128), lambda i: (i, O_BA // 128)), wspec, one, one]
        + _halo_specs(tb, D, T) * 3 + [nar] + [pl.BlockSpec((tb, o.shape[1]), lambda i: (i, 0)) for o in others],
        out_specs=[pl.BlockSpec((tb, PW), lambda i: (i, 0)), wspec, one, one],
        out_shape=[jax.ShapeDtypeStruct((T, PW), BF), jax.ShapeDtypeStruct((8, 3 * D), F32),
                   jax.ShapeDtypeStruct((1, 128), F32), jax.ShapeDtypeStruct((1, 128), F32)],
        compiler_params=_cp(),
    )(p, p, p, p, w8, alog_row, dt_row, dq, dq, dq, dk, dk, dk, dv, dv, dv, dgb, *others)


def _dot_hi(a, b):
    return jnp.dot(a, b, precision=HI, preferred_element_type=F32)


def _dot_bf(a, b):
    return jnp.dot(a.astype(BF), b.astype(BF), preferred_element_type=F32)


def _dot_nt_bf(a, b):
    return lax.dot_general(a.astype(BF), b.astype(BF), (_DIMS["nt"], ((), ())), preferred_element_type=F32)


def _dot_tn_bf(a, b):
    return lax.dot_general(a.astype(BF), b.astype(BF), (_DIMS["tn"], ((), ())), preferred_element_type=F32)


def _dot_h3(a, b):
    return jnp.dot(a, b, precision=lax.Precision.HIGH, preferred_element_type=F32)


def _dot_split(fine, coarse, form):
    hi = fine.astype(BF)
    lo = (fine - hi.astype(F32)).astype(BF)
    cb = coarse.astype(BF)
    if form == "tn":
        return lax.dot_general(jnp.concatenate([cb, cb], axis=0), jnp.concatenate([hi, lo], axis=0),
                               (_DIMS["tn"], ((), ())), preferred_element_type=F32)
    parts = jnp.concatenate([hi, lo], axis=1)
    if form == "nt":
        return lax.dot_general(parts, jnp.concatenate([cb, cb], axis=1), (_DIMS["nt"], ((), ())), preferred_element_type=F32)
    return jnp.dot(parts, jnp.concatenate([cb, cb], axis=0), preferred_element_type=F32)


@jax.custom_vjp
def _mm_split(a, b):
    return _dot_split(a, b, "nn")


_mm_split.defvjp(lambda a, b: (_dot_split(a, b, "nn"), (a, b)),
                 lambda res, dc: (_dot_split(dc, res[1], "nt"), _dot_split(dc, res[0], "tn")))


def _unit_tri_inverses(mats):
    r, c = _iota2((CB, CB))
    eye = (r == c).astype(F32)
    a8 = [jnp.where((r // 8) == (c // 8), a, 0.0) for a in mats]
    a2 = [_dot_split(x, x, "nn") for x in a8]
    a4 = [_dot_split(x, x, "nn") for x in a2]
    t = [_dot_split(eye - x, eye + y, "nn") for x, y in zip(a8, a2)]
    t = [_dot_split(x, eye + y, "nn") for x, y in zip(t, a4)]
    b = 8
    while b < CB:
        mask = ((r // (2 * b)) == (c // (2 * b))) & ((r // b) != (c // b))
        te = [_dot_split(x, jnp.where(mask, a, 0.0), "nn") for x, a in zip(t, mats)]
        t = [x - _dot_split(y, x, "nn") for x, y in zip(t, te)]
        b *= 2
    return t


@jax.custom_vjp
def _saved_inverse(a, t):
    return t


_saved_inverse.defvjp(lambda a, t: (t, t),
                      lambda t, dt: (-_dot_split(_dot_split(dt, t, "nt"), t, "tn"), jnp.zeros_like(t)))


def _dn1_decay(gc, reverse):
    r, c = _iota2((CB, CB))
    incl = (c >= r) if reverse else (c <= r)
    return jnp.where(incl, jnp.exp(jnp.where(incl, gc - gc.T, 0.0)), 0.0)


def _dn1_heads(qs, ks, vs, betas, gcs, ts_saved, reverse, kks=None, qks=None):
    r, c = _iota2((CB, CB))
    strict = (c > r) if reverse else (c < r)
    decays = [_dn1_decay(gc, reverse) for gc in gcs]
    kks = kks or [_dot_nt_bf(k, k) for k in ks]
    systems = [jnp.where(strict, b * kk * dc, 0.0) for b, kk, dc in zip(betas, kks, decays)]
    if ts_saved is None:
        ts = _unit_tri_inverses(systems)
    else:
        ts = [_saved_inverse(a, t) for a, t in zip(systems, ts_saved)]
    egs = [jnp.exp(gc) for gc in gcs]
    us = [_mm_split(t, v * b) for t, v, b in zip(ts, vs, betas)]
    ws = [_mm_split(t, k * (b * eg)) for t, k, b, eg in zip(ts, ks, betas, egs)]
    qks = qks or [_dot_nt_bf(q, k) for q, k in zip(qs, ks)]
    last = 0 if reverse else CB - 1
    glogs = [jnp.sum(jnp.where(r == last, gc, 0.0), axis=0, keepdims=True) for gc in gcs]
    outs = [(u, w, q * eg, k * jnp.exp(gl - gc), qk * dc, jnp.exp(gl))
            for u, w, q, k, eg, gl, gc, qk, dc in zip(us, ws, qs, ks, egs, glogs, gcs, qks, decays)]
    return outs, ts


def _cum_matrix(upper):
    r, c = _iota2((CB, CB))
    return ((c >= r) if upper else (c <= r)).astype(F32)


def _lane_bcast(x, col):
    return jnp.broadcast_to(x[:, col:col + 1], x.shape)


_HEAD_SLICES = [slice(h * HD, (h + 1) * HD) for h in range(NH)]


def _dn1_fwd(q, k, v, gb):
    T = q.shape[0]
    nb = T // CB

    def body(q_ref, k_ref, v_ref, gb_ref, *out_refs):
        gbv = gb_ref[...]
        qs = [q_ref[:, sl] for sl in _HEAD_SLICES]
        ks = [k_ref[:, sl] for sl in _HEAD_SLICES]
        vs = [v_ref[:, sl] for sl in _HEAD_SLICES]
        kks = [_dot_nt_bf(x, x) for x in ks]
        qks = [_dot_nt_bf(x, y) for x, y in zip(qs, ks)]
        for d in (0, 1):
            u_ref, w_ref, qg_ref, kd_ref, qkd_ref, gl_ref, t_ref = out_refs[7 * d:7 * d + 7]
            gcum = _dot_h3(_cum_matrix(d == 1), gbv)
            betas = [_lane_bcast(gbv, d * NH + h) for h in range(NH)]
            gcs = [_lane_bcast(gcum, 16 + d * NH + h) for h in range(NH)]
            outs, ts = _dn1_heads(qs, ks, vs, betas, gcs, None, d == 1, kks, qks)
            for h, sl in enumerate(_HEAD_SLICES):
                u, w, qg, kd, qkd, gl = outs[h]
                u_ref[:, sl] = u
                w_ref[:, sl] = w.astype(BF)
                qg_ref[:, sl] = qg.astype(BF)
                kd_ref[:, sl] = kd.astype(BF)
                qkd_ref[:, sl] = qkd.astype(BF)
                gl_ref[h] = gl
                t_ref[:, sl] = ts[h]

    tb = pl.BlockSpec((CB, D), lambda i: (i, 0))
    one_dir_specs = [tb, tb, tb, tb, tb, pl.BlockSpec((NH, 1, 128), lambda i: (i, 0, 0)), tb]
    one_dir_shapes = ([jax.ShapeDtypeStruct((T, D), F32)] + [jax.ShapeDtypeStruct((T, D), BF)] * 4
                      + [jax.ShapeDtypeStruct((nb * NH, 1, 128), F32), jax.ShapeDtypeStruct((T, D), F32)])
    outs = pl.pallas_call(
        body, grid=(nb,), name="dn1_fwd",
        in_specs=[tb, tb, tb, pl.BlockSpec((CB, 128), lambda i: (i, 0))],
        out_specs=one_dir_specs * 2, out_shape=one_dir_shapes * 2, compiler_params=_cp(),
    )(q, k, v, gb)
    return [tuple(outs[:7]), tuple(outs[7:])]


def _dn1_bwd(q, k, v, gb, tinvs, cots):
    T = q.shape[0]
    nb = T // CB

    def body(q_ref, k_ref, v_ref, gb_ref, *refs):
        dir_refs, (dq_ref, dk_ref, dv_ref, dgb_ref) = refs[:14], refs[14:]
        gbv = gb_ref[...]
        qs = [q_ref[:, sl] for sl in _HEAD_SLICES]
        ks = [k_ref[:, sl] for sl in _HEAD_SLICES]
        vs = [v_ref[:, sl] for sl in _HEAD_SLICES]
        lane = lax.broadcasted_iota(jnp.int32, (CB, 128), 1)
        dgb = jnp.zeros((CB, 128), F32)
        for d in (0, 1):
            t_ref, du_ref, dw_ref, dqg_ref, dkd_ref, dqkd_ref, dgl_ref = dir_refs[7 * d:7 * d + 7]
            gcum = _dot_h3(_cum_matrix(d == 1), gbv)
            betas = [_lane_bcast(gbv, d * NH + h) for h in range(NH)]
            gcs = [_lane_bcast(gcum, 16 + d * NH + h) for h in range(NH)]
            ts = [t_ref[:, sl] for sl in _HEAD_SLICES]
            f = lambda qs, ks, vs, betas, gcs: _dn1_heads(qs, ks, vs, betas, gcs, ts, d == 1)[0]
            _, vjp = jax.vjp(f, qs, ks, vs, betas, gcs)
            cot = [(du_ref[:, sl], dw_ref[:, sl], dqg_ref[:, sl], dkd_ref[:, sl], dqkd_ref[:, sl], dgl_ref[h])
                   for h, sl in enumerate(_HEAD_SLICES)]
            dqs, dks, dvs, dbetas, dgcs = vjp(cot)
            dgcum = jnp.zeros((CB, 128), F32)
            for h, sl in enumerate(_HEAD_SLICES):
                if d == 0:
                    dq_ref[:, sl] = dqs[h]
                    dk_ref[:, sl] = dks[h]
                    dv_ref[:, sl] = dvs[h]
                else:
                    dq_ref[:, sl] += dqs[h]
                    dk_ref[:, sl] += dks[h]
                    dv_ref[:, sl] += dvs[h]
                dgb = dgb + jnp.where(lane == d * NH + h, jnp.sum(dbetas[h], axis=1, keepdims=True), 0.0)
                dgcum = dgcum + jnp.where(lane == 16 + d * NH + h, jnp.sum(dgcs[h], axis=1, keepdims=True), 0.0)
            dgb = dgb + _dot_h3(_cum_matrix(d == 0), dgcum)
        dgb_ref[...] = dgb

    tb = pl.BlockSpec((CB, D), lambda i: (i, 0))
    gbs = pl.BlockSpec((CB, 128), lambda i: (i, 0))
    gls = pl.BlockSpec((NH, 1, 128), lambda i: (i, 0, 0))
    args = []
    for d in (0, 1):
        args += [tinvs[d], *cots[d]]
    return pl.pallas_call(
        body, grid=(nb,), name="dn1_bwd",
        in_specs=[tb, tb, tb, gbs] + [tb, tb, tb, tb, tb, tb, gls] * 2, out_specs=[tb, tb, tb, gbs],
        out_shape=[jax.ShapeDtypeStruct((T, D), F32)] * 3 + [jax.ShapeDtypeStruct((T, 128), F32)],
        compiler_params=_cp(),
    )(q, k, v, gb, *args)


def _dn2_steps(chains):
    ws = [_dot_bf(w, s) for _, w, _, _, _, _, s in chains]
    v_new = [c[0] - x for c, x in zip(chains, ws)]
    o_state = [_dot_bf(c[2], c[6]) for c in chains]
    o_local = [_dot_bf(c[4], vn) for c, vn in zip(chains, v_new)]
    grow = [_dot_tn_bf(c[3], vn) for c, vn in zip(chains, v_new)]
    return [a + b for a, b in zip(o_state, o_local)], [c[6] * c[5] + g for c, g in zip(chains, grow)]


def _scan_order(direction, nlat_b, nall_b):
    if direction == 0:
        return lambda i: (i + nlat_b) % nall_b
    return lambda i: nall_b - 1 - i


def _dn2_fwd(per_dir, nlat):
    T = per_dir[0][0].shape[0]
    nb = T // CB
    blks = [_scan_order(d, nlat // CB, nb) for d in (0, 1)]

    def body(*refs):
        ins, outs, s_scr = refs[:12], refs[12:16], refs[16]

        @pl.when(pl.program_id(0) == 0)
        def _():
            s_scr[...] = jnp.zeros_like(s_scr)
        for d in (0, 1):
            outs[2 * d + 1][0] = s_scr[d]
        where = [(d, h, sl) for h, sl in enumerate(_HEAD_SLICES) for d in (0, 1)]
        chains = []
        for d, h, sl in where:
            u_ref, w_ref, qg_ref, kd_ref, qkd_ref, gl_ref = ins[6 * d:6 * d + 6]
            chains.append((u_ref[:, sl], w_ref[:, sl], qg_ref[:, sl], kd_ref[:, sl], qkd_ref[:, sl], gl_ref[h], s_scr[d, h]))
        os, states = _dn2_steps(chains)
        for (d, h, sl), o, s_next in zip(where, os, states):
            outs[2 * d][:, sl] = o
            s_scr[d, h] = s_next

    in_specs, out_specs, args = [], [], []
    for d in (0, 1):
        blk = blks[d]
        tb = pl.BlockSpec((CB, D), lambda i, blk=blk: (blk(i), 0))
        in_specs += [tb] * 5 + [pl.BlockSpec((NH, 1, 128), lambda i, blk=blk: (blk(i), 0, 0))]
        out_specs += [tb, pl.BlockSpec((1, NH, HD, HD), lambda i, blk=blk: (blk(i), 0, 0, 0))]
        args += list(per_dir[d])
    outs = pl.pallas_call(
        body, grid=(nb,), name="dn2_fwd", in_specs=in_specs, out_specs=out_specs,
        out_shape=[jax.ShapeDtypeStruct((T, D), F32), jax.ShapeDtypeStruct((nb, NH, HD, HD), F32)] * 2,
        scratch_shapes=[pltpu.VMEM((2, NH, HD, HD), F32)], compiler_params=_cp(),
    )(*args)
    return [tuple(outs[:2]), tuple(outs[2:])]


def _dn2_bwd(per_dir, do, nlat):
    T = per_dir[0][0].shape[0]
    nb = T // CB
    nlat_b = nlat // CB
    fwd = [_scan_order(d, nlat_b, nb) for d in (0, 1)]
    blks = [lambda i, f=f: f(nb - 1 - i) for f in fwd]

    def body(*refs):
        ins, outs, ds_scr = refs[:16], refs[16:28], refs[28]
        i = pl.program_id(0)

        @pl.when(i == 0)
        def _():
            ds_scr[...] = jnp.zeros_like(ds_scr)
        where = [(d, h, sl) for h, sl in enumerate(_HEAD_SLICES) for d in (0, 1)]
        chains, cot_o, cot_s = [], [], []
        for d, h, sl in where:
            u_ref, w_ref, qg_ref, kd_ref, qkd_ref, gl_ref, sall_ref, do_ref = ins[8 * d:8 * d + 8]
            chains.append((u_ref[:, sl], w_ref[:, sl].astype(F32), qg_ref[:, sl].astype(F32), kd_ref[:, sl].astype(F32),
                           qkd_ref[:, sl].astype(F32), gl_ref[h], sall_ref[0, h]))
            cot_o.append(jnp.where(blks[d](i) < nlat_b, do_ref[:, sl], 0.0))
            cot_s.append(ds_scr[d, h])
        _, vjp = jax.vjp(_dn2_steps, chains)
        for (d, h, sl), (du, dw, dqg, dkd, dqkd, dgl, ds) in zip(where, vjp((cot_o, cot_s))[0]):
            du_ref, dw_ref, dqg_ref, dkd_ref, dqkd_ref, dgl_ref = outs[6 * d:6 * d + 6]
            du_ref[:, sl] = du
            dw_ref[:, sl] = dw
            dqg_ref[:, sl] = dqg
            dkd_ref[:, sl] = dkd
            dqkd_ref[:, sl] = dqkd
            dgl_ref[h] = dgl
            ds_scr[d, h] = ds

    in_specs, out_specs, args = [], [], []
    for d in (0, 1):
        blk = blks[d]
        tb = pl.BlockSpec((CB, D), lambda i, blk=blk: (blk(i), 0))
        gls = pl.BlockSpec((NH, 1, 128), lambda i, blk=blk: (blk(i), 0, 0))
        in_specs += [tb] * 5 + [gls, pl.BlockSpec((1, NH, HD, HD), lambda i, blk=blk: (blk(i), 0, 0, 0)),
                                pl.BlockSpec((CB, D), lambda i, blk=blk: (jnp.minimum(blk(i), nlat_b - 1), 0))]
        out_specs += [tb] * 5 + [gls]
        args += list(per_dir[d]) + [do]
    outs = pl.pallas_call(
        body, grid=(nb,), name="dn2_bwd", in_specs=in_specs, out_specs=out_specs,
        out_shape=([jax.ShapeDtypeStruct((T, D), F32)] * 5 + [jax.ShapeDtypeStruct((nb * NH, 1, 128), F32)]) * 2,
        scratch_shapes=[pltpu.VMEM((2, NH, HD, HD), F32)], compiler_params=_cp(),
    )(*args)
    return [tuple(outs[:6]), tuple(outs[6:])]


def _ghn_fn(o, gt, w):
    y = o * lax.rsqrt(jnp.mean(o * o, axis=-1, keepdims=True) + EPS)
    return (y * w) * jax.nn.silu(gt)


def _ghn_fwd(o_f, o_b, p, w, w_branch, nlat):
    tb = _tile(nlat, (256, 128))

    def body(of_ref, ob_ref, gt_ref, w_ref, wb_ref, y_ref, z_ref):
        for h in range(NH):
            sl = slice(h * HD, (h + 1) * HD)
            y_ref[:, sl] = _ghn_fn(of_ref[:, sl] + ob_ref[:, sl], gt_ref[:, sl], w_ref[...]).astype(BF)
        z_ref[...] = jnp.dot(y_ref[...], wb_ref[...], preferred_element_type=F32)

    row = pl.BlockSpec((tb, D), lambda i: (i, 0))
    return pl.pallas_call(
        body, grid=(nlat // tb,), name="ghn_fwd",
        in_specs=[row, row, pl.BlockSpec((tb, D), lambda i: (i, O_GT // D)), pl.BlockSpec((1, HD), lambda i: (0, 0)), _resident((D, D))],
        out_specs=[row, row], out_shape=[jax.ShapeDtypeStruct((nlat, D), BF), jax.ShapeDtypeStruct((nlat, D), F32)],
    )(o_f, o_b, p, w, w_branch)


def _ghn_bwd(o_f, o_b, p, w, dy, nlat):
    T = p.shape[0]
    tb = _tile(nlat, (256, 128))
    nlb = nlat // tb

    def body(of_ref, ob_ref, gt_ref, w_ref, dy_ref, do_ref, dgt_ref, dw_ref):
        is_lat = pl.program_id(0) < nlb

        @pl.when(pl.program_id(0) == 0)
        def _():
            dw_ref[...] = jnp.zeros_like(dw_ref)
        for h in range(NH):
            sl = slice(h * HD, (h + 1) * HD)
            _, vjp = jax.vjp(_ghn_fn, of_ref[:, sl] + ob_ref[:, sl], gt_ref[:, sl], w_ref[...])
            do, dgt, dw = vjp(dy_ref[:, sl])
            do_ref[:, sl] = do
            dgt_ref[:, sl] = jnp.where(is_lat, dgt, 0.0).astype(BF)
            dw_ref[...] += jnp.where(is_lat, dw, 0.0)

    lat = lambda i: jnp.minimum(i, nlb - 1)
    row = pl.BlockSpec((tb, D), lambda i: (lat(i), 0))
    one = pl.BlockSpec((1, HD), lambda i: (0, 0))
    return pl.pallas_call(
        body, grid=(T // tb,), name="ghn_bwd",
        in_specs=[row, row, pl.BlockSpec((tb, D), lambda i: (lat(i), O_GT // D)), one, row],
        out_specs=[row, pl.BlockSpec((tb, D), lambda i: (i, 0)), one],
        out_shape=[jax.ShapeDtypeStruct((nlat, D), F32), jax.ShapeDtypeStruct((T, D), BF), jax.ShapeDtypeStruct((1, HD), F32)],
    )(o_f, o_b, p, w, dy)


@jax.custom_vjp
def _swap32(x):
    lane = lax.broadcasted_iota(jnp.int32, x.shape, 1)
    return jnp.where((lane & 32) == 0, pltpu.roll(x, 96, 1), pltpu.roll(x, 32, 1))


_swap32.defvjp(lambda x: (_swap32(x), None), lambda _, g: (_swap32(g),))


def _qk_post_fn(xs, w, cos, sin):
    inv = [lax.rsqrt(jnp.mean(x * x, axis=-1, keepdims=True) + EPS) for x in xs]
    ys = [(x * r) * w for x, r in zip(xs, inv)]
    return [y * cos + _swap32(y) * sin for y in ys]


def _attn_prep_fwd(p, qn, kn, cos, sin):
    T = p.shape[0]
    tb = _tile(T, (256, 128))

    def body(q_ref, k_ref, v_ref, qn_ref, kn_ref, cos_ref, sin_ref, qr_ref, kr_ref, vb_ref):
        cos_v, sin_v = cos_ref[...], sin_ref[...]
        for sl, y in zip(_HEAD_SLICES, _qk_post_fn([q_ref[:, sl] for sl in _HEAD_SLICES], qn_ref[...], cos_v, sin_v)):
            qr_ref[:, sl] = y.astype(BF)
        for sl, y in zip(_HEAD_SLICES, _qk_post_fn([k_ref[:, sl] for sl in _HEAD_SLICES[:KVH]], kn_ref[...], cos_v, sin_v)):
            kr_ref[:, sl] = y.astype(BF)
        vb_ref[...] = v_ref[...].astype(BF)

    one = pl.BlockSpec((1, HD), lambda i: (0, 0))
    tab = pl.BlockSpec((tb, HD), lambda i: (i, 0))
    return pl.pallas_call(
        body, grid=(T // tb,), name="attn_prep_fwd",
        in_specs=[pl.BlockSpec((tb, D), lambda i: (i, O_Q // D)), pl.BlockSpec((tb, KV), lambda i: (i, O_K // KV)),
                  pl.BlockSpec((tb, KV), lambda i: (i, O_V // KV)), one, one, tab, tab],
        out_specs=[pl.BlockSpec((tb, D), lambda i: (i, 0)), pl.BlockSpec((tb, KV), lambda i: (i, 0)),
                   pl.BlockSpec((tb, KV), lambda i: (i, 0))],
        out_shape=[jax.ShapeDtypeStruct((T, D), BF), jax.ShapeDtypeStruct((T, KV), BF), jax.ShapeDtypeStruct((T, KV), BF)],
    )(p, p, p, qn, kn, cos, sin)


def _attn_prep_bwd(p, qn, kn, cos, sin, dqr, dkp, dvp, dkc, dvc, nlat):
    T = p.shape[0]
    nqb = nlat // CB
    ncb = (T - nlat) // CB

    def body(q_ref, k_ref, v_ref, qn_ref, kn_ref, cos_ref, sin_ref, dqr_ref, dka_ref, dkb_ref, dkc3_ref, dva_ref, dvb_ref, dvc3_ref,
             dkctx_ref, dvctx_ref, dq_ref, dk_ref, dv_ref, dqn_ref, dkn_ref):
        i = pl.program_id(0)
        is_lat = i < nqb
        cos_v, sin_v = cos_ref[...], sin_ref[...]

        @pl.when(i == 0)
        def _():
            dqn_ref[...] = jnp.zeros_like(dqn_ref)
            dkn_ref[...] = jnp.zeros_like(dkn_ref)

        def band_sum(a_ref, b_ref, c_ref, ctx_ref):
            s = b_ref[0] + jnp.where(i > 0, a_ref[0], 0.0) + jnp.where(i < nqb - 1, c_ref[0], 0.0)
            return jnp.where(is_lat, s, ctx_ref[...])

        dkr = band_sum(dka_ref, dkb_ref, dkc3_ref, dkctx_ref)
        dv_ref[...] = band_sum(dva_ref, dvb_ref, dvc3_ref, dvctx_ref).astype(BF)
        post = lambda xs, w: _qk_post_fn(xs, w, cos_v, sin_v)
        _, vjp = jax.vjp(post, [q_ref[:, sl] for sl in _HEAD_SLICES], qn_ref[...])
        dqs, dqn = vjp([jnp.where(is_lat, dqr_ref[:, sl], 0.0) for sl in _HEAD_SLICES])
        for sl, dq in zip(_HEAD_SLICES, dqs):
            dq_ref[:, sl] = dq.astype(BF)
        dqn_ref[...] += dqn
        _, vjp = jax.vjp(post, [k_ref[:, sl] for sl in _HEAD_SLICES[:KVH]], kn_ref[...])
        dks, dkn = vjp([dkr[:, sl] for sl in _HEAD_SLICES[:KVH]])
        for sl, dk in zip(_HEAD_SLICES, dks):
            dk_ref[:, sl] = dk.astype(BF)
        dkn_ref[...] += dkn

    one = pl.BlockSpec((1, HD), lambda i: (0, 0))
    tab = pl.BlockSpec((CB, HD), lambda i: (i, 0))
    lat = lambda i: jnp.minimum(i, nqb - 1)

    def part(off, slot):
        return pl.BlockSpec((1, CB, KV), lambda i: (jnp.clip(lat(i) + off, 0, nqb - 1) * 3 + slot, 0, 0))

    ctxs = pl.BlockSpec((CB, KV), lambda i: (jnp.clip(i - nqb, 0, ncb - 1), 0))
    kvs = pl.BlockSpec((CB, KV), lambda i: (i, 0))
    return pl.pallas_call(
        body, grid=(T // CB,), name="attn_prep_bwd",
        in_specs=[pl.BlockSpec((CB, D), lambda i: (i, O_Q // D)), pl.BlockSpec((CB, KV), lambda i: (i, O_K // KV)),
                  pl.BlockSpec((CB, KV), lambda i: (i, O_V // KV)), one, one, tab, tab,
                  pl.BlockSpec((CB, D), lambda i: (lat(i), 0)),
                  part(-1, 2), part(0, 1), part(1, 0), part(-1, 2), part(0, 1), part(1, 0), ctxs, ctxs],
        out_specs=[pl.BlockSpec((CB, D), lambda i: (i, 0)), kvs, kvs, one, one],
        out_shape=[jax.ShapeDtypeStruct((T, D), BF), jax.ShapeDtypeStruct((T, KV), BF), jax.ShapeDtypeStruct((T, KV), BF),
                   jax.ShapeDtypeStruct((1, HD), F32), jax.ShapeDtypeStruct((1, HD), F32)],
    )(p, p, p, qn, kn, cos, sin, dqr, dkp, dkp, dkp, dvp, dvp, dvp, dkc, dvc)


def _attn_groups_fn(qs, kalls, valls, sinks, bias):
    groups = range(KVH)
    q = [jnp.concatenate(qs[GRP * g:GRP * (g + 1)], axis=0) for g in groups]
    s = [_dot_nt_bf(q[g], kalls[g]) * (HD ** -0.5) + bias for g in groups]
    sk = [jnp.concatenate([jnp.broadcast_to(jnp.mean(t, axis=1, keepdims=True), (CB, 1)) for t in sinks[GRP * g:GRP * (g + 1)]],
                          axis=0) for g in groups]
    m = [lax.stop_gradient(jnp.maximum(jnp.max(s[g], axis=1, keepdims=True), sk[g])) for g in groups]
    e = [jnp.exp(s[g] - m[g]) for g in groups]
    den = [jnp.sum(e[g], axis=1, keepdims=True) + jnp.exp(sk[g] - m[g]) for g in groups]
    return [_dot_bf(e[g] / den[g], valls[g]) for g in groups]


def _attn_bias(lc):
    r, c = _iota2((GRP * CB, 3 * CB + lc))
    rel = c - (r & (CB - 1))
    win = (rel >= 0) & (rel <= 2 * CB)
    ctx = c >= 3 * CB
    seen = [(win & (c >= CB)) | ctx, win | ctx, (win & (c < 2 * CB)) | ctx]
    return jnp.stack([jnp.where(s, 0.0, -1e30) for s in seen]).astype(F32)


def _attn_specs(nqb, lc, nlat):
    assert nqb >= 2
    qs = pl.BlockSpec((CB, D), lambda i: (i, 0))
    ka = pl.BlockSpec((CB, KV), lambda i: (jnp.maximum(i - 1, 0), 0))
    kb = pl.BlockSpec((CB, KV), lambda i: (i, 0))
    kc = pl.BlockSpec((CB, KV), lambda i: (jnp.minimum(i + 1, nqb - 1), 0))
    kx = pl.BlockSpec((lc, KV), lambda i: (nlat // lc, 0))
    sk = pl.BlockSpec((KVH, 8, 128), lambda i: (0, 0, 0))
    bs = pl.BlockSpec((1, GRP * CB, 3 * CB + lc), lambda i: (jnp.where(i == 0, 0, jnp.where(i == nqb - 1, 2, 1)), 0, 0))
    return qs, ka, kb, kc, kx, sk, bs


def _attn_operands(q_ref, k_refs, v_refs, sk_ref, dtype):
    sls = [slice(g * HD, (g + 1) * HD) for g in range(KVH)]
    kalls = [jnp.concatenate([r[:, sl] for r in k_refs], axis=0).astype(dtype) for sl in sls]
    valls = [jnp.concatenate([r[:, sl] for r in v_refs], axis=0).astype(dtype) for sl in sls]
    qs = [q_ref[:, sl].astype(dtype) for sl in _HEAD_SLICES]
    sinks = [sk_ref[h // GRP, (h % GRP):(h % GRP) + 1, :] for h in range(NH)]
    return qs, kalls, valls, sinks


def _attn_fwd(qr, kr, vb, sink, w_branch, nlat):
    lc = kr.shape[0] - nlat
    nqb = nlat // CB
    qs, ka, kb, kc, kx, sk, bs = _attn_specs(nqb, lc, nlat)

    def body(q_ref, ka_ref, kb_ref, kc_ref, kx_ref, va_ref, vb_ref, vc_ref, vx_ref, sk_ref, bias_ref, wb_ref, o_ref, z_ref):
        operands = _attn_operands(q_ref, (ka_ref, kb_ref, kc_ref, kx_ref), (va_ref, vb_ref, vc_ref, vx_ref), sk_ref, BF)
        outs = _attn_groups_fn(*operands, bias_ref[0])
        for h, sl in enumerate(_HEAD_SLICES):
            o_ref[:, sl] = outs[h // GRP][(h % GRP) * CB:(h % GRP + 1) * CB].astype(BF)
        z_ref[...] = jnp.dot(o_ref[...], wb_ref[...], preferred_element_type=F32)

    return pl.pallas_call(
        body, grid=(nqb,), name="attn_fwd",
        in_specs=[qs, ka, kb, kc, kx, ka, kb, kc, kx, sk, bs, _resident((D, D))], out_specs=[qs, qs],
        out_shape=[jax.ShapeDtypeStruct((nlat, D), BF), jax.ShapeDtypeStruct((nlat, D), F32)], compiler_params=_cp(),
    )(qr, kr, kr, kr, kr, vb, vb, vb, vb, sink, _attn_bias(lc), w_branch)


def _attn_bwd(qr, kr, vb, sink, dy, nlat):
    lc = kr.shape[0] - nlat
    nqb = nlat // CB
    qs, ka, kb, kc, kx, sk, bs = _attn_specs(nqb, lc, nlat)

    def body(q_ref, ka_ref, kb_ref, kc_ref, kx_ref, va_ref, vb_ref, vc_ref, vx_ref, sk_ref, dy_ref, bias_ref,
             dq_ref, dkp_ref, dvp_ref, dkx_ref, dvx_ref, dsk_ref):
        operands = _attn_operands(q_ref, (ka_ref, kb_ref, kc_ref, kx_ref), (va_ref, vb_ref, vc_ref, vx_ref), sk_ref, F32)
        _, vjp = jax.vjp(functools.partial(_attn_groups_fn, bias=bias_ref[0]), *operands)
        dys_g = [jnp.concatenate([dy_ref[:, sl] for sl in _HEAD_SLICES[GRP * g:GRP * (g + 1)]], axis=0) for g in range(KVH)]
        dqs, dks, dvs, dsinks = vjp(dys_g)

        @pl.when(pl.program_id(0) == 0)
        def _():
            dkx_ref[...] = jnp.zeros_like(dkx_ref)
            dvx_ref[...] = jnp.zeros_like(dvx_ref)
            dsk_ref[...] = jnp.zeros_like(dsk_ref)

        for h, sl in enumerate(_HEAD_SLICES):
            dq_ref[:, sl] = dqs[h]
            dsk_ref[h // GRP, (h % GRP):(h % GRP) + 1, :] += dsinks[h]
        for g in range(KVH):
            sl = slice(g * HD, (g + 1) * HD)
            for t in range(3):
                dkp_ref[t, :, sl] = dks[g][t * CB:(t + 1) * CB]
                dvp_ref[t, :, sl] = dvs[g][t * CB:(t + 1) * CB]
            dkx_ref[:, sl] += dks[g][3 * CB:]
            dvx_ref[:, sl] += dvs[g][3 * CB:]

    dys = qs
    parts = pl.BlockSpec((3, CB, KV), lambda i: (i, 0, 0))
    ctxo = pl.BlockSpec((lc, KV), lambda i: (0, 0))
    return pl.pallas_call(
        body, grid=(nqb,), name="attn_bwd",
        in_specs=[qs, ka, kb, kc, kx, ka, kb, kc, kx, sk, dys, bs],
        out_specs=[dys, parts, parts, ctxo, ctxo, sk],
        out_shape=[jax.ShapeDtypeStruct((nlat, D), F32), jax.ShapeDtypeStruct((3 * nqb, CB, KV), F32),
                   jax.ShapeDtypeStruct((3 * nqb, CB, KV), F32), jax.ShapeDtypeStruct((lc, KV), F32),
                   jax.ShapeDtypeStruct((lc, KV), F32), jax.ShapeDtypeStruct((KVH, 8, 128), F32)],
        compiler_params=_cp(),
    )(qr, kr, kr, kr, kr, vb, vb, vb, vb, sink, dy, _attn_bias(lc))


def _merge_fn(z_dn, z_at, g_dn, g_at):
    return jax.nn.sigmoid(g_dn) * z_dn + jax.nn.sigmoid(g_at) * z_at


def _merge_fwd(z_dn, z_at, p, w_out, nlat):
    tb = _tile(nlat, (256, 128))

    def body(zd_ref, za_ref, gd_ref, ga_ref, wo_ref, o_ref, mix_ref):
        o_ref[...] = _merge_fn(zd_ref[...], za_ref[...], gd_ref[...], ga_ref[...]).astype(BF)
        mix_ref[...] = jnp.dot(o_ref[...], wo_ref[...], preferred_element_type=F32)

    row = pl.BlockSpec((tb, D), lambda i: (i, 0))
    return pl.pallas_call(
        body, grid=(nlat // tb,), name="merge_fwd",
        in_specs=[row, row, pl.BlockSpec((tb, D), lambda i: (i, O_MG // D)), pl.BlockSpec((tb, D), lambda i: (i, O_MG // D + 1)),
                  _resident((D, D))],
        out_specs=[row, row], out_shape=[jax.ShapeDtypeStruct((nlat, D), BF), jax.ShapeDtypeStruct((nlat, D), F32)],
    )(z_dn, z_at, p, p, w_out)


def _merge_bwd(z_dn, z_at, p, dm, w_bdn, w_bat, nlat):
    T = p.shape[0]
    tb = _tile(nlat, (256, 128))
    nlb = nlat // tb

    def body(zd_ref, za_ref, gd_ref, ga_ref, dm_ref, wd_ref, wa_ref, dzd_ref, dza_ref, dg_ref, dyd_ref, dya_ref):
        is_lat = pl.program_id(0) < nlb
        _, vjp = jax.vjp(_merge_fn, zd_ref[...], za_ref[...], gd_ref[...], ga_ref[...])
        dzd, dza, dgd, dga = vjp(dm_ref[...])
        dzd_ref[...] = dzd.astype(BF)
        dza_ref[...] = dza.astype(BF)
        dg_ref[:, :D] = jnp.where(is_lat, dgd, 0.0).astype(BF)
        dg_ref[:, D:] = jnp.where(is_lat, dga, 0.0).astype(BF)
        dyd_ref[...] = lax.dot_general(dzd_ref[...], wd_ref[...], (_DIMS["nt"], ((), ())), preferred_element_type=F32)
        dya_ref[...] = lax.dot_general(dza_ref[...], wa_ref[...], (_DIMS["nt"], ((), ())), preferred_element_type=F32)

    lat = lambda i: jnp.minimum(i, nlb - 1)
    row = pl.BlockSpec((tb, D), lambda i: (lat(i), 0))
    return pl.pallas_call(
        body, grid=(T // tb,), name="merge_bwd",
        in_specs=[row, row, pl.BlockSpec((tb, D), lambda i: (lat(i), O_MG // D)),
                  pl.BlockSpec((tb, D), lambda i: (lat(i), O_MG // D + 1)), row, _resident((D, D)), _resident((D, D))],
        out_specs=[row, row, pl.BlockSpec((tb, 2 * D), lambda i: (i, 0)), row, row],
        out_shape=[jax.ShapeDtypeStruct((nlat, D), BF), jax.ShapeDtypeStruct((nlat, D), BF), jax.ShapeDtypeStruct((T, 2 * D), BF),
                   jax.ShapeDtypeStruct((nlat, D), F32), jax.ShapeDtypeStruct((nlat, D), F32)],
    )(z_dn, z_at, p, p, dm, w_bdn, w_bat)


def _swiglu_fn(ug, uv):
    return jax.nn.silu(ug) * uv


FFN_GROUP = 256


def _resident(shape):
    return pl.BlockSpec(shape, lambda i: (0,) * len(shape), pipeline_mode=pl.Buffered(1))


H_HALO = 16


def _up_project(h_refs, wu_ref, u_scr):
    cur_ref, prev_ref, next_ref = h_refs
    rows = jnp.concatenate([prev_ref[...], cur_ref[...], next_ref[...]], axis=0)
    u_scr[...] = jnp.dot(rows, wu_ref[...], preferred_element_type=F32)


def _up_ext_rows(u_scr, cols, keep, tb):
    xe = u_scr[H_HALO - HALO:H_HALO + tb + HALO, cols]
    r = lax.broadcasted_iota(jnp.int32, (tb + 2 * HALO, 1), 0)
    inside = ((r >= HALO) | keep[0]) & ((r < HALO + tb) | keep[1])
    return jnp.where(inside, xe, 0.0)


def _ffn_fwd(h, w_up, w8, bias, w_down):
    n = h.shape[0]
    tb = _tile(n, (256, 128))
    starts, ends = _segment_edges((n,), tb)

    def body(cur_ref, prev_ref, next_ref, wu_ref, w_ref, b_ref, wd_ref, u_ref, o_ref, ff_ref, u_scr):
        keep = _keep_halos(pl.program_id(0), starts, ends)
        _up_project((cur_ref, prev_ref, next_ref), wu_ref, u_scr)
        u_ref[...] = u_scr[H_HALO:H_HALO + tb, :]

        for c0 in range(0, DFF, FFN_GROUP):
            halves = []
            for cols in (slice(c0, c0 + FFN_GROUP), slice(DFF + c0, DFF + c0 + FFN_GROUP)):
                xe = _up_ext_rows(u_scr, cols, keep, tb)
                halves.append(_conv_rows(_shifted_rows(xe, FFN_TAPS), w_ref, cols)[HALO:HALO + tb] + b_ref[:, cols])
            o_ref[:, c0:c0 + FFN_GROUP] = _swiglu_fn(*halves).astype(BF)
        ff_ref[...] = jnp.dot(o_ref[...], wd_ref[...], preferred_element_type=F32)

    return pl.pallas_call(
        body, grid=(n // tb,), name="ffn_fwd",
        in_specs=_halo_specs(tb, D, n, halo=H_HALO) + [_resident((D, 2 * DFF)), pl.BlockSpec((8, 2 * DFF), lambda i: (0, 0)),
                                                        pl.BlockSpec((1, 2 * DFF), lambda i: (0, 0)), _resident((DFF, D))],
        out_specs=[pl.BlockSpec((tb, 2 * DFF), lambda i: (i, 0)), pl.BlockSpec((tb, DFF), lambda i: (i, 0)),
                   pl.BlockSpec((tb, D), lambda i: (i, 0))],
        out_shape=[jax.ShapeDtypeStruct((n, 2 * DFF), F32), jax.ShapeDtypeStruct((n, DFF), BF), jax.ShapeDtypeStruct((n, D), F32)],
        scratch_shapes=[pltpu.VMEM((tb + 2 * H_HALO, 2 * DFF), F32)],
        compiler_params=_cp(),
    )(h, h, h, w_up, w8, bias, w_down)


def _ffn_bwd(u, w_up, w8, bias, da):
    n = u.shape[0]
    tb = _tile(n, (256, 128))
    starts, ends = _segment_edges((n,), tb)

    def body(cur_ref, prev_ref, next_ref, wu_ref, w_ref, b_ref, da_c, da_p, da_n, du_ref, dw_ref, db_ref, dh_ref):
        i = pl.program_id(0)
        keep = _keep_halos(i, starts, ends)

        @pl.when(i == 0)
        def _():
            dw_ref[...] = jnp.zeros_like(dw_ref)
            db_ref[...] = jnp.zeros_like(db_ref)

        for c0 in range(0, DFF, FFN_GROUP):
            col_pair = (slice(c0, c0 + FFN_GROUP), slice(DFF + c0, DFF + c0 + FFN_GROUP))
            shifts = [_shifted_rows(_ext_rows((cur_ref, prev_ref, next_ref), cols, keep), FFN_TAPS) for cols in col_pair]
            convs = [_conv_rows(shifted, w_ref, cols) + b_ref[:, cols] for shifted, cols in zip(shifts, col_pair)]
            dae = _ext_rows((da_c, da_p, da_n), col_pair[0], keep)
            _, vjp = jax.vjp(_swiglu_fn, *convs)
            for shifted, cols, dce in zip(shifts, col_pair, vjp(dae)):
                du_ref[:, cols] = _conv_rows(_shifted_rows(dce, FFN_TAPS, transpose=True), w_ref, cols)[HALO:HALO + tb].astype(BF)
                dcur = dce[HALO:HALO + tb]
                for j, g in enumerate(_tap_grads(dcur, shifted, tb)):
                    dw_ref[j:j + 1, cols] += g
                db_ref[:, cols] += jnp.sum(dcur, axis=0, keepdims=True)
        dh_ref[...] = lax.dot_general(du_ref[...], wu_ref[...], (_DIMS["nt"], ((), ())), preferred_element_type=F32)

    wspec = pl.BlockSpec((8, 2 * DFF), lambda i: (0, 0))
    bspec = pl.BlockSpec((1, 2 * DFF), lambda i: (0, 0))
    return pl.pallas_call(
        body, grid=(n // tb,), name="ffn_bwd",
        in_specs=_halo_specs(tb, 2 * DFF, n) + [_resident((D, 2 * DFF)), wspec, bspec] + _halo_specs(tb, DFF, n),
        out_specs=[pl.BlockSpec((tb, 2 * DFF), lambda i: (i, 0)), wspec, bspec, pl.BlockSpec((tb, D), lambda i: (i, 0))],
        out_shape=[jax.ShapeDtypeStruct((n, 2 * DFF), BF), jax.ShapeDtypeStruct((8, 2 * DFF), F32), jax.ShapeDtypeStruct((1, 2 * DFF), F32),
                   jax.ShapeDtypeStruct((n, D), F32)],
        compiler_params=_cp(),
    )(u, u, u, w_up, w8, bias, da, da, da)


def _loss_kernel(x1, gate, ff, target, w_down):
    n = x1.shape[0]
    tb = _tile(n, (256, 128))

    def body(x_ref, g_ref, f_ref, t_ref, wd_ref, loss_ref, dy_ref, dff_ref, dg_ref, da_ref):
        err = x_ref[...] + g_ref[...] * f_ref[...] - t_ref[...]
        dy = err * (1.0 / D)
        dy_ref[...] = dy
        dff_ref[...] = (g_ref[...] * dy).astype(BF)
        da_ref[...] = lax.dot_general(dff_ref[...], wd_ref[...], (_DIMS["nt"], ((), ())), preferred_element_type=F32)

        @pl.when(pl.program_id(0) == 0)
        def _():
            loss_ref[...] = jnp.zeros_like(loss_ref)
            dg_ref[...] = jnp.zeros_like(dg_ref)
        part = 0.5 * jnp.sum(jnp.sum(err * err, axis=1, keepdims=True) * (1.0 / D), axis=0, keepdims=True)
        loss_ref[...] += jnp.broadcast_to(part, (1, 128))
        dg_ref[...] += jnp.sum(dy * f_ref[...], axis=0, keepdims=True)

    row = pl.BlockSpec((tb, D), lambda i: (i, 0))
    one = pl.BlockSpec((1, D), lambda i: (0, 0))
    return pl.pallas_call(
        body, grid=(n // tb,), name="loss",
        in_specs=[row, one, row, row, _resident((DFF, D))],
        out_specs=[pl.BlockSpec((1, 128), lambda i: (0, 0)), row, row, one, pl.BlockSpec((tb, DFF), lambda i: (i, 0))],
        out_shape=[jax.ShapeDtypeStruct((1, 128), F32), jax.ShapeDtypeStruct((n, D), F32),
                   jax.ShapeDtypeStruct((n, D), BF), jax.ShapeDtypeStruct((1, D), F32), jax.ShapeDtypeStruct((n, DFF), F32)],
        compiler_params=_cp(),
    )(x1, gate, ff, target, w_down)


def _rope_tables(nlat, lc):
    t = jnp.arange(nlat)
    row = (t // GRID_W).astype(F32)
    col = (t % GRID_W).astype(F32)
    inv_freq = ROPE_BASE ** (-jnp.arange(32, dtype=F32) / 32)
    ar, ac = row[:, None] * inv_freq, col[:, None] * inv_freq
    cos = jnp.concatenate([jnp.cos(ar), jnp.cos(ar), jnp.cos(ac), jnp.cos(ac)], axis=1)
    sin = jnp.concatenate([-jnp.sin(ar), jnp.sin(ar), -jnp.sin(ac), jnp.sin(ac)], axis=1)
    cos = jnp.concatenate([cos, jnp.ones((lc, HD), F32)], axis=0)
    sin = jnp.concatenate([sin, jnp.zeros((lc, HD), F32)], axis=0)
    return cos, sin


def _pad_rows8(w):
    return jnp.concatenate([w, jnp.zeros((8 - w.shape[0], w.shape[1]), w.dtype)], axis=0)


def _pack_w_in(w):
    cuts = [sum(IN_SIZES[:i]) for i in range(len(IN_SIZES) + 1)]
    qkv, gt, b, a, q, k, v, mg = [w[:, cuts[i]:cuts[i + 1]] for i in range(len(IN_SIZES))]
    return jnp.concatenate([qkv, gt, q, mg, k, v, b, a, jnp.zeros((w.shape[0], PW - O_BA - 32), w.dtype)], axis=1)


def _unpack_w_in(g):
    return jnp.concatenate([g[:, O_QKV:O_GT], g[:, O_GT:O_Q], g[:, O_BA:O_BA + 32], g[:, O_Q:O_MG], g[:, O_K:O_V],
                            g[:, O_V:O_BA], g[:, O_MG:O_K]], axis=1)


def _local_step(x, ctx, mod_x, mod_c, target, project_in, project_back,
                norm_mix, norm_ffn, dn_conv, a_log, dt_bias, dn_norm, q_norm, k_norm, sink, ffn_conv, ffn_conv_b):
    L, LC = x.shape[0], ctx.shape[0]
    T = L + LC
    seg = lambda r: jnp.stack([mod_x[r], mod_c[r]])[:, None, :]
    sh_a, sc_a = seg(0), seg(1)
    g_a, g_f = mod_x[2][None], mod_x[5][None]
    sh_f, sc_f = mod_x[3][None], mod_x[4][None]
    cos, sin = _rope_tables(L, LC)
    dnc8 = _pad_rows8(dn_conv)
    ffc8 = _pad_rows8(ffn_conv)
    gate_row = lambda a: jnp.concatenate([jnp.zeros((1, 16), F32), a.reshape(1, 16), jnp.zeros((1, 96), F32)], axis=1)
    alog_row, dt_row = gate_row(a_log), gate_row(dt_bias)
    sinkb = jnp.concatenate([jnp.broadcast_to(sink.reshape(KVH, GRP, 1), (KVH, GRP, 128)), jnp.zeros((KVH, 8 - GRP, 128), F32)], axis=1)

    h1 = _norm_mod_fwd(x, ctx, norm_mix, sh_a, sc_a, "norm_mix_fwd")
    p, (w_in_p, w_bdn, w_bat, w_out, w_up, w_down) = project_in(h1)
    q, k, v, gb = _dn_pre_fwd(p, dnc8, alog_row, dt_row, (L, LC))
    wy = _dn1_fwd(q, k, v, gb)
    scans = _dn2_fwd([t[:6] for t in wy], L)
    o_dir = [s[0] for s in scans]
    y_dn, z_dn = _ghn_fwd(o_dir[0], o_dir[1], p, dn_norm, w_bdn, L)
    qr, kr, vb = _attn_prep_fwd(p, q_norm, k_norm, cos, sin)
    y_at, z_at = _attn_fwd(qr, kr, vb, sinkb, w_bat, L)
    merged, mix = _merge_fwd(z_dn, z_at, p, w_out, L)
    x1, h2 = _resid_norm_fwd(x, g_a, mix, norm_ffn, sh_f, sc_f)
    u_raw, act, ff = _ffn_fwd(h2, w_up, ffc8, ffn_conv_b, w_down)
    loss_row, dy, dff, dg_f, dact = _loss_kernel(x1, g_f, ff, target, w_down)

    g_down = _mm(act, dff, form="tn", out_dtype=BF, name="g_ffn_down")
    du_raw, g_ffc8, g_ffb, dh2 = _ffn_bwd(u_raw, w_up, ffc8, ffn_conv_b, dact)
    g_up = _mm(h2, du_raw, form="tn", out_dtype=BF, name="g_ffn_up")
    dx1, dmix, dg_a, g_nffn, dsh_f, dsc_f, dmerged = _resid_norm_bwd(x1, g_a, mix, norm_ffn, sh_f, sc_f, dh2, dy, w_out)

    g_out = _mm(merged, dmix, form="tn", out_dtype=BF, name="g_w_out")
    dz_dn, dz_at, dmg, dy_dn, dy_at = _merge_bwd(z_dn, z_at, p, dmerged, w_bdn, w_bat, L)
    g_bdn = _mm(y_dn, dz_dn, form="tn", out_dtype=BF, name="g_branch_dn")
    g_bat = _mm(y_at, dz_at, form="tn", out_dtype=BF, name="g_branch_at")
    dqr, dkp, dvp, dkx, dvx, dsink = _attn_bwd(qr, kr, vb, sinkb, dy_at, L)
    dq_raw, dk_raw, dv_raw, g_qn, g_kn = _attn_prep_bwd(p, q_norm, k_norm, cos, sin, dqr, dkp, dvp, dkx, dvx, L)
    do, dgt, g_dnn = _ghn_bwd(o_dir[0], o_dir[1], p, dn_norm, dy_dn, L)
    cots = _dn2_bwd([wy[d][:6] + (scans[d][1],) for d in (0, 1)], do, L)
    dq, dk, dv, dgb = _dn1_bwd(q, k, v, gb, [t[6] for t in wy], cots)
    dp, g_dnc8, g_alog, g_dt = _dn_pre_bwd(p, dnc8, alog_row, dt_row, dq, dk, dv, dgb, (dgt, dq_raw, dmg, dk_raw, dv_raw), (L, LC))
    big, dh1 = project_back(h1, dp, w_in_p, (g_bdn, g_bat, g_out, g_up, g_down))
    grad_x, g_nmix_x, dsh_a, dsc_a = _norm_mod_bwd(x, norm_mix, mod_x[0][None], mod_x[1][None], dh1, row0=0,
                                                   name="norm_mix_bwd", residual=dx1)
    g_nmix_c, dsh_c, dsc_c = _norm_mod_bwd(ctx, norm_mix, mod_c[0][None], mod_c[1][None], dh1, row0=L, name="norm_mix_bwd_ctx")
    g_nmix = g_nmix_x + g_nmix_c

    zero = jnp.zeros((D,), F32)
    dmod_x = jnp.stack([dsh_a[0], dsc_a[0], dg_a[0], dsh_f[0], dsc_f[0], dg_f[0]])
    dmod_c = jnp.stack([dsh_c[0], dsc_c[0], zero, zero, zero, zero])
    small = dict(
        dmod_x=dmod_x, dmod_c=dmod_c, norm_mix=g_nmix, norm_ffn=g_nffn, dn_conv=g_dnc8[:5], dn_a_log=g_alog[0, 16:32].reshape(2, 8),
        dn_dt_bias=g_dt[0, 16:32].reshape(2, 8), dn_norm=g_dnn, q_norm=g_qn, k_norm=g_kn,
        attn_sink=jnp.sum(dsink[:, :GRP, :], axis=2).reshape(1, NH), ffn_conv=g_ffc8[:3], ffn_conv_b=g_ffb)
    return loss_row[0, 0], grad_x, big, small


def _exchange(arrays, scatter, name):
    n = len(arrays)

    def body(*refs):
        args = (refs[:n], refs[n:2 * n], *refs[2 * n:], scatter)
        _exchange_start(*args)
        _exchange_wait(*args)

    hbm = pl.BlockSpec(memory_space=pl.ANY)
    out_shape, sems = _exchange_shapes(arrays, scatter)
    return pl.pallas_call(body, name=name, in_specs=[hbm] * n, out_specs=[hbm] * n, out_shape=out_shape,
                          scratch_shapes=sems)(*arrays)


def _gather_two_level(arrays, name):
    n = len(arrays)

    def body(*refs):
        ins, outs = refs[:n], refs[n:2 * n]
        send_sems, recv_sems, local_sems = refs[2 * n:]
        x, y, c = lax.axis_index("x"), lax.axis_index("y"), lax.axis_index("c")
        sibling = (x, y, 1 - c)
        chips = [(1 - x, y), (x, 1 - y), (1 - x, 1 - y)]

        def copy(k, j, block, to, src=None):
            slot = outs[k].at[4 * block[0] + 2 * block[1] + block[2]]
            return pltpu.make_async_remote_copy(src_ref=slot if src is None else src, dst_ref=slot,
                                                send_sem=send_sems.at[7 * k + j], recv_sem=recv_sems.at[7 * k + j],
                                                device_id=to, device_id_type=MESH)

        mine = [pltpu.make_async_copy(ins[k], outs[k].at[4 * x + 2 * y + c], local_sems.at[k]) for k in range(n)]
        for cp in mine:
            cp.start()
        first = []
        for k in range(n):
            first.append(copy(k, 0, (x, y, c), sibling, src=ins[k]))
            first += [copy(k, 1 + j, (x, y, c), (*chip, c), src=ins[k]) for j, chip in enumerate(chips)]
        for cp in first:
            cp.start()
        passed = []
        for k in range(n):
            for j, chip in enumerate(chips):
                copy(k, 1 + j, (*chip, c), (x, y, c)).wait_recv()
                forward = copy(k, 4 + j, (*chip, c), sibling)
                forward.start()
                passed.append(forward)
        for k in range(n):
            copy(k, 0, sibling, (x, y, c)).wait_recv()
            for j, chip in enumerate(chips):
                copy(k, 4 + j, (*chip, 1 - c), (x, y, c)).wait_recv()
        for cp in first + passed:
            cp.wait_send()
        for cp in mine:
            cp.wait()

    hbm = pl.BlockSpec(memory_space=pl.ANY)
    out_shape, sems = _exchange_shapes(arrays, False)
    return pl.pallas_call(body, name=name, in_specs=[hbm] * n, out_specs=[hbm] * n, out_shape=out_shape,
                          scratch_shapes=sems)(*arrays)


def _ada_fwd(c16, w_ada, b_ada):
    def body(c_ref, w_ref, b_ref, o_ref):
        o_ref[...] = _dot_hi(jax.nn.silu(c_ref[...]), w_ref[...]) + b_ref[...]

    return pl.pallas_call(body, name="ada_fwd", out_shape=jax.ShapeDtypeStruct((16, w_ada.shape[1]), F32))(c16, w_ada, b_ada)


def _ada_bwd(c16, w_ada, dmx, dmc):
    def body(c_ref, w_ref, dmx_ref, dmc_ref, gw_ref, pc_ref):
        dmc_tot = dmc_ref[0:1, :]
        for d in range(1, N_DEV):
            dmc_tot = dmc_tot + dmc_ref[d:d + 1, :]
        dm16 = jnp.concatenate([dmx_ref[...], jnp.broadcast_to(dmc_tot, (8, dmc_tot.shape[1]))], axis=0)
        row = lax.broadcasted_iota(jnp.int32, dm16.shape, 0)
        dm16 = jnp.where(row <= 8, dm16, 0.0)
        s = jax.nn.silu(c_ref[...])
        gw_ref[...] = lax.dot_general(s, dm16, (_DIMS["tn"], ((), ())), precision=HI, preferred_element_type=F32)
        pc = lax.dot_general(dm16, w_ref[...], (_DIMS["nt"], ((), ())), precision=HI, preferred_element_type=F32)
        pc_ref[...] = pc[8:9, :]

    return pl.pallas_call(body, name="ada_bwd", out_shape=[jax.ShapeDtypeStruct(w_ada.shape, F32), jax.ShapeDtypeStruct((1, D), F32)],
                          compiler_params=_cp())(c16, w_ada, dmx, dmc)


def _cctx_grad(pc_all, c_ctx_row):
    def body(pc_ref, c_ref, g_ref):
        tot = pc_ref[0]
        for d in range(1, N_DEV):
            tot = tot + pc_ref[d]
        _, vjp = jax.vjp(jax.nn.silu, c_ref[...])
        g_ref[...] = vjp(tot)[0]

    return pl.pallas_call(body, name="cctx_grad", out_shape=jax.ShapeDtypeStruct((1, D), F32))(pc_all, c_ctx_row)


def _adamw(parts, w, m, v, name):
    ns, R, C = parts.shape
    tb = _tile(R, (128, 64, 32, 16, 8))

    def body(p_ref, w_ref, m_ref, v_ref, g_ref, d_ref, mo_ref, vo_ref):
        g = p_ref[0].astype(F32)
        for s in range(1, ns):
            g = g + p_ref[s].astype(F32)
        m2 = ADAM_B1 * m_ref[...] + (1.0 - ADAM_B1) * g
        v2 = ADAM_B2 * v_ref[...] + (1.0 - ADAM_B2) * jnp.square(g)
        m_hat = m2 / (1.0 - ADAM_B1 ** ADAM_STEP)
        v_hat = v2 / (1.0 - ADAM_B2 ** ADAM_STEP)
        g_ref[...] = g
        d_ref[...] = -ADAM_LR * (m_hat / (jnp.sqrt(v_hat) + ADAM_EPS) + ADAM_WD * w_ref[...])
        mo_ref[...] = m2
        vo_ref[...] = v2

    row = pl.BlockSpec((tb, C), lambda i: (i, 0))
    return pl.pallas_call(
        body, grid=(R // tb,), name=name,
        in_specs=[pl.BlockSpec((ns, tb, C), lambda i: (0, i, 0)), row, row, row], out_specs=[row] * 4,
        out_shape=[jax.ShapeDtypeStruct((R, C), F32)] * 4, compiler_params=_cp(),
    )(parts, w, m, v)


_SMALL = (("dmod_x", 6 * D), ("dmod_c", 6 * D), ("b_ada", 6 * D), ("norm_mix", D), ("norm_ffn", D), ("dn_a_log", 16),
          ("dn_dt_bias", 16), ("dn_norm", HD), ("q_norm", HD), ("k_norm", HD), ("attn_sink", NH), ("ffn_conv_b", 2 * DFF),
          ("dn_conv", 5 * 3 * D), ("ffn_conv", 3 * 2 * DFF))
_SMALL_ROWS = -(-sum(n for _, n in _SMALL) // 1024) * 8


def _pack_small(d):
    flat = jnp.concatenate([d[k].reshape(-1).astype(F32) if k in d else jnp.zeros((n,), F32) for k, n in _SMALL])
    return jnp.concatenate([flat, jnp.zeros((_SMALL_ROWS * 128 - flat.shape[0],), F32)]).reshape(_SMALL_ROWS, 128)


def _unpack_small(a):
    flat = a.reshape(a.shape[:-2] + (-1,))
    out, off = {}, 0
    for k, n in _SMALL:
        out[k] = flat[..., off:off + n]
        off += n
    return out


def kernel(x, c, ctx, c_ctx, w_ada, b_ada, norm_mix, norm_ffn, w_in, dn_conv, dn_a_log, dn_dt_bias, dn_norm, q_norm, k_norm, attn_sink, w_branch_dn, w_branch_attn, w_out, ffn_up, ffn_conv, ffn_conv_b, ffn_down, loss_target, m_c_ctx, m_w_ada, m_b_ada, m_norm_mix, m_norm_ffn, m_w_in, m_dn_conv, m_dn_a_log, m_dn_dt_bias, m_dn_norm, m_q_norm, m_k_norm, m_attn_sink, m_w_branch_dn, m_w_branch_attn, m_w_out, m_ffn_up, m_ffn_conv, m_ffn_conv_b, m_ffn_down, v_c_ctx, v_w_ada, v_b_ada, v_norm_mix, v_norm_ffn, v_w_in, v_dn_conv, v_dn_a_log, v_dn_dt_bias, v_dn_norm, v_q_norm, v_k_norm, v_attn_sink, v_w_branch_dn, v_w_branch_attn, v_w_out, v_ffn_up, v_ffn_conv, v_ffn_conv_b, v_ffn_down):
    me = 4 * lax.axis_index("x") + 2 * lax.axis_index("y") + lax.axis_index("c")
    ada_cols = w_ada.shape[2]

    cols = lambda a: jnp.swapaxes(a, 0, 1).reshape(a.shape[1], -1)
    rows = lambda a: a.reshape(-1, a.shape[2])
    col_blocks = lambda g: jnp.swapaxes(g.reshape(g.shape[0], N_DEV, -1), 0, 1)
    row_blocks = lambda g: g.reshape(N_DEV, -1, g.shape[1])

    gathered = _gather_two_level([w_in[0].astype(BF), c, dn_conv[0], ffn_conv[0]], name="gather_first")
    w_in_packed = _pack_w_in(cols(gathered[0]))
    c_all = gathered[1][:, 0, :]

    def project_in(h1):
        p, rest = _mm(h1, w_in_packed, form="nn", out_dtype=F32, name="in_proj",
                      exchange=([w_branch_dn[0].astype(BF), w_branch_attn[0].astype(BF), w_out[0].astype(BF),
                                 ffn_up[0].astype(BF), ffn_down[0].astype(BF)], False))
        return p, (w_in_packed, rows(rest[0]), rows(rest[1]), rows(rest[2]), cols(rest[3]), rows(rest[4]))

    def project_back(h1, dp, w_in_p, grads):
        g_bdn, g_bat, g_out, g_up, g_down = grads
        g_in, landed_rest = _mm(h1, dp, form="tn", out_dtype=BF, name="g_w_in",
                                exchange=([row_blocks(g_bdn), row_blocks(g_bat), row_blocks(g_out), col_blocks(g_up),
                                           row_blocks(g_down)], True))
        dh1, landed_in = _mm(dp, w_in_p, form="nt", out_dtype=F32, name="d_h1",
                             exchange=([col_blocks(_unpack_w_in(g_in))], True))
        return [landed_in[0]] + landed_rest, dh1

    c16 = jnp.concatenate([c_all, c_ctx[None], jnp.zeros((7, D), F32)], axis=0)
    b_loc = lax.dynamic_slice_in_dim(b_ada, me * ada_cols, ada_cols, axis=1)
    mod_part = _ada_fwd(c16, w_ada[0], b_loc)
    mod_all = cols(_exchange([mod_part], scatter=False, name="gather_mod")[0])
    mod_x = lax.dynamic_slice_in_dim(mod_all, me, 1, axis=0).reshape(6, D)
    mod_c = mod_all[8].reshape(6, D)

    loss_loc, grad_x, landed, small = _local_step(
        x[0], ctx[0], mod_x, mod_c, loss_target[0], project_in, project_back,
        norm_mix, norm_ffn, cols(gathered[2]), dn_a_log[0], dn_dt_bias[0], dn_norm, q_norm, k_norm, attn_sink[0], cols(gathered[3]),
        ffn_conv_b)
    loss = lax.psum(loss_loc, ("x", "y", "c"))

    res = {}
    res["w_in"] = _adamw(landed[0], w_in[0], m_w_in[0], v_w_in[0], "adamw_w_in")
    res["w_branch_dn"] = _adamw(landed[1], w_branch_dn[0], m_w_branch_dn[0], v_w_branch_dn[0], "adamw_w_branch_dn")
    res["w_branch_attn"] = _adamw(landed[2], w_branch_attn[0], m_w_branch_attn[0], v_w_branch_attn[0], "adamw_w_branch_attn")
    res["w_out"] = _adamw(landed[3], w_out[0], m_w_out[0], v_w_out[0], "adamw_w_out")
    res["ffn_up"] = _adamw(landed[4], ffn_up[0], m_ffn_up[0], v_ffn_up[0], "adamw_ffn_up")
    res["ffn_down"] = _adamw(landed[5], ffn_down[0], m_ffn_down[0], v_ffn_down[0], "adamw_ffn_down")

    small = dict(small)
    small["b_ada"] = small["dmod_x"] + small["dmod_c"]
    parts = _exchange([_pack_small(small)], scatter=False, name="gather_small")[0]
    per_dev = _unpack_small(parts)
    given = dict(b_ada=(b_ada, m_b_ada, v_b_ada), norm_mix=(norm_mix, m_norm_mix, v_norm_mix), norm_ffn=(norm_ffn, m_norm_ffn, v_norm_ffn),
                 dn_a_log=(dn_a_log, m_dn_a_log, v_dn_a_log), dn_dt_bias=(dn_dt_bias, m_dn_dt_bias, v_dn_dt_bias),
                 dn_norm=(dn_norm, m_dn_norm, v_dn_norm), q_norm=(q_norm, m_q_norm, v_q_norm), k_norm=(k_norm, m_k_norm, v_k_norm),
                 attn_sink=(attn_sink, m_attn_sink, v_attn_sink), ffn_conv_b=(ffn_conv_b, m_ffn_conv_b, v_ffn_conv_b))
    packs = [_pack_small({k: t[j] for k, t in given.items()}) for j in range(3)]
    upd = [_unpack_small(a) for a in _adamw(parts, packs[0], packs[1], packs[2], "adamw_small")]
    for k, t in given.items():
        res[k] = tuple(u[k].reshape(t[0].shape) for u in upd)
    dnc = lax.dynamic_slice_in_dim(upd[0]["dn_conv"].reshape(5, 3 * D), me * dn_conv.shape[2], dn_conv.shape[2], axis=1)
    ffc = lax.dynamic_slice_in_dim(upd[0]["ffn_conv"].reshape(3, 2 * DFF), me * ffn_conv.shape[2], ffn_conv.shape[2], axis=1)
    r8 = lambda a: _pad_rows8(a)
    t = _adamw(r8(dnc)[None], r8(dn_conv[0]), r8(m_dn_conv[0]), r8(v_dn_conv[0]), "adamw_dn_conv")
    res["dn_conv"] = tuple(a[:5][None] for a in t)
    t = _adamw(r8(ffc)[None], r8(ffn_conv[0]), r8(m_ffn_conv[0]), r8(v_ffn_conv[0]), "adamw_ffn_conv")
    res["ffn_conv"] = tuple(a[:3][None] for a in t)

    dmx = lax.dynamic_slice_in_dim(per_dev["dmod_x"], me * ada_cols, ada_cols, axis=1)
    dmc = lax.dynamic_slice_in_dim(per_dev["dmod_c"], me * ada_cols, ada_cols, axis=1)
    g_ada, pc = _ada_bwd(c16, w_ada[0], dmx, dmc)
    res["w_ada"] = _adamw(g_ada[None], w_ada[0], m_w_ada[0], v_w_ada[0], "adamw_w_ada")
    pc_all = _exchange([pc], scatter=False, name="gather_cctx")[0]
    g_cctx = _cctx_grad(pc_all, c_ctx[None])
    r8b = lambda a: jnp.broadcast_to(a, (8, D))
    t = _adamw(r8b(g_cctx)[None], r8b(c_ctx[None]), r8b(m_c_ctx[None]), r8b(v_c_ctx[None]), "adamw_c_ctx")
    res["c_ctx"] = tuple(a[0] for a in t)

    names = ("c_ctx", "w_ada", "b_ada", "norm_mix", "norm_ffn", "w_in", "dn_conv", "dn_a_log", "dn_dt_bias", "dn_norm", "q_norm",
             "k_norm", "attn_sink", "w_branch_dn", "w_branch_attn", "w_out", "ffn_up", "ffn_conv", "ffn_conv_b", "ffn_down")
    lead = ("w_ada", "w_in", "w_branch_dn", "w_branch_attn", "w_out", "ffn_up", "ffn_down")
    fix = lambda k, a: a[None] if k in lead else a
    outs = [loss, grad_x[None]]
    for j in range(4):
        outs += [fix(k, res[k][j]) for k in names]
    return tuple(outs)
```

```python
import functools

import jax
import jax.numpy as jnp
import numpy as np
from jax import lax
from jax.experimental import pallas as pl
from jax.experimental.pallas import tpu as pltpu

F32 = jnp.float32
BF = jnp.bfloat16
HI = lax.Precision.HIGHEST
MESH = pl.DeviceIdType.MESH

D = 1024
NH = 8
HD = 128
KVH = 2
GRP = 4
KV = KVH * HD
DFF = 2816
CB = 128
GRID_W = 64
ROPE_BASE = 10000.0
EPS = 1e-6
N_DEV = 8
PW = 8192
O_QKV, O_GT, O_Q, O_MG, O_K, O_V, O_BA = 0, 3072, 4096, 5120, 7168, 7424, 7680
IN_SIZES = (3072, 1024, 16, 16, 1024, 256, 256, 2048)
IN_DIM = sum(IN_SIZES)
ADAM_LR, ADAM_B1, ADAM_B2, ADAM_EPS, ADAM_WD, ADAM_STEP = 0.001, 0.9, 0.999, 1e-08, 0.01, 10
VMEM_LIMIT = 56 * 1024 * 1024


def _cp():
    return pltpu.CompilerParams(vmem_limit_bytes=VMEM_LIMIT)


def _tile(n, cands):
    for c in cands:
        if n % c == 0:
            return c
    return n


def _iota2(shape):
    return lax.broadcasted_iota(jnp.int32, shape, 0), lax.broadcasted_iota(jnp.int32, shape, 1)


_DIMS = {"nn": ((1,), (0,)), "nt": ((1,), (1,)), "tn": ((0,), (0,))}


def _exchange_copies(ins, outs, send_sems, recv_sems, local_sems, scatter, landings):
    x, y, c = lax.axis_index("x"), lax.axis_index("y"), lax.axis_index("c")
    me = 4 * x + 2 * y + c
    local, remote = [], []
    for k in range(len(ins)):
        local.append(pltpu.make_async_copy(ins[k].at[me] if scatter else ins[k], outs[k].at[me], local_sems.at[k]))
        for m in range(1, N_DEV):
            px = 1 - x if m & 4 else x
            py = 1 - y if m & 2 else y
            pc = 1 - c if m & 1 else c
            peer = 4 * px + 2 * py + pc
            src = ins[k].at[peer] if scatter else ins[k]
            sem = k * (N_DEV - 1) + m - 1
            push = pltpu.make_async_remote_copy(src_ref=src, dst_ref=outs[k].at[me], send_sem=send_sems.at[sem],
                                                recv_sem=recv_sems.at[sem], device_id=(px, py, pc), device_id_type=MESH)
            landing = None
            if landings:
                landing = pltpu.make_async_remote_copy(src_ref=src, dst_ref=outs[k].at[peer], send_sem=send_sems.at[sem],
                                                       recv_sem=recv_sems.at[sem], device_id=(px, py, pc), device_id_type=MESH)
            remote.append((push, landing))
    return local, remote


def _exchange_start(*args):
    local, remote = _exchange_copies(*args, landings=False)
    for cp in local:
        cp.start()
    for push, _ in remote:
        push.start()


def _exchange_wait(*args):
    local, remote = _exchange_copies(*args, landings=True)
    for _, landing in remote:
        landing.wait_recv()
    for push, _ in remote:
        push.wait_send()
    for cp in local:
        cp.wait()


def _exchange_shapes(arrays, scatter):
    out_shape = [jax.ShapeDtypeStruct(a.shape if scatter else (N_DEV,) + a.shape, a.dtype) for a in arrays]
    n = len(arrays)
    sems = [pltpu.SemaphoreType.DMA((n * (N_DEV - 1),)), pltpu.SemaphoreType.DMA((n * (N_DEV - 1),)), pltpu.SemaphoreType.DMA((n,))]
    return out_shape, sems


def _mm(a, b, *, form, out_dtype, name, tm=None, tn=None, tk=None, exchange=None):
    if form == "tn":
        K, M = a.shape
        N = b.shape[1]
    else:
        M, K = a.shape
        N = b.shape[0] if form == "nt" else b.shape[1]
    tm = tm or _tile(M, (1408, 1280, 1024, 640, 512, 256, 128))
    tn = tn or _tile(N, (1408, 1024, 512, 256, 128))
    tk = tk or _tile(K, (2048, 1408, 1280, 1024, 640, 512, 256, 128))
    ni, nj, nk = M // tm, N // tn, K // tk
    dims = (_DIMS[form], ((), ()))
    ex_arrays, scatter = exchange if exchange else ([], False)
    nx = len(ex_arrays)

    def body(a_ref, b_ref, *refs):
        ex_in, o_ref, ex_out, scratch = refs[:nx], refs[nx], refs[nx + 1:2 * nx + 1], refs[2 * nx + 1:]
        i, j, k = pl.program_id(0), pl.program_id(1), pl.program_id(2)
        if nx:
            sems = scratch[-3:]

            @pl.when((i == 0) & (j == 0) & (k == 0))
            def _():
                _exchange_start(ex_in, ex_out, *sems, scatter)

        part = lax.dot_general(a_ref[...].astype(BF), b_ref[...].astype(BF), dims, preferred_element_type=F32)
        if nk == 1:
            o_ref[...] = part.astype(out_dtype)
        else:
            acc_ref = scratch[0]

            @pl.when(k == 0)
            def _():
                acc_ref[...] = part

            @pl.when(k > 0)
            def _():
                acc_ref[...] += part

            @pl.when(k == nk - 1)
            def _():
                o_ref[...] = acc_ref[...].astype(out_dtype)

        if nx:
            @pl.when((i == ni - 1) & (j == nj - 1) & (k == nk - 1))
            def _():
                _exchange_wait(ex_in, ex_out, *sems, scatter)

    if form == "tn":
        a_spec = pl.BlockSpec((tk, tm), lambda i, j, k: (k, i))
    else:
        a_spec = pl.BlockSpec((tm, tk), lambda i, j, k: (i, k))
    if form == "nt":
        b_spec = pl.BlockSpec((tn, tk), lambda i, j, k: (j, k))
    else:
        b_spec = pl.BlockSpec((tk, tn), lambda i, j, k: (k, j))
    hbm = pl.BlockSpec(memory_space=pl.ANY)
    ex_shapes, ex_sems = _exchange_shapes(ex_arrays, scatter) if nx else ([], [])
    outs = pl.pallas_call(
        body, grid=(ni, nj, nk), name=name,
        in_specs=[a_spec, b_spec] + [hbm] * nx, out_specs=[pl.BlockSpec((tm, tn), lambda i, j, k: (i, j))] + [hbm] * nx,
        out_shape=[jax.ShapeDtypeStruct((M, N), out_dtype)] + ex_shapes,
        scratch_shapes=([] if nk == 1 else [pltpu.VMEM((tm, tn), F32)]) + ex_sems,
        compiler_params=_cp(),
    )(a, b, *ex_arrays)
    return (outs[0], list(outs[1:])) if nx else outs[0]


def _norm_mod_fn(x, nw, sh, sc):
    y = x * lax.rsqrt(jnp.mean(x * x, axis=-1, keepdims=True) + EPS)
    return (y * nw) * (1.0 + sc) + sh


def _norm_mod_fwd(x, ctx, nw, sh, sc, name):
    nlat = x.shape[0]
    T = nlat + ctx.shape[0]
    tb = _tile(ctx.shape[0], (256, 128))
    nlb = nlat // tb

    def body(x_ref, c_ref, nw_ref, sh_ref, sc_ref, h_ref):
        rows = jnp.where(pl.program_id(0) < nlb, x_ref[...], c_ref[...])
        h_ref[...] = _norm_mod_fn(rows, nw_ref[...], sh_ref[0], sc_ref[0]).astype(BF)

    seg = pl.BlockSpec((1, 1, D), lambda i: (jnp.where(i >= nlb, 1, 0), 0, 0))
    return pl.pallas_call(
        body, grid=(T // tb,), name=name,
        in_specs=[pl.BlockSpec((tb, D), lambda i: (jnp.minimum(i, nlb - 1), 0)),
                  pl.BlockSpec((tb, D), lambda i: (jnp.maximum(i - nlb, 0), 0)), pl.BlockSpec((1, D), lambda i: (0, 0)), seg, seg],
        out_specs=pl.BlockSpec((tb, D), lambda i: (i, 0)),
        out_shape=jax.ShapeDtypeStruct((T, D), BF),
    )(x, ctx, nw, sh, sc)


def _norm_mod_bwd(x, nw, sh, sc, dh, *, row0, name, residual=None):
    nrows = x.shape[0]
    tb = _tile(nrows, (512, 256, 128))
    b0 = row0 // tb

    def body(x_ref, nw_ref, sh_ref, sc_ref, dh_ref, *refs):
        dnw_ref, dsh_ref, dsc_ref = refs[-3:]
        _, vjp = jax.vjp(_norm_mod_fn, x_ref[...], nw_ref[...], sh_ref[...], sc_ref[...])
        dx, dnw, dsh, dsc = vjp(dh_ref[...])
        if residual is not None:
            refs[1][...] = dx + refs[0][...]

        @pl.when(pl.program_id(0) == 0)
        def _():
            dnw_ref[...] = jnp.zeros_like(dnw_ref)
            dsh_ref[...] = jnp.zeros_like(dsh_ref)
            dsc_ref[...] = jnp.zeros_like(dsc_ref)

        dnw_ref[...] += dnw
        dsh_ref[...] += dsh
        dsc_ref[...] += dsc

    dh_row = pl.BlockSpec((tb, D), lambda i: (b0 + i, 0))
    out_row = pl.BlockSpec((tb, D), lambda i: (i, 0))
    one = pl.BlockSpec((1, D), lambda i: (0, 0))
    with_dx = residual is not None
    return pl.pallas_call(
        body, grid=(nrows // tb,), name=name,
        in_specs=[out_row, one, one, one, dh_row] + [out_row] * with_dx, out_specs=[out_row] * with_dx + [one] * 3,
        out_shape=[jax.ShapeDtypeStruct((nrows, D), F32)] * with_dx + [jax.ShapeDtypeStruct((1, D), F32)] * 3,
        compiler_params=_cp(),
    )(x, nw, sh, sc, dh, *([residual] if with_dx else []))


def _resid_norm_fwd(x, gate, y, nw, sh, sc):
    n = y.shape[0]
    tb = _tile(n, (512, 256, 128))

    def body(x_ref, g_ref, y_ref, nw_ref, sh_ref, sc_ref, x1_ref, h_ref):
        x1 = x_ref[...] + g_ref[...] * y_ref[...]
        x1_ref[...] = x1
        h_ref[...] = _norm_mod_fn(x1, nw_ref[...], sh_ref[...], sc_ref[...]).astype(BF)

    row = pl.BlockSpec((tb, D), lambda i: (i, 0))
    one = pl.BlockSpec((1, D), lambda i: (0, 0))
    return pl.pallas_call(
        body, grid=(n // tb,), name="resid_norm_fwd",
        in_specs=[row, one, row, one, one, one], out_specs=[row, row],
        out_shape=[jax.ShapeDtypeStruct((n, D), F32), jax.ShapeDtypeStruct((n, D), BF)],
        compiler_params=_cp(),
    )(x, gate, y, nw, sh, sc)


def _resid_norm_bwd(x1, gate, y, nw, sh, sc, dh, dx1_direct, w_out):
    n = y.shape[0]
    tb = _tile(n, (512, 256, 128))

    def body(x1_ref, g_ref, y_ref, nw_ref, sh_ref, sc_ref, dh_ref, dd_ref, wo_ref,
             dx_ref, dy_ref, dg_ref, dnw_ref, dsh_ref, dsc_ref, dm_ref):
        _, vjp = jax.vjp(_norm_mod_fn, x1_ref[...], nw_ref[...], sh_ref[...], sc_ref[...])
        dxn, dnw, dsh, dsc = vjp(dh_ref[...])
        dx = dxn + dd_ref[...]
        dx_ref[...] = dx
        dy_ref[...] = (g_ref[...] * dx).astype(BF)
        dm_ref[...] = lax.dot_general(dy_ref[...], wo_ref[...], (_DIMS["nt"], ((), ())), preferred_element_type=F32)

        @pl.when(pl.program_id(0) == 0)
        def _():
            for r in (dg_ref, dnw_ref, dsh_ref, dsc_ref):
                r[...] = jnp.zeros_like(r)

        dg_ref[...] += jnp.sum(dx * y_ref[...], axis=0, keepdims=True)
        dnw_ref[...] += dnw
        dsh_ref[...] += dsh
        dsc_ref[...] += dsc

    row = pl.BlockSpec((tb, D), lambda i: (i, 0))
    one = pl.BlockSpec((1, D), lambda i: (0, 0))
    return pl.pallas_call(
        body, grid=(n // tb,), name="resid_norm_bwd",
        in_specs=[row, one, row, one, one, one, row, row, _resident((D, D))], out_specs=[row, row] + [one] * 4 + [row],
        out_shape=[jax.ShapeDtypeStruct((n, D), F32), jax.ShapeDtypeStruct((n, D), BF)] + [jax.ShapeDtypeStruct((1, D), F32)] * 4
        + [jax.ShapeDtypeStruct((n, D), F32)],
        compiler_params=_cp(),
    )(x1, gate, y, nw, sh, sc, dh, dx1_direct, w_out)


HALO = 8


def _halo_specs(tb, width, nrows, col=0, halo=HALO):
    r8 = tb // halo
    cur = pl.BlockSpec((tb, width), lambda i: (i, col))
    prev = pl.BlockSpec((halo, width), lambda i: (jnp.maximum(i * r8 - 1, 0), col))
    nxt = pl.BlockSpec((halo, width), lambda i: (jnp.minimum((i + 1) * r8, nrows // halo - 1), col))
    return [cur, prev, nxt]


def _segment_edges(seg_rows, tb):
    bounds = [0]
    for s in seg_rows:
        bounds.append(bounds[-1] + s // tb)
    return bounds[:-1], [b - 1 for b in bounds[1:]]


def _keep_halos(i, starts, ends):
    keep_p = functools.reduce(lambda a, b: a & b, [i != s for s in starts])
    keep_n = functools.reduce(lambda a, b: a & b, [i != e for e in ends])
    return keep_p, keep_n


def _ext_rows(refs, cols, keep):
    cur_ref, prev_ref, next_ref = refs
    p = jnp.where(keep[0], prev_ref[:, cols].astype(F32), 0.0)
    n = jnp.where(keep[1], next_ref[:, cols].astype(F32), 0.0)
    return jnp.concatenate([p, cur_ref[:, cols].astype(F32), n], axis=0)


def _shifted_rows(xe, width, transpose=False):
    r = width // 2
    n = xe.shape[0]
    out = []
    for j in range(width):
        s = ((j - r) if transpose else (r - j)) % n
        out.append(xe if s == 0 else pltpu.roll(xe, s, 0))
    return out


def _conv_rows(shifted, w_ref, cols):
    acc = None
    for j, xs in enumerate(shifted):
        term = xs * w_ref[j:j + 1, cols]
        acc = term if acc is None else acc + term
    return acc


def _tap_grads(dcur, shifted, tb):
    return [jnp.sum(dcur * xs[HALO:HALO + tb], axis=0, keepdims=True) for xs in shifted]


def _softplus(x):
    return jnp.maximum(x, 0.0) + jnp.log(1.0 + jnp.exp(-jnp.abs(x)))


def _gates_fn(ba, alog_row, dt_row):
    col = lax.broadcasted_iota(jnp.int32, ba.shape, 1)
    beta = jax.nn.sigmoid(ba)
    g = -jnp.exp(alog_row) * _softplus(ba + dt_row)
    return jnp.where(col < 16, beta, jnp.where(col < 32, g, 0.0))


def _qkv_post_fn(c, kind):
    y = jax.nn.silu(c)
    if kind == 2:
        return y
    n = y * lax.rsqrt(jnp.sum(y * y, axis=-1, keepdims=True) + EPS)
    return n * (HD ** -0.5) if kind == 0 else n


DN_TAPS = 5
FFN_TAPS = 3


def _dn_pre_fwd(p, w8, alog_row, dt_row, seg_rows):
    T = p.shape[0]
    tb = _tile(T, (256, 128))
    starts, ends = _segment_edges(seg_rows, tb)

    def body(cur_ref, prev_ref, next_ref, ba_ref, w_ref, al_ref, dt_ref, q_ref, k_ref, v_ref, gb_ref):
        keep = _keep_halos(pl.program_id(0), starts, ends)
        outs = (q_ref, k_ref, v_ref)
        for kind in range(3):
            for h in range(NH):
                cols = slice(kind * D + h * HD, kind * D + (h + 1) * HD)
                xe = _ext_rows((cur_ref, prev_ref, next_ref), cols, keep)
                conv = _conv_rows(_shifted_rows(xe, DN_TAPS), w_ref, cols)[HALO:HALO + tb]
                outs[kind][:, h * HD:(h + 1) * HD] = _qkv_post_fn(conv, kind)
        gb_ref[...] = _gates_fn(ba_ref[...], al_ref[...], dt_ref[...])

    row = pl.BlockSpec((tb, D), lambda i: (i, 0))
    one = pl.BlockSpec((1, 128), lambda i: (0, 0))
    return pl.pallas_call(
        body, grid=(T // tb,), name="dn_pre_fwd",
        in_specs=_halo_specs(tb, 3 * D, T) + [pl.BlockSpec((tb, 128), lambda i: (i, O_BA // 128)),
                                              pl.BlockSpec((8, 3 * D), lambda i: (0, 0)), one, one],
        out_specs=[row, row, row, pl.BlockSpec((tb, 128), lambda i: (i, 0))],
        out_shape=[jax.ShapeDtypeStruct((T, D), F32)] * 3 + [jax.ShapeDtypeStruct((T, 128), F32)],
        compiler_params=_cp(),
    )(p, p, p, p, w8, alog_row, dt_row)


def _dn_pre_bwd(p, w8, alog_row, dt_row, dq, dk, dv, dgb, others, seg_rows):
    T = p.shape[0]
    tb = _tile(T, (256, 128))
    starts, ends = _segment_edges(seg_rows, tb)
    other_cols = (O_GT, O_Q, O_MG, O_K, O_V)
    assert [o.shape[1] for o in others] == [O_Q - O_GT, O_MG - O_Q, O_K - O_MG, O_V - O_K, O_BA - O_V]

    def body(cur_ref, prev_ref, next_ref, ba_ref, w_ref, al_ref, dt_ref,
             dq_c, dq_p, dq_n, dk_c, dk_p, dk_n, dv_c, dv_p, dv_n, dgb_ref, gt_ref, q_ref, mg_ref, k_ref, v_ref,
             dx_ref, dw_ref, dal_ref, ddt_ref):
        i = pl.program_id(0)
        for c0, ref in zip(other_cols, (gt_ref, q_ref, mg_ref, k_ref, v_ref)):
            dx_ref[:, c0:c0 + ref.shape[1]] = ref[...]
        dx_ref[:, O_BA + 128:] = jnp.zeros((tb, PW - O_BA - 128), BF)
        keep = _keep_halos(i, starts, ends)

        @pl.when(i == 0)
        def _():
            dw_ref[...] = jnp.zeros_like(dw_ref)
            dal_ref[...] = jnp.zeros_like(dal_ref)
            ddt_ref[...] = jnp.zeros_like(ddt_ref)

        douts = ((dq_c, dq_p, dq_n), (dk_c, dk_p, dk_n), (dv_c, dv_p, dv_n))
        for kind in range(3):
            for h in range(NH):
                cols = slice(kind * D + h * HD, kind * D + (h + 1) * HD)
                xe = _ext_rows((cur_ref, prev_ref, next_ref), cols, keep)
                shifted = _shifted_rows(xe, DN_TAPS)
                conv = _conv_rows(shifted, w_ref, cols)
                dye = _ext_rows(douts[kind], slice(h * HD, (h + 1) * HD), keep)
                _, vjp = jax.vjp(functools.partial(_qkv_post_fn, kind=kind), conv)
                dce = vjp(dye)[0]
                dx_ref[:, cols] = _conv_rows(_shifted_rows(dce, DN_TAPS, transpose=True), w_ref, cols)[HALO:HALO + tb].astype(BF)
                for j, g in enumerate(_tap_grads(dce[HALO:HALO + tb], shifted, tb)):
                    dw_ref[j:j + 1, cols] += g
        _, vjp = jax.vjp(_gates_fn, ba_ref[...], al_ref[...], dt_ref[...])
        dba, dal, ddt = vjp(dgb_ref[...])
        dx_ref[:, O_BA:O_BA + 128] = dba.astype(BF)
        dal_ref[...] += dal
        ddt_ref[...] += ddt

    one = pl.BlockSpec((1, 128), lambda i: (0, 0))
    nar = pl.BlockSpec((tb, 128), lambda i: (i, 0))
    wspec = pl.BlockSpec((8, 3 * D), lambda i: (0, 0))
    return pl.pallas_call(
        body, grid=(T // tb,), name="dn_pre_bwd",
        in_specs=_halo_specs(tb, 3 * D, T) + [pl.BlockSpec((tb, 128), lambda i: (i, O_BA // 128)), wspec, one, one]
        + _halo_specs(tb, D, T) * 3 + [nar] + [pl.BlockSpec((tb, o.shape[1]), lambda i: (i, 0)) for o in others],
        out_specs=[pl.BlockSpec((tb, PW), lambda i: (i, 0)), wspec, one, one],
        out_shape=[jax.ShapeDtypeStruct((T, PW), BF), jax.ShapeDtypeStruct((8, 3 * D), F32),
                   jax.ShapeDtypeStruct((1, 128), F32), jax.ShapeDtypeStruct((1, 128), F32)],
        compiler_params=_cp(),
    )(p, p, p, p, w8, alog_row, dt_row, dq, dq, dq, dk, dk, dk, dv, dv, dv, dgb, *others)


def _dot_hi(a, b):
    return jnp.dot(a, b, precision=HI, preferred_element_type=F32)


def _dot_bf(a, b):
    return jnp.dot(a.astype(BF), b.astype(BF), preferred_element_type=F32)


def _dot_nt_bf(a, b):
    return lax.dot_general(a.astype(BF), b.astype(BF), (_DIMS["nt"], ((), ())), preferred_element_type=F32)


def _dot_tn_bf(a, b):
    return lax.dot_general(a.astype(BF), b.astype(BF), (_DIMS["tn"], ((), ())), preferred_element_type=F32)


def _dot_h3(a, b):
    return jnp.dot(a, b, precision=lax.Precision.HIGH, preferred_element_type=F32)


def _dot_split(fine, coarse, form):
    hi = fine.astype(BF)
    lo = (fine - hi.astype(F32)).astype(BF)
    cb = coarse.astype(BF)
    if form == "tn":
        return lax.dot_general(jnp.concatenate([cb, cb], axis=0), jnp.concatenate([hi, lo], axis=0),
                               (_DIMS["tn"], ((), ())), preferred_element_type=F32)
    parts = jnp.concatenate([hi, lo], axis=1)
    if form == "nt":
        return lax.dot_general(parts, jnp.concatenate([cb, cb], axis=1), (_DIMS["nt"], ((), ())), preferred_element_type=F32)
    return jnp.dot(parts, jnp.concatenate([cb, cb], axis=0), preferred_element_type=F32)


@jax.custom_vjp
def _mm_split(a, b):
    return _dot_split(a, b, "nn")


_mm_split.defvjp(lambda a, b: (_dot_split(a, b, "nn"), (a, b)),
                 lambda res, dc: (_dot_split(dc, res[1], "nt"), _dot_split(dc, res[0], "tn")))


def _unit_tri_inverses(mats):
    r, c = _iota2((CB, CB))
    eye = (r == c).astype(F32)
    a8 = [jnp.where((r // 8) == (c // 8), a, 0.0) for a in mats]
    a2 = [_dot_split(x, x, "nn") for x in a8]
    a4 = [_dot_split(x, x, "nn") for x in a2]
    t = [_dot_split(eye - x, eye + y, "nn") for x, y in zip(a8, a2)]
    t = [_dot_split(x, eye + y, "nn") for x, y in zip(t, a4)]
    b = 8
    while b < CB:
        mask = ((r // (2 * b)) == (c // (2 * b))) & ((r // b) != (c // b))
        te = [_dot_split(x, jnp.where(mask, a, 0.0), "nn") for x, a in zip(t, mats)]
        t = [x - _dot_split(y, x, "nn") for x, y in zip(t, te)]
        b *= 2
    return t


@jax.custom_vjp
def _saved_inverse(a, t):
    return t


_saved_inverse.defvjp(lambda a, t: (t, t),
                      lambda t, dt: (-_dot_split(_dot_split(dt, t, "nt"), t, "tn"), jnp.zeros_like(t)))


def _dn1_decay(gc, reverse):
    r, c = _iota2((CB, CB))
    incl = (c >= r) if reverse else (c <= r)
    return jnp.where(incl, jnp.exp(jnp.where(incl, gc - gc.T, 0.0)), 0.0)


def _dn1_heads(qs, ks, vs, betas, gcs, ts_saved, reverse, kks=None, qks=None):
    r, c = _iota2((CB, CB))
    strict = (c > r) if reverse else (c < r)
    decays = [_dn1_decay(gc, reverse) for gc in gcs]
    kks = kks or [_dot_nt_bf(k, k) for k in ks]
    systems = [jnp.where(strict, b * kk * dc, 0.0) for b, kk, dc in zip(betas, kks, decays)]
    if ts_saved is None:
        ts = _unit_tri_inverses(systems)
    else:
        ts = [_saved_inverse(a, t) for a, t in zip(systems, ts_saved)]
    egs = [jnp.exp(gc) for gc in gcs]
    us = [_mm_split(t, v * b) for t, v, b in zip(ts, vs, betas)]
    ws = [_mm_split(t, k * (b * eg)) for t, k, b, eg in zip(ts, ks, betas, egs)]
    qks = qks or [_dot_nt_bf(q, k) for q, k in zip(qs, ks)]
    last = 0 if reverse else CB - 1
    glogs = [jnp.sum(jnp.where(r == last, gc, 0.0), axis=0, keepdims=True) for gc in gcs]
    outs = [(u, w, q * eg, k * jnp.exp(gl - gc), qk * dc, jnp.exp(gl))
            for u, w, q, k, eg, gl, gc, qk, dc in zip(us, ws, qs, ks, egs, glogs, gcs, qks, decays)]
    return outs, ts


def _cum_matrix(upper):
    r, c = _iota2((CB, CB))
    return ((c >= r) if upper else (c <= r)).astype(F32)


def _lane_bcast(x, col):
    return jnp.broadcast_to(x[:, col:col + 1], x.shape)


_HEAD_SLICES = [slice(h * HD, (h + 1) * HD) for h in range(NH)]


def _dn1_fwd(q, k, v, gb):
    T = q.shape[0]
    nb = T // CB

    def body(q_ref, k_ref, v_ref, gb_ref, *out_refs):
        gbv = gb_ref[...]
        qs = [q_ref[:, sl] for sl in _HEAD_SLICES]
        ks = [k_ref[:, sl] for sl in _HEAD_SLICES]
        vs = [v_ref[:, sl] for sl in _HEAD_SLICES]
        kks = [_dot_nt_bf(x, x) for x in ks]
        qks = [_dot_nt_bf(x, y) for x, y in zip(qs, ks)]
        for d in (0, 1):
            u_ref, w_ref, qg_ref, kd_ref, qkd_ref, gl_ref, t_ref = out_refs[7 * d:7 * d + 7]
            gcum = _dot_h3(_cum_matrix(d == 1), gbv)
            betas = [_lane_bcast(gbv, d * NH + h) for h in range(NH)]
            gcs = [_lane_bcast(gcum, 16 + d * NH + h) for h in range(NH)]
            outs, ts = _dn1_heads(qs, ks, vs, betas, gcs, None, d == 1, kks, qks)
            for h, sl in enumerate(_HEAD_SLICES):
                u, w, qg, kd, qkd, gl = outs[h]
                u_ref[:, sl] = u
                w_ref[:, sl] = w.astype(BF)
                qg_ref[:, sl] = qg.astype(BF)
                kd_ref[:, sl] = kd.astype(BF)
                qkd_ref[:, sl] = qkd.astype(BF)
                gl_ref[h] = gl
                t_ref[:, sl] = ts[h]

    tb = pl.BlockSpec((CB, D), lambda i: (i, 0))
    one_dir_specs = [tb, tb, tb, tb, tb, pl.BlockSpec((NH, 1, 128), lambda i: (i, 0, 0)), tb]
    one_dir_shapes = ([jax.ShapeDtypeStruct((T, D), F32)] + [jax.ShapeDtypeStruct((T, D), BF)] * 4
                      + [jax.ShapeDtypeStruct((nb * NH, 1, 128), F32), jax.ShapeDtypeStruct((T, D), F32)])
    outs = pl.pallas_call(
        body, grid=(nb,), name="dn1_fwd",
        in_specs=[tb, tb, tb, pl.BlockSpec((CB, 128), lambda i: (i, 0))],
        out_specs=one_dir_specs * 2, out_shape=one_dir_shapes * 2, compiler_params=_cp(),
    )(q, k, v, gb)
    return [tuple(outs[:7]), tuple(outs[7:])]


def _dn1_bwd(q, k, v, gb, tinvs, cots):
    T = q.shape[0]
    nb = T // CB

    def body(q_ref, k_ref, v_ref, gb_ref, *refs):
        dir_refs, (dq_ref, dk_ref, dv_ref, dgb_ref) = refs[:14], refs[14:]
        gbv = gb_ref[...]
        qs = [q_ref[:, sl] for sl in _HEAD_SLICES]
        ks = [k_ref[:, sl] for sl in _HEAD_SLICES]
        vs = [v_ref[:, sl] for sl in _HEAD_SLICES]
        lane = lax.broadcasted_iota(jnp.int32, (CB, 128), 1)
        dgb = jnp.zeros((CB, 128), F32)
        for d in (0, 1):
            t_ref, du_ref, dw_ref, dqg_ref, dkd_ref, dqkd_ref, dgl_ref = dir_refs[7 * d:7 * d + 7]
            gcum = _dot_h3(_cum_matrix(d == 1), gbv)
            betas = [_lane_bcast(gbv, d * NH + h) for h in range(NH)]
            gcs = [_lane_bcast(gcum, 16 + d * NH + h) for h in range(NH)]
            ts = [t_ref[:, sl] for sl in _HEAD_SLICES]
            f = lambda qs, ks, vs, betas, gcs: _dn1_heads(qs, ks, vs, betas, gcs, ts, d == 1)[0]
            _, vjp = jax.vjp(f, qs, ks, vs, betas, gcs)
            cot = [(du_ref[:, sl], dw_ref[:, sl], dqg_ref[:, sl], dkd_ref[:, sl], dqkd_ref[:, sl], dgl_ref[h])
                   for h, sl in enumerate(_HEAD_SLICES)]
            dqs, dks, dvs, dbetas, dgcs = vjp(cot)
            dgcum = jnp.zeros((CB, 128), F32)
            for h, sl in enumerate(_HEAD_SLICES):
                if d == 0:
                    dq_ref[:, sl] = dqs[h]
                    dk_ref[:, sl] = dks[h]
                    dv_ref[:, sl] = dvs[h]
                else:
                    dq_ref[:, sl] += dqs[h]
                    dk_ref[:, sl] += dks[h]
                    dv_ref[:, sl] += dvs[h]
                dgb = dgb + jnp.where(lane == d * NH + h, jnp.sum(dbetas[h], axis=1, keepdims=True), 0.0)
                dgcum = dgcum + jnp.where(lane == 16 + d * NH + h, jnp.sum(dgcs[h], axis=1, keepdims=True), 0.0)
            dgb = dgb + _dot_h3(_cum_matrix(d == 0), dgcum)
        dgb_ref[...] = dgb

    tb = pl.BlockSpec((CB, D), lambda i: (i, 0))
    gbs = pl.BlockSpec((CB, 128), lambda i: (i, 0))
    gls = pl.BlockSpec((NH, 1, 128), lambda i: (i, 0, 0))
    args = []
    for d in (0, 1):
        args += [tinvs[d], *cots[d]]
    return pl.pallas_call(
        body, grid=(nb,), name="dn1_bwd",
        in_specs=[tb, tb, tb, gbs] + [tb, tb, tb, tb, tb, tb, gls] * 2, out_specs=[tb, tb, tb, gbs],
        out_shape=[jax.ShapeDtypeStruct((T, D), F32)] * 3 + [jax.ShapeDtypeStruct((T, 128), F32)],
        compiler_params=_cp(),
    )(q, k, v, gb, *args)


def _dn2_steps(chains):
    ws = [_dot_bf(w, s) for _, w, _, _, _, _, s in chains]
    v_new = [c[0] - x for c, x in zip(chains, ws)]
    o_state = [_dot_bf(c[2], c[6]) for c in chains]
    o_local = [_dot_bf(c[4], vn) for c, vn in zip(chains, v_new)]
    grow = [_dot_tn_bf(c[3], vn) for c, vn in zip(chains, v_new)]
    return [a + b for a, b in zip(o_state, o_local)], [c[6] * c[5] + g for c, g in zip(chains, grow)]


def _scan_order(direction, nlat_b, nall_b):
    if direction == 0:
        return lambda i: (i + nlat_b) % nall_b
    return lambda i: nall_b - 1 - i


def _dn2_fwd(per_dir, nlat):
    T = per_dir[0][0].shape[0]
    nb = T // CB
    blks = [_scan_order(d, nlat // CB, nb) for d in (0, 1)]

    def body(*refs):
        ins, outs, s_scr = refs[:12], refs[12:16], refs[16]

        @pl.when(pl.program_id(0) == 0)
        def _():
            s_scr[...] = jnp.zeros_like(s_scr)
        for d in (0, 1):
            outs[2 * d + 1][0] = s_scr[d]
        where = [(d, h, sl) for h, sl in enumerate(_HEAD_SLICES) for d in (0, 1)]
        chains = []
        for d, h, sl in where:
            u_ref, w_ref, qg_ref, kd_ref, qkd_ref, gl_ref = ins[6 * d:6 * d + 6]
            chains.append((u_ref[:, sl], w_ref[:, sl], qg_ref[:, sl], kd_ref[:, sl], qkd_ref[:, sl], gl_ref[h], s_scr[d, h]))
        os, states = _dn2_steps(chains)
        for (d, h, sl), o, s_next in zip(where, os, states):
            outs[2 * d][:, sl] = o
            s_scr[d, h] = s_next

    in_specs, out_specs, args = [], [], []
    for d in (0, 1):
        blk = blks[d]
        tb = pl.BlockSpec((CB, D), lambda i, blk=blk: (blk(i), 0))
        in_specs += [tb] * 5 + [pl.BlockSpec((NH, 1, 128), lambda i, blk=blk: (blk(i), 0, 0))]
        out_specs += [tb, pl.BlockSpec((1, NH, HD, HD), lambda i, blk=blk: (blk(i), 0, 0, 0))]
        args += list(per_dir[d])
    outs = pl.pallas_call(
        body, grid=(nb,), name="dn2_fwd", in_specs=in_specs, out_specs=out_specs,
        out_shape=[jax.ShapeDtypeStruct((T, D), F32), jax.ShapeDtypeStruct((nb, NH, HD, HD), F32)] * 2,
        scratch_shapes=[pltpu.VMEM((2, NH, HD, HD), F32)], compiler_params=_cp(),
    )(*args)
    return [tuple(outs[:2]), tuple(outs[2:])]


def _dn2_bwd(per_dir, do, nlat):
    T = per_dir[0][0].shape[0]
    nb = T // CB
    nlat_b = nlat // CB
    fwd = [_scan_order(d, nlat_b, nb) for d in (0, 1)]
    blks = [lambda i, f=f: f(nb - 1 - i) for f in fwd]

    def body(*refs):
        ins, outs, ds_scr = refs[:16], refs[16:28], refs[28]
        i = pl.program_id(0)

        @pl.when(i == 0)
        def _():
            ds_scr[...] = jnp.zeros_like(ds_scr)
        where = [(d, h, sl) for h, sl in enumerate(_HEAD_SLICES) for d in (0, 1)]
        chains, cot_o, cot_s = [], [], []
        for d, h, sl in where:
            u_ref, w_ref, qg_ref, kd_ref, qkd_ref, gl_ref, sall_ref, do_ref = ins[8 * d:8 * d + 8]
            chains.append((u_ref[:, sl], w_ref[:, sl].astype(F32), qg_ref[:, sl].astype(F32), kd_ref[:, sl].astype(F32),
                           qkd_ref[:, sl].astype(F32), gl_ref[h], sall_ref[0, h]))
            cot_o.append(jnp.where(blks[d](i) < nlat_b, do_ref[:, sl], 0.0))
            cot_s.append(ds_scr[d, h])
        _, vjp = jax.vjp(_dn2_steps, chains)
        for (d, h, sl), (du, dw, dqg, dkd, dqkd, dgl, ds) in zip(where, vjp((cot_o, cot_s))[0]):
            du_ref, dw_ref, dqg_ref, dkd_ref, dqkd_ref, dgl_ref = outs[6 * d:6 * d + 6]
            du_ref[:, sl] = du
            dw_ref[:, sl] = dw
            dqg_ref[:, sl] = dqg
            dkd_ref[:, sl] = dkd
            dqkd_ref[:, sl] = dqkd
            dgl_ref[h] = dgl
            ds_scr[d, h] = ds

    in_specs, out_specs, args = [], [], []
    for d in (0, 1):
        blk = blks[d]
        tb = pl.BlockSpec((CB, D), lambda i, blk=blk: (blk(i), 0))
        gls = pl.BlockSpec((NH, 1, 128), lambda i, blk=blk: (blk(i), 0, 0))
        in_specs += [tb] * 5 + [gls, pl.BlockSpec((1, NH, HD, HD), lambda i, blk=blk: (blk(i), 0, 0, 0)),
                                pl.BlockSpec((CB, D), lambda i, blk=blk: (jnp.minimum(blk(i), nlat_b - 1), 0))]
        out_specs += [tb] * 5 + [gls]
        args += list(per_dir[d]) + [do]
    outs = pl.pallas_call(
        body, grid=(nb,), name="dn2_bwd", in_specs=in_specs, out_specs=out_specs,
        out_shape=([jax.ShapeDtypeStruct((T, D), F32)] * 5 + [jax.ShapeDtypeStruct((nb * NH, 1, 128), F32)]) * 2,
        scratch_shapes=[pltpu.VMEM((2, NH, HD, HD), F32)], compiler_params=_cp(),
    )(*args)
    return [tuple(outs[:6]), tuple(outs[6:])]


def _ghn_fn(o, gt, w):
    y = o * lax.rsqrt(jnp.mean(o * o, axis=-1, keepdims=True) + EPS)
    return (y * w) * jax.nn.silu(gt)


def _ghn_fwd(o_f, o_b, p, w, w_branch, nlat):
    tb = _tile(nlat, (512, 256, 128))

    def body(of_ref, ob_ref, gt_ref, w_ref, wb_ref, y_ref, z_ref):
        for h in range(NH):
            sl = slice(h * HD, (h + 1) * HD)
            y_ref[:, sl] = _ghn_fn(of_ref[:, sl] + ob_ref[:, sl], gt_ref[:, sl], w_ref[...]).astype(BF)
        z_ref[...] = jnp.dot(y_ref[...], wb_ref[...], preferred_element_type=F32)

    row = pl.BlockSpec((tb, D), lambda i: (i, 0))
    return pl.pallas_call(
        body, grid=(nlat // tb,), name="ghn_fwd",
        in_specs=[row, row, pl.BlockSpec((tb, D), lambda i: (i, O_GT // D)), pl.BlockSpec((1, HD), lambda i: (0, 0)), _resident((D, D))],
        out_specs=[row, row], out_shape=[jax.ShapeDtypeStruct((nlat, D), BF), jax.ShapeDtypeStruct((nlat, D), F32)],
        compiler_params=_cp(),
    )(o_f, o_b, p, w, w_branch)


def _ghn_bwd(o_f, o_b, p, w, dy, nlat):
    T = p.shape[0]
    tb = _tile(nlat, (256, 128))
    nlb = nlat // tb

    def body(of_ref, ob_ref, gt_ref, w_ref, dy_ref, do_ref, dgt_ref, dw_ref):
        is_lat = pl.program_id(0) < nlb

        @pl.when(pl.program_id(0) == 0)
        def _():
            dw_ref[...] = jnp.zeros_like(dw_ref)
        for h in range(NH):
            sl = slice(h * HD, (h + 1) * HD)
            _, vjp = jax.vjp(_ghn_fn, of_ref[:, sl] + ob_ref[:, sl], gt_ref[:, sl], w_ref[...])
            do, dgt, dw = vjp(dy_ref[:, sl])
            do_ref[:, sl] = do
            dgt_ref[:, sl] = jnp.where(is_lat, dgt, 0.0).astype(BF)
            dw_ref[...] += jnp.where(is_lat, dw, 0.0)

    lat = lambda i: jnp.minimum(i, nlb - 1)
    row = pl.BlockSpec((tb, D), lambda i: (lat(i), 0))
    one = pl.BlockSpec((1, HD), lambda i: (0, 0))
    return pl.pallas_call(
        body, grid=(T // tb,), name="ghn_bwd",
        in_specs=[row, row, pl.BlockSpec((tb, D), lambda i: (lat(i), O_GT // D)), one, row],
        out_specs=[row, pl.BlockSpec((tb, D), lambda i: (i, 0)), one],
        out_shape=[jax.ShapeDtypeStruct((nlat, D), F32), jax.ShapeDtypeStruct((T, D), BF), jax.ShapeDtypeStruct((1, HD), F32)],
    )(o_f, o_b, p, w, dy)


@jax.custom_vjp
def _swap32(x):
    lane = lax.broadcasted_iota(jnp.int32, x.shape, 1)
    return jnp.where((lane & 32) == 0, pltpu.roll(x, 96, 1), pltpu.roll(x, 32, 1))


_swap32.defvjp(lambda x: (_swap32(x), None), lambda _, g: (_swap32(g),))


def _qk_post_fn(xs, w, cos, sin):
    inv = [lax.rsqrt(jnp.mean(x * x, axis=-1, keepdims=True) + EPS) for x in xs]
    ys = [(x * r) * w for x, r in zip(xs, inv)]
    return [y * cos + _swap32(y) * sin for y in ys]


def _attn_prep_fwd(p, qn, kn, cos, sin):
    T = p.shape[0]
    tb = _tile(T, (256, 128))

    def body(q_ref, k_ref, v_ref, qn_ref, kn_ref, cos_ref, sin_ref, qr_ref, kr_ref, vb_ref):
        cos_v, sin_v = cos_ref[...], sin_ref[...]
        for sl, y in zip(_HEAD_SLICES, _qk_post_fn([q_ref[:, sl] for sl in _HEAD_SLICES], qn_ref[...], cos_v, sin_v)):
            qr_ref[:, sl] = y.astype(BF)
        for sl, y in zip(_HEAD_SLICES, _qk_post_fn([k_ref[:, sl] for sl in _HEAD_SLICES[:KVH]], kn_ref[...], cos_v, sin_v)):
            kr_ref[:, sl] = y.astype(BF)
        vb_ref[...] = v_ref[...].astype(BF)

    one = pl.BlockSpec((1, HD), lambda i: (0, 0))
    tab = pl.BlockSpec((tb, HD), lambda i: (i, 0))
    return pl.pallas_call(
        body, grid=(T // tb,), name="attn_prep_fwd",
        in_specs=[pl.BlockSpec((tb, D), lambda i: (i, O_Q // D)), pl.BlockSpec((tb, KV), lambda i: (i, O_K // KV)),
                  pl.BlockSpec((tb, KV), lambda i: (i, O_V // KV)), one, one, tab, tab],
        out_specs=[pl.BlockSpec((tb, D), lambda i: (i, 0)), pl.BlockSpec((tb, KV), lambda i: (i, 0)),
                   pl.BlockSpec((tb, KV), lambda i: (i, 0))],
        out_shape=[jax.ShapeDtypeStruct((T, D), BF), jax.ShapeDtypeStruct((T, KV), BF), jax.ShapeDtypeStruct((T, KV), BF)],
    )(p, p, p, qn, kn, cos, sin)


def _attn_prep_bwd(p, qn, kn, cos, sin, dqr, dkp, dvp, dkc, dvc, nlat):
    T = p.shape[0]
    nqb = nlat // CB
    ncb = (T - nlat) // CB

    def body(q_ref, k_ref, v_ref, qn_ref, kn_ref, cos_ref, sin_ref, dqr_ref, dka_ref, dkb_ref, dkc3_ref, dva_ref, dvb_ref, dvc3_ref,
             dkctx_ref, dvctx_ref, dq_ref, dk_ref, dv_ref, dqn_ref, dkn_ref):
        i = pl.program_id(0)
        is_lat = i < nqb
        cos_v, sin_v = cos_ref[...], sin_ref[...]

        @pl.when(i == 0)
        def _():
            dqn_ref[...] = jnp.zeros_like(dqn_ref)
            dkn_ref[...] = jnp.zeros_like(dkn_ref)

        def band_sum(a_ref, b_ref, c_ref, ctx_ref):
            s = b_ref[0] + jnp.where(i > 0, a_ref[0], 0.0) + jnp.where(i < nqb - 1, c_ref[0], 0.0)
            return jnp.where(is_lat, s, ctx_ref[...])

        dkr = band_sum(dka_ref, dkb_ref, dkc3_ref, dkctx_ref)
        dv_ref[...] = band_sum(dva_ref, dvb_ref, dvc3_ref, dvctx_ref).astype(BF)
        post = lambda xs, w: _qk_post_fn(xs, w, cos_v, sin_v)
        _, vjp = jax.vjp(post, [q_ref[:, sl] for sl in _HEAD_SLICES], qn_ref[...])
        dqs, dqn = vjp([jnp.where(is_lat, dqr_ref[:, sl], 0.0) for sl in _HEAD_SLICES])
        for sl, dq in zip(_HEAD_SLICES, dqs):
            dq_ref[:, sl] = dq.astype(BF)
        dqn_ref[...] += dqn
        _, vjp = jax.vjp(post, [k_ref[:, sl] for sl in _HEAD_SLICES[:KVH]], kn_ref[...])
        dks, dkn = vjp([dkr[:, sl] for sl in _HEAD_SLICES[:KVH]])
        for sl, dk in zip(_HEAD_SLICES, dks):
            dk_ref[:, sl] = dk.astype(BF)
        dkn_ref[...] += dkn

    one = pl.BlockSpec((1, HD), lambda i: (0, 0))
    tab = pl.BlockSpec((CB, HD), lambda i: (i, 0))
    lat = lambda i: jnp.minimum(i, nqb - 1)

    def part(off, slot):
        return pl.BlockSpec((1, CB, KV), lambda i: (jnp.clip(lat(i) + off, 0, nqb - 1) * 3 + slot, 0, 0))

    ctxs = pl.BlockSpec((CB, KV), lambda i: (jnp.clip(i - nqb, 0, ncb - 1), 0))
    kvs = pl.BlockSpec((CB, KV), lambda i: (i, 0))
    return pl.pallas_call(
        body, grid=(T // CB,), name="attn_prep_bwd",
        in_specs=[pl.BlockSpec((CB, D), lambda i: (i, O_Q // D)), pl.BlockSpec((CB, KV), lambda i: (i, O_K // KV)),
                  pl.BlockSpec((CB, KV), lambda i: (i, O_V // KV)), one, one, tab, tab,
                  pl.BlockSpec((CB, D), lambda i: (lat(i), 0)),
                  part(-1, 2), part(0, 1), part(1, 0), part(-1, 2), part(0, 1), part(1, 0), ctxs, ctxs],
        out_specs=[pl.BlockSpec((CB, D), lambda i: (i, 0)), kvs, kvs, one, one],
        out_shape=[jax.ShapeDtypeStruct((T, D), BF), jax.ShapeDtypeStruct((T, KV), BF), jax.ShapeDtypeStruct((T, KV), BF),
                   jax.ShapeDtypeStruct((1, HD), F32), jax.ShapeDtypeStruct((1, HD), F32)],
    )(p, p, p, qn, kn, cos, sin, dqr, dkp, dkp, dkp, dvp, dvp, dvp, dkc, dvc)


def _attn_groups_fn(qs, kalls, valls, sinks, bias):
    groups = range(KVH)
    q = [jnp.concatenate(qs[GRP * g:GRP * (g + 1)], axis=0) for g in groups]
    s = [_dot_nt_bf(q[g], kalls[g]) * (HD ** -0.5) + bias for g in groups]
    sk = [jnp.concatenate([jnp.broadcast_to(jnp.mean(t, axis=1, keepdims=True), (CB, 1)) for t in sinks[GRP * g:GRP * (g + 1)]],
                          axis=0) for g in groups]
    m = [lax.stop_gradient(jnp.maximum(jnp.max(s[g], axis=1, keepdims=True), sk[g])) for g in groups]
    e = [jnp.exp(s[g] - m[g]) for g in groups]
    den = [jnp.sum(e[g], axis=1, keepdims=True) + jnp.exp(sk[g] - m[g]) for g in groups]
    return [_dot_bf(e[g] / den[g], valls[g]) for g in groups]


def _attn_bias(lc):
    r, c = _iota2((GRP * CB, 3 * CB + lc))
    rel = c - (r & (CB - 1))
    win = (rel >= 0) & (rel <= 2 * CB)
    ctx = c >= 3 * CB
    seen = [(win & (c >= CB)) | ctx, win | ctx, (win & (c < 2 * CB)) | ctx]
    return jnp.stack([jnp.where(s, 0.0, -1e30) for s in seen]).astype(F32)


def _attn_specs(nqb, lc, nlat):
    assert nqb >= 2
    qs = pl.BlockSpec((CB, D), lambda i: (i, 0))
    ka = pl.BlockSpec((CB, KV), lambda i: (jnp.maximum(i - 1, 0), 0))
    kb = pl.BlockSpec((CB, KV), lambda i: (i, 0))
    kc = pl.BlockSpec((CB, KV), lambda i: (jnp.minimum(i + 1, nqb - 1), 0))
    kx = pl.BlockSpec((lc, KV), lambda i: (nlat // lc, 0))
    sk = pl.BlockSpec((KVH, 8, 128), lambda i: (0, 0, 0))
    bs = pl.BlockSpec((1, GRP * CB, 3 * CB + lc), lambda i: (jnp.where(i == 0, 0, jnp.where(i == nqb - 1, 2, 1)), 0, 0))
    return qs, ka, kb, kc, kx, sk, bs


def _attn_operands(q_ref, k_refs, v_refs, sk_ref, dtype):
    sls = [slice(g * HD, (g + 1) * HD) for g in range(KVH)]
    kalls = [jnp.concatenate([r[:, sl] for r in k_refs], axis=0).astype(dtype) for sl in sls]
    valls = [jnp.concatenate([r[:, sl] for r in v_refs], axis=0).astype(dtype) for sl in sls]
    qs = [q_ref[:, sl].astype(dtype) for sl in _HEAD_SLICES]
    sinks = [sk_ref[h // GRP, (h % GRP):(h % GRP) + 1, :] for h in range(NH)]
    return qs, kalls, valls, sinks


def _attn_fwd(qr, kr, vb, sink, w_branch, nlat):
    lc = kr.shape[0] - nlat
    nqb = nlat // CB
    qs, ka, kb, kc, kx, sk, bs = _attn_specs(nqb, lc, nlat)

    def body(q_ref, ka_ref, kb_ref, kc_ref, kx_ref, va_ref, vb_ref, vc_ref, vx_ref, sk_ref, bias_ref, wb_ref, o_ref, z_ref):
        operands = _attn_operands(q_ref, (ka_ref, kb_ref, kc_ref, kx_ref), (va_ref, vb_ref, vc_ref, vx_ref), sk_ref, BF)
        outs = _attn_groups_fn(*operands, bias_ref[0])
        for h, sl in enumerate(_HEAD_SLICES):
            o_ref[:, sl] = outs[h // GRP][(h % GRP) * CB:(h % GRP + 1) * CB].astype(BF)
        z_ref[...] = jnp.dot(o_ref[...], wb_ref[...], preferred_element_type=F32)

    return pl.pallas_call(
        body, grid=(nqb,), name="attn_fwd",
        in_specs=[qs, ka, kb, kc, kx, ka, kb, kc, kx, sk, bs, _resident((D, D))], out_specs=[qs, qs],
        out_shape=[jax.ShapeDtypeStruct((nlat, D), BF), jax.ShapeDtypeStruct((nlat, D), F32)], compiler_params=_cp(),
    )(qr, kr, kr, kr, kr, vb, vb, vb, vb, sink, _attn_bias(lc), w_branch)


def _attn_bwd(qr, kr, vb, sink, dy, nlat):
    lc = kr.shape[0] - nlat
    nqb = nlat // CB
    qs, ka, kb, kc, kx, sk, bs = _attn_specs(nqb, lc, nlat)

    def body(q_ref, ka_ref, kb_ref, kc_ref, kx_ref, va_ref, vb_ref, vc_ref, vx_ref, sk_ref, dy_ref, bias_ref,
             dq_ref, dkp_ref, dvp_ref, dkx_ref, dvx_ref, dsk_ref):
        operands = _attn_operands(q_ref, (ka_ref, kb_ref, kc_ref, kx_ref), (va_ref, vb_ref, vc_ref, vx_ref), sk_ref, F32)
        _, vjp = jax.vjp(functools.partial(_attn_groups_fn, bias=bias_ref[0]), *operands)
        dys_g = [jnp.concatenate([dy_ref[:, sl] for sl in _HEAD_SLICES[GRP * g:GRP * (g + 1)]], axis=0) for g in range(KVH)]
        dqs, dks, dvs, dsinks = vjp(dys_g)

        @pl.when(pl.program_id(0) == 0)
        def _():
            dkx_ref[...] = jnp.zeros_like(dkx_ref)
            dvx_ref[...] = jnp.zeros_like(dvx_ref)
            dsk_ref[...] = jnp.zeros_like(dsk_ref)

        for h, sl in enumerate(_HEAD_SLICES):
            dq_ref[:, sl] = dqs[h]
            dsk_ref[h // GRP, (h % GRP):(h % GRP) + 1, :] += dsinks[h]
        for g in range(KVH):
            sl = slice(g * HD, (g + 1) * HD)
            for t in range(3):
                dkp_ref[t, :, sl] = dks[g][t * CB:(t + 1) * CB]
                dvp_ref[t, :, sl] = dvs[g][t * CB:(t + 1) * CB]
            dkx_ref[:, sl] += dks[g][3 * CB:]
            dvx_ref[:, sl] += dvs[g][3 * CB:]

    dys = qs
    parts = pl.BlockSpec((3, CB, KV), lambda i: (i, 0, 0))
    ctxo = pl.BlockSpec((lc, KV), lambda i: (0, 0))
    return pl.pallas_call(
        body, grid=(nqb,), name="attn_bwd",
        in_specs=[qs, ka, kb, kc, kx, ka, kb, kc, kx, sk, dys, bs],
        out_specs=[dys, parts, parts, ctxo, ctxo, sk],
        out_shape=[jax.ShapeDtypeStruct((nlat, D), F32), jax.ShapeDtypeStruct((3 * nqb, CB, KV), F32),
                   jax.ShapeDtypeStruct((3 * nqb, CB, KV), F32), jax.ShapeDtypeStruct((lc, KV), F32),
                   jax.ShapeDtypeStruct((lc, KV), F32), jax.ShapeDtypeStruct((KVH, 8, 128), F32)],
        compiler_params=_cp(),
    )(qr, kr, kr, kr, kr, vb, vb, vb, vb, sink, dy, _attn_bias(lc))


def _merge_fn(z_dn, z_at, g_dn, g_at):
    return jax.nn.sigmoid(g_dn) * z_dn + jax.nn.sigmoid(g_at) * z_at


def _merge_fwd(z_dn, z_at, p, w_out, nlat):
    tb = _tile(nlat, (512, 256, 128))

    def body(zd_ref, za_ref, gd_ref, ga_ref, wo_ref, o_ref, mix_ref):
        o_ref[...] = _merge_fn(zd_ref[...], za_ref[...], gd_ref[...], ga_ref[...]).astype(BF)
        mix_ref[...] = jnp.dot(o_ref[...], wo_ref[...], preferred_element_type=F32)

    row = pl.BlockSpec((tb, D), lambda i: (i, 0))
    return pl.pallas_call(
        body, grid=(nlat // tb,), name="merge_fwd",
        in_specs=[row, row, pl.BlockSpec((tb, D), lambda i: (i, O_MG // D)), pl.BlockSpec((tb, D), lambda i: (i, O_MG // D + 1)),
                  _resident((D, D))],
        out_specs=[row, row], out_shape=[jax.ShapeDtypeStruct((nlat, D), BF), jax.ShapeDtypeStruct((nlat, D), F32)],
        compiler_params=_cp(),
    )(z_dn, z_at, p, p, w_out)


def _merge_bwd(z_dn, z_at, p, dm, w_bdn, w_bat, nlat):
    T = p.shape[0]
    tb = _tile(nlat, (256, 128))
    nlb = nlat // tb

    def body(zd_ref, za_ref, gd_ref, ga_ref, dm_ref, wd_ref, wa_ref, dzd_ref, dza_ref, dg_ref, dyd_ref, dya_ref):
        is_lat = pl.program_id(0) < nlb
        _, vjp = jax.vjp(_merge_fn, zd_ref[...], za_ref[...], gd_ref[...], ga_ref[...])
        dzd, dza, dgd, dga = vjp(dm_ref[...])
        dzd_ref[...] = dzd.astype(BF)
        dza_ref[...] = dza.astype(BF)
        dg_ref[:, :D] = jnp.where(is_lat, dgd, 0.0).astype(BF)
        dg_ref[:, D:] = jnp.where(is_lat, dga, 0.0).astype(BF)
        dyd_ref[...] = lax.dot_general(dzd_ref[...], wd_ref[...], (_DIMS["nt"], ((), ())), preferred_element_type=F32)
        dya_ref[...] = lax.dot_general(dza_ref[...], wa_ref[...], (_DIMS["nt"], ((), ())), preferred_element_type=F32)

    lat = lambda i: jnp.minimum(i, nlb - 1)
    row = pl.BlockSpec((tb, D), lambda i: (lat(i), 0))
    return pl.pallas_call(
        body, grid=(T // tb,), name="merge_bwd",
        in_specs=[row, row, pl.BlockSpec((tb, D), lambda i: (lat(i), O_MG // D)),
                  pl.BlockSpec((tb, D), lambda i: (lat(i), O_MG // D + 1)), row, _resident((D, D)), _resident((D, D))],
        out_specs=[row, row, pl.BlockSpec((tb, 2 * D), lambda i: (i, 0)), row, row],
        out_shape=[jax.ShapeDtypeStruct((nlat, D), BF), jax.ShapeDtypeStruct((nlat, D), BF), jax.ShapeDtypeStruct((T, 2 * D), BF),
                   jax.ShapeDtypeStruct((nlat, D), F32), jax.ShapeDtypeStruct((nlat, D), F32)],
    )(z_dn, z_at, p, p, dm, w_bdn, w_bat)


def _swiglu_fn(ug, uv):
    return jax.nn.silu(ug) * uv


FFN_GROUP = 256


def _resident(shape):
    return pl.BlockSpec(shape, lambda i: (0,) * len(shape), pipeline_mode=pl.Buffered(1))


H_HALO = 16


def _up_project(h_refs, wu_ref, u_scr):
    cur_ref, prev_ref, next_ref = h_refs
    rows = jnp.concatenate([prev_ref[...], cur_ref[...], next_ref[...]], axis=0)
    u_scr[...] = jnp.dot(rows, wu_ref[...], preferred_element_type=F32)


def _up_ext_rows(u_scr, cols, keep, tb):
    xe = u_scr[H_HALO - HALO:H_HALO + tb + HALO, cols]
    r = lax.broadcasted_iota(jnp.int32, (tb + 2 * HALO, 1), 0)
    inside = ((r >= HALO) | keep[0]) & ((r < HALO + tb) | keep[1])
    return jnp.where(inside, xe, 0.0)


def _ffn_fwd(h, w_up, w8, bias, w_down):
    n = h.shape[0]
    tb = _tile(n, (256, 128))
    starts, ends = _segment_edges((n,), tb)

    def body(cur_ref, prev_ref, next_ref, wu_ref, w_ref, b_ref, wd_ref, u_ref, o_ref, ff_ref, u_scr):
        keep = _keep_halos(pl.program_id(0), starts, ends)
        _up_project((cur_ref, prev_ref, next_ref), wu_ref, u_scr)
        u_ref[...] = u_scr[H_HALO:H_HALO + tb, :]

        for c0 in range(0, DFF, FFN_GROUP):
            halves = []
            for cols in (slice(c0, c0 + FFN_GROUP), slice(DFF + c0, DFF + c0 + FFN_GROUP)):
                xe = _up_ext_rows(u_scr, cols, keep, tb)
                halves.append(_conv_rows(_shifted_rows(xe, FFN_TAPS), w_ref, cols)[HALO:HALO + tb] + b_ref[:, cols])
            o_ref[:, c0:c0 + FFN_GROUP] = _swiglu_fn(*halves).astype(BF)
        ff_ref[...] = jnp.dot(o_ref[...], wd_ref[...], preferred_element_type=F32)

    return pl.pallas_call(
        body, grid=(n // tb,), name="ffn_fwd",
        in_specs=_halo_specs(tb, D, n, halo=H_HALO) + [_resident((D, 2 * DFF)), pl.BlockSpec((8, 2 * DFF), lambda i: (0, 0)),
                                                        pl.BlockSpec((1, 2 * DFF), lambda i: (0, 0)), _resident((DFF, D))],
        out_specs=[pl.BlockSpec((tb, 2 * DFF), lambda i: (i, 0)), pl.BlockSpec((tb, DFF), lambda i: (i, 0)),
                   pl.BlockSpec((tb, D), lambda i: (i, 0))],
        out_shape=[jax.ShapeDtypeStruct((n, 2 * DFF), F32), jax.ShapeDtypeStruct((n, DFF), BF), jax.ShapeDtypeStruct((n, D), F32)],
        scratch_shapes=[pltpu.VMEM((tb + 2 * H_HALO, 2 * DFF), F32)],
        compiler_params=_cp(),
    )(h, h, h, w_up, w8, bias, w_down)


def _ffn_bwd(u, w_up, w8, bias, da):
    n = u.shape[0]
    tb = _tile(n, (256, 128))
    starts, ends = _segment_edges((n,), tb)

    def body(cur_ref, prev_ref, next_ref, wu_ref, w_ref, b_ref, da_c, da_p, da_n, du_ref, dw_ref, db_ref, dh_ref):
        i = pl.program_id(0)
        keep = _keep_halos(i, starts, ends)

        @pl.when(i == 0)
        def _():
            dw_ref[...] = jnp.zeros_like(dw_ref)
            db_ref[...] = jnp.zeros_like(db_ref)

        for c0 in range(0, DFF, FFN_GROUP):
            col_pair = (slice(c0, c0 + FFN_GROUP), slice(DFF + c0, DFF + c0 + FFN_GROUP))
            shifts = [_shifted_rows(_ext_rows((cur_ref, prev_ref, next_ref), cols, keep), FFN_TAPS) for cols in col_pair]
            convs = [_conv_rows(shifted, w_ref, cols) + b_ref[:, cols] for shifted, cols in zip(shifts, col_pair)]
            dae = _ext_rows((da_c, da_p, da_n), col_pair[0], keep)
            _, vjp = jax.vjp(_swiglu_fn, *convs)
            for shifted, cols, dce in zip(shifts, col_pair, vjp(dae)):
                du_ref[:, cols] = _conv_rows(_shifted_rows(dce, FFN_TAPS, transpose=True), w_ref, cols)[HALO:HALO + tb].astype(BF)
                dcur = dce[HALO:HALO + tb]
                for j, g in enumerate(_tap_grads(dcur, shifted, tb)):
                    dw_ref[j:j + 1, cols] += g
                db_ref[:, cols] += jnp.sum(dcur, axis=0, keepdims=True)
        dh_ref[...] = lax.dot_general(du_ref[...], wu_ref[...], (_DIMS["nt"], ((), ())), preferred_element_type=F32)

    wspec = pl.BlockSpec((8, 2 * DFF), lambda i: (0, 0))
    bspec = pl.BlockSpec((1, 2 * DFF), lambda i: (0, 0))
    return pl.pallas_call(
        body, grid=(n // tb,), name="ffn_bwd",
        in_specs=_halo_specs(tb, 2 * DFF, n) + [_resident((D, 2 * DFF)), wspec, bspec] + _halo_specs(tb, DFF, n),
        out_specs=[pl.BlockSpec((tb, 2 * DFF), lambda i: (i, 0)), wspec, bspec, pl.BlockSpec((tb, D), lambda i: (i, 0))],
        out_shape=[jax.ShapeDtypeStruct((n, 2 * DFF), BF), jax.ShapeDtypeStruct((8, 2 * DFF), F32), jax.ShapeDtypeStruct((1, 2 * DFF), F32),
                   jax.ShapeDtypeStruct((n, D), F32)],
        compiler_params=_cp(),
    )(u, u, u, w_up, w8, bias, da, da, da)


def _loss_kernel(x1, gate, ff, target, w_down):
    n = x1.shape[0]
    tb = _tile(n, (512, 256, 128))

    def body(x_ref, g_ref, f_ref, t_ref, wd_ref, loss_ref, dy_ref, dff_ref, dg_ref, da_ref):
        err = x_ref[...] + g_ref[...] * f_ref[...] - t_ref[...]
        dy = err * (1.0 / D)
        dy_ref[...] = dy
        dff_ref[...] = (g_ref[...] * dy).astype(BF)
        da_ref[...] = lax.dot_general(dff_ref[...], wd_ref[...], (_DIMS["nt"], ((), ())), preferred_element_type=F32)

        @pl.when(pl.program_id(0) == 0)
        def _():
            loss_ref[...] = jnp.zeros_like(loss_ref)
            dg_ref[...] = jnp.zeros_like(dg_ref)
        part = 0.5 * jnp.sum(jnp.sum(err * err, axis=1, keepdims=True) * (1.0 / D), axis=0, keepdims=True)
        loss_ref[...] += jnp.broadcast_to(part, (1, 128))
        dg_ref[...] += jnp.sum(dy * f_ref[...], axis=0, keepdims=True)

    row = pl.BlockSpec((tb, D), lambda i: (i, 0))
    one = pl.BlockSpec((1, D), lambda i: (0, 0))
    return pl.pallas_call(
        body, grid=(n // tb,), name="loss",
        in_specs=[row, one, row, row, _resident((DFF, D))],
        out_specs=[pl.BlockSpec((1, 128), lambda i: (0, 0)), row, row, one, pl.BlockSpec((tb, DFF), lambda i: (i, 0))],
        out_shape=[jax.ShapeDtypeStruct((1, 128), F32), jax.ShapeDtypeStruct((n, D), F32),
                   jax.ShapeDtypeStruct((n, D), BF), jax.ShapeDtypeStruct((1, D), F32), jax.ShapeDtypeStruct((n, DFF), F32)],
        compiler_params=_cp(),
    )(x1, gate, ff, target, w_down)


def _rope_tables(nlat, lc):
    inv_freq = (np.float32(ROPE_BASE) ** (-np.arange(32, dtype=np.float32) / np.float32(32))).astype(np.float32)
    ar = np.arange(nlat // GRID_W, dtype=np.float32)[:, None] * inv_freq
    ac = np.arange(GRID_W, dtype=np.float32)[:, None] * inv_freq
    by_row = lambda a: jnp.repeat(jnp.asarray(a, F32), GRID_W, axis=0)
    by_col = lambda a: jnp.tile(jnp.asarray(a, F32), (nlat // GRID_W, 1))
    cos = jnp.concatenate([by_row(np.cos(ar)), by_row(np.cos(ar)), by_col(np.cos(ac)), by_col(np.cos(ac))], axis=1)
    sin = jnp.concatenate([by_row(-np.sin(ar)), by_row(np.sin(ar)), by_col(-np.sin(ac)), by_col(np.sin(ac))], axis=1)
    cos = jnp.concatenate([cos, jnp.ones((lc, HD), F32)], axis=0)
    sin = jnp.concatenate([sin, jnp.zeros((lc, HD), F32)], axis=0)
    return cos, sin


def _pad_rows8(w):
    return jnp.concatenate([w, jnp.zeros((8 - w.shape[0], w.shape[1]), w.dtype)], axis=0)


def _pack_w_in(w):
    cuts = [sum(IN_SIZES[:i]) for i in range(len(IN_SIZES) + 1)]
    qkv, gt, b, a, q, k, v, mg = [w[:, cuts[i]:cuts[i + 1]] for i in range(len(IN_SIZES))]
    return jnp.concatenate([qkv, gt, q, mg, k, v, b, a, jnp.zeros((w.shape[0], PW - O_BA - 32), w.dtype)], axis=1)


def _unpack_w_in(g):
    return jnp.concatenate([g[:, O_QKV:O_GT], g[:, O_GT:O_Q], g[:, O_BA:O_BA + 32], g[:, O_Q:O_MG], g[:, O_K:O_V],
                            g[:, O_V:O_BA], g[:, O_MG:O_K]], axis=1)


def _local_step(x, ctx, mod_x, mod_c, target, project_in, project_back,
                norm_mix, norm_ffn, dn_conv, a_log, dt_bias, dn_norm, q_norm, k_norm, sink, ffn_conv, ffn_conv_b):
    L, LC = x.shape[0], ctx.shape[0]
    T = L + LC
    seg = lambda r: jnp.stack([mod_x[r], mod_c[r]])[:, None, :]
    sh_a, sc_a = seg(0), seg(1)
    g_a, g_f = mod_x[2][None], mod_x[5][None]
    sh_f, sc_f = mod_x[3][None], mod_x[4][None]
    cos, sin = _rope_tables(L, LC)
    dnc8 = _pad_rows8(dn_conv)
    ffc8 = _pad_rows8(ffn_conv)
    gate_row = lambda a: jnp.concatenate([jnp.zeros((1, 16), F32), a.reshape(1, 16), jnp.zeros((1, 96), F32)], axis=1)
    alog_row, dt_row = gate_row(a_log), gate_row(dt_bias)
    sinkb = jnp.concatenate([jnp.broadcast_to(sink.reshape(KVH, GRP, 1), (KVH, GRP, 128)), jnp.zeros((KVH, 8 - GRP, 128), F32)], axis=1)

    h1 = _norm_mod_fwd(x, ctx, norm_mix, sh_a, sc_a, "norm_mix_fwd")
    p, (w_in_p, w_bdn, w_bat, w_out, w_up, w_down) = project_in(h1)
    q, k, v, gb = _dn_pre_fwd(p, dnc8, alog_row, dt_row, (L, LC))
    wy = _dn1_fwd(q, k, v, gb)
    scans = _dn2_fwd([t[:6] for t in wy], L)
    o_dir = [s[0] for s in scans]
    y_dn, z_dn = _ghn_fwd(o_dir[0], o_dir[1], p, dn_norm, w_bdn, L)
    qr, kr, vb = _attn_prep_fwd(p, q_norm, k_norm, cos, sin)
    y_at, z_at = _attn_fwd(qr, kr, vb, sinkb, w_bat, L)
    merged, mix = _merge_fwd(z_dn, z_at, p, w_out, L)
    x1, h2 = _resid_norm_fwd(x, g_a, mix, norm_ffn, sh_f, sc_f)
    u_raw, act, ff = _ffn_fwd(h2, w_up, ffc8, ffn_conv_b, w_down)
    loss_row, dy, dff, dg_f, dact = _loss_kernel(x1, g_f, ff, target, w_down)

    g_down = _mm(act, dff, form="tn", out_dtype=BF, name="g_ffn_down")
    du_raw, g_ffc8, g_ffb, dh2 = _ffn_bwd(u_raw, w_up, ffc8, ffn_conv_b, dact)
    g_up = _mm(h2, du_raw, form="tn", out_dtype=BF, name="g_ffn_up")
    dx1, dmix, dg_a, g_nffn, dsh_f, dsc_f, dmerged = _resid_norm_bwd(x1, g_a, mix, norm_ffn, sh_f, sc_f, dh2, dy, w_out)

    g_out = _mm(merged, dmix, form="tn", out_dtype=BF, name="g_w_out")
    dz_dn, dz_at, dmg, dy_dn, dy_at = _merge_bwd(z_dn, z_at, p, dmerged, w_bdn, w_bat, L)
    g_bdn = _mm(y_dn, dz_dn, form="tn", out_dtype=BF, name="g_branch_dn")
    g_bat = _mm(y_at, dz_at, form="tn", out_dtype=BF, name="g_branch_at")
    dqr, dkp, dvp, dkx, dvx, dsink = _attn_bwd(qr, kr, vb, sinkb, dy_at, L)
    dq_raw, dk_raw, dv_raw, g_qn, g_kn = _attn_prep_bwd(p, q_norm, k_norm, cos, sin, dqr, dkp, dvp, dkx, dvx, L)
    do, dgt, g_dnn = _ghn_bwd(o_dir[0], o_dir[1], p, dn_norm, dy_dn, L)
    cots = _dn2_bwd([wy[d][:6] + (scans[d][1],) for d in (0, 1)], do, L)
    dq, dk, dv, dgb = _dn1_bwd(q, k, v, gb, [t[6] for t in wy], cots)
    dp, g_dnc8, g_alog, g_dt = _dn_pre_bwd(p, dnc8, alog_row, dt_row, dq, dk, dv, dgb, (dgt, dq_raw, dmg, dk_raw, dv_raw), (L, LC))
    big, dh1 = project_back(h1, dp, w_in_p, (g_bdn, g_bat, g_out, g_up, g_down))
    grad_x, g_nmix_x, dsh_a, dsc_a = _norm_mod_bwd(x, norm_mix, mod_x[0][None], mod_x[1][None], dh1, row0=0,
                                                   name="norm_mix_bwd", residual=dx1)
    g_nmix_c, dsh_c, dsc_c = _norm_mod_bwd(ctx, norm_mix, mod_c[0][None], mod_c[1][None], dh1, row0=L, name="norm_mix_bwd_ctx")
    g_nmix = g_nmix_x + g_nmix_c

    zero = jnp.zeros((D,), F32)
    dmod_x = jnp.stack([dsh_a[0], dsc_a[0], dg_a[0], dsh_f[0], dsc_f[0], dg_f[0]])
    dmod_c = jnp.stack([dsh_c[0], dsc_c[0], zero, zero, zero, zero])
    small = dict(
        dmod_x=dmod_x, dmod_c=dmod_c, norm_mix=g_nmix, norm_ffn=g_nffn, dn_conv=g_dnc8[:5], dn_a_log=g_alog[0, 16:32].reshape(2, 8),
        dn_dt_bias=g_dt[0, 16:32].reshape(2, 8), dn_norm=g_dnn, q_norm=g_qn, k_norm=g_kn,
        attn_sink=jnp.sum(dsink[:, :GRP, :], axis=2).reshape(1, NH), ffn_conv=g_ffc8[:3], ffn_conv_b=g_ffb)
    return loss_row[0, 0], grad_x, big, small


def _exchange(arrays, scatter, name):
    n = len(arrays)

    def body(*refs):
        args = (refs[:n], refs[n:2 * n], *refs[2 * n:], scatter)
        _exchange_start(*args)
        _exchange_wait(*args)

    hbm = pl.BlockSpec(memory_space=pl.ANY)
    out_shape, sems = _exchange_shapes(arrays, scatter)
    return pl.pallas_call(body, name=name, in_specs=[hbm] * n, out_specs=[hbm] * n, out_shape=out_shape,
                          scratch_shapes=sems)(*arrays)


def _gather_two_level(arrays, name):
    n = len(arrays)

    def body(*refs):
        ins, outs = refs[:n], refs[n:2 * n]
        send_sems, recv_sems, local_sems = refs[2 * n:]
        x, y, c = lax.axis_index("x"), lax.axis_index("y"), lax.axis_index("c")
        sibling = (x, y, 1 - c)
        chips = [(1 - x, y), (x, 1 - y), (1 - x, 1 - y)]

        def copy(k, j, block, to, src=None):
            slot = outs[k].at[4 * block[0] + 2 * block[1] + block[2]]
            return pltpu.make_async_remote_copy(src_ref=slot if src is None else src, dst_ref=slot,
                                                send_sem=send_sems.at[7 * k + j], recv_sem=recv_sems.at[7 * k + j],
                                                device_id=to, device_id_type=MESH)

        mine = [pltpu.make_async_copy(ins[k], outs[k].at[4 * x + 2 * y + c], local_sems.at[k]) for k in range(n)]
        for cp in mine:
            cp.start()
        first = []
        for k in range(n):
            first.append(copy(k, 0, (x, y, c), sibling, src=ins[k]))
            first += [copy(k, 1 + j, (x, y, c), (*chip, c), src=ins[k]) for j, chip in enumerate(chips)]
        for cp in first:
            cp.start()
        passed = []
        for k in range(n):
            for j, chip in enumerate(chips):
                copy(k, 1 + j, (*chip, c), (x, y, c)).wait_recv()
                forward = copy(k, 4 + j, (*chip, c), sibling)
                forward.start()
                passed.append(forward)
        for k in range(n):
            copy(k, 0, sibling, (x, y, c)).wait_recv()
            for j, chip in enumerate(chips):
                copy(k, 4 + j, (*chip, 1 - c), (x, y, c)).wait_recv()
        for cp in first + passed:
            cp.wait_send()
        for cp in mine:
            cp.wait()

    hbm = pl.BlockSpec(memory_space=pl.ANY)
    out_shape, sems = _exchange_shapes(arrays, False)
    return pl.pallas_call(body, name=name, in_specs=[hbm] * n, out_specs=[hbm] * n, out_shape=out_shape,
                          scratch_shapes=sems)(*arrays)


def _ada_fwd(c16, w_ada, b_ada):
    def body(c_ref, w_ref, b_ref, o_ref):
        o_ref[...] = _dot_hi(jax.nn.silu(c_ref[...]), w_ref[...]) + b_ref[...]

    return pl.pallas_call(body, name="ada_fwd", out_shape=jax.ShapeDtypeStruct((16, w_ada.shape[1]), F32))(c16, w_ada, b_ada)


def _ada_bwd(c16, w_ada, dmx, dmc):
    def body(c_ref, w_ref, dmx_ref, dmc_ref, gw_ref, pc_ref):
        dmc_tot = dmc_ref[0:1, :]
        for d in range(1, N_DEV):
            dmc_tot = dmc_tot + dmc_ref[d:d + 1, :]
        dm16 = jnp.concatenate([dmx_ref[...], jnp.broadcast_to(dmc_tot, (8, dmc_tot.shape[1]))], axis=0)
        row = lax.broadcasted_iota(jnp.int32, dm16.shape, 0)
        dm16 = jnp.where(row <= 8, dm16, 0.0)
        s = jax.nn.silu(c_ref[...])
        gw_ref[...] = lax.dot_general(s, dm16, (_DIMS["tn"], ((), ())), precision=HI, preferred_element_type=F32)
        pc = lax.dot_general(dm16, w_ref[...], (_DIMS["nt"], ((), ())), precision=HI, preferred_element_type=F32)
        pc_ref[...] = pc[8:9, :]

    return pl.pallas_call(body, name="ada_bwd", out_shape=[jax.ShapeDtypeStruct(w_ada.shape, F32), jax.ShapeDtypeStruct((1, D), F32)],
                          compiler_params=_cp())(c16, w_ada, dmx, dmc)


def _cctx_grad(pc_all, c_ctx_row):
    def body(pc_ref, c_ref, g_ref):
        tot = pc_ref[0]
        for d in range(1, N_DEV):
            tot = tot + pc_ref[d]
        _, vjp = jax.vjp(jax.nn.silu, c_ref[...])
        g_ref[...] = vjp(tot)[0]

    return pl.pallas_call(body, name="cctx_grad", out_shape=jax.ShapeDtypeStruct((1, D), F32))(pc_all, c_ctx_row)


def _adamw(parts, w, m, v, name):
    ns, R, C = parts.shape
    tb = _tile(R, (128, 64, 32, 16, 8))

    def body(p_ref, w_ref, m_ref, v_ref, g_ref, d_ref, mo_ref, vo_ref):
        g = p_ref[0].astype(F32)
        for s in range(1, ns):
            g = g + p_ref[s].astype(F32)
        m2 = ADAM_B1 * m_ref[...] + (1.0 - ADAM_B1) * g
        v2 = ADAM_B2 * v_ref[...] + (1.0 - ADAM_B2) * jnp.square(g)
        m_hat = m2 / (1.0 - ADAM_B1 ** ADAM_STEP)
        v_hat = v2 / (1.0 - ADAM_B2 ** ADAM_STEP)
        g_ref[...] = g
        d_ref[...] = -ADAM_LR * (m_hat / (jnp.sqrt(v_hat) + ADAM_EPS) + ADAM_WD * w_ref[...])
        mo_ref[...] = m2
        vo_ref[...] = v2

    row = pl.BlockSpec((tb, C), lambda i: (i, 0))
    return pl.pallas_call(
        body, grid=(R // tb,), name=name,
        in_specs=[pl.BlockSpec((ns, tb, C), lambda i: (0, i, 0)), row, row, row], out_specs=[row] * 4,
        out_shape=[jax.ShapeDtypeStruct((R, C), F32)] * 4, compiler_params=_cp(),
    )(parts, w, m, v)


_SMALL = (("dmod_x", 6 * D), ("dmod_c", 6 * D), ("b_ada", 6 * D), ("norm_mix", D), ("norm_ffn", D), ("dn_a_log", 16),
          ("dn_dt_bias", 16), ("dn_norm", HD), ("q_norm", HD), ("k_norm", HD), ("attn_sink", NH), ("ffn_conv_b", 2 * DFF),
          ("dn_conv", 5 * 3 * D), ("ffn_conv", 3 * 2 * DFF))
_SMALL_ROWS = -(-sum(n for _, n in _SMALL) // 1024) * 8


def _pack_small(d):
    flat = jnp.concatenate([d[k].reshape(-1).astype(F32) if k in d else jnp.zeros((n,), F32) for k, n in _SMALL])
    return jnp.concatenate([flat, jnp.zeros((_SMALL_ROWS * 128 - flat.shape[0],), F32)]).reshape(_SMALL_ROWS, 128)


def _unpack_small(a):
    flat = a.reshape(a.shape[:-2] + (-1,))
    out, off = {}, 0
    for k, n in _SMALL:
        out[k] = flat[..., off:off + n]
        off += n
    return out


def kernel(x, c, ctx, c_ctx, w_ada, b_ada, norm_mix, norm_ffn, w_in, dn_conv, dn_a_log, dn_dt_bias, dn_norm, q_norm, k_norm, attn_sink, w_branch_dn, w_branch_attn, w_out, ffn_up, ffn_conv, ffn_conv_b, ffn_down, loss_target, m_c_ctx, m_w_ada, m_b_ada, m_norm_mix, m_norm_ffn, m_w_in, m_dn_conv, m_dn_a_log, m_dn_dt_bias, m_dn_norm, m_q_norm, m_k_norm, m_attn_sink, m_w_branch_dn, m_w_branch_attn, m_w_out, m_ffn_up, m_ffn_conv, m_ffn_conv_b, m_ffn_down, v_c_ctx, v_w_ada, v_b_ada, v_norm_mix, v_norm_ffn, v_w_in, v_dn_conv, v_dn_a_log, v_dn_dt_bias, v_dn_norm, v_q_norm, v_k_norm, v_attn_sink, v_w_branch_dn, v_w_branch_attn, v_w_out, v_ffn_up, v_ffn_conv, v_ffn_conv_b, v_ffn_down):
    me = 4 * lax.axis_index("x") + 2 * lax.axis_index("y") + lax.axis_index("c")
    ada_cols = w_ada.shape[2]

    cols = lambda a: jnp.swapaxes(a, 0, 1).reshape(a.shape[1], -1)
    rows = lambda a: a.reshape(-1, a.shape[2])
    col_blocks = lambda g: jnp.swapaxes(g.reshape(g.shape[0], N_DEV, -1), 0, 1)
    row_blocks = lambda g: g.reshape(N_DEV, -1, g.shape[1])

    gathered = _gather_two_level([w_in[0].astype(BF), c, dn_conv[0], ffn_conv[0]], name="gather_first")
    w_in_packed = _pack_w_in(cols(gathered[0]))
    c_all = gathered[1][:, 0, :]

    def project_in(h1):
        p, rest = _mm(h1, w_in_packed, form="nn", out_dtype=F32, name="in_proj",
                      exchange=([w_branch_dn[0].astype(BF), w_branch_attn[0].astype(BF), w_out[0].astype(BF),
                                 ffn_up[0].astype(BF), ffn_down[0].astype(BF)], False))
        return p, (w_in_packed, rows(rest[0]), rows(rest[1]), rows(rest[2]), cols(rest[3]), rows(rest[4]))

    def project_back(h1, dp, w_in_p, grads):
        g_bdn, g_bat, g_out, g_up, g_down = grads
        g_in, landed_rest = _mm(h1, dp, form="tn", out_dtype=BF, name="g_w_in",
                                exchange=([row_blocks(g_bdn), row_blocks(g_bat), row_blocks(g_out), col_blocks(g_up),
                                           row_blocks(g_down)], True))
        dh1, landed_in = _mm(dp, w_in_p, form="nt", out_dtype=F32, name="d_h1",
                             exchange=([col_blocks(_unpack_w_in(g_in))], True))
        return [landed_in[0]] + landed_rest, dh1

    c16 = jnp.concatenate([c_all, c_ctx[None], jnp.zeros((7, D), F32)], axis=0)
    b_loc = lax.dynamic_slice_in_dim(b_ada, me * ada_cols, ada_cols, axis=1)
    mod_part = _ada_fwd(c16, w_ada[0], b_loc)
    mod_all = cols(_exchange([mod_part], scatter=False, name="gather_mod")[0])
    mod_x = lax.dynamic_slice_in_dim(mod_all, me, 1, axis=0).reshape(6, D)
    mod_c = mod_all[8].reshape(6, D)

    loss_loc, grad_x, landed, small = _local_step(
        x[0], ctx[0], mod_x, mod_c, loss_target[0], project_in, project_back,
        norm_mix, norm_ffn, cols(gathered[2]), dn_a_log[0], dn_dt_bias[0], dn_norm, q_norm, k_norm, attn_sink[0], cols(gathered[3]),
        ffn_conv_b)
    loss = lax.psum(loss_loc, ("x", "y", "c"))

    res = {}
    res["w_in"] = _adamw(landed[0], w_in[0], m_w_in[0], v_w_in[0], "adamw_w_in")
    res["w_branch_dn"] = _adamw(landed[1], w_branch_dn[0], m_w_branch_dn[0], v_w_branch_dn[0], "adamw_w_branch_dn")
    res["w_branch_attn"] = _adamw(landed[2], w_branch_attn[0], m_w_branch_attn[0], v_w_branch_attn[0], "adamw_w_branch_attn")
    res["w_out"] = _adamw(landed[3], w_out[0], m_w_out[0], v_w_out[0], "adamw_w_out")
    res["ffn_up"] = _adamw(landed[4], ffn_up[0], m_ffn_up[0], v_ffn_up[0], "adamw_ffn_up")
    res["ffn_down"] = _adamw(landed[5], ffn_down[0], m_ffn_down[0], v_ffn_down[0], "adamw_ffn_down")

    small = dict(small)
    small["b_ada"] = small["dmod_x"] + small["dmod_c"]
    parts = _exchange([_pack_small(small)], scatter=False, name="gather_small")[0]
    per_dev = _unpack_small(parts)
    given = dict(b_ada=(b_ada, m_b_ada, v_b_ada), norm_mix=(norm_mix, m_norm_mix, v_norm_mix), norm_ffn=(norm_ffn, m_norm_ffn, v_norm_ffn),
                 dn_a_log=(dn_a_log, m_dn_a_log, v_dn_a_log), dn_dt_bias=(dn_dt_bias, m_dn_dt_bias, v_dn_dt_bias),
                 dn_norm=(dn_norm, m_dn_norm, v_dn_norm), q_norm=(q_norm, m_q_norm, v_q_norm), k_norm=(k_norm, m_k_norm, v_k_norm),
                 attn_sink=(attn_sink, m_attn_sink, v_attn_sink), ffn_conv_b=(ffn_conv_b, m_ffn_conv_b, v_ffn_conv_b))
    packs = [_pack_small({k: t[j] for k, t in given.items()}) for j in range(3)]
    upd = [_unpack_small(a) for a in _adamw(parts, packs[0], packs[1], packs[2], "adamw_small")]
    for k, t in given.items():
        res[k] = tuple(u[k].reshape(t[0].shape) for u in upd)
    dnc = lax.dynamic_slice_in_dim(upd[0]["dn_conv"].reshape(5, 3 * D), me * dn_conv.shape[2], dn_conv.shape[2], axis=1)
    ffc = lax.dynamic_slice_in_dim(upd[0]["ffn_conv"].reshape(3, 2 * DFF), me * ffn_conv.shape[2], ffn_conv.shape[2], axis=1)
    r8 = lambda a: _pad_rows8(a)
    t = _adamw(r8(dnc)[None], r8(dn_conv[0]), r8(m_dn_conv[0]), r8(v_dn_conv[0]), "adamw_dn_conv")
    res["dn_conv"] = tuple(a[:5][None] for a in t)
    t = _adamw(r8(ffc)[None], r8(ffn_conv[0]), r8(m_ffn_conv[0]), r8(v_ffn_conv[0]), "adamw_ffn_conv")
    res["ffn_conv"] = tuple(a[:3][None] for a in t)

    dmx = lax.dynamic_slice_in_dim(per_dev["dmod_x"], me * ada_cols, ada_cols, axis=1)
    dmc = lax.dynamic_slice_in_dim(per_dev["dmod_c"], me * ada_cols, ada_cols, axis=1)
    g_ada, pc = _ada_bwd(c16, w_ada[0], dmx, dmc)
    res["w_ada"] = _adamw(g_ada[None], w_ada[0], m_w_ada[0], v_w_ada[0], "adamw_w_ada")
    pc_all = _exchange([pc], scatter=False, name="gather_cctx")[0]
    g_cctx = _cctx_grad(pc_all, c_ctx[None])
    r8b = lambda a: jnp.broadcast_to(a, (8, D))
    t = _adamw(r8b(g_cctx)[None], r8b(c_ctx[None]), r8b(m_c_ctx[None]), r8b(v_c_ctx[None]), "adamw_c_ctx")
    res["c_ctx"] = tuple(a[0] for a in t)

    names = ("c_ctx", "w_ada", "b_ada", "norm_mix", "norm_ffn", "w_in", "dn_conv", "dn_a_log", "dn_dt_bias", "dn_norm", "q_norm",
             "k_norm", "attn_sink", "w_branch_dn", "w_branch_attn", "w_out", "ffn_up", "ffn_conv", "ffn_conv_b", "ffn_down")
    lead = ("w_ada", "w_in", "w_branch_dn", "w_branch_attn", "w_out", "ffn_up", "ffn_down")
    fix = lambda k, a: a[None] if k in lead else a
    outs = [loss, grad_x[None]]
    for j in range(4):
        outs += [fix(k, res[k][j]) for k in names]
    return tuple(outs)
```

```python
import functools

import jax
import jax.numpy as jnp
import numpy as np
from jax import lax
from jax.experimental import pallas as pl
from jax.experimental.pallas import tpu as pltpu

F32 = jnp.float32
BF = jnp.bfloat16
HI = lax.Precision.HIGHEST
MESH = pl.DeviceIdType.MESH

D = 1024
NH = 8
HD = 128
KVH = 2
GRP = 4
KV = KVH * HD
DFF = 2816
CB = 128
GRID_W = 64
ROPE_BASE = 10000.0
EPS = 1e-6
N_DEV = 8
PW = 8192
O_QKV, O_GT, O_Q, O_MG, O_K, O_V, O_BA = 0, 3072, 4096, 5120, 7168, 7424, 7680
IN_SIZES = (3072, 1024, 16, 16, 1024, 256, 256, 2048)
IN_DIM = sum(IN_SIZES)
ADAM_LR, ADAM_B1, ADAM_B2, ADAM_EPS, ADAM_WD, ADAM_STEP = 0.001, 0.9, 0.999, 1e-08, 0.01, 10
VMEM_LIMIT = 56 * 1024 * 1024


def _cp():
    return pltpu.CompilerParams(vmem_limit_bytes=VMEM_LIMIT)


def _tile(n, cands):
    for c in cands:
        if n % c == 0:
            return c
    return n


def _iota2(shape):
    return lax.broadcasted_iota(jnp.int32, shape, 0), lax.broadcasted_iota(jnp.int32, shape, 1)


_DIMS = {"nn": ((1,), (0,)), "nt": ((1,), (1,)), "tn": ((0,), (0,))}


def _exchange_copies(ins, outs, send_sems, recv_sems, local_sems, scatter, landings):
    x, y, c = lax.axis_index("x"), lax.axis_index("y"), lax.axis_index("c")
    me = 4 * x + 2 * y + c
    local, remote = [], []
    for k in range(len(ins)):
        local.append(pltpu.make_async_copy(ins[k].at[me] if scatter else ins[k], outs[k].at[me], local_sems.at[k]))
        for m in range(1, N_DEV):
            px = 1 - x if m & 4 else x
            py = 1 - y if m & 2 else y
            pc = 1 - c if m & 1 else c
            peer = 4 * px + 2 * py + pc
            src = ins[k].at[peer] if scatter else ins[k]
            sem = k * (N_DEV - 1) + m - 1
            push = pltpu.make_async_remote_copy(src_ref=src, dst_ref=outs[k].at[me], send_sem=send_sems.at[sem],
                                                recv_sem=recv_sems.at[sem], device_id=(px, py, pc), device_id_type=MESH)
            landing = None
            if landings:
                landing = pltpu.make_async_remote_copy(src_ref=src, dst_ref=outs[k].at[peer], send_sem=send_sems.at[sem],
                                                       recv_sem=recv_sems.at[sem], device_id=(px, py, pc), device_id_type=MESH)
            remote.append((push, landing))
    return local, remote


def _exchange_start(*args):
    local, remote = _exchange_copies(*args, landings=False)
    for cp in local:
        cp.start()
    for push, _ in remote:
        push.start()


def _exchange_wait(*args):
    local, remote = _exchange_copies(*args, landings=True)
    for _, landing in remote:
        landing.wait_recv()
    for push, _ in remote:
        push.wait_send()
    for cp in local:
        cp.wait()


def _exchange_shapes(arrays, scatter):
    out_shape = [jax.ShapeDtypeStruct(a.shape if scatter else (N_DEV,) + a.shape, a.dtype) for a in arrays]
    n = len(arrays)
    sems = [pltpu.SemaphoreType.DMA((n * (N_DEV - 1),)), pltpu.SemaphoreType.DMA((n * (N_DEV - 1),)), pltpu.SemaphoreType.DMA((n,))]
    return out_shape, sems


def _mm(a, b, *, form, out_dtype, name, tm=None, tn=None, tk=None, exchange=None):
    if form == "tn":
        K, M = a.shape
        N = b.shape[1]
    else:
        M, K = a.shape
        N = b.shape[0] if form == "nt" else b.shape[1]
    tm = tm or _tile(M, (1408, 1280, 1024, 640, 512, 256, 128))
    tn = tn or _tile(N, (1408, 1024, 512, 256, 128))
    tk = tk or _tile(K, (2048, 1408, 1280, 1024, 640, 512, 256, 128))
    ni, nj, nk = M // tm, N // tn, K // tk
    dims = (_DIMS[form], ((), ()))
    ex_arrays, scatter = exchange if exchange else ([], False)
    nx = len(ex_arrays)

    def body(a_ref, b_ref, *refs):
        ex_in, o_ref, ex_out, scratch = refs[:nx], refs[nx], refs[nx + 1:2 * nx + 1], refs[2 * nx + 1:]
        i, j, k = pl.program_id(0), pl.program_id(1), pl.program_id(2)
        if nx:
            sems = scratch[-3:]

            @pl.when((i == 0) & (j == 0) & (k == 0))
            def _():
                _exchange_start(ex_in, ex_out, *sems, scatter)

        part = lax.dot_general(a_ref[...].astype(BF), b_ref[...].astype(BF), dims, preferred_element_type=F32)
        if nk == 1:
            o_ref[...] = part.astype(out_dtype)
        else:
            acc_ref = scratch[0]

            @pl.when(k == 0)
            def _():
                acc_ref[...] = part

            @pl.when(k > 0)
            def _():
                acc_ref[...] += part

            @pl.when(k == nk - 1)
            def _():
                o_ref[...] = acc_ref[...].astype(out_dtype)

        if nx:
            @pl.when((i == ni - 1) & (j == nj - 1) & (k == nk - 1))
            def _():
                _exchange_wait(ex_in, ex_out, *sems, scatter)

    if form == "tn":
        a_spec = pl.BlockSpec((tk, tm), lambda i, j, k: (k, i))
    else:
        a_spec = pl.BlockSpec((tm, tk), lambda i, j, k: (i, k))
    if form == "nt":
        b_spec = pl.BlockSpec((tn, tk), lambda i, j, k: (j, k))
    else:
        b_spec = pl.BlockSpec((tk, tn), lambda i, j, k: (k, j))
    hbm = pl.BlockSpec(memory_space=pl.ANY)
    ex_shapes, ex_sems = _exchange_shapes(ex_arrays, scatter) if nx else ([], [])
    outs = pl.pallas_call(
        body, grid=(ni, nj, nk), name=name,
        in_specs=[a_spec, b_spec] + [hbm] * nx, out_specs=[pl.BlockSpec((tm, tn), lambda i, j, k: (i, j))] + [hbm] * nx,
        out_shape=[jax.ShapeDtypeStruct((M, N), out_dtype)] + ex_shapes,
        scratch_shapes=([] if nk == 1 else [pltpu.VMEM((tm, tn), F32)]) + ex_sems,
        compiler_params=_cp(),
    )(a, b, *ex_arrays)
    return (outs[0], list(outs[1:])) if nx else outs[0]


def _norm_mod_fn(x, nw, sh, sc):
    y = x * lax.rsqrt(jnp.mean(x * x, axis=-1, keepdims=True) + EPS)
    return (y * nw) * (1.0 + sc) + sh


def _norm_mod_fwd(x, ctx, nw, sh, sc, name):
    nlat = x.shape[0]
    T = nlat + ctx.shape[0]
    tb = _tile(ctx.shape[0], (256, 128))
    nlb = nlat // tb

    def body(x_ref, c_ref, nw_ref, sh_ref, sc_ref, h_ref):
        rows = jnp.where(pl.program_id(0) < nlb, x_ref[...], c_ref[...])
        h_ref[...] = _norm_mod_fn(rows, nw_ref[...], sh_ref[0], sc_ref[0]).astype(BF)

    seg = pl.BlockSpec((1, 1, D), lambda i: (jnp.where(i >= nlb, 1, 0), 0, 0))
    return pl.pallas_call(
        body, grid=(T // tb,), name=name,
        in_specs=[pl.BlockSpec((tb, D), lambda i: (jnp.minimum(i, nlb - 1), 0)),
                  pl.BlockSpec((tb, D), lambda i: (jnp.maximum(i - nlb, 0), 0)), pl.BlockSpec((1, D), lambda i: (0, 0)), seg, seg],
        out_specs=pl.BlockSpec((tb, D), lambda i: (i, 0)),
        out_shape=jax.ShapeDtypeStruct((T, D), BF),
    )(x, ctx, nw, sh, sc)


def _norm_mod_bwd(x, nw, sh, sc, dh, *, row0, name, residual=None):
    nrows = x.shape[0]
    tb = _tile(nrows, (512, 256, 128))
    b0 = row0 // tb

    def body(x_ref, nw_ref, sh_ref, sc_ref, dh_ref, *refs):
        dnw_ref, dsh_ref, dsc_ref = refs[-3:]
        _, vjp = jax.vjp(_norm_mod_fn, x_ref[...], nw_ref[...], sh_ref[...], sc_ref[...])
        dx, dnw, dsh, dsc = vjp(dh_ref[...])
        if residual is not None:
            refs[1][...] = dx + refs[0][...]

        @pl.when(pl.program_id(0) == 0)
        def _():
            dnw_ref[...] = jnp.zeros_like(dnw_ref)
            dsh_ref[...] = jnp.zeros_like(dsh_ref)
            dsc_ref[...] = jnp.zeros_like(dsc_ref)

        dnw_ref[...] += dnw
        dsh_ref[...] += dsh
        dsc_ref[...] += dsc

    dh_row = pl.BlockSpec((tb, D), lambda i: (b0 + i, 0))
    out_row = pl.BlockSpec((tb, D), lambda i: (i, 0))
    one = pl.BlockSpec((1, D), lambda i: (0, 0))
    with_dx = residual is not None
    return pl.pallas_call(
        body, grid=(nrows // tb,), name=name,
        in_specs=[out_row, one, one, one, dh_row] + [out_row] * with_dx, out_specs=[out_row] * with_dx + [one] * 3,
        out_shape=[jax.ShapeDtypeStruct((nrows, D), F32)] * with_dx + [jax.ShapeDtypeStruct((1, D), F32)] * 3,
        compiler_params=_cp(),
    )(x, nw, sh, sc, dh, *([residual] if with_dx else []))


def _resid_norm_fwd(x, gate, y, nw, sh, sc):
    n = y.shape[0]
    tb = _tile(n, (512, 256, 128))

    def body(x_ref, g_ref, y_ref, nw_ref, sh_ref, sc_ref, x1_ref, h_ref):
        x1 = x_ref[...] + g_ref[...] * y_ref[...]
        x1_ref[...] = x1
        h_ref[...] = _norm_mod_fn(x1, nw_ref[...], sh_ref[...], sc_ref[...]).astype(BF)

    row = pl.BlockSpec((tb, D), lambda i: (i, 0))
    one = pl.BlockSpec((1, D), lambda i: (0, 0))
    return pl.pallas_call(
        body, grid=(n // tb,), name="resid_norm_fwd",
        in_specs=[row, one, row, one, one, one], out_specs=[row, row],
        out_shape=[jax.ShapeDtypeStruct((n, D), F32), jax.ShapeDtypeStruct((n, D), BF)],
        compiler_params=_cp(),
    )(x, gate, y, nw, sh, sc)


def _resid_norm_bwd(x1, gate, y, nw, sh, sc, dh, dx1_direct, w_out):
    n = y.shape[0]
    tb = _tile(n, (512, 256, 128))

    def body(x1_ref, g_ref, y_ref, nw_ref, sh_ref, sc_ref, dh_ref, dd_ref, wo_ref,
             dx_ref, dy_ref, dg_ref, dnw_ref, dsh_ref, dsc_ref, dm_ref):
        _, vjp = jax.vjp(_norm_mod_fn, x1_ref[...], nw_ref[...], sh_ref[...], sc_ref[...])
        dxn, dnw, dsh, dsc = vjp(dh_ref[...])
        dx = dxn + dd_ref[...]
        dx_ref[...] = dx
        dy_ref[...] = (g_ref[...] * dx).astype(BF)
        dm_ref[...] = lax.dot_general(dy_ref[...], wo_ref[...], (_DIMS["nt"], ((), ())), preferred_element_type=F32)

        @pl.when(pl.program_id(0) == 0)
        def _():
            for r in (dg_ref, dnw_ref, dsh_ref, dsc_ref):
                r[...] = jnp.zeros_like(r)

        dg_ref[...] += jnp.sum(dx * y_ref[...], axis=0, keepdims=True)
        dnw_ref[...] += dnw
        dsh_ref[...] += dsh
        dsc_ref[...] += dsc

    row = pl.BlockSpec((tb, D), lambda i: (i, 0))
    one = pl.BlockSpec((1, D), lambda i: (0, 0))
    return pl.pallas_call(
        body, grid=(n // tb,), name="resid_norm_bwd",
        in_specs=[row, one, row, one, one, one, row, row, _resident((D, D))], out_specs=[row, row] + [one] * 4 + [row],
        out_shape=[jax.ShapeDtypeStruct((n, D), F32), jax.ShapeDtypeStruct((n, D), BF)] + [jax.ShapeDtypeStruct((1, D), F32)] * 4
        + [jax.ShapeDtypeStruct((n, D), F32)],
        compiler_params=_cp(),
    )(x1, gate, y, nw, sh, sc, dh, dx1_direct, w_out)


HALO = 8


def _halo_specs(tb, width, nrows, col=0, halo=HALO):
    r8 = tb // halo
    cur = pl.BlockSpec((tb, width), lambda i: (i, col))
    prev = pl.BlockSpec((halo, width), lambda i: (jnp.maximum(i * r8 - 1, 0), col))
    nxt = pl.BlockSpec((halo, width), lambda i: (jnp.minimum((i + 1) * r8, nrows // halo - 1), col))
    return [cur, prev, nxt]


def _segment_edges(seg_rows, tb):
    bounds = [0]
    for s in seg_rows:
        bounds.append(bounds[-1] + s // tb)
    return bounds[:-1], [b - 1 for b in bounds[1:]]


def _keep_halos(i, starts, ends):
    keep_p = functools.reduce(lambda a, b: a & b, [i != s for s in starts])
    keep_n = functools.reduce(lambda a, b: a & b, [i != e for e in ends])
    return keep_p, keep_n


def _ext_rows(refs, cols, keep):
    cur_ref, prev_ref, next_ref = refs
    p = jnp.where(keep[0], prev_ref[:, cols].astype(F32), 0.0)
    n = jnp.where(keep[1], next_ref[:, cols].astype(F32), 0.0)
    return jnp.concatenate([p, cur_ref[:, cols].astype(F32), n], axis=0)


def _shifted_rows(xe, width, transpose=False):
    r = width // 2
    n = xe.shape[0]
    out = []
    for j in range(width):
        s = ((j - r) if transpose else (r - j)) % n
        out.append(xe if s == 0 else pltpu.roll(xe, s, 0))
    return out


def _conv_rows(shifted, w_ref, cols):
    acc = None
    for j, xs in enumerate(shifted):
        term = xs * w_ref[j:j + 1, cols]
        acc = term if acc is None else acc + term
    return acc


def _tap_grads(dcur, shifted, tb):
    return [jnp.sum(dcur * xs[HALO:HALO + tb], axis=0, keepdims=True) for xs in shifted]


def _softplus(x):
    return jnp.maximum(x, 0.0) + jnp.log(1.0 + jnp.exp(-jnp.abs(x)))


def _gates_fn(ba, alog_row, dt_row):
    col = lax.broadcasted_iota(jnp.int32, ba.shape, 1)
    beta = jax.nn.sigmoid(ba)
    g = -jnp.exp(alog_row) * _softplus(ba + dt_row)
    return jnp.where(col < 16, beta, jnp.where(col < 32, g, 0.0))


def _qkv_post_fn(c, kind):
    y = jax.nn.silu(c)
    if kind == 2:
        return y
    n = y * lax.rsqrt(jnp.sum(y * y, axis=-1, keepdims=True) + EPS)
    return n * (HD ** -0.5) if kind == 0 else n


DN_TAPS = 5
FFN_TAPS = 3


def _dn_pre_fwd(p, w8, alog_row, dt_row, seg_rows):
    T = p.shape[0]
    tb = _tile(T, (256, 128))
    starts, ends = _segment_edges(seg_rows, tb)

    def body(cur_ref, prev_ref, next_ref, ba_ref, w_ref, al_ref, dt_ref, q_ref, k_ref, v_ref, gb_ref):
        keep = _keep_halos(pl.program_id(0), starts, ends)
        outs = (q_ref, k_ref, v_ref)
        for kind in range(3):
            for h in range(NH):
                cols = slice(kind * D + h * HD, kind * D + (h + 1) * HD)
                xe = _ext_rows((cur_ref, prev_ref, next_ref), cols, keep)
                conv = _conv_rows(_shifted_rows(xe, DN_TAPS), w_ref, cols)[HALO:HALO + tb]
                outs[kind][:, h * HD:(h + 1) * HD] = _qkv_post_fn(conv, kind)
        gb_ref[...] = _gates_fn(ba_ref[...], al_ref[...], dt_ref[...])

    row = pl.BlockSpec((tb, D), lambda i: (i, 0))
    one = pl.BlockSpec((1, 128), lambda i: (0, 0))
    return pl.pallas_call(
        body, grid=(T // tb,), name="dn_pre_fwd",
        in_specs=_halo_specs(tb, 3 * D, T) + [pl.BlockSpec((tb, 128), lambda i: (i, O_BA // 128)),
                                              pl.BlockSpec((8, 3 * D), lambda i: (0, 0)), one, one],
        out_specs=[row, row, row, pl.BlockSpec((tb, 128), lambda i: (i, 0))],
        out_shape=[jax.ShapeDtypeStruct((T, D), F32)] * 3 + [jax.ShapeDtypeStruct((T, 128), F32)],
        compiler_params=_cp(),
    )(p, p, p, p, w8, alog_row, dt_row)


def _dn_pre_bwd(p, w8, alog_row, dt_row, dq, dk, dv, dgb, others, seg_rows):
    T = p.shape[0]
    tb = _tile(T, (256, 128))
    starts, ends = _segment_edges(seg_rows, tb)
    other_cols = (O_GT, O_Q, O_MG, O_K, O_V)
    assert [o.shape[1] for o in others] == [O_Q - O_GT, O_MG - O_Q, O_K - O_MG, O_V - O_K, O_BA - O_V]

    def body(cur_ref, prev_ref, next_ref, ba_ref, w_ref, al_ref, dt_ref,
             dq_c, dq_p, dq_n, dk_c, dk_p, dk_n, dv_c, dv_p, dv_n, dgb_ref, gt_ref, q_ref, mg_ref, k_ref, v_ref,
             dx_ref, dw_ref, dal_ref, ddt_ref):
        i = pl.program_id(0)
        for c0, ref in zip(other_cols, (gt_ref, q_ref, mg_ref, k_ref, v_ref)):
            dx_ref[:, c0:c0 + ref.shape[1]] = ref[...]
        dx_ref[:, O_BA + 128:] = jnp.zeros((tb, PW - O_BA - 128), BF)
        keep = _keep_halos(i, starts, ends)

        @pl.when(i == 0)
        def _():
            dw_ref[...] = jnp.zeros_like(dw_ref)
            dal_ref[...] = jnp.zeros_like(dal_ref)
            ddt_ref[...] = jnp.zeros_like(ddt_ref)

        douts = ((dq_c, dq_p, dq_n), (dk_c, dk_p, dk_n), (dv_c, dv_p, dv_n))
        for kind in range(3):
            for h in range(NH):
                cols = slice(kind * D + h * HD, kind * D + (h + 1) * HD)
                xe = _ext_rows((cur_ref, prev_ref, next_ref), cols, keep)
                shifted = _shifted_rows(xe, DN_TAPS)
                conv = _conv_rows(shifted, w_ref, cols)
                dye = _ext_rows(douts[kind], slice(h * HD, (h + 1) * HD), keep)
                _, vjp = jax.vjp(functools.partial(_qkv_post_fn, kind=kind), conv)
                dce = vjp(dye)[0]
                dx_ref[:, cols] = _conv_rows(_shifted_rows(dce, DN_TAPS, transpose=True), w_ref, cols)[HALO:HALO + tb].astype(BF)
                for j, g in enumerate(_tap_grads(dce[HALO:HALO + tb], shifted, tb)):
                    dw_ref[j:j + 1, cols] += g
        _, vjp = jax.vjp(_gates_fn, ba_ref[...], al_ref[...], dt_ref[...])
        dba, dal, ddt = vjp(dgb_ref[...])
        dx_ref[:, O_BA:O_BA + 128] = dba.astype(BF)
        dal_ref[...] += dal
        ddt_ref[...] += ddt

    one = pl.BlockSpec((1, 128), lambda i: (0, 0))
    nar = pl.BlockSpec((tb, 128), lambda i: (i, 0))
    wspec = pl.BlockSpec((8, 3 * D), lambda i: (0, 0))
    return pl.pallas_call(
        body, grid=(T // tb,), name="dn_pre_bwd",
        in_specs=_halo_specs(tb, 3 * D, T) + [pl.BlockSpec((tb, 128), lambda i: (i, O_BA // 128)), wspec, one, one]
        + _halo_specs(tb, D, T) * 3 + [nar] + [pl.BlockSpec((tb, o.shape[1]), lambda i: (i, 0)) for o in others],
        out_specs=[pl.BlockSpec((tb, PW), lambda i: (i, 0)), wspec, one, one],
        out_shape=[jax.ShapeDtypeStruct((T, PW), BF), jax.ShapeDtypeStruct((8, 3 * D), F32),
                   jax.ShapeDtypeStruct((1, 128), F32), jax.ShapeDtypeStruct((1, 128), F32)],
        compiler_params=_cp(),
    )(p, p, p, p, w8, alog_row, dt_row, dq, dq, dq, dk, dk, dk, dv, dv, dv, dgb, *others)


def _dot_hi(a, b):
    return jnp.dot(a, b, precision=HI, preferred_element_type=F32)


def _bf_product(a, b, form):
    return lax.dot_general(a.astype(BF), b.astype(BF), (_DIMS[form], ((), ())), preferred_element_type=F32)


def _dot_tn_bf(a, b):
    return _bf_product(a, b, "tn")


@jax.custom_vjp
def _dot_bf(a, b):
    return _bf_product(a, b, "nn")


@jax.custom_vjp
def _dot_nt_bf(a, b):
    return _bf_product(a, b, "nt")


_dot_bf.defvjp(lambda a, b: (_bf_product(a, b, "nn"), (a, b)),
               lambda res, dc: (_bf_product(dc, res[1], "nt").astype(res[0].dtype), _bf_product(res[0], dc, "tn").astype(res[1].dtype)))
_dot_nt_bf.defvjp(lambda a, b: (_bf_product(a, b, "nt"), (a, b)),
                  lambda res, dc: (_bf_product(dc, res[1], "nn").astype(res[0].dtype), _bf_product(dc, res[0], "tn").astype(res[1].dtype)))


def _dot_h3(a, b):
    return jnp.dot(a, b, precision=lax.Precision.HIGH, preferred_element_type=F32)


def _dot_split(fine, coarse, form):
    hi = fine.astype(BF)
    lo = (fine - hi.astype(F32)).astype(BF)
    cb = coarse.astype(BF)
    if form == "tn":
        return lax.dot_general(jnp.concatenate([cb, cb], axis=0), jnp.concatenate([hi, lo], axis=0),
                               (_DIMS["tn"], ((), ())), preferred_element_type=F32)
    parts = jnp.concatenate([hi, lo], axis=1)
    if form == "nt":
        return lax.dot_general(parts, jnp.concatenate([cb, cb], axis=1), (_DIMS["nt"], ((), ())), preferred_element_type=F32)
    return jnp.dot(parts, jnp.concatenate([cb, cb], axis=0), preferred_element_type=F32)


@jax.custom_vjp
def _mm_split(a, b):
    return _dot_split(a, b, "nn")


_mm_split.defvjp(lambda a, b: (_dot_split(a, b, "nn"), (a, b)),
                 lambda res, dc: (_dot_split(dc, res[1], "nt"), _dot_split(dc, res[0], "tn")))


def _unit_tri_inverses(mats):
    r, c = _iota2((CB, CB))
    eye = (r == c).astype(F32)
    a8 = [jnp.where((r // 8) == (c // 8), a, 0.0) for a in mats]
    a2 = [_dot_split(x, x, "nn") for x in a8]
    a4 = [_dot_split(x, x, "nn") for x in a2]
    t = [_dot_split(eye - x, eye + y, "nn") for x, y in zip(a8, a2)]
    t = [_dot_split(x, eye + y, "nn") for x, y in zip(t, a4)]
    b = 8
    while b < CB:
        mask = ((r // (2 * b)) == (c // (2 * b))) & ((r // b) != (c // b))
        te = [_dot_split(x, jnp.where(mask, a, 0.0), "nn") for x, a in zip(t, mats)]
        t = [x - _dot_split(y, x, "nn") for x, y in zip(t, te)]
        b *= 2
    return t


@jax.custom_vjp
def _saved_inverse(a, t):
    return t


_saved_inverse.defvjp(lambda a, t: (t, t),
                      lambda t, dt: (-_dot_split(_dot_split(dt, t, "nt"), t, "tn"), jnp.zeros_like(t)))


def _dn1_decay(gc, reverse):
    r, c = _iota2((CB, CB))
    incl = (c >= r) if reverse else (c <= r)
    return jnp.where(incl, jnp.exp(jnp.where(incl, gc - gc.T, 0.0)), 0.0)


def _dn1_heads(qs, ks, vs, betas, gcs, ts_saved, reverse, kks=None, qks=None):
    r, c = _iota2((CB, CB))
    strict = (c > r) if reverse else (c < r)
    decays = [_dn1_decay(gc, reverse) for gc in gcs]
    kks = kks or [_dot_nt_bf(k, k) for k in ks]
    systems = [jnp.where(strict, b * kk * dc, 0.0) for b, kk, dc in zip(betas, kks, decays)]
    if ts_saved is None:
        ts = _unit_tri_inverses(systems)
    else:
        ts = [_saved_inverse(a, t) for a, t in zip(systems, ts_saved)]
    egs = [jnp.exp(gc) for gc in gcs]
    us = [_mm_split(t, v * b) for t, v, b in zip(ts, vs, betas)]
    ws = [_mm_split(t, k * (b * eg)) for t, k, b, eg in zip(ts, ks, betas, egs)]
    qks = qks or [_dot_nt_bf(q, k) for q, k in zip(qs, ks)]
    last = 0 if reverse else CB - 1
    glogs = [jnp.sum(jnp.where(r == last, gc, 0.0), axis=0, keepdims=True) for gc in gcs]
    outs = [(u, w, q * eg, k * jnp.exp(gl - gc), qk * dc, jnp.exp(gl))
            for u, w, q, k, eg, gl, gc, qk, dc in zip(us, ws, qs, ks, egs, glogs, gcs, qks, decays)]
    return outs, ts


def _cum_matrix(upper):
    r, c = _iota2((CB, CB))
    return ((c >= r) if upper else (c <= r)).astype(F32)


def _lane_bcast(x, col):
    return jnp.broadcast_to(x[:, col:col + 1], x.shape)


_HEAD_SLICES = [slice(h * HD, (h + 1) * HD) for h in range(NH)]


def _dn1_fwd(q, k, v, gb):
    T = q.shape[0]
    nb = T // CB

    def body(q_ref, k_ref, v_ref, gb_ref, *out_refs):
        gbv = gb_ref[...]
        qs = [q_ref[:, sl] for sl in _HEAD_SLICES]
        ks = [k_ref[:, sl] for sl in _HEAD_SLICES]
        vs = [v_ref[:, sl] for sl in _HEAD_SLICES]
        kks = [_dot_nt_bf(x, x) for x in ks]
        qks = [_dot_nt_bf(x, y) for x, y in zip(qs, ks)]
        for d in (0, 1):
            u_ref, w_ref, qg_ref, kd_ref, qkd_ref, gl_ref, t_ref = out_refs[7 * d:7 * d + 7]
            gcum = _dot_h3(_cum_matrix(d == 1), gbv)
            betas = [_lane_bcast(gbv, d * NH + h) for h in range(NH)]
            gcs = [_lane_bcast(gcum, 16 + d * NH + h) for h in range(NH)]
            outs, ts = _dn1_heads(qs, ks, vs, betas, gcs, None, d == 1, kks, qks)
            for h, sl in enumerate(_HEAD_SLICES):
                u, w, qg, kd, qkd, gl = outs[h]
                u_ref[:, sl] = u
                w_ref[:, sl] = w.astype(BF)
                qg_ref[:, sl] = qg.astype(BF)
                kd_ref[:, sl] = kd.astype(BF)
                qkd_ref[:, sl] = qkd.astype(BF)
                gl_ref[h] = gl
                t_ref[:, sl] = ts[h]

    tb = pl.BlockSpec((CB, D), lambda i: (i, 0))
    one_dir_specs = [tb, tb, tb, tb, tb, pl.BlockSpec((NH, 1, 128), lambda i: (i, 0, 0)), tb]
    one_dir_shapes = ([jax.ShapeDtypeStruct((T, D), F32)] + [jax.ShapeDtypeStruct((T, D), BF)] * 4
                      + [jax.ShapeDtypeStruct((nb * NH, 1, 128), F32), jax.ShapeDtypeStruct((T, D), F32)])
    outs = pl.pallas_call(
        body, grid=(nb,), name="dn1_fwd",
        in_specs=[tb, tb, tb, pl.BlockSpec((CB, 128), lambda i: (i, 0))],
        out_specs=one_dir_specs * 2, out_shape=one_dir_shapes * 2, compiler_params=_cp(),
    )(q, k, v, gb)
    return [tuple(outs[:7]), tuple(outs[7:])]


def _dn1_bwd(q, k, v, gb, tinvs, cots):
    T = q.shape[0]
    nb = T // CB

    def body(q_ref, k_ref, v_ref, gb_ref, *refs):
        dir_refs, (dq_ref, dk_ref, dv_ref, dgb_ref) = refs[:14], refs[14:]
        gbv = gb_ref[...]
        qs = [q_ref[:, sl] for sl in _HEAD_SLICES]
        ks = [k_ref[:, sl] for sl in _HEAD_SLICES]
        vs = [v_ref[:, sl] for sl in _HEAD_SLICES]
        lane = lax.broadcasted_iota(jnp.int32, (CB, 128), 1)
        dgb = jnp.zeros((CB, 128), F32)
        for d in (0, 1):
            t_ref, du_ref, dw_ref, dqg_ref, dkd_ref, dqkd_ref, dgl_ref = dir_refs[7 * d:7 * d + 7]
            gcum = _dot_h3(_cum_matrix(d == 1), gbv)
            betas = [_lane_bcast(gbv, d * NH + h) for h in range(NH)]
            gcs = [_lane_bcast(gcum, 16 + d * NH + h) for h in range(NH)]
            ts = [t_ref[:, sl] for sl in _HEAD_SLICES]
            f = lambda qs, ks, vs, betas, gcs: _dn1_heads(qs, ks, vs, betas, gcs, ts, d == 1)[0]
            _, vjp = jax.vjp(f, qs, ks, vs, betas, gcs)
            cot = [(du_ref[:, sl], dw_ref[:, sl], dqg_ref[:, sl], dkd_ref[:, sl], dqkd_ref[:, sl], dgl_ref[h])
                   for h, sl in enumerate(_HEAD_SLICES)]
            dqs, dks, dvs, dbetas, dgcs = vjp(cot)
            dgcum = jnp.zeros((CB, 128), F32)
            for h, sl in enumerate(_HEAD_SLICES):
                if d == 0:
                    dq_ref[:, sl] = dqs[h]
                    dk_ref[:, sl] = dks[h]
                    dv_ref[:, sl] = dvs[h]
                else:
                    dq_ref[:, sl] += dqs[h]
                    dk_ref[:, sl] += dks[h]
                    dv_ref[:, sl] += dvs[h]
                dgb = dgb + jnp.where(lane == d * NH + h, jnp.sum(dbetas[h], axis=1, keepdims=True), 0.0)
                dgcum = dgcum + jnp.where(lane == 16 + d * NH + h, jnp.sum(dgcs[h], axis=1, keepdims=True), 0.0)
            dgb = dgb + _dot_h3(_cum_matrix(d == 0), dgcum)
        dgb_ref[...] = dgb

    tb = pl.BlockSpec((CB, D), lambda i: (i, 0))
    gbs = pl.BlockSpec((CB, 128), lambda i: (i, 0))
    gls = pl.BlockSpec((NH, 1, 128), lambda i: (i, 0, 0))
    args = []
    for d in (0, 1):
        args += [tinvs[d], *cots[d]]
    return pl.pallas_call(
        body, grid=(nb,), name="dn1_bwd",
        in_specs=[tb, tb, tb, gbs] + [tb, tb, tb, tb, tb, tb, gls] * 2, out_specs=[tb, tb, tb, gbs],
        out_shape=[jax.ShapeDtypeStruct((T, D), F32)] * 3 + [jax.ShapeDtypeStruct((T, 128), F32)],
        compiler_params=_cp(),
    )(q, k, v, gb, *args)


def _dn2_steps(chains):
    ws = [_dot_bf(w, s) for _, w, _, _, _, _, s in chains]
    v_new = [c[0] - x for c, x in zip(chains, ws)]
    o_state = [_dot_bf(c[2], c[6]) for c in chains]
    o_local = [_dot_bf(c[4], vn) for c, vn in zip(chains, v_new)]
    grow = [_dot_tn_bf(c[3], vn) for c, vn in zip(chains, v_new)]
    return [a + b for a, b in zip(o_state, o_local)], [c[6] * c[5] + g for c, g in zip(chains, grow)]


def _dn2_steps_bwd(chains, cot_o, cot_s):
    bf = lambda a: a.astype(BF)
    nt = lambda a, b: lax.dot_general(bf(a), bf(b), (_DIMS["nt"], ((), ())), preferred_element_type=F32)
    v_new = [c[0] - _dot_bf(c[1], c[6]) for c in chains]
    dv = [_dot_bf(jnp.concatenate([c[4].T, c[3]], axis=1), jnp.concatenate([do, ds], axis=0))
          for c, do, ds in zip(chains, cot_o, cot_s)]
    both = [nt(jnp.concatenate([do, x], axis=0), c[6]) for c, do, x in zip(chains, cot_o, dv)]
    dqkd = [nt(do, vn) for do, vn in zip(cot_o, v_new)]
    dkd = [nt(vn, ds) for vn, ds in zip(v_new, cot_s)]
    dstate = [_dot_bf(jnp.concatenate([c[2].T, -c[1].T], axis=1), jnp.concatenate([do, x], axis=0))
              for c, do, x in zip(chains, cot_o, dv)]
    return [(x, -b[CB:], b[:CB], dk, dq, jnp.sum(ds * c[6], axis=0, keepdims=True), ds * c[5] + g)
            for c, x, b, dk, dq, ds, g in zip(chains, dv, both, dkd, dqkd, cot_s, dstate)]


def _scan_order(direction, nlat_b, nall_b):
    if direction == 0:
        return lambda i: (i + nlat_b) % nall_b
    return lambda i: nall_b - 1 - i


def _dn2_fwd(per_dir, nlat):
    T = per_dir[0][0].shape[0]
    nb = T // CB
    blks = [_scan_order(d, nlat // CB, nb) for d in (0, 1)]

    def body(*refs):
        ins, outs, s_scr = refs[:12], refs[12:16], refs[16]

        @pl.when(pl.program_id(0) == 0)
        def _():
            s_scr[...] = jnp.zeros_like(s_scr)
        for d in (0, 1):
            outs[2 * d + 1][0] = s_scr[d]
        where = [(d, h, sl) for h, sl in enumerate(_HEAD_SLICES) for d in (0, 1)]
        chains = []
        for d, h, sl in where:
            u_ref, w_ref, qg_ref, kd_ref, qkd_ref, gl_ref = ins[6 * d:6 * d + 6]
            chains.append((u_ref[:, sl], w_ref[:, sl], qg_ref[:, sl], kd_ref[:, sl], qkd_ref[:, sl], gl_ref[h], s_scr[d, h]))
        os, states = _dn2_steps(chains)
        for (d, h, sl), o, s_next in zip(where, os, states):
            outs[2 * d][:, sl] = o
            s_scr[d, h] = s_next

    in_specs, out_specs, args = [], [], []
    for d in (0, 1):
        blk = blks[d]
        tb = pl.BlockSpec((CB, D), lambda i, blk=blk: (blk(i), 0))
        in_specs += [tb] * 5 + [pl.BlockSpec((NH, 1, 128), lambda i, blk=blk: (blk(i), 0, 0))]
        out_specs += [tb, pl.BlockSpec((1, NH, HD, HD), lambda i, blk=blk: (blk(i), 0, 0, 0))]
        args += list(per_dir[d])
    outs = pl.pallas_call(
        body, grid=(nb,), name="dn2_fwd", in_specs=in_specs, out_specs=out_specs,
        out_shape=[jax.ShapeDtypeStruct((T, D), F32), jax.ShapeDtypeStruct((nb, NH, HD, HD), F32)] * 2,
        scratch_shapes=[pltpu.VMEM((2, NH, HD, HD), F32)], compiler_params=_cp(),
    )(*args)
    return [tuple(outs[:2]), tuple(outs[2:])]


def _dn2_bwd(per_dir, do, nlat):
    T = per_dir[0][0].shape[0]
    nb = T // CB
    nlat_b = nlat // CB
    fwd = [_scan_order(d, nlat_b, nb) for d in (0, 1)]
    blks = [lambda i, f=f: f(nb - 1 - i) for f in fwd]

    def body(*refs):
        ins, outs, ds_scr = refs[:16], refs[16:28], refs[28]
        i = pl.program_id(0)

        @pl.when(i == 0)
        def _():
            ds_scr[...] = jnp.zeros_like(ds_scr)
        where = [(d, h, sl) for h, sl in enumerate(_HEAD_SLICES) for d in (0, 1)]
        chains, cot_o, cot_s = [], [], []
        for d, h, sl in where:
            u_ref, w_ref, qg_ref, kd_ref, qkd_ref, gl_ref, sall_ref, do_ref = ins[8 * d:8 * d + 8]
            chains.append((u_ref[:, sl], w_ref[:, sl].astype(F32), qg_ref[:, sl].astype(F32), kd_ref[:, sl].astype(F32),
                           qkd_ref[:, sl].astype(F32), gl_ref[h], sall_ref[0, h]))
            cot_o.append(jnp.where(blks[d](i) < nlat_b, do_ref[:, sl], 0.0))
            cot_s.append(ds_scr[d, h])
        for (d, h, sl), (du, dw, dqg, dkd, dqkd, dgl, ds) in zip(where, _dn2_steps_bwd(chains, cot_o, cot_s)):
            du_ref, dw_ref, dqg_ref, dkd_ref, dqkd_ref, dgl_ref = outs[6 * d:6 * d + 6]
            du_ref[:, sl] = du
            dw_ref[:, sl] = dw
            dqg_ref[:, sl] = dqg
            dkd_ref[:, sl] = dkd
            dqkd_ref[:, sl] = dqkd
            dgl_ref[h] = dgl
            ds_scr[d, h] = ds

    in_specs, out_specs, args = [], [], []
    for d in (0, 1):
        blk = blks[d]
        tb = pl.BlockSpec((CB, D), lambda i, blk=blk: (blk(i), 0))
        gls = pl.BlockSpec((NH, 1, 128), lambda i, blk=blk: (blk(i), 0, 0))
        in_specs += [tb] * 5 + [gls, pl.BlockSpec((1, NH, HD, HD), lambda i, blk=blk: (blk(i), 0, 0, 0)),
                                pl.BlockSpec((CB, D), lambda i, blk=blk: (jnp.minimum(blk(i), nlat_b - 1), 0))]
        out_specs += [tb] * 5 + [gls]
        args += list(per_dir[d]) + [do]
    outs = pl.pallas_call(
        body, grid=(nb,), name="dn2_bwd", in_specs=in_specs, out_specs=out_specs,
        out_shape=([jax.ShapeDtypeStruct((T, D), F32)] * 5 + [jax.ShapeDtypeStruct((nb * NH, 1, 128), F32)]) * 2,
        scratch_shapes=[pltpu.VMEM((2, NH, HD, HD), F32)], compiler_params=_cp(),
    )(*args)
    return [tuple(outs[:6]), tuple(outs[6:])]


def _ghn_fn(o, gt, w):
    y = o * lax.rsqrt(jnp.mean(o * o, axis=-1, keepdims=True) + EPS)
    return (y * w) * jax.nn.silu(gt)


def _ghn_fwd(o_f, o_b, p, w, w_branch, nlat):
    tb = _tile(nlat, (512, 256, 128))

    def body(of_ref, ob_ref, gt_ref, w_ref, wb_ref, y_ref, z_ref):
        for h in range(NH):
            sl = slice(h * HD, (h + 1) * HD)
            y_ref[:, sl] = _ghn_fn(of_ref[:, sl] + ob_ref[:, sl], gt_ref[:, sl], w_ref[...]).astype(BF)
        z_ref[...] = jnp.dot(y_ref[...], wb_ref[...], preferred_element_type=F32)

    row = pl.BlockSpec((tb, D), lambda i: (i, 0))
    return pl.pallas_call(
        body, grid=(nlat // tb,), name="ghn_fwd",
        in_specs=[row, row, pl.BlockSpec((tb, D), lambda i: (i, O_GT // D)), pl.BlockSpec((1, HD), lambda i: (0, 0)), _resident((D, D))],
        out_specs=[row, row], out_shape=[jax.ShapeDtypeStruct((nlat, D), BF), jax.ShapeDtypeStruct((nlat, D), F32)],
        compiler_params=_cp(),
    )(o_f, o_b, p, w, w_branch)


def _ghn_bwd(o_f, o_b, p, w, dy, nlat):
    T = p.shape[0]
    tb = _tile(nlat, (256, 128))
    nlb = nlat // tb

    def body(of_ref, ob_ref, gt_ref, w_ref, dy_ref, do_ref, dgt_ref, dw_ref):
        is_lat = pl.program_id(0) < nlb

        @pl.when(pl.program_id(0) == 0)
        def _():
            dw_ref[...] = jnp.zeros_like(dw_ref)
        for h in range(NH):
            sl = slice(h * HD, (h + 1) * HD)
            _, vjp = jax.vjp(_ghn_fn, of_ref[:, sl] + ob_ref[:, sl], gt_ref[:, sl], w_ref[...])
            do, dgt, dw = vjp(dy_ref[:, sl])
            do_ref[:, sl] = do
            dgt_ref[:, sl] = jnp.where(is_lat, dgt, 0.0).astype(BF)
            dw_ref[...] += jnp.where(is_lat, dw, 0.0)

    lat = lambda i: jnp.minimum(i, nlb - 1)
    row = pl.BlockSpec((tb, D), lambda i: (lat(i), 0))
    one = pl.BlockSpec((1, HD), lambda i: (0, 0))
    return pl.pallas_call(
        body, grid=(T // tb,), name="ghn_bwd",
        in_specs=[row, row, pl.BlockSpec((tb, D), lambda i: (lat(i), O_GT // D)), one, row],
        out_specs=[row, pl.BlockSpec((tb, D), lambda i: (i, 0)), one],
        out_shape=[jax.ShapeDtypeStruct((nlat, D), F32), jax.ShapeDtypeStruct((T, D), BF), jax.ShapeDtypeStruct((1, HD), F32)],
    )(o_f, o_b, p, w, dy)


@jax.custom_vjp
def _swap32(x):
    lane = lax.broadcasted_iota(jnp.int32, x.shape, 1)
    return jnp.where((lane & 32) == 0, pltpu.roll(x, 96, 1), pltpu.roll(x, 32, 1))


_swap32.defvjp(lambda x: (_swap32(x), None), lambda _, g: (_swap32(g),))


def _qk_post_fn(xs, w, cos, sin):
    inv = [lax.rsqrt(jnp.mean(x * x, axis=-1, keepdims=True) + EPS) for x in xs]
    ys = [(x * r) * w for x, r in zip(xs, inv)]
    return [y * cos + _swap32(y) * sin for y in ys]


def _attn_prep_fwd(p, qn, kn, cos, sin):
    T = p.shape[0]
    tb = _tile(T, (256, 128))

    def body(q_ref, k_ref, v_ref, qn_ref, kn_ref, cos_ref, sin_ref, qr_ref, kr_ref, vb_ref):
        cos_v, sin_v = cos_ref[...], sin_ref[...]
        for sl, y in zip(_HEAD_SLICES, _qk_post_fn([q_ref[:, sl] for sl in _HEAD_SLICES], qn_ref[...], cos_v, sin_v)):
            qr_ref[:, sl] = y.astype(BF)
        for sl, y in zip(_HEAD_SLICES, _qk_post_fn([k_ref[:, sl] for sl in _HEAD_SLICES[:KVH]], kn_ref[...], cos_v, sin_v)):
            kr_ref[:, sl] = y.astype(BF)
        vb_ref[...] = v_ref[...].astype(BF)

    one = pl.BlockSpec((1, HD), lambda i: (0, 0))
    tab = pl.BlockSpec((tb, HD), lambda i: (i, 0))
    return pl.pallas_call(
        body, grid=(T // tb,), name="attn_prep_fwd",
        in_specs=[pl.BlockSpec((tb, D), lambda i: (i, O_Q // D)), pl.BlockSpec((tb, KV), lambda i: (i, O_K // KV)),
                  pl.BlockSpec((tb, KV), lambda i: (i, O_V // KV)), one, one, tab, tab],
        out_specs=[pl.BlockSpec((tb, D), lambda i: (i, 0)), pl.BlockSpec((tb, KV), lambda i: (i, 0)),
                   pl.BlockSpec((tb, KV), lambda i: (i, 0))],
        out_shape=[jax.ShapeDtypeStruct((T, D), BF), jax.ShapeDtypeStruct((T, KV), BF), jax.ShapeDtypeStruct((T, KV), BF)],
    )(p, p, p, qn, kn, cos, sin)


def _attn_prep_bwd(p, qn, kn, cos, sin, dqr, dkp, dvp, dkc, dvc, nlat):
    T = p.shape[0]
    nqb = nlat // CB
    ncb = (T - nlat) // CB

    def body(q_ref, k_ref, v_ref, qn_ref, kn_ref, cos_ref, sin_ref, dqr_ref, dka_ref, dkb_ref, dkc3_ref, dva_ref, dvb_ref, dvc3_ref,
             dkctx_ref, dvctx_ref, dq_ref, dk_ref, dv_ref, dqn_ref, dkn_ref):
        i = pl.program_id(0)
        is_lat = i < nqb
        cos_v, sin_v = cos_ref[...], sin_ref[...]

        @pl.when(i == 0)
        def _():
            dqn_ref[...] = jnp.zeros_like(dqn_ref)
            dkn_ref[...] = jnp.zeros_like(dkn_ref)

        def band_sum(a_ref, b_ref, c_ref, ctx_ref):
            s = b_ref[0] + jnp.where(i > 0, a_ref[0], 0.0) + jnp.where(i < nqb - 1, c_ref[0], 0.0)
            return jnp.where(is_lat, s, ctx_ref[...])

        dkr = band_sum(dka_ref, dkb_ref, dkc3_ref, dkctx_ref)
        dv_ref[...] = band_sum(dva_ref, dvb_ref, dvc3_ref, dvctx_ref).astype(BF)
        post = lambda xs, w: _qk_post_fn(xs, w, cos_v, sin_v)
        _, vjp = jax.vjp(post, [q_ref[:, sl] for sl in _HEAD_SLICES], qn_ref[...])
        dqs, dqn = vjp([jnp.where(is_lat, dqr_ref[:, sl], 0.0) for sl in _HEAD_SLICES])
        for sl, dq in zip(_HEAD_SLICES, dqs):
            dq_ref[:, sl] = dq.astype(BF)
        dqn_ref[...] += dqn
        _, vjp = jax.vjp(post, [k_ref[:, sl] for sl in _HEAD_SLICES[:KVH]], kn_ref[...])
        dks, dkn = vjp([dkr[:, sl] for sl in _HEAD_SLICES[:KVH]])
        for sl, dk in zip(_HEAD_SLICES, dks):
            dk_ref[:, sl] = dk.astype(BF)
        dkn_ref[...] += dkn

    one = pl.BlockSpec((1, HD), lambda i: (0, 0))
    tab = pl.BlockSpec((CB, HD), lambda i: (i, 0))
    lat = lambda i: jnp.minimum(i, nqb - 1)

    def part(off, slot):
        return pl.BlockSpec((1, CB, KV), lambda i: (jnp.clip(lat(i) + off, 0, nqb - 1) * 3 + slot, 0, 0))

    ctxs = pl.BlockSpec((CB, KV), lambda i: (jnp.clip(i - nqb, 0, ncb - 1), 0))
    kvs = pl.BlockSpec((CB, KV), lambda i: (i, 0))
    return pl.pallas_call(
        body, grid=(T // CB,), name="attn_prep_bwd",
        in_specs=[pl.BlockSpec((CB, D), lambda i: (i, O_Q // D)), pl.BlockSpec((CB, KV), lambda i: (i, O_K // KV)),
                  pl.BlockSpec((CB, KV), lambda i: (i, O_V // KV)), one, one, tab, tab,
                  pl.BlockSpec((CB, D), lambda i: (lat(i), 0)),
                  part(-1, 2), part(0, 1), part(1, 0), part(-1, 2), part(0, 1), part(1, 0), ctxs, ctxs],
        out_specs=[pl.BlockSpec((CB, D), lambda i: (i, 0)), kvs, kvs, one, one],
        out_shape=[jax.ShapeDtypeStruct((T, D), BF), jax.ShapeDtypeStruct((T, KV), BF), jax.ShapeDtypeStruct((T, KV), BF),
                   jax.ShapeDtypeStruct((1, HD), F32), jax.ShapeDtypeStruct((1, HD), F32)],
    )(p, p, p, qn, kn, cos, sin, dqr, dkp, dkp, dkp, dvp, dvp, dvp, dkc, dvc)


def _attn_groups_fn(qs, kalls, valls, sinks, bias):
    groups = range(KVH)
    q = [jnp.concatenate(qs[GRP * g:GRP * (g + 1)], axis=0) for g in groups]
    s = [_bf_product(q[g], kalls[g], "nt") * (HD ** -0.5) + bias for g in groups]
    sk = [jnp.concatenate([jnp.broadcast_to(jnp.mean(t, axis=1, keepdims=True), (CB, 1)) for t in sinks[GRP * g:GRP * (g + 1)]],
                          axis=0) for g in groups]
    m = [lax.stop_gradient(jnp.maximum(jnp.max(s[g], axis=1, keepdims=True), sk[g])) for g in groups]
    e = [jnp.exp(s[g] - m[g]) for g in groups]
    den = [jnp.sum(e[g], axis=1, keepdims=True) + jnp.exp(sk[g] - m[g]) for g in groups]
    return [_bf_product(e[g] / den[g], valls[g], "nn") for g in groups]


def _attn_bias(lc):
    r, c = _iota2((GRP * CB, 3 * CB + lc))
    rel = c - (r & (CB - 1))
    win = (rel >= 0) & (rel <= 2 * CB)
    ctx = c >= 3 * CB
    seen = [(win & (c >= CB)) | ctx, win | ctx, (win & (c < 2 * CB)) | ctx]
    return jnp.stack([jnp.where(s, 0.0, -1e30) for s in seen]).astype(F32)


def _attn_specs(nqb, lc, nlat):
    assert nqb >= 2
    qs = pl.BlockSpec((CB, D), lambda i: (i, 0))
    ka = pl.BlockSpec((CB, KV), lambda i: (jnp.maximum(i - 1, 0), 0))
    kb = pl.BlockSpec((CB, KV), lambda i: (i, 0))
    kc = pl.BlockSpec((CB, KV), lambda i: (jnp.minimum(i + 1, nqb - 1), 0))
    kx = pl.BlockSpec((lc, KV), lambda i: (nlat // lc, 0))
    sk = pl.BlockSpec((KVH, 8, 128), lambda i: (0, 0, 0))
    bs = pl.BlockSpec((1, GRP * CB, 3 * CB + lc), lambda i: (jnp.where(i == 0, 0, jnp.where(i == nqb - 1, 2, 1)), 0, 0))
    return qs, ka, kb, kc, kx, sk, bs


def _attn_operands(q_ref, k_refs, v_refs, sk_ref, dtype):
    sls = [slice(g * HD, (g + 1) * HD) for g in range(KVH)]
    kalls = [jnp.concatenate([r[:, sl] for r in k_refs], axis=0).astype(dtype) for sl in sls]
    valls = [jnp.concatenate([r[:, sl] for r in v_refs], axis=0).astype(dtype) for sl in sls]
    qs = [q_ref[:, sl].astype(dtype) for sl in _HEAD_SLICES]
    sinks = [sk_ref[h // GRP, (h % GRP):(h % GRP) + 1, :] for h in range(NH)]
    return qs, kalls, valls, sinks


def _attn_fwd(qr, kr, vb, sink, w_branch, nlat):
    lc = kr.shape[0] - nlat
    nqb = nlat // CB
    qs, ka, kb, kc, kx, sk, bs = _attn_specs(nqb, lc, nlat)

    def body(q_ref, ka_ref, kb_ref, kc_ref, kx_ref, va_ref, vb_ref, vc_ref, vx_ref, sk_ref, bias_ref, wb_ref, o_ref, z_ref):
        operands = _attn_operands(q_ref, (ka_ref, kb_ref, kc_ref, kx_ref), (va_ref, vb_ref, vc_ref, vx_ref), sk_ref, BF)
        outs = _attn_groups_fn(*operands, bias_ref[0])
        for h, sl in enumerate(_HEAD_SLICES):
            o_ref[:, sl] = outs[h // GRP][(h % GRP) * CB:(h % GRP + 1) * CB].astype(BF)
        z_ref[...] = jnp.dot(o_ref[...], wb_ref[...], preferred_element_type=F32)

    return pl.pallas_call(
        body, grid=(nqb,), name="attn_fwd",
        in_specs=[qs, ka, kb, kc, kx, ka, kb, kc, kx, sk, bs, _resident((D, D))], out_specs=[qs, qs],
        out_shape=[jax.ShapeDtypeStruct((nlat, D), BF), jax.ShapeDtypeStruct((nlat, D), F32)], compiler_params=_cp(),
    )(qr, kr, kr, kr, kr, vb, vb, vb, vb, sink, _attn_bias(lc), w_branch)


def _attn_bwd(qr, kr, vb, sink, dy, nlat):
    lc = kr.shape[0] - nlat
    nqb = nlat // CB
    qs, ka, kb, kc, kx, sk, bs = _attn_specs(nqb, lc, nlat)

    def body(q_ref, ka_ref, kb_ref, kc_ref, kx_ref, va_ref, vb_ref, vc_ref, vx_ref, sk_ref, dy_ref, bias_ref,
             dq_ref, dkp_ref, dvp_ref, dkx_ref, dvx_ref, dsk_ref):
        operands = _attn_operands(q_ref, (ka_ref, kb_ref, kc_ref, kx_ref), (va_ref, vb_ref, vc_ref, vx_ref), sk_ref, F32)
        _, vjp = jax.vjp(functools.partial(_attn_groups_fn, bias=bias_ref[0]), *operands)
        dys_g = [jnp.concatenate([dy_ref[:, sl] for sl in _HEAD_SLICES[GRP * g:GRP * (g + 1)]], axis=0) for g in range(KVH)]
        dqs, dks, dvs, dsinks = vjp(dys_g)

        @pl.when(pl.program_id(0) == 0)
        def _():
            dkx_ref[...] = jnp.zeros_like(dkx_ref)
            dvx_ref[...] = jnp.zeros_like(dvx_ref)
            dsk_ref[...] = jnp.zeros_like(dsk_ref)

        for h, sl in enumerate(_HEAD_SLICES):
            dq_ref[:, sl] = dqs[h]
            dsk_ref[h // GRP, (h % GRP):(h % GRP) + 1, :] += dsinks[h]
        for g in range(KVH):
            sl = slice(g * HD, (g + 1) * HD)
            for t in range(3):
                dkp_ref[t, :, sl] = dks[g][t * CB:(t + 1) * CB]
                dvp_ref[t, :, sl] = dvs[g][t * CB:(t + 1) * CB]
            dkx_ref[:, sl] += dks[g][3 * CB:]
            dvx_ref[:, sl] += dvs[g][3 * CB:]

    dys = qs
    parts = pl.BlockSpec((3, CB, KV), lambda i: (i, 0, 0))
    ctxo = pl.BlockSpec((lc, KV), lambda i: (0, 0))
    return pl.pallas_call(
        body, grid=(nqb,), name="attn_bwd",
        in_specs=[qs, ka, kb, kc, kx, ka, kb, kc, kx, sk, dys, bs],
        out_specs=[dys, parts, parts, ctxo, ctxo, sk],
        out_shape=[jax.ShapeDtypeStruct((nlat, D), F32), jax.ShapeDtypeStruct((3 * nqb, CB, KV), F32),
                   jax.ShapeDtypeStruct((3 * nqb, CB, KV), F32), jax.ShapeDtypeStruct((lc, KV), F32),
                   jax.ShapeDtypeStruct((lc, KV), F32), jax.ShapeDtypeStruct((KVH, 8, 128), F32)],
        compiler_params=_cp(),
    )(qr, kr, kr, kr, kr, vb, vb, vb, vb, sink, dy, _attn_bias(lc))


def _merge_fn(z_dn, z_at, g_dn, g_at):
    return jax.nn.sigmoid(g_dn) * z_dn + jax.nn.sigmoid(g_at) * z_at


def _merge_fwd(z_dn, z_at, p, w_out, nlat):
    tb = _tile(nlat, (512, 256, 128))

    def body(zd_ref, za_ref, gd_ref, ga_ref, wo_ref, o_ref, mix_ref):
        o_ref[...] = _merge_fn(zd_ref[...], za_ref[...], gd_ref[...], ga_ref[...]).astype(BF)
        mix_ref[...] = jnp.dot(o_ref[...], wo_ref[...], preferred_element_type=F32)

    row = pl.BlockSpec((tb, D), lambda i: (i, 0))
    return pl.pallas_call(
        body, grid=(nlat // tb,), name="merge_fwd",
        in_specs=[row, row, pl.BlockSpec((tb, D), lambda i: (i, O_MG // D)), pl.BlockSpec((tb, D), lambda i: (i, O_MG // D + 1)),
                  _resident((D, D))],
        out_specs=[row, row], out_shape=[jax.ShapeDtypeStruct((nlat, D), BF), jax.ShapeDtypeStruct((nlat, D), F32)],
        compiler_params=_cp(),
    )(z_dn, z_at, p, p, w_out)


def _merge_bwd(z_dn, z_at, p, dm, w_bdn, w_bat, nlat):
    T = p.shape[0]
    tb = _tile(nlat, (256, 128))
    nlb = nlat // tb

    def body(zd_ref, za_ref, gd_ref, ga_ref, dm_ref, wd_ref, wa_ref, dzd_ref, dza_ref, dg_ref, dyd_ref, dya_ref):
        is_lat = pl.program_id(0) < nlb
        _, vjp = jax.vjp(_merge_fn, zd_ref[...], za_ref[...], gd_ref[...], ga_ref[...])
        dzd, dza, dgd, dga = vjp(dm_ref[...])
        dzd_ref[...] = dzd.astype(BF)
        dza_ref[...] = dza.astype(BF)
        dg_ref[:, :D] = jnp.where(is_lat, dgd, 0.0).astype(BF)
        dg_ref[:, D:] = jnp.where(is_lat, dga, 0.0).astype(BF)
        dyd_ref[...] = lax.dot_general(dzd_ref[...], wd_ref[...], (_DIMS["nt"], ((), ())), preferred_element_type=F32)
        dya_ref[...] = lax.dot_general(dza_ref[...], wa_ref[...], (_DIMS["nt"], ((), ())), preferred_element_type=F32)

    lat = lambda i: jnp.minimum(i, nlb - 1)
    row = pl.BlockSpec((tb, D), lambda i: (lat(i), 0))
    return pl.pallas_call(
        body, grid=(T // tb,), name="merge_bwd",
        in_specs=[row, row, pl.BlockSpec((tb, D), lambda i: (lat(i), O_MG // D)),
                  pl.BlockSpec((tb, D), lambda i: (lat(i), O_MG // D + 1)), row, _resident((D, D)), _resident((D, D))],
        out_specs=[row, row, pl.BlockSpec((tb, 2 * D), lambda i: (i, 0)), row, row],
        out_shape=[jax.ShapeDtypeStruct((nlat, D), BF), jax.ShapeDtypeStruct((nlat, D), BF), jax.ShapeDtypeStruct((T, 2 * D), BF),
                   jax.ShapeDtypeStruct((nlat, D), F32), jax.ShapeDtypeStruct((nlat, D), F32)],
    )(z_dn, z_at, p, p, dm, w_bdn, w_bat)


def _swiglu_fn(ug, uv):
    return jax.nn.silu(ug) * uv


FFN_GROUP = 256


def _resident(shape):
    return pl.BlockSpec(shape, lambda i: (0,) * len(shape), pipeline_mode=pl.Buffered(1))


H_HALO = 16


def _up_project(h_refs, wu_ref, u_scr):
    cur_ref, prev_ref, next_ref = h_refs
    rows = jnp.concatenate([prev_ref[...], cur_ref[...], next_ref[...]], axis=0)
    u_scr[...] = jnp.dot(rows, wu_ref[...], preferred_element_type=F32)


def _up_ext_rows(u_scr, cols, keep, tb):
    xe = u_scr[H_HALO - HALO:H_HALO + tb + HALO, cols]
    r = lax.broadcasted_iota(jnp.int32, (tb + 2 * HALO, 1), 0)
    inside = ((r >= HALO) | keep[0]) & ((r < HALO + tb) | keep[1])
    return jnp.where(inside, xe, 0.0)


def _ffn_fwd(h, w_up, w8, bias, w_down):
    n = h.shape[0]
    tb = _tile(n, (256, 128))
    starts, ends = _segment_edges((n,), tb)

    def body(cur_ref, prev_ref, next_ref, wu_ref, w_ref, b_ref, wd_ref, u_ref, o_ref, ff_ref, u_scr):
        keep = _keep_halos(pl.program_id(0), starts, ends)
        _up_project((cur_ref, prev_ref, next_ref), wu_ref, u_scr)
        u_ref[...] = u_scr[H_HALO:H_HALO + tb, :]

        for c0 in range(0, DFF, FFN_GROUP):
            halves = []
            for cols in (slice(c0, c0 + FFN_GROUP), slice(DFF + c0, DFF + c0 + FFN_GROUP)):
                xe = _up_ext_rows(u_scr, cols, keep, tb)
                halves.append(_conv_rows(_shifted_rows(xe, FFN_TAPS), w_ref, cols)[HALO:HALO + tb] + b_ref[:, cols])
            o_ref[:, c0:c0 + FFN_GROUP] = _swiglu_fn(*halves).astype(BF)
        ff_ref[...] = jnp.dot(o_ref[...], wd_ref[...], preferred_element_type=F32)

    return pl.pallas_call(
        body, grid=(n // tb,), name="ffn_fwd",
        in_specs=_halo_specs(tb, D, n, halo=H_HALO) + [_resident((D, 2 * DFF)), pl.BlockSpec((8, 2 * DFF), lambda i: (0, 0)),
                                                        pl.BlockSpec((1, 2 * DFF), lambda i: (0, 0)), _resident((DFF, D))],
        out_specs=[pl.BlockSpec((tb, 2 * DFF), lambda i: (i, 0)), pl.BlockSpec((tb, DFF), lambda i: (i, 0)),
                   pl.BlockSpec((tb, D), lambda i: (i, 0))],
        out_shape=[jax.ShapeDtypeStruct((n, 2 * DFF), F32), jax.ShapeDtypeStruct((n, DFF), BF), jax.ShapeDtypeStruct((n, D), F32)],
        scratch_shapes=[pltpu.VMEM((tb + 2 * H_HALO, 2 * DFF), F32)],
        compiler_params=_cp(),
    )(h, h, h, w_up, w8, bias, w_down)


def _ffn_bwd(u, w_up, w8, bias, da):
    n = u.shape[0]
    tb = _tile(n, (256, 128))
    starts, ends = _segment_edges((n,), tb)

    def body(cur_ref, prev_ref, next_ref, wu_ref, w_ref, b_ref, da_c, da_p, da_n, du_ref, dw_ref, db_ref, dh_ref):
        i = pl.program_id(0)
        keep = _keep_halos(i, starts, ends)

        @pl.when(i == 0)
        def _():
            dw_ref[...] = jnp.zeros_like(dw_ref)
            db_ref[...] = jnp.zeros_like(db_ref)

        for c0 in range(0, DFF, FFN_GROUP):
            col_pair = (slice(c0, c0 + FFN_GROUP), slice(DFF + c0, DFF + c0 + FFN_GROUP))
            shifts = [_shifted_rows(_ext_rows((cur_ref, prev_ref, next_ref), cols, keep), FFN_TAPS) for cols in col_pair]
            convs = [_conv_rows(shifted, w_ref, cols) + b_ref[:, cols] for shifted, cols in zip(shifts, col_pair)]
            dae = _ext_rows((da_c, da_p, da_n), col_pair[0], keep)
            _, vjp = jax.vjp(_swiglu_fn, *convs)
            for shifted, cols, dce in zip(shifts, col_pair, vjp(dae)):
                du_ref[:, cols] = _conv_rows(_shifted_rows(dce, FFN_TAPS, transpose=True), w_ref, cols)[HALO:HALO + tb].astype(BF)
                dcur = dce[HALO:HALO + tb]
                for j, g in enumerate(_tap_grads(dcur, shifted, tb)):
                    dw_ref[j:j + 1, cols] += g
                db_ref[:, cols] += jnp.sum(dcur, axis=0, keepdims=True)
        dh_ref[...] = lax.dot_general(du_ref[...], wu_ref[...], (_DIMS["nt"], ((), ())), preferred_element_type=F32)

    wspec = pl.BlockSpec((8, 2 * DFF), lambda i: (0, 0))
    bspec = pl.BlockSpec((1, 2 * DFF), lambda i: (0, 0))
    return pl.pallas_call(
        body, grid=(n // tb,), name="ffn_bwd",
        in_specs=_halo_specs(tb, 2 * DFF, n) + [_resident((D, 2 * DFF)), wspec, bspec] + _halo_specs(tb, DFF, n),
        out_specs=[pl.BlockSpec((tb, 2 * DFF), lambda i: (i, 0)), wspec, bspec, pl.BlockSpec((tb, D), lambda i: (i, 0))],
        out_shape=[jax.ShapeDtypeStruct((n, 2 * DFF), BF), jax.ShapeDtypeStruct((8, 2 * DFF), F32), jax.ShapeDtypeStruct((1, 2 * DFF), F32),
                   jax.ShapeDtypeStruct((n, D), F32)],
        compiler_params=_cp(),
    )(u, u, u, w_up, w8, bias, da, da, da)


def _loss_kernel(x1, gate, ff, target, w_down):
    n = x1.shape[0]
    tb = _tile(n, (512, 256, 128))

    def body(x_ref, g_ref, f_ref, t_ref, wd_ref, loss_ref, dy_ref, dff_ref, dg_ref, da_ref):
        err = x_ref[...] + g_ref[...] * f_ref[...] - t_ref[...]
        dy = err * (1.0 / D)
        dy_ref[...] = dy
        dff_ref[...] = (g_ref[...] * dy).astype(BF)
        da_ref[...] = lax.dot_general(dff_ref[...], wd_ref[...], (_DIMS["nt"], ((), ())), preferred_element_type=F32)

        @pl.when(pl.program_id(0) == 0)
        def _():
            loss_ref[...] = jnp.zeros_like(loss_ref)
            dg_ref[...] = jnp.zeros_like(dg_ref)
        part = 0.5 * jnp.sum(jnp.sum(err * err, axis=1, keepdims=True) * (1.0 / D), axis=0, keepdims=True)
        loss_ref[...] += jnp.broadcast_to(part, (1, 128))
        dg_ref[...] += jnp.sum(dy * f_ref[...], axis=0, keepdims=True)

    row = pl.BlockSpec((tb, D), lambda i: (i, 0))
    one = pl.BlockSpec((1, D), lambda i: (0, 0))
    return pl.pallas_call(
        body, grid=(n // tb,), name="loss",
        in_specs=[row, one, row, row, _resident((DFF, D))],
        out_specs=[pl.BlockSpec((1, 128), lambda i: (0, 0)), row, row, one, pl.BlockSpec((tb, DFF), lambda i: (i, 0))],
        out_shape=[jax.ShapeDtypeStruct((1, 128), F32), jax.ShapeDtypeStruct((n, D), F32),
                   jax.ShapeDtypeStruct((n, D), BF), jax.ShapeDtypeStruct((1, D), F32), jax.ShapeDtypeStruct((n, DFF), F32)],
        compiler_params=_cp(),
    )(x1, gate, ff, target, w_down)


def _rope_tables(nlat, lc):
    inv_freq = (np.float32(ROPE_BASE) ** (-np.arange(32, dtype=np.float32) / np.float32(32))).astype(np.float32)
    ar = np.arange(nlat // GRID_W, dtype=np.float32)[:, None] * inv_freq
    ac = np.arange(GRID_W, dtype=np.float32)[:, None] * inv_freq
    by_row = lambda a: jnp.repeat(jnp.asarray(a, F32), GRID_W, axis=0)
    by_col = lambda a: jnp.tile(jnp.asarray(a, F32), (nlat // GRID_W, 1))
    cos = jnp.concatenate([by_row(np.cos(ar)), by_row(np.cos(ar)), by_col(np.cos(ac)), by_col(np.cos(ac))], axis=1)
    sin = jnp.concatenate([by_row(-np.sin(ar)), by_row(np.sin(ar)), by_col(-np.sin(ac)), by_col(np.sin(ac))], axis=1)
    cos = jnp.concatenate([cos, jnp.ones((lc, HD), F32)], axis=0)
    sin = jnp.concatenate([sin, jnp.zeros((lc, HD), F32)], axis=0)
    return cos, sin


def _pad_rows8(w):
    return jnp.concatenate([w, jnp.zeros((8 - w.shape[0], w.shape[1]), w.dtype)], axis=0)


def _pack_w_in(w):
    cuts = [sum(IN_SIZES[:i]) for i in range(len(IN_SIZES) + 1)]
    qkv, gt, b, a, q, k, v, mg = [w[:, cuts[i]:cuts[i + 1]] for i in range(len(IN_SIZES))]
    return jnp.concatenate([qkv, gt, q, mg, k, v, b, a, jnp.zeros((w.shape[0], PW - O_BA - 32), w.dtype)], axis=1)


def _unpack_w_in(g):
    return jnp.concatenate([g[:, O_QKV:O_GT], g[:, O_GT:O_Q], g[:, O_BA:O_BA + 32], g[:, O_Q:O_MG], g[:, O_K:O_V],
                            g[:, O_V:O_BA], g[:, O_MG:O_K]], axis=1)


def _local_step(x, ctx, mod_x, mod_c, target, project_in, project_back,
                norm_mix, norm_ffn, dn_conv, a_log, dt_bias, dn_norm, q_norm, k_norm, sink, ffn_conv, ffn_conv_b):
    L, LC = x.shape[0], ctx.shape[0]
    T = L + LC
    seg = lambda r: jnp.stack([mod_x[r], mod_c[r]])[:, None, :]
    sh_a, sc_a = seg(0), seg(1)
    g_a, g_f = mod_x[2][None], mod_x[5][None]
    sh_f, sc_f = mod_x[3][None], mod_x[4][None]
    cos, sin = _rope_tables(L, LC)
    dnc8 = _pad_rows8(dn_conv)
    ffc8 = _pad_rows8(ffn_conv)
    gate_row = lambda a: jnp.concatenate([jnp.zeros((1, 16), F32), a.reshape(1, 16), jnp.zeros((1, 96), F32)], axis=1)
    alog_row, dt_row = gate_row(a_log), gate_row(dt_bias)
    sinkb = jnp.concatenate([jnp.broadcast_to(sink.reshape(KVH, GRP, 1), (KVH, GRP, 128)), jnp.zeros((KVH, 8 - GRP, 128), F32)], axis=1)

    h1 = _norm_mod_fwd(x, ctx, norm_mix, sh_a, sc_a, "norm_mix_fwd")
    p, (w_in_p, w_bdn, w_bat, w_out, w_up, w_down) = project_in(h1)
    q, k, v, gb = _dn_pre_fwd(p, dnc8, alog_row, dt_row, (L, LC))
    wy = _dn1_fwd(q, k, v, gb)
    scans = _dn2_fwd([t[:6] for t in wy], L)
    o_dir = [s[0] for s in scans]
    y_dn, z_dn = _ghn_fwd(o_dir[0], o_dir[1], p, dn_norm, w_bdn, L)
    qr, kr, vb = _attn_prep_fwd(p, q_norm, k_norm, cos, sin)
    y_at, z_at = _attn_fwd(qr, kr, vb, sinkb, w_bat, L)
    merged, mix = _merge_fwd(z_dn, z_at, p, w_out, L)
    x1, h2 = _resid_norm_fwd(x, g_a, mix, norm_ffn, sh_f, sc_f)
    u_raw, act, ff = _ffn_fwd(h2, w_up, ffc8, ffn_conv_b, w_down)
    loss_row, dy, dff, dg_f, dact = _loss_kernel(x1, g_f, ff, target, w_down)

    g_down = _mm(act, dff, form="tn", out_dtype=BF, name="g_ffn_down")
    du_raw, g_ffc8, g_ffb, dh2 = _ffn_bwd(u_raw, w_up, ffc8, ffn_conv_b, dact)
    g_up = _mm(h2, du_raw, form="tn", out_dtype=BF, name="g_ffn_up")
    dx1, dmix, dg_a, g_nffn, dsh_f, dsc_f, dmerged = _resid_norm_bwd(x1, g_a, mix, norm_ffn, sh_f, sc_f, dh2, dy, w_out)

    g_out = _mm(merged, dmix, form="tn", out_dtype=BF, name="g_w_out")
    dz_dn, dz_at, dmg, dy_dn, dy_at = _merge_bwd(z_dn, z_at, p, dmerged, w_bdn, w_bat, L)
    g_bdn = _mm(y_dn, dz_dn, form="tn", out_dtype=BF, name="g_branch_dn")
    g_bat = _mm(y_at, dz_at, form="tn", out_dtype=BF, name="g_branch_at")
    dqr, dkp, dvp, dkx, dvx, dsink = _attn_bwd(qr, kr, vb, sinkb, dy_at, L)
    dq_raw, dk_raw, dv_raw, g_qn, g_kn = _attn_prep_bwd(p, q_norm, k_norm, cos, sin, dqr, dkp, dvp, dkx, dvx, L)
    do, dgt, g_dnn = _ghn_bwd(o_dir[0], o_dir[1], p, dn_norm, dy_dn, L)
    cots = _dn2_bwd([wy[d][:6] + (scans[d][1],) for d in (0, 1)], do, L)
    dq, dk, dv, dgb = _dn1_bwd(q, k, v, gb, [t[6] for t in wy], cots)
    dp, g_dnc8, g_alog, g_dt = _dn_pre_bwd(p, dnc8, alog_row, dt_row, dq, dk, dv, dgb, (dgt, dq_raw, dmg, dk_raw, dv_raw), (L, LC))
    big, dh1 = project_back(h1, dp, w_in_p, (g_bdn, g_bat, g_out, g_up, g_down))
    grad_x, g_nmix_x, dsh_a, dsc_a = _norm_mod_bwd(x, norm_mix, mod_x[0][None], mod_x[1][None], dh1, row0=0,
                                                   name="norm_mix_bwd", residual=dx1)
    g_nmix_c, dsh_c, dsc_c = _norm_mod_bwd(ctx, norm_mix, mod_c[0][None], mod_c[1][None], dh1, row0=L, name="norm_mix_bwd_ctx")
    g_nmix = g_nmix_x + g_nmix_c

    zero = jnp.zeros((D,), F32)
    dmod_x = jnp.stack([dsh_a[0], dsc_a[0], dg_a[0], dsh_f[0], dsc_f[0], dg_f[0]])
    dmod_c = jnp.stack([dsh_c[0], dsc_c[0], zero, zero, zero, zero])
    small = dict(
        dmod_x=dmod_x, dmod_c=dmod_c, norm_mix=g_nmix, norm_ffn=g_nffn, dn_conv=g_dnc8[:5], dn_a_log=g_alog[0, 16:32].reshape(2, 8),
        dn_dt_bias=g_dt[0, 16:32].reshape(2, 8), dn_norm=g_dnn, q_norm=g_qn, k_norm=g_kn,
        attn_sink=jnp.sum(dsink[:, :GRP, :], axis=2).reshape(1, NH), ffn_conv=g_ffc8[:3], ffn_conv_b=g_ffb)
    return loss_row[0, 0], grad_x, big, small


def _exchange(arrays, scatter, name):
    n = len(arrays)

    def body(*refs):
        args = (refs[:n], refs[n:2 * n], *refs[2 * n:], scatter)
        _exchange_start(*args)
        _exchange_wait(*args)

    hbm = pl.BlockSpec(memory_space=pl.ANY)
    out_shape, sems = _exchange_shapes(arrays, scatter)
    return pl.pallas_call(body, name=name, in_specs=[hbm] * n, out_specs=[hbm] * n, out_shape=out_shape,
                          scratch_shapes=sems)(*arrays)


def _gather_two_level(arrays, name):
    n = len(arrays)

    def body(*refs):
        ins, outs = refs[:n], refs[n:2 * n]
        send_sems, recv_sems, local_sems = refs[2 * n:]
        x, y, c = lax.axis_index("x"), lax.axis_index("y"), lax.axis_index("c")
        sibling = (x, y, 1 - c)
        chips = [(1 - x, y), (x, 1 - y), (1 - x, 1 - y)]

        def copy(k, j, block, to, src=None):
            slot = outs[k].at[4 * block[0] + 2 * block[1] + block[2]]
            return pltpu.make_async_remote_copy(src_ref=slot if src is None else src, dst_ref=slot,
                                                send_sem=send_sems.at[7 * k + j], recv_sem=recv_sems.at[7 * k + j],
                                                device_id=to, device_id_type=MESH)

        mine = [pltpu.make_async_copy(ins[k], outs[k].at[4 * x + 2 * y + c], local_sems.at[k]) for k in range(n)]
        for cp in mine:
            cp.start()
        first = []
        for k in range(n):
            first.append(copy(k, 0, (x, y, c), sibling, src=ins[k]))
            first += [copy(k, 1 + j, (x, y, c), (*chip, c), src=ins[k]) for j, chip in enumerate(chips)]
        for cp in first:
            cp.start()
        passed = []
        for k in range(n):
            for j, chip in enumerate(chips):
                copy(k, 1 + j, (*chip, c), (x, y, c)).wait_recv()
                forward = copy(k, 4 + j, (*chip, c), sibling)
                forward.start()
                passed.append(forward)
        for k in range(n):
            copy(k, 0, sibling, (x, y, c)).wait_recv()
            for j, chip in enumerate(chips):
                copy(k, 4 + j, (*chip, 1 - c), (x, y, c)).wait_recv()
        for cp in first + passed:
            cp.wait_send()
        for cp in mine:
            cp.wait()

    hbm = pl.BlockSpec(memory_space=pl.ANY)
    out_shape, sems = _exchange_shapes(arrays, False)
    return pl.pallas_call(body, name=name, in_specs=[hbm] * n, out_specs=[hbm] * n, out_shape=out_shape,
                          scratch_shapes=sems)(*arrays)


def _ada_fwd(c16, w_ada, b_ada):
    def body(c_ref, w_ref, b_ref, o_ref):
        o_ref[...] = _dot_hi(jax.nn.silu(c_ref[...]), w_ref[...]) + b_ref[...]

    return pl.pallas_call(body, name="ada_fwd", out_shape=jax.ShapeDtypeStruct((16, w_ada.shape[1]), F32))(c16, w_ada, b_ada)


def _ada_bwd(c16, w_ada, dmx, dmc):
    def body(c_ref, w_ref, dmx_ref, dmc_ref, gw_ref, pc_ref):
        dmc_tot = dmc_ref[0:1, :]
        for d in range(1, N_DEV):
            dmc_tot = dmc_tot + dmc_ref[d:d + 1, :]
        dm16 = jnp.concatenate([dmx_ref[...], jnp.broadcast_to(dmc_tot, (8, dmc_tot.shape[1]))], axis=0)
        row = lax.broadcasted_iota(jnp.int32, dm16.shape, 0)
        dm16 = jnp.where(row <= 8, dm16, 0.0)
        s = jax.nn.silu(c_ref[...])
        gw_ref[...] = lax.dot_general(s, dm16, (_DIMS["tn"], ((), ())), precision=HI, preferred_element_type=F32)
        pc = lax.dot_general(dm16, w_ref[...], (_DIMS["nt"], ((), ())), precision=HI, preferred_element_type=F32)
        pc_ref[...] = pc[8:9, :]

    return pl.pallas_call(body, name="ada_bwd", out_shape=[jax.ShapeDtypeStruct(w_ada.shape, F32), jax.ShapeDtypeStruct((1, D), F32)],
                          compiler_params=_cp())(c16, w_ada, dmx, dmc)


def _cctx_grad(pc_all, c_ctx_row):
    def body(pc_ref, c_ref, g_ref):
        tot = pc_ref[0]
        for d in range(1, N_DEV):
            tot = tot + pc_ref[d]
        _, vjp = jax.vjp(jax.nn.silu, c_ref[...])
        g_ref[...] = vjp(tot)[0]

    return pl.pallas_call(body, name="cctx_grad", out_shape=jax.ShapeDtypeStruct((1, D), F32))(pc_all, c_ctx_row)


def _adamw(parts, w, m, v, name):
    ns, R, C = parts.shape
    tb = _tile(R, (128, 64, 32, 16, 8))

    def body(p_ref, w_ref, m_ref, v_ref, g_ref, d_ref, mo_ref, vo_ref):
        g = p_ref[0].astype(F32)
        for s in range(1, ns):
            g = g + p_ref[s].astype(F32)
        m2 = ADAM_B1 * m_ref[...] + (1.0 - ADAM_B1) * g
        v2 = ADAM_B2 * v_ref[...] + (1.0 - ADAM_B2) * jnp.square(g)
        m_hat = m2 / (1.0 - ADAM_B1 ** ADAM_STEP)
        v_hat = v2 / (1.0 - ADAM_B2 ** ADAM_STEP)
        g_ref[...] = g
        d_ref[...] = -ADAM_LR * (m_hat / (jnp.sqrt(v_hat) + ADAM_EPS) + ADAM_WD * w_ref[...])
        mo_ref[...] = m2
        vo_ref[...] = v2

    row = pl.BlockSpec((tb, C), lambda i: (i, 0))
    return pl.pallas_call(
        body, grid=(R // tb,), name=name,
        in_specs=[pl.BlockSpec((ns, tb, C), lambda i: (0, i, 0)), row, row, row], out_specs=[row] * 4,
        out_shape=[jax.ShapeDtypeStruct((R, C), F32)] * 4, compiler_params=_cp(),
    )(parts, w, m, v)


_SMALL = (("dmod_x", 6 * D), ("dmod_c", 6 * D), ("b_ada", 6 * D), ("norm_mix", D), ("norm_ffn", D), ("dn_a_log", 16),
          ("dn_dt_bias", 16), ("dn_norm", HD), ("q_norm", HD), ("k_norm", HD), ("attn_sink", NH), ("ffn_conv_b", 2 * DFF),
          ("dn_conv", 5 * 3 * D), ("ffn_conv", 3 * 2 * DFF))
_SMALL_ROWS = -(-sum(n for _, n in _SMALL) // 1024) * 8


def _pack_small(d):
    flat = jnp.concatenate([d[k].reshape(-1).astype(F32) if k in d else jnp.zeros((n,), F32) for k, n in _SMALL])
    return jnp.concatenate([flat, jnp.zeros((_SMALL_ROWS * 128 - flat.shape[0],), F32)]).reshape(_SMALL_ROWS, 128)


def _unpack_small(a):
    flat = a.reshape(a.shape[:-2] + (-1,))
    out, off = {}, 0
    for k, n in _SMALL:
        out[k] = flat[..., off:off + n]
        off += n
    return out


def kernel(x, c, ctx, c_ctx, w_ada, b_ada, norm_mix, norm_ffn, w_in, dn_conv, dn_a_log, dn_dt_bias, dn_norm, q_norm, k_norm, attn_sink, w_branch_dn, w_branch_attn, w_out, ffn_up, ffn_conv, ffn_conv_b, ffn_down, loss_target, m_c_ctx, m_w_ada, m_b_ada, m_norm_mix, m_norm_ffn, m_w_in, m_dn_conv, m_dn_a_log, m_dn_dt_bias, m_dn_norm, m_q_norm, m_k_norm, m_attn_sink, m_w_branch_dn, m_w_branch_attn, m_w_out, m_ffn_up, m_ffn_conv, m_ffn_conv_b, m_ffn_down, v_c_ctx, v_w_ada, v_b_ada, v_norm_mix, v_norm_ffn, v_w_in, v_dn_conv, v_dn_a_log, v_dn_dt_bias, v_dn_norm, v_q_norm, v_k_norm, v_attn_sink, v_w_branch_dn, v_w_branch_attn, v_w_out, v_ffn_up, v_ffn_conv, v_ffn_conv_b, v_ffn_down):
    me = 4 * lax.axis_index("x") + 2 * lax.axis_index("y") + lax.axis_index("c")
    ada_cols = w_ada.shape[2]

    cols = lambda a: jnp.swapaxes(a, 0, 1).reshape(a.shape[1], -1)
    rows = lambda a: a.reshape(-1, a.shape[2])
    col_blocks = lambda g: jnp.swapaxes(g.reshape(g.shape[0], N_DEV, -1), 0, 1)
    row_blocks = lambda g: g.reshape(N_DEV, -1, g.shape[1])

    gathered = _gather_two_level([w_in[0].astype(BF), c, dn_conv[0], ffn_conv[0]], name="gather_first")
    w_in_packed = _pack_w_in(cols(gathered[0]))
    c_all = gathered[1][:, 0, :]

    def project_in(h1):
        p, rest = _mm(h1, w_in_packed, form="nn", out_dtype=F32, name="in_proj",
                      exchange=([w_branch_dn[0].astype(BF), w_branch_attn[0].astype(BF), w_out[0].astype(BF),
                                 ffn_up[0].astype(BF), ffn_down[0].astype(BF)], False))
        return p, (w_in_packed, rows(rest[0]), rows(rest[1]), rows(rest[2]), cols(rest[3]), rows(rest[4]))

    def project_back(h1, dp, w_in_p, grads):
        g_bdn, g_bat, g_out, g_up, g_down = grads
        g_in, landed_rest = _mm(h1, dp, form="tn", out_dtype=BF, name="g_w_in",
                                exchange=([row_blocks(g_bdn), row_blocks(g_bat), row_blocks(g_out), col_blocks(g_up),
                                           row_blocks(g_down)], True))
        dh1, landed_in = _mm(dp, w_in_p, form="nt", out_dtype=F32, name="d_h1",
                             exchange=([col_blocks(_unpack_w_in(g_in))], True))
        return [landed_in[0]] + landed_rest, dh1

    c16 = jnp.concatenate([c_all, c_ctx[None], jnp.zeros((7, D), F32)], axis=0)
    b_loc = lax.dynamic_slice_in_dim(b_ada, me * ada_cols, ada_cols, axis=1)
    mod_part = _ada_fwd(c16, w_ada[0], b_loc)
    mod_all = cols(_exchange([mod_part], scatter=False, name="gather_mod")[0])
    mod_x = lax.dynamic_slice_in_dim(mod_all, me, 1, axis=0).reshape(6, D)
    mod_c = mod_all[8].reshape(6, D)

    loss_loc, grad_x, landed, small = _local_step(
        x[0], ctx[0], mod_x, mod_c, loss_target[0], project_in, project_back,
        norm_mix, norm_ffn, cols(gathered[2]), dn_a_log[0], dn_dt_bias[0], dn_norm, q_norm, k_norm, attn_sink[0], cols(gathered[3]),
        ffn_conv_b)
    loss = lax.psum(loss_loc, ("x", "y", "c"))

    res = {}
    res["w_in"] = _adamw(landed[0], w_in[0], m_w_in[0], v_w_in[0], "adamw_w_in")
    res["w_branch_dn"] = _adamw(landed[1], w_branch_dn[0], m_w_branch_dn[0], v_w_branch_dn[0], "adamw_w_branch_dn")
    res["w_branch_attn"] = _adamw(landed[2], w_branch_attn[0], m_w_branch_attn[0], v_w_branch_attn[0], "adamw_w_branch_attn")
    res["w_out"] = _adamw(landed[3], w_out[0], m_w_out[0], v_w_out[0], "adamw_w_out")
    res["ffn_up"] = _adamw(landed[4], ffn_up[0], m_ffn_up[0], v_ffn_up[0], "adamw_ffn_up")
    res["ffn_down"] = _adamw(landed[5], ffn_down[0], m_ffn_down[0], v_ffn_down[0], "adamw_ffn_down")

    small = dict(small)
    small["b_ada"] = small["dmod_x"] + small["dmod_c"]
    parts = _exchange([_pack_small(small)], scatter=False, name="gather_small")[0]
    per_dev = _unpack_small(parts)
    given = dict(b_ada=(b_ada, m_b_ada, v_b_ada), norm_mix=(norm_mix, m_norm_mix, v_norm_mix), norm_ffn=(norm_ffn, m_norm_ffn, v_norm_ffn),
                 dn_a_log=(dn_a_log, m_dn_a_log, v_dn_a_log), dn_dt_bias=(dn_dt_bias, m_dn_dt_bias, v_dn_dt_bias),
                 dn_norm=(dn_norm, m_dn_norm, v_dn_norm), q_norm=(q_norm, m_q_norm, v_q_norm), k_norm=(k_norm, m_k_norm, v_k_norm),
                 attn_sink=(attn_sink, m_attn_sink, v_attn_sink), ffn_conv_b=(ffn_conv_b, m_ffn_conv_b, v_ffn_conv_b))
    packs = [_pack_small({k: t[j] for k, t in given.items()}) for j in range(3)]
    upd = [_unpack_small(a) for a in _adamw(parts, packs[0], packs[1], packs[2], "adamw_small")]
    for k, t in given.items():
        res[k] = tuple(u[k].reshape(t[0].shape) for u in upd)
    dnc = lax.dynamic_slice_in_dim(upd[0]["dn_conv"].reshape(5, 3 * D), me * dn_conv.shape[2], dn_conv.shape[2], axis=1)
    ffc = lax.dynamic_slice_in_dim(upd[0]["ffn_conv"].reshape(3, 2 * DFF), me * ffn_conv.shape[2], ffn_conv.shape[2], axis=1)
    r8 = lambda a: _pad_rows8(a)
    t = _adamw(r8(dnc)[None], r8(dn_conv[0]), r8(m_dn_conv[0]), r8(v_dn_conv[0]), "adamw_dn_conv")
    res["dn_conv"] = tuple(a[:5][None] for a in t)
    t = _adamw(r8(ffc)[None], r8(ffn_conv[0]), r8(m_ffn_conv[0]), r8(v_ffn_conv[0]), "adamw_ffn_conv")
    res["ffn_conv"] = tuple(a[:3][None] for a in t)

    dmx = lax.dynamic_slice_in_dim(per_dev["dmod_x"], me * ada_cols, ada_cols, axis=1)
    dmc = lax.dynamic_slice_in_dim(per_dev["dmod_c"], me * ada_cols, ada_cols, axis=1)
    g_ada, pc = _ada_bwd(c16, w_ada[0], dmx, dmc)
    res["w_ada"] = _adamw(g_ada[None], w_ada[0], m_w_ada[0], v_w_ada[0], "adamw_w_ada")
    pc_all = _exchange([pc], scatter=False, name="gather_cctx")[0]
    g_cctx = _cctx_grad(pc_all, c_ctx[None])
    r8b = lambda a: jnp.broadcast_to(a, (8, D))
    t = _adamw(r8b(g_cctx)[None], r8b(c_ctx[None]), r8b(m_c_ctx[None]), r8b(v_c_ctx[None]), "adamw_c_ctx")
    res["c_ctx"] = tuple(a[0] for a in t)

    names = ("c_ctx", "w_ada", "b_ada", "norm_mix", "norm_ffn", "w_in", "dn_conv", "dn_a_log", "dn_dt_bias", "dn_norm", "q_norm",
             "k_norm", "attn_sink", "w_branch_dn", "w_branch_attn", "w_out", "ffn_up", "ffn_conv", "ffn_conv_b", "ffn_down")
    lead = ("w_ada", "w_in", "w_branch_dn", "w_branch_attn", "w_out", "ffn_up", "ffn_down")
    fix = lambda k, a: a[None] if k in lead else a
    outs = [loss, grad_x[None]]
    for j in range(4):
        outs += [fix(k, res[k][j]) for k in names]
    return tuple(outs)
```

```python
import functools

import jax
import jax.numpy as jnp
import numpy as np
from jax import lax
from jax.experimental import pallas as pl
from jax.experimental.pallas import tpu as pltpu

F32 = jnp.float32
BF = jnp.bfloat16
HI = lax.Precision.HIGHEST
MESH = pl.DeviceIdType.MESH

D = 1024
NH = 8
HD = 128
KVH = 2
GRP = 4
KV = KVH * HD
DFF = 2816
CB = 128
GRID_W = 64
ROPE_BASE = 10000.0
EPS = 1e-6
N_DEV = 8
PW = 8192
O_QKV, O_GT, O_Q, O_MG, O_K, O_V, O_BA = 0, 3072, 4096, 5120, 7168, 7424, 7680
IN_SIZES = (3072, 1024, 16, 16, 1024, 256, 256, 2048)
IN_DIM = sum(IN_SIZES)
ADAM_LR, ADAM_B1, ADAM_B2, ADAM_EPS, ADAM_WD, ADAM_STEP = 0.001, 0.9, 0.999, 1e-08, 0.01, 10
VMEM_LIMIT = 56 * 1024 * 1024


def _cp():
    return pltpu.CompilerParams(vmem_limit_bytes=VMEM_LIMIT)


def _tile(n, cands):
    for c in cands:
        if n % c == 0:
            return c
    return n


def _iota2(shape):
    return lax.broadcasted_iota(jnp.int32, shape, 0), lax.broadcasted_iota(jnp.int32, shape, 1)


_DIMS = {"nn": ((1,), (0,)), "nt": ((1,), (1,)), "tn": ((0,), (0,))}


def _exchange_copies(ins, outs, send_sems, recv_sems, local_sems, scatter, landings):
    x, y, c = lax.axis_index("x"), lax.axis_index("y"), lax.axis_index("c")
    me = 4 * x + 2 * y + c
    local, remote = [], []
    for k in range(len(ins)):
        local.append(pltpu.make_async_copy(ins[k].at[me] if scatter else ins[k], outs[k].at[me], local_sems.at[k]))
        for m in range(1, N_DEV):
            px = 1 - x if m & 4 else x
            py = 1 - y if m & 2 else y
            pc = 1 - c if m & 1 else c
            peer = 4 * px + 2 * py + pc
            src = ins[k].at[peer] if scatter else ins[k]
            sem = k * (N_DEV - 1) + m - 1
            push = pltpu.make_async_remote_copy(src_ref=src, dst_ref=outs[k].at[me], send_sem=send_sems.at[sem],
                                                recv_sem=recv_sems.at[sem], device_id=(px, py, pc), device_id_type=MESH)
            landing = None
            if landings:
                landing = pltpu.make_async_remote_copy(src_ref=src, dst_ref=outs[k].at[peer], send_sem=send_sems.at[sem],
                                                       recv_sem=recv_sems.at[sem], device_id=(px, py, pc), device_id_type=MESH)
            remote.append((push, landing))
    return local, remote


def _exchange_start(*args):
    local, remote = _exchange_copies(*args, landings=False)
    for cp in local:
        cp.start()
    for push, _ in remote:
        push.start()


def _exchange_wait(*args):
    local, remote = _exchange_copies(*args, landings=True)
    for _, landing in remote:
        landing.wait_recv()
    for push, _ in remote:
        push.wait_send()
    for cp in local:
        cp.wait()


def _exchange_shapes(arrays, scatter):
    out_shape = [jax.ShapeDtypeStruct(a.shape if scatter else (N_DEV,) + a.shape, a.dtype) for a in arrays]
    n = len(arrays)
    sems = [pltpu.SemaphoreType.DMA((n * (N_DEV - 1),)), pltpu.SemaphoreType.DMA((n * (N_DEV - 1),)), pltpu.SemaphoreType.DMA((n,))]
    return out_shape, sems


def _mm(a, b, *, form, out_dtype, name, tm=None, tn=None, tk=None, exchange=None):
    if form == "tn":
        K, M = a.shape
        N = b.shape[1]
    else:
        M, K = a.shape
        N = b.shape[0] if form == "nt" else b.shape[1]
    tm = tm or _tile(M, (1408, 1280, 1024, 640, 512, 256, 128))
    tn = tn or _tile(N, (1408, 1024, 512, 256, 128))
    tk = tk or _tile(K, (2048, 1408, 1280, 1024, 640, 512, 256, 128))
    ni, nj, nk = M // tm, N // tn, K // tk
    dims = (_DIMS[form], ((), ()))
    ex_arrays, scatter = exchange if exchange else ([], False)
    nx = len(ex_arrays)

    def body(a_ref, b_ref, *refs):
        ex_in, o_ref, ex_out, scratch = refs[:nx], refs[nx], refs[nx + 1:2 * nx + 1], refs[2 * nx + 1:]
        i, j, k = pl.program_id(0), pl.program_id(1), pl.program_id(2)
        if nx:
            sems = scratch[-3:]

            @pl.when((i == 0) & (j == 0) & (k == 0))
            def _():
                _exchange_start(ex_in, ex_out, *sems, scatter)

        part = lax.dot_general(a_ref[...].astype(BF), b_ref[...].astype(BF), dims, preferred_element_type=F32)
        if nk == 1:
            o_ref[...] = part.astype(out_dtype)
        else:
            acc_ref = scratch[0]

            @pl.when(k == 0)
            def _():
                acc_ref[...] = part

            @pl.when(k > 0)
            def _():
                acc_ref[...] += part

            @pl.when(k == nk - 1)
            def _():
                o_ref[...] = acc_ref[...].astype(out_dtype)

        if nx:
            @pl.when((i == ni - 1) & (j == nj - 1) & (k == nk - 1))
            def _():
                _exchange_wait(ex_in, ex_out, *sems, scatter)

    if form == "tn":
        a_spec = pl.BlockSpec((tk, tm), lambda i, j, k: (k, i))
    else:
        a_spec = pl.BlockSpec((tm, tk), lambda i, j, k: (i, k))
    if form == "nt":
        b_spec = pl.BlockSpec((tn, tk), lambda i, j, k: (j, k))
    else:
        b_spec = pl.BlockSpec((tk, tn), lambda i, j, k: (k, j))
    hbm = pl.BlockSpec(memory_space=pl.ANY)
    ex_shapes, ex_sems = _exchange_shapes(ex_arrays, scatter) if nx else ([], [])
    outs = pl.pallas_call(
        body, grid=(ni, nj, nk), name=name,
        in_specs=[a_spec, b_spec] + [hbm] * nx, out_specs=[pl.BlockSpec((tm, tn), lambda i, j, k: (i, j))] + [hbm] * nx,
        out_shape=[jax.ShapeDtypeStruct((M, N), out_dtype)] + ex_shapes,
        scratch_shapes=([] if nk == 1 else [pltpu.VMEM((tm, tn), F32)]) + ex_sems,
        compiler_params=_cp(),
    )(a, b, *ex_arrays)
    return (outs[0], list(outs[1:])) if nx else outs[0]


def _norm_mod_fn(x, nw, sh, sc):
    y = x * lax.rsqrt(jnp.mean(x * x, axis=-1, keepdims=True) + EPS)
    return (y * nw) * (1.0 + sc) + sh


def _norm_mod_fwd(x, ctx, nw, sh, sc, name):
    nlat = x.shape[0]
    T = nlat + ctx.shape[0]
    tb = _tile(ctx.shape[0], (256, 128))
    nlb = nlat // tb

    def body(x_ref, c_ref, nw_ref, sh_ref, sc_ref, h_ref):
        rows = jnp.where(pl.program_id(0) < nlb, x_ref[...], c_ref[...])
        h_ref[...] = _norm_mod_fn(rows, nw_ref[...], sh_ref[0], sc_ref[0]).astype(BF)

    seg = pl.BlockSpec((1, 1, D), lambda i: (jnp.where(i >= nlb, 1, 0), 0, 0))
    return pl.pallas_call(
        body, grid=(T // tb,), name=name,
        in_specs=[pl.BlockSpec((tb, D), lambda i: (jnp.minimum(i, nlb - 1), 0)),
                  pl.BlockSpec((tb, D), lambda i: (jnp.maximum(i - nlb, 0), 0)), pl.BlockSpec((1, D), lambda i: (0, 0)), seg, seg],
        out_specs=pl.BlockSpec((tb, D), lambda i: (i, 0)),
        out_shape=jax.ShapeDtypeStruct((T, D), BF),
    )(x, ctx, nw, sh, sc)


def _norm_mod_bwd(x, nw, sh, sc, dh, *, row0, name, residual=None):
    nrows = x.shape[0]
    tb = _tile(nrows, (512, 256, 128))
    b0 = row0 // tb

    def body(x_ref, nw_ref, sh_ref, sc_ref, dh_ref, *refs):
        dnw_ref, dsh_ref, dsc_ref = refs[-3:]
        _, vjp = jax.vjp(_norm_mod_fn, x_ref[...], nw_ref[...], sh_ref[...], sc_ref[...])
        dx, dnw, dsh, dsc = vjp(dh_ref[...])
        if residual is not None:
            refs[1][...] = dx + refs[0][...]

        @pl.when(pl.program_id(0) == 0)
        def _():
            dnw_ref[...] = jnp.zeros_like(dnw_ref)
            dsh_ref[...] = jnp.zeros_like(dsh_ref)
            dsc_ref[...] = jnp.zeros_like(dsc_ref)

        dnw_ref[...] += dnw
        dsh_ref[...] += dsh
        dsc_ref[...] += dsc

    dh_row = pl.BlockSpec((tb, D), lambda i: (b0 + i, 0))
    out_row = pl.BlockSpec((tb, D), lambda i: (i, 0))
    one = pl.BlockSpec((1, D), lambda i: (0, 0))
    with_dx = residual is not None
    return pl.pallas_call(
        body, grid=(nrows // tb,), name=name,
        in_specs=[out_row, one, one, one, dh_row] + [out_row] * with_dx, out_specs=[out_row] * with_dx + [one] * 3,
        out_shape=[jax.ShapeDtypeStruct((nrows, D), F32)] * with_dx + [jax.ShapeDtypeStruct((1, D), F32)] * 3,
        compiler_params=_cp(),
    )(x, nw, sh, sc, dh, *([residual] if with_dx else []))


def _resid_norm_fwd(x, gate, y, nw, sh, sc):
    n = y.shape[0]
    tb = _tile(n, (512, 256, 128))

    def body(x_ref, g_ref, y_ref, nw_ref, sh_ref, sc_ref, x1_ref, h_ref):
        x1 = x_ref[...] + g_ref[...] * y_ref[...]
        x1_ref[...] = x1
        h_ref[...] = _norm_mod_fn(x1, nw_ref[...], sh_ref[...], sc_ref[...]).astype(BF)

    row = pl.BlockSpec((tb, D), lambda i: (i, 0))
    one = pl.BlockSpec((1, D), lambda i: (0, 0))
    return pl.pallas_call(
        body, grid=(n // tb,), name="resid_norm_fwd",
        in_specs=[row, one, row, one, one, one], out_specs=[row, row],
        out_shape=[jax.ShapeDtypeStruct((n, D), F32), jax.ShapeDtypeStruct((n, D), BF)],
        compiler_params=_cp(),
    )(x, gate, y, nw, sh, sc)


def _resid_norm_bwd(x1, gate, y, nw, sh, sc, dh, dx1_direct, w_out):
    n = y.shape[0]
    tb = _tile(n, (512, 256, 128))

    def body(x1_ref, g_ref, y_ref, nw_ref, sh_ref, sc_ref, dh_ref, dd_ref, wo_ref,
             dx_ref, dy_ref, dg_ref, dnw_ref, dsh_ref, dsc_ref, dm_ref):
        _, vjp = jax.vjp(_norm_mod_fn, x1_ref[...], nw_ref[...], sh_ref[...], sc_ref[...])
        dxn, dnw, dsh, dsc = vjp(dh_ref[...])
        dx = dxn + dd_ref[...]
        dx_ref[...] = dx
        dy_ref[...] = (g_ref[...] * dx).astype(BF)
        dm_ref[...] = lax.dot_general(dy_ref[...], wo_ref[...], (_DIMS["nt"], ((), ())), preferred_element_type=F32)

        @pl.when(pl.program_id(0) == 0)
        def _():
            for r in (dg_ref, dnw_ref, dsh_ref, dsc_ref):
                r[...] = jnp.zeros_like(r)

        dg_ref[...] += jnp.sum(dx * y_ref[...], axis=0, keepdims=True)
        dnw_ref[...] += dnw
        dsh_ref[...] += dsh
        dsc_ref[...] += dsc

    row = pl.BlockSpec((tb, D), lambda i: (i, 0))
    one = pl.BlockSpec((1, D), lambda i: (0, 0))
    return pl.pallas_call(
        body, grid=(n // tb,), name="resid_norm_bwd",
        in_specs=[row, one, row, one, one, one, row, row, _resident((D, D))], out_specs=[row, row] + [one] * 4 + [row],
        out_shape=[jax.ShapeDtypeStruct((n, D), F32), jax.ShapeDtypeStruct((n, D), BF)] + [jax.ShapeDtypeStruct((1, D), F32)] * 4
        + [jax.ShapeDtypeStruct((n, D), F32)],
        compiler_params=_cp(),
    )(x1, gate, y, nw, sh, sc, dh, dx1_direct, w_out)


HALO = 8


def _halo_specs(tb, width, nrows, col=0, halo=HALO):
    r8 = tb // halo
    cur = pl.BlockSpec((tb, width), lambda i: (i, col))
    prev = pl.BlockSpec((halo, width), lambda i: (jnp.maximum(i * r8 - 1, 0), col))
    nxt = pl.BlockSpec((halo, width), lambda i: (jnp.minimum((i + 1) * r8, nrows // halo - 1), col))
    return [cur, prev, nxt]


def _segment_edges(seg_rows, tb):
    bounds = [0]
    for s in seg_rows:
        bounds.append(bounds[-1] + s // tb)
    return bounds[:-1], [b - 1 for b in bounds[1:]]


def _keep_halos(i, starts, ends):
    keep_p = functools.reduce(lambda a, b: a & b, [i != s for s in starts])
    keep_n = functools.reduce(lambda a, b: a & b, [i != e for e in ends])
    return keep_p, keep_n


def _ext_rows(refs, cols, keep):
    cur_ref, prev_ref, next_ref = refs
    p = jnp.where(keep[0], prev_ref[:, cols].astype(F32), 0.0)
    n = jnp.where(keep[1], next_ref[:, cols].astype(F32), 0.0)
    return jnp.concatenate([p, cur_ref[:, cols].astype(F32), n], axis=0)


def _shifted_rows(xe, width, transpose=False):
    r = width // 2
    n = xe.shape[0]
    out = []
    for j in range(width):
        s = ((j - r) if transpose else (r - j)) % n
        out.append(xe if s == 0 else pltpu.roll(xe, s, 0))
    return out


def _conv_rows(shifted, w_ref, cols):
    acc = None
    for j, xs in enumerate(shifted):
        term = xs * w_ref[j:j + 1, cols]
        acc = term if acc is None else acc + term
    return acc


def _tap_grads(dcur, shifted, tb):
    return [jnp.sum(dcur * xs[HALO:HALO + tb], axis=0, keepdims=True) for xs in shifted]


def _softplus(x):
    return jnp.maximum(x, 0.0) + jnp.log(1.0 + jnp.exp(-jnp.abs(x)))


def _gates_fn(ba, alog_row, dt_row):
    col = lax.broadcasted_iota(jnp.int32, ba.shape, 1)
    beta = jax.nn.sigmoid(ba)
    g = -jnp.exp(alog_row) * _softplus(ba + dt_row)
    return jnp.where(col < 16, beta, jnp.where(col < 32, g, 0.0))


def _qkv_post_fn(c, kind):
    y = jax.nn.silu(c)
    if kind == 2:
        return y
    n = y * lax.rsqrt(jnp.sum(y * y, axis=-1, keepdims=True) + EPS)
    return n * (HD ** -0.5) if kind == 0 else n


DN_TAPS = 5
FFN_TAPS = 3


def _dn_pre_fwd(p, w8, alog_row, dt_row, seg_rows):
    T = p.shape[0]
    tb = _tile(T, (256, 128))
    starts, ends = _segment_edges(seg_rows, tb)

    def body(cur_ref, prev_ref, next_ref, ba_ref, w_ref, al_ref, dt_ref, q_ref, k_ref, v_ref, gb_ref):
        keep = _keep_halos(pl.program_id(0), starts, ends)
        outs = (q_ref, k_ref, v_ref)
        for kind in range(3):
            for h in range(NH):
                cols = slice(kind * D + h * HD, kind * D + (h + 1) * HD)
                xe = _ext_rows((cur_ref, prev_ref, next_ref), cols, keep)
                conv = _conv_rows(_shifted_rows(xe, DN_TAPS), w_ref, cols)[HALO:HALO + tb]
                outs[kind][:, h * HD:(h + 1) * HD] = _qkv_post_fn(conv, kind)
        gb_ref[...] = _gates_fn(ba_ref[...], al_ref[...], dt_ref[...])

    row = pl.BlockSpec((tb, D), lambda i: (i, 0))
    one = pl.BlockSpec((1, 128), lambda i: (0, 0))
    return pl.pallas_call(
        body, grid=(T // tb,), name="dn_pre_fwd",
        in_specs=_halo_specs(tb, 3 * D, T) + [pl.BlockSpec((tb, 128), lambda i: (i, O_BA // 128)),
                                              pl.BlockSpec((8, 3 * D), lambda i: (0, 0)), one, one],
        out_specs=[row, row, row, pl.BlockSpec((tb, 128), lambda i: (i, 0))],
        out_shape=[jax.ShapeDtypeStruct((T, D), F32)] * 3 + [jax.ShapeDtypeStruct((T, 128), F32)],
        compiler_params=_cp(),
    )(p, p, p, p, w8, alog_row, dt_row)


def _dn_pre_bwd(p, w8, alog_row, dt_row, dq, dk, dv, dgb, others, seg_rows):
    T = p.shape[0]
    tb = _tile(T, (256, 128))
    starts, ends = _segment_edges(seg_rows, tb)
    other_cols = (O_GT, O_Q, O_MG, O_K, O_V)
    assert [o.shape[1] for o in others] == [O_Q - O_GT, O_MG - O_Q, O_K - O_MG, O_V - O_K, O_BA - O_V]

    def body(cur_ref, prev_ref, next_ref, ba_ref, w_ref, al_ref, dt_ref,
             dq_c, dq_p, dq_n, dk_c, dk_p, dk_n, dv_c, dv_p, dv_n, dgb_ref, gt_ref, q_ref, mg_ref, k_ref, v_ref,
             dx_ref, dw_ref, dal_ref, ddt_ref):
        i = pl.program_id(0)
        for c0, ref in zip(other_cols, (gt_ref, q_ref, mg_ref, k_ref, v_ref)):
            dx_ref[:, c0:c0 + ref.shape[1]] = ref[...]
        dx_ref[:, O_BA + 128:] = jnp.zeros((tb, PW - O_BA - 128), BF)
        keep = _keep_halos(i, starts, ends)

        @pl.when(i == 0)
        def _():
            dw_ref[...] = jnp.zeros_like(dw_ref)
            dal_ref[...] = jnp.zeros_like(dal_ref)
            ddt_ref[...] = jnp.zeros_like(ddt_ref)

        douts = ((dq_c, dq_p, dq_n), (dk_c, dk_p, dk_n), (dv_c, dv_p, dv_n))
        for kind in range(3):
            for h in range(NH):
                cols = slice(kind * D + h * HD, kind * D + (h + 1) * HD)
                xe = _ext_rows((cur_ref, prev_ref, next_ref), cols, keep)
                shifted = _shifted_rows(xe, DN_TAPS)
                conv = _conv_rows(shifted, w_ref, cols)
                dye = _ext_rows(douts[kind], slice(h * HD, (h + 1) * HD), keep)
                _, vjp = jax.vjp(functools.partial(_qkv_post_fn, kind=kind), conv)
                dce = vjp(dye)[0]
                dx_ref[:, cols] = _conv_rows(_shifted_rows(dce, DN_TAPS, transpose=True), w_ref, cols)[HALO:HALO + tb].astype(BF)
                for j, g in enumerate(_tap_grads(dce[HALO:HALO + tb], shifted, tb)):
                    dw_ref[j:j + 1, cols] += g
        _, vjp = jax.vjp(_gates_fn, ba_ref[...], al_ref[...], dt_ref[...])
        dba, dal, ddt = vjp(dgb_ref[...])
        dx_ref[:, O_BA:O_BA + 128] = dba.astype(BF)
        dal_ref[...] += dal
        ddt_ref[...] += ddt

    one = pl.BlockSpec((1, 128), lambda i: (0, 0))
    nar = pl.BlockSpec((tb, 128), lambda i: (i, 0))
    wspec = pl.BlockSpec((8, 3 * D), lambda i: (0, 0))
    return pl.pallas_call(
        body, grid=(T // tb,), name="dn_pre_bwd",
        in_specs=_halo_specs(tb, 3 * D, T) + [pl.BlockSpec((tb, 128), lambda i: (i, O_BA // 128)), wspec, one, one]
        + _halo_specs(tb, D, T) * 3 + [nar] + [pl.BlockSpec((tb, o.shape[1]), lambda i: (i, 0)) for o in others],
        out_specs=[pl.BlockSpec((tb, PW), lambda i: (i, 0)), wspec, one, one],
        out_shape=[jax.ShapeDtypeStruct((T, PW), BF), jax.ShapeDtypeStruct((8, 3 * D), F32),
                   jax.ShapeDtypeStruct((1, 128), F32), jax.ShapeDtypeStruct((1, 128), F32)],
        compiler_params=_cp(),
    )(p, p, p, p, w8, alog_row, dt_row, dq, dq, dq, dk, dk, dk, dv, dv, dv, dgb, *others)


def _dot_hi(a, b):
    return jnp.dot(a, b, precision=HI, preferred_element_type=F32)


def _bf_product(a, b, form):
    return lax.dot_general(a.astype(BF), b.astype(BF), (_DIMS[form], ((), ())), preferred_element_type=F32)


def _dot_tn_bf(a, b):
    return _bf_product(a, b, "tn")


@jax.custom_vjp
def _dot_bf(a, b):
    return _bf_product(a, b, "nn")


@jax.custom_vjp
def _dot_nt_bf(a, b):
    return _bf_product(a, b, "nt")


_dot_bf.defvjp(lambda a, b: (_bf_product(a, b, "nn"), (a, b)),
               lambda res, dc: (_bf_product(dc, res[1], "nt").astype(res[0].dtype), _bf_product(res[0], dc, "tn").astype(res[1].dtype)))
_dot_nt_bf.defvjp(lambda a, b: (_bf_product(a, b, "nt"), (a, b)),
                  lambda res, dc: (_bf_product(dc, res[1], "nn").astype(res[0].dtype), _bf_product(dc, res[0], "tn").astype(res[1].dtype)))


def _dot_h3(a, b):
    return jnp.dot(a, b, precision=lax.Precision.HIGH, preferred_element_type=F32)


def _dot_split(fine, coarse, form):
    hi = fine.astype(BF)
    lo = (fine - hi.astype(F32)).astype(BF)
    cb = coarse.astype(BF)
    if form == "tn":
        return lax.dot_general(jnp.concatenate([cb, cb], axis=0), jnp.concatenate([hi, lo], axis=0),
                               (_DIMS["tn"], ((), ())), preferred_element_type=F32)
    parts = jnp.concatenate([hi, lo], axis=1)
    if form == "nt":
        return lax.dot_general(parts, jnp.concatenate([cb, cb], axis=1), (_DIMS["nt"], ((), ())), preferred_element_type=F32)
    return jnp.dot(parts, jnp.concatenate([cb, cb], axis=0), preferred_element_type=F32)


@jax.custom_vjp
def _mm_split(a, b):
    return _dot_split(a, b, "nn")


_mm_split.defvjp(lambda a, b: (_dot_split(a, b, "nn"), (a, b)),
                 lambda res, dc: (_dot_split(dc, res[1], "nt"), _dot_split(dc, res[0], "tn")))


def _unit_tri_inverses(mats):
    r, c = _iota2((CB, CB))
    eye = (r == c).astype(F32)
    a8 = [jnp.where((r // 8) == (c // 8), a, 0.0) for a in mats]
    a2 = [_dot_split(x, x, "nn") for x in a8]
    a4 = [_dot_split(x, x, "nn") for x in a2]
    t = [_dot_split(eye - x, eye + y, "nn") for x, y in zip(a8, a2)]
    t = [_dot_split(x, eye + y, "nn") for x, y in zip(t, a4)]
    b = 8
    while b < CB:
        mask = ((r // (2 * b)) == (c // (2 * b))) & ((r // b) != (c // b))
        te = [_dot_split(x, jnp.where(mask, a, 0.0), "nn") for x, a in zip(t, mats)]
        t = [x - _dot_split(y, x, "nn") for x, y in zip(t, te)]
        b *= 2
    return t


@jax.custom_vjp
def _saved_inverse(a, t):
    return t


_saved_inverse.defvjp(lambda a, t: (t, t),
                      lambda t, dt: (-_dot_split(_dot_split(dt, t, "nt"), t, "tn"), jnp.zeros_like(t)))


def _dn1_decay(gc, reverse):
    r, c = _iota2((CB, CB))
    incl = (c >= r) if reverse else (c <= r)
    return jnp.where(incl, jnp.exp(jnp.where(incl, gc - gc.T, 0.0)), 0.0)


def _dn1_heads(qs, ks, vs, betas, gcs, ts_saved, reverse, kks=None, qks=None):
    r, c = _iota2((CB, CB))
    strict = (c > r) if reverse else (c < r)
    decays = [_dn1_decay(gc, reverse) for gc in gcs]
    kks = kks or [_dot_nt_bf(k, k) for k in ks]
    systems = [jnp.where(strict, b * kk * dc, 0.0) for b, kk, dc in zip(betas, kks, decays)]
    if ts_saved is None:
        ts = _unit_tri_inverses(systems)
    else:
        ts = [_saved_inverse(a, t) for a, t in zip(systems, ts_saved)]
    egs = [jnp.exp(gc) for gc in gcs]
    us = [_mm_split(t, v * b) for t, v, b in zip(ts, vs, betas)]
    ws = [_mm_split(t, k * (b * eg)) for t, k, b, eg in zip(ts, ks, betas, egs)]
    qks = qks or [_dot_nt_bf(q, k) for q, k in zip(qs, ks)]
    last = 0 if reverse else CB - 1
    glogs = [jnp.sum(jnp.where(r == last, gc, 0.0), axis=0, keepdims=True) for gc in gcs]
    outs = [(u, w, q * eg, k * jnp.exp(gl - gc), qk * dc, jnp.exp(gl))
            for u, w, q, k, eg, gl, gc, qk, dc in zip(us, ws, qs, ks, egs, glogs, gcs, qks, decays)]
    return outs, ts


def _cum_matrix(upper):
    r, c = _iota2((CB, CB))
    return ((c >= r) if upper else (c <= r)).astype(F32)


def _lane_bcast(x, col):
    return jnp.broadcast_to(x[:, col:col + 1], x.shape)


_HEAD_SLICES = [slice(h * HD, (h + 1) * HD) for h in range(NH)]


def _dn1_fwd(q, k, v, gb):
    T = q.shape[0]
    nb = T // CB

    def body(q_ref, k_ref, v_ref, gb_ref, *out_refs):
        gbv = gb_ref[...]
        qs = [q_ref[:, sl] for sl in _HEAD_SLICES]
        ks = [k_ref[:, sl] for sl in _HEAD_SLICES]
        vs = [v_ref[:, sl] for sl in _HEAD_SLICES]
        kks = [_dot_nt_bf(x, x) for x in ks]
        qks = [_dot_nt_bf(x, y) for x, y in zip(qs, ks)]
        for d in (0, 1):
            u_ref, w_ref, qg_ref, kd_ref, qkd_ref, gl_ref, t_ref = out_refs[7 * d:7 * d + 7]
            gcum = _dot_h3(_cum_matrix(d == 1), gbv)
            betas = [_lane_bcast(gbv, d * NH + h) for h in range(NH)]
            gcs = [_lane_bcast(gcum, 16 + d * NH + h) for h in range(NH)]
            outs, ts = _dn1_heads(qs, ks, vs, betas, gcs, None, d == 1, kks, qks)
            for h, sl in enumerate(_HEAD_SLICES):
                u, w, qg, kd, qkd, gl = outs[h]
                u_ref[:, sl] = u
                w_ref[:, sl] = w.astype(BF)
                qg_ref[:, sl] = qg.astype(BF)
                kd_ref[:, sl] = kd.astype(BF)
                qkd_ref[:, sl] = qkd.astype(BF)
                gl_ref[h] = gl
                t_ref[:, sl] = ts[h]

    tb = pl.BlockSpec((CB, D), lambda i: (i, 0))
    one_dir_specs = [tb, tb, tb, tb, tb, pl.BlockSpec((NH, 1, 128), lambda i: (i, 0, 0)), tb]
    one_dir_shapes = ([jax.ShapeDtypeStruct((T, D), F32)] + [jax.ShapeDtypeStruct((T, D), BF)] * 4
                      + [jax.ShapeDtypeStruct((nb * NH, 1, 128), F32), jax.ShapeDtypeStruct((T, D), F32)])
    outs = pl.pallas_call(
        body, grid=(nb,), name="dn1_fwd",
        in_specs=[tb, tb, tb, pl.BlockSpec((CB, 128), lambda i: (i, 0))],
        out_specs=one_dir_specs * 2, out_shape=one_dir_shapes * 2, compiler_params=_cp(),
    )(q, k, v, gb)
    return [tuple(outs[:7]), tuple(outs[7:])]


def _dn1_bwd(q, k, v, gb, tinvs, cots):
    T = q.shape[0]
    nb = T // CB

    def body(q_ref, k_ref, v_ref, gb_ref, *refs):
        dir_refs, (dq_ref, dk_ref, dv_ref, dgb_ref) = refs[:14], refs[14:]
        gbv = gb_ref[...]
        qs = [q_ref[:, sl] for sl in _HEAD_SLICES]
        ks = [k_ref[:, sl] for sl in _HEAD_SLICES]
        vs = [v_ref[:, sl] for sl in _HEAD_SLICES]
        lane = lax.broadcasted_iota(jnp.int32, (CB, 128), 1)
        dgb = jnp.zeros((CB, 128), F32)
        for d in (0, 1):
            t_ref, du_ref, dw_ref, dqg_ref, dkd_ref, dqkd_ref, dgl_ref = dir_refs[7 * d:7 * d + 7]
            gcum = _dot_h3(_cum_matrix(d == 1), gbv)
            betas = [_lane_bcast(gbv, d * NH + h) for h in range(NH)]
            gcs = [_lane_bcast(gcum, 16 + d * NH + h) for h in range(NH)]
            ts = [t_ref[:, sl] for sl in _HEAD_SLICES]
            f = lambda qs, ks, vs, betas, gcs: _dn1_heads(qs, ks, vs, betas, gcs, ts, d == 1)[0]
            _, vjp = jax.vjp(f, qs, ks, vs, betas, gcs)
            cot = [(du_ref[:, sl], dw_ref[:, sl].astype(F32), dqg_ref[:, sl].astype(F32), dkd_ref[:, sl].astype(F32),
                    dqkd_ref[:, sl].astype(F32), dgl_ref[h])
                   for h, sl in enumerate(_HEAD_SLICES)]
            dqs, dks, dvs, dbetas, dgcs = vjp(cot)
            dgcum = jnp.zeros((CB, 128), F32)
            for h, sl in enumerate(_HEAD_SLICES):
                if d == 0:
                    dq_ref[:, sl] = dqs[h]
                    dk_ref[:, sl] = dks[h]
                    dv_ref[:, sl] = dvs[h]
                else:
                    dq_ref[:, sl] += dqs[h]
                    dk_ref[:, sl] += dks[h]
                    dv_ref[:, sl] += dvs[h]
                dgb = dgb + jnp.where(lane == d * NH + h, jnp.sum(dbetas[h], axis=1, keepdims=True), 0.0)
                dgcum = dgcum + jnp.where(lane == 16 + d * NH + h, jnp.sum(dgcs[h], axis=1, keepdims=True), 0.0)
            dgb = dgb + _dot_h3(_cum_matrix(d == 0), dgcum)
        dgb_ref[...] = dgb

    tb = pl.BlockSpec((CB, D), lambda i: (i, 0))
    gbs = pl.BlockSpec((CB, 128), lambda i: (i, 0))
    gls = pl.BlockSpec((NH, 1, 128), lambda i: (i, 0, 0))
    args = []
    for d in (0, 1):
        args += [tinvs[d], *cots[d]]
    return pl.pallas_call(
        body, grid=(nb,), name="dn1_bwd",
        in_specs=[tb, tb, tb, gbs] + [tb, tb, tb, tb, tb, tb, gls] * 2, out_specs=[tb, tb, tb, gbs],
        out_shape=[jax.ShapeDtypeStruct((T, D), F32)] * 3 + [jax.ShapeDtypeStruct((T, 128), F32)],
        compiler_params=_cp(),
    )(q, k, v, gb, *args)


def _dn2_steps(chains):
    ws = [_dot_bf(w, s) for _, w, _, _, _, _, s in chains]
    v_new = [c[0] - x for c, x in zip(chains, ws)]
    o_state = [_dot_bf(c[2], c[6]) for c in chains]
    o_local = [_dot_bf(c[4], vn) for c, vn in zip(chains, v_new)]
    grow = [_dot_tn_bf(c[3], vn) for c, vn in zip(chains, v_new)]
    return [a + b for a, b in zip(o_state, o_local)], [c[6] * c[5] + g for c, g in zip(chains, grow)]


def _dn2_steps_bwd(chains, cot_o, cot_s):
    bf = lambda a: a.astype(BF)
    nt = lambda a, b: lax.dot_general(bf(a), bf(b), (_DIMS["nt"], ((), ())), preferred_element_type=F32)
    v_new = [c[0] - _dot_bf(c[1], c[6]) for c in chains]
    dv = [_dot_bf(jnp.concatenate([c[4].T, c[3]], axis=1), jnp.concatenate([do, ds], axis=0))
          for c, do, ds in zip(chains, cot_o, cot_s)]
    both = [nt(jnp.concatenate([do, x], axis=0), c[6]) for c, do, x in zip(chains, cot_o, dv)]
    dqkd = [nt(do, vn) for do, vn in zip(cot_o, v_new)]
    dkd = [nt(vn, ds) for vn, ds in zip(v_new, cot_s)]
    dstate = [_dot_bf(jnp.concatenate([c[2].T, -c[1].T], axis=1), jnp.concatenate([do, x], axis=0))
              for c, do, x in zip(chains, cot_o, dv)]
    return [(x, -b[CB:], b[:CB], dk, dq, jnp.sum(ds * c[6], axis=0, keepdims=True), ds * c[5] + g)
            for c, x, b, dk, dq, ds, g in zip(chains, dv, both, dkd, dqkd, cot_s, dstate)]


def _scan_order(direction, nlat_b, nall_b):
    if direction == 0:
        return lambda i: (i + nlat_b) % nall_b
    return lambda i: nall_b - 1 - i


def _dn2_fwd(per_dir, nlat):
    T = per_dir[0][0].shape[0]
    nb = T // CB
    blks = [_scan_order(d, nlat // CB, nb) for d in (0, 1)]

    def body(*refs):
        ins, outs, s_scr = refs[:12], refs[12:16], refs[16]

        @pl.when(pl.program_id(0) == 0)
        def _():
            s_scr[...] = jnp.zeros_like(s_scr)
        for d in (0, 1):
            outs[2 * d + 1][0] = s_scr[d]
        where = [(d, h, sl) for h, sl in enumerate(_HEAD_SLICES) for d in (0, 1)]
        chains = []
        for d, h, sl in where:
            u_ref, w_ref, qg_ref, kd_ref, qkd_ref, gl_ref = ins[6 * d:6 * d + 6]
            chains.append((u_ref[:, sl], w_ref[:, sl], qg_ref[:, sl], kd_ref[:, sl], qkd_ref[:, sl], gl_ref[h], s_scr[d, h]))
        os, states = _dn2_steps(chains)
        for (d, h, sl), o, s_next in zip(where, os, states):
            outs[2 * d][:, sl] = o
            s_scr[d, h] = s_next

    in_specs, out_specs, args = [], [], []
    for d in (0, 1):
        blk = blks[d]
        tb = pl.BlockSpec((CB, D), lambda i, blk=blk: (blk(i), 0))
        in_specs += [tb] * 5 + [pl.BlockSpec((NH, 1, 128), lambda i, blk=blk: (blk(i), 0, 0))]
        out_specs += [tb, pl.BlockSpec((1, NH, HD, HD), lambda i, blk=blk: (blk(i), 0, 0, 0))]
        args += list(per_dir[d])
    outs = pl.pallas_call(
        body, grid=(nb,), name="dn2_fwd", in_specs=in_specs, out_specs=out_specs,
        out_shape=[jax.ShapeDtypeStruct((T, D), F32), jax.ShapeDtypeStruct((nb, NH, HD, HD), F32)] * 2,
        scratch_shapes=[pltpu.VMEM((2, NH, HD, HD), F32)], compiler_params=_cp(),
    )(*args)
    return [tuple(outs[:2]), tuple(outs[2:])]


def _dn2_bwd(per_dir, do, nlat):
    T = per_dir[0][0].shape[0]
    nb = T // CB
    nlat_b = nlat // CB
    fwd = [_scan_order(d, nlat_b, nb) for d in (0, 1)]
    blks = [lambda i, f=f: f(nb - 1 - i) for f in fwd]

    def body(*refs):
        ins, outs, ds_scr = refs[:16], refs[16:28], refs[28]
        i = pl.program_id(0)

        @pl.when(i == 0)
        def _():
            ds_scr[...] = jnp.zeros_like(ds_scr)
        where = [(d, h, sl) for h, sl in enumerate(_HEAD_SLICES) for d in (0, 1)]
        chains, cot_o, cot_s = [], [], []
        for d, h, sl in where:
            u_ref, w_ref, qg_ref, kd_ref, qkd_ref, gl_ref, sall_ref, do_ref = ins[8 * d:8 * d + 8]
            chains.append((u_ref[:, sl], w_ref[:, sl].astype(F32), qg_ref[:, sl].astype(F32), kd_ref[:, sl].astype(F32),
                           qkd_ref[:, sl].astype(F32), gl_ref[h], sall_ref[0, h]))
            cot_o.append(jnp.where(blks[d](i) < nlat_b, do_ref[:, sl], 0.0))
            cot_s.append(ds_scr[d, h])
        for (d, h, sl), (du, dw, dqg, dkd, dqkd, dgl, ds) in zip(where, _dn2_steps_bwd(chains, cot_o, cot_s)):
            du_ref, dw_ref, dqg_ref, dkd_ref, dqkd_ref, dgl_ref = outs[6 * d:6 * d + 6]
            du_ref[:, sl] = du
            dw_ref[:, sl] = dw.astype(BF)
            dqg_ref[:, sl] = dqg.astype(BF)
            dkd_ref[:, sl] = dkd.astype(BF)
            dqkd_ref[:, sl] = dqkd.astype(BF)
            dgl_ref[h] = dgl
            ds_scr[d, h] = ds

    in_specs, out_specs, args = [], [], []
    for d in (0, 1):
        blk = blks[d]
        tb = pl.BlockSpec((CB, D), lambda i, blk=blk: (blk(i), 0))
        gls = pl.BlockSpec((NH, 1, 128), lambda i, blk=blk: (blk(i), 0, 0))
        in_specs += [tb] * 5 + [gls, pl.BlockSpec((1, NH, HD, HD), lambda i, blk=blk: (blk(i), 0, 0, 0)),
                                pl.BlockSpec((CB, D), lambda i, blk=blk: (jnp.minimum(blk(i), nlat_b - 1), 0))]
        out_specs += [tb] * 5 + [gls]
        args += list(per_dir[d]) + [do]
    outs = pl.pallas_call(
        body, grid=(nb,), name="dn2_bwd", in_specs=in_specs, out_specs=out_specs,
        out_shape=([jax.ShapeDtypeStruct((T, D), F32)] + [jax.ShapeDtypeStruct((T, D), BF)] * 4
                   + [jax.ShapeDtypeStruct((nb * NH, 1, 128), F32)]) * 2,
        scratch_shapes=[pltpu.VMEM((2, NH, HD, HD), F32)], compiler_params=_cp(),
    )(*args)
    return [tuple(outs[:6]), tuple(outs[6:])]


def _ghn_fn(o, gt, w):
    y = o * lax.rsqrt(jnp.mean(o * o, axis=-1, keepdims=True) + EPS)
    return (y * w) * jax.nn.silu(gt)


def _ghn_fwd(o_f, o_b, p, w, w_branch, nlat):
    tb = _tile(nlat, (512, 256, 128))

    def body(of_ref, ob_ref, gt_ref, w_ref, wb_ref, y_ref, z_ref):
        for h in range(NH):
            sl = slice(h * HD, (h + 1) * HD)
            y_ref[:, sl] = _ghn_fn(of_ref[:, sl] + ob_ref[:, sl], gt_ref[:, sl], w_ref[...]).astype(BF)
        z_ref[...] = jnp.dot(y_ref[...], wb_ref[...], preferred_element_type=F32)

    row = pl.BlockSpec((tb, D), lambda i: (i, 0))
    return pl.pallas_call(
        body, grid=(nlat // tb,), name="ghn_fwd",
        in_specs=[row, row, pl.BlockSpec((tb, D), lambda i: (i, O_GT // D)), pl.BlockSpec((1, HD), lambda i: (0, 0)), _resident((D, D))],
        out_specs=[row, row], out_shape=[jax.ShapeDtypeStruct((nlat, D), BF), jax.ShapeDtypeStruct((nlat, D), F32)],
        compiler_params=_cp(),
    )(o_f, o_b, p, w, w_branch)


def _ghn_bwd(o_f, o_b, p, w, dy, nlat):
    T = p.shape[0]
    tb = _tile(nlat, (256, 128))
    nlb = nlat // tb

    def body(of_ref, ob_ref, gt_ref, w_ref, dy_ref, do_ref, dgt_ref, dw_ref):
        is_lat = pl.program_id(0) < nlb

        @pl.when(pl.program_id(0) == 0)
        def _():
            dw_ref[...] = jnp.zeros_like(dw_ref)
        for h in range(NH):
            sl = slice(h * HD, (h + 1) * HD)
            _, vjp = jax.vjp(_ghn_fn, of_ref[:, sl] + ob_ref[:, sl], gt_ref[:, sl], w_ref[...])
            do, dgt, dw = vjp(dy_ref[:, sl])
            do_ref[:, sl] = do
            dgt_ref[:, sl] = jnp.where(is_lat, dgt, 0.0).astype(BF)
            dw_ref[...] += jnp.where(is_lat, dw, 0.0)

    lat = lambda i: jnp.minimum(i, nlb - 1)
    row = pl.BlockSpec((tb, D), lambda i: (lat(i), 0))
    one = pl.BlockSpec((1, HD), lambda i: (0, 0))
    return pl.pallas_call(
        body, grid=(T // tb,), name="ghn_bwd",
        in_specs=[row, row, pl.BlockSpec((tb, D), lambda i: (lat(i), O_GT // D)), one, row],
        out_specs=[row, pl.BlockSpec((tb, D), lambda i: (i, 0)), one],
        out_shape=[jax.ShapeDtypeStruct((nlat, D), F32), jax.ShapeDtypeStruct((T, D), BF), jax.ShapeDtypeStruct((1, HD), F32)],
    )(o_f, o_b, p, w, dy)


@jax.custom_vjp
def _swap32(x):
    lane = lax.broadcasted_iota(jnp.int32, x.shape, 1)
    return jnp.where((lane & 32) == 0, pltpu.roll(x, 96, 1), pltpu.roll(x, 32, 1))


_swap32.defvjp(lambda x: (_swap32(x), None), lambda _, g: (_swap32(g),))


def _qk_post_fn(xs, w, cos, sin):
    inv = [lax.rsqrt(jnp.mean(x * x, axis=-1, keepdims=True) + EPS) for x in xs]
    ys = [(x * r) * w for x, r in zip(xs, inv)]
    return [y * cos + _swap32(y) * sin for y in ys]


def _attn_prep_fwd(p, qn, kn, cos, sin):
    T = p.shape[0]
    tb = _tile(T, (256, 128))

    def body(q_ref, k_ref, v_ref, qn_ref, kn_ref, cos_ref, sin_ref, qr_ref, kr_ref, vb_ref):
        cos_v, sin_v = cos_ref[...], sin_ref[...]
        for sl, y in zip(_HEAD_SLICES, _qk_post_fn([q_ref[:, sl] for sl in _HEAD_SLICES], qn_ref[...], cos_v, sin_v)):
            qr_ref[:, sl] = y.astype(BF)
        for sl, y in zip(_HEAD_SLICES, _qk_post_fn([k_ref[:, sl] for sl in _HEAD_SLICES[:KVH]], kn_ref[...], cos_v, sin_v)):
            kr_ref[:, sl] = y.astype(BF)
        vb_ref[...] = v_ref[...].astype(BF)

    one = pl.BlockSpec((1, HD), lambda i: (0, 0))
    tab = pl.BlockSpec((tb, HD), lambda i: (i, 0))
    return pl.pallas_call(
        body, grid=(T // tb,), name="attn_prep_fwd",
        in_specs=[pl.BlockSpec((tb, D), lambda i: (i, O_Q // D)), pl.BlockSpec((tb, KV), lambda i: (i, O_K // KV)),
                  pl.BlockSpec((tb, KV), lambda i: (i, O_V // KV)), one, one, tab, tab],
        out_specs=[pl.BlockSpec((tb, D), lambda i: (i, 0)), pl.BlockSpec((tb, KV), lambda i: (i, 0)),
                   pl.BlockSpec((tb, KV), lambda i: (i, 0))],
        out_shape=[jax.ShapeDtypeStruct((T, D), BF), jax.ShapeDtypeStruct((T, KV), BF), jax.ShapeDtypeStruct((T, KV), BF)],
    )(p, p, p, qn, kn, cos, sin)


def _attn_prep_bwd(p, qn, kn, cos, sin, dqr, dkp, dvp, dkc, dvc, nlat):
    T = p.shape[0]
    nqb = nlat // CB
    ncb = (T - nlat) // CB

    def body(q_ref, k_ref, v_ref, qn_ref, kn_ref, cos_ref, sin_ref, dqr_ref, dka_ref, dkb_ref, dkc3_ref, dva_ref, dvb_ref, dvc3_ref,
             dkctx_ref, dvctx_ref, dq_ref, dk_ref, dv_ref, dqn_ref, dkn_ref):
        i = pl.program_id(0)
        is_lat = i < nqb
        cos_v, sin_v = cos_ref[...], sin_ref[...]

        @pl.when(i == 0)
        def _():
            dqn_ref[...] = jnp.zeros_like(dqn_ref)
            dkn_ref[...] = jnp.zeros_like(dkn_ref)

        def band_sum(a_ref, b_ref, c_ref, ctx_ref):
            s = b_ref[0] + jnp.where(i > 0, a_ref[0], 0.0) + jnp.where(i < nqb - 1, c_ref[0], 0.0)
            return jnp.where(is_lat, s, ctx_ref[...])

        dkr = band_sum(dka_ref, dkb_ref, dkc3_ref, dkctx_ref)
        dv_ref[...] = band_sum(dva_ref, dvb_ref, dvc3_ref, dvctx_ref).astype(BF)
        post = lambda xs, w: _qk_post_fn(xs, w, cos_v, sin_v)
        _, vjp = jax.vjp(post, [q_ref[:, sl] for sl in _HEAD_SLICES], qn_ref[...])
        dqs, dqn = vjp([jnp.where(is_lat, dqr_ref[:, sl], 0.0) for sl in _HEAD_SLICES])
        for sl, dq in zip(_HEAD_SLICES, dqs):
            dq_ref[:, sl] = dq.astype(BF)
        dqn_ref[...] += dqn
        _, vjp = jax.vjp(post, [k_ref[:, sl] for sl in _HEAD_SLICES[:KVH]], kn_ref[...])
        dks, dkn = vjp([dkr[:, sl] for sl in _HEAD_SLICES[:KVH]])
        for sl, dk in zip(_HEAD_SLICES, dks):
            dk_ref[:, sl] = dk.astype(BF)
        dkn_ref[...] += dkn

    one = pl.BlockSpec((1, HD), lambda i: (0, 0))
    tab = pl.BlockSpec((CB, HD), lambda i: (i, 0))
    lat = lambda i: jnp.minimum(i, nqb - 1)

    def part(off, slot):
        return pl.BlockSpec((1, CB, KV), lambda i: (jnp.clip(lat(i) + off, 0, nqb - 1) * 3 + slot, 0, 0))

    ctxs = pl.BlockSpec((CB, KV), lambda i: (jnp.clip(i - nqb, 0, ncb - 1), 0))
    kvs = pl.BlockSpec((CB, KV), lambda i: (i, 0))
    return pl.pallas_call(
        body, grid=(T // CB,), name="attn_prep_bwd",
        in_specs=[pl.BlockSpec((CB, D), lambda i: (i, O_Q // D)), pl.BlockSpec((CB, KV), lambda i: (i, O_K // KV)),
                  pl.BlockSpec((CB, KV), lambda i: (i, O_V // KV)), one, one, tab, tab,
                  pl.BlockSpec((CB, D), lambda i: (lat(i), 0)),
                  part(-1, 2), part(0, 1), part(1, 0), part(-1, 2), part(0, 1), part(1, 0), ctxs, ctxs],
        out_specs=[pl.BlockSpec((CB, D), lambda i: (i, 0)), kvs, kvs, one, one],
        out_shape=[jax.ShapeDtypeStruct((T, D), BF), jax.ShapeDtypeStruct((T, KV), BF), jax.ShapeDtypeStruct((T, KV), BF),
                   jax.ShapeDtypeStruct((1, HD), F32), jax.ShapeDtypeStruct((1, HD), F32)],
    )(p, p, p, qn, kn, cos, sin, dqr, dkp, dkp, dkp, dvp, dvp, dvp, dkc, dvc)


def _attn_groups_fn(qs, kalls, valls, sinks, bias):
    groups = range(KVH)
    q = [jnp.concatenate(qs[GRP * g:GRP * (g + 1)], axis=0) for g in groups]
    s = [_bf_product(q[g], kalls[g], "nt") * (HD ** -0.5) + bias for g in groups]
    sk = [jnp.concatenate([jnp.broadcast_to(jnp.mean(t, axis=1, keepdims=True), (CB, 1)) for t in sinks[GRP * g:GRP * (g + 1)]],
                          axis=0) for g in groups]
    m = [lax.stop_gradient(jnp.maximum(jnp.max(s[g], axis=1, keepdims=True), sk[g])) for g in groups]
    e = [jnp.exp(s[g] - m[g]) for g in groups]
    den = [jnp.sum(e[g], axis=1, keepdims=True) + jnp.exp(sk[g] - m[g]) for g in groups]
    return [_bf_product(e[g] / den[g], valls[g], "nn") for g in groups]


def _attn_bias(lc):
    r, c = _iota2((GRP * CB, 3 * CB + lc))
    rel = c - (r & (CB - 1))
    win = (rel >= 0) & (rel <= 2 * CB)
    ctx = c >= 3 * CB
    seen = [(win & (c >= CB)) | ctx, win | ctx, (win & (c < 2 * CB)) | ctx]
    return jnp.stack([jnp.where(s, 0.0, -1e30) for s in seen]).astype(F32)


def _attn_specs(nqb, lc, nlat):
    assert nqb >= 2
    qs = pl.BlockSpec((CB, D), lambda i: (i, 0))
    ka = pl.BlockSpec((CB, KV), lambda i: (jnp.maximum(i - 1, 0), 0))
    kb = pl.BlockSpec((CB, KV), lambda i: (i, 0))
    kc = pl.BlockSpec((CB, KV), lambda i: (jnp.minimum(i + 1, nqb - 1), 0))
    kx = pl.BlockSpec((lc, KV), lambda i: (nlat // lc, 0))
    sk = pl.BlockSpec((KVH, 8, 128), lambda i: (0, 0, 0))
    bs = pl.BlockSpec((1, GRP * CB, 3 * CB + lc), lambda i: (jnp.where(i == 0, 0, jnp.where(i == nqb - 1, 2, 1)), 0, 0))
    return qs, ka, kb, kc, kx, sk, bs


def _attn_operands(q_ref, k_refs, v_refs, sk_ref, dtype):
    sls = [slice(g * HD, (g + 1) * HD) for g in range(KVH)]
    kalls = [jnp.concatenate([r[:, sl] for r in k_refs], axis=0).astype(dtype) for sl in sls]
    valls = [jnp.concatenate([r[:, sl] for r in v_refs], axis=0).astype(dtype) for sl in sls]
    qs = [q_ref[:, sl].astype(dtype) for sl in _HEAD_SLICES]
    sinks = [sk_ref[h // GRP, (h % GRP):(h % GRP) + 1, :] for h in range(NH)]
    return qs, kalls, valls, sinks


def _attn_fwd(qr, kr, vb, sink, w_branch, nlat):
    lc = kr.shape[0] - nlat
    nqb = nlat // CB
    qs, ka, kb, kc, kx, sk, bs = _attn_specs(nqb, lc, nlat)

    def body(q_ref, ka_ref, kb_ref, kc_ref, kx_ref, va_ref, vb_ref, vc_ref, vx_ref, sk_ref, bias_ref, wb_ref, o_ref, z_ref):
        operands = _attn_operands(q_ref, (ka_ref, kb_ref, kc_ref, kx_ref), (va_ref, vb_ref, vc_ref, vx_ref), sk_ref, BF)
        outs = _attn_groups_fn(*operands, bias_ref[0])
        for h, sl in enumerate(_HEAD_SLICES):
            o_ref[:, sl] = outs[h // GRP][(h % GRP) * CB:(h % GRP + 1) * CB].astype(BF)
        z_ref[...] = jnp.dot(o_ref[...], wb_ref[...], preferred_element_type=F32)

    return pl.pallas_call(
        body, grid=(nqb,), name="attn_fwd",
        in_specs=[qs, ka, kb, kc, kx, ka, kb, kc, kx, sk, bs, _resident((D, D))], out_specs=[qs, qs],
        out_shape=[jax.ShapeDtypeStruct((nlat, D), BF), jax.ShapeDtypeStruct((nlat, D), F32)], compiler_params=_cp(),
    )(qr, kr, kr, kr, kr, vb, vb, vb, vb, sink, _attn_bias(lc), w_branch)


def _attn_bwd(qr, kr, vb, sink, dy, nlat):
    lc = kr.shape[0] - nlat
    nqb = nlat // CB
    qs, ka, kb, kc, kx, sk, bs = _attn_specs(nqb, lc, nlat)

    def body(q_ref, ka_ref, kb_ref, kc_ref, kx_ref, va_ref, vb_ref, vc_ref, vx_ref, sk_ref, dy_ref, bias_ref,
             dq_ref, dkp_ref, dvp_ref, dkx_ref, dvx_ref, dsk_ref):
        operands = _attn_operands(q_ref, (ka_ref, kb_ref, kc_ref, kx_ref), (va_ref, vb_ref, vc_ref, vx_ref), sk_ref, F32)
        _, vjp = jax.vjp(functools.partial(_attn_groups_fn, bias=bias_ref[0]), *operands)
        dys_g = [jnp.concatenate([dy_ref[:, sl] for sl in _HEAD_SLICES[GRP * g:GRP * (g + 1)]], axis=0) for g in range(KVH)]
        dqs, dks, dvs, dsinks = vjp(dys_g)

        @pl.when(pl.program_id(0) == 0)
        def _():
            dkx_ref[...] = jnp.zeros_like(dkx_ref)
            dvx_ref[...] = jnp.zeros_like(dvx_ref)
            dsk_ref[...] = jnp.zeros_like(dsk_ref)

        for h, sl in enumerate(_HEAD_SLICES):
            dq_ref[:, sl] = dqs[h]
            dsk_ref[h // GRP, (h % GRP):(h % GRP) + 1, :] += dsinks[h]
        for g in range(KVH):
            sl = slice(g * HD, (g + 1) * HD)
            for t in range(3):
                dkp_ref[t, :, sl] = dks[g][t * CB:(t + 1) * CB]
                dvp_ref[t, :, sl] = dvs[g][t * CB:(t + 1) * CB]
            dkx_ref[:, sl] += dks[g][3 * CB:]
            dvx_ref[:, sl] += dvs[g][3 * CB:]

    dys = qs
    parts = pl.BlockSpec((3, CB, KV), lambda i: (i, 0, 0))
    ctxo = pl.BlockSpec((lc, KV), lambda i: (0, 0))
    return pl.pallas_call(
        body, grid=(nqb,), name="attn_bwd",
        in_specs=[qs, ka, kb, kc, kx, ka, kb, kc, kx, sk, dys, bs],
        out_specs=[dys, parts, parts, ctxo, ctxo, sk],
        out_shape=[jax.ShapeDtypeStruct((nlat, D), F32), jax.ShapeDtypeStruct((3 * nqb, CB, KV), F32),
                   jax.ShapeDtypeStruct((3 * nqb, CB, KV), F32), jax.ShapeDtypeStruct((lc, KV), F32),
                   jax.ShapeDtypeStruct((lc, KV), F32), jax.ShapeDtypeStruct((KVH, 8, 128), F32)],
        compiler_params=_cp(),
    )(qr, kr, kr, kr, kr, vb, vb, vb, vb, sink, dy, _attn_bias(lc))


def _merge_fn(z_dn, z_at, g_dn, g_at):
    return jax.nn.sigmoid(g_dn) * z_dn + jax.nn.sigmoid(g_at) * z_at


def _merge_fwd(z_dn, z_at, p, w_out, nlat):
    tb = _tile(nlat, (512, 256, 128))

    def body(zd_ref, za_ref, gd_ref, ga_ref, wo_ref, o_ref, mix_ref):
        o_ref[...] = _merge_fn(zd_ref[...], za_ref[...], gd_ref[...], ga_ref[...]).astype(BF)
        mix_ref[...] = jnp.dot(o_ref[...], wo_ref[...], preferred_element_type=F32)

    row = pl.BlockSpec((tb, D), lambda i: (i, 0))
    return pl.pallas_call(
        body, grid=(nlat // tb,), name="merge_fwd",
        in_specs=[row, row, pl.BlockSpec((tb, D), lambda i: (i, O_MG // D)), pl.BlockSpec((tb, D), lambda i: (i, O_MG // D + 1)),
                  _resident((D, D))],
        out_specs=[row, row], out_shape=[jax.ShapeDtypeStruct((nlat, D), BF), jax.ShapeDtypeStruct((nlat, D), F32)],
        compiler_params=_cp(),
    )(z_dn, z_at, p, p, w_out)


def _merge_bwd(z_dn, z_at, p, dm, w_bdn, w_bat, nlat):
    T = p.shape[0]
    tb = _tile(nlat, (256, 128))
    nlb = nlat // tb

    def body(zd_ref, za_ref, gd_ref, ga_ref, dm_ref, wd_ref, wa_ref, dzd_ref, dza_ref, dg_ref, dyd_ref, dya_ref):
        is_lat = pl.program_id(0) < nlb
        _, vjp = jax.vjp(_merge_fn, zd_ref[...], za_ref[...], gd_ref[...], ga_ref[...])
        dzd, dza, dgd, dga = vjp(dm_ref[...])
        dzd_ref[...] = dzd.astype(BF)
        dza_ref[...] = dza.astype(BF)
        dg_ref[:, :D] = jnp.where(is_lat, dgd, 0.0).astype(BF)
        dg_ref[:, D:] = jnp.where(is_lat, dga, 0.0).astype(BF)
        dyd_ref[...] = lax.dot_general(dzd_ref[...], wd_ref[...], (_DIMS["nt"], ((), ())), preferred_element_type=F32)
        dya_ref[...] = lax.dot_general(dza_ref[...], wa_ref[...], (_DIMS["nt"], ((), ())), preferred_element_type=F32)

    lat = lambda i: jnp.minimum(i, nlb - 1)
    row = pl.BlockSpec((tb, D), lambda i: (lat(i), 0))
    return pl.pallas_call(
        body, grid=(T // tb,), name="merge_bwd",
        in_specs=[row, row, pl.BlockSpec((tb, D), lambda i: (lat(i), O_MG // D)),
                  pl.BlockSpec((tb, D), lambda i: (lat(i), O_MG // D + 1)), row, _resident((D, D)), _resident((D, D))],
        out_specs=[row, row, pl.BlockSpec((tb, 2 * D), lambda i: (i, 0)), row, row],
        out_shape=[jax.ShapeDtypeStruct((nlat, D), BF), jax.ShapeDtypeStruct((nlat, D), BF), jax.ShapeDtypeStruct((T, 2 * D), BF),
                   jax.ShapeDtypeStruct((nlat, D), F32), jax.ShapeDtypeStruct((nlat, D), F32)],
    )(z_dn, z_at, p, p, dm, w_bdn, w_bat)


def _swiglu_fn(ug, uv):
    return jax.nn.silu(ug) * uv


FFN_GROUP = 256


def _resident(shape):
    return pl.BlockSpec(shape, lambda i: (0,) * len(shape), pipeline_mode=pl.Buffered(1))


H_HALO = 16


def _up_project(h_refs, wu_ref, u_scr):
    cur_ref, prev_ref, next_ref = h_refs
    rows = jnp.concatenate([prev_ref[...], cur_ref[...], next_ref[...]], axis=0)
    u_scr[...] = jnp.dot(rows, wu_ref[...], preferred_element_type=F32)


def _up_ext_rows(u_scr, cols, keep, tb):
    xe = u_scr[H_HALO - HALO:H_HALO + tb + HALO, cols]
    r = lax.broadcasted_iota(jnp.int32, (tb + 2 * HALO, 1), 0)
    inside = ((r >= HALO) | keep[0]) & ((r < HALO + tb) | keep[1])
    return jnp.where(inside, xe, 0.0)


def _ffn_fwd(h, w_up, w8, bias, w_down):
    n = h.shape[0]
    tb = _tile(n, (256, 128))
    starts, ends = _segment_edges((n,), tb)

    def body(cur_ref, prev_ref, next_ref, wu_ref, w_ref, b_ref, wd_ref, u_ref, o_ref, ff_ref, u_scr):
        keep = _keep_halos(pl.program_id(0), starts, ends)
        _up_project((cur_ref, prev_ref, next_ref), wu_ref, u_scr)
        u_ref[...] = u_scr[H_HALO:H_HALO + tb, :]

        for c0 in range(0, DFF, FFN_GROUP):
            halves = []
            for cols in (slice(c0, c0 + FFN_GROUP), slice(DFF + c0, DFF + c0 + FFN_GROUP)):
                xe = _up_ext_rows(u_scr, cols, keep, tb)
                halves.append(_conv_rows(_shifted_rows(xe, FFN_TAPS), w_ref, cols)[HALO:HALO + tb] + b_ref[:, cols])
            o_ref[:, c0:c0 + FFN_GROUP] = _swiglu_fn(*halves).astype(BF)
        ff_ref[...] = jnp.dot(o_ref[...], wd_ref[...], preferred_element_type=F32)

    return pl.pallas_call(
        body, grid=(n // tb,), name="ffn_fwd",
        in_specs=_halo_specs(tb, D, n, halo=H_HALO) + [_resident((D, 2 * DFF)), pl.BlockSpec((8, 2 * DFF), lambda i: (0, 0)),
                                                        pl.BlockSpec((1, 2 * DFF), lambda i: (0, 0)), _resident((DFF, D))],
        out_specs=[pl.BlockSpec((tb, 2 * DFF), lambda i: (i, 0)), pl.BlockSpec((tb, DFF), lambda i: (i, 0)),
                   pl.BlockSpec((tb, D), lambda i: (i, 0))],
        out_shape=[jax.ShapeDtypeStruct((n, 2 * DFF), F32), jax.ShapeDtypeStruct((n, DFF), BF), jax.ShapeDtypeStruct((n, D), F32)],
        scratch_shapes=[pltpu.VMEM((tb + 2 * H_HALO, 2 * DFF), F32)],
        compiler_params=_cp(),
    )(h, h, h, w_up, w8, bias, w_down)


def _ffn_bwd(u, w_up, w8, bias, da):
    n = u.shape[0]
    tb = _tile(n, (256, 128))
    starts, ends = _segment_edges((n,), tb)

    def body(cur_ref, prev_ref, next_ref, wu_ref, w_ref, b_ref, da_c, da_p, da_n, du_ref, dw_ref, db_ref, dh_ref):
        i = pl.program_id(0)
        keep = _keep_halos(i, starts, ends)

        @pl.when(i == 0)
        def _():
            dw_ref[...] = jnp.zeros_like(dw_ref)
            db_ref[...] = jnp.zeros_like(db_ref)

        for c0 in range(0, DFF, FFN_GROUP):
            col_pair = (slice(c0, c0 + FFN_GROUP), slice(DFF + c0, DFF + c0 + FFN_GROUP))
            shifts = [_shifted_rows(_ext_rows((cur_ref, prev_ref, next_ref), cols, keep), FFN_TAPS) for cols in col_pair]
            convs = [_conv_rows(shifted, w_ref, cols) + b_ref[:, cols] for shifted, cols in zip(shifts, col_pair)]
            dae = _ext_rows((da_c, da_p, da_n), col_pair[0], keep)
            _, vjp = jax.vjp(_swiglu_fn, *convs)
            for shifted, cols, dce in zip(shifts, col_pair, vjp(dae)):
                du_ref[:, cols] = _conv_rows(_shifted_rows(dce, FFN_TAPS, transpose=True), w_ref, cols)[HALO:HALO + tb].astype(BF)
                dcur = dce[HALO:HALO + tb]
                for j, g in enumerate(_tap_grads(dcur, shifted, tb)):
                    dw_ref[j:j + 1, cols] += g
                db_ref[:, cols] += jnp.sum(dcur, axis=0, keepdims=True)
        dh_ref[...] = lax.dot_general(du_ref[...], wu_ref[...], (_DIMS["nt"], ((), ())), preferred_element_type=F32)

    wspec = pl.BlockSpec((8, 2 * DFF), lambda i: (0, 0))
    bspec = pl.BlockSpec((1, 2 * DFF), lambda i: (0, 0))
    return pl.pallas_call(
        body, grid=(n // tb,), name="ffn_bwd",
        in_specs=_halo_specs(tb, 2 * DFF, n) + [_resident((D, 2 * DFF)), wspec, bspec] + _halo_specs(tb, DFF, n),
        out_specs=[pl.BlockSpec((tb, 2 * DFF), lambda i: (i, 0)), wspec, bspec, pl.BlockSpec((tb, D), lambda i: (i, 0))],
        out_shape=[jax.ShapeDtypeStruct((n, 2 * DFF), BF), jax.ShapeDtypeStruct((8, 2 * DFF), F32), jax.ShapeDtypeStruct((1, 2 * DFF), F32),
                   jax.ShapeDtypeStruct((n, D), F32)],
        compiler_params=_cp(),
    )(u, u, u, w_up, w8, bias, da, da, da)


def _loss_kernel(x1, gate, ff, target, w_down):
    n = x1.shape[0]
    tb = _tile(n, (512, 256, 128))

    def body(x_ref, g_ref, f_ref, t_ref, wd_ref, loss_ref, dy_ref, dff_ref, dg_ref, da_ref):
        err = x_ref[...] + g_ref[...] * f_ref[...] - t_ref[...]
        dy = err * (1.0 / D)
        dy_ref[...] = dy
        dff_ref[...] = (g_ref[...] * dy).astype(BF)
        da_ref[...] = lax.dot_general(dff_ref[...], wd_ref[...], (_DIMS["nt"], ((), ())), preferred_element_type=F32)

        @pl.when(pl.program_id(0) == 0)
        def _():
            loss_ref[...] = jnp.zeros_like(loss_ref)
            dg_ref[...] = jnp.zeros_like(dg_ref)
        part = 0.5 * jnp.sum(jnp.sum(err * err, axis=1, keepdims=True) * (1.0 / D), axis=0, keepdims=True)
        loss_ref[...] += jnp.broadcast_to(part, (1, 128))
        dg_ref[...] += jnp.sum(dy * f_ref[...], axis=0, keepdims=True)

    row = pl.BlockSpec((tb, D), lambda i: (i, 0))
    one = pl.BlockSpec((1, D), lambda i: (0, 0))
    return pl.pallas_call(
        body, grid=(n // tb,), name="loss",
        in_specs=[row, one, row, row, _resident((DFF, D))],
        out_specs=[pl.BlockSpec((1, 128), lambda i: (0, 0)), row, row, one, pl.BlockSpec((tb, DFF), lambda i: (i, 0))],
        out_shape=[jax.ShapeDtypeStruct((1, 128), F32), jax.ShapeDtypeStruct((n, D), F32),
                   jax.ShapeDtypeStruct((n, D), BF), jax.ShapeDtypeStruct((1, D), F32), jax.ShapeDtypeStruct((n, DFF), F32)],
        compiler_params=_cp(),
    )(x1, gate, ff, target, w_down)


def _rope_tables(nlat, lc):
    inv_freq = (np.float32(ROPE_BASE) ** (-np.arange(32, dtype=np.float32) / np.float32(32))).astype(np.float32)
    ar = np.arange(nlat // GRID_W, dtype=np.float32)[:, None] * inv_freq
    ac = np.arange(GRID_W, dtype=np.float32)[:, None] * inv_freq
    by_row = lambda a: jnp.repeat(jnp.asarray(a, F32), GRID_W, axis=0)
    by_col = lambda a: jnp.tile(jnp.asarray(a, F32), (nlat // GRID_W, 1))
    cos = jnp.concatenate([by_row(np.cos(ar)), by_row(np.cos(ar)), by_col(np.cos(ac)), by_col(np.cos(ac))], axis=1)
    sin = jnp.concatenate([by_row(-np.sin(ar)), by_row(np.sin(ar)), by_col(-np.sin(ac)), by_col(np.sin(ac))], axis=1)
    cos = jnp.concatenate([cos, jnp.ones((lc, HD), F32)], axis=0)
    sin = jnp.concatenate([sin, jnp.zeros((lc, HD), F32)], axis=0)
    return cos, sin


def _pad_rows8(w):
    return jnp.concatenate([w, jnp.zeros((8 - w.shape[0], w.shape[1]), w.dtype)], axis=0)


def _pack_w_in(w):
    cuts = [sum(IN_SIZES[:i]) for i in range(len(IN_SIZES) + 1)]
    qkv, gt, b, a, q, k, v, mg = [w[:, cuts[i]:cuts[i + 1]] for i in range(len(IN_SIZES))]
    return jnp.concatenate([qkv, gt, q, mg, k, v, b, a, jnp.zeros((w.shape[0], PW - O_BA - 32), w.dtype)], axis=1)


def _unpack_w_in(g):
    return jnp.concatenate([g[:, O_QKV:O_GT], g[:, O_GT:O_Q], g[:, O_BA:O_BA + 32], g[:, O_Q:O_MG], g[:, O_K:O_V],
                            g[:, O_V:O_BA], g[:, O_MG:O_K]], axis=1)


def _local_step(x, ctx, mod_x, mod_c, target, project_in, project_back,
                norm_mix, norm_ffn, dn_conv, a_log, dt_bias, dn_norm, q_norm, k_norm, sink, ffn_conv, ffn_conv_b):
    L, LC = x.shape[0], ctx.shape[0]
    T = L + LC
    seg = lambda r: jnp.stack([mod_x[r], mod_c[r]])[:, None, :]
    sh_a, sc_a = seg(0), seg(1)
    g_a, g_f = mod_x[2][None], mod_x[5][None]
    sh_f, sc_f = mod_x[3][None], mod_x[4][None]
    cos, sin = _rope_tables(L, LC)
    dnc8 = _pad_rows8(dn_conv)
    ffc8 = _pad_rows8(ffn_conv)
    gate_row = lambda a: jnp.concatenate([jnp.zeros((1, 16), F32), a.reshape(1, 16), jnp.zeros((1, 96), F32)], axis=1)
    alog_row, dt_row = gate_row(a_log), gate_row(dt_bias)
    sinkb = jnp.concatenate([jnp.broadcast_to(sink.reshape(KVH, GRP, 1), (KVH, GRP, 128)), jnp.zeros((KVH, 8 - GRP, 128), F32)], axis=1)

    h1 = _norm_mod_fwd(x, ctx, norm_mix, sh_a, sc_a, "norm_mix_fwd")
    p, (w_in_p, w_bdn, w_bat, w_out, w_up, w_down) = project_in(h1)
    q, k, v, gb = _dn_pre_fwd(p, dnc8, alog_row, dt_row, (L, LC))
    wy = _dn1_fwd(q, k, v, gb)
    scans = _dn2_fwd([t[:6] for t in wy], L)
    o_dir = [s[0] for s in scans]
    y_dn, z_dn = _ghn_fwd(o_dir[0], o_dir[1], p, dn_norm, w_bdn, L)
    qr, kr, vb = _attn_prep_fwd(p, q_norm, k_norm, cos, sin)
    y_at, z_at = _attn_fwd(qr, kr, vb, sinkb, w_bat, L)
    merged, mix = _merge_fwd(z_dn, z_at, p, w_out, L)
    x1, h2 = _resid_norm_fwd(x, g_a, mix, norm_ffn, sh_f, sc_f)
    u_raw, act, ff = _ffn_fwd(h2, w_up, ffc8, ffn_conv_b, w_down)
    loss_row, dy, dff, dg_f, dact = _loss_kernel(x1, g_f, ff, target, w_down)

    g_down = _mm(act, dff, form="tn", out_dtype=BF, name="g_ffn_down")
    du_raw, g_ffc8, g_ffb, dh2 = _ffn_bwd(u_raw, w_up, ffc8, ffn_conv_b, dact)
    g_up = _mm(h2, du_raw, form="tn", out_dtype=BF, name="g_ffn_up")
    dx1, dmix, dg_a, g_nffn, dsh_f, dsc_f, dmerged = _resid_norm_bwd(x1, g_a, mix, norm_ffn, sh_f, sc_f, dh2, dy, w_out)

    g_out = _mm(merged, dmix, form="tn", out_dtype=BF, name="g_w_out")
    dz_dn, dz_at, dmg, dy_dn, dy_at = _merge_bwd(z_dn, z_at, p, dmerged, w_bdn, w_bat, L)
    g_bdn = _mm(y_dn, dz_dn, form="tn", out_dtype=BF, name="g_branch_dn")
    g_bat = _mm(y_at, dz_at, form="tn", out_dtype=BF, name="g_branch_at")
    dqr, dkp, dvp, dkx, dvx, dsink = _attn_bwd(qr, kr, vb, sinkb, dy_at, L)
    dq_raw, dk_raw, dv_raw, g_qn, g_kn = _attn_prep_bwd(p, q_norm, k_norm, cos, sin, dqr, dkp, dvp, dkx, dvx, L)
    do, dgt, g_dnn = _ghn_bwd(o_dir[0], o_dir[1], p, dn_norm, dy_dn, L)
    cots = _dn2_bwd([wy[d][:6] + (scans[d][1],) for d in (0, 1)], do, L)
    dq, dk, dv, dgb = _dn1_bwd(q, k, v, gb, [t[6] for t in wy], cots)
    dp, g_dnc8, g_alog, g_dt = _dn_pre_bwd(p, dnc8, alog_row, dt_row, dq, dk, dv, dgb, (dgt, dq_raw, dmg, dk_raw, dv_raw), (L, LC))
    big, dh1 = project_back(h1, dp, w_in_p, (g_bdn, g_bat, g_out, g_up, g_down))
    grad_x, g_nmix_x, dsh_a, dsc_a = _norm_mod_bwd(x, norm_mix, mod_x[0][None], mod_x[1][None], dh1, row0=0,
                                                   name="norm_mix_bwd", residual=dx1)
    g_nmix_c, dsh_c, dsc_c = _norm_mod_bwd(ctx, norm_mix, mod_c[0][None], mod_c[1][None], dh1, row0=L, name="norm_mix_bwd_ctx")
    g_nmix = g_nmix_x + g_nmix_c

    zero = jnp.zeros((D,), F32)
    dmod_x = jnp.stack([dsh_a[0], dsc_a[0], dg_a[0], dsh_f[0], dsc_f[0], dg_f[0]])
    dmod_c = jnp.stack([dsh_c[0], dsc_c[0], zero, zero, zero, zero])
    small = dict(
        dmod_x=dmod_x, dmod_c=dmod_c, norm_mix=g_nmix, norm_ffn=g_nffn, dn_conv=g_dnc8[:5], dn_a_log=g_alog[0, 16:32].reshape(2, 8),
        dn_dt_bias=g_dt[0, 16:32].reshape(2, 8), dn_norm=g_dnn, q_norm=g_qn, k_norm=g_kn,
        attn_sink=jnp.sum(dsink[:, :GRP, :], axis=2).reshape(1, NH), ffn_conv=g_ffc8[:3], ffn_conv_b=g_ffb)
    return loss_row[0, 0], grad_x, big, small


def _exchange(arrays, scatter, name):
    n = len(arrays)

    def body(*refs):
        args = (refs[:n], refs[n:2 * n], *refs[2 * n:], scatter)
        _exchange_start(*args)
        _exchange_wait(*args)

    hbm = pl.BlockSpec(memory_space=pl.ANY)
    out_shape, sems = _exchange_shapes(arrays, scatter)
    return pl.pallas_call(body, name=name, in_specs=[hbm] * n, out_specs=[hbm] * n, out_shape=out_shape,
                          scratch_shapes=sems)(*arrays)


def _gather_two_level(arrays, name):
    n = len(arrays)

    def body(*refs):
        ins, outs = refs[:n], refs[n:2 * n]
        send_sems, recv_sems, local_sems = refs[2 * n:]
        x, y, c = lax.axis_index("x"), lax.axis_index("y"), lax.axis_index("c")
        sibling = (x, y, 1 - c)
        chips = [(1 - x, y), (x, 1 - y), (1 - x, 1 - y)]

        def copy(k, j, block, to, src=None):
            slot = outs[k].at[4 * block[0] + 2 * block[1] + block[2]]
            return pltpu.make_async_remote_copy(src_ref=slot if src is None else src, dst_ref=slot,
                                                send_sem=send_sems.at[7 * k + j], recv_sem=recv_sems.at[7 * k + j],
                                                device_id=to, device_id_type=MESH)

        mine = [pltpu.make_async_copy(ins[k], outs[k].at[4 * x + 2 * y + c], local_sems.at[k]) for k in range(n)]
        for cp in mine:
            cp.start()
        first = []
        for k in range(n):
            first.append(copy(k, 0, (x, y, c), sibling, src=ins[k]))
            first += [copy(k, 1 + j, (x, y, c), (*chip, c), src=ins[k]) for j, chip in enumerate(chips)]
        for cp in first:
            cp.start()
        passed = []
        for k in range(n):
            for j, chip in enumerate(chips):
                copy(k, 1 + j, (*chip, c), (x, y, c)).wait_recv()
                forward = copy(k, 4 + j, (*chip, c), sibling)
                forward.start()
                passed.append(forward)
        for k in range(n):
            copy(k, 0, sibling, (x, y, c)).wait_recv()
            for j, chip in enumerate(chips):
                copy(k, 4 + j, (*chip, 1 - c), (x, y, c)).wait_recv()
        for cp in first + passed:
            cp.wait_send()
        for cp in mine:
            cp.wait()

    hbm = pl.BlockSpec(memory_space=pl.ANY)
    out_shape, sems = _exchange_shapes(arrays, False)
    return pl.pallas_call(body, name=name, in_specs=[hbm] * n, out_specs=[hbm] * n, out_shape=out_shape,
                          scratch_shapes=sems)(*arrays)


def _ada_fwd(c16, w_ada, b_ada):
    def body(c_ref, w_ref, b_ref, o_ref):
        o_ref[...] = _dot_hi(jax.nn.silu(c_ref[...]), w_ref[...]) + b_ref[...]

    return pl.pallas_call(body, name="ada_fwd", out_shape=jax.ShapeDtypeStruct((16, w_ada.shape[1]), F32))(c16, w_ada, b_ada)


def _ada_bwd(c16, w_ada, dmx, dmc):
    def body(c_ref, w_ref, dmx_ref, dmc_ref, gw_ref, pc_ref):
        dmc_tot = dmc_ref[0:1, :]
        for d in range(1, N_DEV):
            dmc_tot = dmc_tot + dmc_ref[d:d + 1, :]
        dm16 = jnp.concatenate([dmx_ref[...], jnp.broadcast_to(dmc_tot, (8, dmc_tot.shape[1]))], axis=0)
        row = lax.broadcasted_iota(jnp.int32, dm16.shape, 0)
        dm16 = jnp.where(row <= 8, dm16, 0.0)
        s = jax.nn.silu(c_ref[...])
        gw_ref[...] = lax.dot_general(s, dm16, (_DIMS["tn"], ((), ())), precision=HI, preferred_element_type=F32)
        pc = lax.dot_general(dm16, w_ref[...], (_DIMS["nt"], ((), ())), precision=HI, preferred_element_type=F32)
        pc_ref[...] = pc[8:9, :]

    return pl.pallas_call(body, name="ada_bwd", out_shape=[jax.ShapeDtypeStruct(w_ada.shape, F32), jax.ShapeDtypeStruct((1, D), F32)],
                          compiler_params=_cp())(c16, w_ada, dmx, dmc)


def _cctx_grad(pc_all, c_ctx_row):
    def body(pc_ref, c_ref, g_ref):
        tot = pc_ref[0]
        for d in range(1, N_DEV):
            tot = tot + pc_ref[d]
        _, vjp = jax.vjp(jax.nn.silu, c_ref[...])
        g_ref[...] = vjp(tot)[0]

    return pl.pallas_call(body, name="cctx_grad", out_shape=jax.ShapeDtypeStruct((1, D), F32))(pc_all, c_ctx_row)


def _adamw(parts, w, m, v, name):
    ns, R, C = parts.shape
    tb = _tile(R, (128, 64, 32, 16, 8))

    def body(p_ref, w_ref, m_ref, v_ref, g_ref, d_ref, mo_ref, vo_ref):
        g = p_ref[0].astype(F32)
        for s in range(1, ns):
            g = g + p_ref[s].astype(F32)
        m2 = ADAM_B1 * m_ref[...] + (1.0 - ADAM_B1) * g
        v2 = ADAM_B2 * v_ref[...] + (1.0 - ADAM_B2) * jnp.square(g)
        m_hat = m2 / (1.0 - ADAM_B1 ** ADAM_STEP)
        v_hat = v2 / (1.0 - ADAM_B2 ** ADAM_STEP)
        g_ref[...] = g
        d_ref[...] = -ADAM_LR * (m_hat / (jnp.sqrt(v_hat) + ADAM_EPS) + ADAM_WD * w_ref[...])
        mo_ref[...] = m2
        vo_ref[...] = v2

    row = pl.BlockSpec((tb, C), lambda i: (i, 0))
    return pl.pallas_call(
        body, grid=(R // tb,), name=name,
        in_specs=[pl.BlockSpec((ns, tb, C), lambda i: (0, i, 0)), row, row, row], out_specs=[row] * 4,
        out_shape=[jax.ShapeDtypeStruct((R, C), F32)] * 4, compiler_params=_cp(),
    )(parts, w, m, v)


_SMALL = (("dmod_x", 6 * D), ("dmod_c", 6 * D), ("b_ada", 6 * D), ("norm_mix", D), ("norm_ffn", D), ("dn_a_log", 16),
          ("dn_dt_bias", 16), ("dn_norm", HD), ("q_norm", HD), ("k_norm", HD), ("attn_sink", NH), ("ffn_conv_b", 2 * DFF),
          ("dn_conv", 5 * 3 * D), ("ffn_conv", 3 * 2 * DFF))
_SMALL_ROWS = -(-sum(n for _, n in _SMALL) // 1024) * 8


def _pack_small(d):
    flat = jnp.concatenate([d[k].reshape(-1).astype(F32) if k in d else jnp.zeros((n,), F32) for k, n in _SMALL])
    return jnp.concatenate([flat, jnp.zeros((_SMALL_ROWS * 128 - flat.shape[0],), F32)]).reshape(_SMALL_ROWS, 128)


def _unpack_small(a):
    flat = a.reshape(a.shape[:-2] + (-1,))
    out, off = {}, 0
    for k, n in _SMALL:
        out[k] = flat[..., off:off + n]
        off += n
    return out


def kernel(x, c, ctx, c_ctx, w_ada, b_ada, norm_mix, norm_ffn, w_in, dn_conv, dn_a_log, dn_dt_bias, dn_norm, q_norm, k_norm, attn_sink, w_branch_dn, w_branch_attn, w_out, ffn_up, ffn_conv, ffn_conv_b, ffn_down, loss_target, m_c_ctx, m_w_ada, m_b_ada, m_norm_mix, m_norm_ffn, m_w_in, m_dn_conv, m_dn_a_log, m_dn_dt_bias, m_dn_norm, m_q_norm, m_k_norm, m_attn_sink, m_w_branch_dn, m_w_branch_attn, m_w_out, m_ffn_up, m_ffn_conv, m_ffn_conv_b, m_ffn_down, v_c_ctx, v_w_ada, v_b_ada, v_norm_mix, v_norm_ffn, v_w_in, v_dn_conv, v_dn_a_log, v_dn_dt_bias, v_dn_norm, v_q_norm, v_k_norm, v_attn_sink, v_w_branch_dn, v_w_branch_attn, v_w_out, v_ffn_up, v_ffn_conv, v_ffn_conv_b, v_ffn_down):
    me = 4 * lax.axis_index("x") + 2 * lax.axis_index("y") + lax.axis_index("c")
    ada_cols = w_ada.shape[2]

    cols = lambda a: jnp.swapaxes(a, 0, 1).reshape(a.shape[1], -1)
    rows = lambda a: a.reshape(-1, a.shape[2])
    col_blocks = lambda g: jnp.swapaxes(g.reshape(g.shape[0], N_DEV, -1), 0, 1)
    row_blocks = lambda g: g.reshape(N_DEV, -1, g.shape[1])

    gathered = _gather_two_level([w_in[0].astype(BF), c, dn_conv[0], ffn_conv[0]], name="gather_first")
    w_in_packed = _pack_w_in(cols(gathered[0]))
    c_all = gathered[1][:, 0, :]

    def project_in(h1):
        p, rest = _mm(h1, w_in_packed, form="nn", out_dtype=F32, name="in_proj",
                      exchange=([w_branch_dn[0].astype(BF), w_branch_attn[0].astype(BF), w_out[0].astype(BF),
                                 ffn_up[0].astype(BF), ffn_down[0].astype(BF)], False))
        return p, (w_in_packed, rows(rest[0]), rows(rest[1]), rows(rest[2]), cols(rest[3]), rows(rest[4]))

    def project_back(h1, dp, w_in_p, grads):
        g_bdn, g_bat, g_out, g_up, g_down = grads
        g_in, landed_rest = _mm(h1, dp, form="tn", out_dtype=BF, name="g_w_in",
                                exchange=([row_blocks(g_bdn), row_blocks(g_bat), row_blocks(g_out), col_blocks(g_up),
                                           row_blocks(g_down)], True))
        dh1, landed_in = _mm(dp, w_in_p, form="nt", out_dtype=F32, name="d_h1",
                             exchange=([col_blocks(_unpack_w_in(g_in))], True))
        return [landed_in[0]] + landed_rest, dh1

    c16 = jnp.concatenate([c_all, c_ctx[None], jnp.zeros((7, D), F32)], axis=0)
    b_loc = lax.dynamic_slice_in_dim(b_ada, me * ada_cols, ada_cols, axis=1)
    mod_part = _ada_fwd(c16, w_ada[0], b_loc)
    mod_all = cols(_exchange([mod_part], scatter=False, name="gather_mod")[0])
    mod_x = lax.dynamic_slice_in_dim(mod_all, me, 1, axis=0).reshape(6, D)
    mod_c = mod_all[8].reshape(6, D)

    loss_loc, grad_x, landed, small = _local_step(
        x[0], ctx[0], mod_x, mod_c, loss_target[0], project_in, project_back,
        norm_mix, norm_ffn, cols(gathered[2]), dn_a_log[0], dn_dt_bias[0], dn_norm, q_norm, k_norm, attn_sink[0], cols(gathered[3]),
        ffn_conv_b)
    loss = lax.psum(loss_loc, ("x", "y", "c"))

    res = {}
    res["w_in"] = _adamw(landed[0], w_in[0], m_w_in[0], v_w_in[0], "adamw_w_in")
    res["w_branch_dn"] = _adamw(landed[1], w_branch_dn[0], m_w_branch_dn[0], v_w_branch_dn[0], "adamw_w_branch_dn")
    res["w_branch_attn"] = _adamw(landed[2], w_branch_attn[0], m_w_branch_attn[0], v_w_branch_attn[0], "adamw_w_branch_attn")
    res["w_out"] = _adamw(landed[3], w_out[0], m_w_out[0], v_w_out[0], "adamw_w_out")
    res["ffn_up"] = _adamw(landed[4], ffn_up[0], m_ffn_up[0], v_ffn_up[0], "adamw_ffn_up")
    res["ffn_down"] = _adamw(landed[5], ffn_down[0], m_ffn_down[0], v_ffn_down[0], "adamw_ffn_down")

    small = dict(small)
    small["b_ada"] = small["dmod_x"] + small["dmod_c"]
    parts = _exchange([_pack_small(small)], scatter=False, name="gather_small")[0]
    per_dev = _unpack_small(parts)
    given = dict(b_ada=(b_ada, m_b_ada, v_b_ada), norm_mix=(norm_mix, m_norm_mix, v_norm_mix), norm_ffn=(norm_ffn, m_norm_ffn, v_norm_ffn),
                 dn_a_log=(dn_a_log, m_dn_a_log, v_dn_a_log), dn_dt_bias=(dn_dt_bias, m_dn_dt_bias, v_dn_dt_bias),
                 dn_norm=(dn_norm, m_dn_norm, v_dn_norm), q_norm=(q_norm, m_q_norm, v_q_norm), k_norm=(k_norm, m_k_norm, v_k_norm),
                 attn_sink=(attn_sink, m_attn_sink, v_attn_sink), ffn_conv_b=(ffn_conv_b, m_ffn_conv_b, v_ffn_conv_b))
    packs = [_pack_small({k: t[j] for k, t in given.items()}) for j in range(3)]
    upd = [_unpack_small(a) for a in _adamw(parts, packs[0], packs[1], packs[2], "adamw_small")]
    for k, t in given.items():
        res[k] = tuple(u[k].reshape(t[0].shape) for u in upd)
    dnc = lax.dynamic_slice_in_dim(upd[0]["dn_conv"].reshape(5, 3 * D), me * dn_conv.shape[2], dn_conv.shape[2], axis=1)
    ffc = lax.dynamic_slice_in_dim(upd[0]["ffn_conv"].reshape(3, 2 * DFF), me * ffn_conv.shape[2], ffn_conv.shape[2], axis=1)
    r8 = lambda a: _pad_rows8(a)
    t = _adamw(r8(dnc)[None], r8(dn_conv[0]), r8(m_dn_conv[0]), r8(v_dn_conv[0]), "adamw_dn_conv")
    res["dn_conv"] = tuple(a[:5][None] for a in t)
    t = _adamw(r8(ffc)[None], r8(ffn_conv[0]), r8(m_ffn_conv[0]), r8(v_ffn_conv[0]), "adamw_ffn_conv")
    res["ffn_conv"] = tuple(a[:3][None] for a in t)

    dmx = lax.dynamic_slice_in_dim(per_dev["dmod_x"], me * ada_cols, ada_cols, axis=1)
    dmc = lax.dynamic_slice_in_dim(per_dev["dmod_c"], me * ada_cols, ada_cols, axis=1)
    g_ada, pc = _ada_bwd(c16, w_ada[0], dmx, dmc)
    res["w_ada"] = _adamw(g_ada[None], w_ada[0], m_w_ada[0], v_w_ada[0], "adamw_w_ada")
    pc_all = _exchange([pc], scatter=False, name="gather_cctx")[0]
    g_cctx = _cctx_grad(pc_all, c_ctx[None])
    r8b = lambda a: jnp.broadcast_to(a, (8, D))
    t = _adamw(r8b(g_cctx)[None], r8b(c_ctx[None]), r8b(m_c_ctx[None]), r8b(v_c_ctx[None]), "adamw_c_ctx")
    res["c_ctx"] = tuple(a[0] for a in t)

    names = ("c_ctx", "w_ada", "b_ada", "norm_mix", "norm_ffn", "w_in", "dn_conv", "dn_a_log", "dn_dt_bias", "dn_norm", "q_norm",
             "k_norm", "attn_sink", "w_branch_dn", "w_branch_attn", "w_out", "ffn_up", "ffn_conv", "ffn_conv_b", "ffn_down")
    lead = ("w_ada", "w_in", "w_branch_dn", "w_branch_attn", "w_out", "ffn_up", "ffn_down")
    fix = lambda k, a: a[None] if k in lead else a
    outs = [loss, grad_x[None]]
    for j in range(4):
        outs += [fix(k, res[k][j]) for k in names]
    return tuple(outs)
```

```python
import functools

import jax
import jax.numpy as jnp
import numpy as np
from jax import lax
from jax.experimental import pallas as pl
from jax.experimental.pallas import tpu as pltpu

F32 = jnp.float32
BF = jnp.bfloat16
HI = lax.Precision.HIGHEST
MESH = pl.DeviceIdType.MESH

D = 1024
NH = 8
HD = 128
KVH = 2
GRP = 4
KV = KVH * HD
DFF = 2816
CB = 128
GRID_W = 64
ROPE_BASE = 10000.0
EPS = 1e-6
N_DEV = 8
PW = 8192
O_QKV, O_GT, O_Q, O_MG, O_K, O_V, O_BA = 0, 3072, 4096, 5120, 7168, 7424, 7680
IN_SIZES = (3072, 1024, 16, 16, 1024, 256, 256, 2048)
IN_DIM = sum(IN_SIZES)
ADAM_LR, ADAM_B1, ADAM_B2, ADAM_EPS, ADAM_WD, ADAM_STEP = 0.001, 0.9, 0.999, 1e-08, 0.01, 10
VMEM_LIMIT = 56 * 1024 * 1024


def _cp():
    return pltpu.CompilerParams(vmem_limit_bytes=VMEM_LIMIT)


def _tile(n, cands):
    for c in cands:
        if n % c == 0:
            return c
    return n


def _iota2(shape):
    return lax.broadcasted_iota(jnp.int32, shape, 0), lax.broadcasted_iota(jnp.int32, shape, 1)


_DIMS = {"nn": ((1,), (0,)), "nt": ((1,), (1,)), "tn": ((0,), (0,))}


def _exchange_copies(ins, outs, send_sems, recv_sems, local_sems, scatter, landings):
    x, y, c = lax.axis_index("x"), lax.axis_index("y"), lax.axis_index("c")
    me = 4 * x + 2 * y + c
    local, remote = [], []
    for k in range(len(ins)):
        local.append(pltpu.make_async_copy(ins[k].at[me] if scatter else ins[k], outs[k].at[me], local_sems.at[k]))
        for m in range(1, N_DEV):
            px = 1 - x if m & 4 else x
            py = 1 - y if m & 2 else y
            pc = 1 - c if m & 1 else c
            peer = 4 * px + 2 * py + pc
            src = ins[k].at[peer] if scatter else ins[k]
            sem = k * (N_DEV - 1) + m - 1
            push = pltpu.make_async_remote_copy(src_ref=src, dst_ref=outs[k].at[me], send_sem=send_sems.at[sem],
                                                recv_sem=recv_sems.at[sem], device_id=(px, py, pc), device_id_type=MESH)
            landing = None
            if landings:
                landing = pltpu.make_async_remote_copy(src_ref=src, dst_ref=outs[k].at[peer], send_sem=send_sems.at[sem],
                                                       recv_sem=recv_sems.at[sem], device_id=(px, py, pc), device_id_type=MESH)
            remote.append((push, landing))
    return local, remote


def _exchange_start(*args):
    local, remote = _exchange_copies(*args, landings=False)
    for cp in local:
        cp.start()
    for push, _ in remote:
        push.start()


def _exchange_wait(*args):
    local, remote = _exchange_copies(*args, landings=True)
    for _, landing in remote:
        landing.wait_recv()
    for push, _ in remote:
        push.wait_send()
    for cp in local:
        cp.wait()


def _exchange_shapes(arrays, scatter):
    out_shape = [jax.ShapeDtypeStruct(a.shape if scatter else (N_DEV,) + a.shape, a.dtype) for a in arrays]
    n = len(arrays)
    sems = [pltpu.SemaphoreType.DMA((n * (N_DEV - 1),)), pltpu.SemaphoreType.DMA((n * (N_DEV - 1),)), pltpu.SemaphoreType.DMA((n,))]
    return out_shape, sems


def _mm(a, b, *, form, out_dtype, name, tm=None, tn=None, tk=None, exchange=None):
    if form == "tn":
        K, M = a.shape
        N = b.shape[1]
    else:
        M, K = a.shape
        N = b.shape[0] if form == "nt" else b.shape[1]
    tm = tm or _tile(M, (1408, 1280, 1024, 640, 512, 256, 128))
    tn = tn or _tile(N, (1408, 1024, 512, 256, 128))
    tk = tk or _tile(K, (2048, 1408, 1280, 1024, 640, 512, 256, 128))
    ni, nj, nk = M // tm, N // tn, K // tk
    dims = (_DIMS[form], ((), ()))
    ex_arrays, scatter = exchange if exchange else ([], False)
    nx = len(ex_arrays)

    def body(a_ref, b_ref, *refs):
        ex_in, o_ref, ex_out, scratch = refs[:nx], refs[nx], refs[nx + 1:2 * nx + 1], refs[2 * nx + 1:]
        i, j, k = pl.program_id(0), pl.program_id(1), pl.program_id(2)
        if nx:
            sems = scratch[-3:]

            @pl.when((i == 0) & (j == 0) & (k == 0))
            def _():
                _exchange_start(ex_in, ex_out, *sems, scatter)

        part = lax.dot_general(a_ref[...].astype(BF), b_ref[...].astype(BF), dims, preferred_element_type=F32)
        if nk == 1:
            o_ref[...] = part.astype(out_dtype)
        else:
            acc_ref = scratch[0]

            @pl.when(k == 0)
            def _():
                acc_ref[...] = part

            @pl.when(k > 0)
            def _():
                acc_ref[...] += part

            @pl.when(k == nk - 1)
            def _():
                o_ref[...] = acc_ref[...].astype(out_dtype)

        if nx:
            @pl.when((i == ni - 1) & (j == nj - 1) & (k == nk - 1))
            def _():
                _exchange_wait(ex_in, ex_out, *sems, scatter)

    if form == "tn":
        a_spec = pl.BlockSpec((tk, tm), lambda i, j, k: (k, i))
    else:
        a_spec = pl.BlockSpec((tm, tk), lambda i, j, k: (i, k))
    if form == "nt":
        b_spec = pl.BlockSpec((tn, tk), lambda i, j, k: (j, k))
    else:
        b_spec = pl.BlockSpec((tk, tn), lambda i, j, k: (k, j))
    hbm = pl.BlockSpec(memory_space=pl.ANY)
    ex_shapes, ex_sems = _exchange_shapes(ex_arrays, scatter) if nx else ([], [])
    outs = pl.pallas_call(
        body, grid=(ni, nj, nk), name=name,
        in_specs=[a_spec, b_spec] + [hbm] * nx, out_specs=[pl.BlockSpec((tm, tn), lambda i, j, k: (i, j))] + [hbm] * nx,
        out_shape=[jax.ShapeDtypeStruct((M, N), out_dtype)] + ex_shapes,
        scratch_shapes=([] if nk == 1 else [pltpu.VMEM((tm, tn), F32)]) + ex_sems,
        compiler_params=_cp(),
    )(a, b, *ex_arrays)
    return (outs[0], list(outs[1:])) if nx else outs[0]


def _norm_mod_fn(x, nw, sh, sc):
    y = x * lax.rsqrt(jnp.mean(x * x, axis=-1, keepdims=True) + EPS)
    return (y * nw) * (1.0 + sc) + sh


def _norm_mod_fwd(x, ctx, nw, sh, sc, name):
    nlat = x.shape[0]
    T = nlat + ctx.shape[0]
    tb = _tile(ctx.shape[0], (256, 128))
    nlb = nlat // tb

    def body(x_ref, c_ref, nw_ref, sh_ref, sc_ref, h_ref):
        rows = jnp.where(pl.program_id(0) < nlb, x_ref[...], c_ref[...])
        h_ref[...] = _norm_mod_fn(rows, nw_ref[...], sh_ref[0], sc_ref[0]).astype(BF)

    seg = pl.BlockSpec((1, 1, D), lambda i: (jnp.where(i >= nlb, 1, 0), 0, 0))
    return pl.pallas_call(
        body, grid=(T // tb,), name=name,
        in_specs=[pl.BlockSpec((tb, D), lambda i: (jnp.minimum(i, nlb - 1), 0)),
                  pl.BlockSpec((tb, D), lambda i: (jnp.maximum(i - nlb, 0), 0)), pl.BlockSpec((1, D), lambda i: (0, 0)), seg, seg],
        out_specs=pl.BlockSpec((tb, D), lambda i: (i, 0)),
        out_shape=jax.ShapeDtypeStruct((T, D), BF),
    )(x, ctx, nw, sh, sc)


def _norm_mod_bwd(x, nw, sh, sc, dh, *, row0, name, residual=None):
    nrows = x.shape[0]
    tb = _tile(nrows, (512, 256, 128))
    b0 = row0 // tb

    def body(x_ref, nw_ref, sh_ref, sc_ref, dh_ref, *refs):
        dnw_ref, dsh_ref, dsc_ref = refs[-3:]
        _, vjp = jax.vjp(_norm_mod_fn, x_ref[...], nw_ref[...], sh_ref[...], sc_ref[...])
        dx, dnw, dsh, dsc = vjp(dh_ref[...])
        if residual is not None:
            refs[1][...] = dx + refs[0][...]

        @pl.when(pl.program_id(0) == 0)
        def _():
            dnw_ref[...] = jnp.zeros_like(dnw_ref)
            dsh_ref[...] = jnp.zeros_like(dsh_ref)
            dsc_ref[...] = jnp.zeros_like(dsc_ref)

        dnw_ref[...] += dnw
        dsh_ref[...] += dsh
        dsc_ref[...] += dsc

    dh_row = pl.BlockSpec((tb, D), lambda i: (b0 + i, 0))
    out_row = pl.BlockSpec((tb, D), lambda i: (i, 0))
    one = pl.BlockSpec((1, D), lambda i: (0, 0))
    with_dx = residual is not None
    return pl.pallas_call(
        body, grid=(nrows // tb,), name=name,
        in_specs=[out_row, one, one, one, dh_row] + [out_row] * with_dx, out_specs=[out_row] * with_dx + [one] * 3,
        out_shape=[jax.ShapeDtypeStruct((nrows, D), F32)] * with_dx + [jax.ShapeDtypeStruct((1, D), F32)] * 3,
        compiler_params=_cp(),
    )(x, nw, sh, sc, dh, *([residual] if with_dx else []))


def _resid_norm_fwd(x, gate, y, nw, sh, sc):
    n = y.shape[0]
    tb = _tile(n, (512, 256, 128))

    def body(x_ref, g_ref, y_ref, nw_ref, sh_ref, sc_ref, x1_ref, h_ref):
        x1 = x_ref[...] + g_ref[...] * y_ref[...]
        x1_ref[...] = x1
        h_ref[...] = _norm_mod_fn(x1, nw_ref[...], sh_ref[...], sc_ref[...]).astype(BF)

    row = pl.BlockSpec((tb, D), lambda i: (i, 0))
    one = pl.BlockSpec((1, D), lambda i: (0, 0))
    return pl.pallas_call(
        body, grid=(n // tb,), name="resid_norm_fwd",
        in_specs=[row, one, row, one, one, one], out_specs=[row, row],
        out_shape=[jax.ShapeDtypeStruct((n, D), F32), jax.ShapeDtypeStruct((n, D), BF)],
        compiler_params=_cp(),
    )(x, gate, y, nw, sh, sc)


def _resid_norm_bwd(x1, gate, y, nw, sh, sc, dh, dx1_direct, w_out):
    n = y.shape[0]
    tb = _tile(n, (512, 256, 128))

    def body(x1_ref, g_ref, y_ref, nw_ref, sh_ref, sc_ref, dh_ref, dd_ref, wo_ref,
             dx_ref, dy_ref, dg_ref, dnw_ref, dsh_ref, dsc_ref, dm_ref):
        _, vjp = jax.vjp(_norm_mod_fn, x1_ref[...], nw_ref[...], sh_ref[...], sc_ref[...])
        dxn, dnw, dsh, dsc = vjp(dh_ref[...])
        dx = dxn + dd_ref[...]
        dx_ref[...] = dx
        dy_ref[...] = (g_ref[...] * dx).astype(BF)
        dm_ref[...] = lax.dot_general(dy_ref[...], wo_ref[...], (_DIMS["nt"], ((), ())), preferred_element_type=F32)

        @pl.when(pl.program_id(0) == 0)
        def _():
            for r in (dg_ref, dnw_ref, dsh_ref, dsc_ref):
                r[...] = jnp.zeros_like(r)

        dg_ref[...] += jnp.sum(dx * y_ref[...], axis=0, keepdims=True)
        dnw_ref[...] += dnw
        dsh_ref[...] += dsh
        dsc_ref[...] += dsc

    row = pl.BlockSpec((tb, D), lambda i: (i, 0))
    one = pl.BlockSpec((1, D), lambda i: (0, 0))
    return pl.pallas_call(
        body, grid=(n // tb,), name="resid_norm_bwd",
        in_specs=[row, one, row, one, one, one, row, row, _resident((D, D))], out_specs=[row, row] + [one] * 4 + [row],
        out_shape=[jax.ShapeDtypeStruct((n, D), F32), jax.ShapeDtypeStruct((n, D), BF)] + [jax.ShapeDtypeStruct((1, D), F32)] * 4
        + [jax.ShapeDtypeStruct((n, D), F32)],
        compiler_params=_cp(),
    )(x1, gate, y, nw, sh, sc, dh, dx1_direct, w_out)


HALO = 8


def _halo_specs(tb, width, nrows, col=0, halo=HALO):
    r8 = tb // halo
    cur = pl.BlockSpec((tb, width), lambda i: (i, col))
    prev = pl.BlockSpec((halo, width), lambda i: (jnp.maximum(i * r8 - 1, 0), col))
    nxt = pl.BlockSpec((halo, width), lambda i: (jnp.minimum((i + 1) * r8, nrows // halo - 1), col))
    return [cur, prev, nxt]


def _segment_edges(seg_rows, tb):
    bounds = [0]
    for s in seg_rows:
        bounds.append(bounds[-1] + s // tb)
    return bounds[:-1], [b - 1 for b in bounds[1:]]


def _keep_halos(i, starts, ends):
    keep_p = functools.reduce(lambda a, b: a & b, [i != s for s in starts])
    keep_n = functools.reduce(lambda a, b: a & b, [i != e for e in ends])
    return keep_p, keep_n


def _ext_rows(refs, cols, keep):
    cur_ref, prev_ref, next_ref = refs
    p = jnp.where(keep[0], prev_ref[:, cols].astype(F32), 0.0)
    n = jnp.where(keep[1], next_ref[:, cols].astype(F32), 0.0)
    return jnp.concatenate([p, cur_ref[:, cols].astype(F32), n], axis=0)


def _shifted_rows(xe, width, transpose=False):
    r = width // 2
    n = xe.shape[0]
    out = []
    for j in range(width):
        s = ((j - r) if transpose else (r - j)) % n
        out.append(xe if s == 0 else pltpu.roll(xe, s, 0))
    return out


def _conv_rows(shifted, w_ref, cols):
    acc = None
    for j, xs in enumerate(shifted):
        term = xs * w_ref[j:j + 1, cols]
        acc = term if acc is None else acc + term
    return acc


def _tap_grads(dcur, shifted, tb):
    return [jnp.sum(dcur * xs[HALO:HALO + tb], axis=0, keepdims=True) for xs in shifted]


def _softplus(x):
    return jnp.maximum(x, 0.0) + jnp.log(1.0 + jnp.exp(-jnp.abs(x)))


def _gates_fn(ba, alog_row, dt_row):
    col = lax.broadcasted_iota(jnp.int32, ba.shape, 1)
    beta = jax.nn.sigmoid(ba)
    g = -jnp.exp(alog_row) * _softplus(ba + dt_row)
    return jnp.where(col < 16, beta, jnp.where(col < 32, g, 0.0))


def _qkv_post_fn(c, kind):
    y = jax.nn.silu(c)
    if kind == 2:
        return y
    n = y * lax.rsqrt(jnp.sum(y * y, axis=-1, keepdims=True) + EPS)
    return n * (HD ** -0.5) if kind == 0 else n


DN_TAPS = 5
FFN_TAPS = 3


def _dn_pre_fwd(p, w8, alog_row, dt_row, seg_rows):
    T = p.shape[0]
    tb = _tile(T, (256, 128))
    starts, ends = _segment_edges(seg_rows, tb)

    def body(cur_ref, prev_ref, next_ref, ba_ref, w_ref, al_ref, dt_ref, q_ref, k_ref, v_ref, gb_ref):
        keep = _keep_halos(pl.program_id(0), starts, ends)
        outs = (q_ref, k_ref, v_ref)
        for kind in range(3):
            for h in range(NH):
                cols = slice(kind * D + h * HD, kind * D + (h + 1) * HD)
                xe = _ext_rows((cur_ref, prev_ref, next_ref), cols, keep)
                conv = _conv_rows(_shifted_rows(xe, DN_TAPS), w_ref, cols)[HALO:HALO + tb]
                outs[kind][:, h * HD:(h + 1) * HD] = _qkv_post_fn(conv, kind)
        gb_ref[...] = _gates_fn(ba_ref[...], al_ref[...], dt_ref[...])

    row = pl.BlockSpec((tb, D), lambda i: (i, 0))
    one = pl.BlockSpec((1, 128), lambda i: (0, 0))
    return pl.pallas_call(
        body, grid=(T // tb,), name="dn_pre_fwd",
        in_specs=_halo_specs(tb, 3 * D, T) + [pl.BlockSpec((tb, 128), lambda i: (i, O_BA // 128)),
                                              pl.BlockSpec((8, 3 * D), lambda i: (0, 0)), one, one],
        out_specs=[row, row, row, pl.BlockSpec((tb, 128), lambda i: (i, 0))],
        out_shape=[jax.ShapeDtypeStruct((T, D), F32)] * 3 + [jax.ShapeDtypeStruct((T, 128), F32)],
        compiler_params=_cp(),
    )(p, p, p, p, w8, alog_row, dt_row)


def _dn_pre_bwd(p, w8, alog_row, dt_row, dq, dk, dv, dgb, others, seg_rows):
    T = p.shape[0]
    tb = _tile(T, (256, 128))
    starts, ends = _segment_edges(seg_rows, tb)
    other_cols = (O_GT, O_Q, O_MG, O_K, O_V)
    assert [o.shape[1] for o in others] == [O_Q - O_GT, O_MG - O_Q, O_K - O_MG, O_V - O_K, O_BA - O_V]

    def body(cur_ref, prev_ref, next_ref, ba_ref, w_ref, al_ref, dt_ref,
             dq_c, dq_p, dq_n, dk_c, dk_p, dk_n, dv_c, dv_p, dv_n, dgb_ref, gt_ref, q_ref, mg_ref, k_ref, v_ref,
             dx_ref, dw_ref, dal_ref, ddt_ref):
        i = pl.program_id(0)
        for c0, ref in zip(other_cols, (gt_ref, q_ref, mg_ref, k_ref, v_ref)):
            dx_ref[:, c0:c0 + ref.shape[1]] = ref[...]
        dx_ref[:, O_BA + 128:] = jnp.zeros((tb, PW - O_BA - 128), BF)
        keep = _keep_halos(i, starts, ends)

        @pl.when(i == 0)
        def _():
            dw_ref[...] = jnp.zeros_like(dw_ref)
            dal_ref[...] = jnp.zeros_like(dal_ref)
            ddt_ref[...] = jnp.zeros_like(ddt_ref)

        douts = ((dq_c, dq_p, dq_n), (dk_c, dk_p, dk_n), (dv_c, dv_p, dv_n))
        for kind in range(3):
            for h in range(NH):
                cols = slice(kind * D + h * HD, kind * D + (h + 1) * HD)
                xe = _ext_rows((cur_ref, prev_ref, next_ref), cols, keep)
                shifted = _shifted_rows(xe, DN_TAPS)
                conv = _conv_rows(shifted, w_ref, cols)
                dye = _ext_rows(douts[kind], slice(h * HD, (h + 1) * HD), keep)
                _, vjp = jax.vjp(functools.partial(_qkv_post_fn, kind=kind), conv)
                dce = vjp(dye)[0]
                dx_ref[:, cols] = _conv_rows(_shifted_rows(dce, DN_TAPS, transpose=True), w_ref, cols)[HALO:HALO + tb].astype(BF)
                for j, g in enumerate(_tap_grads(dce[HALO:HALO + tb], shifted, tb)):
                    dw_ref[j:j + 1, cols] += g
        _, vjp = jax.vjp(_gates_fn, ba_ref[...], al_ref[...], dt_ref[...])
        dba, dal, ddt = vjp(dgb_ref[...])
        dx_ref[:, O_BA:O_BA + 128] = dba.astype(BF)
        dal_ref[...] += dal
        ddt_ref[...] += ddt

    one = pl.BlockSpec((1, 128), lambda i: (0, 0))
    nar = pl.BlockSpec((tb, 128), lambda i: (i, 0))
    wspec = pl.BlockSpec((8, 3 * D), lambda i: (0, 0))
    return pl.pallas_call(
        body, grid=(T // tb,), name="dn_pre_bwd",
        in_specs=_halo_specs(tb, 3 * D, T) + [pl.BlockSpec((tb, 128), lambda i: (i, O_BA // 128)), wspec, one, one]
        + _halo_specs(tb, D, T) * 3 + [nar] + [pl.BlockSpec((tb, o.shape[1]), lambda i: (i, 0)) for o in others],
        out_specs=[pl.BlockSpec((tb, PW), lambda i: (i, 0)), wspec, one, one],
        out_shape=[jax.ShapeDtypeStruct((T, PW), BF), jax.ShapeDtypeStruct((8, 3 * D), F32),
                   jax.ShapeDtypeStruct((1, 128), F32), jax.ShapeDtypeStruct((1, 128), F32)],
        compiler_params=_cp(),
    )(p, p, p, p, w8, alog_row, dt_row, dq, dq, dq, dk, dk, dk, dv, dv, dv, dgb, *others)


def _dot_hi(a, b):
    return jnp.dot(a, b, precision=HI, preferred_element_type=F32)


def _bf_product(a, b, form):
    return lax.dot_general(a.astype(BF), b.astype(BF), (_DIMS[form], ((), ())), preferred_element_type=F32)


def _dot_tn_bf(a, b):
    return _bf_product(a, b, "tn")


@jax.custom_vjp
def _dot_bf(a, b):
    return _bf_product(a, b, "nn")


@jax.custom_vjp
def _dot_nt_bf(a, b):
    return _bf_product(a, b, "nt")


_dot_bf.defvjp(lambda a, b: (_bf_product(a, b, "nn"), (a, b)),
               lambda res, dc: (_bf_product(dc, res[1], "nt").astype(res[0].dtype), _bf_product(res[0], dc, "tn").astype(res[1].dtype)))
_dot_nt_bf.defvjp(lambda a, b: (_bf_product(a, b, "nt"), (a, b)),
                  lambda res, dc: (_bf_product(dc, res[1], "nn").astype(res[0].dtype), _bf_product(dc, res[0], "tn").astype(res[1].dtype)))


def _dot_h3(a, b):
    return jnp.dot(a, b, precision=lax.Precision.HIGH, preferred_element_type=F32)


def _dot_split(fine, coarse, form):
    hi = fine.astype(BF)
    lo = (fine - hi.astype(F32)).astype(BF)
    cb = coarse.astype(BF)
    if form == "tn":
        return lax.dot_general(jnp.concatenate([cb, cb], axis=0), jnp.concatenate([hi, lo], axis=0),
                               (_DIMS["tn"], ((), ())), preferred_element_type=F32)
    parts = jnp.concatenate([hi, lo], axis=1)
    if form == "nt":
        return lax.dot_general(parts, jnp.concatenate([cb, cb], axis=1), (_DIMS["nt"], ((), ())), preferred_element_type=F32)
    return jnp.dot(parts, jnp.concatenate([cb, cb], axis=0), preferred_element_type=F32)


@jax.custom_vjp
def _mm_split(a, b):
    return _dot_split(a, b, "nn")


_mm_split.defvjp(lambda a, b: (_dot_split(a, b, "nn"), (a, b)),
                 lambda res, dc: (_dot_split(dc, res[1], "nt"), _dot_split(dc, res[0], "tn")))


def _unit_tri_inverses(mats):
    r, c = _iota2((CB, CB))
    eye = (r == c).astype(F32)
    a8 = [jnp.where((r // 8) == (c // 8), a, 0.0) for a in mats]
    a2 = [_dot_split(x, x, "nn") for x in a8]
    a4 = [_dot_split(x, x, "nn") for x in a2]
    t = [_dot_split(eye - x, eye + y, "nn") for x, y in zip(a8, a2)]
    t = [_dot_split(x, eye + y, "nn") for x, y in zip(t, a4)]
    b = 8
    while b < CB:
        mask = ((r // (2 * b)) == (c // (2 * b))) & ((r // b) != (c // b))
        te = [_dot_split(x, jnp.where(mask, a, 0.0), "nn") for x, a in zip(t, mats)]
        t = [x - _dot_split(y, x, "nn") for x, y in zip(t, te)]
        b *= 2
    return t


@jax.custom_vjp
def _saved_inverse(a, t):
    return t


_saved_inverse.defvjp(lambda a, t: (t, t),
                      lambda t, dt: (-_dot_split(_dot_split(dt, t, "nt"), t, "tn"), jnp.zeros_like(t)))


def _dn1_decay(gc, reverse):
    r, c = _iota2((CB, CB))
    incl = (c >= r) if reverse else (c <= r)
    return jnp.where(incl, jnp.exp(jnp.where(incl, gc - gc.T, 0.0)), 0.0)


def _dn1_heads(qs, ks, vs, betas, gcs, ts_saved, reverse, kks=None, qks=None):
    r, c = _iota2((CB, CB))
    strict = (c > r) if reverse else (c < r)
    decays = [_dn1_decay(gc, reverse) for gc in gcs]
    kks = kks or [_dot_nt_bf(k, k) for k in ks]
    systems = [jnp.where(strict, b * kk * dc, 0.0) for b, kk, dc in zip(betas, kks, decays)]
    if ts_saved is None:
        ts = _unit_tri_inverses(systems)
    else:
        ts = [_saved_inverse(a, t) for a, t in zip(systems, ts_saved)]
    egs = [jnp.exp(gc) for gc in gcs]
    us = [_mm_split(t, v * b) for t, v, b in zip(ts, vs, betas)]
    ws = [_mm_split(t, k * (b * eg)) for t, k, b, eg in zip(ts, ks, betas, egs)]
    qks = qks or [_dot_nt_bf(q, k) for q, k in zip(qs, ks)]
    last = 0 if reverse else CB - 1
    glogs = [jnp.sum(jnp.where(r == last, gc, 0.0), axis=0, keepdims=True) for gc in gcs]
    outs = [(u, w, q * eg, k * jnp.exp(gl - gc), qk * dc, jnp.exp(gl))
            for u, w, q, k, eg, gl, gc, qk, dc in zip(us, ws, qs, ks, egs, glogs, gcs, qks, decays)]
    return outs, ts


def _cum_matrix(upper):
    r, c = _iota2((CB, CB))
    return ((c >= r) if upper else (c <= r)).astype(F32)


def _lane_bcast(x, col):
    return jnp.broadcast_to(x[:, col:col + 1], x.shape)


_HEAD_SLICES = [slice(h * HD, (h + 1) * HD) for h in range(NH)]


def _dn1_fwd(q, k, v, gb):
    T = q.shape[0]
    nb = T // CB

    def body(q_ref, k_ref, v_ref, gb_ref, *out_refs):
        gbv = gb_ref[...]
        qs = [q_ref[:, sl] for sl in _HEAD_SLICES]
        ks = [k_ref[:, sl] for sl in _HEAD_SLICES]
        vs = [v_ref[:, sl] for sl in _HEAD_SLICES]
        kks = [_dot_nt_bf(x, x) for x in ks]
        qks = [_dot_nt_bf(x, y) for x, y in zip(qs, ks)]
        for d in (0, 1):
            u_ref, w_ref, qg_ref, kd_ref, qkd_ref, gl_ref, t_ref = out_refs[7 * d:7 * d + 7]
            gcum = _dot_h3(_cum_matrix(d == 1), gbv)
            betas = [_lane_bcast(gbv, d * NH + h) for h in range(NH)]
            gcs = [_lane_bcast(gcum, 16 + d * NH + h) for h in range(NH)]
            outs, ts = _dn1_heads(qs, ks, vs, betas, gcs, None, d == 1, kks, qks)
            for h, sl in enumerate(_HEAD_SLICES):
                u, w, qg, kd, qkd, gl = outs[h]
                u_ref[:, sl] = u
                w_ref[:, sl] = w.astype(BF)
                qg_ref[:, sl] = qg.astype(BF)
                kd_ref[:, sl] = kd.astype(BF)
                qkd_ref[:, sl] = qkd.astype(BF)
                gl_ref[h] = gl
                t_ref[:, sl] = ts[h]

    tb = pl.BlockSpec((CB, D), lambda i: (i, 0))
    one_dir_specs = [tb, tb, tb, tb, tb, pl.BlockSpec((NH, 1, 128), lambda i: (i, 0, 0)), tb]
    one_dir_shapes = ([jax.ShapeDtypeStruct((T, D), F32)] + [jax.ShapeDtypeStruct((T, D), BF)] * 4
                      + [jax.ShapeDtypeStruct((nb * NH, 1, 128), F32), jax.ShapeDtypeStruct((T, D), F32)])
    outs = pl.pallas_call(
        body, grid=(nb,), name="dn1_fwd",
        in_specs=[tb, tb, tb, pl.BlockSpec((CB, 128), lambda i: (i, 0))],
        out_specs=one_dir_specs * 2, out_shape=one_dir_shapes * 2, compiler_params=_cp(),
    )(q, k, v, gb)
    return [tuple(outs[:7]), tuple(outs[7:])]


def _dn1_bwd(q, k, v, gb, tinvs, cots):
    T = q.shape[0]
    nb = T // CB

    def body(q_ref, k_ref, v_ref, gb_ref, *refs):
        dir_refs, (dq_ref, dk_ref, dv_ref, dgb_ref) = refs[:14], refs[14:]
        gbv = gb_ref[...]
        qs = [q_ref[:, sl] for sl in _HEAD_SLICES]
        ks = [k_ref[:, sl] for sl in _HEAD_SLICES]
        vs = [v_ref[:, sl] for sl in _HEAD_SLICES]
        lane = lax.broadcasted_iota(jnp.int32, (CB, 128), 1)
        dgb = jnp.zeros((CB, 128), F32)
        for d in (0, 1):
            t_ref, du_ref, dw_ref, dqg_ref, dkd_ref, dqkd_ref, dgl_ref = dir_refs[7 * d:7 * d + 7]
            gcum = _dot_h3(_cum_matrix(d == 1), gbv)
            betas = [_lane_bcast(gbv, d * NH + h) for h in range(NH)]
            gcs = [_lane_bcast(gcum, 16 + d * NH + h) for h in range(NH)]
            ts = [t_ref[:, sl] for sl in _HEAD_SLICES]
            f = lambda qs, ks, vs, betas, gcs: _dn1_heads(qs, ks, vs, betas, gcs, ts, d == 1)[0]
            _, vjp = jax.vjp(f, qs, ks, vs, betas, gcs)
            cot = [(du_ref[:, sl], dw_ref[:, sl].astype(F32), dqg_ref[:, sl].astype(F32), dkd_ref[:, sl].astype(F32),
                    dqkd_ref[:, sl].astype(F32), dgl_ref[h])
                   for h, sl in enumerate(_HEAD_SLICES)]
            dqs, dks, dvs, dbetas, dgcs = vjp(cot)
            dgcum = jnp.zeros((CB, 128), F32)
            for h, sl in enumerate(_HEAD_SLICES):
                if d == 0:
                    dq_ref[:, sl] = dqs[h]
                    dk_ref[:, sl] = dks[h]
                    dv_ref[:, sl] = dvs[h]
                else:
                    dq_ref[:, sl] += dqs[h]
                    dk_ref[:, sl] += dks[h]
                    dv_ref[:, sl] += dvs[h]
                dgb = dgb + jnp.where(lane == d * NH + h, jnp.sum(dbetas[h], axis=1, keepdims=True), 0.0)
                dgcum = dgcum + jnp.where(lane == 16 + d * NH + h, jnp.sum(dgcs[h], axis=1, keepdims=True), 0.0)
            dgb = dgb + _dot_h3(_cum_matrix(d == 0), dgcum)
        dgb_ref[...] = dgb

    tb = pl.BlockSpec((CB, D), lambda i: (i, 0))
    gbs = pl.BlockSpec((CB, 128), lambda i: (i, 0))
    gls = pl.BlockSpec((NH, 1, 128), lambda i: (i, 0, 0))
    args = []
    for d in (0, 1):
        args += [tinvs[d], *cots[d]]
    return pl.pallas_call(
        body, grid=(nb,), name="dn1_bwd",
        in_specs=[tb, tb, tb, gbs] + [tb, tb, tb, tb, tb, tb, gls] * 2, out_specs=[tb, tb, tb, gbs],
        out_shape=[jax.ShapeDtypeStruct((T, D), F32)] * 3 + [jax.ShapeDtypeStruct((T, 128), F32)],
        compiler_params=_cp(),
    )(q, k, v, gb, *args)


def _dn2_steps(chains):
    ws = [_dot_bf(w, s) for _, w, _, _, _, _, s in chains]
    v_new = [c[0] - x for c, x in zip(chains, ws)]
    o_state = [_dot_bf(c[2], c[6]) for c in chains]
    o_local = [_dot_bf(c[4], vn) for c, vn in zip(chains, v_new)]
    grow = [_dot_tn_bf(c[3], vn) for c, vn in zip(chains, v_new)]
    return [a + b for a, b in zip(o_state, o_local)], [c[6] * c[5] + g for c, g in zip(chains, grow)]


def _dn2_steps_bwd(chains, cot_o, cot_s):
    bf = lambda a: a.astype(BF)
    nt = lambda a, b: lax.dot_general(bf(a), bf(b), (_DIMS["nt"], ((), ())), preferred_element_type=F32)
    v_new = [c[0] - _dot_bf(c[1], c[6]) for c in chains]
    dv = [_dot_bf(jnp.concatenate([c[4].T, c[3]], axis=1), jnp.concatenate([do, ds], axis=0))
          for c, do, ds in zip(chains, cot_o, cot_s)]
    both = [nt(jnp.concatenate([do, x], axis=0), c[6]) for c, do, x in zip(chains, cot_o, dv)]
    dqkd = [nt(do, vn) for do, vn in zip(cot_o, v_new)]
    dkd = [nt(vn, ds) for vn, ds in zip(v_new, cot_s)]
    dstate = [_dot_bf(jnp.concatenate([c[2].T, -c[1].T], axis=1), jnp.concatenate([do, x], axis=0))
              for c, do, x in zip(chains, cot_o, dv)]
    return [(x, -b[CB:], b[:CB], dk, dq, jnp.sum(ds * c[6], axis=0, keepdims=True), ds * c[5] + g)
            for c, x, b, dk, dq, ds, g in zip(chains, dv, both, dkd, dqkd, cot_s, dstate)]


def _scan_order(direction, nlat_b, nall_b):
    if direction == 0:
        return lambda i: (i + nlat_b) % nall_b
    return lambda i: nall_b - 1 - i


def _dn2_fwd(per_dir, nlat):
    T = per_dir[0][0].shape[0]
    nb = T // CB
    blks = [_scan_order(d, nlat // CB, nb) for d in (0, 1)]

    def body(*refs):
        ins, outs, s_scr = refs[:12], refs[12:16], refs[16]

        @pl.when(pl.program_id(0) == 0)
        def _():
            s_scr[...] = jnp.zeros_like(s_scr)
        for d in (0, 1):
            outs[2 * d + 1][0] = s_scr[d].astype(BF)
        where = [(d, h, sl) for h, sl in enumerate(_HEAD_SLICES) for d in (0, 1)]
        chains = []
        for d, h, sl in where:
            u_ref, w_ref, qg_ref, kd_ref, qkd_ref, gl_ref = ins[6 * d:6 * d + 6]
            chains.append((u_ref[:, sl], w_ref[:, sl], qg_ref[:, sl], kd_ref[:, sl], qkd_ref[:, sl], gl_ref[h], s_scr[d, h]))
        os, states = _dn2_steps(chains)
        for (d, h, sl), o, s_next in zip(where, os, states):
            outs[2 * d][:, sl] = o
            s_scr[d, h] = s_next

    in_specs, out_specs, args = [], [], []
    for d in (0, 1):
        blk = blks[d]
        tb = pl.BlockSpec((CB, D), lambda i, blk=blk: (blk(i), 0))
        in_specs += [tb] * 5 + [pl.BlockSpec((NH, 1, 128), lambda i, blk=blk: (blk(i), 0, 0))]
        out_specs += [tb, pl.BlockSpec((1, NH, HD, HD), lambda i, blk=blk: (blk(i), 0, 0, 0))]
        args += list(per_dir[d])
    outs = pl.pallas_call(
        body, grid=(nb,), name="dn2_fwd", in_specs=in_specs, out_specs=out_specs,
        out_shape=[jax.ShapeDtypeStruct((T, D), F32), jax.ShapeDtypeStruct((nb, NH, HD, HD), BF)] * 2,
        scratch_shapes=[pltpu.VMEM((2, NH, HD, HD), F32)], compiler_params=_cp(),
    )(*args)
    return [tuple(outs[:2]), tuple(outs[2:])]


def _dn2_bwd(per_dir, do, nlat):
    T = per_dir[0][0].shape[0]
    nb = T // CB
    nlat_b = nlat // CB
    fwd = [_scan_order(d, nlat_b, nb) for d in (0, 1)]
    blks = [lambda i, f=f: f(nb - 1 - i) for f in fwd]

    def body(*refs):
        ins, outs, ds_scr = refs[:16], refs[16:28], refs[28]
        i = pl.program_id(0)

        @pl.when(i == 0)
        def _():
            ds_scr[...] = jnp.zeros_like(ds_scr)
        where = [(d, h, sl) for h, sl in enumerate(_HEAD_SLICES) for d in (0, 1)]
        chains, cot_o, cot_s = [], [], []
        for d, h, sl in where:
            u_ref, w_ref, qg_ref, kd_ref, qkd_ref, gl_ref, sall_ref, do_ref = ins[8 * d:8 * d + 8]
            chains.append((u_ref[:, sl], w_ref[:, sl].astype(F32), qg_ref[:, sl].astype(F32), kd_ref[:, sl].astype(F32),
                           qkd_ref[:, sl].astype(F32), gl_ref[h], sall_ref[0, h].astype(F32)))
            cot_o.append(jnp.where(blks[d](i) < nlat_b, do_ref[:, sl], 0.0))
            cot_s.append(ds_scr[d, h])
        for (d, h, sl), (du, dw, dqg, dkd, dqkd, dgl, ds) in zip(where, _dn2_steps_bwd(chains, cot_o, cot_s)):
            du_ref, dw_ref, dqg_ref, dkd_ref, dqkd_ref, dgl_ref = outs[6 * d:6 * d + 6]
            du_ref[:, sl] = du
            dw_ref[:, sl] = dw.astype(BF)
            dqg_ref[:, sl] = dqg.astype(BF)
            dkd_ref[:, sl] = dkd.astype(BF)
            dqkd_ref[:, sl] = dqkd.astype(BF)
            dgl_ref[h] = dgl
            ds_scr[d, h] = ds

    in_specs, out_specs, args = [], [], []
    for d in (0, 1):
        blk = blks[d]
        tb = pl.BlockSpec((CB, D), lambda i, blk=blk: (blk(i), 0))
        gls = pl.BlockSpec((NH, 1, 128), lambda i, blk=blk: (blk(i), 0, 0))
        in_specs += [tb] * 5 + [gls, pl.BlockSpec((1, NH, HD, HD), lambda i, blk=blk: (blk(i), 0, 0, 0)),
                                pl.BlockSpec((CB, D), lambda i, blk=blk: (jnp.minimum(blk(i), nlat_b - 1), 0))]
        out_specs += [tb] * 5 + [gls]
        args += list(per_dir[d]) + [do]
    outs = pl.pallas_call(
        body, grid=(nb,), name="dn2_bwd", in_specs=in_specs, out_specs=out_specs,
        out_shape=([jax.ShapeDtypeStruct((T, D), F32)] + [jax.ShapeDtypeStruct((T, D), BF)] * 4
                   + [jax.ShapeDtypeStruct((nb * NH, 1, 128), F32)]) * 2,
        scratch_shapes=[pltpu.VMEM((2, NH, HD, HD), F32)], compiler_params=_cp(),
    )(*args)
    return [tuple(outs[:6]), tuple(outs[6:])]


def _ghn_fn(o, gt, w):
    y = o * lax.rsqrt(jnp.mean(o * o, axis=-1, keepdims=True) + EPS)
    return (y * w) * jax.nn.silu(gt)


def _ghn_fwd(o_f, o_b, p, w, w_branch, nlat):
    tb = _tile(nlat, (512, 256, 128))

    def body(of_ref, ob_ref, gt_ref, w_ref, wb_ref, y_ref, z_ref):
        for h in range(NH):
            sl = slice(h * HD, (h + 1) * HD)
            y_ref[:, sl] = _ghn_fn(of_ref[:, sl] + ob_ref[:, sl], gt_ref[:, sl], w_ref[...]).astype(BF)
        z_ref[...] = jnp.dot(y_ref[...], wb_ref[...], preferred_element_type=F32)

    row = pl.BlockSpec((tb, D), lambda i: (i, 0))
    return pl.pallas_call(
        body, grid=(nlat // tb,), name="ghn_fwd",
        in_specs=[row, row, pl.BlockSpec((tb, D), lambda i: (i, O_GT // D)), pl.BlockSpec((1, HD), lambda i: (0, 0)), _resident((D, D))],
        out_specs=[row, row], out_shape=[jax.ShapeDtypeStruct((nlat, D), BF), jax.ShapeDtypeStruct((nlat, D), F32)],
        compiler_params=_cp(),
    )(o_f, o_b, p, w, w_branch)


def _ghn_bwd(o_f, o_b, p, w, dy, nlat):
    T = p.shape[0]
    tb = _tile(nlat, (256, 128))
    nlb = nlat // tb

    def body(of_ref, ob_ref, gt_ref, w_ref, dy_ref, do_ref, dgt_ref, dw_ref):
        is_lat = pl.program_id(0) < nlb

        @pl.when(pl.program_id(0) == 0)
        def _():
            dw_ref[...] = jnp.zeros_like(dw_ref)
        for h in range(NH):
            sl = slice(h * HD, (h + 1) * HD)
            _, vjp = jax.vjp(_ghn_fn, of_ref[:, sl] + ob_ref[:, sl], gt_ref[:, sl], w_ref[...])
            do, dgt, dw = vjp(dy_ref[:, sl])
            do_ref[:, sl] = do
            dgt_ref[:, sl] = jnp.where(is_lat, dgt, 0.0).astype(BF)
            dw_ref[...] += jnp.where(is_lat, dw, 0.0)

    lat = lambda i: jnp.minimum(i, nlb - 1)
    row = pl.BlockSpec((tb, D), lambda i: (lat(i), 0))
    one = pl.BlockSpec((1, HD), lambda i: (0, 0))
    return pl.pallas_call(
        body, grid=(T // tb,), name="ghn_bwd",
        in_specs=[row, row, pl.BlockSpec((tb, D), lambda i: (lat(i), O_GT // D)), one, row],
        out_specs=[row, pl.BlockSpec((tb, D), lambda i: (i, 0)), one],
        out_shape=[jax.ShapeDtypeStruct((nlat, D), F32), jax.ShapeDtypeStruct((T, D), BF), jax.ShapeDtypeStruct((1, HD), F32)],
    )(o_f, o_b, p, w, dy)


@jax.custom_vjp
def _swap32(x):
    lane = lax.broadcasted_iota(jnp.int32, x.shape, 1)
    return jnp.where((lane & 32) == 0, pltpu.roll(x, 96, 1), pltpu.roll(x, 32, 1))


_swap32.defvjp(lambda x: (_swap32(x), None), lambda _, g: (_swap32(g),))


def _qk_post_fn(xs, w, cos, sin):
    inv = [lax.rsqrt(jnp.mean(x * x, axis=-1, keepdims=True) + EPS) for x in xs]
    ys = [(x * r) * w for x, r in zip(xs, inv)]
    return [y * cos + _swap32(y) * sin for y in ys]


def _attn_prep_fwd(p, qn, kn, cos, sin):
    T = p.shape[0]
    tb = _tile(T, (256, 128))

    def body(q_ref, k_ref, v_ref, qn_ref, kn_ref, cos_ref, sin_ref, qr_ref, kr_ref, vb_ref):
        cos_v, sin_v = cos_ref[...], sin_ref[...]
        for sl, y in zip(_HEAD_SLICES, _qk_post_fn([q_ref[:, sl] for sl in _HEAD_SLICES], qn_ref[...], cos_v, sin_v)):
            qr_ref[:, sl] = y.astype(BF)
        for sl, y in zip(_HEAD_SLICES, _qk_post_fn([k_ref[:, sl] for sl in _HEAD_SLICES[:KVH]], kn_ref[...], cos_v, sin_v)):
            kr_ref[:, sl] = y.astype(BF)
        vb_ref[...] = v_ref[...].astype(BF)

    one = pl.BlockSpec((1, HD), lambda i: (0, 0))
    tab = pl.BlockSpec((tb, HD), lambda i: (i, 0))
    return pl.pallas_call(
        body, grid=(T // tb,), name="attn_prep_fwd",
        in_specs=[pl.BlockSpec((tb, D), lambda i: (i, O_Q // D)), pl.BlockSpec((tb, KV), lambda i: (i, O_K // KV)),
                  pl.BlockSpec((tb, KV), lambda i: (i, O_V // KV)), one, one, tab, tab],
        out_specs=[pl.BlockSpec((tb, D), lambda i: (i, 0)), pl.BlockSpec((tb, KV), lambda i: (i, 0)),
                   pl.BlockSpec((tb, KV), lambda i: (i, 0))],
        out_shape=[jax.ShapeDtypeStruct((T, D), BF), jax.ShapeDtypeStruct((T, KV), BF), jax.ShapeDtypeStruct((T, KV), BF)],
    )(p, p, p, qn, kn, cos, sin)


def _attn_prep_bwd(p, qn, kn, cos, sin, dqr, dkp, dvp, dkc, dvc, nlat):
    T = p.shape[0]
    nqb = nlat // CB
    ncb = (T - nlat) // CB

    def body(q_ref, k_ref, v_ref, qn_ref, kn_ref, cos_ref, sin_ref, dqr_ref, dka_ref, dkb_ref, dkc3_ref, dva_ref, dvb_ref, dvc3_ref,
             dkctx_ref, dvctx_ref, dq_ref, dk_ref, dv_ref, dqn_ref, dkn_ref):
        i = pl.program_id(0)
        is_lat = i < nqb
        cos_v, sin_v = cos_ref[...], sin_ref[...]

        @pl.when(i == 0)
        def _():
            dqn_ref[...] = jnp.zeros_like(dqn_ref)
            dkn_ref[...] = jnp.zeros_like(dkn_ref)

        def band_sum(a_ref, b_ref, c_ref, ctx_ref):
            s = b_ref[0] + jnp.where(i > 0, a_ref[0], 0.0) + jnp.where(i < nqb - 1, c_ref[0], 0.0)
            return jnp.where(is_lat, s, ctx_ref[...])

        dkr = band_sum(dka_ref, dkb_ref, dkc3_ref, dkctx_ref)
        dv_ref[...] = band_sum(dva_ref, dvb_ref, dvc3_ref, dvctx_ref).astype(BF)
        post = lambda xs, w: _qk_post_fn(xs, w, cos_v, sin_v)
        _, vjp = jax.vjp(post, [q_ref[:, sl] for sl in _HEAD_SLICES], qn_ref[...])
        dqs, dqn = vjp([jnp.where(is_lat, dqr_ref[:, sl], 0.0) for sl in _HEAD_SLICES])
        for sl, dq in zip(_HEAD_SLICES, dqs):
            dq_ref[:, sl] = dq.astype(BF)
        dqn_ref[...] += dqn
        _, vjp = jax.vjp(post, [k_ref[:, sl] for sl in _HEAD_SLICES[:KVH]], kn_ref[...])
        dks, dkn = vjp([dkr[:, sl] for sl in _HEAD_SLICES[:KVH]])
        for sl, dk in zip(_HEAD_SLICES, dks):
            dk_ref[:, sl] = dk.astype(BF)
        dkn_ref[...] += dkn

    one = pl.BlockSpec((1, HD), lambda i: (0, 0))
    tab = pl.BlockSpec((CB, HD), lambda i: (i, 0))
    lat = lambda i: jnp.minimum(i, nqb - 1)

    def part(off, slot):
        return pl.BlockSpec((1, CB, KV), lambda i: (jnp.clip(lat(i) + off, 0, nqb - 1) * 3 + slot, 0, 0))

    ctxs = pl.BlockSpec((CB, KV), lambda i: (jnp.clip(i - nqb, 0, ncb - 1), 0))
    kvs = pl.BlockSpec((CB, KV), lambda i: (i, 0))
    return pl.pallas_call(
        body, grid=(T // CB,), name="attn_prep_bwd",
        in_specs=[pl.BlockSpec((CB, D), lambda i: (i, O_Q // D)), pl.BlockSpec((CB, KV), lambda i: (i, O_K // KV)),
                  pl.BlockSpec((CB, KV), lambda i: (i, O_V // KV)), one, one, tab, tab,
                  pl.BlockSpec((CB, D), lambda i: (lat(i), 0)),
                  part(-1, 2), part(0, 1), part(1, 0), part(-1, 2), part(0, 1), part(1, 0), ctxs, ctxs],
        out_specs=[pl.BlockSpec((CB, D), lambda i: (i, 0)), kvs, kvs, one, one],
        out_shape=[jax.ShapeDtypeStruct((T, D), BF), jax.ShapeDtypeStruct((T, KV), BF), jax.ShapeDtypeStruct((T, KV), BF),
                   jax.ShapeDtypeStruct((1, HD), F32), jax.ShapeDtypeStruct((1, HD), F32)],
    )(p, p, p, qn, kn, cos, sin, dqr, dkp, dkp, dkp, dvp, dvp, dvp, dkc, dvc)


def _attn_groups_fn(qs, kalls, valls, sinks, bias):
    groups = range(KVH)
    q = [jnp.concatenate(qs[GRP * g:GRP * (g + 1)], axis=0) for g in groups]
    s = [_bf_product(q[g], kalls[g], "nt") * (HD ** -0.5) + bias for g in groups]
    sk = [jnp.concatenate([jnp.broadcast_to(jnp.mean(t, axis=1, keepdims=True), (CB, 1)) for t in sinks[GRP * g:GRP * (g + 1)]],
                          axis=0) for g in groups]
    m = [lax.stop_gradient(jnp.maximum(jnp.max(s[g], axis=1, keepdims=True), sk[g])) for g in groups]
    e = [jnp.exp(s[g] - m[g]) for g in groups]
    den = [jnp.sum(e[g], axis=1, keepdims=True) + jnp.exp(sk[g] - m[g]) for g in groups]
    return [_bf_product(e[g] / den[g], valls[g], "nn") for g in groups]


def _attn_bias(lc):
    r, c = _iota2((GRP * CB, 3 * CB + lc))
    rel = c - (r & (CB - 1))
    win = (rel >= 0) & (rel <= 2 * CB)
    ctx = c >= 3 * CB
    seen = [(win & (c >= CB)) | ctx, win | ctx, (win & (c < 2 * CB)) | ctx]
    return jnp.stack([jnp.where(s, 0.0, -1e30) for s in seen]).astype(F32)


def _attn_specs(nqb, lc, nlat):
    assert nqb >= 2
    qs = pl.BlockSpec((CB, D), lambda i: (i, 0))
    ka = pl.BlockSpec((CB, KV), lambda i: (jnp.maximum(i - 1, 0), 0))
    kb = pl.BlockSpec((CB, KV), lambda i: (i, 0))
    kc = pl.BlockSpec((CB, KV), lambda i: (jnp.minimum(i + 1, nqb - 1), 0))
    kx = pl.BlockSpec((lc, KV), lambda i: (nlat // lc, 0))
    sk = pl.BlockSpec((KVH, 8, 128), lambda i: (0, 0, 0))
    bs = pl.BlockSpec((1, GRP * CB, 3 * CB + lc), lambda i: (jnp.where(i == 0, 0, jnp.where(i == nqb - 1, 2, 1)), 0, 0))
    return qs, ka, kb, kc, kx, sk, bs


def _attn_operands(q_ref, k_refs, v_refs, sk_ref, dtype):
    sls = [slice(g * HD, (g + 1) * HD) for g in range(KVH)]
    kalls = [jnp.concatenate([r[:, sl] for r in k_refs], axis=0).astype(dtype) for sl in sls]
    valls = [jnp.concatenate([r[:, sl] for r in v_refs], axis=0).astype(dtype) for sl in sls]
    qs = [q_ref[:, sl].astype(dtype) for sl in _HEAD_SLICES]
    sinks = [sk_ref[h // GRP, (h % GRP):(h % GRP) + 1, :] for h in range(NH)]
    return qs, kalls, valls, sinks


def _attn_fwd(qr, kr, vb, sink, w_branch, nlat):
    lc = kr.shape[0] - nlat
    nqb = nlat // CB
    qs, ka, kb, kc, kx, sk, bs = _attn_specs(nqb, lc, nlat)

    def body(q_ref, ka_ref, kb_ref, kc_ref, kx_ref, va_ref, vb_ref, vc_ref, vx_ref, sk_ref, bias_ref, wb_ref, o_ref, z_ref):
        operands = _attn_operands(q_ref, (ka_ref, kb_ref, kc_ref, kx_ref), (va_ref, vb_ref, vc_ref, vx_ref), sk_ref, BF)
        outs = _attn_groups_fn(*operands, bias_ref[0])
        for h, sl in enumerate(_HEAD_SLICES):
            o_ref[:, sl] = outs[h // GRP][(h % GRP) * CB:(h % GRP + 1) * CB].astype(BF)
        z_ref[...] = jnp.dot(o_ref[...], wb_ref[...], preferred_element_type=F32)

    return pl.pallas_call(
        body, grid=(nqb,), name="attn_fwd",
        in_specs=[qs, ka, kb, kc, kx, ka, kb, kc, kx, sk, bs, _resident((D, D))], out_specs=[qs, qs],
        out_shape=[jax.ShapeDtypeStruct((nlat, D), BF), jax.ShapeDtypeStruct((nlat, D), F32)], compiler_params=_cp(),
    )(qr, kr, kr, kr, kr, vb, vb, vb, vb, sink, _attn_bias(lc), w_branch)


def _attn_bwd(qr, kr, vb, sink, dy, nlat):
    lc = kr.shape[0] - nlat
    nqb = nlat // CB
    qs, ka, kb, kc, kx, sk, bs = _attn_specs(nqb, lc, nlat)

    def body(q_ref, ka_ref, kb_ref, kc_ref, kx_ref, va_ref, vb_ref, vc_ref, vx_ref, sk_ref, dy_ref, bias_ref,
             dq_ref, dkp_ref, dvp_ref, dkx_ref, dvx_ref, dsk_ref):
        operands = _attn_operands(q_ref, (ka_ref, kb_ref, kc_ref, kx_ref), (va_ref, vb_ref, vc_ref, vx_ref), sk_ref, F32)
        _, vjp = jax.vjp(functools.partial(_attn_groups_fn, bias=bias_ref[0]), *operands)
        dys_g = [jnp.concatenate([dy_ref[:, sl] for sl in _HEAD_SLICES[GRP * g:GRP * (g + 1)]], axis=0) for g in range(KVH)]
        dqs, dks, dvs, dsinks = vjp(dys_g)

        @pl.when(pl.program_id(0) == 0)
        def _():
            dkx_ref[...] = jnp.zeros_like(dkx_ref)
            dvx_ref[...] = jnp.zeros_like(dvx_ref)
            dsk_ref[...] = jnp.zeros_like(dsk_ref)

        for h, sl in enumerate(_HEAD_SLICES):
            dq_ref[:, sl] = dqs[h]
            dsk_ref[h // GRP, (h % GRP):(h % GRP) + 1, :] += dsinks[h]
        for g in range(KVH):
            sl = slice(g * HD, (g + 1) * HD)
            for t in range(3):
                dkp_ref[t, :, sl] = dks[g][t * CB:(t + 1) * CB]
                dvp_ref[t, :, sl] = dvs[g][t * CB:(t + 1) * CB]
            dkx_ref[:, sl] += dks[g][3 * CB:]
            dvx_ref[:, sl] += dvs[g][3 * CB:]

    dys = qs
    parts = pl.BlockSpec((3, CB, KV), lambda i: (i, 0, 0))
    ctxo = pl.BlockSpec((lc, KV), lambda i: (0, 0))
    return pl.pallas_call(
        body, grid=(nqb,), name="attn_bwd",
        in_specs=[qs, ka, kb, kc, kx, ka, kb, kc, kx, sk, dys, bs],
        out_specs=[dys, parts, parts, ctxo, ctxo, sk],
        out_shape=[jax.ShapeDtypeStruct((nlat, D), F32), jax.ShapeDtypeStruct((3 * nqb, CB, KV), F32),
                   jax.ShapeDtypeStruct((3 * nqb, CB, KV), F32), jax.ShapeDtypeStruct((lc, KV), F32),
                   jax.ShapeDtypeStruct((lc, KV), F32), jax.ShapeDtypeStruct((KVH, 8, 128), F32)],
        compiler_params=_cp(),
    )(qr, kr, kr, kr, kr, vb, vb, vb, vb, sink, dy, _attn_bias(lc))


def _merge_fn(z_dn, z_at, g_dn, g_at):
    return jax.nn.sigmoid(g_dn) * z_dn + jax.nn.sigmoid(g_at) * z_at


def _merge_fwd(z_dn, z_at, p, w_out, nlat):
    tb = _tile(nlat, (512, 256, 128))

    def body(zd_ref, za_ref, gd_ref, ga_ref, wo_ref, o_ref, mix_ref):
        o_ref[...] = _merge_fn(zd_ref[...], za_ref[...], gd_ref[...], ga_ref[...]).astype(BF)
        mix_ref[...] = jnp.dot(o_ref[...], wo_ref[...], preferred_element_type=F32)

    row = pl.BlockSpec((tb, D), lambda i: (i, 0))
    return pl.pallas_call(
        body, grid=(nlat // tb,), name="merge_fwd",
        in_specs=[row, row, pl.BlockSpec((tb, D), lambda i: (i, O_MG // D)), pl.BlockSpec((tb, D), lambda i: (i, O_MG // D + 1)),
                  _resident((D, D))],
        out_specs=[row, row], out_shape=[jax.ShapeDtypeStruct((nlat, D), BF), jax.ShapeDtypeStruct((nlat, D), F32)],
        compiler_params=_cp(),
    )(z_dn, z_at, p, p, w_out)


def _merge_bwd(z_dn, z_at, p, dm, w_bdn, w_bat, nlat):
    T = p.shape[0]
    tb = _tile(nlat, (256, 128))
    nlb = nlat // tb

    def body(zd_ref, za_ref, gd_ref, ga_ref, dm_ref, wd_ref, wa_ref, dzd_ref, dza_ref, dg_ref, dyd_ref, dya_ref):
        is_lat = pl.program_id(0) < nlb
        _, vjp = jax.vjp(_merge_fn, zd_ref[...], za_ref[...], gd_ref[...], ga_ref[...])
        dzd, dza, dgd, dga = vjp(dm_ref[...])
        dzd_ref[...] = dzd.astype(BF)
        dza_ref[...] = dza.astype(BF)
        dg_ref[:, :D] = jnp.where(is_lat, dgd, 0.0).astype(BF)
        dg_ref[:, D:] = jnp.where(is_lat, dga, 0.0).astype(BF)
        dyd_ref[...] = lax.dot_general(dzd_ref[...], wd_ref[...], (_DIMS["nt"], ((), ())), preferred_element_type=F32)
        dya_ref[...] = lax.dot_general(dza_ref[...], wa_ref[...], (_DIMS["nt"], ((), ())), preferred_element_type=F32)

    lat = lambda i: jnp.minimum(i, nlb - 1)
    row = pl.BlockSpec((tb, D), lambda i: (lat(i), 0))
    return pl.pallas_call(
        body, grid=(T // tb,), name="merge_bwd",
        in_specs=[row, row, pl.BlockSpec((tb, D), lambda i: (lat(i), O_MG // D)),
                  pl.BlockSpec((tb, D), lambda i: (lat(i), O_MG // D + 1)), row, _resident((D, D)), _resident((D, D))],
        out_specs=[row, row, pl.BlockSpec((tb, 2 * D), lambda i: (i, 0)), row, row],
        out_shape=[jax.ShapeDtypeStruct((nlat, D), BF), jax.ShapeDtypeStruct((nlat, D), BF), jax.ShapeDtypeStruct((T, 2 * D), BF),
                   jax.ShapeDtypeStruct((nlat, D), F32), jax.ShapeDtypeStruct((nlat, D), F32)],
    )(z_dn, z_at, p, p, dm, w_bdn, w_bat)


def _swiglu_fn(ug, uv):
    return jax.nn.silu(ug) * uv


FFN_GROUP = 256


def _resident(shape):
    return pl.BlockSpec(shape, lambda i: (0,) * len(shape), pipeline_mode=pl.Buffered(1))


H_HALO = 16


def _up_project(h_refs, wu_ref, u_scr):
    cur_ref, prev_ref, next_ref = h_refs
    rows = jnp.concatenate([prev_ref[...], cur_ref[...], next_ref[...]], axis=0)
    u_scr[...] = jnp.dot(rows, wu_ref[...], preferred_element_type=F32)


def _up_ext_rows(u_scr, cols, keep, tb):
    xe = u_scr[H_HALO - HALO:H_HALO + tb + HALO, cols]
    r = lax.broadcasted_iota(jnp.int32, (tb + 2 * HALO, 1), 0)
    inside = ((r >= HALO) | keep[0]) & ((r < HALO + tb) | keep[1])
    return jnp.where(inside, xe, 0.0)


def _ffn_fwd(h, w_up, w8, bias, w_down):
    n = h.shape[0]
    tb = _tile(n, (256, 128))
    starts, ends = _segment_edges((n,), tb)

    def body(cur_ref, prev_ref, next_ref, wu_ref, w_ref, b_ref, wd_ref, u_ref, o_ref, ff_ref, u_scr):
        keep = _keep_halos(pl.program_id(0), starts, ends)
        _up_project((cur_ref, prev_ref, next_ref), wu_ref, u_scr)
        u_ref[...] = u_scr[H_HALO:H_HALO + tb, :]

        for c0 in range(0, DFF, FFN_GROUP):
            halves = []
            for cols in (slice(c0, c0 + FFN_GROUP), slice(DFF + c0, DFF + c0 + FFN_GROUP)):
                xe = _up_ext_rows(u_scr, cols, keep, tb)
                halves.append(_conv_rows(_shifted_rows(xe, FFN_TAPS), w_ref, cols)[HALO:HALO + tb] + b_ref[:, cols])
            o_ref[:, c0:c0 + FFN_GROUP] = _swiglu_fn(*halves).astype(BF)
        ff_ref[...] = jnp.dot(o_ref[...], wd_ref[...], preferred_element_type=F32)

    return pl.pallas_call(
        body, grid=(n // tb,), name="ffn_fwd",
        in_specs=_halo_specs(tb, D, n, halo=H_HALO) + [_resident((D, 2 * DFF)), pl.BlockSpec((8, 2 * DFF), lambda i: (0, 0)),
                                                        pl.BlockSpec((1, 2 * DFF), lambda i: (0, 0)), _resident((DFF, D))],
        out_specs=[pl.BlockSpec((tb, 2 * DFF), lambda i: (i, 0)), pl.BlockSpec((tb, DFF), lambda i: (i, 0)),
                   pl.BlockSpec((tb, D), lambda i: (i, 0))],
        out_shape=[jax.ShapeDtypeStruct((n, 2 * DFF), F32), jax.ShapeDtypeStruct((n, DFF), BF), jax.ShapeDtypeStruct((n, D), F32)],
        scratch_shapes=[pltpu.VMEM((tb + 2 * H_HALO, 2 * DFF), F32)],
        compiler_params=_cp(),
    )(h, h, h, w_up, w8, bias, w_down)


def _ffn_bwd(u, w_up, w8, bias, da):
    n = u.shape[0]
    tb = _tile(n, (256, 128))
    starts, ends = _segment_edges((n,), tb)

    def body(cur_ref, prev_ref, next_ref, wu_ref, w_ref, b_ref, da_c, da_p, da_n, du_ref, dw_ref, db_ref, dh_ref):
        i = pl.program_id(0)
        keep = _keep_halos(i, starts, ends)

        @pl.when(i == 0)
        def _():
            dw_ref[...] = jnp.zeros_like(dw_ref)
            db_ref[...] = jnp.zeros_like(db_ref)

        for c0 in range(0, DFF, FFN_GROUP):
            col_pair = (slice(c0, c0 + FFN_GROUP), slice(DFF + c0, DFF + c0 + FFN_GROUP))
            shifts = [_shifted_rows(_ext_rows((cur_ref, prev_ref, next_ref), cols, keep), FFN_TAPS) for cols in col_pair]
            convs = [_conv_rows(shifted, w_ref, cols) + b_ref[:, cols] for shifted, cols in zip(shifts, col_pair)]
            dae = _ext_rows((da_c, da_p, da_n), col_pair[0], keep)
            _, vjp = jax.vjp(_swiglu_fn, *convs)
            for shifted, cols, dce in zip(shifts, col_pair, vjp(dae)):
                du_ref[:, cols] = _conv_rows(_shifted_rows(dce, FFN_TAPS, transpose=True), w_ref, cols)[HALO:HALO + tb].astype(BF)
                dcur = dce[HALO:HALO + tb]
                for j, g in enumerate(_tap_grads(dcur, shifted, tb)):
                    dw_ref[j:j + 1, cols] += g
                db_ref[:, cols] += jnp.sum(dcur, axis=0, keepdims=True)
        dh_ref[...] = lax.dot_general(du_ref[...], wu_ref[...], (_DIMS["nt"], ((), ())), preferred_element_type=F32)

    wspec = pl.BlockSpec((8, 2 * DFF), lambda i: (0, 0))
    bspec = pl.BlockSpec((1, 2 * DFF), lambda i: (0, 0))
    return pl.pallas_call(
        body, grid=(n // tb,), name="ffn_bwd",
        in_specs=_halo_specs(tb, 2 * DFF, n) + [_resident((D, 2 * DFF)), wspec, bspec] + _halo_specs(tb, DFF, n),
        out_specs=[pl.BlockSpec((tb, 2 * DFF), lambda i: (i, 0)), wspec, bspec, pl.BlockSpec((tb, D), lambda i: (i, 0))],
        out_shape=[jax.ShapeDtypeStruct((n, 2 * DFF), BF), jax.ShapeDtypeStruct((8, 2 * DFF), F32), jax.ShapeDtypeStruct((1, 2 * DFF), F32),
                   jax.ShapeDtypeStruct((n, D), F32)],
        compiler_params=_cp(),
    )(u, u, u, w_up, w8, bias, da, da, da)


def _loss_kernel(x1, gate, ff, target, w_down):
    n = x1.shape[0]
    tb = _tile(n, (512, 256, 128))

    def body(x_ref, g_ref, f_ref, t_ref, wd_ref, loss_ref, dy_ref, dff_ref, dg_ref, da_ref):
        err = x_ref[...] + g_ref[...] * f_ref[...] - t_ref[...]
        dy = err * (1.0 / D)
        dy_ref[...] = dy
        dff_ref[...] = (g_ref[...] * dy).astype(BF)
        da_ref[...] = lax.dot_general(dff_ref[...], wd_ref[...], (_DIMS["nt"], ((), ())), preferred_element_type=F32)

        @pl.when(pl.program_id(0) == 0)
        def _():
            loss_ref[...] = jnp.zeros_like(loss_ref)
            dg_ref[...] = jnp.zeros_like(dg_ref)
        part = 0.5 * jnp.sum(jnp.sum(err * err, axis=1, keepdims=True) * (1.0 / D), axis=0, keepdims=True)
        loss_ref[...] += jnp.broadcast_to(part, (1, 128))
        dg_ref[...] += jnp.sum(dy * f_ref[...], axis=0, keepdims=True)

    row = pl.BlockSpec((tb, D), lambda i: (i, 0))
    one = pl.BlockSpec((1, D), lambda i: (0, 0))
    return pl.pallas_call(
        body, grid=(n // tb,), name="loss",
        in_specs=[row, one, row, row, _resident((DFF, D))],
        out_specs=[pl.BlockSpec((1, 128), lambda i: (0, 0)), row, row, one, pl.BlockSpec((tb, DFF), lambda i: (i, 0))],
        out_shape=[jax.ShapeDtypeStruct((1, 128), F32), jax.ShapeDtypeStruct((n, D), F32),
                   jax.ShapeDtypeStruct((n, D), BF), jax.ShapeDtypeStruct((1, D), F32), jax.ShapeDtypeStruct((n, DFF), F32)],
        compiler_params=_cp(),
    )(x1, gate, ff, target, w_down)


def _rope_tables(nlat, lc):
    inv_freq = (np.float32(ROPE_BASE) ** (-np.arange(32, dtype=np.float32) / np.float32(32))).astype(np.float32)
    ar = np.arange(nlat // GRID_W, dtype=np.float32)[:, None] * inv_freq
    ac = np.arange(GRID_W, dtype=np.float32)[:, None] * inv_freq
    by_row = lambda a: jnp.repeat(jnp.asarray(a, F32), GRID_W, axis=0)
    by_col = lambda a: jnp.tile(jnp.asarray(a, F32), (nlat // GRID_W, 1))
    cos = jnp.concatenate([by_row(np.cos(ar)), by_row(np.cos(ar)), by_col(np.cos(ac)), by_col(np.cos(ac))], axis=1)
    sin = jnp.concatenate([by_row(-np.sin(ar)), by_row(np.sin(ar)), by_col(-np.sin(ac)), by_col(np.sin(ac))], axis=1)
    cos = jnp.concatenate([cos, jnp.ones((lc, HD), F32)], axis=0)
    sin = jnp.concatenate([sin, jnp.zeros((lc, HD), F32)], axis=0)
    return cos, sin


def _pad_rows8(w):
    return jnp.concatenate([w, jnp.zeros((8 - w.shape[0], w.shape[1]), w.dtype)], axis=0)


def _pack_w_in(w):
    cuts = [sum(IN_SIZES[:i]) for i in range(len(IN_SIZES) + 1)]
    qkv, gt, b, a, q, k, v, mg = [w[:, cuts[i]:cuts[i + 1]] for i in range(len(IN_SIZES))]
    return jnp.concatenate([qkv, gt, q, mg, k, v, b, a, jnp.zeros((w.shape[0], PW - O_BA - 32), w.dtype)], axis=1)


def _unpack_w_in(g):
    return jnp.concatenate([g[:, O_QKV:O_GT], g[:, O_GT:O_Q], g[:, O_BA:O_BA + 32], g[:, O_Q:O_MG], g[:, O_K:O_V],
                            g[:, O_V:O_BA], g[:, O_MG:O_K]], axis=1)


def _local_step(x, ctx, mod_x, mod_c, target, project_in, project_back,
                norm_mix, norm_ffn, dn_conv, a_log, dt_bias, dn_norm, q_norm, k_norm, sink, ffn_conv, ffn_conv_b):
    L, LC = x.shape[0], ctx.shape[0]
    T = L + LC
    seg = lambda r: jnp.stack([mod_x[r], mod_c[r]])[:, None, :]
    sh_a, sc_a = seg(0), seg(1)
    g_a, g_f = mod_x[2][None], mod_x[5][None]
    sh_f, sc_f = mod_x[3][None], mod_x[4][None]
    cos, sin = _rope_tables(L, LC)
    dnc8 = _pad_rows8(dn_conv)
    ffc8 = _pad_rows8(ffn_conv)
    gate_row = lambda a: jnp.concatenate([jnp.zeros((1, 16), F32), a.reshape(1, 16), jnp.zeros((1, 96), F32)], axis=1)
    alog_row, dt_row = gate_row(a_log), gate_row(dt_bias)
    sinkb = jnp.concatenate([jnp.broadcast_to(sink.reshape(KVH, GRP, 1), (KVH, GRP, 128)), jnp.zeros((KVH, 8 - GRP, 128), F32)], axis=1)

    h1 = _norm_mod_fwd(x, ctx, norm_mix, sh_a, sc_a, "norm_mix_fwd")
    p, (w_in_p, w_bdn, w_bat, w_out, w_up, w_down) = project_in(h1)
    q, k, v, gb = _dn_pre_fwd(p, dnc8, alog_row, dt_row, (L, LC))
    wy = _dn1_fwd(q, k, v, gb)
    scans = _dn2_fwd([t[:6] for t in wy], L)
    o_dir = [s[0] for s in scans]
    y_dn, z_dn = _ghn_fwd(o_dir[0], o_dir[1], p, dn_norm, w_bdn, L)
    qr, kr, vb = _attn_prep_fwd(p, q_norm, k_norm, cos, sin)
    y_at, z_at = _attn_fwd(qr, kr, vb, sinkb, w_bat, L)
    merged, mix = _merge_fwd(z_dn, z_at, p, w_out, L)
    x1, h2 = _resid_norm_fwd(x, g_a, mix, norm_ffn, sh_f, sc_f)
    u_raw, act, ff = _ffn_fwd(h2, w_up, ffc8, ffn_conv_b, w_down)
    loss_row, dy, dff, dg_f, dact = _loss_kernel(x1, g_f, ff, target, w_down)

    g_down = _mm(act, dff, form="tn", out_dtype=BF, name="g_ffn_down")
    du_raw, g_ffc8, g_ffb, dh2 = _ffn_bwd(u_raw, w_up, ffc8, ffn_conv_b, dact)
    g_up = _mm(h2, du_raw, form="tn", out_dtype=BF, name="g_ffn_up")
    dx1, dmix, dg_a, g_nffn, dsh_f, dsc_f, dmerged = _resid_norm_bwd(x1, g_a, mix, norm_ffn, sh_f, sc_f, dh2, dy, w_out)

    g_out = _mm(merged, dmix, form="tn", out_dtype=BF, name="g_w_out")
    dz_dn, dz_at, dmg, dy_dn, dy_at = _merge_bwd(z_dn, z_at, p, dmerged, w_bdn, w_bat, L)
    g_bdn = _mm(y_dn, dz_dn, form="tn", out_dtype=BF, name="g_branch_dn")
    g_bat = _mm(y_at, dz_at, form="tn", out_dtype=BF, name="g_branch_at")
    dqr, dkp, dvp, dkx, dvx, dsink = _attn_bwd(qr, kr, vb, sinkb, dy_at, L)
    dq_raw, dk_raw, dv_raw, g_qn, g_kn = _attn_prep_bwd(p, q_norm, k_norm, cos, sin, dqr, dkp, dvp, dkx, dvx, L)
    do, dgt, g_dnn = _ghn_bwd(o_dir[0], o_dir[1], p, dn_norm, dy_dn, L)
    cots = _dn2_bwd([wy[d][:6] + (scans[d][1],) for d in (0, 1)], do, L)
    dq, dk, dv, dgb = _dn1_bwd(q, k, v, gb, [t[6] for t in wy], cots)
    dp, g_dnc8, g_alog, g_dt = _dn_pre_bwd(p, dnc8, alog_row, dt_row, dq, dk, dv, dgb, (dgt, dq_raw, dmg, dk_raw, dv_raw), (L, LC))
    big, dh1 = project_back(h1, dp, w_in_p, (g_bdn, g_bat, g_out, g_up, g_down))
    grad_x, g_nmix_x, dsh_a, dsc_a = _norm_mod_bwd(x, norm_mix, mod_x[0][None], mod_x[1][None], dh1, row0=0,
                                                   name="norm_mix_bwd", residual=dx1)
    g_nmix_c, dsh_c, dsc_c = _norm_mod_bwd(ctx, norm_mix, mod_c[0][None], mod_c[1][None], dh1, row0=L, name="norm_mix_bwd_ctx")
    g_nmix = g_nmix_x + g_nmix_c

    zero = jnp.zeros((D,), F32)
    dmod_x = jnp.stack([dsh_a[0], dsc_a[0], dg_a[0], dsh_f[0], dsc_f[0], dg_f[0]])
    dmod_c = jnp.stack([dsh_c[0], dsc_c[0], zero, zero, zero, zero])
    small = dict(
        dmod_x=dmod_x, dmod_c=dmod_c, norm_mix=g_nmix, norm_ffn=g_nffn, dn_conv=g_dnc8[:5], dn_a_log=g_alog[0, 16:32].reshape(2, 8),
        dn_dt_bias=g_dt[0, 16:32].reshape(2, 8), dn_norm=g_dnn, q_norm=g_qn, k_norm=g_kn,
        attn_sink=jnp.sum(dsink[:, :GRP, :], axis=2).reshape(1, NH), ffn_conv=g_ffc8[:3], ffn_conv_b=g_ffb)
    return loss_row[0, 0], grad_x, big, small


def _exchange(arrays, scatter, name):
    n = len(arrays)

    def body(*refs):
        args = (refs[:n], refs[n:2 * n], *refs[2 * n:], scatter)
        _exchange_start(*args)
        _exchange_wait(*args)

    hbm = pl.BlockSpec(memory_space=pl.ANY)
    out_shape, sems = _exchange_shapes(arrays, scatter)
    return pl.pallas_call(body, name=name, in_specs=[hbm] * n, out_specs=[hbm] * n, out_shape=out_shape,
                          scratch_shapes=sems)(*arrays)


def _gather_two_level(arrays, name):
    n = len(arrays)

    def body(*refs):
        ins, outs = refs[:n], refs[n:2 * n]
        send_sems, recv_sems, local_sems = refs[2 * n:]
        x, y, c = lax.axis_index("x"), lax.axis_index("y"), lax.axis_index("c")
        sibling = (x, y, 1 - c)
        chips = [(1 - x, y), (x, 1 - y), (1 - x, 1 - y)]

        def copy(k, j, block, to, src=None):
            slot = outs[k].at[4 * block[0] + 2 * block[1] + block[2]]
            return pltpu.make_async_remote_copy(src_ref=slot if src is None else src, dst_ref=slot,
                                                send_sem=send_sems.at[7 * k + j], recv_sem=recv_sems.at[7 * k + j],
                                                device_id=to, device_id_type=MESH)

        mine = [pltpu.make_async_copy(ins[k], outs[k].at[4 * x + 2 * y + c], local_sems.at[k]) for k in range(n)]
        for cp in mine:
            cp.start()
        first = []
        for k in range(n):
            first.append(copy(k, 0, (x, y, c), sibling, src=ins[k]))
            first += [copy(k, 1 + j, (x, y, c), (*chip, c), src=ins[k]) for j, chip in enumerate(chips)]
        for cp in first:
            cp.start()
        passed = []
        for k in range(n):
            for j, chip in enumerate(chips):
                copy(k, 1 + j, (*chip, c), (x, y, c)).wait_recv()
                forward = copy(k, 4 + j, (*chip, c), sibling)
                forward.start()
                passed.append(forward)
        for k in range(n):
            copy(k, 0, sibling, (x, y, c)).wait_recv()
            for j, chip in enumerate(chips):
                copy(k, 4 + j, (*chip, 1 - c), (x, y, c)).wait_recv()
        for cp in first + passed:
            cp.wait_send()
        for cp in mine:
            cp.wait()

    hbm = pl.BlockSpec(memory_space=pl.ANY)
    out_shape, sems = _exchange_shapes(arrays, False)
    return pl.pallas_call(body, name=name, in_specs=[hbm] * n, out_specs=[hbm] * n, out_shape=out_shape,
                          scratch_shapes=sems)(*arrays)


def _ada_fwd(c16, w_ada, b_ada):
    def body(c_ref, w_ref, b_ref, o_ref):
        o_ref[...] = _dot_hi(jax.nn.silu(c_ref[...]), w_ref[...]) + b_ref[...]

    return pl.pallas_call(body, name="ada_fwd", out_shape=jax.ShapeDtypeStruct((16, w_ada.shape[1]), F32))(c16, w_ada, b_ada)


def _ada_bwd(c16, w_ada, dmx, dmc):
    def body(c_ref, w_ref, dmx_ref, dmc_ref, gw_ref, pc_ref):
        dmc_tot = dmc_ref[0:1, :]
        for d in range(1, N_DEV):
            dmc_tot = dmc_tot + dmc_ref[d:d + 1, :]
        dm16 = jnp.concatenate([dmx_ref[...], jnp.broadcast_to(dmc_tot, (8, dmc_tot.shape[1]))], axis=0)
        row = lax.broadcasted_iota(jnp.int32, dm16.shape, 0)
        dm16 = jnp.where(row <= 8, dm16, 0.0)
        s = jax.nn.silu(c_ref[...])
        gw_ref[...] = lax.dot_general(s, dm16, (_DIMS["tn"], ((), ())), precision=HI, preferred_element_type=F32)
        pc = lax.dot_general(dm16, w_ref[...], (_DIMS["nt"], ((), ())), precision=HI, preferred_element_type=F32)
        pc_ref[...] = pc[8:9, :]

    return pl.pallas_call(body, name="ada_bwd", out_shape=[jax.ShapeDtypeStruct(w_ada.shape, F32), jax.ShapeDtypeStruct((1, D), F32)],
                          compiler_params=_cp())(c16, w_ada, dmx, dmc)


def _cctx_grad(pc_all, c_ctx_row):
    def body(pc_ref, c_ref, g_ref):
        tot = pc_ref[0]
        for d in range(1, N_DEV):
            tot = tot + pc_ref[d]
        _, vjp = jax.vjp(jax.nn.silu, c_ref[...])
        g_ref[...] = vjp(tot)[0]

    return pl.pallas_call(body, name="cctx_grad", out_shape=jax.ShapeDtypeStruct((1, D), F32))(pc_all, c_ctx_row)


def _adamw(parts, w, m, v, name):
    ns, R, C = parts.shape
    tb = _tile(R, (128, 64, 32, 16, 8))

    def body(p_ref, w_ref, m_ref, v_ref, g_ref, d_ref, mo_ref, vo_ref):
        g = p_ref[0].astype(F32)
        for s in range(1, ns):
            g = g + p_ref[s].astype(F32)
        m2 = ADAM_B1 * m_ref[...] + (1.0 - ADAM_B1) * g
        v2 = ADAM_B2 * v_ref[...] + (1.0 - ADAM_B2) * jnp.square(g)
        m_hat = m2 / (1.0 - ADAM_B1 ** ADAM_STEP)
        v_hat = v2 / (1.0 - ADAM_B2 ** ADAM_STEP)
        g_ref[...] = g
        d_ref[...] = -ADAM_LR * (m_hat / (jnp.sqrt(v_hat) + ADAM_EPS) + ADAM_WD * w_ref[...])
        mo_ref[...] = m2
        vo_ref[...] = v2

    row = pl.BlockSpec((tb, C), lambda i: (i, 0))
    return pl.pallas_call(
        body, grid=(R // tb,), name=name,
        in_specs=[pl.BlockSpec((ns, tb, C), lambda i: (0, i, 0)), row, row, row], out_specs=[row] * 4,
        out_shape=[jax.ShapeDtypeStruct((R, C), F32)] * 4, compiler_params=_cp(),
    )(parts, w, m, v)


_SMALL = (("dmod_x", 6 * D), ("dmod_c", 6 * D), ("b_ada", 6 * D), ("norm_mix", D), ("norm_ffn", D), ("dn_a_log", 16),
          ("dn_dt_bias", 16), ("dn_norm", HD), ("q_norm", HD), ("k_norm", HD), ("attn_sink", NH), ("ffn_conv_b", 2 * DFF),
          ("dn_conv", 5 * 3 * D), ("ffn_conv", 3 * 2 * DFF))
_SMALL_ROWS = -(-sum(n for _, n in _SMALL) // 1024) * 8


def _pack_small(d):
    flat = jnp.concatenate([d[k].reshape(-1).astype(F32) if k in d else jnp.zeros((n,), F32) for k, n in _SMALL])
    return jnp.concatenate([flat, jnp.zeros((_SMALL_ROWS * 128 - flat.shape[0],), F32)]).reshape(_SMALL_ROWS, 128)


def _unpack_small(a):
    flat = a.reshape(a.shape[:-2] + (-1,))
    out, off = {}, 0
    for k, n in _SMALL:
        out[k] = flat[..., off:off + n]
        off += n
    return out


def kernel(x, c, ctx, c_ctx, w_ada, b_ada, norm_mix, norm_ffn, w_in, dn_conv, dn_a_log, dn_dt_bias, dn_norm, q_norm, k_norm, attn_sink, w_branch_dn, w_branch_attn, w_out, ffn_up, ffn_conv, ffn_conv_b, ffn_down, loss_target, m_c_ctx, m_w_ada, m_b_ada, m_norm_mix, m_norm_ffn, m_w_in, m_dn_conv, m_dn_a_log, m_dn_dt_bias, m_dn_norm, m_q_norm, m_k_norm, m_attn_sink, m_w_branch_dn, m_w_branch_attn, m_w_out, m_ffn_up, m_ffn_conv, m_ffn_conv_b, m_ffn_down, v_c_ctx, v_w_ada, v_b_ada, v_norm_mix, v_norm_ffn, v_w_in, v_dn_conv, v_dn_a_log, v_dn_dt_bias, v_dn_norm, v_q_norm, v_k_norm, v_attn_sink, v_w_branch_dn, v_w_branch_attn, v_w_out, v_ffn_up, v_ffn_conv, v_ffn_conv_b, v_ffn_down):
    me = 4 * lax.axis_index("x") + 2 * lax.axis_index("y") + lax.axis_index("c")
    ada_cols = w_ada.shape[2]

    cols = lambda a: jnp.swapaxes(a, 0, 1).reshape(a.shape[1], -1)
    rows = lambda a: a.reshape(-1, a.shape[2])
    col_blocks = lambda g: jnp.swapaxes(g.reshape(g.shape[0], N_DEV, -1), 0, 1)
    row_blocks = lambda g: g.reshape(N_DEV, -1, g.shape[1])

    gathered = _gather_two_level([w_in[0].astype(BF), c, dn_conv[0], ffn_conv[0]], name="gather_first")
    w_in_packed = _pack_w_in(cols(gathered[0]))
    c_all = gathered[1][:, 0, :]

    def project_in(h1):
        p, rest = _mm(h1, w_in_packed, form="nn", out_dtype=F32, name="in_proj",
                      exchange=([w_branch_dn[0].astype(BF), w_branch_attn[0].astype(BF), w_out[0].astype(BF),
                                 ffn_up[0].astype(BF), ffn_down[0].astype(BF)], False))
        return p, (w_in_packed, rows(rest[0]), rows(rest[1]), rows(rest[2]), cols(rest[3]), rows(rest[4]))

    def project_back(h1, dp, w_in_p, grads):
        g_bdn, g_bat, g_out, g_up, g_down = grads
        g_in, landed_rest = _mm(h1, dp, form="tn", out_dtype=BF, name="g_w_in",
                                exchange=([row_blocks(g_bdn), row_blocks(g_bat), row_blocks(g_out), col_blocks(g_up),
                                           row_blocks(g_down)], True))
        dh1, landed_in = _mm(dp, w_in_p, form="nt", out_dtype=F32, name="d_h1",
                             exchange=([col_blocks(_unpack_w_in(g_in))], True))
        return [landed_in[0]] + landed_rest, dh1

    c16 = jnp.concatenate([c_all, c_ctx[None], jnp.zeros((7, D), F32)], axis=0)
    b_loc = lax.dynamic_slice_in_dim(b_ada, me * ada_cols, ada_cols, axis=1)
    mod_part = _ada_fwd(c16, w_ada[0], b_loc)
    mod_all = cols(_exchange([mod_part], scatter=False, name="gather_mod")[0])
    mod_x = lax.dynamic_slice_in_dim(mod_all, me, 1, axis=0).reshape(6, D)
    mod_c = mod_all[8].reshape(6, D)

    loss_loc, grad_x, landed, small = _local_step(
        x[0], ctx[0], mod_x, mod_c, loss_target[0], project_in, project_back,
        norm_mix, norm_ffn, cols(gathered[2]), dn_a_log[0], dn_dt_bias[0], dn_norm, q_norm, k_norm, attn_sink[0], cols(gathered[3]),
        ffn_conv_b)
    loss = lax.psum(loss_loc, ("x", "y", "c"))

    res = {}
    res["w_in"] = _adamw(landed[0], w_in[0], m_w_in[0], v_w_in[0], "adamw_w_in")
    res["w_branch_dn"] = _adamw(landed[1], w_branch_dn[0], m_w_branch_dn[0], v_w_branch_dn[0], "adamw_w_branch_dn")
    res["w_branch_attn"] = _adamw(landed[2], w_branch_attn[0], m_w_branch_attn[0], v_w_branch_attn[0], "adamw_w_branch_attn")
    res["w_out"] = _adamw(landed[3], w_out[0], m_w_out[0], v_w_out[0], "adamw_w_out")
    res["ffn_up"] = _adamw(landed[4], ffn_up[0], m_ffn_up[0], v_ffn_up[0], "adamw_ffn_up")
    res["ffn_down"] = _adamw(landed[5], ffn_down[0], m_ffn_down[0], v_ffn_down[0], "adamw_ffn_down")

    small = dict(small)
    small["b_ada"] = small["dmod_x"] + small["dmod_c"]
    parts = _exchange([_pack_small(small)], scatter=False, name="gather_small")[0]
    per_dev = _unpack_small(parts)
    given = dict(b_ada=(b_ada, m_b_ada, v_b_ada), norm_mix=(norm_mix, m_norm_mix, v_norm_mix), norm_ffn=(norm_ffn, m_norm_ffn, v_norm_ffn),
                 dn_a_log=(dn_a_log, m_dn_a_log, v_dn_a_log), dn_dt_bias=(dn_dt_bias, m_dn_dt_bias, v_dn_dt_bias),
                 dn_norm=(dn_norm, m_dn_norm, v_dn_norm), q_norm=(q_norm, m_q_norm, v_q_norm), k_norm=(k_norm, m_k_norm, v_k_norm),
                 attn_sink=(attn_sink, m_attn_sink, v_attn_sink), ffn_conv_b=(ffn_conv_b, m_ffn_conv_b, v_ffn_conv_b))
    packs = [_pack_small({k: t[j] for k, t in given.items()}) for j in range(3)]
    upd = [_unpack_small(a) for a in _adamw(parts, packs[0], packs[1], packs[2], "adamw_small")]
    for k, t in given.items():
        res[k] = tuple(u[k].reshape(t[0].shape) for u in upd)
    dnc = lax.dynamic_slice_in_dim(upd[0]["dn_conv"].reshape(5, 3 * D), me * dn_conv.shape[2], dn_conv.shape[2], axis=1)
    ffc = lax.dynamic_slice_in_dim(upd[0]["ffn_conv"].reshape(3, 2 * DFF), me * ffn_conv.shape[2], ffn_conv.shape[2], axis=1)
    r8 = lambda a: _pad_rows8(a)
    t = _adamw(r8(dnc)[None], r8(dn_conv[0]), r8(m_dn_conv[0]), r8(v_dn_conv[0]), "adamw_dn_conv")
    res["dn_conv"] = tuple(a[:5][None] for a in t)
    t = _adamw(r8(ffc)[None], r8(ffn_conv[0]), r8(m_ffn_conv[0]), r8(v_ffn_conv[0]), "adamw_ffn_conv")
    res["ffn_conv"] = tuple(a[:3][None] for a in t)

    dmx = lax.dynamic_slice_in_dim(per_dev["dmod_x"], me * ada_cols, ada_cols, axis=1)
    dmc = lax.dynamic_slice_in_dim(per_dev["dmod_c"], me * ada_cols, ada_cols, axis=1)
    g_ada, pc = _ada_bwd(c16, w_ada[0], dmx, dmc)
    res["w_ada"] = _adamw(g_ada[None], w_ada[0], m_w_ada[0], v_w_ada[0], "adamw_w_ada")
    pc_all = _exchange([pc], scatter=False, name="gather_cctx")[0]
    g_cctx = _cctx_grad(pc_all, c_ctx[None])
    r8b = lambda a: jnp.broadcast_to(a, (8, D))
    t = _adamw(r8b(g_cctx)[None], r8b(c_ctx[None]), r8b(m_c_ctx[None]), r8b(v_c_ctx[None]), "adamw_c_ctx")
    res["c_ctx"] = tuple(a[0] for a in t)

    names = ("c_ctx", "w_ada", "b_ada", "norm_mix", "norm_ffn", "w_in", "dn_conv", "dn_a_log", "dn_dt_bias", "dn_norm", "q_norm",
             "k_norm", "attn_sink", "w_branch_dn", "w_branch_attn", "w_out", "ffn_up", "ffn_conv", "ffn_conv_b", "ffn_down")
    lead = ("w_ada", "w_in", "w_branch_dn", "w_branch_attn", "w_out", "ffn_up", "ffn_down")
    fix = lambda k, a: a[None] if k in lead else a
    outs = [loss, grad_x[None]]
    for j in range(4):
        outs += [fix(k, res[k][j]) for k in names]
    return tuple(outs)
```

```python
import functools

import jax
import jax.numpy as jnp
import numpy as np
from jax import lax
from jax.experimental import pallas as pl
from jax.experimental.pallas import tpu as pltpu

F32 = jnp.float32
BF = jnp.bfloat16
HI = lax.Precision.HIGHEST
MESH = pl.DeviceIdType.MESH

D = 1024
NH = 8
HD = 128
KVH = 2
GRP = 4
KV = KVH * HD
DFF = 2816
CB = 128
GRID_W = 64
ROPE_BASE = 10000.0
EPS = 1e-6
N_DEV = 8
PW = 8192
O_QKV, O_GT, O_Q, O_MG, O_K, O_V, O_BA = 0, 3072, 4096, 5120, 7168, 7424, 7680
IN_SIZES = (3072, 1024, 16, 16, 1024, 256, 256, 2048)
IN_DIM = sum(IN_SIZES)
ADAM_LR, ADAM_B1, ADAM_B2, ADAM_EPS, ADAM_WD, ADAM_STEP = 0.001, 0.9, 0.999, 1e-08, 0.01, 10
VMEM_LIMIT = 56 * 1024 * 1024


def _cp():
    return pltpu.CompilerParams(vmem_limit_bytes=VMEM_LIMIT)


def _tile(n, cands):
    for c in cands:
        if n % c == 0:
            return c
    return n


def _iota2(shape):
    return lax.broadcasted_iota(jnp.int32, shape, 0), lax.broadcasted_iota(jnp.int32, shape, 1)


_DIMS = {"nn": ((1,), (0,)), "nt": ((1,), (1,)), "tn": ((0,), (0,))}


def _exchange_copies(ins, outs, send_sems, recv_sems, local_sems, scatter, landings):
    x, y, c = lax.axis_index("x"), lax.axis_index("y"), lax.axis_index("c")
    me = 4 * x + 2 * y + c
    local, remote = [], []
    for k in range(len(ins)):
        local.append(pltpu.make_async_copy(ins[k].at[me] if scatter else ins[k], outs[k].at[me], local_sems.at[k]))
        for m in range(1, N_DEV):
            px = 1 - x if m & 4 else x
            py = 1 - y if m & 2 else y
            pc = 1 - c if m & 1 else c
            peer = 4 * px + 2 * py + pc
            src = ins[k].at[peer] if scatter else ins[k]
            sem = k * (N_DEV - 1) + m - 1
            push = pltpu.make_async_remote_copy(src_ref=src, dst_ref=outs[k].at[me], send_sem=send_sems.at[sem],
                                                recv_sem=recv_sems.at[sem], device_id=(px, py, pc), device_id_type=MESH)
            landing = None
            if landings:
                landing = pltpu.make_async_remote_copy(src_ref=src, dst_ref=outs[k].at[peer], send_sem=send_sems.at[sem],
                                                       recv_sem=recv_sems.at[sem], device_id=(px, py, pc), device_id_type=MESH)
            remote.append((push, landing))
    return local, remote


def _exchange_start(*args):
    local, remote = _exchange_copies(*args, landings=False)
    for cp in local:
        cp.start()
    for push, _ in remote:
        push.start()


def _exchange_wait(*args):
    local, remote = _exchange_copies(*args, landings=True)
    for _, landing in remote:
        landing.wait_recv()
    for push, _ in remote:
        push.wait_send()
    for cp in local:
        cp.wait()


def _exchange_shapes(arrays, scatter):
    out_shape = [jax.ShapeDtypeStruct(a.shape if scatter else (N_DEV,) + a.shape, a.dtype) for a in arrays]
    n = len(arrays)
    sems = [pltpu.SemaphoreType.DMA((n * (N_DEV - 1),)), pltpu.SemaphoreType.DMA((n * (N_DEV - 1),)), pltpu.SemaphoreType.DMA((n,))]
    return out_shape, sems


def _mm(a, b, *, form, out_dtype, name, tm=None, tn=None, tk=None, exchange=None):
    if form == "tn":
        K, M = a.shape
        N = b.shape[1]
    else:
        M, K = a.shape
        N = b.shape[0] if form == "nt" else b.shape[1]
    tm = tm or _tile(M, (1408, 1280, 1024, 640, 512, 256, 128))
    tn = tn or _tile(N, (1408, 1024, 512, 256, 128))
    tk = tk or _tile(K, (2048, 1408, 1280, 1024, 640, 512, 256, 128))
    ni, nj, nk = M // tm, N // tn, K // tk
    dims = (_DIMS[form], ((), ()))
    ex_arrays, scatter = exchange if exchange else ([], False)
    nx = len(ex_arrays)

    def body(a_ref, b_ref, *refs):
        ex_in, o_ref, ex_out, scratch = refs[:nx], refs[nx], refs[nx + 1:2 * nx + 1], refs[2 * nx + 1:]
        i, j, k = pl.program_id(0), pl.program_id(1), pl.program_id(2)
        if nx:
            sems = scratch[-3:]

            @pl.when((i == 0) & (j == 0) & (k == 0))
            def _():
                _exchange_start(ex_in, ex_out, *sems, scatter)

        part = lax.dot_general(a_ref[...].astype(BF), b_ref[...].astype(BF), dims, preferred_element_type=F32)
        if nk == 1:
            o_ref[...] = part.astype(out_dtype)
        else:
            acc_ref = scratch[0]

            @pl.when(k == 0)
            def _():
                acc_ref[...] = part

            @pl.when(k > 0)
            def _():
                acc_ref[...] += part

            @pl.when(k == nk - 1)
            def _():
                o_ref[...] = acc_ref[...].astype(out_dtype)

        if nx:
            @pl.when((i == ni - 1) & (j == nj - 1) & (k == nk - 1))
            def _():
                _exchange_wait(ex_in, ex_out, *sems, scatter)

    if form == "tn":
        a_spec = pl.BlockSpec((tk, tm), lambda i, j, k: (k, i))
    else:
        a_spec = pl.BlockSpec((tm, tk), lambda i, j, k: (i, k))
    if form == "nt":
        b_spec = pl.BlockSpec((tn, tk), lambda i, j, k: (j, k))
    else:
        b_spec = pl.BlockSpec((tk, tn), lambda i, j, k: (k, j))
    hbm = pl.BlockSpec(memory_space=pl.ANY)
    ex_shapes, ex_sems = _exchange_shapes(ex_arrays, scatter) if nx else ([], [])
    outs = pl.pallas_call(
        body, grid=(ni, nj, nk), name=name,
        in_specs=[a_spec, b_spec] + [hbm] * nx, out_specs=[pl.BlockSpec((tm, tn), lambda i, j, k: (i, j))] + [hbm] * nx,
        out_shape=[jax.ShapeDtypeStruct((M, N), out_dtype)] + ex_shapes,
        scratch_shapes=([] if nk == 1 else [pltpu.VMEM((tm, tn), F32)]) + ex_sems,
        compiler_params=_cp(),
    )(a, b, *ex_arrays)
    return (outs[0], list(outs[1:])) if nx else outs[0]


def _norm_mod_fn(x, nw, sh, sc):
    y = x * lax.rsqrt(jnp.mean(x * x, axis=-1, keepdims=True) + EPS)
    return (y * nw) * (1.0 + sc) + sh


def _norm_mod_fwd(x, ctx, nw, sh, sc, name):
    nlat = x.shape[0]
    T = nlat + ctx.shape[0]
    tb = _tile(ctx.shape[0], (256, 128))
    nlb = nlat // tb

    def body(x_ref, c_ref, nw_ref, sh_ref, sc_ref, h_ref):
        rows = jnp.where(pl.program_id(0) < nlb, x_ref[...], c_ref[...])
        h_ref[...] = _norm_mod_fn(rows, nw_ref[...], sh_ref[0], sc_ref[0]).astype(BF)

    seg = pl.BlockSpec((1, 1, D), lambda i: (jnp.where(i >= nlb, 1, 0), 0, 0))
    return pl.pallas_call(
        body, grid=(T // tb,), name=name,
        in_specs=[pl.BlockSpec((tb, D), lambda i: (jnp.minimum(i, nlb - 1), 0)),
                  pl.BlockSpec((tb, D), lambda i: (jnp.maximum(i - nlb, 0), 0)), pl.BlockSpec((1, D), lambda i: (0, 0)), seg, seg],
        out_specs=pl.BlockSpec((tb, D), lambda i: (i, 0)),
        out_shape=jax.ShapeDtypeStruct((T, D), BF),
    )(x, ctx, nw, sh, sc)


def _norm_mod_bwd(x, nw, sh, sc, dh, *, row0, name, residual=None):
    nrows = x.shape[0]
    tb = _tile(nrows, (512, 256, 128))
    b0 = row0 // tb

    def body(x_ref, nw_ref, sh_ref, sc_ref, dh_ref, *refs):
        dnw_ref, dsh_ref, dsc_ref = refs[-3:]
        _, vjp = jax.vjp(_norm_mod_fn, x_ref[...], nw_ref[...], sh_ref[...], sc_ref[...])
        dx, dnw, dsh, dsc = vjp(dh_ref[...])
        if residual is not None:
            refs[1][...] = dx + refs[0][...]

        @pl.when(pl.program_id(0) == 0)
        def _():
            dnw_ref[...] = jnp.zeros_like(dnw_ref)
            dsh_ref[...] = jnp.zeros_like(dsh_ref)
            dsc_ref[...] = jnp.zeros_like(dsc_ref)

        dnw_ref[...] += dnw
        dsh_ref[...] += dsh
        dsc_ref[...] += dsc

    dh_row = pl.BlockSpec((tb, D), lambda i: (b0 + i, 0))
    out_row = pl.BlockSpec((tb, D), lambda i: (i, 0))
    one = pl.BlockSpec((1, D), lambda i: (0, 0))
    with_dx = residual is not None
    return pl.pallas_call(
        body, grid=(nrows // tb,), name=name,
        in_specs=[out_row, one, one, one, dh_row] + [out_row] * with_dx, out_specs=[out_row] * with_dx + [one] * 3,
        out_shape=[jax.ShapeDtypeStruct((nrows, D), F32)] * with_dx + [jax.ShapeDtypeStruct((1, D), F32)] * 3,
        compiler_params=_cp(),
    )(x, nw, sh, sc, dh, *([residual] if with_dx else []))


def _resid_norm_fwd(x, gate, y, nw, sh, sc):
    n = y.shape[0]
    tb = _tile(n, (512, 256, 128))

    def body(x_ref, g_ref, y_ref, nw_ref, sh_ref, sc_ref, x1_ref, h_ref):
        x1 = x_ref[...] + g_ref[...] * y_ref[...]
        x1_ref[...] = x1
        h_ref[...] = _norm_mod_fn(x1, nw_ref[...], sh_ref[...], sc_ref[...]).astype(BF)

    row = pl.BlockSpec((tb, D), lambda i: (i, 0))
    one = pl.BlockSpec((1, D), lambda i: (0, 0))
    return pl.pallas_call(
        body, grid=(n // tb,), name="resid_norm_fwd",
        in_specs=[row, one, row, one, one, one], out_specs=[row, row],
        out_shape=[jax.ShapeDtypeStruct((n, D), F32), jax.ShapeDtypeStruct((n, D), BF)],
        compiler_params=_cp(),
    )(x, gate, y, nw, sh, sc)


def _resid_norm_bwd(x1, gate, y, nw, sh, sc, dh, dx1_direct, w_out):
    n = y.shape[0]
    tb = _tile(n, (512, 256, 128))

    def body(x1_ref, g_ref, y_ref, nw_ref, sh_ref, sc_ref, dh_ref, dd_ref, wo_ref,
             dx_ref, dy_ref, dg_ref, dnw_ref, dsh_ref, dsc_ref, dm_ref):
        _, vjp = jax.vjp(_norm_mod_fn, x1_ref[...], nw_ref[...], sh_ref[...], sc_ref[...])
        dxn, dnw, dsh, dsc = vjp(dh_ref[...])
        dx = dxn + dd_ref[...]
        dx_ref[...] = dx
        dy_ref[...] = (g_ref[...] * dx).astype(BF)
        dm_ref[...] = lax.dot_general(dy_ref[...], wo_ref[...], (_DIMS["nt"], ((), ())), preferred_element_type=F32)

        @pl.when(pl.program_id(0) == 0)
        def _():
            for r in (dg_ref, dnw_ref, dsh_ref, dsc_ref):
                r[...] = jnp.zeros_like(r)

        dg_ref[...] += jnp.sum(dx * y_ref[...], axis=0, keepdims=True)
        dnw_ref[...] += dnw
        dsh_ref[...] += dsh
        dsc_ref[...] += dsc

    row = pl.BlockSpec((tb, D), lambda i: (i, 0))
    one = pl.BlockSpec((1, D), lambda i: (0, 0))
    return pl.pallas_call(
        body, grid=(n // tb,), name="resid_norm_bwd",
        in_specs=[row, one, row, one, one, one, row, row, _resident((D, D))], out_specs=[row, row] + [one] * 4 + [row],
        out_shape=[jax.ShapeDtypeStruct((n, D), F32), jax.ShapeDtypeStruct((n, D), BF)] + [jax.ShapeDtypeStruct((1, D), F32)] * 4
        + [jax.ShapeDtypeStruct((n, D), F32)],
        compiler_params=_cp(),
    )(x1, gate, y, nw, sh, sc, dh, dx1_direct, w_out)


HALO = 8


def _halo_specs(tb, width, nrows, col=0, halo=HALO):
    r8 = tb // halo
    cur = pl.BlockSpec((tb, width), lambda i: (i, col))
    prev = pl.BlockSpec((halo, width), lambda i: (jnp.maximum(i * r8 - 1, 0), col))
    nxt = pl.BlockSpec((halo, width), lambda i: (jnp.minimum((i + 1) * r8, nrows // halo - 1), col))
    return [cur, prev, nxt]


def _segment_edges(seg_rows, tb):
    bounds = [0]
    for s in seg_rows:
        bounds.append(bounds[-1] + s // tb)
    return bounds[:-1], [b - 1 for b in bounds[1:]]


def _keep_halos(i, starts, ends):
    keep_p = functools.reduce(lambda a, b: a & b, [i != s for s in starts])
    keep_n = functools.reduce(lambda a, b: a & b, [i != e for e in ends])
    return keep_p, keep_n


def _ext_rows(refs, cols, keep):
    cur_ref, prev_ref, next_ref = refs
    p = jnp.where(keep[0], prev_ref[:, cols].astype(F32), 0.0)
    n = jnp.where(keep[1], next_ref[:, cols].astype(F32), 0.0)
    return jnp.concatenate([p, cur_ref[:, cols].astype(F32), n], axis=0)


def _shifted_rows(xe, width, transpose=False):
    r = width // 2
    n = xe.shape[0]
    out = []
    for j in range(width):
        s = ((j - r) if transpose else (r - j)) % n
        out.append(xe if s == 0 else pltpu.roll(xe, s, 0))
    return out


def _conv_rows(shifted, w_ref, cols):
    acc = None
    for j, xs in enumerate(shifted):
        term = xs * w_ref[j:j + 1, cols]
        acc = term if acc is None else acc + term
    return acc


def _tap_grads(dcur, shifted, tb):
    return [jnp.sum(dcur * xs[HALO:HALO + tb], axis=0, keepdims=True) for xs in shifted]


def _softplus(x):
    return jnp.maximum(x, 0.0) + jnp.log(1.0 + jnp.exp(-jnp.abs(x)))


def _gates_fn(ba, alog_row, dt_row):
    col = lax.broadcasted_iota(jnp.int32, ba.shape, 1)
    beta = jax.nn.sigmoid(ba)
    g = -jnp.exp(alog_row) * _softplus(ba + dt_row)
    return jnp.where(col < 16, beta, jnp.where(col < 32, g, 0.0))


def _qkv_post_fn(c, kind):
    y = jax.nn.silu(c)
    if kind == 2:
        return y
    n = y * lax.rsqrt(jnp.sum(y * y, axis=-1, keepdims=True) + EPS)
    return n * (HD ** -0.5) if kind == 0 else n


DN_TAPS = 5
FFN_TAPS = 3


def _dn_pre_fwd(p, w8, alog_row, dt_row, seg_rows):
    T = p.shape[0]
    tb = _tile(T, (256, 128))
    starts, ends = _segment_edges(seg_rows, tb)

    def body(cur_ref, prev_ref, next_ref, ba_ref, w_ref, al_ref, dt_ref, q_ref, k_ref, v_ref, gb_ref):
        keep = _keep_halos(pl.program_id(0), starts, ends)
        outs = (q_ref, k_ref, v_ref)
        for kind in range(3):
            for h in range(NH):
                cols = slice(kind * D + h * HD, kind * D + (h + 1) * HD)
                xe = _ext_rows((cur_ref, prev_ref, next_ref), cols, keep)
                conv = _conv_rows(_shifted_rows(xe, DN_TAPS), w_ref, cols)[HALO:HALO + tb]
                outs[kind][:, h * HD:(h + 1) * HD] = _qkv_post_fn(conv, kind)
        gb_ref[...] = _gates_fn(ba_ref[...], al_ref[...], dt_ref[...])

    row = pl.BlockSpec((tb, D), lambda i: (i, 0))
    one = pl.BlockSpec((1, 128), lambda i: (0, 0))
    return pl.pallas_call(
        body, grid=(T // tb,), name="dn_pre_fwd",
        in_specs=_halo_specs(tb, 3 * D, T) + [pl.BlockSpec((tb, 128), lambda i: (i, O_BA // 128)),
                                              pl.BlockSpec((8, 3 * D), lambda i: (0, 0)), one, one],
        out_specs=[row, row, row, pl.BlockSpec((tb, 128), lambda i: (i, 0))],
        out_shape=[jax.ShapeDtypeStruct((T, D), F32)] * 3 + [jax.ShapeDtypeStruct((T, 128), F32)],
        compiler_params=_cp(),
    )(p, p, p, p, w8, alog_row, dt_row)


def _dn_pre_bwd(p, w8, alog_row, dt_row, dq, dk, dv, dgb, others, seg_rows):
    T = p.shape[0]
    tb = _tile(T, (256, 128))
    starts, ends = _segment_edges(seg_rows, tb)
    other_cols = (O_GT, O_Q, O_MG, O_K, O_V)
    assert [o.shape[1] for o in others] == [O_Q - O_GT, O_MG - O_Q, O_K - O_MG, O_V - O_K, O_BA - O_V]

    def body(cur_ref, prev_ref, next_ref, ba_ref, w_ref, al_ref, dt_ref,
             dq_c, dq_p, dq_n, dk_c, dk_p, dk_n, dv_c, dv_p, dv_n, dgb_ref, gt_ref, q_ref, mg_ref, k_ref, v_ref,
             dx_ref, dw_ref, dal_ref, ddt_ref):
        i = pl.program_id(0)
        for c0, ref in zip(other_cols, (gt_ref, q_ref, mg_ref, k_ref, v_ref)):
            dx_ref[:, c0:c0 + ref.shape[1]] = ref[...]
        dx_ref[:, O_BA + 128:] = jnp.zeros((tb, PW - O_BA - 128), BF)
        keep = _keep_halos(i, starts, ends)

        @pl.when(i == 0)
        def _():
            dw_ref[...] = jnp.zeros_like(dw_ref)
            dal_ref[...] = jnp.zeros_like(dal_ref)
            ddt_ref[...] = jnp.zeros_like(ddt_ref)

        douts = ((dq_c, dq_p, dq_n), (dk_c, dk_p, dk_n), (dv_c, dv_p, dv_n))
        for kind in range(3):
            for h in range(NH):
                cols = slice(kind * D + h * HD, kind * D + (h + 1) * HD)
                xe = _ext_rows((cur_ref, prev_ref, next_ref), cols, keep)
                shifted = _shifted_rows(xe, DN_TAPS)
                conv = _conv_rows(shifted, w_ref, cols)
                dye = _ext_rows(douts[kind], slice(h * HD, (h + 1) * HD), keep)
                _, vjp = jax.vjp(functools.partial(_qkv_post_fn, kind=kind), conv)
                dce = vjp(dye)[0]
                dx_ref[:, cols] = _conv_rows(_shifted_rows(dce, DN_TAPS, transpose=True), w_ref, cols)[HALO:HALO + tb].astype(BF)
                for j, g in enumerate(_tap_grads(dce[HALO:HALO + tb], shifted, tb)):
                    dw_ref[j:j + 1, cols] += g
        _, vjp = jax.vjp(_gates_fn, ba_ref[...], al_ref[...], dt_ref[...])
        dba, dal, ddt = vjp(dgb_ref[...])
        dx_ref[:, O_BA:O_BA + 128] = dba.astype(BF)
        dal_ref[...] += dal
        ddt_ref[...] += ddt

    one = pl.BlockSpec((1, 128), lambda i: (0, 0))
    nar = pl.BlockSpec((tb, 128), lambda i: (i, 0))
    wspec = pl.BlockSpec((8, 3 * D), lambda i: (0, 0))
    return pl.pallas_call(
        body, grid=(T // tb,), name="dn_pre_bwd",
        in_specs=_halo_specs(tb, 3 * D, T) + [pl.BlockSpec((tb, 128), lambda i: (i, O_BA // 128)), wspec, one, one]
        + _halo_specs(tb, D, T) * 3 + [nar] + [pl.BlockSpec((tb, o.shape[1]), lambda i: (i, 0)) for o in others],
        out_specs=[pl.BlockSpec((tb, PW), lambda i: (i, 0)), wspec, one, one],
        out_shape=[jax.ShapeDtypeStruct((T, PW), BF), jax.ShapeDtypeStruct((8, 3 * D), F32),
                   jax.ShapeDtypeStruct((1, 128), F32), jax.ShapeDtypeStruct((1, 128), F32)],
        compiler_params=_cp(),
    )(p, p, p, p, w8, alog_row, dt_row, dq, dq, dq, dk, dk, dk, dv, dv, dv, dgb, *others)


def _dot_hi(a, b):
    return jnp.dot(a, b, precision=HI, preferred_element_type=F32)


def _bf_product(a, b, form):
    return lax.dot_general(a.astype(BF), b.astype(BF), (_DIMS[form], ((), ())), preferred_element_type=F32)


def _dot_tn_bf(a, b):
    return _bf_product(a, b, "tn")


@jax.custom_vjp
def _dot_bf(a, b):
    return _bf_product(a, b, "nn")


@jax.custom_vjp
def _dot_nt_bf(a, b):
    return _bf_product(a, b, "nt")


_dot_bf.defvjp(lambda a, b: (_bf_product(a, b, "nn"), (a, b)),
               lambda res, dc: (_bf_product(dc, res[1], "nt").astype(res[0].dtype), _bf_product(res[0], dc, "tn").astype(res[1].dtype)))
_dot_nt_bf.defvjp(lambda a, b: (_bf_product(a, b, "nt"), (a, b)),
                  lambda res, dc: (_bf_product(dc, res[1], "nn").astype(res[0].dtype), _bf_product(dc, res[0], "tn").astype(res[1].dtype)))


def _dot_h3(a, b):
    return jnp.dot(a, b, precision=lax.Precision.HIGH, preferred_element_type=F32)


def _dot_split(fine, coarse, form):
    hi = fine.astype(BF)
    lo = (fine - hi.astype(F32)).astype(BF)
    cb = coarse.astype(BF)
    if form == "tn":
        return lax.dot_general(jnp.concatenate([cb, cb], axis=0), jnp.concatenate([hi, lo], axis=0),
                               (_DIMS["tn"], ((), ())), preferred_element_type=F32)
    parts = jnp.concatenate([hi, lo], axis=1)
    if form == "nt":
        return lax.dot_general(parts, jnp.concatenate([cb, cb], axis=1), (_DIMS["nt"], ((), ())), preferred_element_type=F32)
    return jnp.dot(parts, jnp.concatenate([cb, cb], axis=0), preferred_element_type=F32)


@jax.custom_vjp
def _mm_split(a, b):
    return _dot_split(a, b, "nn")


_mm_split.defvjp(lambda a, b: (_dot_split(a, b, "nn"), (a, b)),
                 lambda res, dc: (_dot_split(dc, res[1], "nt"), _dot_split(dc, res[0], "tn")))


def _unit_tri_inverses(mats):
    r, c = _iota2((CB, CB))
    eye = (r == c).astype(F32)
    a8 = [jnp.where((r // 8) == (c // 8), a, 0.0) for a in mats]
    a2 = [_dot_split(x, x, "nn") for x in a8]
    a4 = [_dot_split(x, x, "nn") for x in a2]
    t = [_dot_split(eye - x, eye + y, "nn") for x, y in zip(a8, a2)]
    t = [_dot_split(x, eye + y, "nn") for x, y in zip(t, a4)]
    b = 8
    while b < CB:
        mask = ((r // (2 * b)) == (c // (2 * b))) & ((r // b) != (c // b))
        te = [_dot_split(x, jnp.where(mask, a, 0.0), "nn") for x, a in zip(t, mats)]
        t = [x - _dot_split(y, x, "nn") for x, y in zip(t, te)]
        b *= 2
    return t


@jax.custom_vjp
def _saved_inverse(a, t):
    return t


_saved_inverse.defvjp(lambda a, t: (t, t),
                      lambda t, dt: (-_dot_split(_dot_split(dt, t, "nt"), t, "tn"), jnp.zeros_like(t)))


def _dn1_decay(gc, reverse):
    r, c = _iota2((CB, CB))
    incl = (c >= r) if reverse else (c <= r)
    return jnp.where(incl, jnp.exp(jnp.where(incl, gc - gc.T, 0.0)), 0.0)


def _dn1_heads(qs, ks, vs, betas, gcs, ts_saved, reverse, kks=None, qks=None):
    r, c = _iota2((CB, CB))
    strict = (c > r) if reverse else (c < r)
    decays = [_dn1_decay(gc, reverse) for gc in gcs]
    kks = kks or [_dot_nt_bf(k, k) for k in ks]
    systems = [jnp.where(strict, b * kk * dc, 0.0) for b, kk, dc in zip(betas, kks, decays)]
    if ts_saved is None:
        ts = _unit_tri_inverses(systems)
    else:
        ts = [_saved_inverse(a, t) for a, t in zip(systems, ts_saved)]
    egs = [jnp.exp(gc) for gc in gcs]
    us = [_mm_split(t, v * b) for t, v, b in zip(ts, vs, betas)]
    ws = [_mm_split(t, k * (b * eg)) for t, k, b, eg in zip(ts, ks, betas, egs)]
    qks = qks or [_dot_nt_bf(q, k) for q, k in zip(qs, ks)]
    last = 0 if reverse else CB - 1
    glogs = [jnp.sum(jnp.where(r == last, gc, 0.0), axis=0, keepdims=True) for gc in gcs]
    outs = [(u, w, q * eg, k * jnp.exp(gl - gc), qk * dc, jnp.exp(gl))
            for u, w, q, k, eg, gl, gc, qk, dc in zip(us, ws, qs, ks, egs, glogs, gcs, qks, decays)]
    return outs, ts


def _cum_matrix(upper):
    r, c = _iota2((CB, CB))
    return ((c >= r) if upper else (c <= r)).astype(F32)


def _lane_bcast(x, col):
    return jnp.broadcast_to(x[:, col:col + 1], x.shape)


_HEAD_SLICES = [slice(h * HD, (h + 1) * HD) for h in range(NH)]


def _dn1_fwd(q, k, v, gb):
    T = q.shape[0]
    nb = T // CB

    def body(q_ref, k_ref, v_ref, gb_ref, *out_refs):
        gbv = gb_ref[...]
        qs = [q_ref[:, sl] for sl in _HEAD_SLICES]
        ks = [k_ref[:, sl] for sl in _HEAD_SLICES]
        vs = [v_ref[:, sl] for sl in _HEAD_SLICES]
        kks = [_dot_nt_bf(x, x) for x in ks]
        qks = [_dot_nt_bf(x, y) for x, y in zip(qs, ks)]
        for d in (0, 1):
            u_ref, w_ref, qg_ref, kd_ref, qkd_ref, gl_ref, t_ref = out_refs[7 * d:7 * d + 7]
            gcum = _dot_h3(_cum_matrix(d == 1), gbv)
            betas = [_lane_bcast(gbv, d * NH + h) for h in range(NH)]
            gcs = [_lane_bcast(gcum, 16 + d * NH + h) for h in range(NH)]
            outs, ts = _dn1_heads(qs, ks, vs, betas, gcs, None, d == 1, kks, qks)
            for h, sl in enumerate(_HEAD_SLICES):
                u, w, qg, kd, qkd, gl = outs[h]
                u_ref[:, sl] = u
                w_ref[:, sl] = w.astype(BF)
                qg_ref[:, sl] = qg.astype(BF)
                kd_ref[:, sl] = kd.astype(BF)
                qkd_ref[:, sl] = qkd.astype(BF)
                gl_ref[h] = gl
                t_ref[:, sl] = ts[h]

    tb = pl.BlockSpec((CB, D), lambda i: (i, 0))
    one_dir_specs = [tb, tb, tb, tb, tb, pl.BlockSpec((NH, 1, 128), lambda i: (i, 0, 0)), tb]
    one_dir_shapes = ([jax.ShapeDtypeStruct((T, D), F32)] + [jax.ShapeDtypeStruct((T, D), BF)] * 4
                      + [jax.ShapeDtypeStruct((nb * NH, 1, 128), F32), jax.ShapeDtypeStruct((T, D), F32)])
    outs = pl.pallas_call(
        body, grid=(nb,), name="dn1_fwd",
        in_specs=[tb, tb, tb, pl.BlockSpec((CB, 128), lambda i: (i, 0))],
        out_specs=one_dir_specs * 2, out_shape=one_dir_shapes * 2, compiler_params=_cp(),
    )(q, k, v, gb)
    return [tuple(outs[:7]), tuple(outs[7:])]


def _dn1_bwd(q, k, v, gb, tinvs, cots):
    T = q.shape[0]
    nb = T // CB

    def body(q_ref, k_ref, v_ref, gb_ref, *refs):
        dir_refs, (dq_ref, dk_ref, dv_ref, dgb_ref) = refs[:14], refs[14:]
        gbv = gb_ref[...]
        qs = [q_ref[:, sl] for sl in _HEAD_SLICES]
        ks = [k_ref[:, sl] for sl in _HEAD_SLICES]
        vs = [v_ref[:, sl] for sl in _HEAD_SLICES]
        lane = lax.broadcasted_iota(jnp.int32, (CB, 128), 1)
        dgb = jnp.zeros((CB, 128), F32)
        for d in (0, 1):
            t_ref, du_ref, dw_ref, dqg_ref, dkd_ref, dqkd_ref, dgl_ref = dir_refs[7 * d:7 * d + 7]
            gcum = _dot_h3(_cum_matrix(d == 1), gbv)
            betas = [_lane_bcast(gbv, d * NH + h) for h in range(NH)]
            gcs = [_lane_bcast(gcum, 16 + d * NH + h) for h in range(NH)]
            ts = [t_ref[:, sl] for sl in _HEAD_SLICES]
            f = lambda qs, ks, vs, betas, gcs: _dn1_heads(qs, ks, vs, betas, gcs, ts, d == 1)[0]
            _, vjp = jax.vjp(f, qs, ks, vs, betas, gcs)
            cot = [(du_ref[:, sl], dw_ref[:, sl].astype(F32), dqg_ref[:, sl].astype(F32), dkd_ref[:, sl].astype(F32),
                    dqkd_ref[:, sl].astype(F32), dgl_ref[h])
                   for h, sl in enumerate(_HEAD_SLICES)]
            dqs, dks, dvs, dbetas, dgcs = vjp(cot)
            dgcum = jnp.zeros((CB, 128), F32)
            for h, sl in enumerate(_HEAD_SLICES):
                if d == 0:
                    dq_ref[:, sl] = dqs[h]
                    dk_ref[:, sl] = dks[h]
                    dv_ref[:, sl] = dvs[h]
                else:
                    dq_ref[:, sl] += dqs[h]
                    dk_ref[:, sl] += dks[h]
                    dv_ref[:, sl] += dvs[h]
                dgb = dgb + jnp.where(lane == d * NH + h, jnp.sum(dbetas[h], axis=1, keepdims=True), 0.0)
                dgcum = dgcum + jnp.where(lane == 16 + d * NH + h, jnp.sum(dgcs[h], axis=1, keepdims=True), 0.0)
            dgb = dgb + _dot_h3(_cum_matrix(d == 0), dgcum)
        dgb_ref[...] = dgb

    tb = pl.BlockSpec((CB, D), lambda i: (i, 0))
    gbs = pl.BlockSpec((CB, 128), lambda i: (i, 0))
    gls = pl.BlockSpec((NH, 1, 128), lambda i: (i, 0, 0))
    args = []
    for d in (0, 1):
        args += [tinvs[d], *cots[d]]
    return pl.pallas_call(
        body, grid=(nb,), name="dn1_bwd",
        in_specs=[tb, tb, tb, gbs] + [tb, tb, tb, tb, tb, tb, gls] * 2, out_specs=[tb, tb, tb, gbs],
        out_shape=[jax.ShapeDtypeStruct((T, D), F32)] * 3 + [jax.ShapeDtypeStruct((T, 128), F32)],
        compiler_params=_cp(),
    )(q, k, v, gb, *args)


SCAN_CHUNKS = 2


def _dn2_steps(chains):
    ws = [_dot_bf(w, s) for _, w, _, _, _, _, s in chains]
    v_new = [c[0] - x for c, x in zip(chains, ws)]
    o_state = [_dot_bf(c[2], c[6]) for c in chains]
    o_local = [_dot_bf(c[4], vn) for c, vn in zip(chains, v_new)]
    grow = [_dot_tn_bf(c[3], vn) for c, vn in zip(chains, v_new)]
    return [a + b for a, b in zip(o_state, o_local)], [c[6] * c[5] + g for c, g in zip(chains, grow)]


def _dn2_steps_bwd(chains, cot_o, cot_s):
    bf = lambda a: a.astype(BF)
    nt = lambda a, b: lax.dot_general(bf(a), bf(b), (_DIMS["nt"], ((), ())), preferred_element_type=F32)
    v_new = [c[0] - _dot_bf(c[1], c[6]) for c in chains]
    dv = [_dot_bf(jnp.concatenate([c[4].T, c[3]], axis=1), jnp.concatenate([do, ds], axis=0))
          for c, do, ds in zip(chains, cot_o, cot_s)]
    both = [nt(jnp.concatenate([do, x], axis=0), c[6]) for c, do, x in zip(chains, cot_o, dv)]
    dqkd = [nt(do, vn) for do, vn in zip(cot_o, v_new)]
    dkd = [nt(vn, ds) for vn, ds in zip(v_new, cot_s)]
    dstate = [_dot_bf(jnp.concatenate([c[2].T, -c[1].T], axis=1), jnp.concatenate([do, x], axis=0))
              for c, do, x in zip(chains, cot_o, dv)]
    return [(x, -b[CB:], b[:CB], dk, dq, jnp.sum(ds * c[6], axis=0, keepdims=True), ds * c[5] + g)
            for c, x, b, dk, dq, ds, g in zip(chains, dv, both, dkd, dqkd, cot_s, dstate)]


def _scan_order(direction, nlat_b, nall_b):
    if direction == 0:
        return lambda i: (i + nlat_b) % nall_b
    return lambda i: nall_b - 1 - i


def _dn2_fwd(per_dir, nlat):
    T = per_dir[0][0].shape[0]
    nb = T // CB
    assert (nlat // CB) % SCAN_CHUNKS == 0 and nb % SCAN_CHUNKS == 0
    blks = [_scan_order(d, nlat // CB // SCAN_CHUNKS, nb // SCAN_CHUNKS) for d in (0, 1)]

    def body(*refs):
        ins, outs, s_scr = refs[:12], refs[12:16], refs[16]

        @pl.when(pl.program_id(0) == 0)
        def _():
            s_scr[...] = jnp.zeros_like(s_scr)
        where = [(d, h, sl) for h, sl in enumerate(_HEAD_SLICES) for d in (0, 1)]
        for step in range(SCAN_CHUNKS):
            sub = [step, SCAN_CHUNKS - 1 - step]
            rows = [pl.ds(s * CB, CB) for s in sub]
            for d in (0, 1):
                outs[2 * d + 1][sub[d]] = s_scr[d]
            chains = []
            for d, h, sl in where:
                u_ref, w_ref, qg_ref, kd_ref, qkd_ref, gl_ref = ins[6 * d:6 * d + 6]
                chains.append((u_ref[rows[d], sl], w_ref[rows[d], sl], qg_ref[rows[d], sl], kd_ref[rows[d], sl], qkd_ref[rows[d], sl],
                               gl_ref[sub[d] * NH + h], s_scr[d, h]))
            os, states = _dn2_steps(chains)
            for (d, h, sl), o, s_next in zip(where, os, states):
                outs[2 * d][rows[d], sl] = o
                s_scr[d, h] = s_next

    in_specs, out_specs, args = [], [], []
    for d in (0, 1):
        blk = blks[d]
        tb = pl.BlockSpec((SCAN_CHUNKS * CB, D), lambda i, blk=blk: (blk(i), 0))
        in_specs += [tb] * 5 + [pl.BlockSpec((SCAN_CHUNKS * NH, 1, 128), lambda i, blk=blk: (blk(i), 0, 0))]
        out_specs += [tb, pl.BlockSpec((SCAN_CHUNKS, NH, HD, HD), lambda i, blk=blk: (blk(i), 0, 0, 0))]
        args += list(per_dir[d])
    outs = pl.pallas_call(
        body, grid=(nb // SCAN_CHUNKS,), name="dn2_fwd", in_specs=in_specs, out_specs=out_specs,
        out_shape=[jax.ShapeDtypeStruct((T, D), F32), jax.ShapeDtypeStruct((nb, NH, HD, HD), F32)] * 2,
        scratch_shapes=[pltpu.VMEM((2, NH, HD, HD), F32)], compiler_params=_cp(),
    )(*args)
    return [tuple(outs[:2]), tuple(outs[2:])]


def _dn2_bwd(per_dir, do, nlat):
    T = per_dir[0][0].shape[0]
    nb = T // CB
    nlat_b = nlat // CB
    fwd = [_scan_order(d, nlat_b, nb) for d in (0, 1)]
    blks = [lambda i, f=f: f(nb - 1 - i) for f in fwd]

    def body(*refs):
        ins, outs, ds_scr = refs[:16], refs[16:28], refs[28]
        i = pl.program_id(0)

        @pl.when(i == 0)
        def _():
            ds_scr[...] = jnp.zeros_like(ds_scr)
        where = [(d, h, sl) for h, sl in enumerate(_HEAD_SLICES) for d in (0, 1)]
        chains, cot_o, cot_s = [], [], []
        for d, h, sl in where:
            u_ref, w_ref, qg_ref, kd_ref, qkd_ref, gl_ref, sall_ref, do_ref = ins[8 * d:8 * d + 8]
            chains.append((u_ref[:, sl], w_ref[:, sl].astype(F32), qg_ref[:, sl].astype(F32), kd_ref[:, sl].astype(F32),
                           qkd_ref[:, sl].astype(F32), gl_ref[h], sall_ref[0, h]))
            cot_o.append(jnp.where(blks[d](i) < nlat_b, do_ref[:, sl], 0.0))
            cot_s.append(ds_scr[d, h])
        for (d, h, sl), (du, dw, dqg, dkd, dqkd, dgl, ds) in zip(where, _dn2_steps_bwd(chains, cot_o, cot_s)):
            du_ref, dw_ref, dqg_ref, dkd_ref, dqkd_ref, dgl_ref = outs[6 * d:6 * d + 6]
            du_ref[:, sl] = du
            dw_ref[:, sl] = dw.astype(BF)
            dqg_ref[:, sl] = dqg.astype(BF)
            dkd_ref[:, sl] = dkd.astype(BF)
            dqkd_ref[:, sl] = dqkd.astype(BF)
            dgl_ref[h] = dgl
            ds_scr[d, h] = ds

    in_specs, out_specs, args = [], [], []
    for d in (0, 1):
        blk = blks[d]
        tb = pl.BlockSpec((CB, D), lambda i, blk=blk: (blk(i), 0))
        gls = pl.BlockSpec((NH, 1, 128), lambda i, blk=blk: (blk(i), 0, 0))
        in_specs += [tb] * 5 + [gls, pl.BlockSpec((1, NH, HD, HD), lambda i, blk=blk: (blk(i), 0, 0, 0)),
                                pl.BlockSpec((CB, D), lambda i, blk=blk: (jnp.minimum(blk(i), nlat_b - 1), 0))]
        out_specs += [tb] * 5 + [gls]
        args += list(per_dir[d]) + [do]
    outs = pl.pallas_call(
        body, grid=(nb,), name="dn2_bwd", in_specs=in_specs, out_specs=out_specs,
        out_shape=([jax.ShapeDtypeStruct((T, D), F32)] + [jax.ShapeDtypeStruct((T, D), BF)] * 4
                   + [jax.ShapeDtypeStruct((nb * NH, 1, 128), F32)]) * 2,
        scratch_shapes=[pltpu.VMEM((2, NH, HD, HD), F32)], compiler_params=_cp(),
    )(*args)
    return [tuple(outs[:6]), tuple(outs[6:])]


def _ghn_fn(o, gt, w):
    y = o * lax.rsqrt(jnp.mean(o * o, axis=-1, keepdims=True) + EPS)
    return (y * w) * jax.nn.silu(gt)


def _ghn_fwd(o_f, o_b, p, w, w_branch, nlat):
    tb = _tile(nlat, (512, 256, 128))

    def body(of_ref, ob_ref, gt_ref, w_ref, wb_ref, y_ref, z_ref):
        for h in range(NH):
            sl = slice(h * HD, (h + 1) * HD)
            y_ref[:, sl] = _ghn_fn(of_ref[:, sl] + ob_ref[:, sl], gt_ref[:, sl], w_ref[...]).astype(BF)
        z_ref[...] = jnp.dot(y_ref[...], wb_ref[...], preferred_element_type=F32)

    row = pl.BlockSpec((tb, D), lambda i: (i, 0))
    return pl.pallas_call(
        body, grid=(nlat // tb,), name="ghn_fwd",
        in_specs=[row, row, pl.BlockSpec((tb, D), lambda i: (i, O_GT // D)), pl.BlockSpec((1, HD), lambda i: (0, 0)), _resident((D, D))],
        out_specs=[row, row], out_shape=[jax.ShapeDtypeStruct((nlat, D), BF), jax.ShapeDtypeStruct((nlat, D), F32)],
        compiler_params=_cp(),
    )(o_f, o_b, p, w, w_branch)


def _ghn_bwd(o_f, o_b, p, w, dy, nlat):
    T = p.shape[0]
    tb = _tile(nlat, (256, 128))
    nlb = nlat // tb

    def body(of_ref, ob_ref, gt_ref, w_ref, dy_ref, do_ref, dgt_ref, dw_ref):
        is_lat = pl.program_id(0) < nlb

        @pl.when(pl.program_id(0) == 0)
        def _():
            dw_ref[...] = jnp.zeros_like(dw_ref)
        for h in range(NH):
            sl = slice(h * HD, (h + 1) * HD)
            _, vjp = jax.vjp(_ghn_fn, of_ref[:, sl] + ob_ref[:, sl], gt_ref[:, sl], w_ref[...])
            do, dgt, dw = vjp(dy_ref[:, sl])
            do_ref[:, sl] = do
            dgt_ref[:, sl] = jnp.where(is_lat, dgt, 0.0).astype(BF)
            dw_ref[...] += jnp.where(is_lat, dw, 0.0)

    lat = lambda i: jnp.minimum(i, nlb - 1)
    row = pl.BlockSpec((tb, D), lambda i: (lat(i), 0))
    one = pl.BlockSpec((1, HD), lambda i: (0, 0))
    return pl.pallas_call(
        body, grid=(T // tb,), name="ghn_bwd",
        in_specs=[row, row, pl.BlockSpec((tb, D), lambda i: (lat(i), O_GT // D)), one, row],
        out_specs=[row, pl.BlockSpec((tb, D), lambda i: (i, 0)), one],
        out_shape=[jax.ShapeDtypeStruct((nlat, D), F32), jax.ShapeDtypeStruct((T, D), BF), jax.ShapeDtypeStruct((1, HD), F32)],
    )(o_f, o_b, p, w, dy)


@jax.custom_vjp
def _swap32(x):
    lane = lax.broadcasted_iota(jnp.int32, x.shape, 1)
    return jnp.where((lane & 32) == 0, pltpu.roll(x, 96, 1), pltpu.roll(x, 32, 1))


_swap32.defvjp(lambda x: (_swap32(x), None), lambda _, g: (_swap32(g),))


def _qk_post_fn(xs, w, cos, sin):
    inv = [lax.rsqrt(jnp.mean(x * x, axis=-1, keepdims=True) + EPS) for x in xs]
    ys = [(x * r) * w for x, r in zip(xs, inv)]
    return [y * cos + _swap32(y) * sin for y in ys]


def _attn_prep_fwd(p, qn, kn, cos, sin):
    T = p.shape[0]
    tb = _tile(T, (256, 128))

    def body(q_ref, k_ref, v_ref, qn_ref, kn_ref, cos_ref, sin_ref, qr_ref, kr_ref, vb_ref):
        cos_v, sin_v = cos_ref[...], sin_ref[...]
        for sl, y in zip(_HEAD_SLICES, _qk_post_fn([q_ref[:, sl] for sl in _HEAD_SLICES], qn_ref[...], cos_v, sin_v)):
            qr_ref[:, sl] = y.astype(BF)
        for sl, y in zip(_HEAD_SLICES, _qk_post_fn([k_ref[:, sl] for sl in _HEAD_SLICES[:KVH]], kn_ref[...], cos_v, sin_v)):
            kr_ref[:, sl] = y.astype(BF)
        vb_ref[...] = v_ref[...].astype(BF)

    one = pl.BlockSpec((1, HD), lambda i: (0, 0))
    tab = pl.BlockSpec((tb, HD), lambda i: (i, 0))
    return pl.pallas_call(
        body, grid=(T // tb,), name="attn_prep_fwd",
        in_specs=[pl.BlockSpec((tb, D), lambda i: (i, O_Q // D)), pl.BlockSpec((tb, KV), lambda i: (i, O_K // KV)),
                  pl.BlockSpec((tb, KV), lambda i: (i, O_V // KV)), one, one, tab, tab],
        out_specs=[pl.BlockSpec((tb, D), lambda i: (i, 0)), pl.BlockSpec((tb, KV), lambda i: (i, 0)),
                   pl.BlockSpec((tb, KV), lambda i: (i, 0))],
        out_shape=[jax.ShapeDtypeStruct((T, D), BF), jax.ShapeDtypeStruct((T, KV), BF), jax.ShapeDtypeStruct((T, KV), BF)],
    )(p, p, p, qn, kn, cos, sin)


def _attn_prep_bwd(p, qn, kn, cos, sin, dqr, dkp, dvp, dkc, dvc, nlat):
    T = p.shape[0]
    nqb = nlat // CB
    ncb = (T - nlat) // CB

    def body(q_ref, k_ref, v_ref, qn_ref, kn_ref, cos_ref, sin_ref, dqr_ref, dka_ref, dkb_ref, dkc3_ref, dva_ref, dvb_ref, dvc3_ref,
             dkctx_ref, dvctx_ref, dq_ref, dk_ref, dv_ref, dqn_ref, dkn_ref):
        i = pl.program_id(0)
        is_lat = i < nqb
        cos_v, sin_v = cos_ref[...], sin_ref[...]

        @pl.when(i == 0)
        def _():
            dqn_ref[...] = jnp.zeros_like(dqn_ref)
            dkn_ref[...] = jnp.zeros_like(dkn_ref)

        def band_sum(a_ref, b_ref, c_ref, ctx_ref):
            s = b_ref[0] + jnp.where(i > 0, a_ref[0], 0.0) + jnp.where(i < nqb - 1, c_ref[0], 0.0)
            return jnp.where(is_lat, s, ctx_ref[...])

        dkr = band_sum(dka_ref, dkb_ref, dkc3_ref, dkctx_ref)
        dv_ref[...] = band_sum(dva_ref, dvb_ref, dvc3_ref, dvctx_ref).astype(BF)
        post = lambda xs, w: _qk_post_fn(xs, w, cos_v, sin_v)
        _, vjp = jax.vjp(post, [q_ref[:, sl] for sl in _HEAD_SLICES], qn_ref[...])
        dqs, dqn = vjp([jnp.where(is_lat, dqr_ref[:, sl], 0.0) for sl in _HEAD_SLICES])
        for sl, dq in zip(_HEAD_SLICES, dqs):
            dq_ref[:, sl] = dq.astype(BF)
        dqn_ref[...] += dqn
        _, vjp = jax.vjp(post, [k_ref[:, sl] for sl in _HEAD_SLICES[:KVH]], kn_ref[...])
        dks, dkn = vjp([dkr[:, sl] for sl in _HEAD_SLICES[:KVH]])
        for sl, dk in zip(_HEAD_SLICES, dks):
            dk_ref[:, sl] = dk.astype(BF)
        dkn_ref[...] += dkn

    one = pl.BlockSpec((1, HD), lambda i: (0, 0))
    tab = pl.BlockSpec((CB, HD), lambda i: (i, 0))
    lat = lambda i: jnp.minimum(i, nqb - 1)

    def part(off, slot):
        return pl.BlockSpec((1, CB, KV), lambda i: (jnp.clip(lat(i) + off, 0, nqb - 1) * 3 + slot, 0, 0))

    ctxs = pl.BlockSpec((CB, KV), lambda i: (jnp.clip(i - nqb, 0, ncb - 1), 0))
    kvs = pl.BlockSpec((CB, KV), lambda i: (i, 0))
    return pl.pallas_call(
        body, grid=(T // CB,), name="attn_prep_bwd",
        in_specs=[pl.BlockSpec((CB, D), lambda i: (i, O_Q // D)), pl.BlockSpec((CB, KV), lambda i: (i, O_K // KV)),
                  pl.BlockSpec((CB, KV), lambda i: (i, O_V // KV)), one, one, tab, tab,
                  pl.BlockSpec((CB, D), lambda i: (lat(i), 0)),
                  part(-1, 2), part(0, 1), part(1, 0), part(-1, 2), part(0, 1), part(1, 0), ctxs, ctxs],
        out_specs=[pl.BlockSpec((CB, D), lambda i: (i, 0)), kvs, kvs, one, one],
        out_shape=[jax.ShapeDtypeStruct((T, D), BF), jax.ShapeDtypeStruct((T, KV), BF), jax.ShapeDtypeStruct((T, KV), BF),
                   jax.ShapeDtypeStruct((1, HD), F32), jax.ShapeDtypeStruct((1, HD), F32)],
    )(p, p, p, qn, kn, cos, sin, dqr, dkp, dkp, dkp, dvp, dvp, dvp, dkc, dvc)


def _attn_groups_fn(qs, kalls, valls, sinks, bias):
    groups = range(KVH)
    q = [jnp.concatenate(qs[GRP * g:GRP * (g + 1)], axis=0) for g in groups]
    s = [_bf_product(q[g], kalls[g], "nt") * (HD ** -0.5) + bias for g in groups]
    sk = [jnp.concatenate([jnp.broadcast_to(jnp.mean(t, axis=1, keepdims=True), (CB, 1)) for t in sinks[GRP * g:GRP * (g + 1)]],
                          axis=0) for g in groups]
    m = [lax.stop_gradient(jnp.maximum(jnp.max(s[g], axis=1, keepdims=True), sk[g])) for g in groups]
    e = [jnp.exp(s[g] - m[g]) for g in groups]
    den = [jnp.sum(e[g], axis=1, keepdims=True) + jnp.exp(sk[g] - m[g]) for g in groups]
    return [_bf_product(e[g] / den[g], valls[g], "nn") for g in groups]


def _attn_bias(lc):
    r, c = _iota2((GRP * CB, 3 * CB + lc))
    rel = c - (r & (CB - 1))
    win = (rel >= 0) & (rel <= 2 * CB)
    ctx = c >= 3 * CB
    seen = [(win & (c >= CB)) | ctx, win | ctx, (win & (c < 2 * CB)) | ctx]
    return jnp.stack([jnp.where(s, 0.0, -1e30) for s in seen]).astype(F32)


def _attn_specs(nqb, lc, nlat):
    assert nqb >= 2
    qs = pl.BlockSpec((CB, D), lambda i: (i, 0))
    ka = pl.BlockSpec((CB, KV), lambda i: (jnp.maximum(i - 1, 0), 0))
    kb = pl.BlockSpec((CB, KV), lambda i: (i, 0))
    kc = pl.BlockSpec((CB, KV), lambda i: (jnp.minimum(i + 1, nqb - 1), 0))
    kx = pl.BlockSpec((lc, KV), lambda i: (nlat // lc, 0))
    sk = pl.BlockSpec((KVH, 8, 128), lambda i: (0, 0, 0))
    bs = pl.BlockSpec((1, GRP * CB, 3 * CB + lc), lambda i: (jnp.where(i == 0, 0, jnp.where(i == nqb - 1, 2, 1)), 0, 0))
    return qs, ka, kb, kc, kx, sk, bs


def _attn_operands(q_ref, k_refs, v_refs, sk_ref, dtype):
    sls = [slice(g * HD, (g + 1) * HD) for g in range(KVH)]
    kalls = [jnp.concatenate([r[:, sl] for r in k_refs], axis=0).astype(dtype) for sl in sls]
    valls = [jnp.concatenate([r[:, sl] for r in v_refs], axis=0).astype(dtype) for sl in sls]
    qs = [q_ref[:, sl].astype(dtype) for sl in _HEAD_SLICES]
    sinks = [sk_ref[h // GRP, (h % GRP):(h % GRP) + 1, :] for h in range(NH)]
    return qs, kalls, valls, sinks


def _attn_fwd(qr, kr, vb, sink, w_branch, nlat):
    lc = kr.shape[0] - nlat
    nqb = nlat // CB
    qs, ka, kb, kc, kx, sk, bs = _attn_specs(nqb, lc, nlat)

    def body(q_ref, ka_ref, kb_ref, kc_ref, kx_ref, va_ref, vb_ref, vc_ref, vx_ref, sk_ref, bias_ref, wb_ref, o_ref, z_ref):
        operands = _attn_operands(q_ref, (ka_ref, kb_ref, kc_ref, kx_ref), (va_ref, vb_ref, vc_ref, vx_ref), sk_ref, BF)
        outs = _attn_groups_fn(*operands, bias_ref[0])
        for h, sl in enumerate(_HEAD_SLICES):
            o_ref[:, sl] = outs[h // GRP][(h % GRP) * CB:(h % GRP + 1) * CB].astype(BF)
        z_ref[...] = jnp.dot(o_ref[...], wb_ref[...], preferred_element_type=F32)

    return pl.pallas_call(
        body, grid=(nqb,), name="attn_fwd",
        in_specs=[qs, ka, kb, kc, kx, ka, kb, kc, kx, sk, bs, _resident((D, D))], out_specs=[qs, qs],
        out_shape=[jax.ShapeDtypeStruct((nlat, D), BF), jax.ShapeDtypeStruct((nlat, D), F32)], compiler_params=_cp(),
    )(qr, kr, kr, kr, kr, vb, vb, vb, vb, sink, _attn_bias(lc), w_branch)


def _attn_bwd(qr, kr, vb, sink, dy, nlat):
    lc = kr.shape[0] - nlat
    nqb = nlat // CB
    qs, ka, kb, kc, kx, sk, bs = _attn_specs(nqb, lc, nlat)

    def body(q_ref, ka_ref, kb_ref, kc_ref, kx_ref, va_ref, vb_ref, vc_ref, vx_ref, sk_ref, dy_ref, bias_ref,
             dq_ref, dkp_ref, dvp_ref, dkx_ref, dvx_ref, dsk_ref):
        operands = _attn_operands(q_ref, (ka_ref, kb_ref, kc_ref, kx_ref), (va_ref, vb_ref, vc_ref, vx_ref), sk_ref, F32)
        _, vjp = jax.vjp(functools.partial(_attn_groups_fn, bias=bias_ref[0]), *operands)
        dys_g = [jnp.concatenate([dy_ref[:, sl] for sl in _HEAD_SLICES[GRP * g:GRP * (g + 1)]], axis=0) for g in range(KVH)]
        dqs, dks, dvs, dsinks = vjp(dys_g)

        @pl.when(pl.program_id(0) == 0)
        def _():
            dkx_ref[...] = jnp.zeros_like(dkx_ref)
            dvx_ref[...] = jnp.zeros_like(dvx_ref)
            dsk_ref[...] = jnp.zeros_like(dsk_ref)

        for h, sl in enumerate(_HEAD_SLICES):
            dq_ref[:, sl] = dqs[h]
            dsk_ref[h // GRP, (h % GRP):(h % GRP) + 1, :] += dsinks[h]
        for g in range(KVH):
            sl = slice(g * HD, (g + 1) * HD)
            for t in range(3):
                dkp_ref[t, :, sl] = dks[g][t * CB:(t + 1) * CB]
                dvp_ref[t, :, sl] = dvs[g][t * CB:(t + 1) * CB]
            dkx_ref[:, sl] += dks[g][3 * CB:]
            dvx_ref[:, sl] += dvs[g][3 * CB:]

    dys = qs
    parts = pl.BlockSpec((3, CB, KV), lambda i: (i, 0, 0))
    ctxo = pl.BlockSpec((lc, KV), lambda i: (0, 0))
    return pl.pallas_call(
        body, grid=(nqb,), name="attn_bwd",
        in_specs=[qs, ka, kb, kc, kx, ka, kb, kc, kx, sk, dys, bs],
        out_specs=[dys, parts, parts, ctxo, ctxo, sk],
        out_shape=[jax.ShapeDtypeStruct((nlat, D), F32), jax.ShapeDtypeStruct((3 * nqb, CB, KV), F32),
                   jax.ShapeDtypeStruct((3 * nqb, CB, KV), F32), jax.ShapeDtypeStruct((lc, KV), F32),
                   jax.ShapeDtypeStruct((lc, KV), F32), jax.ShapeDtypeStruct((KVH, 8, 128), F32)],
        compiler_params=_cp(),
    )(qr, kr, kr, kr, kr, vb, vb, vb, vb, sink, dy, _attn_bias(lc))


def _merge_fn(z_dn, z_at, g_dn, g_at):
    return jax.nn.sigmoid(g_dn) * z_dn + jax.nn.sigmoid(g_at) * z_at


def _merge_fwd(z_dn, z_at, p, w_out, nlat):
    tb = _tile(nlat, (512, 256, 128))

    def body(zd_ref, za_ref, gd_ref, ga_ref, wo_ref, o_ref, mix_ref):
        o_ref[...] = _merge_fn(zd_ref[...], za_ref[...], gd_ref[...], ga_ref[...]).astype(BF)
        mix_ref[...] = jnp.dot(o_ref[...], wo_ref[...], preferred_element_type=F32)

    row = pl.BlockSpec((tb, D), lambda i: (i, 0))
    return pl.pallas_call(
        body, grid=(nlat // tb,), name="merge_fwd",
        in_specs=[row, row, pl.BlockSpec((tb, D), lambda i: (i, O_MG // D)), pl.BlockSpec((tb, D), lambda i: (i, O_MG // D + 1)),
                  _resident((D, D))],
        out_specs=[row, row], out_shape=[jax.ShapeDtypeStruct((nlat, D), BF), jax.ShapeDtypeStruct((nlat, D), F32)],
        compiler_params=_cp(),
    )(z_dn, z_at, p, p, w_out)


def _merge_bwd(z_dn, z_at, p, dm, w_bdn, w_bat, nlat):
    T = p.shape[0]
    tb = _tile(nlat, (256, 128))
    nlb = nlat // tb

    def body(zd_ref, za_ref, gd_ref, ga_ref, dm_ref, wd_ref, wa_ref, dzd_ref, dza_ref, dg_ref, dyd_ref, dya_ref):
        is_lat = pl.program_id(0) < nlb
        _, vjp = jax.vjp(_merge_fn, zd_ref[...], za_ref[...], gd_ref[...], ga_ref[...])
        dzd, dza, dgd, dga = vjp(dm_ref[...])
        dzd_ref[...] = dzd.astype(BF)
        dza_ref[...] = dza.astype(BF)
        dg_ref[:, :D] = jnp.where(is_lat, dgd, 0.0).astype(BF)
        dg_ref[:, D:] = jnp.where(is_lat, dga, 0.0).astype(BF)
        dyd_ref[...] = lax.dot_general(dzd_ref[...], wd_ref[...], (_DIMS["nt"], ((), ())), preferred_element_type=F32)
        dya_ref[...] = lax.dot_general(dza_ref[...], wa_ref[...], (_DIMS["nt"], ((), ())), preferred_element_type=F32)

    lat = lambda i: jnp.minimum(i, nlb - 1)
    row = pl.BlockSpec((tb, D), lambda i: (lat(i), 0))
    return pl.pallas_call(
        body, grid=(T // tb,), name="merge_bwd",
        in_specs=[row, row, pl.BlockSpec((tb, D), lambda i: (lat(i), O_MG // D)),
                  pl.BlockSpec((tb, D), lambda i: (lat(i), O_MG // D + 1)), row, _resident((D, D)), _resident((D, D))],
        out_specs=[row, row, pl.BlockSpec((tb, 2 * D), lambda i: (i, 0)), row, row],
        out_shape=[jax.ShapeDtypeStruct((nlat, D), BF), jax.ShapeDtypeStruct((nlat, D), BF), jax.ShapeDtypeStruct((T, 2 * D), BF),
                   jax.ShapeDtypeStruct((nlat, D), F32), jax.ShapeDtypeStruct((nlat, D), F32)],
    )(z_dn, z_at, p, p, dm, w_bdn, w_bat)


def _swiglu_fn(ug, uv):
    return jax.nn.silu(ug) * uv


FFN_GROUP = 256


def _resident(shape):
    return pl.BlockSpec(shape, lambda i: (0,) * len(shape), pipeline_mode=pl.Buffered(1))


H_HALO = 16


def _up_project(h_refs, wu_ref, u_scr):
    cur_ref, prev_ref, next_ref = h_refs
    rows = jnp.concatenate([prev_ref[...], cur_ref[...], next_ref[...]], axis=0)
    u_scr[...] = jnp.dot(rows, wu_ref[...], preferred_element_type=F32)


def _up_ext_rows(u_scr, cols, keep, tb):
    xe = u_scr[H_HALO - HALO:H_HALO + tb + HALO, cols]
    r = lax.broadcasted_iota(jnp.int32, (tb + 2 * HALO, 1), 0)
    inside = ((r >= HALO) | keep[0]) & ((r < HALO + tb) | keep[1])
    return jnp.where(inside, xe, 0.0)


def _ffn_fwd(h, w_up, w8, bias, w_down):
    n = h.shape[0]
    tb = _tile(n, (256, 128))
    starts, ends = _segment_edges((n,), tb)

    def body(cur_ref, prev_ref, next_ref, wu_ref, w_ref, b_ref, wd_ref, u_ref, o_ref, ff_ref, u_scr):
        keep = _keep_halos(pl.program_id(0), starts, ends)
        _up_project((cur_ref, prev_ref, next_ref), wu_ref, u_scr)
        u_ref[...] = u_scr[H_HALO:H_HALO + tb, :]

        for c0 in range(0, DFF, FFN_GROUP):
            halves = []
            for cols in (slice(c0, c0 + FFN_GROUP), slice(DFF + c0, DFF + c0 + FFN_GROUP)):
                xe = _up_ext_rows(u_scr, cols, keep, tb)
                halves.append(_conv_rows(_shifted_rows(xe, FFN_TAPS), w_ref, cols)[HALO:HALO + tb] + b_ref[:, cols])
            o_ref[:, c0:c0 + FFN_GROUP] = _swiglu_fn(*halves).astype(BF)
        ff_ref[...] = jnp.dot(o_ref[...], wd_ref[...], preferred_element_type=F32)

    return pl.pallas_call(
        body, grid=(n // tb,), name="ffn_fwd",
        in_specs=_halo_specs(tb, D, n, halo=H_HALO) + [_resident((D, 2 * DFF)), pl.BlockSpec((8, 2 * DFF), lambda i: (0, 0)),
                                                        pl.BlockSpec((1, 2 * DFF), lambda i: (0, 0)), _resident((DFF, D))],
        out_specs=[pl.BlockSpec((tb, 2 * DFF), lambda i: (i, 0)), pl.BlockSpec((tb, DFF), lambda i: (i, 0)),
                   pl.BlockSpec((tb, D), lambda i: (i, 0))],
        out_shape=[jax.ShapeDtypeStruct((n, 2 * DFF), F32), jax.ShapeDtypeStruct((n, DFF), BF), jax.ShapeDtypeStruct((n, D), F32)],
        scratch_shapes=[pltpu.VMEM((tb + 2 * H_HALO, 2 * DFF), F32)],
        compiler_params=_cp(),
    )(h, h, h, w_up, w8, bias, w_down)


def _ffn_bwd(u, w_up, w8, bias, da):
    n = u.shape[0]
    tb = _tile(n, (256, 128))
    starts, ends = _segment_edges((n,), tb)

    def body(cur_ref, prev_ref, next_ref, wu_ref, w_ref, b_ref, da_c, da_p, da_n, du_ref, dw_ref, db_ref, dh_ref):
        i = pl.program_id(0)
        keep = _keep_halos(i, starts, ends)

        @pl.when(i == 0)
        def _():
            dw_ref[...] = jnp.zeros_like(dw_ref)
            db_ref[...] = jnp.zeros_like(db_ref)

        for c0 in range(0, DFF, FFN_GROUP):
            col_pair = (slice(c0, c0 + FFN_GROUP), slice(DFF + c0, DFF + c0 + FFN_GROUP))
            shifts = [_shifted_rows(_ext_rows((cur_ref, prev_ref, next_ref), cols, keep), FFN_TAPS) for cols in col_pair]
            convs = [_conv_rows(shifted, w_ref, cols) + b_ref[:, cols] for shifted, cols in zip(shifts, col_pair)]
            dae = _ext_rows((da_c, da_p, da_n), col_pair[0], keep)
            _, vjp = jax.vjp(_swiglu_fn, *convs)
            for shifted, cols, dce in zip(shifts, col_pair, vjp(dae)):
                du_ref[:, cols] = _conv_rows(_shifted_rows(dce, FFN_TAPS, transpose=True), w_ref, cols)[HALO:HALO + tb].astype(BF)
                dcur = dce[HALO:HALO + tb]
                for j, g in enumerate(_tap_grads(dcur, shifted, tb)):
                    dw_ref[j:j + 1, cols] += g
                db_ref[:, cols] += jnp.sum(dcur, axis=0, keepdims=True)
        dh_ref[...] = lax.dot_general(du_ref[...], wu_ref[...], (_DIMS["nt"], ((), ())), preferred_element_type=F32)

    wspec = pl.BlockSpec((8, 2 * DFF), lambda i: (0, 0))
    bspec = pl.BlockSpec((1, 2 * DFF), lambda i: (0, 0))
    return pl.pallas_call(
        body, grid=(n // tb,), name="ffn_bwd",
        in_specs=_halo_specs(tb, 2 * DFF, n) + [_resident((D, 2 * DFF)), wspec, bspec] + _halo_specs(tb, DFF, n),
        out_specs=[pl.BlockSpec((tb, 2 * DFF), lambda i: (i, 0)), wspec, bspec, pl.BlockSpec((tb, D), lambda i: (i, 0))],
        out_shape=[jax.ShapeDtypeStruct((n, 2 * DFF), BF), jax.ShapeDtypeStruct((8, 2 * DFF), F32), jax.ShapeDtypeStruct((1, 2 * DFF), F32),
                   jax.ShapeDtypeStruct((n, D), F32)],
        compiler_params=_cp(),
    )(u, u, u, w_up, w8, bias, da, da, da)


def _loss_kernel(x1, gate, ff, target, w_down):
    n = x1.shape[0]
    tb = _tile(n, (512, 256, 128))

    def body(x_ref, g_ref, f_ref, t_ref, wd_ref, loss_ref, dy_ref, dff_ref, dg_ref, da_ref):
        err = x_ref[...] + g_ref[...] * f_ref[...] - t_ref[...]
        dy = err * (1.0 / D)
        dy_ref[...] = dy
        dff_ref[...] = (g_ref[...] * dy).astype(BF)
        da_ref[...] = lax.dot_general(dff_ref[...], wd_ref[...], (_DIMS["nt"], ((), ())), preferred_element_type=F32)

        @pl.when(pl.program_id(0) == 0)
        def _():
            loss_ref[...] = jnp.zeros_like(loss_ref)
            dg_ref[...] = jnp.zeros_like(dg_ref)
        part = 0.5 * jnp.sum(jnp.sum(err * err, axis=1, keepdims=True) * (1.0 / D), axis=0, keepdims=True)
        loss_ref[...] += jnp.broadcast_to(part, (1, 128))
        dg_ref[...] += jnp.sum(dy * f_ref[...], axis=0, keepdims=True)

    row = pl.BlockSpec((tb, D), lambda i: (i, 0))
    one = pl.BlockSpec((1, D), lambda i: (0, 0))
    return pl.pallas_call(
        body, grid=(n // tb,), name="loss",
        in_specs=[row, one, row, row, _resident((DFF, D))],
        out_specs=[pl.BlockSpec((1, 128), lambda i: (0, 0)), row, row, one, pl.BlockSpec((tb, DFF), lambda i: (i, 0))],
        out_shape=[jax.ShapeDtypeStruct((1, 128), F32), jax.ShapeDtypeStruct((n, D), F32),
                   jax.ShapeDtypeStruct((n, D), BF), jax.ShapeDtypeStruct((1, D), F32), jax.ShapeDtypeStruct((n, DFF), F32)],
        compiler_params=_cp(),
    )(x1, gate, ff, target, w_down)


def _rope_tables(nlat, lc):
    inv_freq = (np.float32(ROPE_BASE) ** (-np.arange(32, dtype=np.float32) / np.float32(32))).astype(np.float32)
    ar = np.arange(nlat // GRID_W, dtype=np.float32)[:, None] * inv_freq
    ac = np.arange(GRID_W, dtype=np.float32)[:, None] * inv_freq
    by_row = lambda a: jnp.repeat(jnp.asarray(a, F32), GRID_W, axis=0)
    by_col = lambda a: jnp.tile(jnp.asarray(a, F32), (nlat // GRID_W, 1))
    cos = jnp.concatenate([by_row(np.cos(ar)), by_row(np.cos(ar)), by_col(np.cos(ac)), by_col(np.cos(ac))], axis=1)
    sin = jnp.concatenate([by_row(-np.sin(ar)), by_row(np.sin(ar)), by_col(-np.sin(ac)), by_col(np.sin(ac))], axis=1)
    cos = jnp.concatenate([cos, jnp.ones((lc, HD), F32)], axis=0)
    sin = jnp.concatenate([sin, jnp.zeros((lc, HD), F32)], axis=0)
    return cos, sin


def _pad_rows8(w):
    return jnp.concatenate([w, jnp.zeros((8 - w.shape[0], w.shape[1]), w.dtype)], axis=0)


def _pack_w_in(w):
    cuts = [sum(IN_SIZES[:i]) for i in range(len(IN_SIZES) + 1)]
    qkv, gt, b, a, q, k, v, mg = [w[:, cuts[i]:cuts[i + 1]] for i in range(len(IN_SIZES))]
    return jnp.concatenate([qkv, gt, q, mg, k, v, b, a, jnp.zeros((w.shape[0], PW - O_BA - 32), w.dtype)], axis=1)


def _unpack_w_in(g):
    return jnp.concatenate([g[:, O_QKV:O_GT], g[:, O_GT:O_Q], g[:, O_BA:O_BA + 32], g[:, O_Q:O_MG], g[:, O_K:O_V],
                            g[:, O_V:O_BA], g[:, O_MG:O_K]], axis=1)


def _local_step(x, ctx, mod_x, mod_c, target, project_in, project_back,
                norm_mix, norm_ffn, dn_conv, a_log, dt_bias, dn_norm, q_norm, k_norm, sink, ffn_conv, ffn_conv_b):
    L, LC = x.shape[0], ctx.shape[0]
    T = L + LC
    seg = lambda r: jnp.stack([mod_x[r], mod_c[r]])[:, None, :]
    sh_a, sc_a = seg(0), seg(1)
    g_a, g_f = mod_x[2][None], mod_x[5][None]
    sh_f, sc_f = mod_x[3][None], mod_x[4][None]
    cos, sin = _rope_tables(L, LC)
    dnc8 = _pad_rows8(dn_conv)
    ffc8 = _pad_rows8(ffn_conv)
    gate_row = lambda a: jnp.concatenate([jnp.zeros((1, 16), F32), a.reshape(1, 16), jnp.zeros((1, 96), F32)], axis=1)
    alog_row, dt_row = gate_row(a_log), gate_row(dt_bias)
    sinkb = jnp.concatenate([jnp.broadcast_to(sink.reshape(KVH, GRP, 1), (KVH, GRP, 128)), jnp.zeros((KVH, 8 - GRP, 128), F32)], axis=1)

    h1 = _norm_mod_fwd(x, ctx, norm_mix, sh_a, sc_a, "norm_mix_fwd")
    p, (w_in_p, w_bdn, w_bat, w_out, w_up, w_down) = project_in(h1)
    q, k, v, gb = _dn_pre_fwd(p, dnc8, alog_row, dt_row, (L, LC))
    wy = _dn1_fwd(q, k, v, gb)
    scans = _dn2_fwd([t[:6] for t in wy], L)
    o_dir = [s[0] for s in scans]
    y_dn, z_dn = _ghn_fwd(o_dir[0], o_dir[1], p, dn_norm, w_bdn, L)
    qr, kr, vb = _attn_prep_fwd(p, q_norm, k_norm, cos, sin)
    y_at, z_at = _attn_fwd(qr, kr, vb, sinkb, w_bat, L)
    merged, mix = _merge_fwd(z_dn, z_at, p, w_out, L)
    x1, h2 = _resid_norm_fwd(x, g_a, mix, norm_ffn, sh_f, sc_f)
    u_raw, act, ff = _ffn_fwd(h2, w_up, ffc8, ffn_conv_b, w_down)
    loss_row, dy, dff, dg_f, dact = _loss_kernel(x1, g_f, ff, target, w_down)

    g_down = _mm(act, dff, form="tn", out_dtype=BF, name="g_ffn_down")
    du_raw, g_ffc8, g_ffb, dh2 = _ffn_bwd(u_raw, w_up, ffc8, ffn_conv_b, dact)
    g_up = _mm(h2, du_raw, form="tn", out_dtype=BF, name="g_ffn_up")
    dx1, dmix, dg_a, g_nffn, dsh_f, dsc_f, dmerged = _resid_norm_bwd(x1, g_a, mix, norm_ffn, sh_f, sc_f, dh2, dy, w_out)

    g_out = _mm(merged, dmix, form="tn", out_dtype=BF, name="g_w_out")
    dz_dn, dz_at, dmg, dy_dn, dy_at = _merge_bwd(z_dn, z_at, p, dmerged, w_bdn, w_bat, L)
    g_bdn = _mm(y_dn, dz_dn, form="tn", out_dtype=BF, name="g_branch_dn")
    g_bat = _mm(y_at, dz_at, form="tn", out_dtype=BF, name="g_branch_at")
    dqr, dkp, dvp, dkx, dvx, dsink = _attn_bwd(qr, kr, vb, sinkb, dy_at, L)
    dq_raw, dk_raw, dv_raw, g_qn, g_kn = _attn_prep_bwd(p, q_norm, k_norm, cos, sin, dqr, dkp, dvp, dkx, dvx, L)
    do, dgt, g_dnn = _ghn_bwd(o_dir[0], o_dir[1], p, dn_norm, dy_dn, L)
    cots = _dn2_bwd([wy[d][:6] + (scans[d][1],) for d in (0, 1)], do, L)
    dq, dk, dv, dgb = _dn1_bwd(q, k, v, gb, [t[6] for t in wy], cots)
    dp, g_dnc8, g_alog, g_dt = _dn_pre_bwd(p, dnc8, alog_row, dt_row, dq, dk, dv, dgb, (dgt, dq_raw, dmg, dk_raw, dv_raw), (L, LC))
    big, dh1 = project_back(h1, dp, w_in_p, (g_bdn, g_bat, g_out, g_up, g_down))
    grad_x, g_nmix_x, dsh_a, dsc_a = _norm_mod_bwd(x, norm_mix, mod_x[0][None], mod_x[1][None], dh1, row0=0,
                                                   name="norm_mix_bwd", residual=dx1)
    g_nmix_c, dsh_c, dsc_c = _norm_mod_bwd(ctx, norm_mix, mod_c[0][None], mod_c[1][None], dh1, row0=L, name="norm_mix_bwd_ctx")
    g_nmix = g_nmix_x + g_nmix_c

    zero = jnp.zeros((D,), F32)
    dmod_x = jnp.stack([dsh_a[0], dsc_a[0], dg_a[0], dsh_f[0], dsc_f[0], dg_f[0]])
    dmod_c = jnp.stack([dsh_c[0], dsc_c[0], zero, zero, zero, zero])
    small = dict(
        dmod_x=dmod_x, dmod_c=dmod_c, norm_mix=g_nmix, norm_ffn=g_nffn, dn_conv=g_dnc8[:5], dn_a_log=g_alog[0, 16:32].reshape(2, 8),
        dn_dt_bias=g_dt[0, 16:32].reshape(2, 8), dn_norm=g_dnn, q_norm=g_qn, k_norm=g_kn,
        attn_sink=jnp.sum(dsink[:, :GRP, :], axis=2).reshape(1, NH), ffn_conv=g_ffc8[:3], ffn_conv_b=g_ffb)
    return loss_row[0, 0], grad_x, big, small


def _exchange(arrays, scatter, name):
    n = len(arrays)

    def body(*refs):
        args = (refs[:n], refs[n:2 * n], *refs[2 * n:], scatter)
        _exchange_start(*args)
        _exchange_wait(*args)

    hbm = pl.BlockSpec(memory_space=pl.ANY)
    out_shape, sems = _exchange_shapes(arrays, scatter)
    return pl.pallas_call(body, name=name, in_specs=[hbm] * n, out_specs=[hbm] * n, out_shape=out_shape,
                          scratch_shapes=sems)(*arrays)


def _gather_two_level(arrays, name):
    n = len(arrays)

    def body(*refs):
        ins, outs = refs[:n], refs[n:2 * n]
        send_sems, recv_sems, local_sems = refs[2 * n:]
        x, y, c = lax.axis_index("x"), lax.axis_index("y"), lax.axis_index("c")
        sibling = (x, y, 1 - c)
        chips = [(1 - x, y), (x, 1 - y), (1 - x, 1 - y)]

        def copy(k, j, block, to, src=None):
            slot = outs[k].at[4 * block[0] + 2 * block[1] + block[2]]
            return pltpu.make_async_remote_copy(src_ref=slot if src is None else src, dst_ref=slot,
                                                send_sem=send_sems.at[7 * k + j], recv_sem=recv_sems.at[7 * k + j],
                                                device_id=to, device_id_type=MESH)

        mine = [pltpu.make_async_copy(ins[k], outs[k].at[4 * x + 2 * y + c], local_sems.at[k]) for k in range(n)]
        for cp in mine:
            cp.start()
        first = []
        for k in range(n):
            first.append(copy(k, 0, (x, y, c), sibling, src=ins[k]))
            first += [copy(k, 1 + j, (x, y, c), (*chip, c), src=ins[k]) for j, chip in enumerate(chips)]
        for cp in first:
            cp.start()
        passed = []
        for k in range(n):
            for j, chip in enumerate(chips):
                copy(k, 1 + j, (*chip, c), (x, y, c)).wait_recv()
                forward = copy(k, 4 + j, (*chip, c), sibling)
                forward.start()
                passed.append(forward)
        for k in range(n):
            copy(k, 0, sibling, (x, y, c)).wait_recv()
            for j, chip in enumerate(chips):
                copy(k, 4 + j, (*chip, 1 - c), (x, y, c)).wait_recv()
        for cp in first + passed:
            cp.wait_send()
        for cp in mine:
            cp.wait()

    hbm = pl.BlockSpec(memory_space=pl.ANY)
    out_shape, sems = _exchange_shapes(arrays, False)
    return pl.pallas_call(body, name=name, in_specs=[hbm] * n, out_specs=[hbm] * n, out_shape=out_shape,
                          scratch_shapes=sems)(*arrays)


def _ada_fwd(c16, w_ada, b_ada):
    def body(c_ref, w_ref, b_ref, o_ref):
        o_ref[...] = _dot_hi(jax.nn.silu(c_ref[...]), w_ref[...]) + b_ref[...]

    return pl.pallas_call(body, name="ada_fwd", out_shape=jax.ShapeDtypeStruct((16, w_ada.shape[1]), F32))(c16, w_ada, b_ada)


def _ada_bwd(c16, w_ada, dmx, dmc):
    def body(c_ref, w_ref, dmx_ref, dmc_ref, gw_ref, pc_ref):
        dmc_tot = dmc_ref[0:1, :]
        for d in range(1, N_DEV):
            dmc_tot = dmc_tot + dmc_ref[d:d + 1, :]
        dm16 = jnp.concatenate([dmx_ref[...], jnp.broadcast_to(dmc_tot, (8, dmc_tot.shape[1]))], axis=0)
        row = lax.broadcasted_iota(jnp.int32, dm16.shape, 0)
        dm16 = jnp.where(row <= 8, dm16, 0.0)
        s = jax.nn.silu(c_ref[...])
        gw_ref[...] = lax.dot_general(s, dm16, (_DIMS["tn"], ((), ())), precision=HI, preferred_element_type=F32)
        pc = lax.dot_general(dm16, w_ref[...], (_DIMS["nt"], ((), ())), precision=HI, preferred_element_type=F32)
        pc_ref[...] = pc[8:9, :]

    return pl.pallas_call(body, name="ada_bwd", out_shape=[jax.ShapeDtypeStruct(w_ada.shape, F32), jax.ShapeDtypeStruct((1, D), F32)],
                          compiler_params=_cp())(c16, w_ada, dmx, dmc)


def _cctx_grad(pc_all, c_ctx_row):
    def body(pc_ref, c_ref, g_ref):
        tot = pc_ref[0]
        for d in range(1, N_DEV):
            tot = tot + pc_ref[d]
        _, vjp = jax.vjp(jax.nn.silu, c_ref[...])
        g_ref[...] = vjp(tot)[0]

    return pl.pallas_call(body, name="cctx_grad", out_shape=jax.ShapeDtypeStruct((1, D), F32))(pc_all, c_ctx_row)


def _adamw(parts, w, m, v, name):
    ns, R, C = parts.shape
    tb = _tile(R, (128, 64, 32, 16, 8))

    def body(p_ref, w_ref, m_ref, v_ref, g_ref, d_ref, mo_ref, vo_ref):
        g = p_ref[0].astype(F32)
        for s in range(1, ns):
            g = g + p_ref[s].astype(F32)
        m2 = ADAM_B1 * m_ref[...] + (1.0 - ADAM_B1) * g
        v2 = ADAM_B2 * v_ref[...] + (1.0 - ADAM_B2) * jnp.square(g)
        m_hat = m2 / (1.0 - ADAM_B1 ** ADAM_STEP)
        v_hat = v2 / (1.0 - ADAM_B2 ** ADAM_STEP)
        g_ref[...] = g
        d_ref[...] = -ADAM_LR * (m_hat / (jnp.sqrt(v_hat) + ADAM_EPS) + ADAM_WD * w_ref[...])
        mo_ref[...] = m2
        vo_ref[...] = v2

    row = pl.BlockSpec((tb, C), lambda i: (i, 0))
    return pl.pallas_call(
        body, grid=(R // tb,), name=name,
        in_specs=[pl.BlockSpec((ns, tb, C), lambda i: (0, i, 0)), row, row, row], out_specs=[row] * 4,
        out_shape=[jax.ShapeDtypeStruct((R, C), F32)] * 4, compiler_params=_cp(),
    )(parts, w, m, v)


_SMALL = (("dmod_x", 6 * D), ("dmod_c", 6 * D), ("b_ada", 6 * D), ("norm_mix", D), ("norm_ffn", D), ("dn_a_log", 16),
          ("dn_dt_bias", 16), ("dn_norm", HD), ("q_norm", HD), ("k_norm", HD), ("attn_sink", NH), ("ffn_conv_b", 2 * DFF),
          ("dn_conv", 5 * 3 * D), ("ffn_conv", 3 * 2 * DFF))
_SMALL_ROWS = -(-sum(n for _, n in _SMALL) // 1024) * 8


def _pack_small(d):
    flat = jnp.concatenate([d[k].reshape(-1).astype(F32) if k in d else jnp.zeros((n,), F32) for k, n in _SMALL])
    return jnp.concatenate([flat, jnp.zeros((_SMALL_ROWS * 128 - flat.shape[0],), F32)]).reshape(_SMALL_ROWS, 128)


def _unpack_small(a):
    flat = a.reshape(a.shape[:-2] + (-1,))
    out, off = {}, 0
    for k, n in _SMALL:
        out[k] = flat[..., off:off + n]
        off += n
    return out


def kernel(x, c, ctx, c_ctx, w_ada, b_ada, norm_mix, norm_ffn, w_in, dn_conv, dn_a_log, dn_dt_bias, dn_norm, q_norm, k_norm, attn_sink, w_branch_dn, w_branch_attn, w_out, ffn_up, ffn_conv, ffn_conv_b, ffn_down, loss_target, m_c_ctx, m_w_ada, m_b_ada, m_norm_mix, m_norm_ffn, m_w_in, m_dn_conv, m_dn_a_log, m_dn_dt_bias, m_dn_norm, m_q_norm, m_k_norm, m_attn_sink, m_w_branch_dn, m_w_branch_attn, m_w_out, m_ffn_up, m_ffn_conv, m_ffn_conv_b, m_ffn_down, v_c_ctx, v_w_ada, v_b_ada, v_norm_mix, v_norm_ffn, v_w_in, v_dn_conv, v_dn_a_log, v_dn_dt_bias, v_dn_norm, v_q_norm, v_k_norm, v_attn_sink, v_w_branch_dn, v_w_branch_attn, v_w_out, v_ffn_up, v_ffn_conv, v_ffn_conv_b, v_ffn_down):
    me = 4 * lax.axis_index("x") + 2 * lax.axis_index("y") + lax.axis_index("c")
    ada_cols = w_ada.shape[2]

    cols = lambda a: jnp.swapaxes(a, 0, 1).reshape(a.shape[1], -1)
    rows = lambda a: a.reshape(-1, a.shape[2])
    col_blocks = lambda g: jnp.swapaxes(g.reshape(g.shape[0], N_DEV, -1), 0, 1)
    row_blocks = lambda g: g.reshape(N_DEV, -1, g.shape[1])

    gathered = _gather_two_level([w_in[0].astype(BF), c, dn_conv[0], ffn_conv[0]], name="gather_first")
    w_in_packed = _pack_w_in(cols(gathered[0]))
    c_all = gathered[1][:, 0, :]

    def project_in(h1):
        p, rest = _mm(h1, w_in_packed, form="nn", out_dtype=F32, name="in_proj",
                      exchange=([w_branch_dn[0].astype(BF), w_branch_attn[0].astype(BF), w_out[0].astype(BF),
                                 ffn_up[0].astype(BF), ffn_down[0].astype(BF)], False))
        return p, (w_in_packed, rows(rest[0]), rows(rest[1]), rows(rest[2]), cols(rest[3]), rows(rest[4]))

    def project_back(h1, dp, w_in_p, grads):
        g_bdn, g_bat, g_out, g_up, g_down = grads
        g_in, landed_rest = _mm(h1, dp, form="tn", out_dtype=BF, name="g_w_in",
                                exchange=([row_blocks(g_bdn), row_blocks(g_bat), row_blocks(g_out), col_blocks(g_up),
                                           row_blocks(g_down)], True))
        dh1, landed_in = _mm(dp, w_in_p, form="nt", out_dtype=F32, name="d_h1",
                             exchange=([col_blocks(_unpack_w_in(g_in))], True))
        return [landed_in[0]] + landed_rest, dh1

    c16 = jnp.concatenate([c_all, c_ctx[None], jnp.zeros((7, D), F32)], axis=0)
    b_loc = lax.dynamic_slice_in_dim(b_ada, me * ada_cols, ada_cols, axis=1)
    mod_part = _ada_fwd(c16, w_ada[0], b_loc)
    mod_all = cols(_exchange([mod_part], scatter=False, name="gather_mod")[0])
    mod_x = lax.dynamic_slice_in_dim(mod_all, me, 1, axis=0).reshape(6, D)
    mod_c = mod_all[8].reshape(6, D)

    loss_loc, grad_x, landed, small = _local_step(
        x[0], ctx[0], mod_x, mod_c, loss_target[0], project_in, project_back,
        norm_mix, norm_ffn, cols(gathered[2]), dn_a_log[0], dn_dt_bias[0], dn_norm, q_norm, k_norm, attn_sink[0], cols(gathered[3]),
        ffn_conv_b)
    loss = lax.psum(loss_loc, ("x", "y", "c"))

    res = {}
    res["w_in"] = _adamw(landed[0], w_in[0], m_w_in[0], v_w_in[0], "adamw_w_in")
    res["w_branch_dn"] = _adamw(landed[1], w_branch_dn[0], m_w_branch_dn[0], v_w_branch_dn[0], "adamw_w_branch_dn")
    res["w_branch_attn"] = _adamw(landed[2], w_branch_attn[0], m_w_branch_attn[0], v_w_branch_attn[0], "adamw_w_branch_attn")
    res["w_out"] = _adamw(landed[3], w_out[0], m_w_out[0], v_w_out[0], "adamw_w_out")
    res["ffn_up"] = _adamw(landed[4], ffn_up[0], m_ffn_up[0], v_ffn_up[0], "adamw_ffn_up")
    res["ffn_down"] = _adamw(landed[5], ffn_down[0], m_ffn_down[0], v_ffn_down[0], "adamw_ffn_down")

    small = dict(small)
    small["b_ada"] = small["dmod_x"] + small["dmod_c"]
    parts = _exchange([_pack_small(small)], scatter=False, name="gather_small")[0]
    per_dev = _unpack_small(parts)
    given = dict(b_ada=(b_ada, m_b_ada, v_b_ada), norm_mix=(norm_mix, m_norm_mix, v_norm_mix), norm_ffn=(norm_ffn, m_norm_ffn, v_norm_ffn),
                 dn_a_log=(dn_a_log, m_dn_a_log, v_dn_a_log), dn_dt_bias=(dn_dt_bias, m_dn_dt_bias, v_dn_dt_bias),
                 dn_norm=(dn_norm, m_dn_norm, v_dn_norm), q_norm=(q_norm, m_q_norm, v_q_norm), k_norm=(k_norm, m_k_norm, v_k_norm),
                 attn_sink=(attn_sink, m_attn_sink, v_attn_sink), ffn_conv_b=(ffn_conv_b, m_ffn_conv_b, v_ffn_conv_b))
    packs = [_pack_small({k: t[j] for k, t in given.items()}) for j in range(3)]
    upd = [_unpack_small(a) for a in _adamw(parts, packs[0], packs[1], packs[2], "adamw_small")]
    for k, t in given.items():
        res[k] = tuple(u[k].reshape(t[0].shape) for u in upd)
    dnc = lax.dynamic_slice_in_dim(upd[0]["dn_conv"].reshape(5, 3 * D), me * dn_conv.shape[2], dn_conv.shape[2], axis=1)
    ffc = lax.dynamic_slice_in_dim(upd[0]["ffn_conv"].reshape(3, 2 * DFF), me * ffn_conv.shape[2], ffn_conv.shape[2], axis=1)
    r8 = lambda a: _pad_rows8(a)
    t = _adamw(r8(dnc)[None], r8(dn_conv[0]), r8(m_dn_conv[0]), r8(v_dn_conv[0]), "adamw_dn_conv")
    res["dn_conv"] = tuple(a[:5][None] for a in t)
    t = _adamw(r8(ffc)[None], r8(ffn_conv[0]), r8(m_ffn_conv[0]), r8(v_ffn_conv[0]), "adamw_ffn_conv")
    res["ffn_conv"] = tuple(a[:3][None] for a in t)

    dmx = lax.dynamic_slice_in_dim(per_dev["dmod_x"], me * ada_cols, ada_cols, axis=1)
    dmc = lax.dynamic_slice_in_dim(per_dev["dmod_c"], me * ada_cols, ada_cols, axis=1)
    g_ada, pc = _ada_bwd(c16, w_ada[0], dmx, dmc)
    res["w_ada"] = _adamw(g_ada[None], w_ada[0], m_w_ada[0], v_w_ada[0], "adamw_w_ada")
    pc_all = _exchange([pc], scatter=False, name="gather_cctx")[0]
    g_cctx = _cctx_grad(pc_all, c_ctx[None])
    r8b = lambda a: jnp.broadcast_to(a, (8, D))
    t = _adamw(r8b(g_cctx)[None], r8b(c_ctx[None]), r8b(m_c_ctx[None]), r8b(v_c_ctx[None]), "adamw_c_ctx")
    res["c_ctx"] = tuple(a[0] for a in t)

    names = ("c_ctx", "w_ada", "b_ada", "norm_mix", "norm_ffn", "w_in", "dn_conv", "dn_a_log", "dn_dt_bias", "dn_norm", "q_norm",
             "k_norm", "attn_sink", "w_branch_dn", "w_branch_attn", "w_out", "ffn_up", "ffn_conv", "ffn_conv_b", "ffn_down")
    lead = ("w_ada", "w_in", "w_branch_dn", "w_branch_attn", "w_out", "ffn_up", "ffn_down")
    fix = lambda k, a: a[None] if k in lead else a
    outs = [loss, grad_x[None]]
    for j in range(4):
        outs += [fix(k, res[k][j]) for k in names]
    return tuple(outs)
```

```python
import functools

import jax
import jax.numpy as jnp
import numpy as np
from jax import lax
from jax.experimental import pallas as pl
from jax.experimental.pallas import tpu as pltpu

F32 = jnp.float32
BF = jnp.bfloat16
HI = lax.Precision.HIGHEST
MESH = pl.DeviceIdType.MESH

D = 1024
NH = 8
HD = 128
KVH = 2
GRP = 4
KV = KVH * HD
DFF = 2816
CB = 128
GRID_W = 64
ROPE_BASE = 10000.0
EPS = 1e-6
N_DEV = 8
PW = 8192
O_QKV, O_GT, O_Q, O_MG, O_K, O_V, O_BA = 0, 3072, 4096, 5120, 7168, 7424, 7680
IN_SIZES = (3072, 1024, 16, 16, 1024, 256, 256, 2048)
IN_DIM = sum(IN_SIZES)
ADAM_LR, ADAM_B1, ADAM_B2, ADAM_EPS, ADAM_WD, ADAM_STEP = 0.001, 0.9, 0.999, 1e-08, 0.01, 10
VMEM_LIMIT = 56 * 1024 * 1024


def _cp():
    return pltpu.CompilerParams(vmem_limit_bytes=VMEM_LIMIT)


def _tile(n, cands):
    for c in cands:
        if n % c == 0:
            return c
    return n


def _iota2(shape):
    return lax.broadcasted_iota(jnp.int32, shape, 0), lax.broadcasted_iota(jnp.int32, shape, 1)


_DIMS = {"nn": ((1,), (0,)), "nt": ((1,), (1,)), "tn": ((0,), (0,))}


def _exchange_copies(ins, outs, send_sems, recv_sems, local_sems, scatter, landings):
    x, y, c = lax.axis_index("x"), lax.axis_index("y"), lax.axis_index("c")
    me = 4 * x + 2 * y + c
    local, remote = [], []
    for k in range(len(ins)):
        local.append(pltpu.make_async_copy(ins[k].at[me] if scatter else ins[k], outs[k].at[me], local_sems.at[k]))
        for m in range(1, N_DEV):
            px = 1 - x if m & 4 else x
            py = 1 - y if m & 2 else y
            pc = 1 - c if m & 1 else c
            peer = 4 * px + 2 * py + pc
            src = ins[k].at[peer] if scatter else ins[k]
            sem = k * (N_DEV - 1) + m - 1
            push = pltpu.make_async_remote_copy(src_ref=src, dst_ref=outs[k].at[me], send_sem=send_sems.at[sem],
                                                recv_sem=recv_sems.at[sem], device_id=(px, py, pc), device_id_type=MESH)
            landing = None
            if landings:
                landing = pltpu.make_async_remote_copy(src_ref=src, dst_ref=outs[k].at[peer], send_sem=send_sems.at[sem],
                                                       recv_sem=recv_sems.at[sem], device_id=(px, py, pc), device_id_type=MESH)
            remote.append((push, landing))
    return local, remote


def _exchange_start(*args):
    local, remote = _exchange_copies(*args, landings=False)
    for cp in local:
        cp.start()
    for push, _ in remote:
        push.start()


def _exchange_wait(*args):
    local, remote = _exchange_copies(*args, landings=True)
    for _, landing in remote:
        landing.wait_recv()
    for push, _ in remote:
        push.wait_send()
    for cp in local:
        cp.wait()


def _exchange_shapes(arrays, scatter):
    out_shape = [jax.ShapeDtypeStruct(a.shape if scatter else (N_DEV,) + a.shape, a.dtype) for a in arrays]
    n = len(arrays)
    sems = [pltpu.SemaphoreType.DMA((n * (N_DEV - 1),)), pltpu.SemaphoreType.DMA((n * (N_DEV - 1),)), pltpu.SemaphoreType.DMA((n,))]
    return out_shape, sems


def _mm(a, b, *, form, out_dtype, name, tm=None, tn=None, tk=None, exchange=None):
    if form == "tn":
        K, M = a.shape
        N = b.shape[1]
    else:
        M, K = a.shape
        N = b.shape[0] if form == "nt" else b.shape[1]
    tm = tm or _tile(M, (1408, 1280, 1024, 640, 512, 256, 128))
    tn = tn or _tile(N, (1408, 1024, 512, 256, 128))
    tk = tk or _tile(K, (2048, 1408, 1280, 1024, 640, 512, 256, 128))
    ni, nj, nk = M // tm, N // tn, K // tk
    dims = (_DIMS[form], ((), ()))
    ex_arrays, scatter = exchange if exchange else ([], False)
    nx = len(ex_arrays)

    def body(a_ref, b_ref, *refs):
        ex_in, o_ref, ex_out, scratch = refs[:nx], refs[nx], refs[nx + 1:2 * nx + 1], refs[2 * nx + 1:]
        i, j, k = pl.program_id(0), pl.program_id(1), pl.program_id(2)
        if nx:
            sems = scratch[-3:]

            @pl.when((i == 0) & (j == 0) & (k == 0))
            def _():
                _exchange_start(ex_in, ex_out, *sems, scatter)

        part = lax.dot_general(a_ref[...].astype(BF), b_ref[...].astype(BF), dims, preferred_element_type=F32)
        if nk == 1:
            o_ref[...] = part.astype(out_dtype)
        else:
            acc_ref = scratch[0]

            @pl.when(k == 0)
            def _():
                acc_ref[...] = part

            @pl.when(k > 0)
            def _():
                acc_ref[...] += part

            @pl.when(k == nk - 1)
            def _():
                o_ref[...] = acc_ref[...].astype(out_dtype)

        if nx:
            @pl.when((i == ni - 1) & (j == nj - 1) & (k == nk - 1))
            def _():
                _exchange_wait(ex_in, ex_out, *sems, scatter)

    if form == "tn":
        a_spec = pl.BlockSpec((tk, tm), lambda i, j, k: (k, i))
    else:
        a_spec = pl.BlockSpec((tm, tk), lambda i, j, k: (i, k))
    if form == "nt":
        b_spec = pl.BlockSpec((tn, tk), lambda i, j, k: (j, k))
    else:
        b_spec = pl.BlockSpec((tk, tn), lambda i, j, k: (k, j))
    hbm = pl.BlockSpec(memory_space=pl.ANY)
    ex_shapes, ex_sems = _exchange_shapes(ex_arrays, scatter) if nx else ([], [])
    outs = pl.pallas_call(
        body, grid=(ni, nj, nk), name=name,
        in_specs=[a_spec, b_spec] + [hbm] * nx, out_specs=[pl.BlockSpec((tm, tn), lambda i, j, k: (i, j))] + [hbm] * nx,
        out_shape=[jax.ShapeDtypeStruct((M, N), out_dtype)] + ex_shapes,
        scratch_shapes=([] if nk == 1 else [pltpu.VMEM((tm, tn), F32)]) + ex_sems,
        compiler_params=_cp(),
    )(a, b, *ex_arrays)
    return (outs[0], list(outs[1:])) if nx else outs[0]


def _norm_mod_fn(x, nw, sh, sc):
    y = x * lax.rsqrt(jnp.mean(x * x, axis=-1, keepdims=True) + EPS)
    return (y * nw) * (1.0 + sc) + sh


def _norm_mod_fwd(x, ctx, nw, sh, sc, name):
    nlat = x.shape[0]
    T = nlat + ctx.shape[0]
    tb = _tile(ctx.shape[0], (256, 128))
    nlb = nlat // tb

    def body(x_ref, c_ref, nw_ref, sh_ref, sc_ref, h_ref):
        rows = jnp.where(pl.program_id(0) < nlb, x_ref[...], c_ref[...])
        h_ref[...] = _norm_mod_fn(rows, nw_ref[...], sh_ref[0], sc_ref[0]).astype(BF)

    seg = pl.BlockSpec((1, 1, D), lambda i: (jnp.where(i >= nlb, 1, 0), 0, 0))
    return pl.pallas_call(
        body, grid=(T // tb,), name=name,
        in_specs=[pl.BlockSpec((tb, D), lambda i: (jnp.minimum(i, nlb - 1), 0)),
                  pl.BlockSpec((tb, D), lambda i: (jnp.maximum(i - nlb, 0), 0)), pl.BlockSpec((1, D), lambda i: (0, 0)), seg, seg],
        out_specs=pl.BlockSpec((tb, D), lambda i: (i, 0)),
        out_shape=jax.ShapeDtypeStruct((T, D), BF),
    )(x, ctx, nw, sh, sc)


def _norm_mod_bwd(x, nw, sh, sc, dh, *, row0, name, residual=None):
    nrows = x.shape[0]
    tb = _tile(nrows, (512, 256, 128))
    b0 = row0 // tb

    def body(x_ref, nw_ref, sh_ref, sc_ref, dh_ref, *refs):
        dnw_ref, dsh_ref, dsc_ref = refs[-3:]
        _, vjp = jax.vjp(_norm_mod_fn, x_ref[...], nw_ref[...], sh_ref[...], sc_ref[...])
        dx, dnw, dsh, dsc = vjp(dh_ref[...])
        if residual is not None:
            refs[1][...] = dx + refs[0][...]

        @pl.when(pl.program_id(0) == 0)
        def _():
            dnw_ref[...] = jnp.zeros_like(dnw_ref)
            dsh_ref[...] = jnp.zeros_like(dsh_ref)
            dsc_ref[...] = jnp.zeros_like(dsc_ref)

        dnw_ref[...] += dnw
        dsh_ref[...] += dsh
        dsc_ref[...] += dsc

    dh_row = pl.BlockSpec((tb, D), lambda i: (b0 + i, 0))
    out_row = pl.BlockSpec((tb, D), lambda i: (i, 0))
    one = pl.BlockSpec((1, D), lambda i: (0, 0))
    with_dx = residual is not None
    return pl.pallas_call(
        body, grid=(nrows // tb,), name=name,
        in_specs=[out_row, one, one, one, dh_row] + [out_row] * with_dx, out_specs=[out_row] * with_dx + [one] * 3,
        out_shape=[jax.ShapeDtypeStruct((nrows, D), F32)] * with_dx + [jax.ShapeDtypeStruct((1, D), F32)] * 3,
        compiler_params=_cp(),
    )(x, nw, sh, sc, dh, *([residual] if with_dx else []))


def _resid_norm_fwd(x, gate, y, nw, sh, sc):
    n = y.shape[0]
    tb = _tile(n, (512, 256, 128))

    def body(x_ref, g_ref, y_ref, nw_ref, sh_ref, sc_ref, x1_ref, h_ref):
        x1 = x_ref[...] + g_ref[...] * y_ref[...]
        x1_ref[...] = x1
        h_ref[...] = _norm_mod_fn(x1, nw_ref[...], sh_ref[...], sc_ref[...]).astype(BF)

    row = pl.BlockSpec((tb, D), lambda i: (i, 0))
    one = pl.BlockSpec((1, D), lambda i: (0, 0))
    return pl.pallas_call(
        body, grid=(n // tb,), name="resid_norm_fwd",
        in_specs=[row, one, row, one, one, one], out_specs=[row, row],
        out_shape=[jax.ShapeDtypeStruct((n, D), F32), jax.ShapeDtypeStruct((n, D), BF)],
        compiler_params=_cp(),
    )(x, gate, y, nw, sh, sc)


def _resid_norm_bwd(x1, gate, y, nw, sh, sc, dh, dx1_direct, w_out):
    n = y.shape[0]
    tb = _tile(n, (512, 256, 128))

    def body(x1_ref, g_ref, y_ref, nw_ref, sh_ref, sc_ref, dh_ref, dd_ref, wo_ref,
             dx_ref, dy_ref, dg_ref, dnw_ref, dsh_ref, dsc_ref, dm_ref):
        _, vjp = jax.vjp(_norm_mod_fn, x1_ref[...], nw_ref[...], sh_ref[...], sc_ref[...])
        dxn, dnw, dsh, dsc = vjp(dh_ref[...])
        dx = dxn + dd_ref[...]
        dx_ref[...] = dx
        dy_ref[...] = (g_ref[...] * dx).astype(BF)
        dm_ref[...] = lax.dot_general(dy_ref[...], wo_ref[...], (_DIMS["nt"], ((), ())), preferred_element_type=F32)

        @pl.when(pl.program_id(0) == 0)
        def _():
            for r in (dg_ref, dnw_ref, dsh_ref, dsc_ref):
                r[...] = jnp.zeros_like(r)

        dg_ref[...] += jnp.sum(dx * y_ref[...], axis=0, keepdims=True)
        dnw_ref[...] += dnw
        dsh_ref[...] += dsh
        dsc_ref[...] += dsc

    row = pl.BlockSpec((tb, D), lambda i: (i, 0))
    one = pl.BlockSpec((1, D), lambda i: (0, 0))
    return pl.pallas_call(
        body, grid=(n // tb,), name="resid_norm_bwd",
        in_specs=[row, one, row, one, one, one, row, row, _resident((D, D))], out_specs=[row, row] + [one] * 4 + [row],
        out_shape=[jax.ShapeDtypeStruct((n, D), F32), jax.ShapeDtypeStruct((n, D), BF)] + [jax.ShapeDtypeStruct((1, D), F32)] * 4
        + [jax.ShapeDtypeStruct((n, D), F32)],
        compiler_params=_cp(),
    )(x1, gate, y, nw, sh, sc, dh, dx1_direct, w_out)


HALO = 8


def _halo_specs(tb, width, nrows, col=0, halo=HALO):
    r8 = tb // halo
    cur = pl.BlockSpec((tb, width), lambda i: (i, col))
    prev = pl.BlockSpec((halo, width), lambda i: (jnp.maximum(i * r8 - 1, 0), col))
    nxt = pl.BlockSpec((halo, width), lambda i: (jnp.minimum((i + 1) * r8, nrows // halo - 1), col))
    return [cur, prev, nxt]


def _segment_edges(seg_rows, tb):
    bounds = [0]
    for s in seg_rows:
        bounds.append(bounds[-1] + s // tb)
    return bounds[:-1], [b - 1 for b in bounds[1:]]


def _keep_halos(i, starts, ends):
    keep_p = functools.reduce(lambda a, b: a & b, [i != s for s in starts])
    keep_n = functools.reduce(lambda a, b: a & b, [i != e for e in ends])
    return keep_p, keep_n


def _ext_rows(refs, cols, keep):
    cur_ref, prev_ref, next_ref = refs
    p = jnp.where(keep[0], prev_ref[:, cols].astype(F32), 0.0)
    n = jnp.where(keep[1], next_ref[:, cols].astype(F32), 0.0)
    return jnp.concatenate([p, cur_ref[:, cols].astype(F32), n], axis=0)


def _shifted_rows(xe, width, transpose=False):
    r = width // 2
    n = xe.shape[0]
    out = []
    for j in range(width):
        s = ((j - r) if transpose else (r - j)) % n
        out.append(xe if s == 0 else pltpu.roll(xe, s, 0))
    return out


def _conv_rows(shifted, w_ref, cols):
    acc = None
    for j, xs in enumerate(shifted):
        term = xs * w_ref[j:j + 1, cols]
        acc = term if acc is None else acc + term
    return acc


def _tap_grads(dcur, shifted, tb):
    return [jnp.sum(dcur * xs[HALO:HALO + tb], axis=0, keepdims=True) for xs in shifted]


def _softplus(x):
    return jnp.maximum(x, 0.0) + jnp.log(1.0 + jnp.exp(-jnp.abs(x)))


def _gates_fn(ba, alog_row, dt_row):
    col = lax.broadcasted_iota(jnp.int32, ba.shape, 1)
    beta = jax.nn.sigmoid(ba)
    g = -jnp.exp(alog_row) * _softplus(ba + dt_row)
    return jnp.where(col < 16, beta, jnp.where(col < 32, g, 0.0))


def _qkv_post_fn(c, kind):
    y = jax.nn.silu(c)
    if kind == 2:
        return y
    n = y * lax.rsqrt(jnp.sum(y * y, axis=-1, keepdims=True) + EPS)
    return n * (HD ** -0.5) if kind == 0 else n


DN_TAPS = 5
FFN_TAPS = 3


def _dn_pre_fwd(p, w8, alog_row, dt_row, seg_rows):
    T = p.shape[0]
    tb = _tile(T, (256, 128))
    starts, ends = _segment_edges(seg_rows, tb)

    def body(cur_ref, prev_ref, next_ref, ba_ref, w_ref, al_ref, dt_ref, q_ref, k_ref, v_ref, gb_ref):
        keep = _keep_halos(pl.program_id(0), starts, ends)
        outs = (q_ref, k_ref, v_ref)
        for kind in range(3):
            for h in range(NH):
                cols = slice(kind * D + h * HD, kind * D + (h + 1) * HD)
                xe = _ext_rows((cur_ref, prev_ref, next_ref), cols, keep)
                conv = _conv_rows(_shifted_rows(xe, DN_TAPS), w_ref, cols)[HALO:HALO + tb]
                outs[kind][:, h * HD:(h + 1) * HD] = _qkv_post_fn(conv, kind)
        gb_ref[...] = _gates_fn(ba_ref[...], al_ref[...], dt_ref[...])

    row = pl.BlockSpec((tb, D), lambda i: (i, 0))
    one = pl.BlockSpec((1, 128), lambda i: (0, 0))
    return pl.pallas_call(
        body, grid=(T // tb,), name="dn_pre_fwd",
        in_specs=_halo_specs(tb, 3 * D, T) + [pl.BlockSpec((tb, 128), lambda i: (i, O_BA // 128)),
                                              pl.BlockSpec((8, 3 * D), lambda i: (0, 0)), one, one],
        out_specs=[row, row, row, pl.BlockSpec((tb, 128), lambda i: (i, 0))],
        out_shape=[jax.ShapeDtypeStruct((T, D), F32)] * 3 + [jax.ShapeDtypeStruct((T, 128), F32)],
        compiler_params=_cp(),
    )(p, p, p, p, w8, alog_row, dt_row)


def _dn_pre_bwd(p, w8, alog_row, dt_row, dq, dk, dv, dgb, others, seg_rows):
    T = p.shape[0]
    tb = _tile(T, (256, 128))
    starts, ends = _segment_edges(seg_rows, tb)
    other_cols = (O_GT, O_Q, O_MG, O_K, O_V)
    assert [o.shape[1] for o in others] == [O_Q - O_GT, O_MG - O_Q, O_K - O_MG, O_V - O_K, O_BA - O_V]

    def body(cur_ref, prev_ref, next_ref, ba_ref, w_ref, al_ref, dt_ref,
             dq_c, dq_p, dq_n, dk_c, dk_p, dk_n, dv_c, dv_p, dv_n, dgb_ref, gt_ref, q_ref, mg_ref, k_ref, v_ref,
             dx_ref, dw_ref, dal_ref, ddt_ref):
        i = pl.program_id(0)
        for c0, ref in zip(other_cols, (gt_ref, q_ref, mg_ref, k_ref, v_ref)):
            dx_ref[:, c0:c0 + ref.shape[1]] = ref[...]
        dx_ref[:, O_BA + 128:] = jnp.zeros((tb, PW - O_BA - 128), BF)
        keep = _keep_halos(i, starts, ends)

        @pl.when(i == 0)
        def _():
            dw_ref[...] = jnp.zeros_like(dw_ref)
            dal_ref[...] = jnp.zeros_like(dal_ref)
            ddt_ref[...] = jnp.zeros_like(ddt_ref)

        douts = ((dq_c, dq_p, dq_n), (dk_c, dk_p, dk_n), (dv_c, dv_p, dv_n))
        for kind in range(3):
            for h in range(NH):
                cols = slice(kind * D + h * HD, kind * D + (h + 1) * HD)
                xe = _ext_rows((cur_ref, prev_ref, next_ref), cols, keep)
                shifted = _shifted_rows(xe, DN_TAPS)
                conv = _conv_rows(shifted, w_ref, cols)
                dye = _ext_rows(douts[kind], slice(h * HD, (h + 1) * HD), keep)
                _, vjp = jax.vjp(functools.partial(_qkv_post_fn, kind=kind), conv)
                dce = vjp(dye)[0]
                dx_ref[:, cols] = _conv_rows(_shifted_rows(dce, DN_TAPS, transpose=True), w_ref, cols)[HALO:HALO + tb].astype(BF)
                for j, g in enumerate(_tap_grads(dce[HALO:HALO + tb], shifted, tb)):
                    dw_ref[j:j + 1, cols] += g
        _, vjp = jax.vjp(_gates_fn, ba_ref[...], al_ref[...], dt_ref[...])
        dba, dal, ddt = vjp(dgb_ref[...])
        dx_ref[:, O_BA:O_BA + 128] = dba.astype(BF)
        dal_ref[...] += dal
        ddt_ref[...] += ddt

    one = pl.BlockSpec((1, 128), lambda i: (0, 0))
    nar = pl.BlockSpec((tb, 128), lambda i: (i, 0))
    wspec = pl.BlockSpec((8, 3 * D), lambda i: (0, 0))
    return pl.pallas_call(
        body, grid=(T // tb,), name="dn_pre_bwd",
        in_specs=_halo_specs(tb, 3 * D, T) + [pl.BlockSpec((tb, 128), lambda i: (i, O_BA // 128)), wspec, one, one]
        + _halo_specs(tb, D, T) * 3 + [nar] + [pl.BlockSpec((tb, o.shape[1]), lambda i: (i, 0)) for o in others],
        out_specs=[pl.BlockSpec((tb, PW), lambda i: (i, 0)), wspec, one, one],
        out_shape=[jax.ShapeDtypeStruct((T, PW), BF), jax.ShapeDtypeStruct((8, 3 * D), F32),
                   jax.ShapeDtypeStruct((1, 128), F32), jax.ShapeDtypeStruct((1, 128), F32)],
        compiler_params=_cp(),
    )(p, p, p, p, w8, alog_row, dt_row, dq, dq, dq, dk, dk, dk, dv, dv, dv, dgb, *others)


def _dot_hi(a, b):
    return jnp.dot(a, b, precision=HI, preferred_element_type=F32)


def _bf_product(a, b, form):
    return lax.dot_general(a.astype(BF), b.astype(BF), (_DIMS[form], ((), ())), preferred_element_type=F32)


def _dot_tn_bf(a, b):
    return _bf_product(a, b, "tn")


@jax.custom_vjp
def _dot_bf(a, b):
    return _bf_product(a, b, "nn")


@jax.custom_vjp
def _dot_nt_bf(a, b):
    return _bf_product(a, b, "nt")


_dot_bf.defvjp(lambda a, b: (_bf_product(a, b, "nn"), (a, b)),
               lambda res, dc: (_bf_product(dc, res[1], "nt").astype(res[0].dtype), _bf_product(res[0], dc, "tn").astype(res[1].dtype)))
_dot_nt_bf.defvjp(lambda a, b: (_bf_product(a, b, "nt"), (a, b)),
                  lambda res, dc: (_bf_product(dc, res[1], "nn").astype(res[0].dtype), _bf_product(dc, res[0], "tn").astype(res[1].dtype)))


def _dot_h3(a, b):
    return jnp.dot(a, b, precision=lax.Precision.HIGH, preferred_element_type=F32)


def _dot_split(fine, coarse, form):
    hi = fine.astype(BF)
    lo = (fine - hi.astype(F32)).astype(BF)
    cb = coarse.astype(BF)
    if form == "tn":
        return lax.dot_general(jnp.concatenate([cb, cb], axis=0), jnp.concatenate([hi, lo], axis=0),
                               (_DIMS["tn"], ((), ())), preferred_element_type=F32)
    parts = jnp.concatenate([hi, lo], axis=1)
    if form == "nt":
        return lax.dot_general(parts, jnp.concatenate([cb, cb], axis=1), (_DIMS["nt"], ((), ())), preferred_element_type=F32)
    return jnp.dot(parts, jnp.concatenate([cb, cb], axis=0), preferred_element_type=F32)


@jax.custom_vjp
def _mm_split(a, b):
    return _dot_split(a, b, "nn")


_mm_split.defvjp(lambda a, b: (_dot_split(a, b, "nn"), (a, b)),
                 lambda res, dc: (_dot_split(dc, res[1], "nt"), _dot_split(dc, res[0], "tn")))


def _unit_tri_inverses(mats):
    r, c = _iota2((CB, CB))
    eye = (r == c).astype(F32)
    a8 = [jnp.where((r // 8) == (c // 8), a, 0.0) for a in mats]
    a2 = [_dot_split(x, x, "nn") for x in a8]
    a4 = [_dot_split(x, x, "nn") for x in a2]
    t = [_dot_split(eye - x, eye + y, "nn") for x, y in zip(a8, a2)]
    t = [_dot_split(x, eye + y, "nn") for x, y in zip(t, a4)]
    b = 8
    while b < CB:
        mask = ((r // (2 * b)) == (c // (2 * b))) & ((r // b) != (c // b))
        te = [_dot_split(x, jnp.where(mask, a, 0.0), "nn") for x, a in zip(t, mats)]
        t = [x - _dot_split(y, x, "nn") for x, y in zip(t, te)]
        b *= 2
    return t


@jax.custom_vjp
def _saved_inverse(a, t):
    return t


_saved_inverse.defvjp(lambda a, t: (t, t),
                      lambda t, dt: (-_dot_split(_dot_split(dt, t, "nt"), t, "tn"), jnp.zeros_like(t)))


def _dn1_decay(gc, reverse):
    r, c = _iota2((CB, CB))
    incl = (c >= r) if reverse else (c <= r)
    return jnp.where(incl, jnp.exp(jnp.where(incl, gc - gc.T, 0.0)), 0.0)


def _dn1_heads(qs, ks, vs, betas, gcs, ts_saved, reverse, kks=None, qks=None):
    r, c = _iota2((CB, CB))
    strict = (c > r) if reverse else (c < r)
    decays = [_dn1_decay(gc, reverse) for gc in gcs]
    kks = kks or [_dot_nt_bf(k, k) for k in ks]
    systems = [jnp.where(strict, b * kk * dc, 0.0) for b, kk, dc in zip(betas, kks, decays)]
    if ts_saved is None:
        ts = _unit_tri_inverses(systems)
    else:
        ts = [_saved_inverse(a, t) for a, t in zip(systems, ts_saved)]
    egs = [jnp.exp(gc) for gc in gcs]
    us = [_mm_split(t, v * b) for t, v, b in zip(ts, vs, betas)]
    ws = [_mm_split(t, k * (b * eg)) for t, k, b, eg in zip(ts, ks, betas, egs)]
    qks = qks or [_dot_nt_bf(q, k) for q, k in zip(qs, ks)]
    last = 0 if reverse else CB - 1
    glogs = [jnp.sum(jnp.where(r == last, gc, 0.0), axis=0, keepdims=True) for gc in gcs]
    outs = [(u, w, q * eg, k * jnp.exp(gl - gc), qk * dc, jnp.exp(gl))
            for u, w, q, k, eg, gl, gc, qk, dc in zip(us, ws, qs, ks, egs, glogs, gcs, qks, decays)]
    return outs, ts


def _cum_matrix(upper):
    r, c = _iota2((CB, CB))
    return ((c >= r) if upper else (c <= r)).astype(F32)


def _lane_bcast(x, col):
    return jnp.broadcast_to(x[:, col:col + 1], x.shape)


_HEAD_SLICES = [slice(h * HD, (h + 1) * HD) for h in range(NH)]


def _dn1_fwd(q, k, v, gb):
    T = q.shape[0]
    nb = T // CB

    def body(q_ref, k_ref, v_ref, gb_ref, *out_refs):
        gbv = gb_ref[...]
        qs = [q_ref[:, sl] for sl in _HEAD_SLICES]
        ks = [k_ref[:, sl] for sl in _HEAD_SLICES]
        vs = [v_ref[:, sl] for sl in _HEAD_SLICES]
        kks = [_dot_nt_bf(x, x) for x in ks]
        qks = [_dot_nt_bf(x, y) for x, y in zip(qs, ks)]
        for d in (0, 1):
            u_ref, w_ref, qg_ref, kd_ref, qkd_ref, gl_ref, t_ref = out_refs[7 * d:7 * d + 7]
            gcum = _dot_h3(_cum_matrix(d == 1), gbv)
            betas = [_lane_bcast(gbv, d * NH + h) for h in range(NH)]
            gcs = [_lane_bcast(gcum, 16 + d * NH + h) for h in range(NH)]
            outs, ts = _dn1_heads(qs, ks, vs, betas, gcs, None, d == 1, kks, qks)
            for h, sl in enumerate(_HEAD_SLICES):
                u, w, qg, kd, qkd, gl = outs[h]
                u_ref[:, sl] = u
                w_ref[:, sl] = w.astype(BF)
                qg_ref[:, sl] = qg.astype(BF)
                kd_ref[:, sl] = kd.astype(BF)
                qkd_ref[:, sl] = qkd.astype(BF)
                gl_ref[h] = gl
                t_ref[:, sl] = ts[h]

    tb = pl.BlockSpec((CB, D), lambda i: (i, 0))
    one_dir_specs = [tb, tb, tb, tb, tb, pl.BlockSpec((NH, 1, 128), lambda i: (i, 0, 0)), tb]
    one_dir_shapes = ([jax.ShapeDtypeStruct((T, D), F32)] + [jax.ShapeDtypeStruct((T, D), BF)] * 4
                      + [jax.ShapeDtypeStruct((nb * NH, 1, 128), F32), jax.ShapeDtypeStruct((T, D), F32)])
    outs = pl.pallas_call(
        body, grid=(nb,), name="dn1_fwd",
        in_specs=[tb, tb, tb, pl.BlockSpec((CB, 128), lambda i: (i, 0))],
        out_specs=one_dir_specs * 2, out_shape=one_dir_shapes * 2, compiler_params=_cp(),
    )(q, k, v, gb)
    return [tuple(outs[:7]), tuple(outs[7:])]


def _dn1_bwd(q, k, v, gb, tinvs, cots):
    T = q.shape[0]
    nb = T // CB

    def body(q_ref, k_ref, v_ref, gb_ref, *refs):
        dir_refs, (dq_ref, dk_ref, dv_ref, dgb_ref) = refs[:14], refs[14:]
        gbv = gb_ref[...]
        qs = [q_ref[:, sl] for sl in _HEAD_SLICES]
        ks = [k_ref[:, sl] for sl in _HEAD_SLICES]
        vs = [v_ref[:, sl] for sl in _HEAD_SLICES]
        lane = lax.broadcasted_iota(jnp.int32, (CB, 128), 1)
        dgb = jnp.zeros((CB, 128), F32)
        for d in (0, 1):
            t_ref, du_ref, dw_ref, dqg_ref, dkd_ref, dqkd_ref, dgl_ref = dir_refs[7 * d:7 * d + 7]
            gcum = _dot_h3(_cum_matrix(d == 1), gbv)
            betas = [_lane_bcast(gbv, d * NH + h) for h in range(NH)]
            gcs = [_lane_bcast(gcum, 16 + d * NH + h) for h in range(NH)]
            ts = [t_ref[:, sl] for sl in _HEAD_SLICES]
            f = lambda qs, ks, vs, betas, gcs: _dn1_heads(qs, ks, vs, betas, gcs, ts, d == 1)[0]
            _, vjp = jax.vjp(f, qs, ks, vs, betas, gcs)
            cot = [(du_ref[:, sl], dw_ref[:, sl].astype(F32), dqg_ref[:, sl].astype(F32), dkd_ref[:, sl].astype(F32),
                    dqkd_ref[:, sl].astype(F32), dgl_ref[h])
                   for h, sl in enumerate(_HEAD_SLICES)]
            dqs, dks, dvs, dbetas, dgcs = vjp(cot)
            dgcum = jnp.zeros((CB, 128), F32)
            for h, sl in enumerate(_HEAD_SLICES):
                if d == 0:
                    dq_ref[:, sl] = dqs[h]
                    dk_ref[:, sl] = dks[h]
                    dv_ref[:, sl] = dvs[h]
                else:
                    dq_ref[:, sl] += dqs[h]
                    dk_ref[:, sl] += dks[h]
                    dv_ref[:, sl] += dvs[h]
                dgb = dgb + jnp.where(lane == d * NH + h, jnp.sum(dbetas[h], axis=1, keepdims=True), 0.0)
                dgcum = dgcum + jnp.where(lane == 16 + d * NH + h, jnp.sum(dgcs[h], axis=1, keepdims=True), 0.0)
            dgb = dgb + _dot_h3(_cum_matrix(d == 0), dgcum)
        dgb_ref[...] = dgb

    tb = pl.BlockSpec((CB, D), lambda i: (i, 0))
    gbs = pl.BlockSpec((CB, 128), lambda i: (i, 0))
    gls = pl.BlockSpec((NH, 1, 128), lambda i: (i, 0, 0))
    args = []
    for d in (0, 1):
        args += [tinvs[d], *cots[d]]
    return pl.pallas_call(
        body, grid=(nb,), name="dn1_bwd",
        in_specs=[tb, tb, tb, gbs] + [tb, tb, tb, tb, tb, tb, gls] * 2, out_specs=[tb, tb, tb, gbs],
        out_shape=[jax.ShapeDtypeStruct((T, D), F32)] * 3 + [jax.ShapeDtypeStruct((T, 128), F32)],
        compiler_params=_cp(),
    )(q, k, v, gb, *args)


SCAN_CHUNKS = 2


def _dn2_steps(chains):
    ws = [_dot_bf(w, s) for _, w, _, _, _, _, s in chains]
    v_new = [c[0] - x for c, x in zip(chains, ws)]
    o_state = [_dot_bf(c[2], c[6]) for c in chains]
    o_local = [_dot_bf(c[4], vn) for c, vn in zip(chains, v_new)]
    grow = [_dot_tn_bf(c[3], vn) for c, vn in zip(chains, v_new)]
    return [a + b for a, b in zip(o_state, o_local)], [c[6] * c[5] + g for c, g in zip(chains, grow)]


def _dn2_steps_bwd(chains, cot_o, cot_s):
    bf = lambda a: a.astype(BF)
    nt = lambda a, b: lax.dot_general(bf(a), bf(b), (_DIMS["nt"], ((), ())), preferred_element_type=F32)
    v_new = [c[0] - _dot_bf(c[1], c[6]) for c in chains]
    dv = [_dot_bf(jnp.concatenate([c[4].T, c[3]], axis=1), jnp.concatenate([do, ds], axis=0))
          for c, do, ds in zip(chains, cot_o, cot_s)]
    both = [nt(jnp.concatenate([do, x], axis=0), c[6]) for c, do, x in zip(chains, cot_o, dv)]
    dqkd = [nt(do, vn) for do, vn in zip(cot_o, v_new)]
    dkd = [nt(vn, ds) for vn, ds in zip(v_new, cot_s)]
    dstate = [_dot_bf(jnp.concatenate([c[2].T, -c[1].T], axis=1), jnp.concatenate([do, x], axis=0))
              for c, do, x in zip(chains, cot_o, dv)]
    return [(x, -b[CB:], b[:CB], dk, dq, jnp.sum(ds * c[6], axis=0, keepdims=True), ds * c[5] + g)
            for c, x, b, dk, dq, ds, g in zip(chains, dv, both, dkd, dqkd, cot_s, dstate)]


def _scan_order(direction, nlat_b, nall_b):
    if direction == 0:
        return lambda i: (i + nlat_b) % nall_b
    return lambda i: nall_b - 1 - i


def _dn2_fwd(per_dir, nlat):
    T = per_dir[0][0].shape[0]
    nb = T // CB
    assert (nlat // CB) % SCAN_CHUNKS == 0 and nb % SCAN_CHUNKS == 0
    blks = [_scan_order(d, nlat // CB // SCAN_CHUNKS, nb // SCAN_CHUNKS) for d in (0, 1)]

    def body(*refs):
        ins, outs, s_scr = refs[:12], refs[12:16], refs[16]

        @pl.when(pl.program_id(0) == 0)
        def _():
            s_scr[...] = jnp.zeros_like(s_scr)
        where = [(d, h, sl) for h, sl in enumerate(_HEAD_SLICES) for d in (0, 1)]
        for step in range(SCAN_CHUNKS):
            sub = [step, SCAN_CHUNKS - 1 - step]
            rows = [pl.ds(s * CB, CB) for s in sub]
            for d in (0, 1):
                outs[2 * d + 1][sub[d]] = s_scr[d]
            chains = []
            for d, h, sl in where:
                u_ref, w_ref, qg_ref, kd_ref, qkd_ref, gl_ref = ins[6 * d:6 * d + 6]
                chains.append((u_ref[rows[d], sl], w_ref[rows[d], sl], qg_ref[rows[d], sl], kd_ref[rows[d], sl], qkd_ref[rows[d], sl],
                               gl_ref[sub[d] * NH + h], s_scr[d, h]))
            os, states = _dn2_steps(chains)
            for (d, h, sl), o, s_next in zip(where, os, states):
                outs[2 * d][rows[d], sl] = o
                s_scr[d, h] = s_next

    in_specs, out_specs, args = [], [], []
    for d in (0, 1):
        blk = blks[d]
        tb = pl.BlockSpec((SCAN_CHUNKS * CB, D), lambda i, blk=blk: (blk(i), 0))
        in_specs += [tb] * 5 + [pl.BlockSpec((SCAN_CHUNKS * NH, 1, 128), lambda i, blk=blk: (blk(i), 0, 0))]
        out_specs += [tb, pl.BlockSpec((SCAN_CHUNKS, NH, HD, HD), lambda i, blk=blk: (blk(i), 0, 0, 0))]
        args += list(per_dir[d])
    outs = pl.pallas_call(
        body, grid=(nb // SCAN_CHUNKS,), name="dn2_fwd", in_specs=in_specs, out_specs=out_specs,
        out_shape=[jax.ShapeDtypeStruct((T, D), F32), jax.ShapeDtypeStruct((nb, NH, HD, HD), F32)] * 2,
        scratch_shapes=[pltpu.VMEM((2, NH, HD, HD), F32)], compiler_params=_cp(),
    )(*args)
    return [tuple(outs[:2]), tuple(outs[2:])]


def _dn2_bwd(per_dir, do, nlat):
    T = per_dir[0][0].shape[0]
    nb = T // CB
    nb2, nlat_b2 = nb // SCAN_CHUNKS, nlat // CB // SCAN_CHUNKS
    fwd = [_scan_order(d, nlat_b2, nb2) for d in (0, 1)]
    blks = [lambda i, f=f: f(nb2 - 1 - i) for f in fwd]

    def body(*refs):
        ins, outs, ds_scr = refs[:16], refs[16:28], refs[28]
        i = pl.program_id(0)

        @pl.when(i == 0)
        def _():
            ds_scr[...] = jnp.zeros_like(ds_scr)
        where = [(d, h, sl) for h, sl in enumerate(_HEAD_SLICES) for d in (0, 1)]
        for step in range(SCAN_CHUNKS):
            sub = [SCAN_CHUNKS - 1 - step, step]
            rows = [pl.ds(s * CB, CB) for s in sub]
            chains, cot_o, cot_s = [], [], []
            for d, h, sl in where:
                u_ref, w_ref, qg_ref, kd_ref, qkd_ref, gl_ref, sall_ref, do_ref = ins[8 * d:8 * d + 8]
                r = rows[d]
                chains.append((u_ref[r, sl], w_ref[r, sl].astype(F32), qg_ref[r, sl].astype(F32), kd_ref[r, sl].astype(F32),
                               qkd_ref[r, sl].astype(F32), gl_ref[sub[d] * NH + h], sall_ref[sub[d], h]))
                cot_o.append(jnp.where(blks[d](i) < nlat_b2, do_ref[r, sl], 0.0))
                cot_s.append(ds_scr[d, h])
            for (d, h, sl), (du, dw, dqg, dkd, dqkd, dgl, ds) in zip(where, _dn2_steps_bwd(chains, cot_o, cot_s)):
                du_ref, dw_ref, dqg_ref, dkd_ref, dqkd_ref, dgl_ref = outs[6 * d:6 * d + 6]
                r = rows[d]
                du_ref[r, sl] = du
                dw_ref[r, sl] = dw.astype(BF)
                dqg_ref[r, sl] = dqg.astype(BF)
                dkd_ref[r, sl] = dkd.astype(BF)
                dqkd_ref[r, sl] = dqkd.astype(BF)
                dgl_ref[sub[d] * NH + h] = dgl
                ds_scr[d, h] = ds

    in_specs, out_specs, args = [], [], []
    for d in (0, 1):
        blk = blks[d]
        tb = pl.BlockSpec((SCAN_CHUNKS * CB, D), lambda i, blk=blk: (blk(i), 0))
        gls = pl.BlockSpec((SCAN_CHUNKS * NH, 1, 128), lambda i, blk=blk: (blk(i), 0, 0))
        in_specs += [tb] * 5 + [gls, pl.BlockSpec((SCAN_CHUNKS, NH, HD, HD), lambda i, blk=blk: (blk(i), 0, 0, 0)),
                                pl.BlockSpec((SCAN_CHUNKS * CB, D), lambda i, blk=blk: (jnp.minimum(blk(i), nlat_b2 - 1), 0))]
        out_specs += [tb] * 5 + [gls]
        args += list(per_dir[d]) + [do]
    outs = pl.pallas_call(
        body, grid=(nb2,), name="dn2_bwd", in_specs=in_specs, out_specs=out_specs,
        out_shape=([jax.ShapeDtypeStruct((T, D), F32)] + [jax.ShapeDtypeStruct((T, D), BF)] * 4
                   + [jax.ShapeDtypeStruct((nb * NH, 1, 128), F32)]) * 2,
        scratch_shapes=[pltpu.VMEM((2, NH, HD, HD), F32)], compiler_params=_cp(),
    )(*args)
    return [tuple(outs[:6]), tuple(outs[6:])]


def _ghn_fn(o, gt, w):
    y = o * lax.rsqrt(jnp.mean(o * o, axis=-1, keepdims=True) + EPS)
    return (y * w) * jax.nn.silu(gt)


def _ghn_fwd(o_f, o_b, p, w, w_branch, nlat):
    tb = _tile(nlat, (512, 256, 128))

    def body(of_ref, ob_ref, gt_ref, w_ref, wb_ref, y_ref, z_ref):
        for h in range(NH):
            sl = slice(h * HD, (h + 1) * HD)
            y_ref[:, sl] = _ghn_fn(of_ref[:, sl] + ob_ref[:, sl], gt_ref[:, sl], w_ref[...]).astype(BF)
        z_ref[...] = jnp.dot(y_ref[...], wb_ref[...], preferred_element_type=F32)

    row = pl.BlockSpec((tb, D), lambda i: (i, 0))
    return pl.pallas_call(
        body, grid=(nlat // tb,), name="ghn_fwd",
        in_specs=[row, row, pl.BlockSpec((tb, D), lambda i: (i, O_GT // D)), pl.BlockSpec((1, HD), lambda i: (0, 0)), _resident((D, D))],
        out_specs=[row, row], out_shape=[jax.ShapeDtypeStruct((nlat, D), BF), jax.ShapeDtypeStruct((nlat, D), F32)],
        compiler_params=_cp(),
    )(o_f, o_b, p, w, w_branch)


def _ghn_bwd(o_f, o_b, p, w, dy, nlat):
    T = p.shape[0]
    tb = _tile(nlat, (256, 128))
    nlb = nlat // tb

    def body(of_ref, ob_ref, gt_ref, w_ref, dy_ref, do_ref, dgt_ref, dw_ref):
        is_lat = pl.program_id(0) < nlb

        @pl.when(pl.program_id(0) == 0)
        def _():
            dw_ref[...] = jnp.zeros_like(dw_ref)
        for h in range(NH):
            sl = slice(h * HD, (h + 1) * HD)
            _, vjp = jax.vjp(_ghn_fn, of_ref[:, sl] + ob_ref[:, sl], gt_ref[:, sl], w_ref[...])
            do, dgt, dw = vjp(dy_ref[:, sl])
            do_ref[:, sl] = do
            dgt_ref[:, sl] = jnp.where(is_lat, dgt, 0.0).astype(BF)
            dw_ref[...] += jnp.where(is_lat, dw, 0.0)

    lat = lambda i: jnp.minimum(i, nlb - 1)
    row = pl.BlockSpec((tb, D), lambda i: (lat(i), 0))
    one = pl.BlockSpec((1, HD), lambda i: (0, 0))
    return pl.pallas_call(
        body, grid=(T // tb,), name="ghn_bwd",
        in_specs=[row, row, pl.BlockSpec((tb, D), lambda i: (lat(i), O_GT // D)), one, row],
        out_specs=[row, pl.BlockSpec((tb, D), lambda i: (i, 0)), one],
        out_shape=[jax.ShapeDtypeStruct((nlat, D), F32), jax.ShapeDtypeStruct((T, D), BF), jax.ShapeDtypeStruct((1, HD), F32)],
    )(o_f, o_b, p, w, dy)


@jax.custom_vjp
def _swap32(x):
    lane = lax.broadcasted_iota(jnp.int32, x.shape, 1)
    return jnp.where((lane & 32) == 0, pltpu.roll(x, 96, 1), pltpu.roll(x, 32, 1))


_swap32.defvjp(lambda x: (_swap32(x), None), lambda _, g: (_swap32(g),))


def _qk_post_fn(xs, w, cos, sin):
    inv = [lax.rsqrt(jnp.mean(x * x, axis=-1, keepdims=True) + EPS) for x in xs]
    ys = [(x * r) * w for x, r in zip(xs, inv)]
    return [y * cos + _swap32(y) * sin for y in ys]


def _attn_prep_fwd(p, qn, kn, cos, sin):
    T = p.shape[0]
    tb = _tile(T, (256, 128))

    def body(q_ref, k_ref, v_ref, qn_ref, kn_ref, cos_ref, sin_ref, qr_ref, kr_ref, vb_ref):
        cos_v, sin_v = cos_ref[...], sin_ref[...]
        for sl, y in zip(_HEAD_SLICES, _qk_post_fn([q_ref[:, sl] for sl in _HEAD_SLICES], qn_ref[...], cos_v, sin_v)):
            qr_ref[:, sl] = y.astype(BF)
        for sl, y in zip(_HEAD_SLICES, _qk_post_fn([k_ref[:, sl] for sl in _HEAD_SLICES[:KVH]], kn_ref[...], cos_v, sin_v)):
            kr_ref[:, sl] = y.astype(BF)
        vb_ref[...] = v_ref[...].astype(BF)

    one = pl.BlockSpec((1, HD), lambda i: (0, 0))
    tab = pl.BlockSpec((tb, HD), lambda i: (i, 0))
    return pl.pallas_call(
        body, grid=(T // tb,), name="attn_prep_fwd",
        in_specs=[pl.BlockSpec((tb, D), lambda i: (i, O_Q // D)), pl.BlockSpec((tb, KV), lambda i: (i, O_K // KV)),
                  pl.BlockSpec((tb, KV), lambda i: (i, O_V // KV)), one, one, tab, tab],
        out_specs=[pl.BlockSpec((tb, D), lambda i: (i, 0)), pl.BlockSpec((tb, KV), lambda i: (i, 0)),
                   pl.BlockSpec((tb, KV), lambda i: (i, 0))],
        out_shape=[jax.ShapeDtypeStruct((T, D), BF), jax.ShapeDtypeStruct((T, KV), BF), jax.ShapeDtypeStruct((T, KV), BF)],
    )(p, p, p, qn, kn, cos, sin)


def _attn_prep_bwd(p, qn, kn, cos, sin, dqr, dkp, dvp, dkc, dvc, nlat):
    T = p.shape[0]
    nqb = nlat // CB
    ncb = (T - nlat) // CB

    def body(q_ref, k_ref, v_ref, qn_ref, kn_ref, cos_ref, sin_ref, dqr_ref, dka_ref, dkb_ref, dkc3_ref, dva_ref, dvb_ref, dvc3_ref,
             dkctx_ref, dvctx_ref, dq_ref, dk_ref, dv_ref, dqn_ref, dkn_ref):
        i = pl.program_id(0)
        is_lat = i < nqb
        cos_v, sin_v = cos_ref[...], sin_ref[...]

        @pl.when(i == 0)
        def _():
            dqn_ref[...] = jnp.zeros_like(dqn_ref)
            dkn_ref[...] = jnp.zeros_like(dkn_ref)

        def band_sum(a_ref, b_ref, c_ref, ctx_ref):
            s = b_ref[0] + jnp.where(i > 0, a_ref[0], 0.0) + jnp.where(i < nqb - 1, c_ref[0], 0.0)
            return jnp.where(is_lat, s, ctx_ref[...])

        dkr = band_sum(dka_ref, dkb_ref, dkc3_ref, dkctx_ref)
        dv_ref[...] = band_sum(dva_ref, dvb_ref, dvc3_ref, dvctx_ref).astype(BF)
        post = lambda xs, w: _qk_post_fn(xs, w, cos_v, sin_v)
        _, vjp = jax.vjp(post, [q_ref[:, sl] for sl in _HEAD_SLICES], qn_ref[...])
        dqs, dqn = vjp([jnp.where(is_lat, dqr_ref[:, sl], 0.0) for sl in _HEAD_SLICES])
        for sl, dq in zip(_HEAD_SLICES, dqs):
            dq_ref[:, sl] = dq.astype(BF)
        dqn_ref[...] += dqn
        _, vjp = jax.vjp(post, [k_ref[:, sl] for sl in _HEAD_SLICES[:KVH]], kn_ref[...])
        dks, dkn = vjp([dkr[:, sl] for sl in _HEAD_SLICES[:KVH]])
        for sl, dk in zip(_HEAD_SLICES, dks):
            dk_ref[:, sl] = dk.astype(BF)
        dkn_ref[...] += dkn

    one = pl.BlockSpec((1, HD), lambda i: (0, 0))
    tab = pl.BlockSpec((CB, HD), lambda i: (i, 0))
    lat = lambda i: jnp.minimum(i, nqb - 1)

    def part(off, slot):
        return pl.BlockSpec((1, CB, KV), lambda i: (jnp.clip(lat(i) + off, 0, nqb - 1) * 3 + slot, 0, 0))

    ctxs = pl.BlockSpec((CB, KV), lambda i: (jnp.clip(i - nqb, 0, ncb - 1), 0))
    kvs = pl.BlockSpec((CB, KV), lambda i: (i, 0))
    return pl.pallas_call(
        body, grid=(T // CB,), name="attn_prep_bwd",
        in_specs=[pl.BlockSpec((CB, D), lambda i: (i, O_Q // D)), pl.BlockSpec((CB, KV), lambda i: (i, O_K // KV)),
                  pl.BlockSpec((CB, KV), lambda i: (i, O_V // KV)), one, one, tab, tab,
                  pl.BlockSpec((CB, D), lambda i: (lat(i), 0)),
                  part(-1, 2), part(0, 1), part(1, 0), part(-1, 2), part(0, 1), part(1, 0), ctxs, ctxs],
        out_specs=[pl.BlockSpec((CB, D), lambda i: (i, 0)), kvs, kvs, one, one],
        out_shape=[jax.ShapeDtypeStruct((T, D), BF), jax.ShapeDtypeStruct((T, KV), BF), jax.ShapeDtypeStruct((T, KV), BF),
                   jax.ShapeDtypeStruct((1, HD), F32), jax.ShapeDtypeStruct((1, HD), F32)],
    )(p, p, p, qn, kn, cos, sin, dqr, dkp, dkp, dkp, dvp, dvp, dvp, dkc, dvc)


def _attn_groups_fn(qs, kalls, valls, sinks, bias):
    groups = range(KVH)
    q = [jnp.concatenate(qs[GRP * g:GRP * (g + 1)], axis=0) for g in groups]
    s = [_bf_product(q[g], kalls[g], "nt") * (HD ** -0.5) + bias for g in groups]
    sk = [jnp.concatenate([jnp.broadcast_to(jnp.mean(t, axis=1, keepdims=True), (CB, 1)) for t in sinks[GRP * g:GRP * (g + 1)]],
                          axis=0) for g in groups]
    m = [lax.stop_gradient(jnp.maximum(jnp.max(s[g], axis=1, keepdims=True), sk[g])) for g in groups]
    e = [jnp.exp(s[g] - m[g]) for g in groups]
    den = [jnp.sum(e[g], axis=1, keepdims=True) + jnp.exp(sk[g] - m[g]) for g in groups]
    return [_bf_product(e[g] / den[g], valls[g], "nn") for g in groups]


def _attn_bias(lc):
    r, c = _iota2((GRP * CB, 3 * CB + lc))
    rel = c - (r & (CB - 1))
    win = (rel >= 0) & (rel <= 2 * CB)
    ctx = c >= 3 * CB
    seen = [(win & (c >= CB)) | ctx, win | ctx, (win & (c < 2 * CB)) | ctx]
    return jnp.stack([jnp.where(s, 0.0, -1e30) for s in seen]).astype(F32)


def _attn_specs(nqb, lc, nlat):
    assert nqb >= 2
    qs = pl.BlockSpec((CB, D), lambda i: (i, 0))
    ka = pl.BlockSpec((CB, KV), lambda i: (jnp.maximum(i - 1, 0), 0))
    kb = pl.BlockSpec((CB, KV), lambda i: (i, 0))
    kc = pl.BlockSpec((CB, KV), lambda i: (jnp.minimum(i + 1, nqb - 1), 0))
    kx = pl.BlockSpec((lc, KV), lambda i: (nlat // lc, 0))
    sk = pl.BlockSpec((KVH, 8, 128), lambda i: (0, 0, 0))
    bs = pl.BlockSpec((1, GRP * CB, 3 * CB + lc), lambda i: (jnp.where(i == 0, 0, jnp.where(i == nqb - 1, 2, 1)), 0, 0))
    return qs, ka, kb, kc, kx, sk, bs


def _attn_operands(q_ref, k_refs, v_refs, sk_ref, dtype):
    sls = [slice(g * HD, (g + 1) * HD) for g in range(KVH)]
    kalls = [jnp.concatenate([r[:, sl] for r in k_refs], axis=0).astype(dtype) for sl in sls]
    valls = [jnp.concatenate([r[:, sl] for r in v_refs], axis=0).astype(dtype) for sl in sls]
    qs = [q_ref[:, sl].astype(dtype) for sl in _HEAD_SLICES]
    sinks = [sk_ref[h // GRP, (h % GRP):(h % GRP) + 1, :] for h in range(NH)]
    return qs, kalls, valls, sinks


def _attn_fwd(qr, kr, vb, sink, w_branch, nlat):
    lc = kr.shape[0] - nlat
    nqb = nlat // CB
    qs, ka, kb, kc, kx, sk, bs = _attn_specs(nqb, lc, nlat)

    def body(q_ref, ka_ref, kb_ref, kc_ref, kx_ref, va_ref, vb_ref, vc_ref, vx_ref, sk_ref, bias_ref, wb_ref, o_ref, z_ref):
        operands = _attn_operands(q_ref, (ka_ref, kb_ref, kc_ref, kx_ref), (va_ref, vb_ref, vc_ref, vx_ref), sk_ref, BF)
        outs = _attn_groups_fn(*operands, bias_ref[0])
        for h, sl in enumerate(_HEAD_SLICES):
            o_ref[:, sl] = outs[h // GRP][(h % GRP) * CB:(h % GRP + 1) * CB].astype(BF)
        z_ref[...] = jnp.dot(o_ref[...], wb_ref[...], preferred_element_type=F32)

    return pl.pallas_call(
        body, grid=(nqb,), name="attn_fwd",
        in_specs=[qs, ka, kb, kc, kx, ka, kb, kc, kx, sk, bs, _resident((D, D))], out_specs=[qs, qs],
        out_shape=[jax.ShapeDtypeStruct((nlat, D), BF), jax.ShapeDtypeStruct((nlat, D), F32)], compiler_params=_cp(),
    )(qr, kr, kr, kr, kr, vb, vb, vb, vb, sink, _attn_bias(lc), w_branch)


def _attn_bwd(qr, kr, vb, sink, dy, nlat):
    lc = kr.shape[0] - nlat
    nqb = nlat // CB
    qs, ka, kb, kc, kx, sk, bs = _attn_specs(nqb, lc, nlat)

    def body(q_ref, ka_ref, kb_ref, kc_ref, kx_ref, va_ref, vb_ref, vc_ref, vx_ref, sk_ref, dy_ref, bias_ref,
             dq_ref, dkp_ref, dvp_ref, dkx_ref, dvx_ref, dsk_ref):
        operands = _attn_operands(q_ref, (ka_ref, kb_ref, kc_ref, kx_ref), (va_ref, vb_ref, vc_ref, vx_ref), sk_ref, F32)
        _, vjp = jax.vjp(functools.partial(_attn_groups_fn, bias=bias_ref[0]), *operands)
        dys_g = [jnp.concatenate([dy_ref[:, sl] for sl in _HEAD_SLICES[GRP * g:GRP * (g + 1)]], axis=0) for g in range(KVH)]
        dqs, dks, dvs, dsinks = vjp(dys_g)

        @pl.when(pl.program_id(0) == 0)
        def _():
            dkx_ref[...] = jnp.zeros_like(dkx_ref)
            dvx_ref[...] = jnp.zeros_like(dvx_ref)
            dsk_ref[...] = jnp.zeros_like(dsk_ref)

        for h, sl in enumerate(_HEAD_SLICES):
            dq_ref[:, sl] = dqs[h]
            dsk_ref[h // GRP, (h % GRP):(h % GRP) + 1, :] += dsinks[h]
        for g in range(KVH):
            sl = slice(g * HD, (g + 1) * HD)
            for t in range(3):
                dkp_ref[t, :, sl] = dks[g][t * CB:(t + 1) * CB]
                dvp_ref[t, :, sl] = dvs[g][t * CB:(t + 1) * CB]
            dkx_ref[:, sl] += dks[g][3 * CB:]
            dvx_ref[:, sl] += dvs[g][3 * CB:]

    dys = qs
    parts = pl.BlockSpec((3, CB, KV), lambda i: (i, 0, 0))
    ctxo = pl.BlockSpec((lc, KV), lambda i: (0, 0))
    return pl.pallas_call(
        body, grid=(nqb,), name="attn_bwd",
        in_specs=[qs, ka, kb, kc, kx, ka, kb, kc, kx, sk, dys, bs],
        out_specs=[dys, parts, parts, ctxo, ctxo, sk],
        out_shape=[jax.ShapeDtypeStruct((nlat, D), F32), jax.ShapeDtypeStruct((3 * nqb, CB, KV), F32),
                   jax.ShapeDtypeStruct((3 * nqb, CB, KV), F32), jax.ShapeDtypeStruct((lc, KV), F32),
                   jax.ShapeDtypeStruct((lc, KV), F32), jax.ShapeDtypeStruct((KVH, 8, 128), F32)],
        compiler_params=_cp(),
    )(qr, kr, kr, kr, kr, vb, vb, vb, vb, sink, dy, _attn_bias(lc))


def _merge_fn(z_dn, z_at, g_dn, g_at):
    return jax.nn.sigmoid(g_dn) * z_dn + jax.nn.sigmoid(g_at) * z_at


def _merge_fwd(z_dn, z_at, p, w_out, nlat):
    tb = _tile(nlat, (512, 256, 128))

    def body(zd_ref, za_ref, gd_ref, ga_ref, wo_ref, o_ref, mix_ref):
        o_ref[...] = _merge_fn(zd_ref[...], za_ref[...], gd_ref[...], ga_ref[...]).astype(BF)
        mix_ref[...] = jnp.dot(o_ref[...], wo_ref[...], preferred_element_type=F32)

    row = pl.BlockSpec((tb, D), lambda i: (i, 0))
    return pl.pallas_call(
        body, grid=(nlat // tb,), name="merge_fwd",
        in_specs=[row, row, pl.BlockSpec((tb, D), lambda i: (i, O_MG // D)), pl.BlockSpec((tb, D), lambda i: (i, O_MG // D + 1)),
                  _resident((D, D))],
        out_specs=[row, row], out_shape=[jax.ShapeDtypeStruct((nlat, D), BF), jax.ShapeDtypeStruct((nlat, D), F32)],
        compiler_params=_cp(),
    )(z_dn, z_at, p, p, w_out)


def _merge_bwd(z_dn, z_at, p, dm, w_bdn, w_bat, nlat):
    T = p.shape[0]
    tb = _tile(nlat, (256, 128))
    nlb = nlat // tb

    def body(zd_ref, za_ref, gd_ref, ga_ref, dm_ref, wd_ref, wa_ref, dzd_ref, dza_ref, dg_ref, dyd_ref, dya_ref):
        is_lat = pl.program_id(0) < nlb
        _, vjp = jax.vjp(_merge_fn, zd_ref[...], za_ref[...], gd_ref[...], ga_ref[...])
        dzd, dza, dgd, dga = vjp(dm_ref[...])
        dzd_ref[...] = dzd.astype(BF)
        dza_ref[...] = dza.astype(BF)
        dg_ref[:, :D] = jnp.where(is_lat, dgd, 0.0).astype(BF)
        dg_ref[:, D:] = jnp.where(is_lat, dga, 0.0).astype(BF)
        dyd_ref[...] = lax.dot_general(dzd_ref[...], wd_ref[...], (_DIMS["nt"], ((), ())), preferred_element_type=F32)
        dya_ref[...] = lax.dot_general(dza_ref[...], wa_ref[...], (_DIMS["nt"], ((), ())), preferred_element_type=F32)

    lat = lambda i: jnp.minimum(i, nlb - 1)
    row = pl.BlockSpec((tb, D), lambda i: (lat(i), 0))
    return pl.pallas_call(
        body, grid=(T // tb,), name="merge_bwd",
        in_specs=[row, row, pl.BlockSpec((tb, D), lambda i: (lat(i), O_MG // D)),
                  pl.BlockSpec((tb, D), lambda i: (lat(i), O_MG // D + 1)), row, _resident((D, D)), _resident((D, D))],
        out_specs=[row, row, pl.BlockSpec((tb, 2 * D), lambda i: (i, 0)), row, row],
        out_shape=[jax.ShapeDtypeStruct((nlat, D), BF), jax.ShapeDtypeStruct((nlat, D), BF), jax.ShapeDtypeStruct((T, 2 * D), BF),
                   jax.ShapeDtypeStruct((nlat, D), F32), jax.ShapeDtypeStruct((nlat, D), F32)],
    )(z_dn, z_at, p, p, dm, w_bdn, w_bat)


def _swiglu_fn(ug, uv):
    return jax.nn.silu(ug) * uv


FFN_GROUP = 256


def _resident(shape):
    return pl.BlockSpec(shape, lambda i: (0,) * len(shape), pipeline_mode=pl.Buffered(1))


H_HALO = 16


def _up_project(h_refs, wu_ref, u_scr):
    cur_ref, prev_ref, next_ref = h_refs
    rows = jnp.concatenate([prev_ref[...], cur_ref[...], next_ref[...]], axis=0)
    u_scr[...] = jnp.dot(rows, wu_ref[...], preferred_element_type=F32)


def _up_ext_rows(u_scr, cols, keep, tb):
    xe = u_scr[H_HALO - HALO:H_HALO + tb + HALO, cols]
    r = lax.broadcasted_iota(jnp.int32, (tb + 2 * HALO, 1), 0)
    inside = ((r >= HALO) | keep[0]) & ((r < HALO + tb) | keep[1])
    return jnp.where(inside, xe, 0.0)


def _ffn_fwd(h, w_up, w8, bias, w_down):
    n = h.shape[0]
    tb = _tile(n, (256, 128))
    starts, ends = _segment_edges((n,), tb)

    def body(cur_ref, prev_ref, next_ref, wu_ref, w_ref, b_ref, wd_ref, u_ref, o_ref, ff_ref, u_scr):
        keep = _keep_halos(pl.program_id(0), starts, ends)
        _up_project((cur_ref, prev_ref, next_ref), wu_ref, u_scr)
        u_ref[...] = u_scr[H_HALO:H_HALO + tb, :]

        for c0 in range(0, DFF, FFN_GROUP):
            halves = []
            for cols in (slice(c0, c0 + FFN_GROUP), slice(DFF + c0, DFF + c0 + FFN_GROUP)):
                xe = _up_ext_rows(u_scr, cols, keep, tb)
                halves.append(_conv_rows(_shifted_rows(xe, FFN_TAPS), w_ref, cols)[HALO:HALO + tb] + b_ref[:, cols])
            o_ref[:, c0:c0 + FFN_GROUP] = _swiglu_fn(*halves).astype(BF)
        ff_ref[...] = jnp.dot(o_ref[...], wd_ref[...], preferred_element_type=F32)

    return pl.pallas_call(
        body, grid=(n // tb,), name="ffn_fwd",
        in_specs=_halo_specs(tb, D, n, halo=H_HALO) + [_resident((D, 2 * DFF)), pl.BlockSpec((8, 2 * DFF), lambda i: (0, 0)),
                                                        pl.BlockSpec((1, 2 * DFF), lambda i: (0, 0)), _resident((DFF, D))],
        out_specs=[pl.BlockSpec((tb, 2 * DFF), lambda i: (i, 0)), pl.BlockSpec((tb, DFF), lambda i: (i, 0)),
                   pl.BlockSpec((tb, D), lambda i: (i, 0))],
        out_shape=[jax.ShapeDtypeStruct((n, 2 * DFF), F32), jax.ShapeDtypeStruct((n, DFF), BF), jax.ShapeDtypeStruct((n, D), F32)],
        scratch_shapes=[pltpu.VMEM((tb + 2 * H_HALO, 2 * DFF), F32)],
        compiler_params=_cp(),
    )(h, h, h, w_up, w8, bias, w_down)


def _ffn_bwd(u, w_up, w8, bias, da):
    n = u.shape[0]
    tb = _tile(n, (256, 128))
    starts, ends = _segment_edges((n,), tb)

    def body(cur_ref, prev_ref, next_ref, wu_ref, w_ref, b_ref, da_c, da_p, da_n, du_ref, dw_ref, db_ref, dh_ref):
        i = pl.program_id(0)
        keep = _keep_halos(i, starts, ends)

        @pl.when(i == 0)
        def _():
            dw_ref[...] = jnp.zeros_like(dw_ref)
            db_ref[...] = jnp.zeros_like(db_ref)

        for c0 in range(0, DFF, FFN_GROUP):
            col_pair = (slice(c0, c0 + FFN_GROUP), slice(DFF + c0, DFF + c0 + FFN_GROUP))
            shifts = [_shifted_rows(_ext_rows((cur_ref, prev_ref, next_ref), cols, keep), FFN_TAPS) for cols in col_pair]
            convs = [_conv_rows(shifted, w_ref, cols) + b_ref[:, cols] for shifted, cols in zip(shifts, col_pair)]
            dae = _ext_rows((da_c, da_p, da_n), col_pair[0], keep)
            _, vjp = jax.vjp(_swiglu_fn, *convs)
            for shifted, cols, dce in zip(shifts, col_pair, vjp(dae)):
                du_ref[:, cols] = _conv_rows(_shifted_rows(dce, FFN_TAPS, transpose=True), w_ref, cols)[HALO:HALO + tb].astype(BF)
                dcur = dce[HALO:HALO + tb]
                for j, g in enumerate(_tap_grads(dcur, shifted, tb)):
                    dw_ref[j:j + 1, cols] += g
                db_ref[:, cols] += jnp.sum(dcur, axis=0, keepdims=True)
        dh_ref[...] = lax.dot_general(du_ref[...], wu_ref[...], (_DIMS["nt"], ((), ())), preferred_element_type=F32)

    wspec = pl.BlockSpec((8, 2 * DFF), lambda i: (0, 0))
    bspec = pl.BlockSpec((1, 2 * DFF), lambda i: (0, 0))
    return pl.pallas_call(
        body, grid=(n // tb,), name="ffn_bwd",
        in_specs=_halo_specs(tb, 2 * DFF, n) + [_resident((D, 2 * DFF)), wspec, bspec] + _halo_specs(tb, DFF, n),
        out_specs=[pl.BlockSpec((tb, 2 * DFF), lambda i: (i, 0)), wspec, bspec, pl.BlockSpec((tb, D), lambda i: (i, 0))],
        out_shape=[jax.ShapeDtypeStruct((n, 2 * DFF), BF), jax.ShapeDtypeStruct((8, 2 * DFF), F32), jax.ShapeDtypeStruct((1, 2 * DFF), F32),
                   jax.ShapeDtypeStruct((n, D), F32)],
        compiler_params=_cp(),
    )(u, u, u, w_up, w8, bias, da, da, da)


def _loss_kernel(x1, gate, ff, target, w_down):
    n = x1.shape[0]
    tb = _tile(n, (512, 256, 128))

    def body(x_ref, g_ref, f_ref, t_ref, wd_ref, loss_ref, dy_ref, dff_ref, dg_ref, da_ref):
        err = x_ref[...] + g_ref[...] * f_ref[...] - t_ref[...]
        dy = err * (1.0 / D)
        dy_ref[...] = dy
        dff_ref[...] = (g_ref[...] * dy).astype(BF)
        da_ref[...] = lax.dot_general(dff_ref[...], wd_ref[...], (_DIMS["nt"], ((), ())), preferred_element_type=F32)

        @pl.when(pl.program_id(0) == 0)
        def _():
            loss_ref[...] = jnp.zeros_like(loss_ref)
            dg_ref[...] = jnp.zeros_like(dg_ref)
        part = 0.5 * jnp.sum(jnp.sum(err * err, axis=1, keepdims=True) * (1.0 / D), axis=0, keepdims=True)
        loss_ref[...] += jnp.broadcast_to(part, (1, 128))
        dg_ref[...] += jnp.sum(dy * f_ref[...], axis=0, keepdims=True)

    row = pl.BlockSpec((tb, D), lambda i: (i, 0))
    one = pl.BlockSpec((1, D), lambda i: (0, 0))
    return pl.pallas_call(
        body, grid=(n // tb,), name="loss",
        in_specs=[row, one, row, row, _resident((DFF, D))],
        out_specs=[pl.BlockSpec((1, 128), lambda i: (0, 0)), row, row, one, pl.BlockSpec((tb, DFF), lambda i: (i, 0))],
        out_shape=[jax.ShapeDtypeStruct((1, 128), F32), jax.ShapeDtypeStruct((n, D), F32),
                   jax.ShapeDtypeStruct((n, D), BF), jax.ShapeDtypeStruct((1, D), F32), jax.ShapeDtypeStruct((n, DFF), F32)],
        compiler_params=_cp(),
    )(x1, gate, ff, target, w_down)


def _rope_tables(nlat, lc):
    inv_freq = (np.float32(ROPE_BASE) ** (-np.arange(32, dtype=np.float32) / np.float32(32))).astype(np.float32)
    ar = np.arange(nlat // GRID_W, dtype=np.float32)[:, None] * inv_freq
    ac = np.arange(GRID_W, dtype=np.float32)[:, None] * inv_freq
    by_row = lambda a: jnp.repeat(jnp.asarray(a, F32), GRID_W, axis=0)
    by_col = lambda a: jnp.tile(jnp.asarray(a, F32), (nlat // GRID_W, 1))
    cos = jnp.concatenate([by_row(np.cos(ar)), by_row(np.cos(ar)), by_col(np.cos(ac)), by_col(np.cos(ac))], axis=1)
    sin = jnp.concatenate([by_row(-np.sin(ar)), by_row(np.sin(ar)), by_col(-np.sin(ac)), by_col(np.sin(ac))], axis=1)
    cos = jnp.concatenate([cos, jnp.ones((lc, HD), F32)], axis=0)
    sin = jnp.concatenate([sin, jnp.zeros((lc, HD), F32)], axis=0)
    return cos, sin


def _pad_rows8(w):
    return jnp.concatenate([w, jnp.zeros((8 - w.shape[0], w.shape[1]), w.dtype)], axis=0)


def _pack_w_in(w):
    cuts = [sum(IN_SIZES[:i]) for i in range(len(IN_SIZES) + 1)]
    qkv, gt, b, a, q, k, v, mg = [w[:, cuts[i]:cuts[i + 1]] for i in range(len(IN_SIZES))]
    return jnp.concatenate([qkv, gt, q, mg, k, v, b, a, jnp.zeros((w.shape[0], PW - O_BA - 32), w.dtype)], axis=1)


def _unpack_w_in(g):
    return jnp.concatenate([g[:, O_QKV:O_GT], g[:, O_GT:O_Q], g[:, O_BA:O_BA + 32], g[:, O_Q:O_MG], g[:, O_K:O_V],
                            g[:, O_V:O_BA], g[:, O_MG:O_K]], axis=1)


def _local_step(x, ctx, mod_x, mod_c, target, project_in, project_back,
                norm_mix, norm_ffn, dn_conv, a_log, dt_bias, dn_norm, q_norm, k_norm, sink, ffn_conv, ffn_conv_b):
    L, LC = x.shape[0], ctx.shape[0]
    T = L + LC
    seg = lambda r: jnp.stack([mod_x[r], mod_c[r]])[:, None, :]
    sh_a, sc_a = seg(0), seg(1)
    g_a, g_f = mod_x[2][None], mod_x[5][None]
    sh_f, sc_f = mod_x[3][None], mod_x[4][None]
    cos, sin = _rope_tables(L, LC)
    dnc8 = _pad_rows8(dn_conv)
    ffc8 = _pad_rows8(ffn_conv)
    gate_row = lambda a: jnp.concatenate([jnp.zeros((1, 16), F32), a.reshape(1, 16), jnp.zeros((1, 96), F32)], axis=1)
    alog_row, dt_row = gate_row(a_log), gate_row(dt_bias)
    sinkb = jnp.concatenate([jnp.broadcast_to(sink.reshape(KVH, GRP, 1), (KVH, GRP, 128)), jnp.zeros((KVH, 8 - GRP, 128), F32)], axis=1)

    h1 = _norm_mod_fwd(x, ctx, norm_mix, sh_a, sc_a, "norm_mix_fwd")
    p, (w_in_p, w_bdn, w_bat, w_out, w_up, w_down) = project_in(h1)
    q, k, v, gb = _dn_pre_fwd(p, dnc8, alog_row, dt_row, (L, LC))
    wy = _dn1_fwd(q, k, v, gb)
    scans = _dn2_fwd([t[:6] for t in wy], L)
    o_dir = [s[0] for s in scans]
    y_dn, z_dn = _ghn_fwd(o_dir[0], o_dir[1], p, dn_norm, w_bdn, L)
    qr, kr, vb = _attn_prep_fwd(p, q_norm, k_norm, cos, sin)
    y_at, z_at = _attn_fwd(qr, kr, vb, sinkb, w_bat, L)
    merged, mix = _merge_fwd(z_dn, z_at, p, w_out, L)
    x1, h2 = _resid_norm_fwd(x, g_a, mix, norm_ffn, sh_f, sc_f)
    u_raw, act, ff = _ffn_fwd(h2, w_up, ffc8, ffn_conv_b, w_down)
    loss_row, dy, dff, dg_f, dact = _loss_kernel(x1, g_f, ff, target, w_down)

    g_down = _mm(act, dff, form="tn", out_dtype=BF, name="g_ffn_down")
    du_raw, g_ffc8, g_ffb, dh2 = _ffn_bwd(u_raw, w_up, ffc8, ffn_conv_b, dact)
    g_up = _mm(h2, du_raw, form="tn", out_dtype=BF, name="g_ffn_up")
    dx1, dmix, dg_a, g_nffn, dsh_f, dsc_f, dmerged = _resid_norm_bwd(x1, g_a, mix, norm_ffn, sh_f, sc_f, dh2, dy, w_out)

    g_out = _mm(merged, dmix, form="tn", out_dtype=BF, name="g_w_out")
    dz_dn, dz_at, dmg, dy_dn, dy_at = _merge_bwd(z_dn, z_at, p, dmerged, w_bdn, w_bat, L)
    g_bdn = _mm(y_dn, dz_dn, form="tn", out_dtype=BF, name="g_branch_dn")
    g_bat = _mm(y_at, dz_at, form="tn", out_dtype=BF, name="g_branch_at")
    dqr, dkp, dvp, dkx, dvx, dsink = _attn_bwd(qr, kr, vb, sinkb, dy_at, L)
    dq_raw, dk_raw, dv_raw, g_qn, g_kn = _attn_prep_bwd(p, q_norm, k_norm, cos, sin, dqr, dkp, dvp, dkx, dvx, L)
    do, dgt, g_dnn = _ghn_bwd(o_dir[0], o_dir[1], p, dn_norm, dy_dn, L)
    cots = _dn2_bwd([wy[d][:6] + (scans[d][1],) for d in (0, 1)], do, L)
    dq, dk, dv, dgb = _dn1_bwd(q, k, v, gb, [t[6] for t in wy], cots)
    dp, g_dnc8, g_alog, g_dt = _dn_pre_bwd(p, dnc8, alog_row, dt_row, dq, dk, dv, dgb, (dgt, dq_raw, dmg, dk_raw, dv_raw), (L, LC))
    big, dh1 = project_back(h1, dp, w_in_p, (g_bdn, g_bat, g_out, g_up, g_down))
    grad_x, g_nmix_x, dsh_a, dsc_a = _norm_mod_bwd(x, norm_mix, mod_x[0][None], mod_x[1][None], dh1, row0=0,
                                                   name="norm_mix_bwd", residual=dx1)
    g_nmix_c, dsh_c, dsc_c = _norm_mod_bwd(ctx, norm_mix, mod_c[0][None], mod_c[1][None], dh1, row0=L, name="norm_mix_bwd_ctx")
    g_nmix = g_nmix_x + g_nmix_c

    zero = jnp.zeros((D,), F32)
    dmod_x = jnp.stack([dsh_a[0], dsc_a[0], dg_a[0], dsh_f[0], dsc_f[0], dg_f[0]])
    dmod_c = jnp.stack([dsh_c[0], dsc_c[0], zero, zero, zero, zero])
    small = dict(
        dmod_x=dmod_x, dmod_c=dmod_c, norm_mix=g_nmix, norm_ffn=g_nffn, dn_conv=g_dnc8[:5], dn_a_log=g_alog[0, 16:32].reshape(2, 8),
        dn_dt_bias=g_dt[0, 16:32].reshape(2, 8), dn_norm=g_dnn, q_norm=g_qn, k_norm=g_kn,
        attn_sink=jnp.sum(dsink[:, :GRP, :], axis=2).reshape(1, NH), ffn_conv=g_ffc8[:3], ffn_conv_b=g_ffb)
    return loss_row[0, 0], grad_x, big, small


def _exchange(arrays, scatter, name):
    n = len(arrays)

    def body(*refs):
        args = (refs[:n], refs[n:2 * n], *refs[2 * n:], scatter)
        _exchange_start(*args)
        _exchange_wait(*args)

    hbm = pl.BlockSpec(memory_space=pl.ANY)
    out_shape, sems = _exchange_shapes(arrays, scatter)
    return pl.pallas_call(body, name=name, in_specs=[hbm] * n, out_specs=[hbm] * n, out_shape=out_shape,
                          scratch_shapes=sems)(*arrays)


def _gather_two_level(arrays, name):
    n = len(arrays)

    def body(*refs):
        ins, outs = refs[:n], refs[n:2 * n]
        send_sems, recv_sems, local_sems = refs[2 * n:]
        x, y, c = lax.axis_index("x"), lax.axis_index("y"), lax.axis_index("c")
        sibling = (x, y, 1 - c)
        chips = [(1 - x, y), (x, 1 - y), (1 - x, 1 - y)]

        def copy(k, j, block, to, src=None):
            slot = outs[k].at[4 * block[0] + 2 * block[1] + block[2]]
            return pltpu.make_async_remote_copy(src_ref=slot if src is None else src, dst_ref=slot,
                                                send_sem=send_sems.at[7 * k + j], recv_sem=recv_sems.at[7 * k + j],
                                                device_id=to, device_id_type=MESH)

        mine = [pltpu.make_async_copy(ins[k], outs[k].at[4 * x + 2 * y + c], local_sems.at[k]) for k in range(n)]
        for cp in mine:
            cp.start()
        first = []
        for k in range(n):
            first.append(copy(k, 0, (x, y, c), sibling, src=ins[k]))
            first += [copy(k, 1 + j, (x, y, c), (*chip, c), src=ins[k]) for j, chip in enumerate(chips)]
        for cp in first:
            cp.start()
        passed = []
        for k in range(n):
            for j, chip in enumerate(chips):
                copy(k, 1 + j, (*chip, c), (x, y, c)).wait_recv()
                forward = copy(k, 4 + j, (*chip, c), sibling)
                forward.start()
                passed.append(forward)
        for k in range(n):
            copy(k, 0, sibling, (x, y, c)).wait_recv()
            for j, chip in enumerate(chips):
                copy(k, 4 + j, (*chip, 1 - c), (x, y, c)).wait_recv()
        for cp in first + passed:
            cp.wait_send()
        for cp in mine:
            cp.wait()

    hbm = pl.BlockSpec(memory_space=pl.ANY)
    out_shape, sems = _exchange_shapes(arrays, False)
    return pl.pallas_call(body, name=name, in_specs=[hbm] * n, out_specs=[hbm] * n, out_shape=out_shape,
                          scratch_shapes=sems)(*arrays)


def _ada_fwd(c16, w_ada, b_ada):
    def body(c_ref, w_ref, b_ref, o_ref):
        o_ref[...] = _dot_hi(jax.nn.silu(c_ref[...]), w_ref[...]) + b_ref[...]

    return pl.pallas_call(body, name="ada_fwd", out_shape=jax.ShapeDtypeStruct((16, w_ada.shape[1]), F32))(c16, w_ada, b_ada)


def _ada_bwd(c16, w_ada, dmx, dmc):
    def body(c_ref, w_ref, dmx_ref, dmc_ref, gw_ref, pc_ref):
        dmc_tot = dmc_ref[0:1, :]
        for d in range(1, N_DEV):
            dmc_tot = dmc_tot + dmc_ref[d:d + 1, :]
        dm16 = jnp.concatenate([dmx_ref[...], jnp.broadcast_to(dmc_tot, (8, dmc_tot.shape[1]))], axis=0)
        row = lax.broadcasted_iota(jnp.int32, dm16.shape, 0)
        dm16 = jnp.where(row <= 8, dm16, 0.0)
        s = jax.nn.silu(c_ref[...])
        gw_ref[...] = lax.dot_general(s, dm16, (_DIMS["tn"], ((), ())), precision=HI, preferred_element_type=F32)
        pc = lax.dot_general(dm16, w_ref[...], (_DIMS["nt"], ((), ())), precision=HI, preferred_element_type=F32)
        pc_ref[...] = pc[8:9, :]

    return pl.pallas_call(body, name="ada_bwd", out_shape=[jax.ShapeDtypeStruct(w_ada.shape, F32), jax.ShapeDtypeStruct((1, D), F32)],
                          compiler_params=_cp())(c16, w_ada, dmx, dmc)


def _cctx_grad(pc_all, c_ctx_row):
    def body(pc_ref, c_ref, g_ref):
        tot = pc_ref[0]
        for d in range(1, N_DEV):
            tot = tot + pc_ref[d]
        _, vjp = jax.vjp(jax.nn.silu, c_ref[...])
        g_ref[...] = vjp(tot)[0]

    return pl.pallas_call(body, name="cctx_grad", out_shape=jax.ShapeDtypeStruct((1, D), F32))(pc_all, c_ctx_row)


def _adamw(parts, w, m, v, name):
    ns, R, C = parts.shape
    tb = _tile(R, (128, 64, 32, 16, 8))

    def body(p_ref, w_ref, m_ref, v_ref, g_ref, d_ref, mo_ref, vo_ref):
        g = p_ref[0].astype(F32)
        for s in range(1, ns):
            g = g + p_ref[s].astype(F32)
        m2 = ADAM_B1 * m_ref[...] + (1.0 - ADAM_B1) * g
        v2 = ADAM_B2 * v_ref[...] + (1.0 - ADAM_B2) * jnp.square(g)
        m_hat = m2 / (1.0 - ADAM_B1 ** ADAM_STEP)
        v_hat = v2 / (1.0 - ADAM_B2 ** ADAM_STEP)
        g_ref[...] = g
        d_ref[...] = -ADAM_LR * (m_hat / (jnp.sqrt(v_hat) + ADAM_EPS) + ADAM_WD * w_ref[...])
        mo_ref[...] = m2
        vo_ref[...] = v2

    row = pl.BlockSpec((tb, C), lambda i: (i, 0))
    return pl.pallas_call(
        body, grid=(R // tb,), name=name,
        in_specs=[pl.BlockSpec((ns, tb, C), lambda i: (0, i, 0)), row, row, row], out_specs=[row] * 4,
        out_shape=[jax.ShapeDtypeStruct((R, C), F32)] * 4, compiler_params=_cp(),
    )(parts, w, m, v)


_SMALL = (("dmod_x", 6 * D), ("dmod_c", 6 * D), ("b_ada", 6 * D), ("norm_mix", D), ("norm_ffn", D), ("dn_a_log", 16),
          ("dn_dt_bias", 16), ("dn_norm", HD), ("q_norm", HD), ("k_norm", HD), ("attn_sink", NH), ("ffn_conv_b", 2 * DFF),
          ("dn_conv", 5 * 3 * D), ("ffn_conv", 3 * 2 * DFF))
_SMALL_ROWS = -(-sum(n for _, n in _SMALL) // 1024) * 8


def _pack_small(d):
    flat = jnp.concatenate([d[k].reshape(-1).astype(F32) if k in d else jnp.zeros((n,), F32) for k, n in _SMALL])
    return jnp.concatenate([flat, jnp.zeros((_SMALL_ROWS * 128 - flat.shape[0],), F32)]).reshape(_SMALL_ROWS, 128)


def _unpack_small(a):
    flat = a.reshape(a.shape[:-2] + (-1,))
    out, off = {}, 0
    for k, n in _SMALL:
        out[k] = flat[..., off:off + n]
        off += n
    return out


def kernel(x, c, ctx, c_ctx, w_ada, b_ada, norm_mix, norm_ffn, w_in, dn_conv, dn_a_log, dn_dt_bias, dn_norm, q_norm, k_norm, attn_sink, w_branch_dn, w_branch_attn, w_out, ffn_up, ffn_conv, ffn_conv_b, ffn_down, loss_target, m_c_ctx, m_w_ada, m_b_ada, m_norm_mix, m_norm_ffn, m_w_in, m_dn_conv, m_dn_a_log, m_dn_dt_bias, m_dn_norm, m_q_norm, m_k_norm, m_attn_sink, m_w_branch_dn, m_w_branch_attn, m_w_out, m_ffn_up, m_ffn_conv, m_ffn_conv_b, m_ffn_down, v_c_ctx, v_w_ada, v_b_ada, v_norm_mix, v_norm_ffn, v_w_in, v_dn_conv, v_dn_a_log, v_dn_dt_bias, v_dn_norm, v_q_norm, v_k_norm, v_attn_sink, v_w_branch_dn, v_w_branch_attn, v_w_out, v_ffn_up, v_ffn_conv, v_ffn_conv_b, v_ffn_down):
    me = 4 * lax.axis_index("x") + 2 * lax.axis_index("y") + lax.axis_index("c")
    ada_cols = w_ada.shape[2]

    cols = lambda a: jnp.swapaxes(a, 0, 1).reshape(a.shape[1], -1)
    rows = lambda a: a.reshape(-1, a.shape[2])
    col_blocks = lambda g: jnp.swapaxes(g.reshape(g.shape[0], N_DEV, -1), 0, 1)
    row_blocks = lambda g: g.reshape(N_DEV, -1, g.shape[1])

    gathered = _gather_two_level([w_in[0].astype(BF), c, dn_conv[0], ffn_conv[0]], name="gather_first")
    w_in_packed = _pack_w_in(cols(gathered[0]))
    c_all = gathered[1][:, 0, :]

    def project_in(h1):
        p, rest = _mm(h1, w_in_packed, form="nn", out_dtype=F32, name="in_proj",
                      exchange=([w_branch_dn[0].astype(BF), w_branch_attn[0].astype(BF), w_out[0].astype(BF),
                                 ffn_up[0].astype(BF), ffn_down[0].astype(BF)], False))
        return p, (w_in_packed, rows(rest[0]), rows(rest[1]), rows(rest[2]), cols(rest[3]), rows(rest[4]))

    def project_back(h1, dp, w_in_p, grads):
        g_bdn, g_bat, g_out, g_up, g_down = grads
        g_in, landed_rest = _mm(h1, dp, form="tn", out_dtype=BF, name="g_w_in",
                                exchange=([row_blocks(g_bdn), row_blocks(g_bat), row_blocks(g_out), col_blocks(g_up),
                                           row_blocks(g_down)], True))
        dh1, landed_in = _mm(dp, w_in_p, form="nt", out_dtype=F32, name="d_h1",
                             exchange=([col_blocks(_unpack_w_in(g_in))], True))
        return [landed_in[0]] + landed_rest, dh1

    c16 = jnp.concatenate([c_all, c_ctx[None], jnp.zeros((7, D), F32)], axis=0)
    b_loc = lax.dynamic_slice_in_dim(b_ada, me * ada_cols, ada_cols, axis=1)
    mod_part = _ada_fwd(c16, w_ada[0], b_loc)
    mod_all = cols(_exchange([mod_part], scatter=False, name="gather_mod")[0])
    mod_x = lax.dynamic_slice_in_dim(mod_all, me, 1, axis=0).reshape(6, D)
    mod_c = mod_all[8].reshape(6, D)

    loss_loc, grad_x, landed, small = _local_step(
        x[0], ctx[0], mod_x, mod_c, loss_target[0], project_in, project_back,
        norm_mix, norm_ffn, cols(gathered[2]), dn_a_log[0], dn_dt_bias[0], dn_norm, q_norm, k_norm, attn_sink[0], cols(gathered[3]),
        ffn_conv_b)
    loss = lax.psum(loss_loc, ("x", "y", "c"))

    res = {}
    res["w_in"] = _adamw(landed[0], w_in[0], m_w_in[0], v_w_in[0], "adamw_w_in")
    res["w_branch_dn"] = _adamw(landed[1], w_branch_dn[0], m_w_branch_dn[0], v_w_branch_dn[0], "adamw_w_branch_dn")
    res["w_branch_attn"] = _adamw(landed[2], w_branch_attn[0], m_w_branch_attn[0], v_w_branch_attn[0], "adamw_w_branch_attn")
    res["w_out"] = _adamw(landed[3], w_out[0], m_w_out[0], v_w_out[0], "adamw_w_out")
    res["ffn_up"] = _adamw(landed[4], ffn_up[0], m_ffn_up[0], v_ffn_up[0], "adamw_ffn_up")
    res["ffn_down"] = _adamw(landed[5], ffn_down[0], m_ffn_down[0], v_ffn_down[0], "adamw_ffn_down")

    small = dict(small)
    small["b_ada"] = small["dmod_x"] + small["dmod_c"]
    parts = _exchange([_pack_small(small)], scatter=False, name="gather_small")[0]
    per_dev = _unpack_small(parts)
    given = dict(b_ada=(b_ada, m_b_ada, v_b_ada), norm_mix=(norm_mix, m_norm_mix, v_norm_mix), norm_ffn=(norm_ffn, m_norm_ffn, v_norm_ffn),
                 dn_a_log=(dn_a_log, m_dn_a_log, v_dn_a_log), dn_dt_bias=(dn_dt_bias, m_dn_dt_bias, v_dn_dt_bias),
                 dn_norm=(dn_norm, m_dn_norm, v_dn_norm), q_norm=(q_norm, m_q_norm, v_q_norm), k_norm=(k_norm, m_k_norm, v_k_norm),
                 attn_sink=(attn_sink, m_attn_sink, v_attn_sink), ffn_conv_b=(ffn_conv_b, m_ffn_conv_b, v_ffn_conv_b))
    packs = [_pack_small({k: t[j] for k, t in given.items()}) for j in range(3)]
    upd = [_unpack_small(a) for a in _adamw(parts, packs[0], packs[1], packs[2], "adamw_small")]
    for k, t in given.items():
        res[k] = tuple(u[k].reshape(t[0].shape) for u in upd)
    dnc = lax.dynamic_slice_in_dim(upd[0]["dn_conv"].reshape(5, 3 * D), me * dn_conv.shape[2], dn_conv.shape[2], axis=1)
    ffc = lax.dynamic_slice_in_dim(upd[0]["ffn_conv"].reshape(3, 2 * DFF), me * ffn_conv.shape[2], ffn_conv.shape[2], axis=1)
    r8 = lambda a: _pad_rows8(a)
    t = _adamw(r8(dnc)[None], r8(dn_conv[0]), r8(m_dn_conv[0]), r8(v_dn_conv[0]), "adamw_dn_conv")
    res["dn_conv"] = tuple(a[:5][None] for a in t)
    t = _adamw(r8(ffc)[None], r8(ffn_conv[0]), r8(m_ffn_conv[0]), r8(v_ffn_conv[0]), "adamw_ffn_conv")
    res["ffn_conv"] = tuple(a[:3][None] for a in t)

    dmx = lax.dynamic_slice_in_dim(per_dev["dmod_x"], me * ada_cols, ada_cols, axis=1)
    dmc = lax.dynamic_slice_in_dim(per_dev["dmod_c"], me * ada_cols, ada_cols, axis=1)
    g_ada, pc = _ada_bwd(c16, w_ada[0], dmx, dmc)
    res["w_ada"] = _adamw(g_ada[None], w_ada[0], m_w_ada[0], v_w_ada[0], "adamw_w_ada")
    pc_all = _exchange([pc], scatter=False, name="gather_cctx")[0]
    g_cctx = _cctx_grad(pc_all, c_ctx[None])
    r8b = lambda a: jnp.broadcast_to(a, (8, D))
    t = _adamw(r8b(g_cctx)[None], r8b(c_ctx[None]), r8b(m_c_ctx[None]), r8b(v_c_ctx[None]), "adamw_c_ctx")
    res["c_ctx"] = tuple(a[0] for a in t)

    names = ("c_ctx", "w_ada", "b_ada", "norm_mix", "norm_ffn", "w_in", "dn_conv", "dn_a_log", "dn_dt_bias", "dn_norm", "q_norm",
             "k_norm", "attn_sink", "w_branch_dn", "w_branch_attn", "w_out", "ffn_up", "ffn_conv", "ffn_conv_b", "ffn_down")
    lead = ("w_ada", "w_in", "w_branch_dn", "w_branch_attn", "w_out", "ffn_up", "ffn_down")
    fix = lambda k, a: a[None] if k in lead else a
    outs = [loss, grad_x[None]]
    for j in range(4):
        outs += [fix(k, res[k][j]) for k in names]
    return tuple(outs)
```
